```python
import math
import jax, jax.numpy as jnp
from jax import lax
import numpy as np

D_MODEL = 1024
BATCH = 32
SEQ = 2048
DEPTH = 2

D_MIX = D_MODEL
SSM_WIDTH = D_MIX // 2
SSM_CH = 16
SSM_GROUPS = SSM_WIDTH // SSM_CH
SSM_STATE = 64
GM_WIDTH = D_MIX - SSM_WIDTH
GM_HEADS = 4
GM_HEAD_DIM = GM_WIDTH // GM_HEADS
GM_CHUNK = 128
D_FF = ((8 * D_MODEL // 3 + 127) // 128) * 128
IN_COLS = SSM_WIDTH + 2 * GM_WIDTH
EPS = 1e-6
DT_MIN = 1e-3
DT_MAX = 1e-1

kernel_name = "hybrid_s5_gmlp_macaron"


def rmsnorm(x, g):
    xf = x.astype(jnp.float32)
    y = xf * lax.rsqrt(jnp.mean(xf * xf, axis=-1, keepdims=True) + EPS)
    return (y * g.astype(jnp.float32)).astype(x.dtype)


def swiglu(h, w_in, w_out):
    gu = h @ w_in
    g, u = gu[..., :D_FF], gu[..., D_FF:]
    return (jax.nn.silu(g) * u) @ w_out


def s5_group(u, a_re, a_im, log_dt, b_re, b_im, c_re, c_im, d_skip, glu_w, glu_b):
    bsz, seq = u.shape[0], u.shape[1]
    f32 = jnp.float32
    ug = u.reshape(bsz, seq, SSM_GROUPS, SSM_CH).astype(f32)
    lam = lax.complex(a_re.astype(f32), a_im.astype(f32))
    dt = jnp.exp(log_dt.astype(f32))[:, None]
    lam_bar = jnp.exp(lam * dt)
    b_mat = lax.complex(b_re.astype(f32), b_im.astype(f32))
    b_bar = ((lam_bar - 1.0) / lam)[..., None] * b_mat
    bu = jnp.einsum('blgc,gpc->blgp', ug, b_bar)
    a_elems = jnp.broadcast_to(lam_bar, (seq, SSM_GROUPS, SSM_STATE))

    def combine(left, right):
        a_l, b_l = left
        a_r, b_r = right
        return a_r * a_l, a_r * b_l + b_r

    def scan_one(bu_b):
        _, h = lax.associative_scan(combine, (a_elems, bu_b), axis=0)
        return h

    h = jax.vmap(scan_one)(bu)
    c_mat = lax.complex(c_re.astype(f32), c_im.astype(f32))
    y = jnp.einsum('blgp,gcp->blgc', h, c_mat).real + d_skip.astype(f32) * ug
    y = jax.nn.gelu(y)
    z = jnp.einsum('blgc,gce->blge', y, glu_w.astype(f32)) + glu_b.astype(f32)
    out = z[..., :SSM_CH] * jax.nn.sigmoid(z[..., SSM_CH:])
    return out.reshape(bsz, seq, SSM_WIDTH).astype(u.dtype)


def gmlp_group(u, v, v_gain, w_s, b_s):
    bsz, seq = u.shape[0], u.shape[1]
    u = jax.nn.gelu(u)
    v = jax.nn.gelu(v).reshape(bsz, seq // GM_CHUNK, GM_CHUNK, GM_HEADS, GM_HEAD_DIM)
    v = rmsnorm(v, v_gain.reshape(GM_HEADS, GM_HEAD_DIM))
    mask = jnp.tril(jnp.ones((GM_CHUNK, GM_CHUNK), dtype=bool))
    ws = jnp.where(mask[None], w_s, jnp.zeros_like(w_s))
    s = jnp.einsum('hts,bnshd->bnthd', ws, v) + b_s.T[None, None, :, :, None]
    return u * s.reshape(bsz, seq, GM_WIDTH)


def _fwd_setup_inputs(seed: int = 0) -> dict:
    key = jax.random.key(seed)
    ks = jax.random.split(key, 26)
    f32 = jnp.float32
    nrm = lambda k, shape, scale: scale * jax.random.normal(k, shape, f32)
    gain = lambda k, shape: 1.0 + 0.05 * jax.random.normal(k, shape, f32)
    n_idx = jnp.arange(SSM_STATE, dtype=f32)
    return {
        "x": jax.random.normal(ks[0], (BATCH, SEQ, D_MODEL), f32),
        "norm_ffn1": gain(ks[1], (DEPTH, D_MODEL)),
        "ffn1_w_in": nrm(ks[2], (DEPTH, D_MODEL, 2 * D_FF), D_MODEL ** -0.5),
        "ffn1_w_out": nrm(ks[3], (DEPTH, D_FF, D_MODEL), D_FF ** -0.5),
        "norm_mix": gain(ks[4], (DEPTH, D_MODEL)),
        "mix_w_in": nrm(ks[5], (DEPTH, D_MODEL, IN_COLS), D_MODEL ** -0.5),
        "ssm_a_re": -0.5 + nrm(ks[6], (DEPTH, SSM_GROUPS, SSM_STATE), 0.01),
        "ssm_a_im": math.pi * n_idx + nrm(ks[7], (DEPTH, SSM_GROUPS, SSM_STATE), 0.01),
        "ssm_log_dt": jax.random.uniform(ks[8], (DEPTH, SSM_GROUPS), f32,
                                         math.log(DT_MIN), math.log(DT_MAX)),
        "ssm_b_re": nrm(ks[9], (DEPTH, SSM_GROUPS, SSM_STATE, SSM_CH), (2 * SSM_CH) ** -0.5),
        "ssm_b_im": nrm(ks[10], (DEPTH, SSM_GROUPS, SSM_STATE, SSM_CH), (2 * SSM_CH) ** -0.5),
        "ssm_c_re": nrm(ks[11], (DEPTH, SSM_GROUPS, SSM_CH, SSM_STATE), (2 * SSM_STATE) ** -0.5),
        "ssm_c_im": nrm(ks[12], (DEPTH, SSM_GROUPS, SSM_CH, SSM_STATE), (2 * SSM_STATE) ** -0.5),
        "ssm_d": nrm(ks[13], (DEPTH, SSM_GROUPS, SSM_CH), 1.0),
        "ssm_glu_w": nrm(ks[14], (DEPTH, SSM_GROUPS, SSM_CH, 2 * SSM_CH), SSM_CH ** -0.5),
        "ssm_glu_b": nrm(ks[15], (DEPTH, SSM_GROUPS, 2 * SSM_CH), 0.02),
        "gm_v_gain": gain(ks[16], (DEPTH, GM_WIDTH)),
        "gm_w_s": nrm(ks[17], (DEPTH, GM_HEADS, GM_CHUNK, GM_CHUNK), 0.5 * GM_CHUNK ** -0.5),
        "gm_b_s": 1.0 + nrm(ks[18], (DEPTH, GM_HEADS, GM_CHUNK), 0.1),
        "gain_ssm_out": gain(ks[19], (DEPTH, SSM_WIDTH)),
        "gain_gm_out": gain(ks[20], (DEPTH, GM_WIDTH)),
        "mix_w_out": nrm(ks[21], (DEPTH, D_MIX, D_MODEL), D_MIX ** -0.5),
        "norm_ffn2": gain(ks[22], (DEPTH, D_MODEL)),
        "ffn2_w_in": nrm(ks[23], (DEPTH, D_MODEL, 2 * D_FF), D_MODEL ** -0.5),
        "ffn2_w_out": nrm(ks[24], (DEPTH, D_FF, D_MODEL), D_FF ** -0.5),
        "norm_final": gain(ks[25], (D_MODEL,)),
    }


def _fwd_reference(x, norm_ffn1, ffn1_w_in, ffn1_w_out, norm_mix, mix_w_in,
              ssm_a_re, ssm_a_im, ssm_log_dt, ssm_b_re, ssm_b_im, ssm_c_re, ssm_c_im,
              ssm_d, ssm_glu_w, ssm_glu_b, gm_v_gain, gm_w_s, gm_b_s,
              gain_ssm_out, gain_gm_out, mix_w_out, norm_ffn2, ffn2_w_in, ffn2_w_out,
              norm_final):
    for l in range(DEPTH):
        x = x + 0.5 * swiglu(rmsnorm(x, norm_ffn1[l]), ffn1_w_in[l], ffn1_w_out[l])
        z = rmsnorm(x, norm_mix[l]) @ mix_w_in[l]
        u_ssm = z[..., :SSM_WIDTH]
        u_gm = z[..., SSM_WIDTH:SSM_WIDTH + GM_WIDTH]
        v_gm = z[..., SSM_WIDTH + GM_WIDTH:]
        y_ssm = s5_group(u_ssm, ssm_a_re[l], ssm_a_im[l], ssm_log_dt[l],
                         ssm_b_re[l], ssm_b_im[l], ssm_c_re[l], ssm_c_im[l],
                         ssm_d[l], ssm_glu_w[l], ssm_glu_b[l])
        y_gm = gmlp_group(u_gm, v_gm, gm_v_gain[l], gm_w_s[l], gm_b_s[l])
        y = jnp.concatenate([rmsnorm(y_ssm, gain_ssm_out[l]),
                             rmsnorm(y_gm, gain_gm_out[l])], axis=-1)
        x = x + y @ mix_w_out[l]
        x = x + 0.5 * swiglu(rmsnorm(x, norm_ffn2[l]), ffn2_w_in[l], ffn2_w_out[l])
    return rmsnorm(x, norm_final)


import jax as _jax
import jax.numpy as _jnp

TWIN_FORMAT = 'train_step'
FWD_PARAMS = ['x', 'norm_ffn1', 'ffn1_w_in', 'ffn1_w_out', 'norm_mix', 'mix_w_in', 'ssm_a_re', 'ssm_a_im', 'ssm_log_dt', 'ssm_b_re', 'ssm_b_im', 'ssm_c_re', 'ssm_c_im', 'ssm_d', 'ssm_glu_w', 'ssm_glu_b', 'gm_v_gain', 'gm_w_s', 'gm_b_s', 'gain_ssm_out', 'gain_gm_out', 'mix_w_out', 'norm_ffn2', 'ffn2_w_in', 'ffn2_w_out', 'norm_final']
TWIN_WEIGHTS = ['norm_ffn1', 'ffn1_w_in', 'ffn1_w_out', 'norm_mix', 'mix_w_in', 'ssm_a_re', 'ssm_a_im', 'ssm_log_dt', 'ssm_b_re', 'ssm_b_im', 'ssm_c_re', 'ssm_c_im', 'ssm_d', 'ssm_glu_w', 'ssm_glu_b', 'gm_v_gain', 'gm_w_s', 'gm_b_s', 'gain_ssm_out', 'gain_gm_out', 'mix_w_out', 'norm_ffn2', 'ffn2_w_in', 'ffn2_w_out', 'norm_final']
TWIN_DIFF_INPUT = 'x'
TWIN_INPUTS = ['x', 'norm_ffn1', 'ffn1_w_in', 'ffn1_w_out', 'norm_mix', 'mix_w_in', 'ssm_a_re', 'ssm_a_im', 'ssm_log_dt', 'ssm_b_re', 'ssm_b_im', 'ssm_c_re', 'ssm_c_im', 'ssm_d', 'ssm_glu_w', 'ssm_glu_b', 'gm_v_gain', 'gm_w_s', 'gm_b_s', 'gain_ssm_out', 'gain_gm_out', 'mix_w_out', 'norm_ffn2', 'ffn2_w_in', 'ffn2_w_out', 'norm_final', 'loss_target', 'm_norm_ffn1', 'm_ffn1_w_in', 'm_ffn1_w_out', 'm_norm_mix', 'm_mix_w_in', 'm_ssm_a_re', 'm_ssm_a_im', 'm_ssm_log_dt', 'm_ssm_b_re', 'm_ssm_b_im', 'm_ssm_c_re', 'm_ssm_c_im', 'm_ssm_d', 'm_ssm_glu_w', 'm_ssm_glu_b', 'm_gm_v_gain', 'm_gm_w_s', 'm_gm_b_s', 'm_gain_ssm_out', 'm_gain_gm_out', 'm_mix_w_out', 'm_norm_ffn2', 'm_ffn2_w_in', 'm_ffn2_w_out', 'm_norm_final', 'v_norm_ffn1', 'v_ffn1_w_in', 'v_ffn1_w_out', 'v_norm_mix', 'v_mix_w_in', 'v_ssm_a_re', 'v_ssm_a_im', 'v_ssm_log_dt', 'v_ssm_b_re', 'v_ssm_b_im', 'v_ssm_c_re', 'v_ssm_c_im', 'v_ssm_d', 'v_ssm_glu_w', 'v_ssm_glu_b', 'v_gm_v_gain', 'v_gm_w_s', 'v_gm_b_s', 'v_gain_ssm_out', 'v_gain_gm_out', 'v_mix_w_out', 'v_norm_ffn2', 'v_ffn2_w_in', 'v_ffn2_w_out', 'v_norm_final']
TWIN_OUTPUTS = ['loss', 'grad_x', 'grad_norm_ffn1', 'grad_ffn1_w_in', 'grad_ffn1_w_out', 'grad_norm_mix', 'grad_mix_w_in', 'grad_ssm_a_re', 'grad_ssm_a_im', 'grad_ssm_log_dt', 'grad_ssm_b_re', 'grad_ssm_b_im', 'grad_ssm_c_re', 'grad_ssm_c_im', 'grad_ssm_d', 'grad_ssm_glu_w', 'grad_ssm_glu_b', 'grad_gm_v_gain', 'grad_gm_w_s', 'grad_gm_b_s', 'grad_gain_ssm_out', 'grad_gain_gm_out', 'grad_mix_w_out', 'grad_norm_ffn2', 'grad_ffn2_w_in', 'grad_ffn2_w_out', 'grad_norm_final', 'delta_norm_ffn1', 'delta_ffn1_w_in', 'delta_ffn1_w_out', 'delta_norm_mix', 'delta_mix_w_in', 'delta_ssm_a_re', 'delta_ssm_a_im', 'delta_ssm_log_dt', 'delta_ssm_b_re', 'delta_ssm_b_im', 'delta_ssm_c_re', 'delta_ssm_c_im', 'delta_ssm_d', 'delta_ssm_glu_w', 'delta_ssm_glu_b', 'delta_gm_v_gain', 'delta_gm_w_s', 'delta_gm_b_s', 'delta_gain_ssm_out', 'delta_gain_gm_out', 'delta_mix_w_out', 'delta_norm_ffn2', 'delta_ffn2_w_in', 'delta_ffn2_w_out', 'delta_norm_final', 'new_m_norm_ffn1', 'new_m_ffn1_w_in', 'new_m_ffn1_w_out', 'new_m_norm_mix', 'new_m_mix_w_in', 'new_m_ssm_a_re', 'new_m_ssm_a_im', 'new_m_ssm_log_dt', 'new_m_ssm_b_re', 'new_m_ssm_b_im', 'new_m_ssm_c_re', 'new_m_ssm_c_im', 'new_m_ssm_d', 'new_m_ssm_glu_w', 'new_m_ssm_glu_b', 'new_m_gm_v_gain', 'new_m_gm_w_s', 'new_m_gm_b_s', 'new_m_gain_ssm_out', 'new_m_gain_gm_out', 'new_m_mix_w_out', 'new_m_norm_ffn2', 'new_m_ffn2_w_in', 'new_m_ffn2_w_out', 'new_m_norm_final', 'new_v_norm_ffn1', 'new_v_ffn1_w_in', 'new_v_ffn1_w_out', 'new_v_norm_mix', 'new_v_mix_w_in', 'new_v_ssm_a_re', 'new_v_ssm_a_im', 'new_v_ssm_log_dt', 'new_v_ssm_b_re', 'new_v_ssm_b_im', 'new_v_ssm_c_re', 'new_v_ssm_c_im', 'new_v_ssm_d', 'new_v_ssm_glu_w', 'new_v_ssm_glu_b', 'new_v_gm_v_gain', 'new_v_gm_w_s', 'new_v_gm_b_s', 'new_v_gain_ssm_out', 'new_v_gain_gm_out', 'new_v_mix_w_out', 'new_v_norm_ffn2', 'new_v_ffn2_w_in', 'new_v_ffn2_w_out', 'new_v_norm_final']
TWIN_LEAF_KINDS = {'loss': 'loss', 'grad_x': 'grad_x', 'grad_norm_ffn1': 'grad_w', 'grad_ffn1_w_in': 'grad_w', 'grad_ffn1_w_out': 'grad_w', 'grad_norm_mix': 'grad_w', 'grad_mix_w_in': 'grad_w', 'grad_ssm_a_re': 'grad_w', 'grad_ssm_a_im': 'grad_w', 'grad_ssm_log_dt': 'grad_w', 'grad_ssm_b_re': 'grad_w', 'grad_ssm_b_im': 'grad_w', 'grad_ssm_c_re': 'grad_w', 'grad_ssm_c_im': 'grad_w', 'grad_ssm_d': 'grad_w', 'grad_ssm_glu_w': 'grad_w', 'grad_ssm_glu_b': 'grad_w', 'grad_gm_v_gain': 'grad_w', 'grad_gm_w_s': 'grad_w', 'grad_gm_b_s': 'grad_w', 'grad_gain_ssm_out': 'grad_w', 'grad_gain_gm_out': 'grad_w', 'grad_mix_w_out': 'grad_w', 'grad_norm_ffn2': 'grad_w', 'grad_ffn2_w_in': 'grad_w', 'grad_ffn2_w_out': 'grad_w', 'grad_norm_final': 'grad_w', 'delta_norm_ffn1': 'delta_w', 'delta_ffn1_w_in': 'delta_w', 'delta_ffn1_w_out': 'delta_w', 'delta_norm_mix': 'delta_w', 'delta_mix_w_in': 'delta_w', 'delta_ssm_a_re': 'delta_w', 'delta_ssm_a_im': 'delta_w', 'delta_ssm_log_dt': 'delta_w', 'delta_ssm_b_re': 'delta_w', 'delta_ssm_b_im': 'delta_w', 'delta_ssm_c_re': 'delta_w', 'delta_ssm_c_im': 'delta_w', 'delta_ssm_d': 'delta_w', 'delta_ssm_glu_w': 'delta_w', 'delta_ssm_glu_b': 'delta_w', 'delta_gm_v_gain': 'delta_w', 'delta_gm_w_s': 'delta_w', 'delta_gm_b_s': 'delta_w', 'delta_gain_ssm_out': 'delta_w', 'delta_gain_gm_out': 'delta_w', 'delta_mix_w_out': 'delta_w', 'delta_norm_ffn2': 'delta_w', 'delta_ffn2_w_in': 'delta_w', 'delta_ffn2_w_out': 'delta_w', 'delta_norm_final': 'delta_w', 'new_m_norm_ffn1': 'new_m', 'new_m_ffn1_w_in': 'new_m', 'new_m_ffn1_w_out': 'new_m', 'new_m_norm_mix': 'new_m', 'new_m_mix_w_in': 'new_m', 'new_m_ssm_a_re': 'new_m', 'new_m_ssm_a_im': 'new_m', 'new_m_ssm_log_dt': 'new_m', 'new_m_ssm_b_re': 'new_m', 'new_m_ssm_b_im': 'new_m', 'new_m_ssm_c_re': 'new_m', 'new_m_ssm_c_im': 'new_m', 'new_m_ssm_d': 'new_m', 'new_m_ssm_glu_w': 'new_m', 'new_m_ssm_glu_b': 'new_m', 'new_m_gm_v_gain': 'new_m', 'new_m_gm_w_s': 'new_m', 'new_m_gm_b_s': 'new_m', 'new_m_gain_ssm_out': 'new_m', 'new_m_gain_gm_out': 'new_m', 'new_m_mix_w_out': 'new_m', 'new_m_norm_ffn2': 'new_m', 'new_m_ffn2_w_in': 'new_m', 'new_m_ffn2_w_out': 'new_m', 'new_m_norm_final': 'new_m', 'new_v_norm_ffn1': 'new_v', 'new_v_ffn1_w_in': 'new_v', 'new_v_ffn1_w_out': 'new_v', 'new_v_norm_mix': 'new_v', 'new_v_mix_w_in': 'new_v', 'new_v_ssm_a_re': 'new_v', 'new_v_ssm_a_im': 'new_v', 'new_v_ssm_log_dt': 'new_v', 'new_v_ssm_b_re': 'new_v', 'new_v_ssm_b_im': 'new_v', 'new_v_ssm_c_re': 'new_v', 'new_v_ssm_c_im': 'new_v', 'new_v_ssm_d': 'new_v', 'new_v_ssm_glu_w': 'new_v', 'new_v_ssm_glu_b': 'new_v', 'new_v_gm_v_gain': 'new_v', 'new_v_gm_w_s': 'new_v', 'new_v_gm_b_s': 'new_v', 'new_v_gain_ssm_out': 'new_v', 'new_v_gain_gm_out': 'new_v', 'new_v_mix_w_out': 'new_v', 'new_v_norm_ffn2': 'new_v', 'new_v_ffn2_w_in': 'new_v', 'new_v_ffn2_w_out': 'new_v', 'new_v_norm_final': 'new_v'}


def _forward(args):
    return _fwd_reference(*[args[k] for k in FWD_PARAMS])


def _output_shape():
    out = _jax.eval_shape(lambda: _forward(_fwd_setup_inputs(0)))
    return out.shape, out.dtype

N_MICROBATCH = 1
ADAM_LR = 0.001
ADAM_B1 = 0.9
ADAM_B2 = 0.999
ADAM_EPS = 1e-08
ADAM_WD = 0.01
ADAM_STEP = 10
PER_EXAMPLE_BATCH_AXIS = {'x': 0, 'loss_target': 0}
SHARED_INPUTS = []
_WEIGHT_DTYPES = {'norm_ffn1': _jnp.float32, 'ffn1_w_in': _jnp.float32, 'ffn1_w_out': _jnp.float32, 'norm_mix': _jnp.float32, 'mix_w_in': _jnp.float32, 'ssm_a_re': _jnp.float32, 'ssm_a_im': _jnp.float32, 'ssm_log_dt': _jnp.float32, 'ssm_b_re': _jnp.float32, 'ssm_b_im': _jnp.float32, 'ssm_c_re': _jnp.float32, 'ssm_c_im': _jnp.float32, 'ssm_d': _jnp.float32, 'ssm_glu_w': _jnp.float32, 'ssm_glu_b': _jnp.float32, 'gm_v_gain': _jnp.float32, 'gm_w_s': _jnp.float32, 'gm_b_s': _jnp.float32, 'gain_ssm_out': _jnp.float32, 'gain_gm_out': _jnp.float32, 'mix_w_out': _jnp.float32, 'norm_ffn2': _jnp.float32, 'ffn2_w_in': _jnp.float32, 'ffn2_w_out': _jnp.float32, 'norm_final': _jnp.float32}
MOMENT_SCALE = {'norm_ffn1': 1.010447e-01, 'ffn1_w_in': 4.320731e-02, 'ffn1_w_out': 7.119898e-02, 'norm_mix': 1.888772e-01, 'mix_w_in': 1.664071e-01, 'ssm_a_re': 1.947259e-02, 'ssm_a_im': 1.840850e-02, 'ssm_log_dt': 3.344908e+00, 'ssm_b_re': 8.063380e-03, 'ssm_b_im': 9.062160e-03, 'ssm_c_re': 2.092270e-02, 'ssm_c_im': 2.215282e-02, 'ssm_d': 3.880471e-01, 'ssm_glu_w': 2.885443e-01, 'ssm_glu_b': 9.446385e-01, 'gm_v_gain': 5.503030e-02, 'gm_w_s': 1.083434e-01, 'gm_b_s': 1.428523e-01, 'gain_ssm_out': 3.625098e-01, 'gain_gm_out': 3.272595e-01, 'mix_w_out': 3.577974e-01, 'norm_ffn2': 7.851360e-02, 'ffn2_w_in': 3.247042e-02, 'ffn2_w_out': 5.443175e-02, 'norm_final': 6.478877e+01}


def _to_microbatches(a, axis):
    t = _jnp.moveaxis(a, axis, 0)
    t = t.reshape((N_MICROBATCH, t.shape[0] // N_MICROBATCH) + t.shape[1:])
    return _jnp.moveaxis(t, 1, axis + 1)


def setup_inputs(seed: int = 0) -> dict:
    inp = _fwd_setup_inputs(seed)
    key = _jax.random.fold_in(_jax.random.key(seed), 7919)
    shape, _ = _output_shape()
    out = dict(inp)
    out["loss_target"] = _jax.random.normal(_jax.random.fold_in(key, 0), shape, _jnp.float32)
    for i, name in enumerate(TWIN_WEIGHTS):
        w = inp[name].astype(_jnp.float32)
        if MOMENT_SCALE is None:
            s = _jnp.sqrt(_jnp.mean(_jnp.square(w)) + 1e-30)
        else:
            s = MOMENT_SCALE[name]
        km, kv = _jax.random.split(_jax.random.fold_in(key, i + 1))
        out[name] = w
        out["m_" + name] = s * _jax.random.normal(km, w.shape, _jnp.float32)
        out["v_" + name] = (s * s) * _jax.random.uniform(kv, w.shape, _jnp.float32, 0.5, 1.5)
    if N_MICROBATCH > 1:
        for name, axis in PER_EXAMPLE_BATCH_AXIS.items():
            out[name] = _to_microbatches(out[name], axis)
    return {'x': out['x'], 'norm_ffn1': out['norm_ffn1'], 'ffn1_w_in': out['ffn1_w_in'], 'ffn1_w_out': out['ffn1_w_out'], 'norm_mix': out['norm_mix'], 'mix_w_in': out['mix_w_in'], 'ssm_a_re': out['ssm_a_re'], 'ssm_a_im': out['ssm_a_im'], 'ssm_log_dt': out['ssm_log_dt'], 'ssm_b_re': out['ssm_b_re'], 'ssm_b_im': out['ssm_b_im'], 'ssm_c_re': out['ssm_c_re'], 'ssm_c_im': out['ssm_c_im'], 'ssm_d': out['ssm_d'], 'ssm_glu_w': out['ssm_glu_w'], 'ssm_glu_b': out['ssm_glu_b'], 'gm_v_gain': out['gm_v_gain'], 'gm_w_s': out['gm_w_s'], 'gm_b_s': out['gm_b_s'], 'gain_ssm_out': out['gain_ssm_out'], 'gain_gm_out': out['gain_gm_out'], 'mix_w_out': out['mix_w_out'], 'norm_ffn2': out['norm_ffn2'], 'ffn2_w_in': out['ffn2_w_in'], 'ffn2_w_out': out['ffn2_w_out'], 'norm_final': out['norm_final'], 'loss_target': out['loss_target'], 'm_norm_ffn1': out['m_norm_ffn1'], 'm_ffn1_w_in': out['m_ffn1_w_in'], 'm_ffn1_w_out': out['m_ffn1_w_out'], 'm_norm_mix': out['m_norm_mix'], 'm_mix_w_in': out['m_mix_w_in'], 'm_ssm_a_re': out['m_ssm_a_re'], 'm_ssm_a_im': out['m_ssm_a_im'], 'm_ssm_log_dt': out['m_ssm_log_dt'], 'm_ssm_b_re': out['m_ssm_b_re'], 'm_ssm_b_im': out['m_ssm_b_im'], 'm_ssm_c_re': out['m_ssm_c_re'], 'm_ssm_c_im': out['m_ssm_c_im'], 'm_ssm_d': out['m_ssm_d'], 'm_ssm_glu_w': out['m_ssm_glu_w'], 'm_ssm_glu_b': out['m_ssm_glu_b'], 'm_gm_v_gain': out['m_gm_v_gain'], 'm_gm_w_s': out['m_gm_w_s'], 'm_gm_b_s': out['m_gm_b_s'], 'm_gain_ssm_out': out['m_gain_ssm_out'], 'm_gain_gm_out': out['m_gain_gm_out'], 'm_mix_w_out': out['m_mix_w_out'], 'm_norm_ffn2': out['m_norm_ffn2'], 'm_ffn2_w_in': out['m_ffn2_w_in'], 'm_ffn2_w_out': out['m_ffn2_w_out'], 'm_norm_final': out['m_norm_final'], 'v_norm_ffn1': out['v_norm_ffn1'], 'v_ffn1_w_in': out['v_ffn1_w_in'], 'v_ffn1_w_out': out['v_ffn1_w_out'], 'v_norm_mix': out['v_norm_mix'], 'v_mix_w_in': out['v_mix_w_in'], 'v_ssm_a_re': out['v_ssm_a_re'], 'v_ssm_a_im': out['v_ssm_a_im'], 'v_ssm_log_dt': out['v_ssm_log_dt'], 'v_ssm_b_re': out['v_ssm_b_re'], 'v_ssm_b_im': out['v_ssm_b_im'], 'v_ssm_c_re': out['v_ssm_c_re'], 'v_ssm_c_im': out['v_ssm_c_im'], 'v_ssm_d': out['v_ssm_d'], 'v_ssm_glu_w': out['v_ssm_glu_w'], 'v_ssm_glu_b': out['v_ssm_glu_b'], 'v_gm_v_gain': out['v_gm_v_gain'], 'v_gm_w_s': out['v_gm_w_s'], 'v_gm_b_s': out['v_gm_b_s'], 'v_gain_ssm_out': out['v_gain_ssm_out'], 'v_gain_gm_out': out['v_gain_gm_out'], 'v_mix_w_out': out['v_mix_w_out'], 'v_norm_ffn2': out['v_norm_ffn2'], 'v_ffn2_w_in': out['v_ffn2_w_in'], 'v_ffn2_w_out': out['v_ffn2_w_out'], 'v_norm_final': out['v_norm_final']}


def _loss(weights, diff, rest, loss_target):
    with _jax.named_scope("forward"):
        args = {**rest, TWIN_DIFF_INPUT: diff, **{k: w.astype(_WEIGHT_DTYPES[k]) for k, w in weights.items()}}
        y = _forward(args)
    with _jax.named_scope("loss_head"):
        err = _jnp.square(y.astype(_jnp.float32) - loss_target)
        return 0.5 * _jnp.sum(_jnp.mean(err, axis=-1)) if err.ndim else 0.5 * err


def _adamw(w, g, m, v):
    m = ADAM_B1 * m + (1.0 - ADAM_B1) * g
    v = ADAM_B2 * v + (1.0 - ADAM_B2) * _jnp.square(g)
    m_hat = m / (1.0 - ADAM_B1 ** ADAM_STEP)
    v_hat = v / (1.0 - ADAM_B2 ** ADAM_STEP)
    delta = -ADAM_LR * (m_hat / (_jnp.sqrt(v_hat) + ADAM_EPS) + ADAM_WD * w)
    return delta, m, v


def reference(x, norm_ffn1, ffn1_w_in, ffn1_w_out, norm_mix, mix_w_in, ssm_a_re, ssm_a_im, ssm_log_dt, ssm_b_re, ssm_b_im, ssm_c_re, ssm_c_im, ssm_d, ssm_glu_w, ssm_glu_b, gm_v_gain, gm_w_s, gm_b_s, gain_ssm_out, gain_gm_out, mix_w_out, norm_ffn2, ffn2_w_in, ffn2_w_out, norm_final, loss_target, m_norm_ffn1, m_ffn1_w_in, m_ffn1_w_out, m_norm_mix, m_mix_w_in, m_ssm_a_re, m_ssm_a_im, m_ssm_log_dt, m_ssm_b_re, m_ssm_b_im, m_ssm_c_re, m_ssm_c_im, m_ssm_d, m_ssm_glu_w, m_ssm_glu_b, m_gm_v_gain, m_gm_w_s, m_gm_b_s, m_gain_ssm_out, m_gain_gm_out, m_mix_w_out, m_norm_ffn2, m_ffn2_w_in, m_ffn2_w_out, m_norm_final, v_norm_ffn1, v_ffn1_w_in, v_ffn1_w_out, v_norm_mix, v_mix_w_in, v_ssm_a_re, v_ssm_a_im, v_ssm_log_dt, v_ssm_b_re, v_ssm_b_im, v_ssm_c_re, v_ssm_c_im, v_ssm_d, v_ssm_glu_w, v_ssm_glu_b, v_gm_v_gain, v_gm_w_s, v_gm_b_s, v_gain_ssm_out, v_gain_gm_out, v_mix_w_out, v_norm_ffn2, v_ffn2_w_in, v_ffn2_w_out, v_norm_final):
    given = dict(x=x, norm_ffn1=norm_ffn1, ffn1_w_in=ffn1_w_in, ffn1_w_out=ffn1_w_out, norm_mix=norm_mix, mix_w_in=mix_w_in, ssm_a_re=ssm_a_re, ssm_a_im=ssm_a_im, ssm_log_dt=ssm_log_dt, ssm_b_re=ssm_b_re, ssm_b_im=ssm_b_im, ssm_c_re=ssm_c_re, ssm_c_im=ssm_c_im, ssm_d=ssm_d, ssm_glu_w=ssm_glu_w, ssm_glu_b=ssm_glu_b, gm_v_gain=gm_v_gain, gm_w_s=gm_w_s, gm_b_s=gm_b_s, gain_ssm_out=gain_ssm_out, gain_gm_out=gain_gm_out, mix_w_out=mix_w_out, norm_ffn2=norm_ffn2, ffn2_w_in=ffn2_w_in, ffn2_w_out=ffn2_w_out, norm_final=norm_final, loss_target=loss_target, m_norm_ffn1=m_norm_ffn1, m_ffn1_w_in=m_ffn1_w_in, m_ffn1_w_out=m_ffn1_w_out, m_norm_mix=m_norm_mix, m_mix_w_in=m_mix_w_in, m_ssm_a_re=m_ssm_a_re, m_ssm_a_im=m_ssm_a_im, m_ssm_log_dt=m_ssm_log_dt, m_ssm_b_re=m_ssm_b_re, m_ssm_b_im=m_ssm_b_im, m_ssm_c_re=m_ssm_c_re, m_ssm_c_im=m_ssm_c_im, m_ssm_d=m_ssm_d, m_ssm_glu_w=m_ssm_glu_w, m_ssm_glu_b=m_ssm_glu_b, m_gm_v_gain=m_gm_v_gain, m_gm_w_s=m_gm_w_s, m_gm_b_s=m_gm_b_s, m_gain_ssm_out=m_gain_ssm_out, m_gain_gm_out=m_gain_gm_out, m_mix_w_out=m_mix_w_out, m_norm_ffn2=m_norm_ffn2, m_ffn2_w_in=m_ffn2_w_in, m_ffn2_w_out=m_ffn2_w_out, m_norm_final=m_norm_final, v_norm_ffn1=v_norm_ffn1, v_ffn1_w_in=v_ffn1_w_in, v_ffn1_w_out=v_ffn1_w_out, v_norm_mix=v_norm_mix, v_mix_w_in=v_mix_w_in, v_ssm_a_re=v_ssm_a_re, v_ssm_a_im=v_ssm_a_im, v_ssm_log_dt=v_ssm_log_dt, v_ssm_b_re=v_ssm_b_re, v_ssm_b_im=v_ssm_b_im, v_ssm_c_re=v_ssm_c_re, v_ssm_c_im=v_ssm_c_im, v_ssm_d=v_ssm_d, v_ssm_glu_w=v_ssm_glu_w, v_ssm_glu_b=v_ssm_glu_b, v_gm_v_gain=v_gm_v_gain, v_gm_w_s=v_gm_w_s, v_gm_b_s=v_gm_b_s, v_gain_ssm_out=v_gain_ssm_out, v_gain_gm_out=v_gain_gm_out, v_mix_w_out=v_mix_w_out, v_norm_ffn2=v_norm_ffn2, v_ffn2_w_in=v_ffn2_w_in, v_ffn2_w_out=v_ffn2_w_out, v_norm_final=v_norm_final)
    weights = {n: given[n] for n in TWIN_WEIGHTS}
    shared = {n: given[n] for n in SHARED_INPUTS}
    per_example = {n: given[n] for n in ['x']}
    grad_fn = _jax.value_and_grad(_loss, argnums=(0, 1))

    def one_microbatch(ex, loss_target):
        ex = dict(ex)
        diff = ex.pop(TWIN_DIFF_INPUT)
        return grad_fn(weights, diff, {**shared, **ex}, loss_target)

    if N_MICROBATCH == 1:
        loss, (grad_w, grad_x) = one_microbatch(per_example, given["loss_target"])
    else:
        def body(carry, xs):
            loss_sum, grad_sum = carry
            l_k, (gw_k, gx_k) = one_microbatch(xs[0], xs[1])
            with _jax.named_scope("update"):
                return (loss_sum + l_k, _jax.tree.map(_jnp.add, grad_sum, gw_k)), gx_k

        init = (_jnp.zeros((), _jnp.float32), _jax.tree.map(_jnp.zeros_like, weights))
        (loss, grad_w), grad_x = _jax.lax.scan(body, init, (per_example, given["loss_target"]))
    with _jax.named_scope("update"):
        delta_w, new_m, new_v = {}, {}, {}
        for n in TWIN_WEIGHTS:
            delta_w[n], new_m[n], new_v[n] = _adamw(weights[n], grad_w[n], given["m_" + n], given["v_" + n])
    return (loss, grad_x, *[grad_w[n] for n in TWIN_WEIGHTS], *[delta_w[n] for n in TWIN_WEIGHTS],
            *[new_m[n] for n in TWIN_WEIGHTS], *[new_v[n] for n in TWIN_WEIGHTS])
```

```python
import functools
import math

import jax
import jax.numpy as jnp
from jax import lax
from jax.experimental import pallas as pl
from jax.experimental.pallas import tpu as pltpu

F32 = jnp.float32
BF16 = jnp.bfloat16
MESH = pl.DeviceIdType.MESH
AXES = ("x", "y", "c")

N_DEV = 8
D_MODEL = 1024
D_FF = 2816
FF_SHARD = 2 * D_FF // N_DEV
FF_CHUNKS = 4
SSM_WIDTH = 512
SSM_CH = 16
SSM_GROUPS = 32
SSM_STATE = 64
HALF_GROUPS = 16
HALF_IN = HALF_GROUPS * SSM_CH
HALF_ST = HALF_GROUPS * SSM_STATE
GM_WIDTH = 512
GM_HEADS = 4
GM_HEAD_DIM = 128
GM_CHUNK = 128
IN_COLS = SSM_WIDTH + 2 * GM_WIDTH
EPS = 1e-6
SUBLANES = 8
LANES = 128

ADAM_LR = 0.001
ADAM_B1 = 0.9
ADAM_B2 = 0.999
ADAM_EPS = 1e-08
ADAM_WD = 0.01
ADAM_STEP = 10

VMEM_LIMIT = 52 * 1024 * 1024


def _cp(*sem):
    return pltpu.CompilerParams(dimension_semantics=sem, vmem_limit_bytes=VMEM_LIMIT)


def _rms_fwd(x, g):
    r = lax.rsqrt(jnp.mean(x * x, axis=-1, keepdims=True) + EPS)
    xh = x * r
    return xh * g, xh, r


def _rms_bwd(dy, xh, r, g):
    dxh = dy * g
    dx = r * (dxh - xh * jnp.mean(dxh * xh, axis=-1, keepdims=True))
    return dx, dy * xh


def _rows8(a):
    m, n = a.shape
    return a.reshape(m // SUBLANES, SUBLANES, n).sum(axis=0)


_GELU_K = math.sqrt(2.0 / math.pi)
_GELU_C = 0.044715


def _gelu(x):
    th = jnp.tanh(_GELU_K * (x + _GELU_C * x * x * x))
    return 0.5 * x * (1.0 + th), th


def _gelu_grad(x, th):
    return 0.5 * (1.0 + th) + 0.5 * x * (1.0 - th * th) * (_GELU_K * (1.0 + 3.0 * _GELU_C * x * x))


def _dot(a, b):
    return jnp.dot(a, b, preferred_element_type=F32)


def _dot_nt(a, b):
    return lax.dot_general(a, b, (((1,), (1,)), ((), ())), preferred_element_type=F32)


def _dot_tn(a, b):
    return lax.dot_general(a, b, (((0,), (0,)), ((), ())), preferred_element_type=F32)


def _ffn_fwd(x, gain, w_in_ag, w_out_ag, layer, tm, name):
    n = x.shape[0]

    def body(x_ref, g_ref, wg_ref, wu_ref, wo_ref, o_ref, xn_ref):
        j = pl.program_id(1)

        @pl.when(j == 0)
        def _():
            xv = x_ref[...]
            y, _, _ = _rms_fwd(xv, g_ref[...])
            xn_ref[...] = y.astype(BF16)
            o_ref[...] = xv

        xn = xn_ref[...]
        gg = _dot(xn, wg_ref[...])
        uu = _dot(xn, wu_ref[...])
        act = (gg * jax.nn.sigmoid(gg) * uu).astype(BF16)
        o_ref[...] += 0.5 * _dot(act, wo_ref[...].reshape(FF_SHARD, D_MODEL))

    return pl.pallas_call(
        body, name=name, grid=(n // tm, FF_CHUNKS),
        in_specs=[
            pl.BlockSpec((tm, D_MODEL), lambda i, j: (i, 0)),
            pl.BlockSpec((1, D_MODEL), lambda i, j: (0, 0)),
            pl.BlockSpec((None, None, D_MODEL, FF_SHARD), lambda i, j: (j, layer, 0, 0)),
            pl.BlockSpec((None, None, D_MODEL, FF_SHARD), lambda i, j: (j + FF_CHUNKS, layer, 0, 0)),
            pl.BlockSpec((2, None, FF_SHARD // 2, D_MODEL), lambda i, j: (j, layer, 0, 0)),
        ],
        out_specs=pl.BlockSpec((tm, D_MODEL), lambda i, j: (i, 0)),
        out_shape=jax.ShapeDtypeStruct((n, D_MODEL), F32),
        scratch_shapes=[pltpu.VMEM((tm, D_MODEL), BF16)],
        compiler_params=_cp("parallel", "arbitrary"),
    )(x, gain, w_in_ag, w_in_ag, w_out_ag)


def _ffn_bwd(x, gain, dy, w_in_ag, w_out_ag, layer, tm, name):
    n = x.shape[0]

    def body(x_ref, g_ref, dy_ref, wg_ref, wu_ref, wo_ref, dx_ref, xn_ref, dgu_ref, act_ref, dgain_ref, acc_ref):
        i, j = pl.program_id(0), pl.program_id(1)

        @pl.when(jnp.logical_and(i == 0, j == 0))
        def _():
            dgain_ref[...] = jnp.zeros_like(dgain_ref)

        @pl.when(j == 0)
        def _():
            y, _, _ = _rms_fwd(x_ref[...], g_ref[...])
            xn_ref[...] = y.astype(BF16)
            acc_ref[...] = jnp.zeros_like(acc_ref)

        xn = xn_ref[...]
        wg, wu = wg_ref[...], wu_ref[...]
        gg = _dot(xn, wg)
        uu = _dot(xn, wu)
        dact = 0.5 * _dot_nt(dy_ref[...].astype(BF16), wo_ref[...].reshape(FF_SHARD, D_MODEL))
        sig = jax.nn.sigmoid(gg)
        silu = gg * sig
        act_ref[...] = (silu * uu).astype(BF16)
        du = (dact * silu).astype(BF16)
        dg = (dact * uu * (sig * (1.0 + gg * (1.0 - sig)))).astype(BF16)
        dgu_ref[0] = dg
        dgu_ref[1] = du
        acc_ref[...] += _dot_nt(dg, wg) + _dot_nt(du, wu)

        @pl.when(j == FF_CHUNKS - 1)
        def _():
            g = g_ref[...]
            _, xh, r = _rms_fwd(x_ref[...], g)
            dx, dgr = _rms_bwd(acc_ref[...], xh, r, g)
            dx_ref[...] = dy_ref[...] + dx
            dgain_ref[...] += _rows8(dgr)

    return pl.pallas_call(
        body, name=name, grid=(n // tm, FF_CHUNKS),
        in_specs=[
            pl.BlockSpec((tm, D_MODEL), lambda i, j: (i, 0)),
            pl.BlockSpec((1, D_MODEL), lambda i, j: (0, 0)),
            pl.BlockSpec((tm, D_MODEL), lambda i, j: (i, 0)),
            pl.BlockSpec((None, None, D_MODEL, FF_SHARD), lambda i, j: (j, layer, 0, 0)),
            pl.BlockSpec((None, None, D_MODEL, FF_SHARD), lambda i, j: (j + FF_CHUNKS, layer, 0, 0)),
            pl.BlockSpec((2, None, FF_SHARD // 2, D_MODEL), lambda i, j: (j, layer, 0, 0)),
        ],
        out_specs=[
            pl.BlockSpec((tm, D_MODEL), lambda i, j: (i, 0)),
            pl.BlockSpec((tm, D_MODEL), lambda i, j: (i, 0)),
            pl.BlockSpec((None, 2, tm, FF_SHARD), lambda i, j: (j, 0, i, 0)),
            pl.BlockSpec((None, tm, FF_SHARD), lambda i, j: (j, i, 0)),
            pl.BlockSpec((SUBLANES, D_MODEL), lambda i, j: (0, 0)),
        ],
        out_shape=[
            jax.ShapeDtypeStruct((n, D_MODEL), F32),
            jax.ShapeDtypeStruct((n, D_MODEL), BF16),
            jax.ShapeDtypeStruct((FF_CHUNKS, 2, n, FF_SHARD), BF16),
            jax.ShapeDtypeStruct((FF_CHUNKS, n, FF_SHARD), BF16),
            jax.ShapeDtypeStruct((SUBLANES, D_MODEL), F32),
        ],
        scratch_shapes=[pltpu.VMEM((tm, D_MODEL), F32)],
        compiler_params=_cp("arbitrary", "arbitrary"),
    )(x, gain, dy, w_in_ag, w_in_ag, w_out_ag)


def _ffn_dw_in(xn, dgu, tk, name):
    n = xn.shape[0]
    nk = n // tk

    def body(a_ref, b_ref, o_ref, acc_ref):
        k = pl.program_id(2)

        @pl.when(k == 0)
        def _():
            acc_ref[...] = jnp.zeros_like(acc_ref)

        acc_ref[...] += _dot_tn(a_ref[...], b_ref[...])

        @pl.when(k == nk - 1)
        def _():
            o_ref[...] = acc_ref[...].astype(BF16)

    return pl.pallas_call(
        body, name=name, grid=(FF_CHUNKS, 2, nk),
        in_specs=[
            pl.BlockSpec((tk, D_MODEL), lambda j, p, k: (k, 0)),
            pl.BlockSpec((None, None, tk, FF_SHARD), lambda j, p, k: (j, p, k, 0)),
        ],
        out_specs=pl.BlockSpec((None, D_MODEL, FF_SHARD), lambda j, p, k: (FF_CHUNKS * p + j, 0, 0)),
        out_shape=jax.ShapeDtypeStruct((N_DEV, D_MODEL, FF_SHARD), BF16),
        scratch_shapes=[pltpu.VMEM((D_MODEL, FF_SHARD), F32)],
        compiler_params=_cp("parallel", "parallel", "arbitrary"),
    )(xn, dgu)


def _ffn_dw_out(act, dy, tk, name):
    n = act.shape[1]
    nk = n // tk

    def body(a_ref, b_ref, o_ref, acc_ref):
        k = pl.program_id(1)

        @pl.when(k == 0)
        def _():
            acc_ref[...] = jnp.zeros_like(acc_ref)

        acc_ref[...] += _dot_tn(a_ref[...], b_ref[...].astype(BF16))

        @pl.when(k == nk - 1)
        def _():
            o_ref[...] = (0.5 * acc_ref[...]).astype(BF16)

    return pl.pallas_call(
        body, name=name, grid=(FF_CHUNKS, nk),
        in_specs=[
            pl.BlockSpec((None, tk, FF_SHARD), lambda j, k: (j, k, 0)),
            pl.BlockSpec((tk, D_MODEL), lambda j, k: (k, 0)),
        ],
        out_specs=pl.BlockSpec((None, FF_SHARD, D_MODEL), lambda j, k: (j, 0, 0)),
        out_shape=jax.ShapeDtypeStruct((FF_CHUNKS, FF_SHARD, D_MODEL), BF16),
        scratch_shapes=[pltpu.VMEM((FF_SHARD, D_MODEL), F32)],
        compiler_params=_cp("parallel", "arbitrary"),
    )(act, dy)


def _mix_in_fwd(x, gain, w, tm, name):
    n = x.shape[0]

    def body(x_ref, g_ref, w_ref, z_ref):
        y, _, _ = _rms_fwd(x_ref[...], g_ref[...])
        z_ref[...] = _dot(y.astype(BF16), w_ref[...])

    return pl.pallas_call(
        body, name=name, grid=(n // tm,),
        in_specs=[
            pl.BlockSpec((tm, D_MODEL), lambda i: (i, 0)),
            pl.BlockSpec((1, D_MODEL), lambda i: (0, 0)),
            pl.BlockSpec((D_MODEL, IN_COLS), lambda i: (0, 0)),
        ],
        out_specs=pl.BlockSpec((tm, IN_COLS), lambda i: (i, 0)),
        out_shape=jax.ShapeDtypeStruct((n, IN_COLS), F32),
        compiler_params=_cp("parallel"),
    )(x, gain, w)


def _mix_in_bwd(x, gain, du_ssm, du_gm, dv_gm, d_res, w, tm, name):
    n = x.shape[0]

    def body(x_ref, g_ref, d0_ref, d1_ref, d2_ref, dres_ref, w_ref, dx_ref, dw_ref, dgain_ref):
        i = pl.program_id(0)

        @pl.when(i == 0)
        def _():
            dw_ref[...] = jnp.zeros_like(dw_ref)
            dgain_ref[...] = jnp.zeros_like(dgain_ref)

        g = g_ref[...]
        y, xh, r = _rms_fwd(x_ref[...], g)
        xn = y.astype(BF16)
        dxn = jnp.zeros((tm, D_MODEL), F32)
        for k, d_ref in enumerate((d0_ref, d1_ref, d2_ref)):
            dz = d_ref[...].astype(BF16)
            cols = slice(k * SSM_WIDTH, (k + 1) * SSM_WIDTH)
            dxn += _dot_nt(dz, w_ref[:, cols])
            dw_ref[:, cols] += _dot_tn(xn, dz)
        dx, dgr = _rms_bwd(dxn, xh, r, g)
        dx_ref[...] = dres_ref[...] + dx
        dgain_ref[...] += _rows8(dgr)

    row = lambda i: (i, 0)
    fixed = lambda i: (0, 0)
    return pl.pallas_call(
        body, name=name, grid=(n // tm,),
        in_specs=[
            pl.BlockSpec((tm, D_MODEL), row),
            pl.BlockSpec((1, D_MODEL), fixed),
            pl.BlockSpec((tm, SSM_WIDTH), row),
            pl.BlockSpec((tm, GM_WIDTH), row),
            pl.BlockSpec((tm, GM_WIDTH), row),
            pl.BlockSpec((tm, D_MODEL), row),
            pl.BlockSpec((D_MODEL, IN_COLS), fixed),
        ],
        out_specs=[
            pl.BlockSpec((tm, D_MODEL), row),
            pl.BlockSpec((D_MODEL, IN_COLS), fixed),
            pl.BlockSpec((SUBLANES, D_MODEL), fixed),
        ],
        out_shape=[
            jax.ShapeDtypeStruct((n, D_MODEL), F32),
            jax.ShapeDtypeStruct((D_MODEL, IN_COLS), F32),
            jax.ShapeDtypeStruct((SUBLANES, D_MODEL), F32),
        ],
        compiler_params=_cp("arbitrary"),
    )(x, gain, du_ssm, du_gm, dv_gm, d_res, w)


def _mix_out_fwd(y_ssm, y_gm, g_ssm, g_gm, w, x, tm, name):
    n = x.shape[0]

    def body(ys_ref, yg_ref, gs_ref, gg_ref, w_ref, x_ref, o_ref):
        a, _, _ = _rms_fwd(ys_ref[...], gs_ref[...])
        b, _, _ = _rms_fwd(yg_ref[...], gg_ref[...])
        o_ref[...] = (x_ref[...] + _dot(a.astype(BF16), w_ref[0:SSM_WIDTH, :])
                      + _dot(b.astype(BF16), w_ref[SSM_WIDTH:D_MODEL, :]))

    row = lambda i: (i, 0)
    fixed = lambda i: (0, 0)
    return pl.pallas_call(
        body, name=name, grid=(n // tm,),
        in_specs=[
            pl.BlockSpec((tm, SSM_WIDTH), row), pl.BlockSpec((tm, GM_WIDTH), row),
            pl.BlockSpec((1, SSM_WIDTH), fixed), pl.BlockSpec((1, GM_WIDTH), fixed),
            pl.BlockSpec((D_MODEL, D_MODEL), fixed), pl.BlockSpec((tm, D_MODEL), row),
        ],
        out_specs=pl.BlockSpec((tm, D_MODEL), row),
        out_shape=jax.ShapeDtypeStruct((n, D_MODEL), F32),
        compiler_params=_cp("parallel"),
    )(y_ssm, y_gm, g_ssm, g_gm, w, x)


def _mix_out_bwd(y_ssm, y_gm, g_ssm, g_gm, w, dx, tm, name):
    n = dx.shape[0]

    def body(ys_ref, yg_ref, gs_ref, gg_ref, w_ref, dx_ref, dys_ref, dyg_ref, dw_ref, dgs_ref, dgg_ref):
        i = pl.program_id(0)

        @pl.when(i == 0)
        def _():
            dw_ref[...] = jnp.zeros_like(dw_ref)
            dgs_ref[...] = jnp.zeros_like(dgs_ref)
            dgg_ref[...] = jnp.zeros_like(dgg_ref)

        dxb = dx_ref[...].astype(BF16)
        parts = ((ys_ref, gs_ref, dys_ref, dgs_ref, 0), (yg_ref, gg_ref, dyg_ref, dgg_ref, SSM_WIDTH))
        for y_ref, g_ref, dy_ref, dg_ref, off in parts:
            g = g_ref[...]
            yn, xh, r = _rms_fwd(y_ref[...], g)
            rows = slice(off, off + SSM_WIDTH)
            dyn = _dot_nt(dxb, w_ref[rows, :])
            dw_ref[rows, :] += _dot_tn(yn.astype(BF16), dxb)
            dy, dgr = _rms_bwd(dyn, xh, r, g)
            dy_ref[...] = dy
            dg_ref[...] += _rows8(dgr)

    row = lambda i: (i, 0)
    fixed = lambda i: (0, 0)
    return pl.pallas_call(
        body, name=name, grid=(n // tm,),
        in_specs=[
            pl.BlockSpec((tm, SSM_WIDTH), row), pl.BlockSpec((tm, GM_WIDTH), row),
            pl.BlockSpec((1, SSM_WIDTH), fixed), pl.BlockSpec((1, GM_WIDTH), fixed),
            pl.BlockSpec((D_MODEL, D_MODEL), fixed), pl.BlockSpec((tm, D_MODEL), row),
        ],
        out_specs=[
            pl.BlockSpec((tm, SSM_WIDTH), row), pl.BlockSpec((tm, GM_WIDTH), row),
            pl.BlockSpec((D_MODEL, D_MODEL), fixed),
            pl.BlockSpec((SUBLANES, SSM_WIDTH), fixed), pl.BlockSpec((SUBLANES, GM_WIDTH), fixed),
        ],
        out_shape=[
            jax.ShapeDtypeStruct((n, SSM_WIDTH), F32), jax.ShapeDtypeStruct((n, GM_WIDTH), F32),
            jax.ShapeDtypeStruct((D_MODEL, D_MODEL), F32),
            jax.ShapeDtypeStruct((SUBLANES, SSM_WIDTH), F32), jax.ShapeDtypeStruct((SUBLANES, GM_WIDTH), F32),
        ],
        compiler_params=_cp("arbitrary"),
    )(y_ssm, y_gm, g_ssm, g_gm, w, dx)


SCAN_W = 512
SCAN_PIECES = HALF_ST // SCAN_W


def _scan_tiles(src_ref, dst_ref, dst_off, c_ref, half, carry_ref, n_tiles, reverse, extra=None):
    shifts = (1, 2, 4)
    carry_row = 0 if reverse else SUBLANES - 1

    def cols(piece, im):
        lo = im * HALF_ST + piece * SCAN_W
        return slice(lo, lo + SCAN_W)

    def step(t, state):
        carries, accs = state
        k = (n_tiles - 1 - t) if reverse else t
        rows = pl.ds(pl.multiple_of(k * SUBLANES, SUBLANES), SUBLANES)
        new_carries, new_accs = [], []
        for piece in range(SCAN_PIECES):
            cr, ci = carries[piece]
            xr0 = src_ref[rows, cols(piece, 0)]
            xi0 = src_ref[rows, cols(piece, 1)]
            xr, xi = xr0, xi0
            for si, s in enumerate(shifts):
                ar = c_ref[half, si, :, cols(piece, 0)]
                ai = c_ref[half, si, :, cols(piece, 1)]
                sh = (SUBLANES - s) if reverse else s
                sr = pltpu.roll(xr, sh, 0)
                sm = pltpu.roll(xi, sh, 0)
                xr, xi = xr + (ar * sr - ai * sm), xi + (ar * sm + ai * sr)
            pr = c_ref[half, 3, :, cols(piece, 0)]
            pi = c_ref[half, 3, :, cols(piece, 1)]
            hr = xr + (pr * cr - pi * ci)
            hi = xi + (pr * ci + pi * cr)
            dst_ref[rows, pl.ds(dst_off + piece * SCAN_W, SCAN_W)] = hr
            dst_ref[rows, pl.ds(dst_off + HALF_ST + piece * SCAN_W, SCAN_W)] = hi
            new_carries.append((jnp.broadcast_to(hr[carry_row:carry_row + 1, :], (SUBLANES, SCAN_W)),
                                jnp.broadcast_to(hi[carry_row:carry_row + 1, :], (SUBLANES, SCAN_W))))
            if extra is not None:
                new_accs.append(extra(rows, piece, (xr0, xi0), (hr, hi), accs[piece]))
        return tuple(new_carries), tuple(new_accs)

    base = half * 2 * HALF_ST
    carries0 = tuple((carry_ref[:, pl.ds(base + p * SCAN_W, SCAN_W)],
                      carry_ref[:, pl.ds(base + HALF_ST + p * SCAN_W, SCAN_W)]) for p in range(SCAN_PIECES))
    zero = jnp.zeros((SUBLANES, SCAN_W), F32)
    accs0 = tuple((zero, zero) for _ in range(SCAN_PIECES)) if extra is not None else ()
    carries, accs = lax.fori_loop(0, n_tiles, step, (carries0, accs0))
    for p in range(SCAN_PIECES):
        carry_ref[:, pl.ds(base + p * SCAN_W, SCAN_W)] = carries[p][0]
        carry_ref[:, pl.ds(base + HALF_ST + p * SCAN_W, SCAN_W)] = carries[p][1]
    return accs


def _ssm_tail(hb, u, c_ref, glu_ref, glub_ref, dskip_ref):
    ypre = u * dskip_ref[...]
    parts = []
    for half in range(2):
        parts.append(_dot(hb[half], c_ref[half]))
    ypre = ypre + jnp.concatenate(parts, axis=1)
    yg, th = _gelu(ypre)
    zz = _dot(yg.astype(BF16), glu_ref[...]) + glub_ref[...]
    z1, z2 = zz[:, :SSM_WIDTH], zz[:, SSM_WIDTH:]
    sg = jax.nn.sigmoid(z2)
    return ypre, th, yg, z1, sg


def _ssm_fwd(z, bblk, cblk, glu, glub, dskip, fwdc, batch, t_chunk, name):
    n = z.shape[0]
    nk = n // batch // t_chunk
    n_tiles = t_chunk // SUBLANES

    def body(u_ref, b_ref, c_ref, glu_ref, glub_ref, dskip_ref, k_ref, y_ref, h_ref, bu_ref, carry_ref):
        @pl.when(pl.program_id(1) == 0)
        def _():
            carry_ref[...] = jnp.zeros_like(carry_ref)

        u = u_ref[...]
        ub = u.astype(BF16)
        for half in range(2):
            bu_ref[...] = _dot(ub[:, half * HALF_IN:(half + 1) * HALF_IN], b_ref[half])
            _scan_tiles(bu_ref, h_ref, half * 2 * HALF_ST, k_ref, half, carry_ref, n_tiles, False)
        hb = [h_ref[:, half * 2 * HALF_ST:(half + 1) * 2 * HALF_ST].astype(BF16) for half in range(2)]
        _, _, _, z1, sg = _ssm_tail(hb, u, c_ref, glu_ref, glub_ref, dskip_ref)
        y_ref[...] = z1 * sg

    fixed2 = lambda b, k: (0, 0)
    fixed3 = lambda b, k: (0, 0, 0)
    row = lambda b, k: (b * nk + k, 0)
    return pl.pallas_call(
        body, name=name, grid=(batch, nk),
        in_specs=[
            pl.BlockSpec((t_chunk, SSM_WIDTH), row),
            pl.BlockSpec((2, HALF_IN, 2 * HALF_ST), fixed3),
            pl.BlockSpec((2, 2 * HALF_ST, HALF_IN), fixed3),
            pl.BlockSpec((SSM_WIDTH, 2 * SSM_WIDTH), fixed2),
            pl.BlockSpec((1, 2 * SSM_WIDTH), fixed2),
            pl.BlockSpec((1, SSM_WIDTH), fixed2),
            pl.BlockSpec((2, 4, SUBLANES, 2 * HALF_ST), lambda b, k: (0, 0, 0, 0)),
        ],
        out_specs=[pl.BlockSpec((t_chunk, SSM_WIDTH), row), pl.BlockSpec((t_chunk, 4 * HALF_ST), row)],
        out_shape=[jax.ShapeDtypeStruct((n, SSM_WIDTH), F32), jax.ShapeDtypeStruct((n, 4 * HALF_ST), F32)],
        scratch_shapes=[pltpu.VMEM((t_chunk, 2 * HALF_ST), F32), pltpu.VMEM((SUBLANES, 4 * HALF_ST), F32)],
        compiler_params=_cp("parallel", "arbitrary"),
    )(z, bblk, cblk, glu, glub, dskip, fwdc)


def _ssm_bwd(z, h, dy, bblk, cblk, glu, glub, dskip, revc, batch, t_chunk, name):
    n = z.shape[0]
    nk = n // batch // t_chunk
    n_tiles = t_chunk // SUBLANES

    def body(u_ref, h_ref, dy_ref, b_ref, c_ref, glu_ref, glub_ref, dskip_ref, k_ref,
             du_ref, dglu_ref, dglub_ref, ddskip_ref, dct_ref, db_ref, q_ref, g_ref, carry_ref):
        first = jnp.logical_and(pl.program_id(0) == 0, pl.program_id(1) == 0)

        @pl.when(first)
        def _():
            for r in (dglu_ref, dglub_ref, ddskip_ref, dct_ref, db_ref, q_ref):
                r[...] = jnp.zeros_like(r)

        @pl.when(pl.program_id(1) == 0)
        def _():
            carry_ref[...] = jnp.zeros_like(carry_ref)

        u = u_ref[...]
        ub = u.astype(BF16)
        hb = [h_ref[:, half * 2 * HALF_ST:(half + 1) * 2 * HALF_ST].astype(BF16) for half in range(2)]
        ypre, th, yg, z1, sg = _ssm_tail(hb, u, c_ref, glu_ref, glub_ref, dskip_ref)
        dout = dy_ref[...]
        dz = jnp.concatenate([dout * sg, dout * z1 * sg * (1.0 - sg)], axis=1)
        dzb = dz.astype(BF16)
        dglu_ref[...] += _dot_tn(yg.astype(BF16), dzb)
        dglub_ref[...] += _rows8(dz)
        dypre = _dot_nt(dzb, glu_ref[...]) * _gelu_grad(ypre, th)
        ddskip_ref[...] += _rows8(dypre * u)
        dypb = dypre.astype(BF16)
        du_parts = []
        for half in range(2):
            dyp_h = dypb[:, half * HALF_IN:(half + 1) * HALF_IN]
            dct_ref[half] += _dot_tn(dyp_h, hb[half])
            g_ref[...] = _dot_nt(dyp_h, c_ref[half])

            def extra(rows, piece, x_in, g_out, acc, half=half):
                er, ei = g_out[0] - x_in[0], g_out[1] - x_in[1]
                base = half * 2 * HALF_ST + piece * SCAN_W
                hr = h_ref[rows, pl.ds(base, SCAN_W)]
                hi = h_ref[rows, pl.ds(base + HALF_ST, SCAN_W)]
                return acc[0] + (er * hr + ei * hi), acc[1] + (er * hi - ei * hr)

            accs = _scan_tiles(g_ref, g_ref, 0, k_ref, half, carry_ref, n_tiles, True, extra)
            for piece in range(SCAN_PIECES):
                base = half * 2 * HALF_ST + piece * SCAN_W
                q_ref[:, pl.ds(base, SCAN_W)] += accs[piece][0]
                q_ref[:, pl.ds(base + HALF_ST, SCAN_W)] += accs[piece][1]
            gb = g_ref[...].astype(BF16)
            db_ref[half] += _dot_tn(ub[:, half * HALF_IN:(half + 1) * HALF_IN], gb)
            du_parts.append(_dot_nt(gb, b_ref[half]))
        du_ref[...] = dypre * dskip_ref[...] + jnp.concatenate(du_parts, axis=1)

    fixed2 = lambda b, k: (0, 0)
    fixed3 = lambda b, k: (0, 0, 0)
    row = lambda b, k: (b * nk + (nk - 1 - k), 0)
    return pl.pallas_call(
        body, name=name, grid=(batch, nk),
        in_specs=[
            pl.BlockSpec((t_chunk, SSM_WIDTH), row),
            pl.BlockSpec((t_chunk, 4 * HALF_ST), row),
            pl.BlockSpec((t_chunk, SSM_WIDTH), row),
            pl.BlockSpec((2, HALF_IN, 2 * HALF_ST), fixed3),
            pl.BlockSpec((2, 2 * HALF_ST, HALF_IN), fixed3),
            pl.BlockSpec((SSM_WIDTH, 2 * SSM_WIDTH), fixed2),
            pl.BlockSpec((1, 2 * SSM_WIDTH), fixed2),
            pl.BlockSpec((1, SSM_WIDTH), fixed2),
            pl.BlockSpec((2, 4, SUBLANES, 2 * HALF_ST), lambda b, k: (0, 0, 0, 0)),
        ],
        out_specs=[
            pl.BlockSpec((t_chunk, SSM_WIDTH), row),
            pl.BlockSpec((SSM_WIDTH, 2 * SSM_WIDTH), fixed2),
            pl.BlockSpec((SUBLANES, 2 * SSM_WIDTH), fixed2),
            pl.BlockSpec((SUBLANES, SSM_WIDTH), fixed2),
            pl.BlockSpec((2, HALF_IN, 2 * HALF_ST), fixed3),
            pl.BlockSpec((2, HALF_IN, 2 * HALF_ST), fixed3),
            pl.BlockSpec((SUBLANES, 4 * HALF_ST), fixed2),
        ],
        out_shape=[
            jax.ShapeDtypeStruct((n, SSM_WIDTH), F32),
            jax.ShapeDtypeStruct((SSM_WIDTH, 2 * SSM_WIDTH), F32),
            jax.ShapeDtypeStruct((SUBLANES, 2 * SSM_WIDTH), F32),
            jax.ShapeDtypeStruct((SUBLANES, SSM_WIDTH), F32),
            jax.ShapeDtypeStruct((2, HALF_IN, 2 * HALF_ST), F32),
            jax.ShapeDtypeStruct((2, HALF_IN, 2 * HALF_ST), F32),
            jax.ShapeDtypeStruct((SUBLANES, 4 * HALF_ST), F32),
        ],
        scratch_shapes=[pltpu.VMEM((t_chunk, 2 * HALF_ST), F32), pltpu.VMEM((SUBLANES, 4 * HALF_ST), F32)],
        compiler_params=_cp("arbitrary", "arbitrary"),
    )(z, h, dy, bblk, cblk, glu, glub, dskip, revc)


def _gm_chunk_fwd(u, v, gain_ref, w_ref, bias_ref):
    ug, thu = _gelu(u)
    vg, thv = _gelu(v)
    rs, vns, ss = [], [], []
    for hh in range(GM_HEADS):
        cs = slice(hh * GM_HEAD_DIM, (hh + 1) * GM_HEAD_DIM)
        vn, _, r = _rms_fwd(vg[:, cs], gain_ref[:, cs])
        s = _dot(w_ref[hh], vn.astype(BF16)) + bias_ref[:, cs]
        rs.append(r)
        vns.append(vn)
        ss.append(s)
    return ug, thu, thv, vg, rs, vns, ss


def _gm_fwd(z, gain, w_tril, bias, rows, name):
    n = z.shape[0]
    chunks = rows // GM_CHUNK

    def body(u_ref, v_ref, gain_ref, w_ref, bias_ref, y_ref):
        for c in range(chunks):
            rs_ = slice(c * GM_CHUNK, (c + 1) * GM_CHUNK)
            ug, _, _, _, _, _, ss = _gm_chunk_fwd(u_ref[rs_, :], v_ref[rs_, :], gain_ref, w_ref, bias_ref)
            y_ref[rs_, :] = ug * jnp.concatenate(ss, axis=1)

    return pl.pallas_call(
        body, name=name, grid=(n // rows,),
        in_specs=[
            pl.BlockSpec((rows, GM_WIDTH), lambda i: (i, 1)),
            pl.BlockSpec((rows, GM_WIDTH), lambda i: (i, 2)),
            pl.BlockSpec((1, GM_WIDTH), lambda i: (0, 0)),
            pl.BlockSpec((GM_HEADS, GM_CHUNK, GM_CHUNK), lambda i: (0, 0, 0)),
            pl.BlockSpec((GM_CHUNK, GM_WIDTH), lambda i: (0, 0)),
        ],
        out_specs=pl.BlockSpec((rows, GM_WIDTH), lambda i: (i, 0)),
        out_shape=jax.ShapeDtypeStruct((n, GM_WIDTH), F32),
        compiler_params=_cp("parallel"),
    )(z, z, gain, w_tril, bias)


def _gm_bwd(z, dy, gain, w_tril, bias, rows, name):
    n = z.shape[0]
    chunks = rows // GM_CHUNK

    def body(u_ref, v_ref, dy_ref, gain_ref, w_ref, bias_ref, du_ref, dv_ref, dw_ref, dbias_ref, dgain_ref):
        @pl.when(pl.program_id(0) == 0)
        def _():
            dw_ref[...] = jnp.zeros_like(dw_ref)
            dbias_ref[...] = jnp.zeros_like(dbias_ref)
            dgain_ref[...] = jnp.zeros_like(dgain_ref)

        for c in range(chunks):
            rs_ = slice(c * GM_CHUNK, (c + 1) * GM_CHUNK)
            u, v = u_ref[rs_, :], v_ref[rs_, :]
            ug, thu, thv, vg, rs, vns, ss = _gm_chunk_fwd(u, v, gain_ref, w_ref, bias_ref)
            dout = dy_ref[rs_, :]
            ds = dout * ug
            du_ref[rs_, :] = dout * jnp.concatenate(ss, axis=1) * _gelu_grad(u, thu)
            dbias_ref[...] += ds
            dvg_parts, dgain_parts = [], []
            for hh in range(GM_HEADS):
                cs = slice(hh * GM_HEAD_DIM, (hh + 1) * GM_HEAD_DIM)
                dsb = ds[:, cs].astype(BF16)
                dvn = _dot_tn(w_ref[hh], dsb)
                dw_ref[hh] += _dot_nt(dsb, vns[hh].astype(BF16))
                g = gain_ref[:, cs]
                xh = vg[:, cs] * rs[hh]
                dvg, dgr = _rms_bwd(dvn, xh, rs[hh], g)
                dvg_parts.append(dvg)
                dgain_parts.append(dgr)
            dv_ref[rs_, :] = jnp.concatenate(dvg_parts, axis=1) * _gelu_grad(v, thv)
            dgain_ref[...] += _rows8(jnp.concatenate(dgain_parts, axis=1))

    row = lambda i: (i, 0)
    return pl.pallas_call(
        body, name=name, grid=(n // rows,),
        in_specs=[
            pl.BlockSpec((rows, GM_WIDTH), lambda i: (i, 1)),
            pl.BlockSpec((rows, GM_WIDTH), lambda i: (i, 2)),
            pl.BlockSpec((rows, GM_WIDTH), row),
            pl.BlockSpec((1, GM_WIDTH), lambda i: (0, 0)),
            pl.BlockSpec((GM_HEADS, GM_CHUNK, GM_CHUNK), lambda i: (0, 0, 0)),
            pl.BlockSpec((GM_CHUNK, GM_WIDTH), lambda i: (0, 0)),
        ],
        out_specs=[
            pl.BlockSpec((rows, GM_WIDTH), row), pl.BlockSpec((rows, GM_WIDTH), row),
            pl.BlockSpec((GM_HEADS, GM_CHUNK, GM_CHUNK), lambda i: (0, 0, 0)),
            pl.BlockSpec((GM_CHUNK, GM_WIDTH), lambda i: (0, 0)),
            pl.BlockSpec((SUBLANES, GM_WIDTH), lambda i: (0, 0)),
        ],
        out_shape=[
            jax.ShapeDtypeStruct((n, GM_WIDTH), F32), jax.ShapeDtypeStruct((n, GM_WIDTH), F32),
            jax.ShapeDtypeStruct((GM_HEADS, GM_CHUNK, GM_CHUNK), F32),
            jax.ShapeDtypeStruct((GM_CHUNK, GM_WIDTH), F32),
            jax.ShapeDtypeStruct((SUBLANES, GM_WIDTH), F32),
        ],
        compiler_params=_cp("arbitrary"),
    )(z, z, dy, gain, w_tril, bias)


def _loss_head(x, gain, target, tm, name):
    n = x.shape[0]

    def body(x_ref, g_ref, t_ref, dx_ref, sq_ref, dgain_ref):
        @pl.when(pl.program_id(0) == 0)
        def _():
            sq_ref[...] = jnp.zeros_like(sq_ref)
            dgain_ref[...] = jnp.zeros_like(dgain_ref)

        g = g_ref[...]
        y, xh, r = _rms_fwd(x_ref[...], g)
        err = y - t_ref[...]
        sq_ref[...] += _rows8(err * err)
        dx, dgr = _rms_bwd(err * (1.0 / D_MODEL), xh, r, g)
        dx_ref[...] = dx
        dgain_ref[...] += _rows8(dgr)

    row = lambda i: (i, 0)
    fixed = lambda i: (0, 0)
    return pl.pallas_call(
        body, name=name, grid=(n // tm,),
        in_specs=[pl.BlockSpec((tm, D_MODEL), row), pl.BlockSpec((1, D_MODEL), fixed), pl.BlockSpec((tm, D_MODEL), row)],
        out_specs=[pl.BlockSpec((tm, D_MODEL), row), pl.BlockSpec((SUBLANES, D_MODEL), fixed),
                   pl.BlockSpec((SUBLANES, D_MODEL), fixed)],
        out_shape=[jax.ShapeDtypeStruct((n, D_MODEL), F32), jax.ShapeDtypeStruct((SUBLANES, D_MODEL), F32),
                   jax.ShapeDtypeStruct((SUBLANES, D_MODEL), F32)],
        compiler_params=_cp("arbitrary"),
    )(x, gain, target)


def _adam_math(w, g, m, v):
    m2 = ADAM_B1 * m + (1.0 - ADAM_B1) * g
    v2 = ADAM_B2 * v + (1.0 - ADAM_B2) * (g * g)
    m_hat = m2 / (1.0 - ADAM_B1 ** ADAM_STEP)
    v_hat = v2 / (1.0 - ADAM_B2 ** ADAM_STEP)
    delta = -ADAM_LR * (m_hat / (jnp.sqrt(v_hat) + ADAM_EPS) + ADAM_WD * w)
    return delta, m2, v2


def _adam_sharded(parts, w, m, v, tr, name):
    r, c = w.shape

    def body(p_ref, w_ref, m_ref, v_ref, g_ref, d_ref, m2_ref, v2_ref):
        g = p_ref[0].astype(F32)
        for s in range(1, N_DEV):
            g = g + p_ref[s].astype(F32)
        delta, m2, v2 = _adam_math(w_ref[...], g, m_ref[...], v_ref[...])
        g_ref[...] = g
        d_ref[...] = delta
        m2_ref[...] = m2
        v2_ref[...] = v2

    blk = pl.BlockSpec((tr, c), lambda i: (i, 0))
    return pl.pallas_call(
        body, name=name, grid=(r // tr,),
        in_specs=[pl.BlockSpec((N_DEV, tr, c), lambda i: (0, i, 0)), blk, blk, blk],
        out_specs=[blk, blk, blk, blk],
        out_shape=[jax.ShapeDtypeStruct((r, c), F32)] * 4,
        compiler_params=_cp("parallel"),
    )(parts, w, m, v)


def _adam_packed(g, w, m, v, name):
    r, c = g.shape

    def body(g_ref, w_ref, m_ref, v_ref, d_ref, m2_ref, v2_ref):
        delta, m2, v2 = _adam_math(w_ref[...], g_ref[...], m_ref[...], v_ref[...])
        d_ref[...] = delta
        m2_ref[...] = m2
        v2_ref[...] = v2

    blk = pl.BlockSpec((r, c), lambda i: (0, 0))
    return pl.pallas_call(
        body, name=name, grid=(1,),
        in_specs=[blk, blk, blk, blk], out_specs=[blk, blk, blk],
        out_shape=[jax.ShapeDtypeStruct((r, c), F32)] * 3,
        compiler_params=_cp("arbitrary"),
    )(g, w, m, v)


def _my_place():
    return lax.axis_index("x"), lax.axis_index("y"), lax.axis_index("c")


def _flip(place, rel):
    x, y, c = place
    return (1 - x if rel & 4 else x, 1 - y if rel & 2 else y, 1 - c if rel & 1 else c)


def _index(place):
    return 4 * place[0] + 2 * place[1] + place[2]


def _all_gather(shards, name):
    na = len(shards)

    def body(*refs):
        xs, outs = refs[:na], refs[na:2 * na]
        send_sems, recv_sems, local_sems = refs[2 * na:]
        me = _my_place()
        sibling = _flip(me, 1)
        chips = [_flip(me, 4), _flip(me, 2), _flip(me, 6)]

        def copy(a, k, block, to, src=None):
            slot = outs[a].at[_index(block)]
            return pltpu.make_async_remote_copy(
                src_ref=slot if src is None else src, dst_ref=slot,
                send_sem=send_sems.at[a, k], recv_sem=recv_sems.at[a, k],
                device_id=to, device_id_type=MESH)

        mine = [pltpu.make_async_copy(xs[a], outs[a].at[_index(me)], local_sems.at[a]) for a in range(na)]
        for cp in mine:
            cp.start()
        first = []
        for a in range(na):
            first.append(copy(a, 0, me, sibling, src=xs[a]))
            first += [copy(a, 1 + j, me, chip, src=xs[a]) for j, chip in enumerate(chips)]
        for cp in first:
            cp.start()
        passed = []
        for a in range(na):
            for j, chip in enumerate(chips):
                copy(a, 1 + j, chip, me).wait_recv()
                fwd = copy(a, 4 + j, chip, sibling)
                fwd.start()
                passed.append(fwd)
        for a in range(na):
            copy(a, 0, sibling, me).wait_recv()
            for j, chip in enumerate(chips):
                copy(a, 4 + j, _flip(chip, 1), me).wait_recv()
        for cp in first + passed:
            cp.wait_send()
        for cp in mine:
            cp.wait()

    hbm = pl.BlockSpec(memory_space=pl.ANY)
    return pl.pallas_call(
        body, name=name,
        in_specs=[hbm] * na, out_specs=[hbm] * na,
        out_shape=[jax.ShapeDtypeStruct((N_DEV,) + s.shape, s.dtype) for s in shards],
        scratch_shapes=[pltpu.SemaphoreType.DMA((na, 7)), pltpu.SemaphoreType.DMA((na, 7)),
                        pltpu.SemaphoreType.DMA((na,))],
    )(*shards)


def _reduce_scatter_send(parts, name):
    na = len(parts)

    def body(*refs):
        ps, outs = refs[:na], refs[na:2 * na]
        send_sems, recv_sems, local_sems = refs[2 * na:]
        me = _my_place()
        mine = [pltpu.make_async_copy(ps[a].at[_index(me)], outs[a].at[_index(me)], local_sems.at[a])
                for a in range(na)]
        for cp in mine:
            cp.start()
        sends = []
        for a in range(na):
            for rel in range(1, N_DEV):
                peer = _flip(me, rel)
                sends.append(pltpu.make_async_remote_copy(
                    src_ref=ps[a].at[_index(peer)], dst_ref=outs[a].at[_index(me)],
                    send_sem=send_sems.at[a, rel - 1], recv_sem=recv_sems.at[a, rel - 1],
                    device_id=peer, device_id_type=MESH))
        for cp in sends:
            cp.start()
        for a in range(na):
            for rel in range(1, N_DEV):
                peer = _flip(me, rel)
                pltpu.make_async_remote_copy(
                    src_ref=ps[a].at[_index(me)], dst_ref=outs[a].at[_index(peer)],
                    send_sem=send_sems.at[a, rel - 1], recv_sem=recv_sems.at[a, rel - 1],
                    device_id=peer, device_id_type=MESH).wait_recv()
        for cp in sends:
            cp.wait_send()
        for cp in mine:
            cp.wait()

    hbm = pl.BlockSpec(memory_space=pl.ANY)
    return pl.pallas_call(
        body, name=name,
        in_specs=[hbm] * na, out_specs=[hbm] * na,
        out_shape=[jax.ShapeDtypeStruct(p.shape, p.dtype) for p in parts],
        scratch_shapes=[pltpu.SemaphoreType.DMA((na, 7)), pltpu.SemaphoreType.DMA((na, 7)),
                        pltpu.SemaphoreType.DMA((na,))],
    )(*parts)


def _all_reduce_small(g, name):
    _, r, c = g.shape

    def body(g_ref, o_ref, land_ref, red_ref, send1, recv1, send2, recv2):
        me = _my_place()
        idx = _index(me)

        def scatter(rel):
            peer = _flip(me, rel)
            return pltpu.make_async_remote_copy(
                src_ref=g_ref.at[_index(peer)], dst_ref=land_ref.at[idx],
                send_sem=send1.at[rel - 1], recv_sem=recv1.at[rel - 1], device_id=peer, device_id_type=MESH)

        def gather(rel):
            peer = _flip(me, rel)
            return pltpu.make_async_remote_copy(
                src_ref=red_ref, dst_ref=o_ref.at[idx],
                send_sem=send2.at[rel - 1], recv_sem=recv2.at[rel - 1], device_id=peer, device_id_type=MESH)

        for rel in range(1, N_DEV):
            scatter(rel).start()
        land_ref[idx] = g_ref[idx]
        for rel in range(1, N_DEV):
            scatter(rel).wait()
        acc = land_ref[0]
        for s in range(1, N_DEV):
            acc = acc + land_ref[s]
        red_ref[...] = acc
        for rel in range(1, N_DEV):
            gather(rel).start()
        o_ref[idx] = acc
        for rel in range(1, N_DEV):
            gather(rel).wait()

    vmem = pl.BlockSpec(memory_space=pltpu.VMEM)
    return pl.pallas_call(
        body, name=name,
        in_specs=[vmem], out_specs=vmem,
        out_shape=jax.ShapeDtypeStruct(g.shape, F32),
        scratch_shapes=[pltpu.VMEM(g.shape, F32), pltpu.VMEM((r, c), F32)]
        + [pltpu.SemaphoreType.DMA((N_DEV - 1,))] * 4,
        compiler_params=pltpu.CompilerParams(vmem_limit_bytes=VMEM_LIMIT),
    )(g)


def _ssm_discretize(a_re, a_im, log_dt, b_re, b_im):
    dt = jnp.exp(log_dt)[:, None]
    mag = jnp.exp(a_re * dt)
    lr, li = mag * jnp.cos(a_im * dt), mag * jnp.sin(a_im * dt)
    den = a_re * a_re + a_im * a_im
    qr = ((lr - 1.0) * a_re + li * a_im) / den
    qi = (li * a_re - (lr - 1.0) * a_im) / den
    bbr = qr[..., None] * b_re - qi[..., None] * b_im
    bbi = qr[..., None] * b_im + qi[..., None] * b_re
    return lr, li, bbr, bbi


def _halves(a):
    return a.reshape((2, HALF_GROUPS) + a.shape[1:])


def _block_diag(blocks):
    g, r, c = blocks.shape
    eye = jnp.eye(g, dtype=blocks.dtype)
    return jnp.einsum("grc,gh->grhc", blocks, eye).reshape(g * r, g * c)


def _block_diag_take(dense, g, r, c):
    return jnp.einsum("grhc,gh->grc", dense.reshape(g, r, g, c), jnp.eye(g, dtype=dense.dtype))


def _ssm_matrices(bbr, bbi, c_re, c_im, glu_w, glu_b, d_skip):
    bre, bim = _halves(jnp.swapaxes(bbr, 1, 2)), _halves(jnp.swapaxes(bbi, 1, 2))
    bblk = jnp.stack([jnp.concatenate([_block_diag(bre[h]), _block_diag(bim[h])], axis=1) for h in range(2)])
    cre, cim = _halves(jnp.swapaxes(c_re, 1, 2)), _halves(jnp.swapaxes(c_im, 1, 2))
    cblk = jnp.stack([jnp.concatenate([_block_diag(cre[h]), -_block_diag(cim[h])], axis=0) for h in range(2)])
    glu = jnp.concatenate([_block_diag(glu_w[:, :, :SSM_CH]), _block_diag(glu_w[:, :, SSM_CH:])], axis=1)
    glub = jnp.concatenate([glu_b[:, :SSM_CH].reshape(1, -1), glu_b[:, SSM_CH:].reshape(1, -1)], axis=1)
    return bblk.astype(BF16), cblk.astype(BF16), glu.astype(BF16), glub, d_skip.reshape(1, -1)


def _scan_constants(lr, li, reverse):
    if reverse:
        li = -li
    pows = [(lr, li)]
    for _ in range(SUBLANES - 1):
        pr, pi = pows[-1]
        pows.append((pr * lr - pi * li, pr * li + pi * lr))
    row = jnp.arange(SUBLANES)[:, None]

    def flat(a):
        return a.reshape(2, 1, HALF_ST)

    mats = []
    for s in (1, 2, 4):
        keep = (row + s <= SUBLANES - 1) if reverse else (row >= s)
        mats.append(tuple(jnp.where(keep[None], flat(p), 0.0) for p in pows[s - 1]))
    order = [SUBLANES - 1 - j for j in range(SUBLANES)] if reverse else list(range(SUBLANES))
    mats.append(tuple(jnp.concatenate([flat(pows[j][k]) for j in order], axis=1) for k in range(2)))
    return jnp.stack([jnp.concatenate([m[0], m[1]], axis=2) for m in mats], axis=1)


def _pack(arrs, rows):
    flat = jnp.concatenate([a.reshape(-1) for a in arrs])
    return jnp.pad(flat, (0, rows * LANES - flat.shape[0])).reshape(rows, LANES)


def _unpack(buf, like):
    flat = buf.reshape(-1)
    out, off = [], 0
    for a in like:
        out.append(flat[off:off + a.size].reshape(a.shape))
        off += a.size
    return out


SMALL = ("norm_ffn1", "norm_mix", "ssm_a_re", "ssm_a_im", "ssm_log_dt", "ssm_b_re", "ssm_b_im", "ssm_c_re",
         "ssm_c_im", "ssm_d", "ssm_glu_w", "ssm_glu_b", "gm_v_gain", "gm_w_s", "gm_b_s", "gain_ssm_out",
         "gain_gm_out", "norm_ffn2", "norm_final")
BIG = ("ffn1_w_in", "ffn1_w_out", "mix_w_in", "mix_w_out", "ffn2_w_in", "ffn2_w_out")
WEIGHTS = ("norm_ffn1", "ffn1_w_in", "ffn1_w_out", "norm_mix", "mix_w_in", "ssm_a_re", "ssm_a_im", "ssm_log_dt",
           "ssm_b_re", "ssm_b_im", "ssm_c_re", "ssm_c_im", "ssm_d", "ssm_glu_w", "ssm_glu_b", "gm_v_gain", "gm_w_s",
           "gm_b_s", "gain_ssm_out", "gain_gm_out", "mix_w_out", "norm_ffn2", "ffn2_w_in", "ffn2_w_out", "norm_final")


def _step(x, target, w, m, v):
    batch, seq, _ = x.shape
    n = batch * seq
    depth = w["norm_ffn1"].shape[0]
    tm = min(512, n)
    tk = min(2048, n)
    t_chunk = min(256, seq)
    gm_rows = min(512, seq)
    x = x.reshape(n, D_MODEL)
    target = target.reshape(n, D_MODEL)

    big_bf = [w[k].astype(BF16) for k in BIG]
    f1i, f1o, mwi, mwo, f2i, f2o = _all_gather(big_bf, "all_gather_weights")
    mwi = jnp.transpose(mwi, (1, 2, 0, 3)).reshape(depth, D_MODEL, IN_COLS)
    mwo = jnp.transpose(mwo, (1, 0, 2, 3)).reshape(depth, D_MODEL, D_MODEL)

    tril = jnp.tril(jnp.ones((GM_CHUNK, GM_CHUNK), bool))
    layers = []
    for l in range(depth):
        disc, disc_vjp = jax.vjp(_ssm_discretize, w["ssm_a_re"][l], w["ssm_a_im"][l], w["ssm_log_dt"][l],
                                 w["ssm_b_re"][l], w["ssm_b_im"][l])
        lr, li, bbr, bbi = disc
        bblk, cblk, glu, glub, dskip = _ssm_matrices(bbr, bbi, w["ssm_c_re"][l], w["ssm_c_im"][l],
                                                     w["ssm_glu_w"][l], w["ssm_glu_b"][l], w["ssm_d"][l])
        layers.append(dict(
            disc_vjp=disc_vjp, lr=lr, li=li, bblk=bblk, cblk=cblk, glu=glu, glub=glub, dskip=dskip,
            fwdc=_scan_constants(lr, li, False), revc=_scan_constants(lr, li, True),
            w_tril=jnp.where(tril[None], w["gm_w_s"][l], 0.0).astype(BF16),
            gm_bias=jnp.repeat(w["gm_b_s"][l].T, GM_HEAD_DIM, axis=1),
            g1=w["norm_ffn1"][l][None], gmix=w["norm_mix"][l][None], g2=w["norm_ffn2"][l][None],
            gv=w["gm_v_gain"][l][None], gs=w["gain_ssm_out"][l][None], gg=w["gain_gm_out"][l][None],
        ))

    saved = []
    for l in range(depth):
        p = layers[l]
        x0 = x
        x1 = _ffn_fwd(x0, p["g1"], f1i, f1o, l, tm, f"ffn1_fwd_{l}")
        z = _mix_in_fwd(x1, p["gmix"], mwi[l], tm, f"mix_in_fwd_{l}")
        y_ssm, h = _ssm_fwd(z, p["bblk"], p["cblk"], p["glu"], p["glub"], p["dskip"], p["fwdc"], batch, t_chunk,
                            f"ssm_fwd_{l}")
        y_gm = _gm_fwd(z, p["gv"], p["w_tril"], p["gm_bias"], gm_rows, f"gm_fwd_{l}")
        x2 = _mix_out_fwd(y_ssm, y_gm, p["gs"], p["gg"], mwo[l], x1, tm, f"mix_out_fwd_{l}")
        x = _ffn_fwd(x2, p["g2"], f2i, f2o, l, tm, f"ffn2_fwd_{l}")
        saved.append((x0, x1, x2, z, h, y_ssm, y_gm))

    dx, sq, dnf = _loss_head(x, w["norm_final"][None], target, tm, "loss_head")
    loss = lax.psum((0.5 / D_MODEL) * jnp.sum(sq), AXES)

    small = {k: [None] * depth for k in SMALL if k != "norm_final"}
    big_parts = {k: [None] * depth for k in BIG}
    for l in reversed(range(depth)):
        p = layers[l]
        x0, x1, x2, z, h, y_ssm, y_gm = saved[l]
        dx_out = dx
        dx, xn, dgu, act, dgain = _ffn_bwd(x2, p["g2"], dx_out, f2i, f2o, l, tm, f"ffn2_bwd_{l}")
        big_parts["ffn2_w_in"][l] = _ffn_dw_in(xn, dgu, tk, f"ffn2_dw_in_{l}")
        big_parts["ffn2_w_out"][l] = _ffn_dw_out(act, dx_out, tk, f"ffn2_dw_out_{l}").reshape(
            N_DEV, FF_SHARD // 2, D_MODEL)
        small["norm_ffn2"][l] = dgain.sum(0)

        dy_ssm, dy_gm, dwo, dgs, dgg = _mix_out_bwd(y_ssm, y_gm, p["gs"], p["gg"], mwo[l], dx, tm, f"mix_out_bwd_{l}")
        big_parts["mix_w_out"][l] = dwo.astype(BF16).reshape(N_DEV, D_MODEL // N_DEV, D_MODEL)
        small["gain_ssm_out"][l] = dgs.sum(0)
        small["gain_gm_out"][l] = dgg.sum(0)

        du_ssm, dglu, dglub, ddskip, dct, db, q = _ssm_bwd(
            z, h, dy_ssm, p["bblk"], p["cblk"], p["glu"], p["glub"], p["dskip"], p["revc"], batch, t_chunk,
            f"ssm_bwd_{l}")
        du_gm, dv_gm, dws, dbias, dgv = _gm_bwd(z, dy_gm, p["gv"], p["w_tril"], p["gm_bias"], gm_rows, f"gm_bwd_{l}")

        q = q.sum(0).reshape(2, 2, HALF_GROUPS, SSM_STATE)
        qr, qi = q[:, 0].reshape(SSM_GROUPS, SSM_STATE), q[:, 1].reshape(SSM_GROUPS, SSM_STATE)
        den = p["lr"] * p["lr"] + p["li"] * p["li"]
        d_re = (qr * p["lr"] + qi * p["li"]) / den
        d_im = (qi * p["lr"] - qr * p["li"]) / den
        dbb = jnp.stack([_block_diag_take(db[hf, :, k * HALF_ST:(k + 1) * HALF_ST], HALF_GROUPS, SSM_CH, SSM_STATE)
                         for k in range(2) for hf in range(2)]).reshape(2, SSM_GROUPS, SSM_CH, SSM_STATE)
        dcc = jnp.stack([_block_diag_take(dct[hf, :, k * HALF_ST:(k + 1) * HALF_ST], HALF_GROUPS, SSM_CH, SSM_STATE)
                         for k in range(2) for hf in range(2)]).reshape(2, SSM_GROUPS, SSM_CH, SSM_STATE)
        da_re, da_im, dlog_dt, db_re, db_im = p["disc_vjp"](
            (d_re, -d_im, jnp.swapaxes(dbb[0], 1, 2), jnp.swapaxes(dbb[1], 1, 2)))
        small["ssm_a_re"][l], small["ssm_a_im"][l], small["ssm_log_dt"][l] = da_re, da_im, dlog_dt
        small["ssm_b_re"][l], small["ssm_b_im"][l] = db_re, db_im
        small["ssm_c_re"][l], small["ssm_c_im"][l] = dcc[0], -dcc[1]
        small["ssm_d"][l] = ddskip.sum(0).reshape(SSM_GROUPS, SSM_CH)
        small["ssm_glu_w"][l] = jnp.concatenate(
            [_block_diag_take(dglu[:, :SSM_WIDTH], SSM_GROUPS, SSM_CH, SSM_CH),
             _block_diag_take(dglu[:, SSM_WIDTH:], SSM_GROUPS, SSM_CH, SSM_CH)], axis=2)
        dglub = dglub.sum(0)
        small["ssm_glu_b"][l] = jnp.concatenate(
            [dglub[:SSM_WIDTH].reshape(SSM_GROUPS, SSM_CH), dglub[SSM_WIDTH:].reshape(SSM_GROUPS, SSM_CH)], axis=1)
        small["gm_v_gain"][l] = dgv.sum(0)
        small["gm_w_s"][l] = jnp.where(tril[None], dws, 0.0)
        small["gm_b_s"][l] = dbias.reshape(GM_CHUNK, GM_HEADS, GM_HEAD_DIM).sum(-1).T

        dx, dwi, dgain = _mix_in_bwd(x1, p["gmix"], du_ssm, du_gm, dv_gm, dx, mwi[l], tm, f"mix_in_bwd_{l}")
        big_parts["mix_w_in"][l] = jnp.transpose(
            dwi.astype(BF16).reshape(D_MODEL, N_DEV, IN_COLS // N_DEV), (1, 0, 2))
        small["norm_mix"][l] = dgain.sum(0)

        dx_out = dx
        dx, xn, dgu, act, dgain = _ffn_bwd(x0, p["g1"], dx_out, f1i, f1o, l, tm, f"ffn1_bwd_{l}")
        big_parts["ffn1_w_in"][l] = _ffn_dw_in(xn, dgu, tk, f"ffn1_dw_in_{l}")
        big_parts["ffn1_w_out"][l] = _ffn_dw_out(act, dx_out, tk, f"ffn1_dw_out_{l}").reshape(
            N_DEV, FF_SHARD // 2, D_MODEL)
        small["norm_ffn1"][l] = dgain.sum(0)

    grad_x = dx.reshape(batch, seq, D_MODEL)

    grads, deltas, new_m, new_v = {}, {}, {}, {}
    parts = [jnp.stack(big_parts[k], axis=1) for k in BIG]
    landed = _reduce_scatter_send(parts, "reduce_scatter_grads")
    for k, got in zip(BIG, landed):
        shape = w[k].shape
        r, c = shape[0] * shape[1], shape[2]
        tr = max(t for t in range(16, 129, 16) if r % t == 0)
        outs = _adam_sharded(got.reshape(N_DEV, r, c), w[k].reshape(r, c), m[k].reshape(r, c), v[k].reshape(r, c),
                             tr, f"adam_{k}")
        grads[k], deltas[k], new_m[k], new_v[k] = [o.reshape(shape) for o in outs]

    small_g = [jnp.stack(small[k]) if k != "norm_final" else dnf.sum(0) for k in SMALL]
    total = sum(int(math.prod(w[k].shape)) for k in SMALL)
    rows = -(-total // (LANES * N_DEV * SUBLANES)) * N_DEV * SUBLANES
    g_all = _all_reduce_small(_pack(small_g, rows).reshape(N_DEV, rows // N_DEV, LANES), "all_reduce_small")
    g_all = g_all.reshape(rows, LANES)
    like = [w[k] for k in SMALL]
    d_p, m_p, v_p = _adam_packed(g_all, _pack(like, rows), _pack([m[k] for k in SMALL], rows),
                                 _pack([v[k] for k in SMALL], rows), "adam_small")
    for k, g_, d_, m_, v_ in zip(SMALL, _unpack(g_all, like), _unpack(d_p, like), _unpack(m_p, like),
                                 _unpack(v_p, like)):
        grads[k], deltas[k], new_m[k], new_v[k] = g_, d_, m_, v_
    return loss, grad_x, grads, deltas, new_m, new_v


def kernel(x, norm_ffn1, ffn1_w_in, ffn1_w_out, norm_mix, mix_w_in, ssm_a_re, ssm_a_im, ssm_log_dt, ssm_b_re, ssm_b_im, ssm_c_re, ssm_c_im, ssm_d, ssm_glu_w, ssm_glu_b, gm_v_gain, gm_w_s, gm_b_s, gain_ssm_out, gain_gm_out, mix_w_out, norm_ffn2, ffn2_w_in, ffn2_w_out, norm_final, loss_target, m_norm_ffn1, m_ffn1_w_in, m_ffn1_w_out, m_norm_mix, m_mix_w_in, m_ssm_a_re, m_ssm_a_im, m_ssm_log_dt, m_ssm_b_re, m_ssm_b_im, m_ssm_c_re, m_ssm_c_im, m_ssm_d, m_ssm_glu_w, m_ssm_glu_b, m_gm_v_gain, m_gm_w_s, m_gm_b_s, m_gain_ssm_out, m_gain_gm_out, m_mix_w_out, m_norm_ffn2, m_ffn2_w_in, m_ffn2_w_out, m_norm_final, v_norm_ffn1, v_ffn1_w_in, v_ffn1_w_out, v_norm_mix, v_mix_w_in, v_ssm_a_re, v_ssm_a_im, v_ssm_log_dt, v_ssm_b_re, v_ssm_b_im, v_ssm_c_re, v_ssm_c_im, v_ssm_d, v_ssm_glu_w, v_ssm_glu_b, v_gm_v_gain, v_gm_w_s, v_gm_b_s, v_gain_ssm_out, v_gain_gm_out, v_mix_w_out, v_norm_ffn2, v_ffn2_w_in, v_ffn2_w_out, v_norm_final):
    args = locals()
    w = {k: args[k] for k in WEIGHTS}
    m = {k: args["m_" + k] for k in WEIGHTS}
    v = {k: args["v_" + k] for k in WEIGHTS}
    loss, grad_x, grads, deltas, new_m, new_v = _step(x, loss_target, w, m, v)
    return (loss, grad_x, *[grads[k] for k in WEIGHTS], *[deltas[k] for k in WEIGHTS],
            *[new_m[k] for k in WEIGHTS], *[new_v[k] for k in WEIGHTS])
```

```python
import functools
import math

import jax
import jax.numpy as jnp
from jax import lax
from jax.experimental import pallas as pl
from jax.experimental.pallas import tpu as pltpu

F32 = jnp.float32
BF16 = jnp.bfloat16
MESH = pl.DeviceIdType.MESH
AXES = ("x", "y", "c")

N_DEV = 8
D_MODEL = 1024
D_FF = 2816
FF_SHARD = 2 * D_FF // N_DEV
FF_CHUNKS = 4
SSM_WIDTH = 512
SSM_CH = 16
SSM_GROUPS = 32
SSM_STATE = 64
HALF_GROUPS = 16
HALF_IN = HALF_GROUPS * SSM_CH
HALF_ST = HALF_GROUPS * SSM_STATE
GM_WIDTH = 512
GM_HEADS = 4
GM_HEAD_DIM = 128
GM_CHUNK = 128
IN_COLS = SSM_WIDTH + 2 * GM_WIDTH
EPS = 1e-6
SUBLANES = 8
LANES = 128

ADAM_LR = 0.001
ADAM_B1 = 0.9
ADAM_B2 = 0.999
ADAM_EPS = 1e-08
ADAM_WD = 0.01
ADAM_STEP = 10

VMEM_LIMIT = 52 * 1024 * 1024


def _cp(*sem):
    return pltpu.CompilerParams(dimension_semantics=sem, vmem_limit_bytes=VMEM_LIMIT)


def _rms_fwd(x, g):
    r = lax.rsqrt(jnp.mean(x * x, axis=-1, keepdims=True) + EPS)
    xh = x * r
    return xh * g, xh, r


def _rms_bwd(dy, xh, r, g):
    dxh = dy * g
    dx = r * (dxh - xh * jnp.mean(dxh * xh, axis=-1, keepdims=True))
    return dx, dy * xh


def _rows8(a):
    m, n = a.shape
    return a.reshape(m // SUBLANES, SUBLANES, n).sum(axis=0)


_GELU_K = math.sqrt(2.0 / math.pi)
_GELU_C = 0.044715


def _gelu(x):
    th = jnp.tanh(_GELU_K * (x + _GELU_C * x * x * x))
    return 0.5 * x * (1.0 + th), th


def _gelu_grad(x, th):
    return 0.5 * (1.0 + th) + 0.5 * x * (1.0 - th * th) * (_GELU_K * (1.0 + 3.0 * _GELU_C * x * x))


def _dot(a, b):
    return jnp.dot(a, b, preferred_element_type=F32)


def _dot_nt(a, b):
    return lax.dot_general(a, b, (((1,), (1,)), ((), ())), preferred_element_type=F32)


def _dot_tn(a, b):
    return lax.dot_general(a, b, (((0,), (0,)), ((), ())), preferred_element_type=F32)


def _ffn_fwd(x, gain, w_in_ag, w_out_ag, tm, name):
    n = x.shape[0]

    def body(x_ref, g_ref, wg_ref, wu_ref, wo_ref, o_ref, xn_ref):
        j = pl.program_id(1)

        @pl.when(j == 0)
        def _():
            xv = x_ref[...]
            y, _, _ = _rms_fwd(xv, g_ref[...])
            xn_ref[...] = y.astype(BF16)
            o_ref[...] = xv

        xn = xn_ref[...]
        gg = _dot(xn, wg_ref[...])
        uu = _dot(xn, wu_ref[...])
        act = (gg * jax.nn.sigmoid(gg) * uu).astype(BF16)
        o_ref[...] += 0.5 * _dot(act, wo_ref[...].reshape(FF_SHARD, D_MODEL))

    return pl.pallas_call(
        body, name=name, grid=(n // tm, FF_CHUNKS),
        in_specs=[
            pl.BlockSpec((tm, D_MODEL), lambda i, j: (i, 0)),
            pl.BlockSpec((1, D_MODEL), lambda i, j: (0, 0)),
            pl.BlockSpec((None, D_MODEL, FF_SHARD), lambda i, j: (j, 0, 0)),
            pl.BlockSpec((None, D_MODEL, FF_SHARD), lambda i, j: (j + FF_CHUNKS, 0, 0)),
            pl.BlockSpec((2, FF_SHARD // 2, D_MODEL), lambda i, j: (j, 0, 0)),
        ],
        out_specs=pl.BlockSpec((tm, D_MODEL), lambda i, j: (i, 0)),
        out_shape=jax.ShapeDtypeStruct((n, D_MODEL), F32),
        scratch_shapes=[pltpu.VMEM((tm, D_MODEL), BF16)],
        compiler_params=_cp("parallel", "arbitrary"),
    )(x, gain, w_in_ag, w_in_ag, w_out_ag)


def _ffn_bwd(x, gain, dy, w_in_ag, w_out_ag, tm, name):
    n = x.shape[0]

    def body(x_ref, g_ref, dy_ref, wg_ref, wu_ref, wo_ref, dx_ref, xn_ref, dgu_ref, act_ref, dgain_ref, acc_ref):
        i, j = pl.program_id(0), pl.program_id(1)

        @pl.when(jnp.logical_and(i == 0, j == 0))
        def _():
            dgain_ref[...] = jnp.zeros_like(dgain_ref)

        @pl.when(j == 0)
        def _():
            y, _, _ = _rms_fwd(x_ref[...], g_ref[...])
            xn_ref[...] = y.astype(BF16)
            acc_ref[...] = jnp.zeros_like(acc_ref)

        xn = xn_ref[...]
        wg, wu = wg_ref[...], wu_ref[...]
        gg = _dot(xn, wg)
        uu = _dot(xn, wu)
        dact = 0.5 * _dot_nt(dy_ref[...].astype(BF16), wo_ref[...].reshape(FF_SHARD, D_MODEL))
        sig = jax.nn.sigmoid(gg)
        silu = gg * sig
        act_ref[...] = (silu * uu).astype(BF16)
        du = (dact * silu).astype(BF16)
        dg = (dact * uu * (sig * (1.0 + gg * (1.0 - sig)))).astype(BF16)
        dgu_ref[0] = dg
        dgu_ref[1] = du
        acc_ref[...] += _dot_nt(dg, wg) + _dot_nt(du, wu)

        @pl.when(j == FF_CHUNKS - 1)
        def _():
            g = g_ref[...]
            _, xh, r = _rms_fwd(x_ref[...], g)
            dx, dgr = _rms_bwd(acc_ref[...], xh, r, g)
            dx_ref[...] = dy_ref[...] + dx
            dgain_ref[...] += _rows8(dgr)

    return pl.pallas_call(
        body, name=name, grid=(n // tm, FF_CHUNKS),
        in_specs=[
            pl.BlockSpec((tm, D_MODEL), lambda i, j: (i, 0)),
            pl.BlockSpec((1, D_MODEL), lambda i, j: (0, 0)),
            pl.BlockSpec((tm, D_MODEL), lambda i, j: (i, 0)),
            pl.BlockSpec((None, D_MODEL, FF_SHARD), lambda i, j: (j, 0, 0)),
            pl.BlockSpec((None, D_MODEL, FF_SHARD), lambda i, j: (j + FF_CHUNKS, 0, 0)),
            pl.BlockSpec((2, FF_SHARD // 2, D_MODEL), lambda i, j: (j, 0, 0)),
        ],
        out_specs=[
            pl.BlockSpec((tm, D_MODEL), lambda i, j: (i, 0)),
            pl.BlockSpec((tm, D_MODEL), lambda i, j: (i, 0)),
            pl.BlockSpec((None, 2, tm, FF_SHARD), lambda i, j: (j, 0, i, 0)),
            pl.BlockSpec((None, tm, FF_SHARD), lambda i, j: (j, i, 0)),
            pl.BlockSpec((SUBLANES, D_MODEL), lambda i, j: (0, 0)),
        ],
        out_shape=[
            jax.ShapeDtypeStruct((n, D_MODEL), F32),
            jax.ShapeDtypeStruct((n, D_MODEL), BF16),
            jax.ShapeDtypeStruct((FF_CHUNKS, 2, n, FF_SHARD), BF16),
            jax.ShapeDtypeStruct((FF_CHUNKS, n, FF_SHARD), BF16),
            jax.ShapeDtypeStruct((SUBLANES, D_MODEL), F32),
        ],
        scratch_shapes=[pltpu.VMEM((tm, D_MODEL), F32)],
        compiler_params=_cp("arbitrary", "arbitrary"),
    )(x, gain, dy, w_in_ag, w_in_ag, w_out_ag)


def _ffn_dw_in(xn, dgu, tk, name):
    n = xn.shape[0]
    nk = n // tk

    def body(a_ref, b_ref, o_ref, acc_ref):
        k = pl.program_id(2)

        @pl.when(k == 0)
        def _():
            acc_ref[...] = jnp.zeros_like(acc_ref)

        acc_ref[...] += _dot_tn(a_ref[...], b_ref[...])

        @pl.when(k == nk - 1)
        def _():
            o_ref[...] = acc_ref[...].astype(BF16)

    return pl.pallas_call(
        body, name=name, grid=(FF_CHUNKS, 2, nk),
        in_specs=[
            pl.BlockSpec((tk, D_MODEL), lambda j, p, k: (k, 0)),
            pl.BlockSpec((None, None, tk, FF_SHARD), lambda j, p, k: (j, p, k, 0)),
        ],
        out_specs=pl.BlockSpec((None, D_MODEL, FF_SHARD), lambda j, p, k: (FF_CHUNKS * p + j, 0, 0)),
        out_shape=jax.ShapeDtypeStruct((N_DEV, D_MODEL, FF_SHARD), BF16),
        scratch_shapes=[pltpu.VMEM((D_MODEL, FF_SHARD), F32)],
        compiler_params=_cp("parallel", "parallel", "arbitrary"),
    )(xn, dgu)


def _ffn_dw_out(act, dy, tk, name):
    n = act.shape[1]
    nk = n // tk

    def body(a_ref, b_ref, o_ref, acc_ref):
        k = pl.program_id(1)

        @pl.when(k == 0)
        def _():
            acc_ref[...] = jnp.zeros_like(acc_ref)

        acc_ref[...] += _dot_tn(a_ref[...], b_ref[...].astype(BF16))

        @pl.when(k == nk - 1)
        def _():
            o_ref[...] = (0.5 * acc_ref[...]).astype(BF16)

    return pl.pallas_call(
        body, name=name, grid=(FF_CHUNKS, nk),
        in_specs=[
            pl.BlockSpec((None, tk, FF_SHARD), lambda j, k: (j, k, 0)),
            pl.BlockSpec((tk, D_MODEL), lambda j, k: (k, 0)),
        ],
        out_specs=pl.BlockSpec((None, FF_SHARD, D_MODEL), lambda j, k: (j, 0, 0)),
        out_shape=jax.ShapeDtypeStruct((FF_CHUNKS, FF_SHARD, D_MODEL), BF16),
        scratch_shapes=[pltpu.VMEM((FF_SHARD, D_MODEL), F32)],
        compiler_params=_cp("parallel", "arbitrary"),
    )(act, dy)


def _mix_in_fwd(x, gain, w, tm, name):
    n = x.shape[0]

    def body(x_ref, g_ref, w_ref, z_ref):
        y, _, _ = _rms_fwd(x_ref[...], g_ref[...])
        z_ref[...] = _dot(y.astype(BF16), w_ref[...])

    return pl.pallas_call(
        body, name=name, grid=(n // tm,),
        in_specs=[
            pl.BlockSpec((tm, D_MODEL), lambda i: (i, 0)),
            pl.BlockSpec((1, D_MODEL), lambda i: (0, 0)),
            pl.BlockSpec((D_MODEL, IN_COLS), lambda i: (0, 0)),
        ],
        out_specs=pl.BlockSpec((tm, IN_COLS), lambda i: (i, 0)),
        out_shape=jax.ShapeDtypeStruct((n, IN_COLS), F32),
        compiler_params=_cp("parallel"),
    )(x, gain, w)


def _mix_in_bwd(x, gain, du_ssm, du_gm, dv_gm, d_res, w, tm, name):
    n = x.shape[0]

    def body(x_ref, g_ref, d0_ref, d1_ref, d2_ref, dres_ref, w_ref, dx_ref, dw_ref, dgain_ref):
        i = pl.program_id(0)

        @pl.when(i == 0)
        def _():
            dw_ref[...] = jnp.zeros_like(dw_ref)
            dgain_ref[...] = jnp.zeros_like(dgain_ref)

        g = g_ref[...]
        y, xh, r = _rms_fwd(x_ref[...], g)
        xn = y.astype(BF16)
        dxn = jnp.zeros((tm, D_MODEL), F32)
        for k, d_ref in enumerate((d0_ref, d1_ref, d2_ref)):
            dz = d_ref[...].astype(BF16)
            cols = slice(k * SSM_WIDTH, (k + 1) * SSM_WIDTH)
            dxn += _dot_nt(dz, w_ref[:, cols])
            dw_ref[:, cols] += _dot_tn(xn, dz)
        dx, dgr = _rms_bwd(dxn, xh, r, g)
        dx_ref[...] = dres_ref[...] + dx
        dgain_ref[...] += _rows8(dgr)

    row = lambda i: (i, 0)
    fixed = lambda i: (0, 0)
    return pl.pallas_call(
        body, name=name, grid=(n // tm,),
        in_specs=[
            pl.BlockSpec((tm, D_MODEL), row),
            pl.BlockSpec((1, D_MODEL), fixed),
            pl.BlockSpec((tm, SSM_WIDTH), row),
            pl.BlockSpec((tm, GM_WIDTH), row),
            pl.BlockSpec((tm, GM_WIDTH), row),
            pl.BlockSpec((tm, D_MODEL), row),
            pl.BlockSpec((D_MODEL, IN_COLS), fixed),
        ],
        out_specs=[
            pl.BlockSpec((tm, D_MODEL), row),
            pl.BlockSpec((D_MODEL, IN_COLS), fixed),
            pl.BlockSpec((SUBLANES, D_MODEL), fixed),
        ],
        out_shape=[
            jax.ShapeDtypeStruct((n, D_MODEL), F32),
            jax.ShapeDtypeStruct((D_MODEL, IN_COLS), F32),
            jax.ShapeDtypeStruct((SUBLANES, D_MODEL), F32),
        ],
        compiler_params=_cp("arbitrary"),
    )(x, gain, du_ssm, du_gm, dv_gm, d_res, w)


def _mix_out_fwd(y_ssm, y_gm, g_ssm, g_gm, w, x, tm, name):
    n = x.shape[0]

    def body(ys_ref, yg_ref, gs_ref, gg_ref, w_ref, x_ref, o_ref):
        a, _, _ = _rms_fwd(ys_ref[...], gs_ref[...])
        b, _, _ = _rms_fwd(yg_ref[...], gg_ref[...])
        o_ref[...] = (x_ref[...] + _dot(a.astype(BF16), w_ref[0:SSM_WIDTH, :])
                      + _dot(b.astype(BF16), w_ref[SSM_WIDTH:D_MODEL, :]))

    row = lambda i: (i, 0)
    fixed = lambda i: (0, 0)
    return pl.pallas_call(
        body, name=name, grid=(n // tm,),
        in_specs=[
            pl.BlockSpec((tm, SSM_WIDTH), row), pl.BlockSpec((tm, GM_WIDTH), row),
            pl.BlockSpec((1, SSM_WIDTH), fixed), pl.BlockSpec((1, GM_WIDTH), fixed),
            pl.BlockSpec((D_MODEL, D_MODEL), fixed), pl.BlockSpec((tm, D_MODEL), row),
        ],
        out_specs=pl.BlockSpec((tm, D_MODEL), row),
        out_shape=jax.ShapeDtypeStruct((n, D_MODEL), F32),
        compiler_params=_cp("parallel"),
    )(y_ssm, y_gm, g_ssm, g_gm, w, x)


def _mix_out_bwd(y_ssm, y_gm, g_ssm, g_gm, w, dx, tm, name):
    n = dx.shape[0]

    def body(ys_ref, yg_ref, gs_ref, gg_ref, w_ref, dx_ref, dys_ref, dyg_ref, dw_ref, dgs_ref, dgg_ref):
        i = pl.program_id(0)

        @pl.when(i == 0)
        def _():
            dw_ref[...] = jnp.zeros_like(dw_ref)
            dgs_ref[...] = jnp.zeros_like(dgs_ref)
            dgg_ref[...] = jnp.zeros_like(dgg_ref)

        dxb = dx_ref[...].astype(BF16)
        parts = ((ys_ref, gs_ref, dys_ref, dgs_ref, 0), (yg_ref, gg_ref, dyg_ref, dgg_ref, SSM_WIDTH))
        for y_ref, g_ref, dy_ref, dg_ref, off in parts:
            g = g_ref[...]
            yn, xh, r = _rms_fwd(y_ref[...], g)
            rows = slice(off, off + SSM_WIDTH)
            dyn = _dot_nt(dxb, w_ref[rows, :])
            dw_ref[rows, :] += _dot_tn(yn.astype(BF16), dxb)
            dy, dgr = _rms_bwd(dyn, xh, r, g)
            dy_ref[...] = dy
            dg_ref[...] += _rows8(dgr)

    row = lambda i: (i, 0)
    fixed = lambda i: (0, 0)
    return pl.pallas_call(
        body, name=name, grid=(n // tm,),
        in_specs=[
            pl.BlockSpec((tm, SSM_WIDTH), row), pl.BlockSpec((tm, GM_WIDTH), row),
            pl.BlockSpec((1, SSM_WIDTH), fixed), pl.BlockSpec((1, GM_WIDTH), fixed),
            pl.BlockSpec((D_MODEL, D_MODEL), fixed), pl.BlockSpec((tm, D_MODEL), row),
        ],
        out_specs=[
            pl.BlockSpec((tm, SSM_WIDTH), row), pl.BlockSpec((tm, GM_WIDTH), row),
            pl.BlockSpec((D_MODEL, D_MODEL), fixed),
            pl.BlockSpec((SUBLANES, SSM_WIDTH), fixed), pl.BlockSpec((SUBLANES, GM_WIDTH), fixed),
        ],
        out_shape=[
            jax.ShapeDtypeStruct((n, SSM_WIDTH), F32), jax.ShapeDtypeStruct((n, GM_WIDTH), F32),
            jax.ShapeDtypeStruct((D_MODEL, D_MODEL), F32),
            jax.ShapeDtypeStruct((SUBLANES, SSM_WIDTH), F32), jax.ShapeDtypeStruct((SUBLANES, GM_WIDTH), F32),
        ],
        compiler_params=_cp("arbitrary"),
    )(y_ssm, y_gm, g_ssm, g_gm, w, dx)


SCAN_W = 512
SCAN_PIECES = HALF_ST // SCAN_W


def _scan_tiles(src_ref, dst_ref, dst_off, c_ref, half, carry_ref, n_tiles, reverse, extra=None):
    shifts = (1, 2, 4)
    carry_row = 0 if reverse else SUBLANES - 1

    def cols(piece, im):
        lo = im * HALF_ST + piece * SCAN_W
        return slice(lo, lo + SCAN_W)

    def step(t, state):
        carries, accs = state
        k = (n_tiles - 1 - t) if reverse else t
        rows = pl.ds(pl.multiple_of(k * SUBLANES, SUBLANES), SUBLANES)
        new_carries, new_accs = [], []
        for piece in range(SCAN_PIECES):
            cr, ci = carries[piece]
            xr0 = src_ref[rows, cols(piece, 0)]
            xi0 = src_ref[rows, cols(piece, 1)]
            xr, xi = xr0, xi0
            for si, s in enumerate(shifts):
                ar = c_ref[half, si, :, cols(piece, 0)]
                ai = c_ref[half, si, :, cols(piece, 1)]
                sh = (SUBLANES - s) if reverse else s
                sr = pltpu.roll(xr, sh, 0)
                sm = pltpu.roll(xi, sh, 0)
                xr, xi = xr + (ar * sr - ai * sm), xi + (ar * sm + ai * sr)
            pr = c_ref[half, 3, :, cols(piece, 0)]
            pi = c_ref[half, 3, :, cols(piece, 1)]
            hr = xr + (pr * cr - pi * ci)
            hi = xi + (pr * ci + pi * cr)
            dst_ref[rows, pl.ds(dst_off + piece * SCAN_W, SCAN_W)] = hr
            dst_ref[rows, pl.ds(dst_off + HALF_ST + piece * SCAN_W, SCAN_W)] = hi
            new_carries.append((jnp.broadcast_to(hr[carry_row:carry_row + 1, :], (SUBLANES, SCAN_W)),
                                jnp.broadcast_to(hi[carry_row:carry_row + 1, :], (SUBLANES, SCAN_W))))
            if extra is not None:
                new_accs.append(extra(rows, piece, (xr0, xi0), (hr, hi), accs[piece]))
        return tuple(new_carries), tuple(new_accs)

    base = half * 2 * HALF_ST
    carries0 = tuple((carry_ref[:, pl.ds(base + p * SCAN_W, SCAN_W)],
                      carry_ref[:, pl.ds(base + HALF_ST + p * SCAN_W, SCAN_W)]) for p in range(SCAN_PIECES))
    zero = jnp.zeros((SUBLANES, SCAN_W), F32)
    accs0 = tuple((zero, zero) for _ in range(SCAN_PIECES)) if extra is not None else ()
    carries, accs = lax.fori_loop(0, n_tiles, step, (carries0, accs0))
    for p in range(SCAN_PIECES):
        carry_ref[:, pl.ds(base + p * SCAN_W, SCAN_W)] = carries[p][0]
        carry_ref[:, pl.ds(base + HALF_ST + p * SCAN_W, SCAN_W)] = carries[p][1]
    return accs


def _ssm_tail(hb, u, c_ref, glu_ref, glub_ref, dskip_ref):
    ypre = u * dskip_ref[...]
    parts = []
    for half in range(2):
        parts.append(_dot(hb[half], c_ref[half]))
    ypre = ypre + jnp.concatenate(parts, axis=1)
    yg, th = _gelu(ypre)
    zz = _dot(yg.astype(BF16), glu_ref[...]) + glub_ref[...]
    z1, z2 = zz[:, :SSM_WIDTH], zz[:, SSM_WIDTH:]
    sg = jax.nn.sigmoid(z2)
    return ypre, th, yg, z1, sg


def _ssm_fwd(z, bblk, cblk, glu, glub, dskip, fwdc, batch, t_chunk, name):
    n = z.shape[0]
    nk = n // batch // t_chunk
    n_tiles = t_chunk // SUBLANES

    def body(u_ref, b_ref, c_ref, glu_ref, glub_ref, dskip_ref, k_ref, y_ref, h_ref, bu_ref, carry_ref):
        @pl.when(pl.program_id(1) == 0)
        def _():
            carry_ref[...] = jnp.zeros_like(carry_ref)

        u = u_ref[...]
        ub = u.astype(BF16)
        for half in range(2):
            bu_ref[...] = _dot(ub[:, half * HALF_IN:(half + 1) * HALF_IN], b_ref[half])
            _scan_tiles(bu_ref, h_ref, half * 2 * HALF_ST, k_ref, half, carry_ref, n_tiles, False)
        hb = [h_ref[:, half * 2 * HALF_ST:(half + 1) * 2 * HALF_ST].astype(BF16) for half in range(2)]
        _, _, _, z1, sg = _ssm_tail(hb, u, c_ref, glu_ref, glub_ref, dskip_ref)
        y_ref[...] = z1 * sg

    fixed2 = lambda b, k: (0, 0)
    fixed3 = lambda b, k: (0, 0, 0)
    row = lambda b, k: (b * nk + k, 0)
    return pl.pallas_call(
        body, name=name, grid=(batch, nk),
        in_specs=[
            pl.BlockSpec((t_chunk, SSM_WIDTH), row),
            pl.BlockSpec((2, HALF_IN, 2 * HALF_ST), fixed3),
            pl.BlockSpec((2, 2 * HALF_ST, HALF_IN), fixed3),
            pl.BlockSpec((SSM_WIDTH, 2 * SSM_WIDTH), fixed2),
            pl.BlockSpec((1, 2 * SSM_WIDTH), fixed2),
            pl.BlockSpec((1, SSM_WIDTH), fixed2),
            pl.BlockSpec((2, 4, SUBLANES, 2 * HALF_ST), lambda b, k: (0, 0, 0, 0)),
        ],
        out_specs=[pl.BlockSpec((t_chunk, SSM_WIDTH), row), pl.BlockSpec((t_chunk, 4 * HALF_ST), row)],
        out_shape=[jax.ShapeDtypeStruct((n, SSM_WIDTH), F32), jax.ShapeDtypeStruct((n, 4 * HALF_ST), F32)],
        scratch_shapes=[pltpu.VMEM((t_chunk, 2 * HALF_ST), F32), pltpu.VMEM((SUBLANES, 4 * HALF_ST), F32)],
        compiler_params=_cp("parallel", "arbitrary"),
    )(z, bblk, cblk, glu, glub, dskip, fwdc)


def _ssm_bwd(z, h, dy, bblk, cblk, glu, glub, dskip, revc, batch, t_chunk, name):
    n = z.shape[0]
    nk = n // batch // t_chunk
    n_tiles = t_chunk // SUBLANES

    def body(u_ref, h_ref, dy_ref, b_ref, c_ref, glu_ref, glub_ref, dskip_ref, k_ref,
             du_ref, dglu_ref, dglub_ref, ddskip_ref, dct_ref, db_ref, q_ref, g_ref, carry_ref):
        first = jnp.logical_and(pl.program_id(0) == 0, pl.program_id(1) == 0)

        @pl.when(first)
        def _():
            for r in (dglu_ref, dglub_ref, ddskip_ref, dct_ref, db_ref, q_ref):
                r[...] = jnp.zeros_like(r)

        @pl.when(pl.program_id(1) == 0)
        def _():
            carry_ref[...] = jnp.zeros_like(carry_ref)

        u = u_ref[...]
        ub = u.astype(BF16)
        hb = [h_ref[:, half * 2 * HALF_ST:(half + 1) * 2 * HALF_ST].astype(BF16) for half in range(2)]
        ypre, th, yg, z1, sg = _ssm_tail(hb, u, c_ref, glu_ref, glub_ref, dskip_ref)
        dout = dy_ref[...]
        dz = jnp.concatenate([dout * sg, dout * z1 * sg * (1.0 - sg)], axis=1)
        dzb = dz.astype(BF16)
        dglu_ref[...] += _dot_tn(yg.astype(BF16), dzb)
        dglub_ref[...] += _rows8(dz)
        dypre = _dot_nt(dzb, glu_ref[...]) * _gelu_grad(ypre, th)
        ddskip_ref[...] += _rows8(dypre * u)
        dypb = dypre.astype(BF16)
        du_parts = []
        for half in range(2):
            dyp_h = dypb[:, half * HALF_IN:(half + 1) * HALF_IN]
            dct_ref[half] += _dot_tn(dyp_h, hb[half])
            g_ref[...] = _dot_nt(dyp_h, c_ref[half])

            def extra(rows, piece, x_in, g_out, acc, half=half):
                er, ei = g_out[0] - x_in[0], g_out[1] - x_in[1]
                base = half * 2 * HALF_ST + piece * SCAN_W
                hr = h_ref[rows, pl.ds(base, SCAN_W)]
                hi = h_ref[rows, pl.ds(base + HALF_ST, SCAN_W)]
                return acc[0] + (er * hr + ei * hi), acc[1] + (er * hi - ei * hr)

            accs = _scan_tiles(g_ref, g_ref, 0, k_ref, half, carry_ref, n_tiles, True, extra)
            for piece in range(SCAN_PIECES):
                base = half * 2 * HALF_ST + piece * SCAN_W
                q_ref[:, pl.ds(base, SCAN_W)] += accs[piece][0]
                q_ref[:, pl.ds(base + HALF_ST, SCAN_W)] += accs[piece][1]
            gb = g_ref[...].astype(BF16)
            db_ref[half] += _dot_tn(ub[:, half * HALF_IN:(half + 1) * HALF_IN], gb)
            du_parts.append(_dot_nt(gb, b_ref[half]))
        du_ref[...] = dypre * dskip_ref[...] + jnp.concatenate(du_parts, axis=1)

    fixed2 = lambda b, k: (0, 0)
    fixed3 = lambda b, k: (0, 0, 0)
    row = lambda b, k: (b * nk + (nk - 1 - k), 0)
    return pl.pallas_call(
        body, name=name, grid=(batch, nk),
        in_specs=[
            pl.BlockSpec((t_chunk, SSM_WIDTH), row),
            pl.BlockSpec((t_chunk, 4 * HALF_ST), row),
            pl.BlockSpec((t_chunk, SSM_WIDTH), row),
            pl.BlockSpec((2, HALF_IN, 2 * HALF_ST), fixed3),
            pl.BlockSpec((2, 2 * HALF_ST, HALF_IN), fixed3),
            pl.BlockSpec((SSM_WIDTH, 2 * SSM_WIDTH), fixed2),
            pl.BlockSpec((1, 2 * SSM_WIDTH), fixed2),
            pl.BlockSpec((1, SSM_WIDTH), fixed2),
            pl.BlockSpec((2, 4, SUBLANES, 2 * HALF_ST), lambda b, k: (0, 0, 0, 0)),
        ],
        out_specs=[
            pl.BlockSpec((t_chunk, SSM_WIDTH), row),
            pl.BlockSpec((SSM_WIDTH, 2 * SSM_WIDTH), fixed2),
            pl.BlockSpec((SUBLANES, 2 * SSM_WIDTH), fixed2),
            pl.BlockSpec((SUBLANES, SSM_WIDTH), fixed2),
            pl.BlockSpec((2, HALF_IN, 2 * HALF_ST), fixed3),
            pl.BlockSpec((2, HALF_IN, 2 * HALF_ST), fixed3),
            pl.BlockSpec((SUBLANES, 4 * HALF_ST), fixed2),
        ],
        out_shape=[
            jax.ShapeDtypeStruct((n, SSM_WIDTH), F32),
            jax.ShapeDtypeStruct((SSM_WIDTH, 2 * SSM_WIDTH), F32),
            jax.ShapeDtypeStruct((SUBLANES, 2 * SSM_WIDTH), F32),
            jax.ShapeDtypeStruct((SUBLANES, SSM_WIDTH), F32),
            jax.ShapeDtypeStruct((2, HALF_IN, 2 * HALF_ST), F32),
            jax.ShapeDtypeStruct((2, HALF_IN, 2 * HALF_ST), F32),
            jax.ShapeDtypeStruct((SUBLANES, 4 * HALF_ST), F32),
        ],
        scratch_shapes=[pltpu.VMEM((t_chunk, 2 * HALF_ST), F32), pltpu.VMEM((SUBLANES, 4 * HALF_ST), F32)],
        compiler_params=_cp("arbitrary", "arbitrary"),
    )(z, h, dy, bblk, cblk, glu, glub, dskip, revc)


def _gm_chunk_fwd(u, v, gain_ref, w_ref, bias_ref):
    ug, thu = _gelu(u)
    vg, thv = _gelu(v)
    rs, vns, ss = [], [], []
    for hh in range(GM_HEADS):
        cs = slice(hh * GM_HEAD_DIM, (hh + 1) * GM_HEAD_DIM)
        vn, _, r = _rms_fwd(vg[:, cs], gain_ref[:, cs])
        s = _dot(w_ref[hh], vn.astype(BF16)) + bias_ref[:, cs]
        rs.append(r)
        vns.append(vn)
        ss.append(s)
    return ug, thu, thv, vg, rs, vns, ss


def _gm_fwd(z, gain, w_tril, bias, rows, name):
    n = z.shape[0]
    chunks = rows // GM_CHUNK

    def body(u_ref, v_ref, gain_ref, w_ref, bias_ref, y_ref):
        for c in range(chunks):
            rs_ = slice(c * GM_CHUNK, (c + 1) * GM_CHUNK)
            ug, _, _, _, _, _, ss = _gm_chunk_fwd(u_ref[rs_, :], v_ref[rs_, :], gain_ref, w_ref, bias_ref)
            y_ref[rs_, :] = ug * jnp.concatenate(ss, axis=1)

    return pl.pallas_call(
        body, name=name, grid=(n // rows,),
        in_specs=[
            pl.BlockSpec((rows, GM_WIDTH), lambda i: (i, 1)),
            pl.BlockSpec((rows, GM_WIDTH), lambda i: (i, 2)),
            pl.BlockSpec((1, GM_WIDTH), lambda i: (0, 0)),
            pl.BlockSpec((GM_HEADS, GM_CHUNK, GM_CHUNK), lambda i: (0, 0, 0)),
            pl.BlockSpec((GM_CHUNK, GM_WIDTH), lambda i: (0, 0)),
        ],
        out_specs=pl.BlockSpec((rows, GM_WIDTH), lambda i: (i, 0)),
        out_shape=jax.ShapeDtypeStruct((n, GM_WIDTH), F32),
        compiler_params=_cp("parallel"),
    )(z, z, gain, w_tril, bias)


def _gm_bwd(z, dy, gain, w_tril, bias, rows, name):
    n = z.shape[0]
    chunks = rows // GM_CHUNK

    def body(u_ref, v_ref, dy_ref, gain_ref, w_ref, bias_ref, du_ref, dv_ref, dw_ref, dbias_ref, dgain_ref):
        @pl.when(pl.program_id(0) == 0)
        def _():
            dw_ref[...] = jnp.zeros_like(dw_ref)
            dbias_ref[...] = jnp.zeros_like(dbias_ref)
            dgain_ref[...] = jnp.zeros_like(dgain_ref)

        for c in range(chunks):
            rs_ = slice(c * GM_CHUNK, (c + 1) * GM_CHUNK)
            u, v = u_ref[rs_, :], v_ref[rs_, :]
            ug, thu, thv, vg, rs, vns, ss = _gm_chunk_fwd(u, v, gain_ref, w_ref, bias_ref)
            dout = dy_ref[rs_, :]
            ds = dout * ug
            du_ref[rs_, :] = dout * jnp.concatenate(ss, axis=1) * _gelu_grad(u, thu)
            dbias_ref[...] += ds
            dvg_parts, dgain_parts = [], []
            for hh in range(GM_HEADS):
                cs = slice(hh * GM_HEAD_DIM, (hh + 1) * GM_HEAD_DIM)
                dsb = ds[:, cs].astype(BF16)
                dvn = _dot_tn(w_ref[hh], dsb)
                dw_ref[hh] += _dot_nt(dsb, vns[hh].astype(BF16))
                g = gain_ref[:, cs]
                xh = vg[:, cs] * rs[hh]
                dvg, dgr = _rms_bwd(dvn, xh, rs[hh], g)
                dvg_parts.append(dvg)
                dgain_parts.append(dgr)
            dv_ref[rs_, :] = jnp.concatenate(dvg_parts, axis=1) * _gelu_grad(v, thv)
            dgain_ref[...] += _rows8(jnp.concatenate(dgain_parts, axis=1))

    row = lambda i: (i, 0)
    return pl.pallas_call(
        body, name=name, grid=(n // rows,),
        in_specs=[
            pl.BlockSpec((rows, GM_WIDTH), lambda i: (i, 1)),
            pl.BlockSpec((rows, GM_WIDTH), lambda i: (i, 2)),
            pl.BlockSpec((rows, GM_WIDTH), row),
            pl.BlockSpec((1, GM_WIDTH), lambda i: (0, 0)),
            pl.BlockSpec((GM_HEADS, GM_CHUNK, GM_CHUNK), lambda i: (0, 0, 0)),
            pl.BlockSpec((GM_CHUNK, GM_WIDTH), lambda i: (0, 0)),
        ],
        out_specs=[
            pl.BlockSpec((rows, GM_WIDTH), row), pl.BlockSpec((rows, GM_WIDTH), row),
            pl.BlockSpec((GM_HEADS, GM_CHUNK, GM_CHUNK), lambda i: (0, 0, 0)),
            pl.BlockSpec((GM_CHUNK, GM_WIDTH), lambda i: (0, 0)),
            pl.BlockSpec((SUBLANES, GM_WIDTH), lambda i: (0, 0)),
        ],
        out_shape=[
            jax.ShapeDtypeStruct((n, GM_WIDTH), F32), jax.ShapeDtypeStruct((n, GM_WIDTH), F32),
            jax.ShapeDtypeStruct((GM_HEADS, GM_CHUNK, GM_CHUNK), F32),
            jax.ShapeDtypeStruct((GM_CHUNK, GM_WIDTH), F32),
            jax.ShapeDtypeStruct((SUBLANES, GM_WIDTH), F32),
        ],
        compiler_params=_cp("arbitrary"),
    )(z, z, dy, gain, w_tril, bias)


def _loss_head(x, gain, target, tm, name):
    n = x.shape[0]

    def body(x_ref, g_ref, t_ref, dx_ref, sq_ref, dgain_ref):
        @pl.when(pl.program_id(0) == 0)
        def _():
            sq_ref[...] = jnp.zeros_like(sq_ref)
            dgain_ref[...] = jnp.zeros_like(dgain_ref)

        g = g_ref[...]
        y, xh, r = _rms_fwd(x_ref[...], g)
        err = y - t_ref[...]
        sq_ref[...] += _rows8(err * err)
        dx, dgr = _rms_bwd(err * (1.0 / D_MODEL), xh, r, g)
        dx_ref[...] = dx
        dgain_ref[...] += _rows8(dgr)

    row = lambda i: (i, 0)
    fixed = lambda i: (0, 0)
    return pl.pallas_call(
        body, name=name, grid=(n // tm,),
        in_specs=[pl.BlockSpec((tm, D_MODEL), row), pl.BlockSpec((1, D_MODEL), fixed), pl.BlockSpec((tm, D_MODEL), row)],
        out_specs=[pl.BlockSpec((tm, D_MODEL), row), pl.BlockSpec((SUBLANES, D_MODEL), fixed),
                   pl.BlockSpec((SUBLANES, D_MODEL), fixed)],
        out_shape=[jax.ShapeDtypeStruct((n, D_MODEL), F32), jax.ShapeDtypeStruct((SUBLANES, D_MODEL), F32),
                   jax.ShapeDtypeStruct((SUBLANES, D_MODEL), F32)],
        compiler_params=_cp("arbitrary"),
    )(x, gain, target)


def _adam_math(w, g, m, v):
    m2 = ADAM_B1 * m + (1.0 - ADAM_B1) * g
    v2 = ADAM_B2 * v + (1.0 - ADAM_B2) * (g * g)
    m_hat = m2 / (1.0 - ADAM_B1 ** ADAM_STEP)
    v_hat = v2 / (1.0 - ADAM_B2 ** ADAM_STEP)
    delta = -ADAM_LR * (m_hat / (jnp.sqrt(v_hat) + ADAM_EPS) + ADAM_WD * w)
    return delta, m2, v2


def _adam_sharded(parts, w, m, v, layer, earlier, name):
    depth, r, c = w.shape
    tr = max(t for t in range(16, 129, 16) if r % t == 0)

    def body(p_ref, w_ref, m_ref, v_ref, *rest):
        g_ref, d_ref, m2_ref, v2_ref = rest[-4:]
        g = p_ref[0].astype(F32)
        for s in range(1, N_DEV):
            g = g + p_ref[s].astype(F32)
        delta, m2, v2 = _adam_math(w_ref[...], g, m_ref[...], v_ref[...])
        g_ref[...] = g
        d_ref[...] = delta
        m2_ref[...] = m2
        v2_ref[...] = v2

    blk = pl.BlockSpec((None, tr, c), lambda i: (layer, i, 0))
    extra = [] if earlier is None else list(earlier)
    return pl.pallas_call(
        body, name=name, grid=(r // tr,),
        in_specs=[pl.BlockSpec((N_DEV, tr, c), lambda i: (0, i, 0)), blk, blk, blk]
        + [pl.BlockSpec(memory_space=pl.ANY)] * len(extra),
        out_specs=[blk, blk, blk, blk],
        out_shape=[jax.ShapeDtypeStruct((depth, r, c), F32)] * 4,
        input_output_aliases={4 + i: i for i in range(len(extra))},
        compiler_params=_cp("parallel"),
    )(parts, w, m, v, *extra)


def _adam_packed(g, w, m, v, name):
    r, c = g.shape

    def body(g_ref, w_ref, m_ref, v_ref, d_ref, m2_ref, v2_ref):
        delta, m2, v2 = _adam_math(w_ref[...], g_ref[...], m_ref[...], v_ref[...])
        d_ref[...] = delta
        m2_ref[...] = m2
        v2_ref[...] = v2

    blk = pl.BlockSpec((r, c), lambda i: (0, 0))
    return pl.pallas_call(
        body, name=name, grid=(1,),
        in_specs=[blk, blk, blk, blk], out_specs=[blk, blk, blk],
        out_shape=[jax.ShapeDtypeStruct((r, c), F32)] * 3,
        compiler_params=_cp("arbitrary"),
    )(g, w, m, v)


def _my_place():
    return lax.axis_index("x"), lax.axis_index("y"), lax.axis_index("c")


def _flip(place, rel):
    x, y, c = place
    return (1 - x if rel & 4 else x, 1 - y if rel & 2 else y, 1 - c if rel & 1 else c)


def _index(place):
    return 4 * place[0] + 2 * place[1] + place[2]


def _all_gather(shards, name):
    na = len(shards)

    def body(*refs):
        xs, outs = refs[:na], refs[na:2 * na]
        send_sems, recv_sems, local_sems = refs[2 * na:]
        me = _my_place()
        sibling = _flip(me, 1)
        chips = [_flip(me, 4), _flip(me, 2), _flip(me, 6)]

        def copy(a, k, block, to, src=None):
            slot = outs[a].at[_index(block)]
            return pltpu.make_async_remote_copy(
                src_ref=slot if src is None else src, dst_ref=slot,
                send_sem=send_sems.at[a, k], recv_sem=recv_sems.at[a, k],
                device_id=to, device_id_type=MESH)

        mine = [pltpu.make_async_copy(xs[a], outs[a].at[_index(me)], local_sems.at[a]) for a in range(na)]
        for cp in mine:
            cp.start()
        first = []
        for a in range(na):
            first.append(copy(a, 0, me, sibling, src=xs[a]))
            first += [copy(a, 1 + j, me, chip, src=xs[a]) for j, chip in enumerate(chips)]
        for cp in first:
            cp.start()
        passed = []
        for a in range(na):
            for j, chip in enumerate(chips):
                copy(a, 1 + j, chip, me).wait_recv()
                fwd = copy(a, 4 + j, chip, sibling)
                fwd.start()
                passed.append(fwd)
        for a in range(na):
            copy(a, 0, sibling, me).wait_recv()
            for j, chip in enumerate(chips):
                copy(a, 4 + j, _flip(chip, 1), me).wait_recv()
        for cp in first + passed:
            cp.wait_send()
        for cp in mine:
            cp.wait()

    hbm = pl.BlockSpec(memory_space=pl.ANY)
    return pl.pallas_call(
        body, name=name,
        in_specs=[hbm] * na, out_specs=[hbm] * na,
        out_shape=[jax.ShapeDtypeStruct((N_DEV,) + s.shape, s.dtype) for s in shards],
        scratch_shapes=[pltpu.SemaphoreType.DMA((na, 7)), pltpu.SemaphoreType.DMA((na, 7)),
                        pltpu.SemaphoreType.DMA((na,))],
    )(*shards)


_HBM = pl.BlockSpec(memory_space=pltpu.HBM)
_SEM = pl.BlockSpec(memory_space=pltpu.SEMAPHORE)
_EFFECT = pltpu.SideEffectType.DATAFLOW_SIDE_EFFECTING


def _exchange_copy(src_ref, land_ref, send_sems, recv_sems, a, rel, me, scatter, landed):
    peer = _flip(me, rel)
    src = src_ref.at[_index(peer)] if scatter else src_ref
    return pltpu.make_async_remote_copy(
        src_ref=src, dst_ref=land_ref.at[_index(peer if landed else me)],
        send_sem=send_sems.at[a * (N_DEV - 1) + rel - 1], recv_sem=recv_sems.at[a * (N_DEV - 1) + rel - 1],
        device_id=peer, device_id_type=MESH)


def _own_slot(data, me, scatter):
    if scatter:
        own = lax.dynamic_slice_in_dim(data, me, 1, axis=0)
        shape = data.shape
    else:
        own = data[None]
        shape = (N_DEV,) + data.shape
    start = (me,) + (0,) * (len(shape) - 1)
    return lax.dynamic_update_slice(lax.empty(shape, data.dtype), own, start)


def _exchange_start(groups, me, scatter, name):
    sizes = [len(g) for g in groups]
    srcs = [a for g in groups for a in g]
    lands = [_own_slot(a, me, scatter) for a in srcs]
    na, ng = len(srcs), len(groups)

    def body(*refs):
        src_refs, land_refs = refs[:na], refs[na:2 * na]
        sems = refs[2 * na:2 * na + 2 * ng]
        token = refs[-1]
        place = _my_place()
        a = 0
        for g, size in enumerate(sizes):
            for k in range(size):
                for rel in range(1, N_DEV):
                    _exchange_copy(src_refs[a], land_refs[a], sems[2 * g], sems[2 * g + 1], k, rel, place, scatter,
                                   False).start()
                a += 1
        token[...] = jnp.zeros_like(token)

    sem_shapes = [pltpu.SemaphoreType.DMA((size * (N_DEV - 1),)) for size in sizes for _ in range(2)]
    outs = pl.pallas_call(
        body, name=name,
        in_specs=[_HBM] * (2 * na),
        out_specs=[_SEM] * (2 * ng) + [_HBM] * (2 * na) + [pl.BlockSpec(memory_space=pltpu.VMEM)],
        out_shape=sem_shapes + [pltpu.HBM(a.shape, a.dtype) for a in srcs + lands]
        + [jax.ShapeDtypeStruct((SUBLANES, LANES), F32)],
        input_output_aliases={i: 2 * ng + i for i in range(2 * na)},
        compiler_params=pltpu.CompilerParams(has_side_effects=_EFFECT),
    )(*[pltpu.with_memory_space_constraint(a, pltpu.HBM) for a in srcs + lands])
    sems, thru, token = outs[:2 * ng], outs[2 * ng:2 * ng + 2 * na], outs[-1]
    handles, a = [], 0
    for g, size in enumerate(sizes):
        handles.append((sems[2 * g], sems[2 * g + 1], thru[a:a + size], thru[na + a:na + a + size]))
        a += size
    return handles, token


def _exchange_wait(handle, after, scatter, name):
    send_sems, recv_sems, srcs, lands = handle
    na = len(srcs)

    def body(*refs):
        src_refs, land_refs = refs[:na], refs[na:2 * na]
        send_ref, recv_ref = refs[2 * na], refs[2 * na + 1]
        place = _my_place()
        for a in range(na):
            for rel in range(1, N_DEV):
                cp = _exchange_copy(src_refs[a], land_refs[a], send_ref, recv_ref, a, rel, place, scatter, True)
                cp.wait_send()
                cp.wait_recv()

    outs = pl.pallas_call(
        body, name=name,
        in_specs=[_HBM] * (2 * na) + [_SEM, _SEM, pl.BlockSpec(memory_space=pl.ANY)],
        out_specs=[_HBM] * (2 * na),
        out_shape=[pltpu.HBM(a.shape, a.dtype) for a in list(srcs) + list(lands)],
        input_output_aliases={i: i for i in range(2 * na)},
        compiler_params=pltpu.CompilerParams(has_side_effects=_EFFECT),
    )(*srcs, *lands, send_sems, recv_sems, after)
    return outs[na:]


def _behind(arr, token):
    return arr + token[0:1, 0:1]


def _all_reduce_small(g, name):
    _, r, c = g.shape

    def body(g_ref, o_ref, land_ref, red_ref, send1, recv1, send2, recv2):
        me = _my_place()
        idx = _index(me)

        def scatter(rel):
            peer = _flip(me, rel)
            return pltpu.make_async_remote_copy(
                src_ref=g_ref.at[_index(peer)], dst_ref=land_ref.at[idx],
                send_sem=send1.at[rel - 1], recv_sem=recv1.at[rel - 1], device_id=peer, device_id_type=MESH)

        def gather(rel):
            peer = _flip(me, rel)
            return pltpu.make_async_remote_copy(
                src_ref=red_ref, dst_ref=o_ref.at[idx],
                send_sem=send2.at[rel - 1], recv_sem=recv2.at[rel - 1], device_id=peer, device_id_type=MESH)

        for rel in range(1, N_DEV):
            scatter(rel).start()
        land_ref[idx] = g_ref[idx]
        for rel in range(1, N_DEV):
            scatter(rel).wait()
        acc = land_ref[0]
        for s in range(1, N_DEV):
            acc = acc + land_ref[s]
        red_ref[...] = acc
        for rel in range(1, N_DEV):
            gather(rel).start()
        o_ref[idx] = acc
        for rel in range(1, N_DEV):
            gather(rel).wait()

    vmem = pl.BlockSpec(memory_space=pltpu.VMEM)
    return pl.pallas_call(
        body, name=name,
        in_specs=[vmem], out_specs=vmem,
        out_shape=jax.ShapeDtypeStruct(g.shape, F32),
        scratch_shapes=[pltpu.VMEM(g.shape, F32), pltpu.VMEM((r, c), F32)]
        + [pltpu.SemaphoreType.DMA((N_DEV - 1,))] * 4,
        compiler_params=pltpu.CompilerParams(vmem_limit_bytes=VMEM_LIMIT),
    )(g)


def _ssm_discretize(a_re, a_im, log_dt, b_re, b_im):
    dt = jnp.exp(log_dt)[:, None]
    mag = jnp.exp(a_re * dt)
    lr, li = mag * jnp.cos(a_im * dt), mag * jnp.sin(a_im * dt)
    den = a_re * a_re + a_im * a_im
    qr = ((lr - 1.0) * a_re + li * a_im) / den
    qi = (li * a_re - (lr - 1.0) * a_im) / den
    bbr = qr[..., None] * b_re - qi[..., None] * b_im
    bbi = qr[..., None] * b_im + qi[..., None] * b_re
    return lr, li, bbr, bbi


def _halves(a):
    return a.reshape((2, HALF_GROUPS) + a.shape[1:])


def _block_diag(blocks):
    g, r, c = blocks.shape
    eye = jnp.eye(g, dtype=blocks.dtype)
    return jnp.einsum("grc,gh->grhc", blocks, eye).reshape(g * r, g * c)


def _block_diag_take(dense, g, r, c):
    return jnp.einsum("grhc,gh->grc", dense.reshape(g, r, g, c), jnp.eye(g, dtype=dense.dtype))


def _ssm_matrices(bbr, bbi, c_re, c_im, glu_w, glu_b, d_skip):
    bre, bim = _halves(jnp.swapaxes(bbr, 1, 2)), _halves(jnp.swapaxes(bbi, 1, 2))
    bblk = jnp.stack([jnp.concatenate([_block_diag(bre[h]), _block_diag(bim[h])], axis=1) for h in range(2)])
    cre, cim = _halves(jnp.swapaxes(c_re, 1, 2)), _halves(jnp.swapaxes(c_im, 1, 2))
    cblk = jnp.stack([jnp.concatenate([_block_diag(cre[h]), -_block_diag(cim[h])], axis=0) for h in range(2)])
    glu = jnp.concatenate([_block_diag(glu_w[:, :, :SSM_CH]), _block_diag(glu_w[:, :, SSM_CH:])], axis=1)
    glub = jnp.concatenate([glu_b[:, :SSM_CH].reshape(1, -1), glu_b[:, SSM_CH:].reshape(1, -1)], axis=1)
    return bblk.astype(BF16), cblk.astype(BF16), glu.astype(BF16), glub, d_skip.reshape(1, -1)


def _scan_constants(lr, li, reverse):
    if reverse:
        li = -li
    pows = [(lr, li)]
    for _ in range(SUBLANES - 1):
        pr, pi = pows[-1]
        pows.append((pr * lr - pi * li, pr * li + pi * lr))
    row = jnp.arange(SUBLANES)[:, None]

    def flat(a):
        return a.reshape(2, 1, HALF_ST)

    mats = []
    for s in (1, 2, 4):
        keep = (row + s <= SUBLANES - 1) if reverse else (row >= s)
        mats.append(tuple(jnp.where(keep[None], flat(p), 0.0) for p in pows[s - 1]))
    order = [SUBLANES - 1 - j for j in range(SUBLANES)] if reverse else list(range(SUBLANES))
    mats.append(tuple(jnp.concatenate([flat(pows[j][k]) for j in order], axis=1) for k in range(2)))
    return jnp.stack([jnp.concatenate([m[0], m[1]], axis=2) for m in mats], axis=1)


def _pack(arrs, rows):
    flat = jnp.concatenate([a.reshape(-1) for a in arrs])
    return jnp.pad(flat, (0, rows * LANES - flat.shape[0])).reshape(rows, LANES)


def _unpack(buf, like):
    flat = buf.reshape(-1)
    out, off = [], 0
    for a in like:
        out.append(flat[off:off + a.size].reshape(a.shape))
        off += a.size
    return out


SMALL = ("norm_ffn1", "norm_mix", "ssm_a_re", "ssm_a_im", "ssm_log_dt", "ssm_b_re", "ssm_b_im", "ssm_c_re",
         "ssm_c_im", "ssm_d", "ssm_glu_w", "ssm_glu_b", "gm_v_gain", "gm_w_s", "gm_b_s", "gain_ssm_out",
         "gain_gm_out", "norm_ffn2", "norm_final")
BIG = ("ffn1_w_in", "ffn1_w_out", "mix_w_in", "mix_w_out", "ffn2_w_in", "ffn2_w_out")
WEIGHTS = ("norm_ffn1", "ffn1_w_in", "ffn1_w_out", "norm_mix", "mix_w_in", "ssm_a_re", "ssm_a_im", "ssm_log_dt",
           "ssm_b_re", "ssm_b_im", "ssm_c_re", "ssm_c_im", "ssm_d", "ssm_glu_w", "ssm_glu_b", "gm_v_gain", "gm_w_s",
           "gm_b_s", "gain_ssm_out", "gain_gm_out", "mix_w_out", "norm_ffn2", "ffn2_w_in", "ffn2_w_out", "norm_final")


def _step(x, target, w, m, v):
    batch, seq, _ = x.shape
    n = batch * seq
    depth = w["norm_ffn1"].shape[0]
    tm = min(512, n)
    tk = min(2048, n)
    t_chunk = min(256, seq)
    gm_rows = min(512, seq)
    x = x.reshape(n, D_MODEL)
    target = target.reshape(n, D_MODEL)

    me = _index(_my_place())
    shard = lambda group, l: [w[f"{group}_w_in"][l].astype(BF16), w[f"{group}_w_out"][l].astype(BF16)]
    gathered = {("ffn1", 0): _all_gather(shard("ffn1", 0), "all_gather_first")}
    later = [(g, l) for l in range(depth) for g in ("ffn1", "mix", "ffn2")][1:]
    handles, token = _exchange_start([shard(g, l) for g, l in later], me, False, "all_gather_start")
    pending = dict(zip(later, handles))

    def weights(group, l, after=None):
        if (group, l) not in gathered:
            w_in, w_out = _exchange_wait(pending[(group, l)], after, False, f"all_gather_wait_{group}_{l}")
            if group == "mix":
                w_in = jnp.transpose(w_in, (1, 0, 2)).reshape(D_MODEL, IN_COLS)
                w_out = w_out.reshape(D_MODEL, D_MODEL)
            gathered[(group, l)] = (w_in, w_out)
        return gathered[(group, l)]

    tril = jnp.tril(jnp.ones((GM_CHUNK, GM_CHUNK), bool))
    layers = []
    for l in range(depth):
        disc, disc_vjp = jax.vjp(_ssm_discretize, w["ssm_a_re"][l], w["ssm_a_im"][l], w["ssm_log_dt"][l],
                                 w["ssm_b_re"][l], w["ssm_b_im"][l])
        lr, li, bbr, bbi = disc
        bblk, cblk, glu, glub, dskip = _ssm_matrices(bbr, bbi, w["ssm_c_re"][l], w["ssm_c_im"][l],
                                                     w["ssm_glu_w"][l], w["ssm_glu_b"][l], w["ssm_d"][l])
        layers.append(dict(
            disc_vjp=disc_vjp, lr=lr, li=li, bblk=bblk, cblk=cblk, glu=glu, glub=glub, dskip=dskip,
            fwdc=_scan_constants(lr, li, False), revc=_scan_constants(lr, li, True),
            w_tril=jnp.where(tril[None], w["gm_w_s"][l], 0.0).astype(BF16),
            gm_bias=jnp.repeat(w["gm_b_s"][l].T, GM_HEAD_DIM, axis=1),
            g1=w["norm_ffn1"][l][None], gmix=w["norm_mix"][l][None], g2=w["norm_ffn2"][l][None],
            gv=w["gm_v_gain"][l][None], gs=w["gain_ssm_out"][l][None], gg=w["gain_gm_out"][l][None],
        ))

    saved = []
    for l in range(depth):
        p = layers[l]
        x0 = x
        g1 = _behind(p["g1"], token) if l == 0 else p["g1"]
        x1 = _ffn_fwd(x0, g1, *weights("ffn1", l, x0), tm, f"ffn1_fwd_{l}")
        mwi, mwo = weights("mix", l, x1)
        z = _mix_in_fwd(x1, p["gmix"], mwi, tm, f"mix_in_fwd_{l}")
        y_ssm, h = _ssm_fwd(z, p["bblk"], p["cblk"], p["glu"], p["glub"], p["dskip"], p["fwdc"], batch, t_chunk,
                            f"ssm_fwd_{l}")
        y_gm = _gm_fwd(z, p["gv"], p["w_tril"], p["gm_bias"], gm_rows, f"gm_fwd_{l}")
        x2 = _mix_out_fwd(y_ssm, y_gm, p["gs"], p["gg"], mwo, x1, tm, f"mix_out_fwd_{l}")
        x = _ffn_fwd(x2, p["g2"], *weights("ffn2", l, x2), tm, f"ffn2_fwd_{l}")
        saved.append((x0, x1, x2, z, h, y_ssm, y_gm))

    dx, sq, dnf = _loss_head(x, w["norm_final"][None], target, tm, "loss_head")
    loss = lax.psum((0.5 / D_MODEL) * jnp.sum(sq), AXES)

    small = {k: [None] * depth for k in SMALL if k != "norm_final"}
    sent = []

    def send(group, l, parts):
        (handle,), tok = _exchange_start([parts], me, True, f"reduce_scatter_start_{group}_{l}")
        sent.append((group, l, handle))
        return tok

    token = None
    for l in reversed(range(depth)):
        p = layers[l]
        x0, x1, x2, z, h, y_ssm, y_gm = saved[l]
        mwi, mwo = weights("mix", l)
        dx_out = dx
        g2 = p["g2"] if token is None else _behind(p["g2"], token)
        dx, xn, dgu, act, dgain = _ffn_bwd(x2, g2, dx_out, *weights("ffn2", l), tm, f"ffn2_bwd_{l}")
        dw_in = _ffn_dw_in(xn, dgu, tk, f"ffn2_dw_in_{l}")
        dw_out = _ffn_dw_out(act, dx_out, tk, f"ffn2_dw_out_{l}").reshape(N_DEV, FF_SHARD // 2, D_MODEL)
        token = send("ffn2", l, [dw_in, dw_out])
        small["norm_ffn2"][l] = dgain.sum(0)

        dy_ssm, dy_gm, dwo, dgs, dgg = _mix_out_bwd(y_ssm, y_gm, _behind(p["gs"], token), p["gg"], mwo, dx, tm,
                                                    f"mix_out_bwd_{l}")
        dwo = dwo.astype(BF16).reshape(N_DEV, D_MODEL // N_DEV, D_MODEL)
        small["gain_ssm_out"][l] = dgs.sum(0)
        small["gain_gm_out"][l] = dgg.sum(0)

        du_ssm, dglu, dglub, ddskip, dct, db, q = _ssm_bwd(
            z, h, dy_ssm, p["bblk"], p["cblk"], p["glu"], p["glub"], p["dskip"], p["revc"], batch, t_chunk,
            f"ssm_bwd_{l}")
        du_gm, dv_gm, dws, dbias, dgv = _gm_bwd(z, dy_gm, p["gv"], p["w_tril"], p["gm_bias"], gm_rows, f"gm_bwd_{l}")

        q = q.sum(0).reshape(2, 2, HALF_GROUPS, SSM_STATE)
        qr, qi = q[:, 0].reshape(SSM_GROUPS, SSM_STATE), q[:, 1].reshape(SSM_GROUPS, SSM_STATE)
        den = p["lr"] * p["lr"] + p["li"] * p["li"]
        d_re = (qr * p["lr"] + qi * p["li"]) / den
        d_im = (qi * p["lr"] - qr * p["li"]) / den
        dbb = jnp.stack([_block_diag_take(db[hf, :, k * HALF_ST:(k + 1) * HALF_ST], HALF_GROUPS, SSM_CH, SSM_STATE)
                         for k in range(2) for hf in range(2)]).reshape(2, SSM_GROUPS, SSM_CH, SSM_STATE)
        dcc = jnp.stack([_block_diag_take(dct[hf, :, k * HALF_ST:(k + 1) * HALF_ST], HALF_GROUPS, SSM_CH, SSM_STATE)
                         for k in range(2) for hf in range(2)]).reshape(2, SSM_GROUPS, SSM_CH, SSM_STATE)
        da_re, da_im, dlog_dt, db_re, db_im = p["disc_vjp"](
            (d_re, -d_im, jnp.swapaxes(dbb[0], 1, 2), jnp.swapaxes(dbb[1], 1, 2)))
        small["ssm_a_re"][l], small["ssm_a_im"][l], small["ssm_log_dt"][l] = da_re, da_im, dlog_dt
        small["ssm_b_re"][l], small["ssm_b_im"][l] = db_re, db_im
        small["ssm_c_re"][l], small["ssm_c_im"][l] = dcc[0], -dcc[1]
        small["ssm_d"][l] = ddskip.sum(0).reshape(SSM_GROUPS, SSM_CH)
        small["ssm_glu_w"][l] = jnp.concatenate(
            [_block_diag_take(dglu[:, :SSM_WIDTH], SSM_GROUPS, SSM_CH, SSM_CH),
             _block_diag_take(dglu[:, SSM_WIDTH:], SSM_GROUPS, SSM_CH, SSM_CH)], axis=2)
        dglub = dglub.sum(0)
        small["ssm_glu_b"][l] = jnp.concatenate(
            [dglub[:SSM_WIDTH].reshape(SSM_GROUPS, SSM_CH), dglub[SSM_WIDTH:].reshape(SSM_GROUPS, SSM_CH)], axis=1)
        small["gm_v_gain"][l] = dgv.sum(0)
        small["gm_w_s"][l] = jnp.where(tril[None], dws, 0.0)
        small["gm_b_s"][l] = dbias.reshape(GM_CHUNK, GM_HEADS, GM_HEAD_DIM).sum(-1).T

        dx, dwi, dgain = _mix_in_bwd(x1, p["gmix"], du_ssm, du_gm, dv_gm, dx, mwi, tm, f"mix_in_bwd_{l}")
        dwi = jnp.transpose(dwi.astype(BF16).reshape(D_MODEL, N_DEV, IN_COLS // N_DEV), (1, 0, 2))
        token = send("mix", l, [dwi, dwo])
        small["norm_mix"][l] = dgain.sum(0)

        dx_out = dx
        dx, xn, dgu, act, dgain = _ffn_bwd(x0, _behind(p["g1"], token), dx_out, *weights("ffn1", l), tm,
                                           f"ffn1_bwd_{l}")
        dw_in = _ffn_dw_in(xn, dgu, tk, f"ffn1_dw_in_{l}")
        dw_out = _ffn_dw_out(act, dx_out, tk, f"ffn1_dw_out_{l}").reshape(N_DEV, FF_SHARD // 2, D_MODEL)
        token = send("ffn1", l, [dw_in, dw_out])
        small["norm_ffn1"][l] = dgain.sum(0)

    grad_x = dx.reshape(batch, seq, D_MODEL)
    grads, deltas, new_m, new_v = {}, {}, {}, {}

    small_g = [jnp.stack(small[k]) if k != "norm_final" else dnf.sum(0) for k in SMALL]
    total = sum(int(math.prod(w[k].shape)) for k in SMALL)
    rows = -(-total // (LANES * N_DEV * SUBLANES)) * N_DEV * SUBLANES
    packed = _behind(_pack(small_g, rows), token)
    g_all = _all_reduce_small(packed.reshape(N_DEV, rows // N_DEV, LANES), "all_reduce_small")
    g_all = g_all.reshape(rows, LANES)
    like = [w[k] for k in SMALL]
    d_p, m_p, v_p = _adam_packed(g_all, _pack(like, rows), _pack([m[k] for k in SMALL], rows),
                                 _pack([v[k] for k in SMALL], rows), "adam_small")
    for k, g_, d_, m_, v_ in zip(SMALL, _unpack(g_all, like), _unpack(d_p, like), _unpack(m_p, like),
                                 _unpack(v_p, like)):
        grads[k], deltas[k], new_m[k], new_v[k] = g_, d_, m_, v_

    results = {}
    after = d_p
    for group, l, handle in sent:
        landed = _exchange_wait(handle, after, True, f"reduce_scatter_wait_{group}_{l}")
        for k, parts in zip((f"{group}_w_in", f"{group}_w_out"), landed):
            results[k] = _adam_sharded(parts, w[k], m[k], v[k], l, results.get(k), f"adam_{k}_{l}")
            after = results[k][0]
    for k in BIG:
        grads[k], deltas[k], new_m[k], new_v[k] = results[k]
    return loss, grad_x, grads, deltas, new_m, new_v


def kernel(x, norm_ffn1, ffn1_w_in, ffn1_w_out, norm_mix, mix_w_in, ssm_a_re, ssm_a_im, ssm_log_dt, ssm_b_re, ssm_b_im, ssm_c_re, ssm_c_im, ssm_d, ssm_glu_w, ssm_glu_b, gm_v_gain, gm_w_s, gm_b_s, gain_ssm_out, gain_gm_out, mix_w_out, norm_ffn2, ffn2_w_in, ffn2_w_out, norm_final, loss_target, m_norm_ffn1, m_ffn1_w_in, m_ffn1_w_out, m_norm_mix, m_mix_w_in, m_ssm_a_re, m_ssm_a_im, m_ssm_log_dt, m_ssm_b_re, m_ssm_b_im, m_ssm_c_re, m_ssm_c_im, m_ssm_d, m_ssm_glu_w, m_ssm_glu_b, m_gm_v_gain, m_gm_w_s, m_gm_b_s, m_gain_ssm_out, m_gain_gm_out, m_mix_w_out, m_norm_ffn2, m_ffn2_w_in, m_ffn2_w_out, m_norm_final, v_norm_ffn1, v_ffn1_w_in, v_ffn1_w_out, v_norm_mix, v_mix_w_in, v_ssm_a_re, v_ssm_a_im, v_ssm_log_dt, v_ssm_b_re, v_ssm_b_im, v_ssm_c_re, v_ssm_c_im, v_ssm_d, v_ssm_glu_w, v_ssm_glu_b, v_gm_v_gain, v_gm_w_s, v_gm_b_s, v_gain_ssm_out, v_gain_gm_out, v_mix_w_out, v_norm_ffn2, v_ffn2_w_in, v_ffn2_w_out, v_norm_final):
    args = locals()
    w = {k: args[k] for k in WEIGHTS}
    m = {k: args["m_" + k] for k in WEIGHTS}
    v = {k: args["v_" + k] for k in WEIGHTS}
    loss, grad_x, grads, deltas, new_m, new_v = _step(x, loss_target, w, m, v)
    return (loss, grad_x, *[grads[k] for k in WEIGHTS], *[deltas[k] for k in WEIGHTS],
            *[new_m[k] for k in WEIGHTS], *[new_v[k] for k in WEIGHTS])
```

```python
import functools
import math

import jax
import jax.numpy as jnp
from jax import lax
from jax.experimental import pallas as pl
from jax.experimental.pallas import tpu as pltpu

F32 = jnp.float32
BF16 = jnp.bfloat16
MESH = pl.DeviceIdType.MESH
AXES = ("x", "y", "c")

N_DEV = 8
D_MODEL = 1024
D_FF = 2816
FF_SHARD = 2 * D_FF // N_DEV
FF_CHUNKS = 4
SSM_WIDTH = 512
SSM_CH = 16
SSM_GROUPS = 32
SSM_STATE = 64
HALF_GROUPS = 16
HALF_IN = HALF_GROUPS * SSM_CH
HALF_ST = HALF_GROUPS * SSM_STATE
GM_WIDTH = 512
GM_HEADS = 4
GM_HEAD_DIM = 128
GM_CHUNK = 128
IN_COLS = SSM_WIDTH + 2 * GM_WIDTH
EPS = 1e-6
SUBLANES = 8
LANES = 128

ADAM_LR = 0.001
ADAM_B1 = 0.9
ADAM_B2 = 0.999
ADAM_EPS = 1e-08
ADAM_WD = 0.01
ADAM_STEP = 10

VMEM_LIMIT = 52 * 1024 * 1024


def _cp(*sem):
    return pltpu.CompilerParams(dimension_semantics=sem, vmem_limit_bytes=VMEM_LIMIT)


def _rms_fwd(x, g):
    r = lax.rsqrt(jnp.mean(x * x, axis=-1, keepdims=True) + EPS)
    xh = x * r
    return xh * g, xh, r


def _rms_bwd(dy, xh, r, g):
    dxh = dy * g
    dx = r * (dxh - xh * jnp.mean(dxh * xh, axis=-1, keepdims=True))
    return dx, dy * xh


def _rows8(a):
    m, n = a.shape
    return a.reshape(m // SUBLANES, SUBLANES, n).sum(axis=0)


_GELU_K = math.sqrt(2.0 / math.pi)
_GELU_C = 0.044715


def _gelu(x):
    th = jnp.tanh(_GELU_K * (x + _GELU_C * x * x * x))
    return 0.5 * x * (1.0 + th), th


def _gelu_grad(x, th):
    return 0.5 * (1.0 + th) + 0.5 * x * (1.0 - th * th) * (_GELU_K * (1.0 + 3.0 * _GELU_C * x * x))


def _dot(a, b):
    return jnp.dot(a, b, preferred_element_type=F32)


def _dot_nt(a, b):
    return lax.dot_general(a, b, (((1,), (1,)), ((), ())), preferred_element_type=F32)


def _dot_tn(a, b):
    return lax.dot_general(a, b, (((0,), (0,)), ((), ())), preferred_element_type=F32)


def _ffn_fwd(x, gain, w_in_ag, w_out_ag, tm, name):
    n = x.shape[0]

    def body(x_ref, g_ref, wg_ref, wu_ref, wo_ref, o_ref, xn_ref):
        j = pl.program_id(1)

        @pl.when(j == 0)
        def _():
            xv = x_ref[...]
            y, _, _ = _rms_fwd(xv, g_ref[...])
            xn_ref[...] = y.astype(BF16)
            o_ref[...] = xv

        xn = xn_ref[...]
        gg = _dot(xn, wg_ref[...])
        uu = _dot(xn, wu_ref[...])
        act = (gg * jax.nn.sigmoid(gg) * uu).astype(BF16)
        o_ref[...] += 0.5 * _dot(act, wo_ref[...].reshape(FF_SHARD, D_MODEL))

    return pl.pallas_call(
        body, name=name, grid=(n // tm, FF_CHUNKS),
        in_specs=[
            pl.BlockSpec((tm, D_MODEL), lambda i, j: (i, 0)),
            pl.BlockSpec((1, D_MODEL), lambda i, j: (0, 0)),
            pl.BlockSpec((None, D_MODEL, FF_SHARD), lambda i, j: (j, 0, 0)),
            pl.BlockSpec((None, D_MODEL, FF_SHARD), lambda i, j: (j + FF_CHUNKS, 0, 0)),
            pl.BlockSpec((2, FF_SHARD // 2, D_MODEL), lambda i, j: (j, 0, 0)),
        ],
        out_specs=pl.BlockSpec((tm, D_MODEL), lambda i, j: (i, 0)),
        out_shape=jax.ShapeDtypeStruct((n, D_MODEL), F32),
        scratch_shapes=[pltpu.VMEM((tm, D_MODEL), BF16)],
        compiler_params=_cp("parallel", "arbitrary"),
    )(x, gain, w_in_ag, w_in_ag, w_out_ag)


def _ffn_bwd(x, gain, dy, w_in_ag, w_out_ag, tm, name):
    n = x.shape[0]

    def body(x_ref, g_ref, dy_ref, wg_ref, wu_ref, wo_ref, dx_ref, xn_ref, dgu_ref, act_ref, dgain_ref, acc_ref):
        i, j = pl.program_id(0), pl.program_id(1)

        @pl.when(jnp.logical_and(i == 0, j == 0))
        def _():
            dgain_ref[...] = jnp.zeros_like(dgain_ref)

        @pl.when(j == 0)
        def _():
            y, _, _ = _rms_fwd(x_ref[...], g_ref[...])
            xn_ref[...] = y.astype(BF16)
            acc_ref[...] = jnp.zeros_like(acc_ref)

        xn = xn_ref[...]
        wg, wu = wg_ref[...], wu_ref[...]
        gg = _dot(xn, wg)
        uu = _dot(xn, wu)
        dact = 0.5 * _dot_nt(dy_ref[...].astype(BF16), wo_ref[...].reshape(FF_SHARD, D_MODEL))
        sig = jax.nn.sigmoid(gg)
        silu = gg * sig
        act_ref[...] = (silu * uu).astype(BF16)
        du = (dact * silu).astype(BF16)
        dg = (dact * uu * (sig * (1.0 + gg * (1.0 - sig)))).astype(BF16)
        dgu_ref[0] = dg
        dgu_ref[1] = du
        acc_ref[...] += _dot_nt(dg, wg) + _dot_nt(du, wu)

        @pl.when(j == FF_CHUNKS - 1)
        def _():
            g = g_ref[...]
            _, xh, r = _rms_fwd(x_ref[...], g)
            dx, dgr = _rms_bwd(acc_ref[...], xh, r, g)
            dx_ref[...] = dy_ref[...] + dx
            dgain_ref[...] += _rows8(dgr)

    return pl.pallas_call(
        body, name=name, grid=(n // tm, FF_CHUNKS),
        in_specs=[
            pl.BlockSpec((tm, D_MODEL), lambda i, j: (i, 0)),
            pl.BlockSpec((1, D_MODEL), lambda i, j: (0, 0)),
            pl.BlockSpec((tm, D_MODEL), lambda i, j: (i, 0)),
            pl.BlockSpec((None, D_MODEL, FF_SHARD), lambda i, j: (j, 0, 0)),
            pl.BlockSpec((None, D_MODEL, FF_SHARD), lambda i, j: (j + FF_CHUNKS, 0, 0)),
            pl.BlockSpec((2, FF_SHARD // 2, D_MODEL), lambda i, j: (j, 0, 0)),
        ],
        out_specs=[
            pl.BlockSpec((tm, D_MODEL), lambda i, j: (i, 0)),
            pl.BlockSpec((tm, D_MODEL), lambda i, j: (i, 0)),
            pl.BlockSpec((None, 2, tm, FF_SHARD), lambda i, j: (j, 0, i, 0)),
            pl.BlockSpec((None, tm, FF_SHARD), lambda i, j: (j, i, 0)),
            pl.BlockSpec((SUBLANES, D_MODEL), lambda i, j: (0, 0)),
        ],
        out_shape=[
            jax.ShapeDtypeStruct((n, D_MODEL), F32),
            jax.ShapeDtypeStruct((n, D_MODEL), BF16),
            jax.ShapeDtypeStruct((FF_CHUNKS, 2, n, FF_SHARD), BF16),
            jax.ShapeDtypeStruct((FF_CHUNKS, n, FF_SHARD), BF16),
            jax.ShapeDtypeStruct((SUBLANES, D_MODEL), F32),
        ],
        scratch_shapes=[pltpu.VMEM((tm, D_MODEL), F32)],
        compiler_params=_cp("arbitrary", "arbitrary"),
    )(x, gain, dy, w_in_ag, w_in_ag, w_out_ag)


def _ffn_dw_in(xn, dgu, tk, name, after=None):
    n = xn.shape[0]
    nk = n // tk
    deps = [] if after is None else [after]

    def body(a_ref, b_ref, *rest):
        o_ref, acc_ref = rest[-2:]
        k = pl.program_id(2)

        @pl.when(k == 0)
        def _():
            acc_ref[...] = jnp.zeros_like(acc_ref)

        acc_ref[...] += _dot_tn(a_ref[...], b_ref[...])

        @pl.when(k == nk - 1)
        def _():
            o_ref[...] = acc_ref[...].astype(BF16)

    return pl.pallas_call(
        body, name=name, grid=(FF_CHUNKS, 2, nk),
        in_specs=[
            pl.BlockSpec((tk, D_MODEL), lambda j, p, k: (k, 0)),
            pl.BlockSpec((None, None, tk, FF_SHARD), lambda j, p, k: (j, p, k, 0)),
        ] + [pl.BlockSpec(memory_space=pl.ANY)] * len(deps),
        out_specs=pl.BlockSpec((None, D_MODEL, FF_SHARD), lambda j, p, k: (FF_CHUNKS * p + j, 0, 0)),
        out_shape=jax.ShapeDtypeStruct((N_DEV, D_MODEL, FF_SHARD), BF16),
        scratch_shapes=[pltpu.VMEM((D_MODEL, FF_SHARD), F32)],
        compiler_params=_cp("parallel", "parallel", "arbitrary"),
    )(xn, dgu, *deps)


def _ffn_dw_out(act, dy, tk, name, after=None):
    n = act.shape[1]
    nk = n // tk
    deps = [] if after is None else [after]

    def body(a_ref, b_ref, *rest):
        o_ref, acc_ref = rest[-2:]
        k = pl.program_id(1)

        @pl.when(k == 0)
        def _():
            acc_ref[...] = jnp.zeros_like(acc_ref)

        acc_ref[...] += _dot_tn(a_ref[...], b_ref[...].astype(BF16))

        @pl.when(k == nk - 1)
        def _():
            o_ref[...] = (0.5 * acc_ref[...]).astype(BF16)

    return pl.pallas_call(
        body, name=name, grid=(FF_CHUNKS, nk),
        in_specs=[
            pl.BlockSpec((None, tk, FF_SHARD), lambda j, k: (j, k, 0)),
            pl.BlockSpec((tk, D_MODEL), lambda j, k: (k, 0)),
        ] + [pl.BlockSpec(memory_space=pl.ANY)] * len(deps),
        out_specs=pl.BlockSpec((None, FF_SHARD, D_MODEL), lambda j, k: (j, 0, 0)),
        out_shape=jax.ShapeDtypeStruct((FF_CHUNKS, FF_SHARD, D_MODEL), BF16),
        scratch_shapes=[pltpu.VMEM((FF_SHARD, D_MODEL), F32)],
        compiler_params=_cp("parallel", "arbitrary"),
    )(act, dy, *deps)


def _mix_in_fwd(x, gain, w, tm, name):
    n = x.shape[0]

    def body(x_ref, g_ref, w_ref, z_ref):
        y, _, _ = _rms_fwd(x_ref[...], g_ref[...])
        z_ref[...] = _dot(y.astype(BF16), w_ref[...])

    return pl.pallas_call(
        body, name=name, grid=(n // tm,),
        in_specs=[
            pl.BlockSpec((tm, D_MODEL), lambda i: (i, 0)),
            pl.BlockSpec((1, D_MODEL), lambda i: (0, 0)),
            pl.BlockSpec((D_MODEL, IN_COLS), lambda i: (0, 0)),
        ],
        out_specs=pl.BlockSpec((tm, IN_COLS), lambda i: (i, 0)),
        out_shape=jax.ShapeDtypeStruct((n, IN_COLS), F32),
        compiler_params=_cp("parallel"),
    )(x, gain, w)


def _mix_in_bwd(x, gain, du_ssm, du_gm, dv_gm, d_res, w, tm, name):
    n = x.shape[0]

    def body(x_ref, g_ref, d0_ref, d1_ref, d2_ref, dres_ref, w_ref, dx_ref, dw_ref, dgain_ref):
        i = pl.program_id(0)

        @pl.when(i == 0)
        def _():
            dw_ref[...] = jnp.zeros_like(dw_ref)
            dgain_ref[...] = jnp.zeros_like(dgain_ref)

        g = g_ref[...]
        y, xh, r = _rms_fwd(x_ref[...], g)
        xn = y.astype(BF16)
        dxn = jnp.zeros((tm, D_MODEL), F32)
        for k, d_ref in enumerate((d0_ref, d1_ref, d2_ref)):
            dz = d_ref[...].astype(BF16)
            cols = slice(k * SSM_WIDTH, (k + 1) * SSM_WIDTH)
            dxn += _dot_nt(dz, w_ref[:, cols])
            dw_ref[:, cols] += _dot_tn(xn, dz)
        dx, dgr = _rms_bwd(dxn, xh, r, g)
        dx_ref[...] = dres_ref[...] + dx
        dgain_ref[...] += _rows8(dgr)

    row = lambda i: (i, 0)
    fixed = lambda i: (0, 0)
    return pl.pallas_call(
        body, name=name, grid=(n // tm,),
        in_specs=[
            pl.BlockSpec((tm, D_MODEL), row),
            pl.BlockSpec((1, D_MODEL), fixed),
            pl.BlockSpec((tm, SSM_WIDTH), row),
            pl.BlockSpec((tm, GM_WIDTH), row),
            pl.BlockSpec((tm, GM_WIDTH), row),
            pl.BlockSpec((tm, D_MODEL), row),
            pl.BlockSpec((D_MODEL, IN_COLS), fixed),
        ],
        out_specs=[
            pl.BlockSpec((tm, D_MODEL), row),
            pl.BlockSpec((D_MODEL, IN_COLS), fixed),
            pl.BlockSpec((SUBLANES, D_MODEL), fixed),
        ],
        out_shape=[
            jax.ShapeDtypeStruct((n, D_MODEL), F32),
            jax.ShapeDtypeStruct((D_MODEL, IN_COLS), F32),
            jax.ShapeDtypeStruct((SUBLANES, D_MODEL), F32),
        ],
        compiler_params=_cp("arbitrary"),
    )(x, gain, du_ssm, du_gm, dv_gm, d_res, w)


def _mix_out_fwd(y_ssm, y_gm, g_ssm, g_gm, w, x, tm, name):
    n = x.shape[0]

    def body(ys_ref, yg_ref, gs_ref, gg_ref, w_ref, x_ref, o_ref):
        a, _, _ = _rms_fwd(ys_ref[...], gs_ref[...])
        b, _, _ = _rms_fwd(yg_ref[...], gg_ref[...])
        o_ref[...] = (x_ref[...] + _dot(a.astype(BF16), w_ref[0:SSM_WIDTH, :])
                      + _dot(b.astype(BF16), w_ref[SSM_WIDTH:D_MODEL, :]))

    row = lambda i: (i, 0)
    fixed = lambda i: (0, 0)
    return pl.pallas_call(
        body, name=name, grid=(n // tm,),
        in_specs=[
            pl.BlockSpec((tm, SSM_WIDTH), row), pl.BlockSpec((tm, GM_WIDTH), row),
            pl.BlockSpec((1, SSM_WIDTH), fixed), pl.BlockSpec((1, GM_WIDTH), fixed),
            pl.BlockSpec((D_MODEL, D_MODEL), fixed), pl.BlockSpec((tm, D_MODEL), row),
        ],
        out_specs=pl.BlockSpec((tm, D_MODEL), row),
        out_shape=jax.ShapeDtypeStruct((n, D_MODEL), F32),
        compiler_params=_cp("parallel"),
    )(y_ssm, y_gm, g_ssm, g_gm, w, x)


def _mix_out_bwd(y_ssm, y_gm, g_ssm, g_gm, w, dx, tm, name):
    n = dx.shape[0]

    def body(ys_ref, yg_ref, gs_ref, gg_ref, w_ref, dx_ref, dys_ref, dyg_ref, dw_ref, dgs_ref, dgg_ref):
        i = pl.program_id(0)

        @pl.when(i == 0)
        def _():
            dw_ref[...] = jnp.zeros_like(dw_ref)
            dgs_ref[...] = jnp.zeros_like(dgs_ref)
            dgg_ref[...] = jnp.zeros_like(dgg_ref)

        dxb = dx_ref[...].astype(BF16)
        parts = ((ys_ref, gs_ref, dys_ref, dgs_ref, 0), (yg_ref, gg_ref, dyg_ref, dgg_ref, SSM_WIDTH))
        for y_ref, g_ref, dy_ref, dg_ref, off in parts:
            g = g_ref[...]
            yn, xh, r = _rms_fwd(y_ref[...], g)
            rows = slice(off, off + SSM_WIDTH)
            dyn = _dot_nt(dxb, w_ref[rows, :])
            dw_ref[rows, :] += _dot_tn(yn.astype(BF16), dxb)
            dy, dgr = _rms_bwd(dyn, xh, r, g)
            dy_ref[...] = dy
            dg_ref[...] += _rows8(dgr)

    row = lambda i: (i, 0)
    fixed = lambda i: (0, 0)
    return pl.pallas_call(
        body, name=name, grid=(n // tm,),
        in_specs=[
            pl.BlockSpec((tm, SSM_WIDTH), row), pl.BlockSpec((tm, GM_WIDTH), row),
            pl.BlockSpec((1, SSM_WIDTH), fixed), pl.BlockSpec((1, GM_WIDTH), fixed),
            pl.BlockSpec((D_MODEL, D_MODEL), fixed), pl.BlockSpec((tm, D_MODEL), row),
        ],
        out_specs=[
            pl.BlockSpec((tm, SSM_WIDTH), row), pl.BlockSpec((tm, GM_WIDTH), row),
            pl.BlockSpec((D_MODEL, D_MODEL), fixed),
            pl.BlockSpec((SUBLANES, SSM_WIDTH), fixed), pl.BlockSpec((SUBLANES, GM_WIDTH), fixed),
        ],
        out_shape=[
            jax.ShapeDtypeStruct((n, SSM_WIDTH), F32), jax.ShapeDtypeStruct((n, GM_WIDTH), F32),
            jax.ShapeDtypeStruct((D_MODEL, D_MODEL), F32),
            jax.ShapeDtypeStruct((SUBLANES, SSM_WIDTH), F32), jax.ShapeDtypeStruct((SUBLANES, GM_WIDTH), F32),
        ],
        compiler_params=_cp("arbitrary"),
    )(y_ssm, y_gm, g_ssm, g_gm, w, dx)


SCAN_W = 512
SCAN_PIECES = HALF_ST // SCAN_W


def _scan_tiles(src_ref, dst_ref, dst_off, c_ref, half, carry_ref, n_tiles, reverse, extra=None):
    shifts = (1, 2, 4)
    carry_row = 0 if reverse else SUBLANES - 1

    def cols(piece, im):
        lo = im * HALF_ST + piece * SCAN_W
        return slice(lo, lo + SCAN_W)

    def step(t, state):
        carries, accs = state
        k = (n_tiles - 1 - t) if reverse else t
        rows = pl.ds(pl.multiple_of(k * SUBLANES, SUBLANES), SUBLANES)
        new_carries, new_accs = [], []
        for piece in range(SCAN_PIECES):
            cr, ci = carries[piece]
            xr0 = src_ref[rows, cols(piece, 0)]
            xi0 = src_ref[rows, cols(piece, 1)]
            xr, xi = xr0, xi0
            for si, s in enumerate(shifts):
                ar = c_ref[half, si, :, cols(piece, 0)]
                ai = c_ref[half, si, :, cols(piece, 1)]
                sh = (SUBLANES - s) if reverse else s
                sr = pltpu.roll(xr, sh, 0)
                sm = pltpu.roll(xi, sh, 0)
                xr, xi = xr + (ar * sr - ai * sm), xi + (ar * sm + ai * sr)
            pr = c_ref[half, 3, :, cols(piece, 0)]
            pi = c_ref[half, 3, :, cols(piece, 1)]
            hr = xr + (pr * cr - pi * ci)
            hi = xi + (pr * ci + pi * cr)
            dst_ref[rows, pl.ds(dst_off + piece * SCAN_W, SCAN_W)] = hr
            dst_ref[rows, pl.ds(dst_off + HALF_ST + piece * SCAN_W, SCAN_W)] = hi
            new_carries.append((jnp.broadcast_to(hr[carry_row:carry_row + 1, :], (SUBLANES, SCAN_W)),
                                jnp.broadcast_to(hi[carry_row:carry_row + 1, :], (SUBLANES, SCAN_W))))
            if extra is not None:
                new_accs.append(extra(rows, piece, (xr0, xi0), (hr, hi), accs[piece]))
        return tuple(new_carries), tuple(new_accs)

    base = half * 2 * HALF_ST
    carries0 = tuple((carry_ref[:, pl.ds(base + p * SCAN_W, SCAN_W)],
                      carry_ref[:, pl.ds(base + HALF_ST + p * SCAN_W, SCAN_W)]) for p in range(SCAN_PIECES))
    zero = jnp.zeros((SUBLANES, SCAN_W), F32)
    accs0 = tuple((zero, zero) for _ in range(SCAN_PIECES)) if extra is not None else ()
    carries, accs = lax.fori_loop(0, n_tiles, step, (carries0, accs0))
    for p in range(SCAN_PIECES):
        carry_ref[:, pl.ds(base + p * SCAN_W, SCAN_W)] = carries[p][0]
        carry_ref[:, pl.ds(base + HALF_ST + p * SCAN_W, SCAN_W)] = carries[p][1]
    return accs


def _ssm_tail(hb, u, c_ref, glu_ref, glub_ref, dskip_ref):
    ypre = u * dskip_ref[...]
    parts = []
    for half in range(2):
        parts.append(_dot(hb[half], c_ref[half]))
    ypre = ypre + jnp.concatenate(parts, axis=1)
    yg, th = _gelu(ypre)
    zz = _dot(yg.astype(BF16), glu_ref[...]) + glub_ref[...]
    z1, z2 = zz[:, :SSM_WIDTH], zz[:, SSM_WIDTH:]
    sg = jax.nn.sigmoid(z2)
    return ypre, th, yg, z1, sg


def _ssm_fwd(z, bblk, cblk, glu, glub, dskip, fwdc, batch, t_chunk, name):
    n = z.shape[0]
    nk = n // batch // t_chunk
    n_tiles = t_chunk // SUBLANES

    def body(u_ref, b_ref, c_ref, glu_ref, glub_ref, dskip_ref, k_ref, y_ref, h_ref, bu_ref, carry_ref):
        @pl.when(pl.program_id(1) == 0)
        def _():
            carry_ref[...] = jnp.zeros_like(carry_ref)

        u = u_ref[...]
        ub = u.astype(BF16)
        for half in range(2):
            bu_ref[...] = _dot(ub[:, half * HALF_IN:(half + 1) * HALF_IN], b_ref[half])
            _scan_tiles(bu_ref, h_ref, half * 2 * HALF_ST, k_ref, half, carry_ref, n_tiles, False)
        hb = [h_ref[:, half * 2 * HALF_ST:(half + 1) * 2 * HALF_ST].astype(BF16) for half in range(2)]
        _, _, _, z1, sg = _ssm_tail(hb, u, c_ref, glu_ref, glub_ref, dskip_ref)
        y_ref[...] = z1 * sg

    fixed2 = lambda b, k: (0, 0)
    fixed3 = lambda b, k: (0, 0, 0)
    row = lambda b, k: (b * nk + k, 0)
    return pl.pallas_call(
        body, name=name, grid=(batch, nk),
        in_specs=[
            pl.BlockSpec((t_chunk, SSM_WIDTH), row),
            pl.BlockSpec((2, HALF_IN, 2 * HALF_ST), fixed3),
            pl.BlockSpec((2, 2 * HALF_ST, HALF_IN), fixed3),
            pl.BlockSpec((SSM_WIDTH, 2 * SSM_WIDTH), fixed2),
            pl.BlockSpec((1, 2 * SSM_WIDTH), fixed2),
            pl.BlockSpec((1, SSM_WIDTH), fixed2),
            pl.BlockSpec((2, 4, SUBLANES, 2 * HALF_ST), lambda b, k: (0, 0, 0, 0)),
        ],
        out_specs=[pl.BlockSpec((t_chunk, SSM_WIDTH), row), pl.BlockSpec((t_chunk, 4 * HALF_ST), row)],
        out_shape=[jax.ShapeDtypeStruct((n, SSM_WIDTH), F32), jax.ShapeDtypeStruct((n, 4 * HALF_ST), F32)],
        scratch_shapes=[pltpu.VMEM((t_chunk, 2 * HALF_ST), F32), pltpu.VMEM((SUBLANES, 4 * HALF_ST), F32)],
        compiler_params=_cp("parallel", "arbitrary"),
    )(z, bblk, cblk, glu, glub, dskip, fwdc)


def _ssm_bwd(z, h, dy, bblk, cblk, glu, glub, dskip, revc, batch, t_chunk, name):
    n = z.shape[0]
    nk = n // batch // t_chunk
    n_tiles = t_chunk // SUBLANES

    def body(u_ref, h_ref, dy_ref, b_ref, c_ref, glu_ref, glub_ref, dskip_ref, k_ref,
             du_ref, dglu_ref, dglub_ref, ddskip_ref, dct_ref, db_ref, q_ref, g_ref, carry_ref):
        first = jnp.logical_and(pl.program_id(0) == 0, pl.program_id(1) == 0)

        @pl.when(first)
        def _():
            for r in (dglu_ref, dglub_ref, ddskip_ref, dct_ref, db_ref, q_ref):
                r[...] = jnp.zeros_like(r)

        @pl.when(pl.program_id(1) == 0)
        def _():
            carry_ref[...] = jnp.zeros_like(carry_ref)

        u = u_ref[...]
        ub = u.astype(BF16)
        hb = [h_ref[:, half * 2 * HALF_ST:(half + 1) * 2 * HALF_ST].astype(BF16) for half in range(2)]
        ypre, th, yg, z1, sg = _ssm_tail(hb, u, c_ref, glu_ref, glub_ref, dskip_ref)
        dout = dy_ref[...]
        dz = jnp.concatenate([dout * sg, dout * z1 * sg * (1.0 - sg)], axis=1)
        dzb = dz.astype(BF16)
        dglu_ref[...] += _dot_tn(yg.astype(BF16), dzb)
        dglub_ref[...] += _rows8(dz)
        dypre = _dot_nt(dzb, glu_ref[...]) * _gelu_grad(ypre, th)
        ddskip_ref[...] += _rows8(dypre * u)
        dypb = dypre.astype(BF16)
        du_parts = []
        for half in range(2):
            dyp_h = dypb[:, half * HALF_IN:(half + 1) * HALF_IN]
            dct_ref[half] += _dot_tn(dyp_h, hb[half])
            g_ref[...] = _dot_nt(dyp_h, c_ref[half])

            def extra(rows, piece, x_in, g_out, acc, half=half):
                er, ei = g_out[0] - x_in[0], g_out[1] - x_in[1]
                base = half * 2 * HALF_ST + piece * SCAN_W
                hr = h_ref[rows, pl.ds(base, SCAN_W)]
                hi = h_ref[rows, pl.ds(base + HALF_ST, SCAN_W)]
                return acc[0] + (er * hr + ei * hi), acc[1] + (er * hi - ei * hr)

            accs = _scan_tiles(g_ref, g_ref, 0, k_ref, half, carry_ref, n_tiles, True, extra)
            for piece in range(SCAN_PIECES):
                base = half * 2 * HALF_ST + piece * SCAN_W
                q_ref[:, pl.ds(base, SCAN_W)] += accs[piece][0]
                q_ref[:, pl.ds(base + HALF_ST, SCAN_W)] += accs[piece][1]
            gb = g_ref[...].astype(BF16)
            db_ref[half] += _dot_tn(ub[:, half * HALF_IN:(half + 1) * HALF_IN], gb)
            du_parts.append(_dot_nt(gb, b_ref[half]))
        du_ref[...] = dypre * dskip_ref[...] + jnp.concatenate(du_parts, axis=1)

    fixed2 = lambda b, k: (0, 0)
    fixed3 = lambda b, k: (0, 0, 0)
    row = lambda b, k: (b * nk + (nk - 1 - k), 0)
    return pl.pallas_call(
        body, name=name, grid=(batch, nk),
        in_specs=[
            pl.BlockSpec((t_chunk, SSM_WIDTH), row),
            pl.BlockSpec((t_chunk, 4 * HALF_ST), row),
            pl.BlockSpec((t_chunk, SSM_WIDTH), row),
            pl.BlockSpec((2, HALF_IN, 2 * HALF_ST), fixed3),
            pl.BlockSpec((2, 2 * HALF_ST, HALF_IN), fixed3),
            pl.BlockSpec((SSM_WIDTH, 2 * SSM_WIDTH), fixed2),
            pl.BlockSpec((1, 2 * SSM_WIDTH), fixed2),
            pl.BlockSpec((1, SSM_WIDTH), fixed2),
            pl.BlockSpec((2, 4, SUBLANES, 2 * HALF_ST), lambda b, k: (0, 0, 0, 0)),
        ],
        out_specs=[
            pl.BlockSpec((t_chunk, SSM_WIDTH), row),
            pl.BlockSpec((SSM_WIDTH, 2 * SSM_WIDTH), fixed2),
            pl.BlockSpec((SUBLANES, 2 * SSM_WIDTH), fixed2),
            pl.BlockSpec((SUBLANES, SSM_WIDTH), fixed2),
            pl.BlockSpec((2, HALF_IN, 2 * HALF_ST), fixed3),
            pl.BlockSpec((2, HALF_IN, 2 * HALF_ST), fixed3),
            pl.BlockSpec((SUBLANES, 4 * HALF_ST), fixed2),
        ],
        out_shape=[
            jax.ShapeDtypeStruct((n, SSM_WIDTH), F32),
            jax.ShapeDtypeStruct((SSM_WIDTH, 2 * SSM_WIDTH), F32),
            jax.ShapeDtypeStruct((SUBLANES, 2 * SSM_WIDTH), F32),
            jax.ShapeDtypeStruct((SUBLANES, SSM_WIDTH), F32),
            jax.ShapeDtypeStruct((2, HALF_IN, 2 * HALF_ST), F32),
            jax.ShapeDtypeStruct((2, HALF_IN, 2 * HALF_ST), F32),
            jax.ShapeDtypeStruct((SUBLANES, 4 * HALF_ST), F32),
        ],
        scratch_shapes=[pltpu.VMEM((t_chunk, 2 * HALF_ST), F32), pltpu.VMEM((SUBLANES, 4 * HALF_ST), F32)],
        compiler_params=_cp("arbitrary", "arbitrary"),
    )(z, h, dy, bblk, cblk, glu, glub, dskip, revc)


def _gm_chunk_fwd(u, v, gain_ref, w_ref, bias_ref):
    ug, thu = _gelu(u)
    vg, thv = _gelu(v)
    rs, vns, ss = [], [], []
    for hh in range(GM_HEADS):
        cs = slice(hh * GM_HEAD_DIM, (hh + 1) * GM_HEAD_DIM)
        vn, _, r = _rms_fwd(vg[:, cs], gain_ref[:, cs])
        s = _dot(w_ref[hh], vn.astype(BF16)) + bias_ref[:, cs]
        rs.append(r)
        vns.append(vn)
        ss.append(s)
    return ug, thu, thv, vg, rs, vns, ss


def _gm_fwd(z, gain, w_tril, bias, rows, name):
    n = z.shape[0]
    chunks = rows // GM_CHUNK

    def body(u_ref, v_ref, gain_ref, w_ref, bias_ref, y_ref):
        for c in range(chunks):
            rs_ = slice(c * GM_CHUNK, (c + 1) * GM_CHUNK)
            ug, _, _, _, _, _, ss = _gm_chunk_fwd(u_ref[rs_, :], v_ref[rs_, :], gain_ref, w_ref, bias_ref)
            y_ref[rs_, :] = ug * jnp.concatenate(ss, axis=1)

    return pl.pallas_call(
        body, name=name, grid=(n // rows,),
        in_specs=[
            pl.BlockSpec((rows, GM_WIDTH), lambda i: (i, 1)),
            pl.BlockSpec((rows, GM_WIDTH), lambda i: (i, 2)),
            pl.BlockSpec((1, GM_WIDTH), lambda i: (0, 0)),
            pl.BlockSpec((GM_HEADS, GM_CHUNK, GM_CHUNK), lambda i: (0, 0, 0)),
            pl.BlockSpec((GM_CHUNK, GM_WIDTH), lambda i: (0, 0)),
        ],
        out_specs=pl.BlockSpec((rows, GM_WIDTH), lambda i: (i, 0)),
        out_shape=jax.ShapeDtypeStruct((n, GM_WIDTH), F32),
        compiler_params=_cp("parallel"),
    )(z, z, gain, w_tril, bias)


def _gm_bwd(z, dy, gain, w_tril, bias, rows, name):
    n = z.shape[0]
    chunks = rows // GM_CHUNK

    def body(u_ref, v_ref, dy_ref, gain_ref, w_ref, bias_ref, du_ref, dv_ref, dw_ref, dbias_ref, dgain_ref):
        @pl.when(pl.program_id(0) == 0)
        def _():
            dw_ref[...] = jnp.zeros_like(dw_ref)
            dbias_ref[...] = jnp.zeros_like(dbias_ref)
            dgain_ref[...] = jnp.zeros_like(dgain_ref)

        for c in range(chunks):
            rs_ = slice(c * GM_CHUNK, (c + 1) * GM_CHUNK)
            u, v = u_ref[rs_, :], v_ref[rs_, :]
            ug, thu, thv, vg, rs, vns, ss = _gm_chunk_fwd(u, v, gain_ref, w_ref, bias_ref)
            dout = dy_ref[rs_, :]
            ds = dout * ug
            du_ref[rs_, :] = dout * jnp.concatenate(ss, axis=1) * _gelu_grad(u, thu)
            dbias_ref[...] += ds
            dvg_parts, dgain_parts = [], []
            for hh in range(GM_HEADS):
                cs = slice(hh * GM_HEAD_DIM, (hh + 1) * GM_HEAD_DIM)
                dsb = ds[:, cs].astype(BF16)
                dvn = _dot_tn(w_ref[hh], dsb)
                dw_ref[hh] += _dot_nt(dsb, vns[hh].astype(BF16))
                g = gain_ref[:, cs]
                xh = vg[:, cs] * rs[hh]
                dvg, dgr = _rms_bwd(dvn, xh, rs[hh], g)
                dvg_parts.append(dvg)
                dgain_parts.append(dgr)
            dv_ref[rs_, :] = jnp.concatenate(dvg_parts, axis=1) * _gelu_grad(v, thv)
            dgain_ref[...] += _rows8(jnp.concatenate(dgain_parts, axis=1))

    row = lambda i: (i, 0)
    return pl.pallas_call(
        body, name=name, grid=(n // rows,),
        in_specs=[
            pl.BlockSpec((rows, GM_WIDTH), lambda i: (i, 1)),
            pl.BlockSpec((rows, GM_WIDTH), lambda i: (i, 2)),
            pl.BlockSpec((rows, GM_WIDTH), row),
            pl.BlockSpec((1, GM_WIDTH), lambda i: (0, 0)),
            pl.BlockSpec((GM_HEADS, GM_CHUNK, GM_CHUNK), lambda i: (0, 0, 0)),
            pl.BlockSpec((GM_CHUNK, GM_WIDTH), lambda i: (0, 0)),
        ],
        out_specs=[
            pl.BlockSpec((rows, GM_WIDTH), row), pl.BlockSpec((rows, GM_WIDTH), row),
            pl.BlockSpec((GM_HEADS, GM_CHUNK, GM_CHUNK), lambda i: (0, 0, 0)),
            pl.BlockSpec((GM_CHUNK, GM_WIDTH), lambda i: (0, 0)),
            pl.BlockSpec((SUBLANES, GM_WIDTH), lambda i: (0, 0)),
        ],
        out_shape=[
            jax.ShapeDtypeStruct((n, GM_WIDTH), F32), jax.ShapeDtypeStruct((n, GM_WIDTH), F32),
            jax.ShapeDtypeStruct((GM_HEADS, GM_CHUNK, GM_CHUNK), F32),
            jax.ShapeDtypeStruct((GM_CHUNK, GM_WIDTH), F32),
            jax.ShapeDtypeStruct((SUBLANES, GM_WIDTH), F32),
        ],
        compiler_params=_cp("arbitrary"),
    )(z, z, dy, gain, w_tril, bias)


def _loss_head(x, gain, target, tm, name):
    n = x.shape[0]

    def body(x_ref, g_ref, t_ref, dx_ref, sq_ref, dgain_ref):
        @pl.when(pl.program_id(0) == 0)
        def _():
            sq_ref[...] = jnp.zeros_like(sq_ref)
            dgain_ref[...] = jnp.zeros_like(dgain_ref)

        g = g_ref[...]
        y, xh, r = _rms_fwd(x_ref[...], g)
        err = y - t_ref[...]
        sq_ref[...] += _rows8(err * err)
        dx, dgr = _rms_bwd(err * (1.0 / D_MODEL), xh, r, g)
        dx_ref[...] = dx
        dgain_ref[...] += _rows8(dgr)

    row = lambda i: (i, 0)
    fixed = lambda i: (0, 0)
    return pl.pallas_call(
        body, name=name, grid=(n // tm,),
        in_specs=[pl.BlockSpec((tm, D_MODEL), row), pl.BlockSpec((1, D_MODEL), fixed), pl.BlockSpec((tm, D_MODEL), row)],
        out_specs=[pl.BlockSpec((tm, D_MODEL), row), pl.BlockSpec((SUBLANES, D_MODEL), fixed),
                   pl.BlockSpec((SUBLANES, D_MODEL), fixed)],
        out_shape=[jax.ShapeDtypeStruct((n, D_MODEL), F32), jax.ShapeDtypeStruct((SUBLANES, D_MODEL), F32),
                   jax.ShapeDtypeStruct((SUBLANES, D_MODEL), F32)],
        compiler_params=_cp("arbitrary"),
    )(x, gain, target)


def _adam_math(w, g, m, v):
    m2 = ADAM_B1 * m + (1.0 - ADAM_B1) * g
    v2 = ADAM_B2 * v + (1.0 - ADAM_B2) * (g * g)
    m_hat = m2 / (1.0 - ADAM_B1 ** ADAM_STEP)
    v_hat = v2 / (1.0 - ADAM_B2 ** ADAM_STEP)
    delta = -ADAM_LR * (m_hat / (jnp.sqrt(v_hat) + ADAM_EPS) + ADAM_WD * w)
    return delta, m2, v2


def _adam_sharded(parts, w, m, v, layer, earlier, name):
    depth, r, c = w.shape
    tr = max(t for t in range(16, 129, 16) if r % t == 0)

    def body(p_ref, w_ref, m_ref, v_ref, *rest):
        g_ref, d_ref, m2_ref, v2_ref = rest[-4:]
        g = p_ref[0].astype(F32)
        for s in range(1, N_DEV):
            g = g + p_ref[s].astype(F32)
        delta, m2, v2 = _adam_math(w_ref[...], g, m_ref[...], v_ref[...])
        g_ref[...] = g
        d_ref[...] = delta
        m2_ref[...] = m2
        v2_ref[...] = v2

    blk = pl.BlockSpec((None, tr, c), lambda i: (layer, i, 0))
    extra = [] if earlier is None else list(earlier)
    return pl.pallas_call(
        body, name=name, grid=(r // tr,),
        in_specs=[pl.BlockSpec((N_DEV, tr, c), lambda i: (0, i, 0)), blk, blk, blk]
        + [pl.BlockSpec(memory_space=pl.ANY)] * len(extra),
        out_specs=[blk, blk, blk, blk],
        out_shape=[jax.ShapeDtypeStruct((depth, r, c), F32)] * 4,
        input_output_aliases={4 + i: i for i in range(len(extra))},
        compiler_params=_cp("parallel"),
    )(parts, w, m, v, *extra)


def _adam_packed(g, w, m, v, name):
    r, c = g.shape

    def body(g_ref, w_ref, m_ref, v_ref, d_ref, m2_ref, v2_ref):
        delta, m2, v2 = _adam_math(w_ref[...], g_ref[...], m_ref[...], v_ref[...])
        d_ref[...] = delta
        m2_ref[...] = m2
        v2_ref[...] = v2

    blk = pl.BlockSpec((r, c), lambda i: (0, 0))
    return pl.pallas_call(
        body, name=name, grid=(1,),
        in_specs=[blk, blk, blk, blk], out_specs=[blk, blk, blk],
        out_shape=[jax.ShapeDtypeStruct((r, c), F32)] * 3,
        compiler_params=_cp("arbitrary"),
    )(g, w, m, v)


def _my_place():
    return lax.axis_index("x"), lax.axis_index("y"), lax.axis_index("c")


def _flip(place, rel):
    x, y, c = place
    return (1 - x if rel & 4 else x, 1 - y if rel & 2 else y, 1 - c if rel & 1 else c)


def _index(place):
    return 4 * place[0] + 2 * place[1] + place[2]


def _all_gather(shards, name):
    na = len(shards)

    def body(*refs):
        xs, outs = refs[:na], refs[na:2 * na]
        send_sems, recv_sems, local_sems = refs[2 * na:]
        me = _my_place()
        sibling = _flip(me, 1)
        chips = [_flip(me, 4), _flip(me, 2), _flip(me, 6)]

        def copy(a, k, block, to, src=None):
            slot = outs[a].at[_index(block)]
            return pltpu.make_async_remote_copy(
                src_ref=slot if src is None else src, dst_ref=slot,
                send_sem=send_sems.at[a, k], recv_sem=recv_sems.at[a, k],
                device_id=to, device_id_type=MESH)

        mine = [pltpu.make_async_copy(xs[a], outs[a].at[_index(me)], local_sems.at[a]) for a in range(na)]
        for cp in mine:
            cp.start()
        first = []
        for a in range(na):
            first.append(copy(a, 0, me, sibling, src=xs[a]))
            first += [copy(a, 1 + j, me, chip, src=xs[a]) for j, chip in enumerate(chips)]
        for cp in first:
            cp.start()
        passed = []
        for a in range(na):
            for j, chip in enumerate(chips):
                copy(a, 1 + j, chip, me).wait_recv()
                fwd = copy(a, 4 + j, chip, sibling)
                fwd.start()
                passed.append(fwd)
        for a in range(na):
            copy(a, 0, sibling, me).wait_recv()
            for j, chip in enumerate(chips):
                copy(a, 4 + j, _flip(chip, 1), me).wait_recv()
        for cp in first + passed:
            cp.wait_send()
        for cp in mine:
            cp.wait()

    hbm = pl.BlockSpec(memory_space=pl.ANY)
    return pl.pallas_call(
        body, name=name,
        in_specs=[hbm] * na, out_specs=[hbm] * na,
        out_shape=[jax.ShapeDtypeStruct((N_DEV,) + s.shape, s.dtype) for s in shards],
        scratch_shapes=[pltpu.SemaphoreType.DMA((na, 7)), pltpu.SemaphoreType.DMA((na, 7)),
                        pltpu.SemaphoreType.DMA((na,))],
    )(*shards)


_HBM = pl.BlockSpec(memory_space=pltpu.HBM)
_SEM = pl.BlockSpec(memory_space=pltpu.SEMAPHORE)
_EFFECT = pltpu.SideEffectType.DATAFLOW_SIDE_EFFECTING


def _exchange_copy(src_ref, land_ref, send_sems, recv_sems, a, rel, me, scatter, landed):
    peer = _flip(me, rel)
    src = src_ref.at[_index(peer)] if scatter else src_ref
    return pltpu.make_async_remote_copy(
        src_ref=src, dst_ref=land_ref.at[_index(peer if landed else me)],
        send_sem=send_sems.at[a * (N_DEV - 1) + rel - 1], recv_sem=recv_sems.at[a * (N_DEV - 1) + rel - 1],
        device_id=peer, device_id_type=MESH)


def _own_slot(data, me, scatter):
    if scatter:
        own = lax.dynamic_slice_in_dim(data, me, 1, axis=0)
        shape = data.shape
    else:
        own = data[None]
        shape = (N_DEV,) + data.shape
    start = (me,) + (0,) * (len(shape) - 1)
    return lax.dynamic_update_slice(lax.empty(shape, data.dtype), own, start)


def _exchange_start(groups, me, scatter, name, after=None):
    sizes = [len(g) for g in groups]
    srcs = [a for g in groups for a in g]
    lands = [_own_slot(a, me, scatter) for a in srcs]
    na, ng = len(srcs), len(groups)
    deps = [] if after is None else [after]

    def body(*refs):
        src_refs, land_refs = refs[:na], refs[na:2 * na]
        sems = refs[2 * na + len(deps):2 * na + len(deps) + 2 * ng]
        token = refs[-1]
        place = _my_place()
        a = 0
        for g, size in enumerate(sizes):
            for k in range(size):
                for rel in range(1, N_DEV):
                    _exchange_copy(src_refs[a], land_refs[a], sems[2 * g], sems[2 * g + 1], k, rel, place, scatter,
                                   False).start()
                a += 1
        token[...] = jnp.zeros_like(token)

    sem_shapes = [pltpu.SemaphoreType.DMA((size * (N_DEV - 1),)) for size in sizes for _ in range(2)]
    outs = pl.pallas_call(
        body, name=name,
        in_specs=[_HBM] * (2 * na) + [pl.BlockSpec(memory_space=pl.ANY)] * len(deps),
        out_specs=[_SEM] * (2 * ng) + [_HBM] * (2 * na) + [pl.BlockSpec(memory_space=pltpu.VMEM)],
        out_shape=sem_shapes + [pltpu.HBM(a.shape, a.dtype) for a in srcs + lands]
        + [jax.ShapeDtypeStruct((SUBLANES, LANES), F32)],
        input_output_aliases={i: 2 * ng + i for i in range(2 * na)},
        compiler_params=pltpu.CompilerParams(has_side_effects=_EFFECT),
    )(*[pltpu.with_memory_space_constraint(a, pltpu.HBM) for a in srcs + lands], *deps)
    sems, thru, token = outs[:2 * ng], outs[2 * ng:2 * ng + 2 * na], outs[-1]
    handles, a = [], 0
    for g, size in enumerate(sizes):
        handles.append((sems[2 * g], sems[2 * g + 1], thru[a:a + size], thru[na + a:na + a + size]))
        a += size
    return handles, token


def _exchange_wait(handle, after, scatter, name):
    send_sems, recv_sems, srcs, lands = handle
    na = len(srcs)

    def body(*refs):
        src_refs, land_refs = refs[:na], refs[na:2 * na]
        send_ref, recv_ref = refs[2 * na], refs[2 * na + 1]
        place = _my_place()
        for a in range(na):
            for rel in range(1, N_DEV):
                cp = _exchange_copy(src_refs[a], land_refs[a], send_ref, recv_ref, a, rel, place, scatter, True)
                cp.wait_send()
                cp.wait_recv()

    outs = pl.pallas_call(
        body, name=name,
        in_specs=[_HBM] * (2 * na) + [_SEM, _SEM, pl.BlockSpec(memory_space=pl.ANY)],
        out_specs=[_HBM] * (2 * na),
        out_shape=[pltpu.HBM(a.shape, a.dtype) for a in list(srcs) + list(lands)],
        input_output_aliases={i: i for i in range(2 * na)},
        compiler_params=pltpu.CompilerParams(has_side_effects=_EFFECT),
    )(*srcs, *lands, send_sems, recv_sems, after)
    return outs[na:]


def _behind(arr, token):
    return arr + token[0:1, 0:1]


def _all_reduce_small(g, name):
    _, r, c = g.shape

    def body(g_ref, o_ref, land_ref, red_ref, send1, recv1, send2, recv2):
        me = _my_place()
        idx = _index(me)

        def scatter(rel):
            peer = _flip(me, rel)
            return pltpu.make_async_remote_copy(
                src_ref=g_ref.at[_index(peer)], dst_ref=land_ref.at[idx],
                send_sem=send1.at[rel - 1], recv_sem=recv1.at[rel - 1], device_id=peer, device_id_type=MESH)

        def gather(rel):
            peer = _flip(me, rel)
            return pltpu.make_async_remote_copy(
                src_ref=red_ref, dst_ref=o_ref.at[idx],
                send_sem=send2.at[rel - 1], recv_sem=recv2.at[rel - 1], device_id=peer, device_id_type=MESH)

        for rel in range(1, N_DEV):
            scatter(rel).start()
        land_ref[idx] = g_ref[idx]
        for rel in range(1, N_DEV):
            scatter(rel).wait()
        acc = land_ref[0]
        for s in range(1, N_DEV):
            acc = acc + land_ref[s]
        red_ref[...] = acc
        for rel in range(1, N_DEV):
            gather(rel).start()
        o_ref[idx] = acc
        for rel in range(1, N_DEV):
            gather(rel).wait()

    vmem = pl.BlockSpec(memory_space=pltpu.VMEM)
    return pl.pallas_call(
        body, name=name,
        in_specs=[vmem], out_specs=vmem,
        out_shape=jax.ShapeDtypeStruct(g.shape, F32),
        scratch_shapes=[pltpu.VMEM(g.shape, F32), pltpu.VMEM((r, c), F32)]
        + [pltpu.SemaphoreType.DMA((N_DEV - 1,))] * 4,
        compiler_params=pltpu.CompilerParams(vmem_limit_bytes=VMEM_LIMIT),
    )(g)


def _ssm_discretize(a_re, a_im, log_dt, b_re, b_im):
    dt = jnp.exp(log_dt)[:, None]
    mag = jnp.exp(a_re * dt)
    lr, li = mag * jnp.cos(a_im * dt), mag * jnp.sin(a_im * dt)
    den = a_re * a_re + a_im * a_im
    qr = ((lr - 1.0) * a_re + li * a_im) / den
    qi = (li * a_re - (lr - 1.0) * a_im) / den
    bbr = qr[..., None] * b_re - qi[..., None] * b_im
    bbi = qr[..., None] * b_im + qi[..., None] * b_re
    return lr, li, bbr, bbi


def _halves(a):
    return a.reshape((2, HALF_GROUPS) + a.shape[1:])


def _block_diag(blocks):
    g, r, c = blocks.shape
    eye = jnp.eye(g, dtype=blocks.dtype)
    return jnp.einsum("grc,gh->grhc", blocks, eye).reshape(g * r, g * c)


def _block_diag_take(dense, g, r, c):
    return jnp.einsum("grhc,gh->grc", dense.reshape(g, r, g, c), jnp.eye(g, dtype=dense.dtype))


def _ssm_matrices(bbr, bbi, c_re, c_im, glu_w, glu_b, d_skip):
    bre, bim = _halves(jnp.swapaxes(bbr, 1, 2)), _halves(jnp.swapaxes(bbi, 1, 2))
    bblk = jnp.stack([jnp.concatenate([_block_diag(bre[h]), _block_diag(bim[h])], axis=1) for h in range(2)])
    cre, cim = _halves(jnp.swapaxes(c_re, 1, 2)), _halves(jnp.swapaxes(c_im, 1, 2))
    cblk = jnp.stack([jnp.concatenate([_block_diag(cre[h]), -_block_diag(cim[h])], axis=0) for h in range(2)])
    glu = jnp.concatenate([_block_diag(glu_w[:, :, :SSM_CH]), _block_diag(glu_w[:, :, SSM_CH:])], axis=1)
    glub = jnp.concatenate([glu_b[:, :SSM_CH].reshape(1, -1), glu_b[:, SSM_CH:].reshape(1, -1)], axis=1)
    return bblk.astype(BF16), cblk.astype(BF16), glu.astype(BF16), glub, d_skip.reshape(1, -1)


def _scan_constants(lr, li, reverse):
    if reverse:
        li = -li
    pows = [(lr, li)]
    for _ in range(SUBLANES - 1):
        pr, pi = pows[-1]
        pows.append((pr * lr - pi * li, pr * li + pi * lr))
    row = jnp.arange(SUBLANES)[:, None]

    def flat(a):
        return a.reshape(2, 1, HALF_ST)

    mats = []
    for s in (1, 2, 4):
        keep = (row + s <= SUBLANES - 1) if reverse else (row >= s)
        mats.append(tuple(jnp.where(keep[None], flat(p), 0.0) for p in pows[s - 1]))
    order = [SUBLANES - 1 - j for j in range(SUBLANES)] if reverse else list(range(SUBLANES))
    mats.append(tuple(jnp.concatenate([flat(pows[j][k]) for j in order], axis=1) for k in range(2)))
    return jnp.stack([jnp.concatenate([m[0], m[1]], axis=2) for m in mats], axis=1)


def _pack(arrs, rows):
    flat = jnp.concatenate([a.reshape(-1) for a in arrs])
    return jnp.pad(flat, (0, rows * LANES - flat.shape[0])).reshape(rows, LANES)


def _unpack(buf, like):
    flat = buf.reshape(-1)
    out, off = [], 0
    for a in like:
        out.append(flat[off:off + a.size].reshape(a.shape))
        off += a.size
    return out


SMALL = ("norm_ffn1", "norm_mix", "ssm_a_re", "ssm_a_im", "ssm_log_dt", "ssm_b_re", "ssm_b_im", "ssm_c_re",
         "ssm_c_im", "ssm_d", "ssm_glu_w", "ssm_glu_b", "gm_v_gain", "gm_w_s", "gm_b_s", "gain_ssm_out",
         "gain_gm_out", "norm_ffn2", "norm_final")
BIG = ("ffn1_w_in", "ffn1_w_out", "mix_w_in", "mix_w_out", "ffn2_w_in", "ffn2_w_out")
WEIGHTS = ("norm_ffn1", "ffn1_w_in", "ffn1_w_out", "norm_mix", "mix_w_in", "ssm_a_re", "ssm_a_im", "ssm_log_dt",
           "ssm_b_re", "ssm_b_im", "ssm_c_re", "ssm_c_im", "ssm_d", "ssm_glu_w", "ssm_glu_b", "gm_v_gain", "gm_w_s",
           "gm_b_s", "gain_ssm_out", "gain_gm_out", "mix_w_out", "norm_ffn2", "ffn2_w_in", "ffn2_w_out", "norm_final")


def _step(x, target, w, m, v):
    batch, seq, _ = x.shape
    n = batch * seq
    depth = w["norm_ffn1"].shape[0]
    tm = min(512, n)
    tk = min(2048, n)
    t_chunk = min(256, seq)
    gm_rows = min(512, seq)
    x = x.reshape(n, D_MODEL)
    target = target.reshape(n, D_MODEL)

    assert depth == 2
    me = _index(_my_place())
    shard = lambda group, l: [w[f"{group}_w_in"][l].astype(BF16), w[f"{group}_w_out"][l].astype(BF16)]
    order = [(g, l) for l in range(depth) for g in ("ffn1", "mix", "ffn2")]
    gathered, pending = {}, {}

    def gather_start(i, after):
        batch_i = order[2 * i:2 * i + 2]
        handles, tok = _exchange_start([shard(g, l) for g, l in batch_i], me, False, f"all_gather_start_{i}", after)
        pending.update(zip(batch_i, handles))
        return tok

    gather_start(0, None)

    def weights(group, l, after=None):
        if (group, l) not in gathered:
            w_in, w_out = _exchange_wait(pending[(group, l)], after, False, f"all_gather_wait_{group}_{l}")
            if group == "mix":
                w_in = jnp.transpose(w_in, (1, 0, 2)).reshape(D_MODEL, IN_COLS)
                w_out = w_out.reshape(D_MODEL, D_MODEL)
            gathered[(group, l)] = (w_in, w_out)
        return gathered[(group, l)]

    tril = jnp.tril(jnp.ones((GM_CHUNK, GM_CHUNK), bool))
    layers = []
    for l in range(depth):
        disc, disc_vjp = jax.vjp(_ssm_discretize, w["ssm_a_re"][l], w["ssm_a_im"][l], w["ssm_log_dt"][l],
                                 w["ssm_b_re"][l], w["ssm_b_im"][l])
        lr, li, bbr, bbi = disc
        bblk, cblk, glu, glub, dskip = _ssm_matrices(bbr, bbi, w["ssm_c_re"][l], w["ssm_c_im"][l],
                                                     w["ssm_glu_w"][l], w["ssm_glu_b"][l], w["ssm_d"][l])
        layers.append(dict(
            disc_vjp=disc_vjp, lr=lr, li=li, bblk=bblk, cblk=cblk, glu=glu, glub=glub, dskip=dskip,
            fwdc=_scan_constants(lr, li, False), revc=_scan_constants(lr, li, True),
            w_tril=jnp.where(tril[None], w["gm_w_s"][l], 0.0).astype(BF16),
            gm_bias=jnp.repeat(w["gm_b_s"][l].T, GM_HEAD_DIM, axis=1),
            g1=w["norm_ffn1"][l][None], gmix=w["norm_mix"][l][None], g2=w["norm_ffn2"][l][None],
            gv=w["gm_v_gain"][l][None], gs=w["gain_ssm_out"][l][None], gg=w["gain_gm_out"][l][None],
        ))

    saved = []
    for l in range(depth):
        p = layers[l]
        x0 = x
        g1, gmix = p["g1"], p["gmix"]
        if l == 0:
            w_in, w_out = weights("ffn1", l, layers[-1]["revc"])
            g1 = _behind(g1, gather_start(1, w_in))
        else:
            w_in, w_out = weights("ffn1", l, x0)
        x1 = _ffn_fwd(x0, g1, w_in, w_out, tm, f"ffn1_fwd_{l}")
        if l == 0:
            gmix = _behind(gmix, gather_start(2, x1))
        mwi, mwo = weights("mix", l, x1)
        z = _mix_in_fwd(x1, gmix, mwi, tm, f"mix_in_fwd_{l}")
        y_ssm, h = _ssm_fwd(z, p["bblk"], p["cblk"], p["glu"], p["glub"], p["dskip"], p["fwdc"], batch, t_chunk,
                            f"ssm_fwd_{l}")
        y_gm = _gm_fwd(z, p["gv"], p["w_tril"], p["gm_bias"], gm_rows, f"gm_fwd_{l}")
        x2 = _mix_out_fwd(y_ssm, y_gm, p["gs"], p["gg"], mwo, x1, tm, f"mix_out_fwd_{l}")
        x = _ffn_fwd(x2, p["g2"], *weights("ffn2", l, x2), tm, f"ffn2_fwd_{l}")
        saved.append((x0, x1, x2, z, h, y_ssm, y_gm))

    dx, sq, dnf = _loss_head(x, w["norm_final"][None], target, tm, "loss_head")
    loss = lax.psum((0.5 / D_MODEL) * jnp.sum(sq), AXES)

    small = {k: [None] * depth for k in SMALL if k != "norm_final"}
    sent = []

    def send(group, l, keys, parts):
        (handle,), tok = _exchange_start([parts], me, True, f"reduce_scatter_start_{group}_{l}")
        sent.append((group, l, keys, handle))
        return tok

    token = None
    for l in reversed(range(depth)):
        p = layers[l]
        x0, x1, x2, z, h, y_ssm, y_gm = saved[l]
        mwi, mwo = weights("mix", l)
        dx_out = dx
        g2 = p["g2"] if token is None else _behind(p["g2"], token)
        dx, xn, dgu, act, dgain = _ffn_bwd(x2, g2, dx_out, *weights("ffn2", l), tm, f"ffn2_bwd_{l}")
        dw_in = _ffn_dw_in(xn, dgu, tk, f"ffn2_dw_in_{l}")
        dw_out = _ffn_dw_out(act, dx_out, tk, f"ffn2_dw_out_{l}").reshape(N_DEV, FF_SHARD // 2, D_MODEL)
        token = send("ffn2", l, ("ffn2_w_in", "ffn2_w_out"), [dw_in, dw_out])
        small["norm_ffn2"][l] = dgain.sum(0)

        dy_ssm, dy_gm, dwo, dgs, dgg = _mix_out_bwd(y_ssm, y_gm, _behind(p["gs"], token), p["gg"], mwo, dx, tm,
                                                    f"mix_out_bwd_{l}")
        dwo = dwo.astype(BF16).reshape(N_DEV, D_MODEL // N_DEV, D_MODEL)
        small["gain_ssm_out"][l] = dgs.sum(0)
        small["gain_gm_out"][l] = dgg.sum(0)

        du_ssm, dglu, dglub, ddskip, dct, db, q = _ssm_bwd(
            z, h, dy_ssm, p["bblk"], p["cblk"], p["glu"], p["glub"], p["dskip"], p["revc"], batch, t_chunk,
            f"ssm_bwd_{l}")
        du_gm, dv_gm, dws, dbias, dgv = _gm_bwd(z, dy_gm, p["gv"], p["w_tril"], p["gm_bias"], gm_rows, f"gm_bwd_{l}")

        q = q.sum(0).reshape(2, 2, HALF_GROUPS, SSM_STATE)
        qr, qi = q[:, 0].reshape(SSM_GROUPS, SSM_STATE), q[:, 1].reshape(SSM_GROUPS, SSM_STATE)
        den = p["lr"] * p["lr"] + p["li"] * p["li"]
        d_re = (qr * p["lr"] + qi * p["li"]) / den
        d_im = (qi * p["lr"] - qr * p["li"]) / den
        dbb = jnp.stack([_block_diag_take(db[hf, :, k * HALF_ST:(k + 1) * HALF_ST], HALF_GROUPS, SSM_CH, SSM_STATE)
                         for k in range(2) for hf in range(2)]).reshape(2, SSM_GROUPS, SSM_CH, SSM_STATE)
        dcc = jnp.stack([_block_diag_take(dct[hf, :, k * HALF_ST:(k + 1) * HALF_ST], HALF_GROUPS, SSM_CH, SSM_STATE)
                         for k in range(2) for hf in range(2)]).reshape(2, SSM_GROUPS, SSM_CH, SSM_STATE)
        da_re, da_im, dlog_dt, db_re, db_im = p["disc_vjp"](
            (d_re, -d_im, jnp.swapaxes(dbb[0], 1, 2), jnp.swapaxes(dbb[1], 1, 2)))
        small["ssm_a_re"][l], small["ssm_a_im"][l], small["ssm_log_dt"][l] = da_re, da_im, dlog_dt
        small["ssm_b_re"][l], small["ssm_b_im"][l] = db_re, db_im
        small["ssm_c_re"][l], small["ssm_c_im"][l] = dcc[0], -dcc[1]
        small["ssm_d"][l] = ddskip.sum(0).reshape(SSM_GROUPS, SSM_CH)
        small["ssm_glu_w"][l] = jnp.concatenate(
            [_block_diag_take(dglu[:, :SSM_WIDTH], SSM_GROUPS, SSM_CH, SSM_CH),
             _block_diag_take(dglu[:, SSM_WIDTH:], SSM_GROUPS, SSM_CH, SSM_CH)], axis=2)
        dglub = dglub.sum(0)
        small["ssm_glu_b"][l] = jnp.concatenate(
            [dglub[:SSM_WIDTH].reshape(SSM_GROUPS, SSM_CH), dglub[SSM_WIDTH:].reshape(SSM_GROUPS, SSM_CH)], axis=1)
        small["gm_v_gain"][l] = dgv.sum(0)
        small["gm_w_s"][l] = jnp.where(tril[None], dws, 0.0)
        small["gm_b_s"][l] = dbias.reshape(GM_CHUNK, GM_HEADS, GM_HEAD_DIM).sum(-1).T

        dx, dwi, dgain = _mix_in_bwd(x1, p["gmix"], du_ssm, du_gm, dv_gm, dx, mwi, tm, f"mix_in_bwd_{l}")
        dwi = jnp.transpose(dwi.astype(BF16).reshape(D_MODEL, N_DEV, IN_COLS // N_DEV), (1, 0, 2))
        token = send("mix", l, ("mix_w_in", "mix_w_out"), [dwi, dwo])
        small["norm_mix"][l] = dgain.sum(0)

        dx_out = dx
        dx, xn, dgu, act, dgain = _ffn_bwd(x0, _behind(p["g1"], token), dx_out, *weights("ffn1", l), tm,
                                           f"ffn1_bwd_{l}")
        small["norm_ffn1"][l] = dgain.sum(0)
        if l > 0:
            dw_in = _ffn_dw_in(xn, dgu, tk, f"ffn1_dw_in_{l}")
            dw_out = _ffn_dw_out(act, dx_out, tk, f"ffn1_dw_out_{l}").reshape(N_DEV, FF_SHARD // 2, D_MODEL)
            token = send("ffn1", l, ("ffn1_w_in", "ffn1_w_out"), [dw_in, dw_out])
            continue
        small_g = [jnp.stack(small[k]) if k != "norm_final" else dnf.sum(0) for k in SMALL]
        total = sum(int(math.prod(w[k].shape)) for k in SMALL)
        rows = -(-total // (LANES * N_DEV * SUBLANES)) * N_DEV * SUBLANES
        g_all = _all_reduce_small(_pack(small_g, rows).reshape(N_DEV, rows // N_DEV, LANES), "all_reduce_small")
        dw_in = _ffn_dw_in(xn, dgu, tk, f"ffn1_dw_in_{l}", after=g_all)
        token = send("ffn1_in", l, ("ffn1_w_in",), [dw_in])
        dw_out = _ffn_dw_out(act, dx_out, tk, f"ffn1_dw_out_{l}", after=token).reshape(
            N_DEV, FF_SHARD // 2, D_MODEL)
        token = send("ffn1_out", l, ("ffn1_w_out",), [dw_out])

    grad_x = dx.reshape(batch, seq, D_MODEL)
    grads, deltas, new_m, new_v = {}, {}, {}, {}

    g_all = _behind(g_all.reshape(rows, LANES), token)
    like = [w[k] for k in SMALL]
    d_p, m_p, v_p = _adam_packed(g_all, _pack(like, rows), _pack([m[k] for k in SMALL], rows),
                                 _pack([v[k] for k in SMALL], rows), "adam_small")
    for k, g_, d_, m_, v_ in zip(SMALL, _unpack(g_all, like), _unpack(d_p, like), _unpack(m_p, like),
                                 _unpack(v_p, like)):
        grads[k], deltas[k], new_m[k], new_v[k] = g_, d_, m_, v_

    results = {}
    after = d_p
    for group, l, keys, handle in sent:
        landed = _exchange_wait(handle, after, True, f"reduce_scatter_wait_{group}_{l}")
        for k, parts in zip(keys, landed):
            results[k] = _adam_sharded(parts, w[k], m[k], v[k], l, results.get(k), f"adam_{k}_{l}")
            after = results[k][0]
    for k in BIG:
        grads[k], deltas[k], new_m[k], new_v[k] = results[k]
    return loss, grad_x, grads, deltas, new_m, new_v


def kernel(x, norm_ffn1, ffn1_w_in, ffn1_w_out, norm_mix, mix_w_in, ssm_a_re, ssm_a_im, ssm_log_dt, ssm_b_re, ssm_b_im, ssm_c_re, ssm_c_im, ssm_d, ssm_glu_w, ssm_glu_b, gm_v_gain, gm_w_s, gm_b_s, gain_ssm_out, gain_gm_out, mix_w_out, norm_ffn2, ffn2_w_in, ffn2_w_out, norm_final, loss_target, m_norm_ffn1, m_ffn1_w_in, m_ffn1_w_out, m_norm_mix, m_mix_w_in, m_ssm_a_re, m_ssm_a_im, m_ssm_log_dt, m_ssm_b_re, m_ssm_b_im, m_ssm_c_re, m_ssm_c_im, m_ssm_d, m_ssm_glu_w, m_ssm_glu_b, m_gm_v_gain, m_gm_w_s, m_gm_b_s, m_gain_ssm_out, m_gain_gm_out, m_mix_w_out, m_norm_ffn2, m_ffn2_w_in, m_ffn2_w_out, m_norm_final, v_norm_ffn1, v_ffn1_w_in, v_ffn1_w_out, v_norm_mix, v_mix_w_in, v_ssm_a_re, v_ssm_a_im, v_ssm_log_dt, v_ssm_b_re, v_ssm_b_im, v_ssm_c_re, v_ssm_c_im, v_ssm_d, v_ssm_glu_w, v_ssm_glu_b, v_gm_v_gain, v_gm_w_s, v_gm_b_s, v_gain_ssm_out, v_gain_gm_out, v_mix_w_out, v_norm_ffn2, v_ffn2_w_in, v_ffn2_w_out, v_norm_final):
    args = locals()
    w = {k: args[k] for k in WEIGHTS}
    m = {k: args["m_" + k] for k in WEIGHTS}
    v = {k: args["v_" + k] for k in WEIGHTS}
    loss, grad_x, grads, deltas, new_m, new_v = _step(x, loss_target, w, m, v)
    return (loss, grad_x, *[grads[k] for k in WEIGHTS], *[deltas[k] for k in WEIGHTS],
            *[new_m[k] for k in WEIGHTS], *[new_v[k] for k in WEIGHTS])
```

```python
import functools
import math

import jax
import jax.numpy as jnp
from jax import lax
from jax.experimental import pallas as pl
from jax.experimental.pallas import tpu as pltpu

F32 = jnp.float32
BF16 = jnp.bfloat16
MESH = pl.DeviceIdType.MESH
AXES = ("x", "y", "c")

N_DEV = 8
D_MODEL = 1024
D_FF = 2816
FF_SHARD = 2 * D_FF // N_DEV
FF_CHUNKS = 4
MXU_DIM = 256
FF_PIECES = tuple((lo, min(lo + MXU_DIM, FF_SHARD)) for lo in range(0, FF_SHARD, MXU_DIM))
SSM_WIDTH = 512
SSM_CH = 16
SSM_GROUPS = 32
SSM_STATE = 64
HALF_GROUPS = 16
HALF_IN = HALF_GROUPS * SSM_CH
HALF_ST = HALF_GROUPS * SSM_STATE
GM_WIDTH = 512
GM_HEADS = 4
GM_HEAD_DIM = 128
GM_CHUNK = 128
IN_COLS = SSM_WIDTH + 2 * GM_WIDTH
EPS = 1e-6
SUBLANES = 8
LANES = 128

ADAM_LR = 0.001
ADAM_B1 = 0.9
ADAM_B2 = 0.999
ADAM_EPS = 1e-08
ADAM_WD = 0.01
ADAM_STEP = 10

VMEM_LIMIT = 52 * 1024 * 1024


def _cp(*sem):
    return pltpu.CompilerParams(dimension_semantics=sem, vmem_limit_bytes=VMEM_LIMIT)


def _rms_fwd(x, g):
    r = lax.rsqrt(jnp.mean(x * x, axis=-1, keepdims=True) + EPS)
    xh = x * r
    return xh * g, xh, r


def _rms_bwd(dy, xh, r, g):
    dxh = dy * g
    dx = r * (dxh - xh * jnp.mean(dxh * xh, axis=-1, keepdims=True))
    return dx, dy * xh


def _rows8(a):
    m, n = a.shape
    return a.reshape(m // SUBLANES, SUBLANES, n).sum(axis=0)


_GELU_K = math.sqrt(2.0 / math.pi)
_GELU_C = 0.044715


def _gelu(x):
    th = jnp.tanh(_GELU_K * (x + _GELU_C * x * x * x))
    return 0.5 * x * (1.0 + th), th


def _gelu_grad(x, th):
    return 0.5 * (1.0 + th) + 0.5 * x * (1.0 - th * th) * (_GELU_K * (1.0 + 3.0 * _GELU_C * x * x))


def _dot(a, b):
    return jnp.dot(a, b, preferred_element_type=F32)


def _dot_nt(a, b):
    return lax.dot_general(a, b, (((1,), (1,)), ((), ())), preferred_element_type=F32)


def _dot_tn(a, b):
    return lax.dot_general(a, b, (((0,), (0,)), ((), ())), preferred_element_type=F32)


def _ffn_fwd(x, gain, w_in_ag, w_out_ag, tm, name):
    n = x.shape[0]

    def body(x_ref, g_ref, wg_ref, wu_ref, wo_ref, o_ref, xn_ref):
        j = pl.program_id(1)

        @pl.when(j == 0)
        def _():
            xv = x_ref[...]
            y, _, _ = _rms_fwd(xv, g_ref[...])
            xn_ref[...] = y.astype(BF16)
            o_ref[...] = xv

        xn = xn_ref[...]
        wo = wo_ref[...].reshape(FF_SHARD, D_MODEL)
        out = None
        for lo, hi in FF_PIECES:
            gg = _dot(xn, wg_ref[:, lo:hi])
            uu = _dot(xn, wu_ref[:, lo:hi])
            act = (gg * jax.nn.sigmoid(gg) * uu).astype(BF16)
            part = _dot(act, wo[lo:hi, :])
            out = part if out is None else out + part
        o_ref[...] += 0.5 * out

    return pl.pallas_call(
        body, name=name, grid=(n // tm, FF_CHUNKS),
        in_specs=[
            pl.BlockSpec((tm, D_MODEL), lambda i, j: (i, 0)),
            pl.BlockSpec((1, D_MODEL), lambda i, j: (0, 0)),
            pl.BlockSpec((None, D_MODEL, FF_SHARD), lambda i, j: (j, 0, 0)),
            pl.BlockSpec((None, D_MODEL, FF_SHARD), lambda i, j: (j + FF_CHUNKS, 0, 0)),
            pl.BlockSpec((2, FF_SHARD // 2, D_MODEL), lambda i, j: (j, 0, 0)),
        ],
        out_specs=pl.BlockSpec((tm, D_MODEL), lambda i, j: (i, 0)),
        out_shape=jax.ShapeDtypeStruct((n, D_MODEL), F32),
        scratch_shapes=[pltpu.VMEM((tm, D_MODEL), BF16)],
        compiler_params=_cp("parallel", "arbitrary"),
    )(x, gain, w_in_ag, w_in_ag, w_out_ag)


def _ffn_bwd(x, gain, dy, w_in_ag, w_out_ag, tm, name):
    n = x.shape[0]

    def body(x_ref, g_ref, dy_ref, wg_ref, wu_ref, wo_ref, dx_ref, xn_ref, dgu_ref, act_ref, dgain_ref, dyb_ref):
        i, j = pl.program_id(0), pl.program_id(1)

        @pl.when(jnp.logical_and(i == 0, j == 0))
        def _():
            dgain_ref[...] = jnp.zeros_like(dgain_ref)

        @pl.when(j == 0)
        def _():
            y, _, _ = _rms_fwd(x_ref[...], g_ref[...])
            xn_ref[...] = y.astype(BF16)
            dyb_ref[...] = dy_ref[...].astype(BF16)
            dx_ref[...] = jnp.zeros_like(dx_ref)

        xn, dyb = xn_ref[...], dyb_ref[...]
        wo = wo_ref[...].reshape(FF_SHARD, D_MODEL)
        for lo, hi in FF_PIECES:
            wg, wu = wg_ref[:, lo:hi], wu_ref[:, lo:hi]
            gg = _dot(xn, wg)
            uu = _dot(xn, wu)
            dact = 0.5 * _dot_nt(dyb, wo[lo:hi, :])
            sig = jax.nn.sigmoid(gg)
            silu = gg * sig
            act_ref[:, lo:hi] = (silu * uu).astype(BF16)
            du = (dact * silu).astype(BF16)
            dg = (dact * uu * (sig * (1.0 + gg * (1.0 - sig)))).astype(BF16)
            dgu_ref[0, :, lo:hi] = dg
            dgu_ref[1, :, lo:hi] = du
            dx_ref[...] += _dot_nt(dg, wg) + _dot_nt(du, wu)

        @pl.when(j == FF_CHUNKS - 1)
        def _():
            g = g_ref[...]
            _, xh, r = _rms_fwd(x_ref[...], g)
            dx, dgr = _rms_bwd(dx_ref[...], xh, r, g)
            dx_ref[...] = dy_ref[...] + dx
            dgain_ref[...] += _rows8(dgr)

    return pl.pallas_call(
        body, name=name, grid=(n // tm, FF_CHUNKS),
        in_specs=[
            pl.BlockSpec((tm, D_MODEL), lambda i, j: (i, 0), pipeline_mode=pl.Buffered(1)),
            pl.BlockSpec((1, D_MODEL), lambda i, j: (0, 0)),
            pl.BlockSpec((tm, D_MODEL), lambda i, j: (i, 0), pipeline_mode=pl.Buffered(1)),
            pl.BlockSpec((None, D_MODEL, FF_SHARD), lambda i, j: (j, 0, 0)),
            pl.BlockSpec((None, D_MODEL, FF_SHARD), lambda i, j: (j + FF_CHUNKS, 0, 0)),
            pl.BlockSpec((2, FF_SHARD // 2, D_MODEL), lambda i, j: (j, 0, 0)),
        ],
        out_specs=[
            pl.BlockSpec((tm, D_MODEL), lambda i, j: (i, 0), pipeline_mode=pl.Buffered(1)),
            pl.BlockSpec((tm, D_MODEL), lambda i, j: (i, 0), pipeline_mode=pl.Buffered(1)),
            pl.BlockSpec((None, 2, tm, FF_SHARD), lambda i, j: (j, 0, i, 0)),
            pl.BlockSpec((None, tm, FF_SHARD), lambda i, j: (j, i, 0)),
            pl.BlockSpec((SUBLANES, D_MODEL), lambda i, j: (0, 0)),
        ],
        out_shape=[
            jax.ShapeDtypeStruct((n, D_MODEL), F32),
            jax.ShapeDtypeStruct((n, D_MODEL), BF16),
            jax.ShapeDtypeStruct((FF_CHUNKS, 2, n, FF_SHARD), BF16),
            jax.ShapeDtypeStruct((FF_CHUNKS, n, FF_SHARD), BF16),
            jax.ShapeDtypeStruct((SUBLANES, D_MODEL), F32),
        ],
        scratch_shapes=[pltpu.VMEM((tm, D_MODEL), BF16)],
        compiler_params=_cp("arbitrary", "arbitrary"),
    )(x, gain, dy, w_in_ag, w_in_ag, w_out_ag)


def _ffn_dw_in(xn, dgu, tk, name, after=None):
    n = xn.shape[0]
    nk = n // tk
    deps = [] if after is None else [after]

    def body(a_ref, b_ref, *rest):
        o_ref, acc_ref = rest[-2:]
        k = pl.program_id(2)

        @pl.when(k == 0)
        def _():
            acc_ref[...] = jnp.zeros_like(acc_ref)

        acc_ref[...] += _dot_tn(a_ref[...], b_ref[...])

        @pl.when(k == nk - 1)
        def _():
            o_ref[...] = acc_ref[...].astype(BF16)

    return pl.pallas_call(
        body, name=name, grid=(FF_CHUNKS, 2, nk),
        in_specs=[
            pl.BlockSpec((tk, D_MODEL), lambda j, p, k: (k, 0)),
            pl.BlockSpec((None, None, tk, FF_SHARD), lambda j, p, k: (j, p, k, 0)),
        ] + [pl.BlockSpec(memory_space=pl.ANY)] * len(deps),
        out_specs=pl.BlockSpec((None, D_MODEL, FF_SHARD), lambda j, p, k: (FF_CHUNKS * p + j, 0, 0)),
        out_shape=jax.ShapeDtypeStruct((N_DEV, D_MODEL, FF_SHARD), BF16),
        scratch_shapes=[pltpu.VMEM((D_MODEL, FF_SHARD), F32)],
        compiler_params=_cp("parallel", "parallel", "arbitrary"),
    )(xn, dgu, *deps)


def _ffn_dw_out(act, dy, tk, name, after=None):
    n = act.shape[1]
    nk = n // tk
    deps = [] if after is None else [after]

    def body(a_ref, b_ref, *rest):
        o_ref, acc_ref = rest[-2:]
        k = pl.program_id(1)

        @pl.when(k == 0)
        def _():
            acc_ref[...] = jnp.zeros_like(acc_ref)

        acc_ref[...] += _dot_tn(a_ref[...], b_ref[...].astype(BF16))

        @pl.when(k == nk - 1)
        def _():
            o_ref[...] = (0.5 * acc_ref[...]).astype(BF16)

    return pl.pallas_call(
        body, name=name, grid=(FF_CHUNKS, nk),
        in_specs=[
            pl.BlockSpec((None, tk, FF_SHARD), lambda j, k: (j, k, 0)),
            pl.BlockSpec((tk, D_MODEL), lambda j, k: (k, 0)),
        ] + [pl.BlockSpec(memory_space=pl.ANY)] * len(deps),
        out_specs=pl.BlockSpec((None, FF_SHARD, D_MODEL), lambda j, k: (j, 0, 0)),
        out_shape=jax.ShapeDtypeStruct((FF_CHUNKS, FF_SHARD, D_MODEL), BF16),
        scratch_shapes=[pltpu.VMEM((FF_SHARD, D_MODEL), F32)],
        compiler_params=_cp("parallel", "arbitrary"),
    )(act, dy, *deps)


def _mix_in_fwd(x, gain, w, tm, name):
    n = x.shape[0]

    def body(x_ref, g_ref, w_ref, z_ref):
        y, _, _ = _rms_fwd(x_ref[...], g_ref[...])
        z_ref[...] = _dot(y.astype(BF16), w_ref[...])

    return pl.pallas_call(
        body, name=name, grid=(n // tm,),
        in_specs=[
            pl.BlockSpec((tm, D_MODEL), lambda i: (i, 0)),
            pl.BlockSpec((1, D_MODEL), lambda i: (0, 0)),
            pl.BlockSpec((D_MODEL, IN_COLS), lambda i: (0, 0)),
        ],
        out_specs=pl.BlockSpec((tm, IN_COLS), lambda i: (i, 0)),
        out_shape=jax.ShapeDtypeStruct((n, IN_COLS), F32),
        compiler_params=_cp("parallel"),
    )(x, gain, w)


def _mix_in_bwd(x, gain, du_ssm, du_gm, dv_gm, d_res, w, tm, name):
    n = x.shape[0]

    def body(x_ref, g_ref, d0_ref, d1_ref, d2_ref, dres_ref, w_ref, dx_ref, dw_ref, dgain_ref):
        i = pl.program_id(0)

        @pl.when(i == 0)
        def _():
            dw_ref[...] = jnp.zeros_like(dw_ref)
            dgain_ref[...] = jnp.zeros_like(dgain_ref)

        g = g_ref[...]
        y, xh, r = _rms_fwd(x_ref[...], g)
        xn = y.astype(BF16)
        dxn = jnp.zeros((tm, D_MODEL), F32)
        for k, d_ref in enumerate((d0_ref, d1_ref, d2_ref)):
            dz = d_ref[...].astype(BF16)
            cols = slice(k * SSM_WIDTH, (k + 1) * SSM_WIDTH)
            dxn += _dot_nt(dz, w_ref[:, cols])
            dw_ref[:, cols] += _dot_tn(xn, dz)
        dx, dgr = _rms_bwd(dxn, xh, r, g)
        dx_ref[...] = dres_ref[...] + dx
        dgain_ref[...] += _rows8(dgr)

    row = lambda i: (i, 0)
    fixed = lambda i: (0, 0)
    return pl.pallas_call(
        body, name=name, grid=(n // tm,),
        in_specs=[
            pl.BlockSpec((tm, D_MODEL), row),
            pl.BlockSpec((1, D_MODEL), fixed),
            pl.BlockSpec((tm, SSM_WIDTH), row),
            pl.BlockSpec((tm, GM_WIDTH), row),
            pl.BlockSpec((tm, GM_WIDTH), row),
            pl.BlockSpec((tm, D_MODEL), row),
            pl.BlockSpec((D_MODEL, IN_COLS), fixed),
        ],
        out_specs=[
            pl.BlockSpec((tm, D_MODEL), row),
            pl.BlockSpec((D_MODEL, IN_COLS), fixed),
            pl.BlockSpec((SUBLANES, D_MODEL), fixed),
        ],
        out_shape=[
            jax.ShapeDtypeStruct((n, D_MODEL), F32),
            jax.ShapeDtypeStruct((D_MODEL, IN_COLS), F32),
            jax.ShapeDtypeStruct((SUBLANES, D_MODEL), F32),
        ],
        compiler_params=_cp("arbitrary"),
    )(x, gain, du_ssm, du_gm, dv_gm, d_res, w)


def _mix_out_fwd(y_ssm, y_gm, g_ssm, g_gm, w, x, tm, name):
    n = x.shape[0]

    def body(ys_ref, yg_ref, gs_ref, gg_ref, w_ref, x_ref, o_ref):
        a, _, _ = _rms_fwd(ys_ref[...], gs_ref[...])
        b, _, _ = _rms_fwd(yg_ref[...], gg_ref[...])
        o_ref[...] = (x_ref[...] + _dot(a.astype(BF16), w_ref[0:SSM_WIDTH, :])
                      + _dot(b.astype(BF16), w_ref[SSM_WIDTH:D_MODEL, :]))

    row = lambda i: (i, 0)
    fixed = lambda i: (0, 0)
    return pl.pallas_call(
        body, name=name, grid=(n // tm,),
        in_specs=[
            pl.BlockSpec((tm, SSM_WIDTH), row), pl.BlockSpec((tm, GM_WIDTH), row),
            pl.BlockSpec((1, SSM_WIDTH), fixed), pl.BlockSpec((1, GM_WIDTH), fixed),
            pl.BlockSpec((D_MODEL, D_MODEL), fixed), pl.BlockSpec((tm, D_MODEL), row),
        ],
        out_specs=pl.BlockSpec((tm, D_MODEL), row),
        out_shape=jax.ShapeDtypeStruct((n, D_MODEL), F32),
        compiler_params=_cp("parallel"),
    )(y_ssm, y_gm, g_ssm, g_gm, w, x)


def _mix_out_bwd(y_ssm, y_gm, g_ssm, g_gm, w, dx, tm, name):
    n = dx.shape[0]

    def body(ys_ref, yg_ref, gs_ref, gg_ref, w_ref, dx_ref, dys_ref, dyg_ref, dw_ref, dgs_ref, dgg_ref):
        i = pl.program_id(0)

        @pl.when(i == 0)
        def _():
            dw_ref[...] = jnp.zeros_like(dw_ref)
            dgs_ref[...] = jnp.zeros_like(dgs_ref)
            dgg_ref[...] = jnp.zeros_like(dgg_ref)

        dxb = dx_ref[...].astype(BF16)
        parts = ((ys_ref, gs_ref, dys_ref, dgs_ref, 0), (yg_ref, gg_ref, dyg_ref, dgg_ref, SSM_WIDTH))
        for y_ref, g_ref, dy_ref, dg_ref, off in parts:
            g = g_ref[...]
            yn, xh, r = _rms_fwd(y_ref[...], g)
            rows = slice(off, off + SSM_WIDTH)
            dyn = _dot_nt(dxb, w_ref[rows, :])
            dw_ref[rows, :] += _dot_tn(yn.astype(BF16), dxb)
            dy, dgr = _rms_bwd(dyn, xh, r, g)
            dy_ref[...] = dy
            dg_ref[...] += _rows8(dgr)

    row = lambda i: (i, 0)
    fixed = lambda i: (0, 0)
    return pl.pallas_call(
        body, name=name, grid=(n // tm,),
        in_specs=[
            pl.BlockSpec((tm, SSM_WIDTH), row), pl.BlockSpec((tm, GM_WIDTH), row),
            pl.BlockSpec((1, SSM_WIDTH), fixed), pl.BlockSpec((1, GM_WIDTH), fixed),
            pl.BlockSpec((D_MODEL, D_MODEL), fixed), pl.BlockSpec((tm, D_MODEL), row),
        ],
        out_specs=[
            pl.BlockSpec((tm, SSM_WIDTH), row), pl.BlockSpec((tm, GM_WIDTH), row),
            pl.BlockSpec((D_MODEL, D_MODEL), fixed),
            pl.BlockSpec((SUBLANES, SSM_WIDTH), fixed), pl.BlockSpec((SUBLANES, GM_WIDTH), fixed),
        ],
        out_shape=[
            jax.ShapeDtypeStruct((n, SSM_WIDTH), F32), jax.ShapeDtypeStruct((n, GM_WIDTH), F32),
            jax.ShapeDtypeStruct((D_MODEL, D_MODEL), F32),
            jax.ShapeDtypeStruct((SUBLANES, SSM_WIDTH), F32), jax.ShapeDtypeStruct((SUBLANES, GM_WIDTH), F32),
        ],
        compiler_params=_cp("arbitrary"),
    )(y_ssm, y_gm, g_ssm, g_gm, w, dx)


SCAN_W = 512
SCAN_PIECES = HALF_ST // SCAN_W


def _scan_tiles(src_ref, dst_ref, dst_off, c_ref, half, carry_ref, n_tiles, reverse, extra=None):
    shifts = (1, 2, 4)
    carry_row = 0 if reverse else SUBLANES - 1

    def cols(piece, im):
        lo = im * HALF_ST + piece * SCAN_W
        return slice(lo, lo + SCAN_W)

    def step(t, state):
        carries, accs = state
        k = (n_tiles - 1 - t) if reverse else t
        rows = slice(k * SUBLANES, (k + 1) * SUBLANES)
        new_carries, new_accs = [], []
        for piece in range(SCAN_PIECES):
            cr, ci = carries[piece]
            xr0 = src_ref[rows, cols(piece, 0)]
            xi0 = src_ref[rows, cols(piece, 1)]
            xr, xi = xr0, xi0
            for si, s in enumerate(shifts):
                ar = c_ref[half, si, :, cols(piece, 0)]
                ai = c_ref[half, si, :, cols(piece, 1)]
                sh = (SUBLANES - s) if reverse else s
                sr = pltpu.roll(xr, sh, 0)
                sm = pltpu.roll(xi, sh, 0)
                xr, xi = xr + (ar * sr - ai * sm), xi + (ar * sm + ai * sr)
            pr = c_ref[half, 3, :, cols(piece, 0)]
            pi = c_ref[half, 3, :, cols(piece, 1)]
            hr = xr + (pr * cr - pi * ci)
            hi = xi + (pr * ci + pi * cr)
            dst_ref[rows, pl.ds(dst_off + piece * SCAN_W, SCAN_W)] = hr
            dst_ref[rows, pl.ds(dst_off + HALF_ST + piece * SCAN_W, SCAN_W)] = hi
            new_carries.append((jnp.broadcast_to(hr[carry_row:carry_row + 1, :], (SUBLANES, SCAN_W)),
                                jnp.broadcast_to(hi[carry_row:carry_row + 1, :], (SUBLANES, SCAN_W))))
            if extra is not None:
                new_accs.append(extra(rows, piece, (xr0, xi0), (hr, hi), accs[piece]))
        return tuple(new_carries), tuple(new_accs)

    base = half * 2 * HALF_ST
    carries0 = tuple((carry_ref[:, pl.ds(base + p * SCAN_W, SCAN_W)],
                      carry_ref[:, pl.ds(base + HALF_ST + p * SCAN_W, SCAN_W)]) for p in range(SCAN_PIECES))
    zero = jnp.zeros((SUBLANES, SCAN_W), F32)
    accs0 = tuple((zero, zero) for _ in range(SCAN_PIECES)) if extra is not None else ()
    state = (carries0, accs0)
    for t in range(n_tiles):
        state = step(t, state)
    carries, accs = state
    for p in range(SCAN_PIECES):
        carry_ref[:, pl.ds(base + p * SCAN_W, SCAN_W)] = carries[p][0]
        carry_ref[:, pl.ds(base + HALF_ST + p * SCAN_W, SCAN_W)] = carries[p][1]
    return accs


def _ssm_tail(hb, u, c_ref, glu_ref, glub_ref, dskip_ref):
    ypre = u * dskip_ref[...]
    parts = []
    for half in range(2):
        parts.append(_dot(hb[half], c_ref[half]))
    ypre = ypre + jnp.concatenate(parts, axis=1)
    yg, th = _gelu(ypre)
    zz = _dot(yg.astype(BF16), glu_ref[...]) + glub_ref[...]
    z1, z2 = zz[:, :SSM_WIDTH], zz[:, SSM_WIDTH:]
    sg = jax.nn.sigmoid(z2)
    return ypre, th, yg, z1, sg


def _ssm_fwd(z, bblk, cblk, glu, glub, dskip, fwdc, batch, t_chunk, name):
    n = z.shape[0]
    nk = n // batch // t_chunk
    n_tiles = t_chunk // SUBLANES

    def body(u_ref, b_ref, c_ref, glu_ref, glub_ref, dskip_ref, k_ref, y_ref, h_ref, bu_ref, carry_ref):
        @pl.when(pl.program_id(1) == 0)
        def _():
            carry_ref[...] = jnp.zeros_like(carry_ref)

        u = u_ref[...]
        ub = u.astype(BF16)
        for half in range(2):
            bu_ref[half] = _dot(ub[:, half * HALF_IN:(half + 1) * HALF_IN], b_ref[half])
            _scan_tiles(bu_ref.at[half], h_ref, half * 2 * HALF_ST, k_ref, half, carry_ref, n_tiles, False)
        hb = [h_ref[:, half * 2 * HALF_ST:(half + 1) * 2 * HALF_ST].astype(BF16) for half in range(2)]
        _, _, _, z1, sg = _ssm_tail(hb, u, c_ref, glu_ref, glub_ref, dskip_ref)
        y_ref[...] = z1 * sg

    fixed2 = lambda b, k: (0, 0)
    fixed3 = lambda b, k: (0, 0, 0)
    row = lambda b, k: (b * nk + k, 0)
    return pl.pallas_call(
        body, name=name, grid=(batch, nk),
        in_specs=[
            pl.BlockSpec((t_chunk, SSM_WIDTH), row),
            pl.BlockSpec((2, HALF_IN, 2 * HALF_ST), fixed3),
            pl.BlockSpec((2, 2 * HALF_ST, HALF_IN), fixed3),
            pl.BlockSpec((SSM_WIDTH, 2 * SSM_WIDTH), fixed2),
            pl.BlockSpec((1, 2 * SSM_WIDTH), fixed2),
            pl.BlockSpec((1, SSM_WIDTH), fixed2),
            pl.BlockSpec((2, 4, SUBLANES, 2 * HALF_ST), lambda b, k: (0, 0, 0, 0)),
        ],
        out_specs=[pl.BlockSpec((t_chunk, SSM_WIDTH), row), pl.BlockSpec((t_chunk, 4 * HALF_ST), row)],
        out_shape=[jax.ShapeDtypeStruct((n, SSM_WIDTH), F32), jax.ShapeDtypeStruct((n, 4 * HALF_ST), F32)],
        scratch_shapes=[pltpu.VMEM((2, t_chunk, 2 * HALF_ST), F32), pltpu.VMEM((SUBLANES, 4 * HALF_ST), F32)],
        compiler_params=_cp("parallel", "arbitrary"),
    )(z, bblk, cblk, glu, glub, dskip, fwdc)


def _ssm_bwd(z, h, dy, bblk, cblk, glu, glub, dskip, revc, batch, t_chunk, name):
    n = z.shape[0]
    nk = n // batch // t_chunk
    n_tiles = t_chunk // SUBLANES

    def body(u_ref, h_ref, dy_ref, b_ref, c_ref, glu_ref, glub_ref, dskip_ref, k_ref,
             du_ref, dglu_ref, dglub_ref, ddskip_ref, dct_ref, db_ref, q_ref, g_ref, carry_ref):
        first = jnp.logical_and(pl.program_id(0) == 0, pl.program_id(1) == 0)

        @pl.when(first)
        def _():
            for r in (dglu_ref, dglub_ref, ddskip_ref, dct_ref, db_ref, q_ref):
                r[...] = jnp.zeros_like(r)

        @pl.when(pl.program_id(1) == 0)
        def _():
            carry_ref[...] = jnp.zeros_like(carry_ref)

        u = u_ref[...]
        ub = u.astype(BF16)
        hb = [h_ref[:, half * 2 * HALF_ST:(half + 1) * 2 * HALF_ST].astype(BF16) for half in range(2)]
        ypre, th, yg, z1, sg = _ssm_tail(hb, u, c_ref, glu_ref, glub_ref, dskip_ref)
        dout = dy_ref[...]
        dz = jnp.concatenate([dout * sg, dout * z1 * sg * (1.0 - sg)], axis=1)
        dzb = dz.astype(BF16)
        dglu_ref[...] += _dot_tn(yg.astype(BF16), dzb)
        dglub_ref[...] += _rows8(dz)
        dypre = _dot_nt(dzb, glu_ref[...]) * _gelu_grad(ypre, th)
        ddskip_ref[...] += _rows8(dypre * u)
        dypb = dypre.astype(BF16)
        du_parts = []
        for half in range(2):
            dyp_h = dypb[:, half * HALF_IN:(half + 1) * HALF_IN]
            dct_ref[half] += _dot_tn(dyp_h, hb[half])
            g_ref[half] = _dot_nt(dyp_h, c_ref[half])

            def extra(rows, piece, x_in, g_out, acc, half=half):
                er, ei = g_out[0] - x_in[0], g_out[1] - x_in[1]
                base = half * 2 * HALF_ST + piece * SCAN_W
                hr = h_ref[rows, pl.ds(base, SCAN_W)]
                hi = h_ref[rows, pl.ds(base + HALF_ST, SCAN_W)]
                return acc[0] + (er * hr + ei * hi), acc[1] + (er * hi - ei * hr)

            accs = _scan_tiles(g_ref.at[half], g_ref.at[half], 0, k_ref, half, carry_ref, n_tiles, True, extra)
            for piece in range(SCAN_PIECES):
                base = half * 2 * HALF_ST + piece * SCAN_W
                q_ref[:, pl.ds(base, SCAN_W)] += accs[piece][0]
                q_ref[:, pl.ds(base + HALF_ST, SCAN_W)] += accs[piece][1]
            gb = g_ref[half].astype(BF16)
            db_ref[half] += _dot_tn(ub[:, half * HALF_IN:(half + 1) * HALF_IN], gb)
            du_parts.append(_dot_nt(gb, b_ref[half]))
        du_ref[...] = dypre * dskip_ref[...] + jnp.concatenate(du_parts, axis=1)

    fixed2 = lambda b, k: (0, 0)
    fixed3 = lambda b, k: (0, 0, 0)
    row = lambda b, k: (b * nk + (nk - 1 - k), 0)
    return pl.pallas_call(
        body, name=name, grid=(batch, nk),
        in_specs=[
            pl.BlockSpec((t_chunk, SSM_WIDTH), row),
            pl.BlockSpec((t_chunk, 4 * HALF_ST), row),
            pl.BlockSpec((t_chunk, SSM_WIDTH), row),
            pl.BlockSpec((2, HALF_IN, 2 * HALF_ST), fixed3),
            pl.BlockSpec((2, 2 * HALF_ST, HALF_IN), fixed3),
            pl.BlockSpec((SSM_WIDTH, 2 * SSM_WIDTH), fixed2),
            pl.BlockSpec((1, 2 * SSM_WIDTH), fixed2),
            pl.BlockSpec((1, SSM_WIDTH), fixed2),
            pl.BlockSpec((2, 4, SUBLANES, 2 * HALF_ST), lambda b, k: (0, 0, 0, 0)),
        ],
        out_specs=[
            pl.BlockSpec((t_chunk, SSM_WIDTH), row),
            pl.BlockSpec((SSM_WIDTH, 2 * SSM_WIDTH), fixed2),
            pl.BlockSpec((SUBLANES, 2 * SSM_WIDTH), fixed2),
            pl.BlockSpec((SUBLANES, SSM_WIDTH), fixed2),
            pl.BlockSpec((2, HALF_IN, 2 * HALF_ST), fixed3),
            pl.BlockSpec((2, HALF_IN, 2 * HALF_ST), fixed3),
            pl.BlockSpec((SUBLANES, 4 * HALF_ST), fixed2),
        ],
        out_shape=[
            jax.ShapeDtypeStruct((n, SSM_WIDTH), F32),
            jax.ShapeDtypeStruct((SSM_WIDTH, 2 * SSM_WIDTH), F32),
            jax.ShapeDtypeStruct((SUBLANES, 2 * SSM_WIDTH), F32),
            jax.ShapeDtypeStruct((SUBLANES, SSM_WIDTH), F32),
            jax.ShapeDtypeStruct((2, HALF_IN, 2 * HALF_ST), F32),
            jax.ShapeDtypeStruct((2, HALF_IN, 2 * HALF_ST), F32),
            jax.ShapeDtypeStruct((SUBLANES, 4 * HALF_ST), F32),
        ],
        scratch_shapes=[pltpu.VMEM((2, t_chunk, 2 * HALF_ST), F32), pltpu.VMEM((SUBLANES, 4 * HALF_ST), F32)],
        compiler_params=_cp("arbitrary", "arbitrary"),
    )(z, h, dy, bblk, cblk, glu, glub, dskip, revc)


def _gm_chunk_fwd(u, v, gain_ref, w_ref, bias_ref):
    ug, thu = _gelu(u)
    vg, thv = _gelu(v)
    rs, vns, ss = [], [], []
    for hh in range(GM_HEADS):
        cs = slice(hh * GM_HEAD_DIM, (hh + 1) * GM_HEAD_DIM)
        vn, _, r = _rms_fwd(vg[:, cs], gain_ref[:, cs])
        s = _dot(w_ref[hh], vn.astype(BF16)) + bias_ref[:, cs]
        rs.append(r)
        vns.append(vn)
        ss.append(s)
    return ug, thu, thv, vg, rs, vns, ss


def _gm_fwd(z, gain, w_tril, bias, rows, name):
    n = z.shape[0]
    chunks = rows // GM_CHUNK

    def body(u_ref, v_ref, gain_ref, w_ref, bias_ref, y_ref):
        for c in range(chunks):
            rs_ = slice(c * GM_CHUNK, (c + 1) * GM_CHUNK)
            ug, _, _, _, _, _, ss = _gm_chunk_fwd(u_ref[rs_, :], v_ref[rs_, :], gain_ref, w_ref, bias_ref)
            y_ref[rs_, :] = ug * jnp.concatenate(ss, axis=1)

    return pl.pallas_call(
        body, name=name, grid=(n // rows,),
        in_specs=[
            pl.BlockSpec((rows, GM_WIDTH), lambda i: (i, 1)),
            pl.BlockSpec((rows, GM_WIDTH), lambda i: (i, 2)),
            pl.BlockSpec((1, GM_WIDTH), lambda i: (0, 0)),
            pl.BlockSpec((GM_HEADS, GM_CHUNK, GM_CHUNK), lambda i: (0, 0, 0)),
            pl.BlockSpec((GM_CHUNK, GM_WIDTH), lambda i: (0, 0)),
        ],
        out_specs=pl.BlockSpec((rows, GM_WIDTH), lambda i: (i, 0)),
        out_shape=jax.ShapeDtypeStruct((n, GM_WIDTH), F32),
        compiler_params=_cp("parallel"),
    )(z, z, gain, w_tril, bias)


def _gm_bwd(z, dy, gain, w_tril, bias, rows, name):
    n = z.shape[0]
    chunks = rows // GM_CHUNK

    def body(u_ref, v_ref, dy_ref, gain_ref, w_ref, bias_ref, du_ref, dv_ref, dw_ref, dbias_ref, dgain_ref):
        @pl.when(pl.program_id(0) == 0)
        def _():
            dw_ref[...] = jnp.zeros_like(dw_ref)
            dbias_ref[...] = jnp.zeros_like(dbias_ref)
            dgain_ref[...] = jnp.zeros_like(dgain_ref)

        for c in range(chunks):
            rs_ = slice(c * GM_CHUNK, (c + 1) * GM_CHUNK)
            u, v = u_ref[rs_, :], v_ref[rs_, :]
            ug, thu, thv, vg, rs, vns, ss = _gm_chunk_fwd(u, v, gain_ref, w_ref, bias_ref)
            dout = dy_ref[rs_, :]
            ds = dout * ug
            du_ref[rs_, :] = dout * jnp.concatenate(ss, axis=1) * _gelu_grad(u, thu)
            dbias_ref[...] += ds
            dvg_parts, dgain_parts = [], []
            for hh in range(GM_HEADS):
                cs = slice(hh * GM_HEAD_DIM, (hh + 1) * GM_HEAD_DIM)
                dsb = ds[:, cs].astype(BF16)
                dvn = _dot_tn(w_ref[hh], dsb)
                dw_ref[hh] += _dot_nt(dsb, vns[hh].astype(BF16))
                g = gain_ref[:, cs]
                xh = vg[:, cs] * rs[hh]
                dvg, dgr = _rms_bwd(dvn, xh, rs[hh], g)
                dvg_parts.append(dvg)
                dgain_parts.append(dgr)
            dv_ref[rs_, :] = jnp.concatenate(dvg_parts, axis=1) * _gelu_grad(v, thv)
            dgain_ref[...] += _rows8(jnp.concatenate(dgain_parts, axis=1))

    row = lambda i: (i, 0)
    return pl.pallas_call(
        body, name=name, grid=(n // rows,),
        in_specs=[
            pl.BlockSpec((rows, GM_WIDTH), lambda i: (i, 1)),
            pl.BlockSpec((rows, GM_WIDTH), lambda i: (i, 2)),
            pl.BlockSpec((rows, GM_WIDTH), row),
            pl.BlockSpec((1, GM_WIDTH), lambda i: (0, 0)),
            pl.BlockSpec((GM_HEADS, GM_CHUNK, GM_CHUNK), lambda i: (0, 0, 0)),
            pl.BlockSpec((GM_CHUNK, GM_WIDTH), lambda i: (0, 0)),
        ],
        out_specs=[
            pl.BlockSpec((rows, GM_WIDTH), row), pl.BlockSpec((rows, GM_WIDTH), row),
            pl.BlockSpec((GM_HEADS, GM_CHUNK, GM_CHUNK), lambda i: (0, 0, 0)),
            pl.BlockSpec((GM_CHUNK, GM_WIDTH), lambda i: (0, 0)),
            pl.BlockSpec((SUBLANES, GM_WIDTH), lambda i: (0, 0)),
        ],
        out_shape=[
            jax.ShapeDtypeStruct((n, GM_WIDTH), F32), jax.ShapeDtypeStruct((n, GM_WIDTH), F32),
            jax.ShapeDtypeStruct((GM_HEADS, GM_CHUNK, GM_CHUNK), F32),
            jax.ShapeDtypeStruct((GM_CHUNK, GM_WIDTH), F32),
            jax.ShapeDtypeStruct((SUBLANES, GM_WIDTH), F32),
        ],
        compiler_params=_cp("arbitrary"),
    )(z, z, dy, gain, w_tril, bias)


def _loss_head(x, gain, target, tm, name):
    n = x.shape[0]

    def body(x_ref, g_ref, t_ref, dx_ref, sq_ref, dgain_ref):
        @pl.when(pl.program_id(0) == 0)
        def _():
            sq_ref[...] = jnp.zeros_like(sq_ref)
            dgain_ref[...] = jnp.zeros_like(dgain_ref)

        g = g_ref[...]
        y, xh, r = _rms_fwd(x_ref[...], g)
        err = y - t_ref[...]
        sq_ref[...] += _rows8(err * err)
        dx, dgr = _rms_bwd(err * (1.0 / D_MODEL), xh, r, g)
        dx_ref[...] = dx
        dgain_ref[...] += _rows8(dgr)

    row = lambda i: (i, 0)
    fixed = lambda i: (0, 0)
    return pl.pallas_call(
        body, name=name, grid=(n // tm,),
        in_specs=[pl.BlockSpec((tm, D_MODEL), row), pl.BlockSpec((1, D_MODEL), fixed), pl.BlockSpec((tm, D_MODEL), row)],
        out_specs=[pl.BlockSpec((tm, D_MODEL), row), pl.BlockSpec((SUBLANES, D_MODEL), fixed),
                   pl.BlockSpec((SUBLANES, D_MODEL), fixed)],
        out_shape=[jax.ShapeDtypeStruct((n, D_MODEL), F32), jax.ShapeDtypeStruct((SUBLANES, D_MODEL), F32),
                   jax.ShapeDtypeStruct((SUBLANES, D_MODEL), F32)],
        compiler_params=_cp("arbitrary"),
    )(x, gain, target)


def _adam_math(w, g, m, v):
    m2 = ADAM_B1 * m + (1.0 - ADAM_B1) * g
    v2 = ADAM_B2 * v + (1.0 - ADAM_B2) * (g * g)
    m_hat = m2 / (1.0 - ADAM_B1 ** ADAM_STEP)
    v_hat = v2 / (1.0 - ADAM_B2 ** ADAM_STEP)
    delta = -ADAM_LR * (m_hat / (jnp.sqrt(v_hat) + ADAM_EPS) + ADAM_WD * w)
    return delta, m2, v2


def _adam_sharded(parts, w, m, v, layer, earlier, name):
    depth, r, c = w.shape
    tr = max(t for t in range(16, 129, 16) if r % t == 0)

    def body(p_ref, w_ref, m_ref, v_ref, *rest):
        g_ref, d_ref, m2_ref, v2_ref = rest[-4:]
        g = p_ref[0].astype(F32)
        for s in range(1, N_DEV):
            g = g + p_ref[s].astype(F32)
        delta, m2, v2 = _adam_math(w_ref[...], g, m_ref[...], v_ref[...])
        g_ref[...] = g
        d_ref[...] = delta
        m2_ref[...] = m2
        v2_ref[...] = v2

    blk = pl.BlockSpec((None, tr, c), lambda i: (layer, i, 0))
    extra = [] if earlier is None else list(earlier)
    return pl.pallas_call(
        body, name=name, grid=(r // tr,),
        in_specs=[pl.BlockSpec((N_DEV, tr, c), lambda i: (0, i, 0)), blk, blk, blk]
        + [pl.BlockSpec(memory_space=pl.ANY)] * len(extra),
        out_specs=[blk, blk, blk, blk],
        out_shape=[jax.ShapeDtypeStruct((depth, r, c), F32)] * 4,
        input_output_aliases={4 + i: i for i in range(len(extra))},
        compiler_params=_cp("parallel"),
    )(parts, w, m, v, *extra)


def _adam_packed(g, w, m, v, name):
    r, c = g.shape

    def body(g_ref, w_ref, m_ref, v_ref, d_ref, m2_ref, v2_ref):
        delta, m2, v2 = _adam_math(w_ref[...], g_ref[...], m_ref[...], v_ref[...])
        d_ref[...] = delta
        m2_ref[...] = m2
        v2_ref[...] = v2

    blk = pl.BlockSpec((r, c), lambda i: (0, 0))
    return pl.pallas_call(
        body, name=name, grid=(1,),
        in_specs=[blk, blk, blk, blk], out_specs=[blk, blk, blk],
        out_shape=[jax.ShapeDtypeStruct((r, c), F32)] * 3,
        compiler_params=_cp("arbitrary"),
    )(g, w, m, v)


def _my_place():
    return lax.axis_index("x"), lax.axis_index("y"), lax.axis_index("c")


def _flip(place, rel):
    x, y, c = place
    return (1 - x if rel & 4 else x, 1 - y if rel & 2 else y, 1 - c if rel & 1 else c)


def _index(place):
    return 4 * place[0] + 2 * place[1] + place[2]


def _all_gather(shards, name):
    na = len(shards)

    def body(*refs):
        xs, outs = refs[:na], refs[na:2 * na]
        send_sems, recv_sems, local_sems = refs[2 * na:]
        me = _my_place()
        sibling = _flip(me, 1)
        chips = [_flip(me, 4), _flip(me, 2), _flip(me, 6)]

        def copy(a, k, block, to, src=None):
            slot = outs[a].at[_index(block)]
            return pltpu.make_async_remote_copy(
                src_ref=slot if src is None else src, dst_ref=slot,
                send_sem=send_sems.at[a, k], recv_sem=recv_sems.at[a, k],
                device_id=to, device_id_type=MESH)

        mine = [pltpu.make_async_copy(xs[a], outs[a].at[_index(me)], local_sems.at[a]) for a in range(na)]
        for cp in mine:
            cp.start()
        first = []
        for a in range(na):
            first.append(copy(a, 0, me, sibling, src=xs[a]))
            first += [copy(a, 1 + j, me, chip, src=xs[a]) for j, chip in enumerate(chips)]
        for cp in first:
            cp.start()
        passed = []
        for a in range(na):
            for j, chip in enumerate(chips):
                copy(a, 1 + j, chip, me).wait_recv()
                fwd = copy(a, 4 + j, chip, sibling)
                fwd.start()
                passed.append(fwd)
        for a in range(na):
            copy(a, 0, sibling, me).wait_recv()
            for j, chip in enumerate(chips):
                copy(a, 4 + j, _flip(chip, 1), me).wait_recv()
        for cp in first + passed:
            cp.wait_send()
        for cp in mine:
            cp.wait()

    hbm = pl.BlockSpec(memory_space=pl.ANY)
    return pl.pallas_call(
        body, name=name,
        in_specs=[hbm] * na, out_specs=[hbm] * na,
        out_shape=[jax.ShapeDtypeStruct((N_DEV,) + s.shape, s.dtype) for s in shards],
        scratch_shapes=[pltpu.SemaphoreType.DMA((na, 7)), pltpu.SemaphoreType.DMA((na, 7)),
                        pltpu.SemaphoreType.DMA((na,))],
    )(*shards)


_HBM = pl.BlockSpec(memory_space=pltpu.HBM)
_SEM = pl.BlockSpec(memory_space=pltpu.SEMAPHORE)
_EFFECT = pltpu.SideEffectType.DATAFLOW_SIDE_EFFECTING


def _exchange_copy(src_ref, land_ref, send_sems, recv_sems, a, rel, me, scatter, landed):
    peer = _flip(me, rel)
    src = src_ref.at[_index(peer)] if scatter else src_ref
    return pltpu.make_async_remote_copy(
        src_ref=src, dst_ref=land_ref.at[_index(peer if landed else me)],
        send_sem=send_sems.at[a * (N_DEV - 1) + rel - 1], recv_sem=recv_sems.at[a * (N_DEV - 1) + rel - 1],
        device_id=peer, device_id_type=MESH)


def _own_slot(data, me, scatter):
    if scatter:
        own = lax.dynamic_slice_in_dim(data, me, 1, axis=0)
        shape = data.shape
    else:
        own = data[None]
        shape = (N_DEV,) + data.shape
    start = (me,) + (0,) * (len(shape) - 1)
    return lax.dynamic_update_slice(lax.empty(shape, data.dtype), own, start)


def _exchange_start(groups, me, scatter, name, after=None):
    sizes = [len(g) for g in groups]
    srcs = [a for g in groups for a in g]
    lands = [_own_slot(a, me, scatter) for a in srcs]
    na, ng = len(srcs), len(groups)
    deps = [] if after is None else [after]

    def body(*refs):
        src_refs, land_refs = refs[:na], refs[na:2 * na]
        sems = refs[2 * na + len(deps):2 * na + len(deps) + 2 * ng]
        token = refs[-1]
        place = _my_place()
        a = 0
        for g, size in enumerate(sizes):
            for k in range(size):
                for rel in range(1, N_DEV):
                    _exchange_copy(src_refs[a], land_refs[a], sems[2 * g], sems[2 * g + 1], k, rel, place, scatter,
                                   False).start()
                a += 1
        token[...] = jnp.zeros_like(token)

    sem_shapes = [pltpu.SemaphoreType.DMA((size * (N_DEV - 1),)) for size in sizes for _ in range(2)]
    outs = pl.pallas_call(
        body, name=name,
        in_specs=[_HBM] * (2 * na) + [pl.BlockSpec(memory_space=pl.ANY)] * len(deps),
        out_specs=[_SEM] * (2 * ng) + [_HBM] * (2 * na) + [pl.BlockSpec(memory_space=pltpu.VMEM)],
        out_shape=sem_shapes + [pltpu.HBM(a.shape, a.dtype) for a in srcs + lands]
        + [jax.ShapeDtypeStruct((SUBLANES, LANES), F32)],
        input_output_aliases={i: 2 * ng + i for i in range(2 * na)},
        compiler_params=pltpu.CompilerParams(has_side_effects=_EFFECT),
    )(*[pltpu.with_memory_space_constraint(a, pltpu.HBM) for a in srcs + lands], *deps)
    sems, thru, token = outs[:2 * ng], outs[2 * ng:2 * ng + 2 * na], outs[-1]
    handles, a = [], 0
    for g, size in enumerate(sizes):
        handles.append((sems[2 * g], sems[2 * g + 1], thru[a:a + size], thru[na + a:na + a + size]))
        a += size
    return handles, token


def _exchange_wait(handle, after, scatter, name):
    send_sems, recv_sems, srcs, lands = handle
    na = len(srcs)

    def body(*refs):
        src_refs, land_refs = refs[:na], refs[na:2 * na]
        send_ref, recv_ref = refs[2 * na], refs[2 * na + 1]
        place = _my_place()
        for a in range(na):
            for rel in range(1, N_DEV):
                cp = _exchange_copy(src_refs[a], land_refs[a], send_ref, recv_ref, a, rel, place, scatter, True)
                cp.wait_send()
                cp.wait_recv()

    outs = pl.pallas_call(
        body, name=name,
        in_specs=[_HBM] * (2 * na) + [_SEM, _SEM, pl.BlockSpec(memory_space=pl.ANY)],
        out_specs=[_HBM] * (2 * na),
        out_shape=[pltpu.HBM(a.shape, a.dtype) for a in list(srcs) + list(lands)],
        input_output_aliases={i: i for i in range(2 * na)},
        compiler_params=pltpu.CompilerParams(has_side_effects=_EFFECT),
    )(*srcs, *lands, send_sems, recv_sems, after)
    return outs[na:]


def _behind(arr, token):
    return arr + token[0:1, 0:1]


def _all_reduce_small(g, name):
    _, r, c = g.shape

    def body(g_ref, o_ref, land_ref, red_ref, send1, recv1, send2, recv2):
        me = _my_place()
        idx = _index(me)

        def scatter(rel):
            peer = _flip(me, rel)
            return pltpu.make_async_remote_copy(
                src_ref=g_ref.at[_index(peer)], dst_ref=land_ref.at[idx],
                send_sem=send1.at[rel - 1], recv_sem=recv1.at[rel - 1], device_id=peer, device_id_type=MESH)

        def gather(rel):
            peer = _flip(me, rel)
            return pltpu.make_async_remote_copy(
                src_ref=red_ref, dst_ref=o_ref.at[idx],
                send_sem=send2.at[rel - 1], recv_sem=recv2.at[rel - 1], device_id=peer, device_id_type=MESH)

        for rel in range(1, N_DEV):
            scatter(rel).start()
        land_ref[idx] = g_ref[idx]
        for rel in range(1, N_DEV):
            scatter(rel).wait()
        acc = land_ref[0]
        for s in range(1, N_DEV):
            acc = acc + land_ref[s]
        red_ref[...] = acc
        for rel in range(1, N_DEV):
            gather(rel).start()
        o_ref[idx] = acc
        for rel in range(1, N_DEV):
            gather(rel).wait()

    vmem = pl.BlockSpec(memory_space=pltpu.VMEM)
    return pl.pallas_call(
        body, name=name,
        in_specs=[vmem], out_specs=vmem,
        out_shape=jax.ShapeDtypeStruct(g.shape, F32),
        scratch_shapes=[pltpu.VMEM(g.shape, F32), pltpu.VMEM((r, c), F32)]
        + [pltpu.SemaphoreType.DMA((N_DEV - 1,))] * 4,
        compiler_params=pltpu.CompilerParams(vmem_limit_bytes=VMEM_LIMIT),
    )(g)


def _ssm_discretize(a_re, a_im, log_dt, b_re, b_im):
    dt = jnp.exp(log_dt)[:, None]
    mag = jnp.exp(a_re * dt)
    lr, li = mag * jnp.cos(a_im * dt), mag * jnp.sin(a_im * dt)
    den = a_re * a_re + a_im * a_im
    qr = ((lr - 1.0) * a_re + li * a_im) / den
    qi = (li * a_re - (lr - 1.0) * a_im) / den
    bbr = qr[..., None] * b_re - qi[..., None] * b_im
    bbi = qr[..., None] * b_im + qi[..., None] * b_re
    return lr, li, bbr, bbi


def _halves(a):
    return a.reshape((2, HALF_GROUPS) + a.shape[1:])


def _block_diag(blocks):
    g, r, c = blocks.shape
    eye = jnp.eye(g, dtype=blocks.dtype)
    return jnp.einsum("grc,gh->grhc", blocks, eye).reshape(g * r, g * c)


def _block_diag_take(dense, g, r, c):
    return jnp.einsum("grhc,gh->grc", dense.reshape(g, r, g, c), jnp.eye(g, dtype=dense.dtype))


def _ssm_matrices(bbr, bbi, c_re, c_im, glu_w, glu_b, d_skip):
    bre, bim = _halves(jnp.swapaxes(bbr, 1, 2)), _halves(jnp.swapaxes(bbi, 1, 2))
    bblk = jnp.stack([jnp.concatenate([_block_diag(bre[h]), _block_diag(bim[h])], axis=1) for h in range(2)])
    cre, cim = _halves(jnp.swapaxes(c_re, 1, 2)), _halves(jnp.swapaxes(c_im, 1, 2))
    cblk = jnp.stack([jnp.concatenate([_block_diag(cre[h]), -_block_diag(cim[h])], axis=0) for h in range(2)])
    glu = jnp.concatenate([_block_diag(glu_w[:, :, :SSM_CH]), _block_diag(glu_w[:, :, SSM_CH:])], axis=1)
    glub = jnp.concatenate([glu_b[:, :SSM_CH].reshape(1, -1), glu_b[:, SSM_CH:].reshape(1, -1)], axis=1)
    return bblk.astype(BF16), cblk.astype(BF16), glu.astype(BF16), glub, d_skip.reshape(1, -1)


def _scan_constants(lr, li, reverse):
    if reverse:
        li = -li
    pows = [(lr, li)]
    for _ in range(SUBLANES - 1):
        pr, pi = pows[-1]
        pows.append((pr * lr - pi * li, pr * li + pi * lr))
    row = jnp.arange(SUBLANES)[:, None]

    def flat(a):
        return a.reshape(2, 1, HALF_ST)

    mats = []
    for s in (1, 2, 4):
        keep = (row + s <= SUBLANES - 1) if reverse else (row >= s)
        mats.append(tuple(jnp.where(keep[None], flat(p), 0.0) for p in pows[s - 1]))
    order = [SUBLANES - 1 - j for j in range(SUBLANES)] if reverse else list(range(SUBLANES))
    mats.append(tuple(jnp.concatenate([flat(pows[j][k]) for j in order], axis=1) for k in range(2)))
    return jnp.stack([jnp.concatenate([m[0], m[1]], axis=2) for m in mats], axis=1)


def _pack(arrs, rows):
    flat = jnp.concatenate([a.reshape(-1) for a in arrs])
    return jnp.pad(flat, (0, rows * LANES - flat.shape[0])).reshape(rows, LANES)


def _unpack(buf, like):
    flat = buf.reshape(-1)
    out, off = [], 0
    for a in like:
        out.append(flat[off:off + a.size].reshape(a.shape))
        off += a.size
    return out


SMALL = ("norm_ffn1", "norm_mix", "ssm_a_re", "ssm_a_im", "ssm_log_dt", "ssm_b_re", "ssm_b_im", "ssm_c_re",
         "ssm_c_im", "ssm_d", "ssm_glu_w", "ssm_glu_b", "gm_v_gain", "gm_w_s", "gm_b_s", "gain_ssm_out",
         "gain_gm_out", "norm_ffn2", "norm_final")
BIG = ("ffn1_w_in", "ffn1_w_out", "mix_w_in", "mix_w_out", "ffn2_w_in", "ffn2_w_out")
WEIGHTS = ("norm_ffn1", "ffn1_w_in", "ffn1_w_out", "norm_mix", "mix_w_in", "ssm_a_re", "ssm_a_im", "ssm_log_dt",
           "ssm_b_re", "ssm_b_im", "ssm_c_re", "ssm_c_im", "ssm_d", "ssm_glu_w", "ssm_glu_b", "gm_v_gain", "gm_w_s",
           "gm_b_s", "gain_ssm_out", "gain_gm_out", "mix_w_out", "norm_ffn2", "ffn2_w_in", "ffn2_w_out", "norm_final")


def _step(x, target, w, m, v):
    batch, seq, _ = x.shape
    n = batch * seq
    depth = w["norm_ffn1"].shape[0]
    tm = min(512, n)
    tm_ffn = min(1024, n)
    tk = min(2048, n)
    t_chunk = min(256, seq)
    gm_rows = min(512, seq)
    x = x.reshape(n, D_MODEL)
    target = target.reshape(n, D_MODEL)

    assert depth == 2
    me = _index(_my_place())
    shard = lambda group, l: [w[f"{group}_w_in"][l].astype(BF16), w[f"{group}_w_out"][l].astype(BF16)]
    order = [(g, l) for l in range(depth) for g in ("ffn1", "mix", "ffn2")]
    gathered, pending = {}, {}

    def gather_start(i, after):
        batch_i = order[2 * i:2 * i + 2]
        handles, tok = _exchange_start([shard(g, l) for g, l in batch_i], me, False, f"all_gather_start_{i}", after)
        pending.update(zip(batch_i, handles))
        return tok

    gather_start(0, None)

    def weights(group, l, after=None):
        if (group, l) not in gathered:
            w_in, w_out = _exchange_wait(pending[(group, l)], after, False, f"all_gather_wait_{group}_{l}")
            if group == "mix":
                w_in = jnp.transpose(w_in, (1, 0, 2)).reshape(D_MODEL, IN_COLS)
                w_out = w_out.reshape(D_MODEL, D_MODEL)
            gathered[(group, l)] = (w_in, w_out)
        return gathered[(group, l)]

    tril = jnp.tril(jnp.ones((GM_CHUNK, GM_CHUNK), bool))
    layers = []
    for l in range(depth):
        disc, disc_vjp = jax.vjp(_ssm_discretize, w["ssm_a_re"][l], w["ssm_a_im"][l], w["ssm_log_dt"][l],
                                 w["ssm_b_re"][l], w["ssm_b_im"][l])
        lr, li, bbr, bbi = disc
        bblk, cblk, glu, glub, dskip = _ssm_matrices(bbr, bbi, w["ssm_c_re"][l], w["ssm_c_im"][l],
                                                     w["ssm_glu_w"][l], w["ssm_glu_b"][l], w["ssm_d"][l])
        layers.append(dict(
            disc_vjp=disc_vjp, lr=lr, li=li, bblk=bblk, cblk=cblk, glu=glu, glub=glub, dskip=dskip,
            fwdc=_scan_constants(lr, li, False), revc=_scan_constants(lr, li, True),
            w_tril=jnp.where(tril[None], w["gm_w_s"][l], 0.0).astype(BF16),
            gm_bias=jnp.repeat(w["gm_b_s"][l].T, GM_HEAD_DIM, axis=1),
            g1=w["norm_ffn1"][l][None], gmix=w["norm_mix"][l][None], g2=w["norm_ffn2"][l][None],
            gv=w["gm_v_gain"][l][None], gs=w["gain_ssm_out"][l][None], gg=w["gain_gm_out"][l][None],
        ))

    saved = []
    for l in range(depth):
        p = layers[l]
        x0 = x
        g1, gmix = p["g1"], p["gmix"]
        if l == 0:
            w_in, w_out = weights("ffn1", l, layers[-1]["revc"])
            g1 = _behind(g1, gather_start(1, w_in))
        else:
            w_in, w_out = weights("ffn1", l, x0)
        x1 = _ffn_fwd(x0, g1, w_in, w_out, tm_ffn, f"ffn1_fwd_{l}")
        if l == 0:
            gmix = _behind(gmix, gather_start(2, x1))
        mwi, mwo = weights("mix", l, x1)
        z = _mix_in_fwd(x1, gmix, mwi, tm, f"mix_in_fwd_{l}")
        y_ssm, h = _ssm_fwd(z, p["bblk"], p["cblk"], p["glu"], p["glub"], p["dskip"], p["fwdc"], batch, t_chunk,
                            f"ssm_fwd_{l}")
        y_gm = _gm_fwd(z, p["gv"], p["w_tril"], p["gm_bias"], gm_rows, f"gm_fwd_{l}")
        x2 = _mix_out_fwd(y_ssm, y_gm, p["gs"], p["gg"], mwo, x1, tm, f"mix_out_fwd_{l}")
        x = _ffn_fwd(x2, p["g2"], *weights("ffn2", l, x2), tm_ffn, f"ffn2_fwd_{l}")
        saved.append((x0, x1, x2, z, h, y_ssm, y_gm))

    dx, sq, dnf = _loss_head(x, w["norm_final"][None], target, tm, "loss_head")
    loss = lax.psum((0.5 / D_MODEL) * jnp.sum(sq), AXES)

    small = {k: [None] * depth for k in SMALL if k != "norm_final"}
    sent = []

    def send(group, l, keys, parts):
        (handle,), tok = _exchange_start([parts], me, True, f"reduce_scatter_start_{group}_{l}")
        sent.append((group, l, keys, handle))
        return tok

    token = None
    for l in reversed(range(depth)):
        p = layers[l]
        x0, x1, x2, z, h, y_ssm, y_gm = saved[l]
        mwi, mwo = weights("mix", l)
        dx_out = dx
        g2 = p["g2"] if token is None else _behind(p["g2"], token)
        dx, xn, dgu, act, dgain = _ffn_bwd(x2, g2, dx_out, *weights("ffn2", l), tm_ffn, f"ffn2_bwd_{l}")
        dw_in = _ffn_dw_in(xn, dgu, tk, f"ffn2_dw_in_{l}")
        dw_out = _ffn_dw_out(act, dx_out, tk, f"ffn2_dw_out_{l}").reshape(N_DEV, FF_SHARD // 2, D_MODEL)
        token = send("ffn2", l, ("ffn2_w_in", "ffn2_w_out"), [dw_in, dw_out])
        small["norm_ffn2"][l] = dgain.sum(0)

        dy_ssm, dy_gm, dwo, dgs, dgg = _mix_out_bwd(y_ssm, y_gm, _behind(p["gs"], token), p["gg"], mwo, dx, tm,
                                                    f"mix_out_bwd_{l}")
        dwo = dwo.astype(BF16).reshape(N_DEV, D_MODEL // N_DEV, D_MODEL)
        small["gain_ssm_out"][l] = dgs.sum(0)
        small["gain_gm_out"][l] = dgg.sum(0)

        du_ssm, dglu, dglub, ddskip, dct, db, q = _ssm_bwd(
            z, h, dy_ssm, p["bblk"], p["cblk"], p["glu"], p["glub"], p["dskip"], p["revc"], batch, t_chunk,
            f"ssm_bwd_{l}")
        du_gm, dv_gm, dws, dbias, dgv = _gm_bwd(z, dy_gm, p["gv"], p["w_tril"], p["gm_bias"], gm_rows, f"gm_bwd_{l}")

        q = q.sum(0).reshape(2, 2, HALF_GROUPS, SSM_STATE)
        qr, qi = q[:, 0].reshape(SSM_GROUPS, SSM_STATE), q[:, 1].reshape(SSM_GROUPS, SSM_STATE)
        den = p["lr"] * p["lr"] + p["li"] * p["li"]
        d_re = (qr * p["lr"] + qi * p["li"]) / den
        d_im = (qi * p["lr"] - qr * p["li"]) / den
        dbb = jnp.stack([_block_diag_take(db[hf, :, k * HALF_ST:(k + 1) * HALF_ST], HALF_GROUPS, SSM_CH, SSM_STATE)
                         for k in range(2) for hf in range(2)]).reshape(2, SSM_GROUPS, SSM_CH, SSM_STATE)
        dcc = jnp.stack([_block_diag_take(dct[hf, :, k * HALF_ST:(k + 1) * HALF_ST], HALF_GROUPS, SSM_CH, SSM_STATE)
                         for k in range(2) for hf in range(2)]).reshape(2, SSM_GROUPS, SSM_CH, SSM_STATE)
        da_re, da_im, dlog_dt, db_re, db_im = p["disc_vjp"](
            (d_re, -d_im, jnp.swapaxes(dbb[0], 1, 2), jnp.swapaxes(dbb[1], 1, 2)))
        small["ssm_a_re"][l], small["ssm_a_im"][l], small["ssm_log_dt"][l] = da_re, da_im, dlog_dt
        small["ssm_b_re"][l], small["ssm_b_im"][l] = db_re, db_im
        small["ssm_c_re"][l], small["ssm_c_im"][l] = dcc[0], -dcc[1]
        small["ssm_d"][l] = ddskip.sum(0).reshape(SSM_GROUPS, SSM_CH)
        small["ssm_glu_w"][l] = jnp.concatenate(
            [_block_diag_take(dglu[:, :SSM_WIDTH], SSM_GROUPS, SSM_CH, SSM_CH),
             _block_diag_take(dglu[:, SSM_WIDTH:], SSM_GROUPS, SSM_CH, SSM_CH)], axis=2)
        dglub = dglub.sum(0)
        small["ssm_glu_b"][l] = jnp.concatenate(
            [dglub[:SSM_WIDTH].reshape(SSM_GROUPS, SSM_CH), dglub[SSM_WIDTH:].reshape(SSM_GROUPS, SSM_CH)], axis=1)
        small["gm_v_gain"][l] = dgv.sum(0)
        small["gm_w_s"][l] = jnp.where(tril[None], dws, 0.0)
        small["gm_b_s"][l] = dbias.reshape(GM_CHUNK, GM_HEADS, GM_HEAD_DIM).sum(-1).T

        dx, dwi, dgain = _mix_in_bwd(x1, p["gmix"], du_ssm, du_gm, dv_gm, dx, mwi, tm, f"mix_in_bwd_{l}")
        dwi = jnp.transpose(dwi.astype(BF16).reshape(D_MODEL, N_DEV, IN_COLS // N_DEV), (1, 0, 2))
        token = send("mix", l, ("mix_w_in", "mix_w_out"), [dwi, dwo])
        small["norm_mix"][l] = dgain.sum(0)

        dx_out = dx
        dx, xn, dgu, act, dgain = _ffn_bwd(x0, _behind(p["g1"], token), dx_out, *weights("ffn1", l), tm_ffn,
                                           f"ffn1_bwd_{l}")
        small["norm_ffn1"][l] = dgain.sum(0)
        if l > 0:
            dw_in = _ffn_dw_in(xn, dgu, tk, f"ffn1_dw_in_{l}")
            dw_out = _ffn_dw_out(act, dx_out, tk, f"ffn1_dw_out_{l}").reshape(N_DEV, FF_SHARD // 2, D_MODEL)
            token = send("ffn1", l, ("ffn1_w_in", "ffn1_w_out"), [dw_in, dw_out])
            continue
        small_g = [jnp.stack(small[k]) if k != "norm_final" else dnf.sum(0) for k in SMALL]
        total = sum(int(math.prod(w[k].shape)) for k in SMALL)
        rows = -(-total // (LANES * N_DEV * SUBLANES)) * N_DEV * SUBLANES
        g_all = _all_reduce_small(_pack(small_g, rows).reshape(N_DEV, rows // N_DEV, LANES), "all_reduce_small")
        dw_in = _ffn_dw_in(xn, dgu, tk, f"ffn1_dw_in_{l}", after=g_all)
        token = send("ffn1_in", l, ("ffn1_w_in",), [dw_in])
        dw_out = _ffn_dw_out(act, dx_out, tk, f"ffn1_dw_out_{l}", after=token).reshape(
            N_DEV, FF_SHARD // 2, D_MODEL)
        token = send("ffn1_out", l, ("ffn1_w_out",), [dw_out])

    grad_x = dx.reshape(batch, seq, D_MODEL)
    grads, deltas, new_m, new_v = {}, {}, {}, {}

    g_all = _behind(g_all.reshape(rows, LANES), token)
    like = [w[k] for k in SMALL]
    d_p, m_p, v_p = _adam_packed(g_all, _pack(like, rows), _pack([m[k] for k in SMALL], rows),
                                 _pack([v[k] for k in SMALL], rows), "adam_small")
    for k, g_, d_, m_, v_ in zip(SMALL, _unpack(g_all, like), _unpack(d_p, like), _unpack(m_p, like),
                                 _unpack(v_p, like)):
        grads[k], deltas[k], new_m[k], new_v[k] = g_, d_, m_, v_

    results = {}
    after = d_p
    for group, l, keys, handle in sent:
        landed = _exchange_wait(handle, after, True, f"reduce_scatter_wait_{group}_{l}")
        for k, parts in zip(keys, landed):
            results[k] = _adam_sharded(parts, w[k], m[k], v[k], l, results.get(k), f"adam_{k}_{l}")
            after = results[k][0]
    for k in BIG:
        grads[k], deltas[k], new_m[k], new_v[k] = results[k]
    return loss, grad_x, grads, deltas, new_m, new_v


def kernel(x, norm_ffn1, ffn1_w_in, ffn1_w_out, norm_mix, mix_w_in, ssm_a_re, ssm_a_im, ssm_log_dt, ssm_b_re, ssm_b_im, ssm_c_re, ssm_c_im, ssm_d, ssm_glu_w, ssm_glu_b, gm_v_gain, gm_w_s, gm_b_s, gain_ssm_out, gain_gm_out, mix_w_out, norm_ffn2, ffn2_w_in, ffn2_w_out, norm_final, loss_target, m_norm_ffn1, m_ffn1_w_in, m_ffn1_w_out, m_norm_mix, m_mix_w_in, m_ssm_a_re, m_ssm_a_im, m_ssm_log_dt, m_ssm_b_re, m_ssm_b_im, m_ssm_c_re, m_ssm_c_im, m_ssm_d, m_ssm_glu_w, m_ssm_glu_b, m_gm_v_gain, m_gm_w_s, m_gm_b_s, m_gain_ssm_out, m_gain_gm_out, m_mix_w_out, m_norm_ffn2, m_ffn2_w_in, m_ffn2_w_out, m_norm_final, v_norm_ffn1, v_ffn1_w_in, v_ffn1_w_out, v_norm_mix, v_mix_w_in, v_ssm_a_re, v_ssm_a_im, v_ssm_log_dt, v_ssm_b_re, v_ssm_b_im, v_ssm_c_re, v_ssm_c_im, v_ssm_d, v_ssm_glu_w, v_ssm_glu_b, v_gm_v_gain, v_gm_w_s, v_gm_b_s, v_gain_ssm_out, v_gain_gm_out, v_mix_w_out, v_norm_ffn2, v_ffn2_w_in, v_ffn2_w_out, v_norm_final):
    args = locals()
    w = {k: args[k] for k in WEIGHTS}
    m = {k: args["m_" + k] for k in WEIGHTS}
    v = {k: args["v_" + k] for k in WEIGHTS}
    loss, grad_x, grads, deltas, new_m, new_v = _step(x, loss_target, w, m, v)
    return (loss, grad_x, *[grads[k] for k in WEIGHTS], *[deltas[k] for k in WEIGHTS],
            *[new_m[k] for k in WEIGHTS], *[new_v[k] for k in WEIGHTS])
```

```python
import functools
import math

import jax
import jax.numpy as jnp
from jax import lax
from jax.experimental import pallas as pl
from jax.experimental.pallas import tpu as pltpu

F32 = jnp.float32
BF16 = jnp.bfloat16
MESH = pl.DeviceIdType.MESH
AXES = ("x", "y", "c")

N_DEV = 8
D_MODEL = 1024
D_FF = 2816
FF_SHARD = 2 * D_FF // N_DEV
FF_CHUNKS = 4
MXU_DIM = 256
FF_PIECES = tuple((lo, min(lo + MXU_DIM, FF_SHARD)) for lo in range(0, FF_SHARD, MXU_DIM))
SSM_WIDTH = 512
SSM_CH = 16
SSM_GROUPS = 32
SSM_STATE = 64
HALF_GROUPS = 16
HALF_IN = HALF_GROUPS * SSM_CH
HALF_ST = HALF_GROUPS * SSM_STATE
GM_WIDTH = 512
GM_HEADS = 4
GM_HEAD_DIM = 128
GM_CHUNK = 128
IN_COLS = SSM_WIDTH + 2 * GM_WIDTH
EPS = 1e-6
SUBLANES = 8
LANES = 128

ADAM_LR = 0.001
ADAM_B1 = 0.9
ADAM_B2 = 0.999
ADAM_EPS = 1e-08
ADAM_WD = 0.01
ADAM_STEP = 10

VMEM_LIMIT = 46 * 1024 * 1024


def _cp(*sem):
    return pltpu.CompilerParams(dimension_semantics=sem, vmem_limit_bytes=VMEM_LIMIT)


def _rms_fwd(x, g):
    r = lax.rsqrt(jnp.mean(x * x, axis=-1, keepdims=True) + EPS)
    xh = x * r
    return xh * g, xh, r


def _rms_bwd(dy, xh, r, g):
    dxh = dy * g
    dx = r * (dxh - xh * jnp.mean(dxh * xh, axis=-1, keepdims=True))
    return dx, dy * xh


def _rows8(a):
    m, n = a.shape
    return a.reshape(m // SUBLANES, SUBLANES, n).sum(axis=0)


_GELU_K = math.sqrt(2.0 / math.pi)
_GELU_C = 0.044715


def _gelu(x):
    th = jnp.tanh(_GELU_K * (x + _GELU_C * x * x * x))
    return 0.5 * x * (1.0 + th), th


def _gelu_grad(x, th):
    return 0.5 * (1.0 + th) + 0.5 * x * (1.0 - th * th) * (_GELU_K * (1.0 + 3.0 * _GELU_C * x * x))


def _dot(a, b):
    return jnp.dot(a, b, preferred_element_type=F32)


def _dot_nt(a, b):
    return lax.dot_general(a, b, (((1,), (1,)), ((), ())), preferred_element_type=F32)


def _dot_tn(a, b):
    return lax.dot_general(a, b, (((0,), (0,)), ((), ())), preferred_element_type=F32)


def _ffn_fwd(x, gain, w_in_ag, w_out_ag, tm, name):
    n = x.shape[0]

    def body(x_ref, g_ref, wg_ref, wu_ref, wo_ref, o_ref, xn_ref, gu_ref):
        j = pl.program_id(1)

        @pl.when(j == 0)
        def _():
            xv = x_ref[...]
            y, _, _ = _rms_fwd(xv, g_ref[...])
            xn_ref[...] = y.astype(BF16)
            o_ref[...] = xv

        xn = xn_ref[...]
        wo = wo_ref[...].reshape(FF_SHARD, D_MODEL)
        out = None
        for lo, hi in FF_PIECES:
            gg = _dot(xn, wg_ref[:, lo:hi])
            uu = _dot(xn, wu_ref[:, lo:hi])
            gu_ref[0, :, lo:hi] = gg.astype(BF16)
            gu_ref[1, :, lo:hi] = uu.astype(BF16)
            act = (gg * jax.nn.sigmoid(gg) * uu).astype(BF16)
            part = _dot(act, wo[lo:hi, :])
            out = part if out is None else out + part
        o_ref[...] += 0.5 * out

    return pl.pallas_call(
        body, name=name, grid=(n // tm, FF_CHUNKS),
        in_specs=[
            pl.BlockSpec((tm, D_MODEL), lambda i, j: (i, 0)),
            pl.BlockSpec((1, D_MODEL), lambda i, j: (0, 0)),
            pl.BlockSpec((None, D_MODEL, FF_SHARD), lambda i, j: (j, 0, 0)),
            pl.BlockSpec((None, D_MODEL, FF_SHARD), lambda i, j: (j + FF_CHUNKS, 0, 0)),
            pl.BlockSpec((2, FF_SHARD // 2, D_MODEL), lambda i, j: (j, 0, 0)),
        ],
        out_specs=[
            pl.BlockSpec((tm, D_MODEL), lambda i, j: (i, 0)),
            pl.BlockSpec((tm, D_MODEL), lambda i, j: (i, 0)),
            pl.BlockSpec((None, 2, tm, FF_SHARD), lambda i, j: (j, 0, i, 0)),
        ],
        out_shape=[
            jax.ShapeDtypeStruct((n, D_MODEL), F32),
            jax.ShapeDtypeStruct((n, D_MODEL), BF16),
            jax.ShapeDtypeStruct((FF_CHUNKS, 2, n, FF_SHARD), BF16),
        ],
        compiler_params=_cp("parallel", "arbitrary"),
    )(x, gain, w_in_ag, w_in_ag, w_out_ag)


def _ffn_bwd(x, gain, dy, gu, w_in_ag, w_out_ag, tm, name):
    n = x.shape[0]

    def body(x_ref, g_ref, dy_ref, gu_ref, wg_ref, wu_ref, wo_ref, dx_ref, dgu_ref, act_ref, dgain_ref, dyb_ref):
        i, j = pl.program_id(0), pl.program_id(1)

        @pl.when(jnp.logical_and(i == 0, j == 0))
        def _():
            dgain_ref[...] = jnp.zeros_like(dgain_ref)

        @pl.when(j == 0)
        def _():
            dyb_ref[...] = dy_ref[...].astype(BF16)
            dx_ref[...] = jnp.zeros_like(dx_ref)

        dyb = dyb_ref[...]
        wo = wo_ref[...].reshape(FF_SHARD, D_MODEL)
        for lo, hi in FF_PIECES:
            wg, wu = wg_ref[:, lo:hi], wu_ref[:, lo:hi]
            gg = gu_ref[0, :, lo:hi].astype(F32)
            uu = gu_ref[1, :, lo:hi].astype(F32)
            dact = 0.5 * _dot_nt(dyb, wo[lo:hi, :])
            sig = jax.nn.sigmoid(gg)
            silu = gg * sig
            act_ref[:, lo:hi] = (silu * uu).astype(BF16)
            du = (dact * silu).astype(BF16)
            dg = (dact * uu * (sig * (1.0 + gg * (1.0 - sig)))).astype(BF16)
            dgu_ref[0, :, lo:hi] = dg
            dgu_ref[1, :, lo:hi] = du
            dx_ref[...] += _dot_nt(dg, wg) + _dot_nt(du, wu)

        @pl.when(j == FF_CHUNKS - 1)
        def _():
            g = g_ref[...]
            _, xh, r = _rms_fwd(x_ref[...], g)
            dx, dgr = _rms_bwd(dx_ref[...], xh, r, g)
            dx_ref[...] = dy_ref[...] + dx
            dgain_ref[...] += _rows8(dgr)

    return pl.pallas_call(
        body, name=name, grid=(n // tm, FF_CHUNKS),
        in_specs=[
            pl.BlockSpec((tm, D_MODEL), lambda i, j: (i, 0)),
            pl.BlockSpec((1, D_MODEL), lambda i, j: (0, 0)),
            pl.BlockSpec((tm, D_MODEL), lambda i, j: (i, 0)),
            pl.BlockSpec((None, 2, tm, FF_SHARD), lambda i, j: (j, 0, i, 0)),
            pl.BlockSpec((None, D_MODEL, FF_SHARD), lambda i, j: (j, 0, 0)),
            pl.BlockSpec((None, D_MODEL, FF_SHARD), lambda i, j: (j + FF_CHUNKS, 0, 0)),
            pl.BlockSpec((2, FF_SHARD // 2, D_MODEL), lambda i, j: (j, 0, 0)),
        ],
        out_specs=[
            pl.BlockSpec((tm, D_MODEL), lambda i, j: (i, 0)),
            pl.BlockSpec((None, 2, tm, FF_SHARD), lambda i, j: (j, 0, i, 0)),
            pl.BlockSpec((None, tm, FF_SHARD), lambda i, j: (j, i, 0)),
            pl.BlockSpec((SUBLANES, D_MODEL), lambda i, j: (0, 0)),
        ],
        out_shape=[
            jax.ShapeDtypeStruct((n, D_MODEL), F32),
            jax.ShapeDtypeStruct((FF_CHUNKS, 2, n, FF_SHARD), BF16),
            jax.ShapeDtypeStruct((FF_CHUNKS, n, FF_SHARD), BF16),
            jax.ShapeDtypeStruct((SUBLANES, D_MODEL), F32),
        ],
        scratch_shapes=[pltpu.VMEM((tm, D_MODEL), BF16)],
        compiler_params=_cp("arbitrary", "arbitrary"),
    )(x, gain, dy, gu, w_in_ag, w_in_ag, w_out_ag)


def _ffn_dw_in(xn, dgu, tk, name, after=None):
    n = xn.shape[0]
    nk = n // tk
    deps = [] if after is None else [after]

    def body(a_ref, b_ref, *rest):
        o_ref, acc_ref = rest[-2:]
        k = pl.program_id(2)

        @pl.when(k == 0)
        def _():
            acc_ref[...] = jnp.zeros_like(acc_ref)

        acc_ref[...] += _dot_tn(a_ref[...], b_ref[...])

        @pl.when(k == nk - 1)
        def _():
            o_ref[...] = acc_ref[...].astype(BF16)

    return pl.pallas_call(
        body, name=name, grid=(FF_CHUNKS, 2, nk),
        in_specs=[
            pl.BlockSpec((tk, D_MODEL), lambda j, p, k: (k, 0)),
            pl.BlockSpec((None, None, tk, FF_SHARD), lambda j, p, k: (j, p, k, 0)),
        ] + [pl.BlockSpec(memory_space=pl.ANY)] * len(deps),
        out_specs=pl.BlockSpec((None, D_MODEL, FF_SHARD), lambda j, p, k: (FF_CHUNKS * p + j, 0, 0)),
        out_shape=jax.ShapeDtypeStruct((N_DEV, D_MODEL, FF_SHARD), BF16),
        scratch_shapes=[pltpu.VMEM((D_MODEL, FF_SHARD), F32)],
        compiler_params=_cp("parallel", "parallel", "arbitrary"),
    )(xn, dgu, *deps)


def _ffn_dw_out(act, dy, tk, name, after=None):
    n = act.shape[1]
    nk = n // tk
    deps = [] if after is None else [after]

    def body(a_ref, b_ref, *rest):
        o_ref, acc_ref = rest[-2:]
        k = pl.program_id(1)

        @pl.when(k == 0)
        def _():
            acc_ref[...] = jnp.zeros_like(acc_ref)

        acc_ref[...] += _dot_tn(a_ref[...], b_ref[...].astype(BF16))

        @pl.when(k == nk - 1)
        def _():
            o_ref[...] = (0.5 * acc_ref[...]).astype(BF16)

    return pl.pallas_call(
        body, name=name, grid=(FF_CHUNKS, nk),
        in_specs=[
            pl.BlockSpec((None, tk, FF_SHARD), lambda j, k: (j, k, 0)),
            pl.BlockSpec((tk, D_MODEL), lambda j, k: (k, 0)),
        ] + [pl.BlockSpec(memory_space=pl.ANY)] * len(deps),
        out_specs=pl.BlockSpec((None, FF_SHARD, D_MODEL), lambda j, k: (j, 0, 0)),
        out_shape=jax.ShapeDtypeStruct((FF_CHUNKS, FF_SHARD, D_MODEL), BF16),
        scratch_shapes=[pltpu.VMEM((FF_SHARD, D_MODEL), F32)],
        compiler_params=_cp("parallel", "arbitrary"),
    )(act, dy, *deps)


def _mix_in_fwd(x, gain, w, tm, name):
    n = x.shape[0]

    def body(x_ref, g_ref, w_ref, z_ref):
        y, _, _ = _rms_fwd(x_ref[...], g_ref[...])
        z_ref[...] = _dot(y.astype(BF16), w_ref[...])

    return pl.pallas_call(
        body, name=name, grid=(n // tm,),
        in_specs=[
            pl.BlockSpec((tm, D_MODEL), lambda i: (i, 0)),
            pl.BlockSpec((1, D_MODEL), lambda i: (0, 0)),
            pl.BlockSpec((D_MODEL, IN_COLS), lambda i: (0, 0)),
        ],
        out_specs=pl.BlockSpec((tm, IN_COLS), lambda i: (i, 0)),
        out_shape=jax.ShapeDtypeStruct((n, IN_COLS), F32),
        compiler_params=_cp("parallel"),
    )(x, gain, w)


def _mix_in_bwd(x, gain, du_ssm, du_gm, dv_gm, d_res, w, tm, name):
    n = x.shape[0]

    def body(x_ref, g_ref, d0_ref, d1_ref, d2_ref, dres_ref, w_ref, dx_ref, dw_ref, dgain_ref):
        i = pl.program_id(0)

        @pl.when(i == 0)
        def _():
            dw_ref[...] = jnp.zeros_like(dw_ref)
            dgain_ref[...] = jnp.zeros_like(dgain_ref)

        g = g_ref[...]
        y, xh, r = _rms_fwd(x_ref[...], g)
        xn = y.astype(BF16)
        dxn = jnp.zeros((tm, D_MODEL), F32)
        for k, d_ref in enumerate((d0_ref, d1_ref, d2_ref)):
            dz = d_ref[...].astype(BF16)
            cols = slice(k * SSM_WIDTH, (k + 1) * SSM_WIDTH)
            dxn += _dot_nt(dz, w_ref[:, cols])
            dw_ref[:, cols] += _dot_tn(xn, dz)
        dx, dgr = _rms_bwd(dxn, xh, r, g)
        dx_ref[...] = dres_ref[...] + dx
        dgain_ref[...] += _rows8(dgr)

    row = lambda i: (i, 0)
    fixed = lambda i: (0, 0)
    return pl.pallas_call(
        body, name=name, grid=(n // tm,),
        in_specs=[
            pl.BlockSpec((tm, D_MODEL), row),
            pl.BlockSpec((1, D_MODEL), fixed),
            pl.BlockSpec((tm, SSM_WIDTH), row),
            pl.BlockSpec((tm, GM_WIDTH), row),
            pl.BlockSpec((tm, GM_WIDTH), row),
            pl.BlockSpec((tm, D_MODEL), row),
            pl.BlockSpec((D_MODEL, IN_COLS), fixed),
        ],
        out_specs=[
            pl.BlockSpec((tm, D_MODEL), row),
            pl.BlockSpec((D_MODEL, IN_COLS), fixed),
            pl.BlockSpec((SUBLANES, D_MODEL), fixed),
        ],
        out_shape=[
            jax.ShapeDtypeStruct((n, D_MODEL), F32),
            jax.ShapeDtypeStruct((D_MODEL, IN_COLS), F32),
            jax.ShapeDtypeStruct((SUBLANES, D_MODEL), F32),
        ],
        compiler_params=_cp("arbitrary"),
    )(x, gain, du_ssm, du_gm, dv_gm, d_res, w)


def _mix_out_fwd(y_ssm, y_gm, g_ssm, g_gm, w, x, tm, name):
    n = x.shape[0]

    def body(ys_ref, yg_ref, gs_ref, gg_ref, w_ref, x_ref, o_ref):
        a, _, _ = _rms_fwd(ys_ref[...], gs_ref[...])
        b, _, _ = _rms_fwd(yg_ref[...], gg_ref[...])
        o_ref[...] = (x_ref[...] + _dot(a.astype(BF16), w_ref[0:SSM_WIDTH, :])
                      + _dot(b.astype(BF16), w_ref[SSM_WIDTH:D_MODEL, :]))

    row = lambda i: (i, 0)
    fixed = lambda i: (0, 0)
    return pl.pallas_call(
        body, name=name, grid=(n // tm,),
        in_specs=[
            pl.BlockSpec((tm, SSM_WIDTH), row), pl.BlockSpec((tm, GM_WIDTH), row),
            pl.BlockSpec((1, SSM_WIDTH), fixed), pl.BlockSpec((1, GM_WIDTH), fixed),
            pl.BlockSpec((D_MODEL, D_MODEL), fixed), pl.BlockSpec((tm, D_MODEL), row),
        ],
        out_specs=pl.BlockSpec((tm, D_MODEL), row),
        out_shape=jax.ShapeDtypeStruct((n, D_MODEL), F32),
        compiler_params=_cp("parallel"),
    )(y_ssm, y_gm, g_ssm, g_gm, w, x)


def _mix_out_bwd(y_ssm, y_gm, g_ssm, g_gm, w, dx, tm, name):
    n = dx.shape[0]

    def body(ys_ref, yg_ref, gs_ref, gg_ref, w_ref, dx_ref, dys_ref, dyg_ref, dw_ref, dgs_ref, dgg_ref):
        i = pl.program_id(0)

        @pl.when(i == 0)
        def _():
            dw_ref[...] = jnp.zeros_like(dw_ref)
            dgs_ref[...] = jnp.zeros_like(dgs_ref)
            dgg_ref[...] = jnp.zeros_like(dgg_ref)

        dxb = dx_ref[...].astype(BF16)
        parts = ((ys_ref, gs_ref, dys_ref, dgs_ref, 0), (yg_ref, gg_ref, dyg_ref, dgg_ref, SSM_WIDTH))
        for y_ref, g_ref, dy_ref, dg_ref, off in parts:
            g = g_ref[...]
            yn, xh, r = _rms_fwd(y_ref[...], g)
            rows = slice(off, off + SSM_WIDTH)
            dyn = _dot_nt(dxb, w_ref[rows, :])
            dw_ref[rows, :] += _dot_tn(yn.astype(BF16), dxb)
            dy, dgr = _rms_bwd(dyn, xh, r, g)
            dy_ref[...] = dy
            dg_ref[...] += _rows8(dgr)

    row = lambda i: (i, 0)
    fixed = lambda i: (0, 0)
    return pl.pallas_call(
        body, name=name, grid=(n // tm,),
        in_specs=[
            pl.BlockSpec((tm, SSM_WIDTH), row), pl.BlockSpec((tm, GM_WIDTH), row),
            pl.BlockSpec((1, SSM_WIDTH), fixed), pl.BlockSpec((1, GM_WIDTH), fixed),
            pl.BlockSpec((D_MODEL, D_MODEL), fixed), pl.BlockSpec((tm, D_MODEL), row),
        ],
        out_specs=[
            pl.BlockSpec((tm, SSM_WIDTH), row), pl.BlockSpec((tm, GM_WIDTH), row),
            pl.BlockSpec((D_MODEL, D_MODEL), fixed),
            pl.BlockSpec((SUBLANES, SSM_WIDTH), fixed), pl.BlockSpec((SUBLANES, GM_WIDTH), fixed),
        ],
        out_shape=[
            jax.ShapeDtypeStruct((n, SSM_WIDTH), F32), jax.ShapeDtypeStruct((n, GM_WIDTH), F32),
            jax.ShapeDtypeStruct((D_MODEL, D_MODEL), F32),
            jax.ShapeDtypeStruct((SUBLANES, SSM_WIDTH), F32), jax.ShapeDtypeStruct((SUBLANES, GM_WIDTH), F32),
        ],
        compiler_params=_cp("arbitrary"),
    )(y_ssm, y_gm, g_ssm, g_gm, w, dx)


SCAN_W = 512
SCAN_PIECES = HALF_ST // SCAN_W


def _scan_tiles(src_ref, dst_ref, dst_off, c_ref, half, carry_ref, n_tiles, reverse, extra=None):
    shifts = (1, 2, 4)
    carry_row = 0 if reverse else SUBLANES - 1

    def cols(piece, im):
        lo = im * HALF_ST + piece * SCAN_W
        return slice(lo, lo + SCAN_W)

    def step(t, state):
        carries, accs = state
        k = (n_tiles - 1 - t) if reverse else t
        rows = slice(k * SUBLANES, (k + 1) * SUBLANES)
        new_carries, new_accs = [], []
        for piece in range(SCAN_PIECES):
            cr, ci = carries[piece]
            xr0 = src_ref[rows, cols(piece, 0)]
            xi0 = src_ref[rows, cols(piece, 1)]
            xr, xi = xr0, xi0
            for si, s in enumerate(shifts):
                ar = c_ref[half, si, :, cols(piece, 0)]
                ai = c_ref[half, si, :, cols(piece, 1)]
                sh = (SUBLANES - s) if reverse else s
                sr = pltpu.roll(xr, sh, 0)
                sm = pltpu.roll(xi, sh, 0)
                xr, xi = xr + (ar * sr - ai * sm), xi + (ar * sm + ai * sr)
            pr = c_ref[half, 3, :, cols(piece, 0)]
            pi = c_ref[half, 3, :, cols(piece, 1)]
            hr = xr + (pr * cr - pi * ci)
            hi = xi + (pr * ci + pi * cr)
            dst_ref[rows, pl.ds(dst_off + piece * SCAN_W, SCAN_W)] = hr
            dst_ref[rows, pl.ds(dst_off + HALF_ST + piece * SCAN_W, SCAN_W)] = hi
            new_carries.append((jnp.broadcast_to(hr[carry_row:carry_row + 1, :], (SUBLANES, SCAN_W)),
                                jnp.broadcast_to(hi[carry_row:carry_row + 1, :], (SUBLANES, SCAN_W))))
            if extra is not None:
                new_accs.append(extra(rows, piece, (xr0, xi0), (hr, hi), accs[piece]))
        return tuple(new_carries), tuple(new_accs)

    base = half * 2 * HALF_ST
    carries0 = tuple((carry_ref[:, pl.ds(base + p * SCAN_W, SCAN_W)],
                      carry_ref[:, pl.ds(base + HALF_ST + p * SCAN_W, SCAN_W)]) for p in range(SCAN_PIECES))
    zero = jnp.zeros((SUBLANES, SCAN_W), F32)
    accs0 = tuple((zero, zero) for _ in range(SCAN_PIECES)) if extra is not None else ()
    state = (carries0, accs0)
    for t in range(n_tiles):
        state = step(t, state)
    carries, accs = state
    for p in range(SCAN_PIECES):
        carry_ref[:, pl.ds(base + p * SCAN_W, SCAN_W)] = carries[p][0]
        carry_ref[:, pl.ds(base + HALF_ST + p * SCAN_W, SCAN_W)] = carries[p][1]
    return accs


def _ssm_tail(hb, u, c_ref, glu_ref, glub_ref, dskip_ref):
    ypre = u * dskip_ref[...]
    parts = []
    for half in range(2):
        parts.append(_dot(hb[half], c_ref[half]))
    ypre = ypre + jnp.concatenate(parts, axis=1)
    yg, th = _gelu(ypre)
    zz = _dot(yg.astype(BF16), glu_ref[...]) + glub_ref[...]
    z1, z2 = zz[:, :SSM_WIDTH], zz[:, SSM_WIDTH:]
    sg = jax.nn.sigmoid(z2)
    return ypre, th, yg, z1, sg


def _ssm_fwd(z, bblk, cblk, glu, glub, dskip, fwdc, batch, t_chunk, name):
    n = z.shape[0]
    nk = n // batch // t_chunk
    n_tiles = t_chunk // SUBLANES

    def body(u_ref, b_ref, c_ref, glu_ref, glub_ref, dskip_ref, k_ref, y_ref, h_ref, bu_ref, carry_ref):
        @pl.when(pl.program_id(1) == 0)
        def _():
            carry_ref[...] = jnp.zeros_like(carry_ref)

        u = u_ref[...]
        ub = u.astype(BF16)
        for half in range(2):
            bu_ref[half] = _dot(ub[:, half * HALF_IN:(half + 1) * HALF_IN], b_ref[half])
            _scan_tiles(bu_ref.at[half], h_ref, half * 2 * HALF_ST, k_ref, half, carry_ref, n_tiles, False)
        hb = [h_ref[:, half * 2 * HALF_ST:(half + 1) * 2 * HALF_ST].astype(BF16) for half in range(2)]
        _, _, _, z1, sg = _ssm_tail(hb, u, c_ref, glu_ref, glub_ref, dskip_ref)
        y_ref[...] = z1 * sg

    fixed2 = lambda b, k: (0, 0)
    fixed3 = lambda b, k: (0, 0, 0)
    row = lambda b, k: (b * nk + k, 0)
    return pl.pallas_call(
        body, name=name, grid=(batch, nk),
        in_specs=[
            pl.BlockSpec((t_chunk, SSM_WIDTH), row),
            pl.BlockSpec((2, HALF_IN, 2 * HALF_ST), fixed3),
            pl.BlockSpec((2, 2 * HALF_ST, HALF_IN), fixed3),
            pl.BlockSpec((SSM_WIDTH, 2 * SSM_WIDTH), fixed2),
            pl.BlockSpec((1, 2 * SSM_WIDTH), fixed2),
            pl.BlockSpec((1, SSM_WIDTH), fixed2),
            pl.BlockSpec((2, 4, SUBLANES, 2 * HALF_ST), lambda b, k: (0, 0, 0, 0)),
        ],
        out_specs=[pl.BlockSpec((t_chunk, SSM_WIDTH), row), pl.BlockSpec((t_chunk, 4 * HALF_ST), row)],
        out_shape=[jax.ShapeDtypeStruct((n, SSM_WIDTH), F32), jax.ShapeDtypeStruct((n, 4 * HALF_ST), F32)],
        scratch_shapes=[pltpu.VMEM((2, t_chunk, 2 * HALF_ST), F32), pltpu.VMEM((SUBLANES, 4 * HALF_ST), F32)],
        compiler_params=_cp("parallel", "arbitrary"),
    )(z, bblk, cblk, glu, glub, dskip, fwdc)


def _ssm_bwd(z, h, dy, bblk, cblk, glu, glub, dskip, revc, batch, t_chunk, name):
    n = z.shape[0]
    nk = n // batch // t_chunk
    n_tiles = t_chunk // SUBLANES

    def body(u_ref, h_ref, dy_ref, b_ref, c_ref, glu_ref, glub_ref, dskip_ref, k_ref,
             du_ref, dglu_ref, dglub_ref, ddskip_ref, dct_ref, db_ref, q_ref, g_ref, carry_ref):
        first = jnp.logical_and(pl.program_id(0) == 0, pl.program_id(1) == 0)

        @pl.when(first)
        def _():
            for r in (dglu_ref, dglub_ref, ddskip_ref, dct_ref, db_ref, q_ref):
                r[...] = jnp.zeros_like(r)

        @pl.when(pl.program_id(1) == 0)
        def _():
            carry_ref[...] = jnp.zeros_like(carry_ref)

        u = u_ref[...]
        ub = u.astype(BF16)
        hb = [h_ref[:, half * 2 * HALF_ST:(half + 1) * 2 * HALF_ST].astype(BF16) for half in range(2)]
        ypre, th, yg, z1, sg = _ssm_tail(hb, u, c_ref, glu_ref, glub_ref, dskip_ref)
        dout = dy_ref[...]
        dz = jnp.concatenate([dout * sg, dout * z1 * sg * (1.0 - sg)], axis=1)
        dzb = dz.astype(BF16)
        dglu_ref[...] += _dot_tn(yg.astype(BF16), dzb)
        dglub_ref[...] += _rows8(dz)
        dypre = _dot_nt(dzb, glu_ref[...]) * _gelu_grad(ypre, th)
        ddskip_ref[...] += _rows8(dypre * u)
        dypb = dypre.astype(BF16)
        du_parts = []
        for half in range(2):
            dyp_h = dypb[:, half * HALF_IN:(half + 1) * HALF_IN]
            dct_ref[half] += _dot_tn(dyp_h, hb[half])
            g_ref[half] = _dot_nt(dyp_h, c_ref[half])

            def extra(rows, piece, x_in, g_out, acc, half=half):
                er, ei = g_out[0] - x_in[0], g_out[1] - x_in[1]
                base = half * 2 * HALF_ST + piece * SCAN_W
                hr = h_ref[rows, pl.ds(base, SCAN_W)]
                hi = h_ref[rows, pl.ds(base + HALF_ST, SCAN_W)]
                return acc[0] + (er * hr + ei * hi), acc[1] + (er * hi - ei * hr)

            accs = _scan_tiles(g_ref.at[half], g_ref.at[half], 0, k_ref, half, carry_ref, n_tiles, True, extra)
            for piece in range(SCAN_PIECES):
                base = half * 2 * HALF_ST + piece * SCAN_W
                q_ref[:, pl.ds(base, SCAN_W)] += accs[piece][0]
                q_ref[:, pl.ds(base + HALF_ST, SCAN_W)] += accs[piece][1]
            gb = g_ref[half].astype(BF16)
            db_ref[half] += _dot_tn(ub[:, half * HALF_IN:(half + 1) * HALF_IN], gb)
            du_parts.append(_dot_nt(gb, b_ref[half]))
        du_ref[...] = dypre * dskip_ref[...] + jnp.concatenate(du_parts, axis=1)

    fixed2 = lambda b, k: (0, 0)
    fixed3 = lambda b, k: (0, 0, 0)
    row = lambda b, k: (b * nk + (nk - 1 - k), 0)
    return pl.pallas_call(
        body, name=name, grid=(batch, nk),
        in_specs=[
            pl.BlockSpec((t_chunk, SSM_WIDTH), row),
            pl.BlockSpec((t_chunk, 4 * HALF_ST), row),
            pl.BlockSpec((t_chunk, SSM_WIDTH), row),
            pl.BlockSpec((2, HALF_IN, 2 * HALF_ST), fixed3),
            pl.BlockSpec((2, 2 * HALF_ST, HALF_IN), fixed3),
            pl.BlockSpec((SSM_WIDTH, 2 * SSM_WIDTH), fixed2),
            pl.BlockSpec((1, 2 * SSM_WIDTH), fixed2),
            pl.BlockSpec((1, SSM_WIDTH), fixed2),
            pl.BlockSpec((2, 4, SUBLANES, 2 * HALF_ST), lambda b, k: (0, 0, 0, 0)),
        ],
        out_specs=[
            pl.BlockSpec((t_chunk, SSM_WIDTH), row),
            pl.BlockSpec((SSM_WIDTH, 2 * SSM_WIDTH), fixed2),
            pl.BlockSpec((SUBLANES, 2 * SSM_WIDTH), fixed2),
            pl.BlockSpec((SUBLANES, SSM_WIDTH), fixed2),
            pl.BlockSpec((2, HALF_IN, 2 * HALF_ST), fixed3),
            pl.BlockSpec((2, HALF_IN, 2 * HALF_ST), fixed3),
            pl.BlockSpec((SUBLANES, 4 * HALF_ST), fixed2),
        ],
        out_shape=[
            jax.ShapeDtypeStruct((n, SSM_WIDTH), F32),
            jax.ShapeDtypeStruct((SSM_WIDTH, 2 * SSM_WIDTH), F32),
            jax.ShapeDtypeStruct((SUBLANES, 2 * SSM_WIDTH), F32),
            jax.ShapeDtypeStruct((SUBLANES, SSM_WIDTH), F32),
            jax.ShapeDtypeStruct((2, HALF_IN, 2 * HALF_ST), F32),
            jax.ShapeDtypeStruct((2, HALF_IN, 2 * HALF_ST), F32),
            jax.ShapeDtypeStruct((SUBLANES, 4 * HALF_ST), F32),
        ],
        scratch_shapes=[pltpu.VMEM((2, t_chunk, 2 * HALF_ST), F32), pltpu.VMEM((SUBLANES, 4 * HALF_ST), F32)],
        compiler_params=_cp("arbitrary", "arbitrary"),
    )(z, h, dy, bblk, cblk, glu, glub, dskip, revc)


def _gm_chunk_fwd(u, v, gain_ref, w_ref, bias_ref):
    ug, thu = _gelu(u)
    vg, thv = _gelu(v)
    rs, vns, ss = [], [], []
    for hh in range(GM_HEADS):
        cs = slice(hh * GM_HEAD_DIM, (hh + 1) * GM_HEAD_DIM)
        vn, _, r = _rms_fwd(vg[:, cs], gain_ref[:, cs])
        s = _dot(w_ref[hh], vn.astype(BF16)) + bias_ref[:, cs]
        rs.append(r)
        vns.append(vn)
        ss.append(s)
    return ug, thu, thv, vg, rs, vns, ss


def _gm_fwd(z, gain, w_tril, bias, rows, name):
    n = z.shape[0]
    chunks = rows // GM_CHUNK

    def body(u_ref, v_ref, gain_ref, w_ref, bias_ref, y_ref):
        for c in range(chunks):
            rs_ = slice(c * GM_CHUNK, (c + 1) * GM_CHUNK)
            ug, _, _, _, _, _, ss = _gm_chunk_fwd(u_ref[rs_, :], v_ref[rs_, :], gain_ref, w_ref, bias_ref)
            y_ref[rs_, :] = ug * jnp.concatenate(ss, axis=1)

    return pl.pallas_call(
        body, name=name, grid=(n // rows,),
        in_specs=[
            pl.BlockSpec((rows, GM_WIDTH), lambda i: (i, 1)),
            pl.BlockSpec((rows, GM_WIDTH), lambda i: (i, 2)),
            pl.BlockSpec((1, GM_WIDTH), lambda i: (0, 0)),
            pl.BlockSpec((GM_HEADS, GM_CHUNK, GM_CHUNK), lambda i: (0, 0, 0)),
            pl.BlockSpec((GM_CHUNK, GM_WIDTH), lambda i: (0, 0)),
        ],
        out_specs=pl.BlockSpec((rows, GM_WIDTH), lambda i: (i, 0)),
        out_shape=jax.ShapeDtypeStruct((n, GM_WIDTH), F32),
        compiler_params=_cp("parallel"),
    )(z, z, gain, w_tril, bias)


def _gm_bwd(z, dy, gain, w_tril, bias, rows, name):
    n = z.shape[0]
    chunks = rows // GM_CHUNK

    def body(u_ref, v_ref, dy_ref, gain_ref, w_ref, bias_ref, du_ref, dv_ref, dw_ref, dbias_ref, dgain_ref):
        @pl.when(pl.program_id(0) == 0)
        def _():
            dw_ref[...] = jnp.zeros_like(dw_ref)
            dbias_ref[...] = jnp.zeros_like(dbias_ref)
            dgain_ref[...] = jnp.zeros_like(dgain_ref)

        for c in range(chunks):
            rs_ = slice(c * GM_CHUNK, (c + 1) * GM_CHUNK)
            u, v = u_ref[rs_, :], v_ref[rs_, :]
            ug, thu, thv, vg, rs, vns, ss = _gm_chunk_fwd(u, v, gain_ref, w_ref, bias_ref)
            dout = dy_ref[rs_, :]
            ds = dout * ug
            du_ref[rs_, :] = dout * jnp.concatenate(ss, axis=1) * _gelu_grad(u, thu)
            dbias_ref[...] += ds
            dvg_parts, dgain_parts = [], []
            for hh in range(GM_HEADS):
                cs = slice(hh * GM_HEAD_DIM, (hh + 1) * GM_HEAD_DIM)
                dsb = ds[:, cs].astype(BF16)
                dvn = _dot_tn(w_ref[hh], dsb)
                dw_ref[hh] += _dot_nt(dsb, vns[hh].astype(BF16))
                g = gain_ref[:, cs]
                xh = vg[:, cs] * rs[hh]
                dvg, dgr = _rms_bwd(dvn, xh, rs[hh], g)
                dvg_parts.append(dvg)
                dgain_parts.append(dgr)
            dv_ref[rs_, :] = jnp.concatenate(dvg_parts, axis=1) * _gelu_grad(v, thv)
            dgain_ref[...] += _rows8(jnp.concatenate(dgain_parts, axis=1))

    row = lambda i: (i, 0)
    return pl.pallas_call(
        body, name=name, grid=(n // rows,),
        in_specs=[
            pl.BlockSpec((rows, GM_WIDTH), lambda i: (i, 1)),
            pl.BlockSpec((rows, GM_WIDTH), lambda i: (i, 2)),
            pl.BlockSpec((rows, GM_WIDTH), row),
            pl.BlockSpec((1, GM_WIDTH), lambda i: (0, 0)),
            pl.BlockSpec((GM_HEADS, GM_CHUNK, GM_CHUNK), lambda i: (0, 0, 0)),
            pl.BlockSpec((GM_CHUNK, GM_WIDTH), lambda i: (0, 0)),
        ],
        out_specs=[
            pl.BlockSpec((rows, GM_WIDTH), row), pl.BlockSpec((rows, GM_WIDTH), row),
            pl.BlockSpec((GM_HEADS, GM_CHUNK, GM_CHUNK), lambda i: (0, 0, 0)),
            pl.BlockSpec((GM_CHUNK, GM_WIDTH), lambda i: (0, 0)),
            pl.BlockSpec((SUBLANES, GM_WIDTH), lambda i: (0, 0)),
        ],
        out_shape=[
            jax.ShapeDtypeStruct((n, GM_WIDTH), F32), jax.ShapeDtypeStruct((n, GM_WIDTH), F32),
            jax.ShapeDtypeStruct((GM_HEADS, GM_CHUNK, GM_CHUNK), F32),
            jax.ShapeDtypeStruct((GM_CHUNK, GM_WIDTH), F32),
            jax.ShapeDtypeStruct((SUBLANES, GM_WIDTH), F32),
        ],
        compiler_params=_cp("arbitrary"),
    )(z, z, dy, gain, w_tril, bias)


def _loss_head(x, gain, target, tm, name):
    n = x.shape[0]

    def body(x_ref, g_ref, t_ref, dx_ref, sq_ref, dgain_ref):
        @pl.when(pl.program_id(0) == 0)
        def _():
            sq_ref[...] = jnp.zeros_like(sq_ref)
            dgain_ref[...] = jnp.zeros_like(dgain_ref)

        g = g_ref[...]
        y, xh, r = _rms_fwd(x_ref[...], g)
        err = y - t_ref[...]
        sq_ref[...] += _rows8(err * err)
        dx, dgr = _rms_bwd(err * (1.0 / D_MODEL), xh, r, g)
        dx_ref[...] = dx
        dgain_ref[...] += _rows8(dgr)

    row = lambda i: (i, 0)
    fixed = lambda i: (0, 0)
    return pl.pallas_call(
        body, name=name, grid=(n // tm,),
        in_specs=[pl.BlockSpec((tm, D_MODEL), row), pl.BlockSpec((1, D_MODEL), fixed), pl.BlockSpec((tm, D_MODEL), row)],
        out_specs=[pl.BlockSpec((tm, D_MODEL), row), pl.BlockSpec((SUBLANES, D_MODEL), fixed),
                   pl.BlockSpec((SUBLANES, D_MODEL), fixed)],
        out_shape=[jax.ShapeDtypeStruct((n, D_MODEL), F32), jax.ShapeDtypeStruct((SUBLANES, D_MODEL), F32),
                   jax.ShapeDtypeStruct((SUBLANES, D_MODEL), F32)],
        compiler_params=_cp("arbitrary"),
    )(x, gain, target)


def _adam_math(w, g, m, v):
    m2 = ADAM_B1 * m + (1.0 - ADAM_B1) * g
    v2 = ADAM_B2 * v + (1.0 - ADAM_B2) * (g * g)
    m_hat = m2 / (1.0 - ADAM_B1 ** ADAM_STEP)
    v_hat = v2 / (1.0 - ADAM_B2 ** ADAM_STEP)
    delta = -ADAM_LR * (m_hat / (jnp.sqrt(v_hat) + ADAM_EPS) + ADAM_WD * w)
    return delta, m2, v2


def _adam_sharded(parts, w, m, v, layer, earlier, name):
    depth, r, c = w.shape
    tr = max(t for t in range(16, 129, 16) if r % t == 0)

    def body(p_ref, w_ref, m_ref, v_ref, *rest):
        g_ref, d_ref, m2_ref, v2_ref = rest[-4:]
        g = p_ref[0].astype(F32)
        for s in range(1, N_DEV):
            g = g + p_ref[s].astype(F32)
        delta, m2, v2 = _adam_math(w_ref[...], g, m_ref[...], v_ref[...])
        g_ref[...] = g
        d_ref[...] = delta
        m2_ref[...] = m2
        v2_ref[...] = v2

    blk = pl.BlockSpec((None, tr, c), lambda i: (layer, i, 0))
    extra = [] if earlier is None else list(earlier)
    return pl.pallas_call(
        body, name=name, grid=(r // tr,),
        in_specs=[pl.BlockSpec((N_DEV, tr, c), lambda i: (0, i, 0)), blk, blk, blk]
        + [pl.BlockSpec(memory_space=pl.ANY)] * len(extra),
        out_specs=[blk, blk, blk, blk],
        out_shape=[jax.ShapeDtypeStruct((depth, r, c), F32)] * 4,
        input_output_aliases={4 + i: i for i in range(len(extra))},
        compiler_params=_cp("parallel"),
    )(parts, w, m, v, *extra)


def _adam_packed(g, w, m, v, name):
    r, c = g.shape

    def body(g_ref, w_ref, m_ref, v_ref, d_ref, m2_ref, v2_ref):
        delta, m2, v2 = _adam_math(w_ref[...], g_ref[...], m_ref[...], v_ref[...])
        d_ref[...] = delta
        m2_ref[...] = m2
        v2_ref[...] = v2

    blk = pl.BlockSpec((r, c), lambda i: (0, 0))
    return pl.pallas_call(
        body, name=name, grid=(1,),
        in_specs=[blk, blk, blk, blk], out_specs=[blk, blk, blk],
        out_shape=[jax.ShapeDtypeStruct((r, c), F32)] * 3,
        compiler_params=_cp("arbitrary"),
    )(g, w, m, v)


def _my_place():
    return lax.axis_index("x"), lax.axis_index("y"), lax.axis_index("c")


def _flip(place, rel):
    x, y, c = place
    return (1 - x if rel & 4 else x, 1 - y if rel & 2 else y, 1 - c if rel & 1 else c)


def _index(place):
    return 4 * place[0] + 2 * place[1] + place[2]


def _all_gather(shards, name):
    na = len(shards)

    def body(*refs):
        xs, outs = refs[:na], refs[na:2 * na]
        send_sems, recv_sems, local_sems = refs[2 * na:]
        me = _my_place()
        sibling = _flip(me, 1)
        chips = [_flip(me, 4), _flip(me, 2), _flip(me, 6)]

        def copy(a, k, block, to, src=None):
            slot = outs[a].at[_index(block)]
            return pltpu.make_async_remote_copy(
                src_ref=slot if src is None else src, dst_ref=slot,
                send_sem=send_sems.at[a, k], recv_sem=recv_sems.at[a, k],
                device_id=to, device_id_type=MESH)

        mine = [pltpu.make_async_copy(xs[a], outs[a].at[_index(me)], local_sems.at[a]) for a in range(na)]
        for cp in mine:
            cp.start()
        first = []
        for a in range(na):
            first.append(copy(a, 0, me, sibling, src=xs[a]))
            first += [copy(a, 1 + j, me, chip, src=xs[a]) for j, chip in enumerate(chips)]
        for cp in first:
            cp.start()
        passed = []
        for a in range(na):
            for j, chip in enumerate(chips):
                copy(a, 1 + j, chip, me).wait_recv()
                fwd = copy(a, 4 + j, chip, sibling)
                fwd.start()
                passed.append(fwd)
        for a in range(na):
            copy(a, 0, sibling, me).wait_recv()
            for j, chip in enumerate(chips):
                copy(a, 4 + j, _flip(chip, 1), me).wait_recv()
        for cp in first + passed:
            cp.wait_send()
        for cp in mine:
            cp.wait()

    hbm = pl.BlockSpec(memory_space=pl.ANY)
    return pl.pallas_call(
        body, name=name,
        in_specs=[hbm] * na, out_specs=[hbm] * na,
        out_shape=[jax.ShapeDtypeStruct((N_DEV,) + s.shape, s.dtype) for s in shards],
        scratch_shapes=[pltpu.SemaphoreType.DMA((na, 7)), pltpu.SemaphoreType.DMA((na, 7)),
                        pltpu.SemaphoreType.DMA((na,))],
    )(*shards)


_HBM = pl.BlockSpec(memory_space=pltpu.HBM)
_SEM = pl.BlockSpec(memory_space=pltpu.SEMAPHORE)
_EFFECT = pltpu.SideEffectType.DATAFLOW_SIDE_EFFECTING


def _exchange_copy(src_ref, land_ref, send_sems, recv_sems, a, rel, me, scatter, landed):
    peer = _flip(me, rel)
    src = src_ref.at[_index(peer)] if scatter else src_ref
    return pltpu.make_async_remote_copy(
        src_ref=src, dst_ref=land_ref.at[_index(peer if landed else me)],
        send_sem=send_sems.at[a * (N_DEV - 1) + rel - 1], recv_sem=recv_sems.at[a * (N_DEV - 1) + rel - 1],
        device_id=peer, device_id_type=MESH)


def _own_slot(data, me, scatter):
    if scatter:
        own = lax.dynamic_slice_in_dim(data, me, 1, axis=0)
        shape = data.shape
    else:
        own = data[None]
        shape = (N_DEV,) + data.shape
    start = (me,) + (0,) * (len(shape) - 1)
    return lax.dynamic_update_slice(lax.empty(shape, data.dtype), own, start)


def _exchange_start(groups, me, scatter, name, after=None):
    sizes = [len(g) for g in groups]
    srcs = [a for g in groups for a in g]
    lands = [_own_slot(a, me, scatter) for a in srcs]
    na, ng = len(srcs), len(groups)
    deps = [] if after is None else [after]

    def body(*refs):
        src_refs, land_refs = refs[:na], refs[na:2 * na]
        sems = refs[2 * na + len(deps):2 * na + len(deps) + 2 * ng]
        token = refs[-1]
        place = _my_place()
        a = 0
        for g, size in enumerate(sizes):
            for k in range(size):
                for rel in range(1, N_DEV):
                    _exchange_copy(src_refs[a], land_refs[a], sems[2 * g], sems[2 * g + 1], k, rel, place, scatter,
                                   False).start()
                a += 1
        token[...] = jnp.zeros_like(token)

    sem_shapes = [pltpu.SemaphoreType.DMA((size * (N_DEV - 1),)) for size in sizes for _ in range(2)]
    outs = pl.pallas_call(
        body, name=name,
        in_specs=[_HBM] * (2 * na) + [pl.BlockSpec(memory_space=pl.ANY)] * len(deps),
        out_specs=[_SEM] * (2 * ng) + [_HBM] * (2 * na) + [pl.BlockSpec(memory_space=pltpu.VMEM)],
        out_shape=sem_shapes + [pltpu.HBM(a.shape, a.dtype) for a in srcs + lands]
        + [jax.ShapeDtypeStruct((SUBLANES, LANES), F32)],
        input_output_aliases={i: 2 * ng + i for i in range(2 * na)},
        compiler_params=pltpu.CompilerParams(has_side_effects=_EFFECT),
    )(*[pltpu.with_memory_space_constraint(a, pltpu.HBM) for a in srcs + lands], *deps)
    sems, thru, token = outs[:2 * ng], outs[2 * ng:2 * ng + 2 * na], outs[-1]
    handles, a = [], 0
    for g, size in enumerate(sizes):
        handles.append((sems[2 * g], sems[2 * g + 1], thru[a:a + size], thru[na + a:na + a + size]))
        a += size
    return handles, token


def _exchange_wait(handle, after, scatter, name):
    send_sems, recv_sems, srcs, lands = handle
    na = len(srcs)

    def body(*refs):
        src_refs, land_refs = refs[:na], refs[na:2 * na]
        send_ref, recv_ref = refs[2 * na], refs[2 * na + 1]
        place = _my_place()
        for a in range(na):
            for rel in range(1, N_DEV):
                cp = _exchange_copy(src_refs[a], land_refs[a], send_ref, recv_ref, a, rel, place, scatter, True)
                cp.wait_send()
                cp.wait_recv()

    outs = pl.pallas_call(
        body, name=name,
        in_specs=[_HBM] * (2 * na) + [_SEM, _SEM, pl.BlockSpec(memory_space=pl.ANY)],
        out_specs=[_HBM] * (2 * na),
        out_shape=[pltpu.HBM(a.shape, a.dtype) for a in list(srcs) + list(lands)],
        input_output_aliases={i: i for i in range(2 * na)},
        compiler_params=pltpu.CompilerParams(has_side_effects=_EFFECT),
    )(*srcs, *lands, send_sems, recv_sems, after)
    return outs[na:]


def _behind(arr, token):
    return arr + token[0:1, 0:1]


def _all_reduce_small(g, name):
    _, r, c = g.shape

    def body(g_ref, o_ref, land_ref, red_ref, send1, recv1, send2, recv2):
        me = _my_place()
        idx = _index(me)

        def scatter(rel):
            peer = _flip(me, rel)
            return pltpu.make_async_remote_copy(
                src_ref=g_ref.at[_index(peer)], dst_ref=land_ref.at[idx],
                send_sem=send1.at[rel - 1], recv_sem=recv1.at[rel - 1], device_id=peer, device_id_type=MESH)

        def gather(rel):
            peer = _flip(me, rel)
            return pltpu.make_async_remote_copy(
                src_ref=red_ref, dst_ref=o_ref.at[idx],
                send_sem=send2.at[rel - 1], recv_sem=recv2.at[rel - 1], device_id=peer, device_id_type=MESH)

        for rel in range(1, N_DEV):
            scatter(rel).start()
        land_ref[idx] = g_ref[idx]
        for rel in range(1, N_DEV):
            scatter(rel).wait()
        acc = land_ref[0]
        for s in range(1, N_DEV):
            acc = acc + land_ref[s]
        red_ref[...] = acc
        for rel in range(1, N_DEV):
            gather(rel).start()
        o_ref[idx] = acc
        for rel in range(1, N_DEV):
            gather(rel).wait()

    vmem = pl.BlockSpec(memory_space=pltpu.VMEM)
    return pl.pallas_call(
        body, name=name,
        in_specs=[vmem], out_specs=vmem,
        out_shape=jax.ShapeDtypeStruct(g.shape, F32),
        scratch_shapes=[pltpu.VMEM(g.shape, F32), pltpu.VMEM((r, c), F32)]
        + [pltpu.SemaphoreType.DMA((N_DEV - 1,))] * 4,
        compiler_params=pltpu.CompilerParams(vmem_limit_bytes=VMEM_LIMIT),
    )(g)


def _ssm_discretize(a_re, a_im, log_dt, b_re, b_im):
    dt = jnp.exp(log_dt)[:, None]
    mag = jnp.exp(a_re * dt)
    lr, li = mag * jnp.cos(a_im * dt), mag * jnp.sin(a_im * dt)
    den = a_re * a_re + a_im * a_im
    qr = ((lr - 1.0) * a_re + li * a_im) / den
    qi = (li * a_re - (lr - 1.0) * a_im) / den
    bbr = qr[..., None] * b_re - qi[..., None] * b_im
    bbi = qr[..., None] * b_im + qi[..., None] * b_re
    return lr, li, bbr, bbi


def _halves(a):
    return a.reshape((2, HALF_GROUPS) + a.shape[1:])


def _block_diag(blocks):
    g, r, c = blocks.shape
    eye = jnp.eye(g, dtype=blocks.dtype)
    return jnp.einsum("grc,gh->grhc", blocks, eye).reshape(g * r, g * c)


def _block_diag_take(dense, g, r, c):
    return jnp.einsum("grhc,gh->grc", dense.reshape(g, r, g, c), jnp.eye(g, dtype=dense.dtype))


def _ssm_matrices(bbr, bbi, c_re, c_im, glu_w, glu_b, d_skip):
    bre, bim = _halves(jnp.swapaxes(bbr, 1, 2)), _halves(jnp.swapaxes(bbi, 1, 2))
    bblk = jnp.stack([jnp.concatenate([_block_diag(bre[h]), _block_diag(bim[h])], axis=1) for h in range(2)])
    cre, cim = _halves(jnp.swapaxes(c_re, 1, 2)), _halves(jnp.swapaxes(c_im, 1, 2))
    cblk = jnp.stack([jnp.concatenate([_block_diag(cre[h]), -_block_diag(cim[h])], axis=0) for h in range(2)])
    glu = jnp.concatenate([_block_diag(glu_w[:, :, :SSM_CH]), _block_diag(glu_w[:, :, SSM_CH:])], axis=1)
    glub = jnp.concatenate([glu_b[:, :SSM_CH].reshape(1, -1), glu_b[:, SSM_CH:].reshape(1, -1)], axis=1)
    return bblk.astype(BF16), cblk.astype(BF16), glu.astype(BF16), glub, d_skip.reshape(1, -1)


def _scan_constants(lr, li, reverse):
    if reverse:
        li = -li
    pows = [(lr, li)]
    for _ in range(SUBLANES - 1):
        pr, pi = pows[-1]
        pows.append((pr * lr - pi * li, pr * li + pi * lr))
    row = jnp.arange(SUBLANES)[:, None]

    def flat(a):
        return a.reshape(2, 1, HALF_ST)

    mats = []
    for s in (1, 2, 4):
        keep = (row + s <= SUBLANES - 1) if reverse else (row >= s)
        mats.append(tuple(jnp.where(keep[None], flat(p), 0.0) for p in pows[s - 1]))
    order = [SUBLANES - 1 - j for j in range(SUBLANES)] if reverse else list(range(SUBLANES))
    mats.append(tuple(jnp.concatenate([flat(pows[j][k]) for j in order], axis=1) for k in range(2)))
    return jnp.stack([jnp.concatenate([m[0], m[1]], axis=2) for m in mats], axis=1)


def _pack(arrs, rows):
    flat = jnp.concatenate([a.reshape(-1) for a in arrs])
    return jnp.pad(flat, (0, rows * LANES - flat.shape[0])).reshape(rows, LANES)


def _unpack(buf, like):
    flat = buf.reshape(-1)
    out, off = [], 0
    for a in like:
        out.append(flat[off:off + a.size].reshape(a.shape))
        off += a.size
    return out


SMALL = ("norm_ffn1", "norm_mix", "ssm_a_re", "ssm_a_im", "ssm_log_dt", "ssm_b_re", "ssm_b_im", "ssm_c_re",
         "ssm_c_im", "ssm_d", "ssm_glu_w", "ssm_glu_b", "gm_v_gain", "gm_w_s", "gm_b_s", "gain_ssm_out",
         "gain_gm_out", "norm_ffn2", "norm_final")
BIG = ("ffn1_w_in", "ffn1_w_out", "mix_w_in", "mix_w_out", "ffn2_w_in", "ffn2_w_out")
WEIGHTS = ("norm_ffn1", "ffn1_w_in", "ffn1_w_out", "norm_mix", "mix_w_in", "ssm_a_re", "ssm_a_im", "ssm_log_dt",
           "ssm_b_re", "ssm_b_im", "ssm_c_re", "ssm_c_im", "ssm_d", "ssm_glu_w", "ssm_glu_b", "gm_v_gain", "gm_w_s",
           "gm_b_s", "gain_ssm_out", "gain_gm_out", "mix_w_out", "norm_ffn2", "ffn2_w_in", "ffn2_w_out", "norm_final")


def _step(x, target, w, m, v):
    batch, seq, _ = x.shape
    n = batch * seq
    depth = w["norm_ffn1"].shape[0]
    tm = min(512, n)
    tm_ffn = min(1024, n)
    tk = min(2048, n)
    t_chunk = min(256, seq)
    gm_rows = min(512, seq)
    x = x.reshape(n, D_MODEL)
    target = target.reshape(n, D_MODEL)

    assert depth == 2
    me = _index(_my_place())
    shard = lambda group, l: [w[f"{group}_w_in"][l].astype(BF16), w[f"{group}_w_out"][l].astype(BF16)]
    order = [(g, l) for l in range(depth) for g in ("ffn1", "mix", "ffn2")]
    gathered, pending = {}, {}

    def gather_start(i, after):
        batch_i = order[2 * i:2 * i + 2]
        handles, tok = _exchange_start([shard(g, l) for g, l in batch_i], me, False, f"all_gather_start_{i}", after)
        pending.update(zip(batch_i, handles))
        return tok

    gather_start(0, None)

    def weights(group, l, after=None):
        if (group, l) not in gathered:
            w_in, w_out = _exchange_wait(pending[(group, l)], after, False, f"all_gather_wait_{group}_{l}")
            if group == "mix":
                w_in = jnp.transpose(w_in, (1, 0, 2)).reshape(D_MODEL, IN_COLS)
                w_out = w_out.reshape(D_MODEL, D_MODEL)
            gathered[(group, l)] = (w_in, w_out)
        return gathered[(group, l)]

    tril = jnp.tril(jnp.ones((GM_CHUNK, GM_CHUNK), bool))
    layers = []
    for l in range(depth):
        disc, disc_vjp = jax.vjp(_ssm_discretize, w["ssm_a_re"][l], w["ssm_a_im"][l], w["ssm_log_dt"][l],
                                 w["ssm_b_re"][l], w["ssm_b_im"][l])
        lr, li, bbr, bbi = disc
        bblk, cblk, glu, glub, dskip = _ssm_matrices(bbr, bbi, w["ssm_c_re"][l], w["ssm_c_im"][l],
                                                     w["ssm_glu_w"][l], w["ssm_glu_b"][l], w["ssm_d"][l])
        layers.append(dict(
            disc_vjp=disc_vjp, lr=lr, li=li, bblk=bblk, cblk=cblk, glu=glu, glub=glub, dskip=dskip,
            fwdc=_scan_constants(lr, li, False), revc=_scan_constants(lr, li, True),
            w_tril=jnp.where(tril[None], w["gm_w_s"][l], 0.0).astype(BF16),
            gm_bias=jnp.repeat(w["gm_b_s"][l].T, GM_HEAD_DIM, axis=1),
            g1=w["norm_ffn1"][l][None], gmix=w["norm_mix"][l][None], g2=w["norm_ffn2"][l][None],
            gv=w["gm_v_gain"][l][None], gs=w["gain_ssm_out"][l][None], gg=w["gain_gm_out"][l][None],
        ))

    saved = []
    for l in range(depth):
        p = layers[l]
        x0 = x
        g1, gmix = p["g1"], p["gmix"]
        if l == 0:
            w_in, w_out = weights("ffn1", l, layers[-1]["revc"])
            g1 = _behind(g1, gather_start(1, w_in))
        else:
            w_in, w_out = weights("ffn1", l, x0)
        x1, xn1, gu1 = _ffn_fwd(x0, g1, w_in, w_out, tm_ffn, f"ffn1_fwd_{l}")
        if l == 0:
            gmix = _behind(gmix, gather_start(2, x1))
        mwi, mwo = weights("mix", l, x1)
        z = _mix_in_fwd(x1, gmix, mwi, tm, f"mix_in_fwd_{l}")
        y_ssm, h = _ssm_fwd(z, p["bblk"], p["cblk"], p["glu"], p["glub"], p["dskip"], p["fwdc"], batch, t_chunk,
                            f"ssm_fwd_{l}")
        y_gm = _gm_fwd(z, p["gv"], p["w_tril"], p["gm_bias"], gm_rows, f"gm_fwd_{l}")
        x2 = _mix_out_fwd(y_ssm, y_gm, p["gs"], p["gg"], mwo, x1, tm, f"mix_out_fwd_{l}")
        x, xn2, gu2 = _ffn_fwd(x2, p["g2"], *weights("ffn2", l, x2), tm_ffn, f"ffn2_fwd_{l}")
        saved.append((x0, x1, x2, z, h, y_ssm, y_gm, xn1, gu1, xn2, gu2))

    dx, sq, dnf = _loss_head(x, w["norm_final"][None], target, tm, "loss_head")
    loss = lax.psum((0.5 / D_MODEL) * jnp.sum(sq), AXES)

    small = {k: [None] * depth for k in SMALL if k != "norm_final"}
    sent = []

    def send(group, l, keys, parts):
        (handle,), tok = _exchange_start([parts], me, True, f"reduce_scatter_start_{group}_{l}")
        sent.append((group, l, keys, handle))
        return tok

    token = None
    for l in reversed(range(depth)):
        p = layers[l]
        x0, x1, x2, z, h, y_ssm, y_gm, xn1, gu1, xn2, gu2 = saved[l]
        mwi, mwo = weights("mix", l)
        dx_out = dx
        g2 = p["g2"] if token is None else _behind(p["g2"], token)
        dx, dgu, act, dgain = _ffn_bwd(x2, g2, dx_out, gu2, *weights("ffn2", l), tm, f"ffn2_bwd_{l}")
        dw_in = _ffn_dw_in(xn2, dgu, tk, f"ffn2_dw_in_{l}")
        dw_out = _ffn_dw_out(act, dx_out, tk, f"ffn2_dw_out_{l}").reshape(N_DEV, FF_SHARD // 2, D_MODEL)
        token = send("ffn2", l, ("ffn2_w_in", "ffn2_w_out"), [dw_in, dw_out])
        small["norm_ffn2"][l] = dgain.sum(0)

        dy_ssm, dy_gm, dwo, dgs, dgg = _mix_out_bwd(y_ssm, y_gm, _behind(p["gs"], token), p["gg"], mwo, dx, tm,
                                                    f"mix_out_bwd_{l}")
        dwo = dwo.astype(BF16).reshape(N_DEV, D_MODEL // N_DEV, D_MODEL)
        small["gain_ssm_out"][l] = dgs.sum(0)
        small["gain_gm_out"][l] = dgg.sum(0)

        du_ssm, dglu, dglub, ddskip, dct, db, q = _ssm_bwd(
            z, h, dy_ssm, p["bblk"], p["cblk"], p["glu"], p["glub"], p["dskip"], p["revc"], batch, t_chunk,
            f"ssm_bwd_{l}")
        du_gm, dv_gm, dws, dbias, dgv = _gm_bwd(z, dy_gm, p["gv"], p["w_tril"], p["gm_bias"], gm_rows, f"gm_bwd_{l}")

        q = q.sum(0).reshape(2, 2, HALF_GROUPS, SSM_STATE)
        qr, qi = q[:, 0].reshape(SSM_GROUPS, SSM_STATE), q[:, 1].reshape(SSM_GROUPS, SSM_STATE)
        den = p["lr"] * p["lr"] + p["li"] * p["li"]
        d_re = (qr * p["lr"] + qi * p["li"]) / den
        d_im = (qi * p["lr"] - qr * p["li"]) / den
        dbb = jnp.stack([_block_diag_take(db[hf, :, k * HALF_ST:(k + 1) * HALF_ST], HALF_GROUPS, SSM_CH, SSM_STATE)
                         for k in range(2) for hf in range(2)]).reshape(2, SSM_GROUPS, SSM_CH, SSM_STATE)
        dcc = jnp.stack([_block_diag_take(dct[hf, :, k * HALF_ST:(k + 1) * HALF_ST], HALF_GROUPS, SSM_CH, SSM_STATE)
                         for k in range(2) for hf in range(2)]).reshape(2, SSM_GROUPS, SSM_CH, SSM_STATE)
        da_re, da_im, dlog_dt, db_re, db_im = p["disc_vjp"](
            (d_re, -d_im, jnp.swapaxes(dbb[0], 1, 2), jnp.swapaxes(dbb[1], 1, 2)))
        small["ssm_a_re"][l], small["ssm_a_im"][l], small["ssm_log_dt"][l] = da_re, da_im, dlog_dt
        small["ssm_b_re"][l], small["ssm_b_im"][l] = db_re, db_im
        small["ssm_c_re"][l], small["ssm_c_im"][l] = dcc[0], -dcc[1]
        small["ssm_d"][l] = ddskip.sum(0).reshape(SSM_GROUPS, SSM_CH)
        small["ssm_glu_w"][l] = jnp.concatenate(
            [_block_diag_take(dglu[:, :SSM_WIDTH], SSM_GROUPS, SSM_CH, SSM_CH),
             _block_diag_take(dglu[:, SSM_WIDTH:], SSM_GROUPS, SSM_CH, SSM_CH)], axis=2)
        dglub = dglub.sum(0)
        small["ssm_glu_b"][l] = jnp.concatenate(
            [dglub[:SSM_WIDTH].reshape(SSM_GROUPS, SSM_CH), dglub[SSM_WIDTH:].reshape(SSM_GROUPS, SSM_CH)], axis=1)
        small["gm_v_gain"][l] = dgv.sum(0)
        small["gm_w_s"][l] = jnp.where(tril[None], dws, 0.0)
        small["gm_b_s"][l] = dbias.reshape(GM_CHUNK, GM_HEADS, GM_HEAD_DIM).sum(-1).T

        dx, dwi, dgain = _mix_in_bwd(x1, p["gmix"], du_ssm, du_gm, dv_gm, dx, mwi, tm, f"mix_in_bwd_{l}")
        dwi = jnp.transpose(dwi.astype(BF16).reshape(D_MODEL, N_DEV, IN_COLS // N_DEV), (1, 0, 2))
        token = send("mix", l, ("mix_w_in", "mix_w_out"), [dwi, dwo])
        small["norm_mix"][l] = dgain.sum(0)

        dx_out = dx
        dx, dgu, act, dgain = _ffn_bwd(x0, _behind(p["g1"], token), dx_out, gu1, *weights("ffn1", l), tm,
                                       f"ffn1_bwd_{l}")
        small["norm_ffn1"][l] = dgain.sum(0)
        if l > 0:
            dw_in = _ffn_dw_in(xn1, dgu, tk, f"ffn1_dw_in_{l}")
            dw_out = _ffn_dw_out(act, dx_out, tk, f"ffn1_dw_out_{l}").reshape(N_DEV, FF_SHARD // 2, D_MODEL)
            token = send("ffn1", l, ("ffn1_w_in", "ffn1_w_out"), [dw_in, dw_out])
            continue
        small_g = [jnp.stack(small[k]) if k != "norm_final" else dnf.sum(0) for k in SMALL]
        total = sum(int(math.prod(w[k].shape)) for k in SMALL)
        rows = -(-total // (LANES * N_DEV * SUBLANES)) * N_DEV * SUBLANES
        g_all = _all_reduce_small(_pack(small_g, rows).reshape(N_DEV, rows // N_DEV, LANES), "all_reduce_small")
        dw_in = _ffn_dw_in(xn1, dgu, tk, f"ffn1_dw_in_{l}", after=g_all)
        token = send("ffn1_in", l, ("ffn1_w_in",), [dw_in])
        dw_out = _ffn_dw_out(act, dx_out, tk, f"ffn1_dw_out_{l}", after=token).reshape(
            N_DEV, FF_SHARD // 2, D_MODEL)
        token = send("ffn1_out", l, ("ffn1_w_out",), [dw_out])

    grad_x = dx.reshape(batch, seq, D_MODEL)
    grads, deltas, new_m, new_v = {}, {}, {}, {}

    g_all = _behind(g_all.reshape(rows, LANES), token)
    like = [w[k] for k in SMALL]
    d_p, m_p, v_p = _adam_packed(g_all, _pack(like, rows), _pack([m[k] for k in SMALL], rows),
                                 _pack([v[k] for k in SMALL], rows), "adam_small")
    for k, g_, d_, m_, v_ in zip(SMALL, _unpack(g_all, like), _unpack(d_p, like), _unpack(m_p, like),
                                 _unpack(v_p, like)):
        grads[k], deltas[k], new_m[k], new_v[k] = g_, d_, m_, v_

    results = {}
    after = d_p
    for group, l, keys, handle in sent:
        landed = _exchange_wait(handle, after, True, f"reduce_scatter_wait_{group}_{l}")
        for k, parts in zip(keys, landed):
            results[k] = _adam_sharded(parts, w[k], m[k], v[k], l, results.get(k), f"adam_{k}_{l}")
            after = results[k][0]
    for k in BIG:
        grads[k], deltas[k], new_m[k], new_v[k] = results[k]
    return loss, grad_x, grads, deltas, new_m, new_v


def kernel(x, norm_ffn1, ffn1_w_in, ffn1_w_out, norm_mix, mix_w_in, ssm_a_re, ssm_a_im, ssm_log_dt, ssm_b_re, ssm_b_im, ssm_c_re, ssm_c_im, ssm_d, ssm_glu_w, ssm_glu_b, gm_v_gain, gm_w_s, gm_b_s, gain_ssm_out, gain_gm_out, mix_w_out, norm_ffn2, ffn2_w_in, ffn2_w_out, norm_final, loss_target, m_norm_ffn1, m_ffn1_w_in, m_ffn1_w_out, m_norm_mix, m_mix_w_in, m_ssm_a_re, m_ssm_a_im, m_ssm_log_dt, m_ssm_b_re, m_ssm_b_im, m_ssm_c_re, m_ssm_c_im, m_ssm_d, m_ssm_glu_w, m_ssm_glu_b, m_gm_v_gain, m_gm_w_s, m_gm_b_s, m_gain_ssm_out, m_gain_gm_out, m_mix_w_out, m_norm_ffn2, m_ffn2_w_in, m_ffn2_w_out, m_norm_final, v_norm_ffn1, v_ffn1_w_in, v_ffn1_w_out, v_norm_mix, v_mix_w_in, v_ssm_a_re, v_ssm_a_im, v_ssm_log_dt, v_ssm_b_re, v_ssm_b_im, v_ssm_c_re, v_ssm_c_im, v_ssm_d, v_ssm_glu_w, v_ssm_glu_b, v_gm_v_gain, v_gm_w_s, v_gm_b_s, v_gain_ssm_out, v_gain_gm_out, v_mix_w_out, v_norm_ffn2, v_ffn2_w_in, v_ffn2_w_out, v_norm_final):
    args = locals()
    w = {k: args[k] for k in WEIGHTS}
    m = {k: args["m_" + k] for k in WEIGHTS}
    v = {k: args["v_" + k] for k in WEIGHTS}
    loss, grad_x, grads, deltas, new_m, new_v = _step(x, loss_target, w, m, v)
    return (loss, grad_x, *[grads[k] for k in WEIGHTS], *[deltas[k] for k in WEIGHTS],
            *[new_m[k] for k in WEIGHTS], *[new_v[k] for k in WEIGHTS])
```

```python
import functools
import math

import jax
import jax.numpy as jnp
from jax import lax
from jax.experimental import pallas as pl
from jax.experimental.pallas import tpu as pltpu

F32 = jnp.float32
BF16 = jnp.bfloat16
MESH = pl.DeviceIdType.MESH
AXES = ("x", "y", "c")

N_DEV = 8
D_MODEL = 1024
D_FF = 2816
FF_SHARD = 2 * D_FF // N_DEV
FF_CHUNKS = 4
MXU_DIM = 256
FF_PIECES = tuple((lo, min(lo + MXU_DIM, FF_SHARD)) for lo in range(0, FF_SHARD, MXU_DIM))
SSM_WIDTH = 512
SSM_CH = 16
SSM_GROUPS = 32
SSM_STATE = 64
HALF_GROUPS = 16
HALF_IN = HALF_GROUPS * SSM_CH
HALF_ST = HALF_GROUPS * SSM_STATE
GM_WIDTH = 512
GM_HEADS = 4
GM_HEAD_DIM = 128
GM_CHUNK = 128
IN_COLS = SSM_WIDTH + 2 * GM_WIDTH
EPS = 1e-6
SUBLANES = 8
LANES = 128

ADAM_LR = 0.001
ADAM_B1 = 0.9
ADAM_B2 = 0.999
ADAM_EPS = 1e-08
ADAM_WD = 0.01
ADAM_STEP = 10

VMEM_LIMIT = 46 * 1024 * 1024


def _cp(*sem):
    return pltpu.CompilerParams(dimension_semantics=sem, vmem_limit_bytes=VMEM_LIMIT)


def _rms_fwd(x, g):
    r = lax.rsqrt(jnp.mean(x * x, axis=-1, keepdims=True) + EPS)
    xh = x * r
    return xh * g, xh, r


def _rms_bwd(dy, xh, r, g):
    dxh = dy * g
    dx = r * (dxh - xh * jnp.mean(dxh * xh, axis=-1, keepdims=True))
    return dx, dy * xh


def _rows8(a):
    m, n = a.shape
    return a.reshape(m // SUBLANES, SUBLANES, n).sum(axis=0)


_GELU_K = math.sqrt(2.0 / math.pi)
_GELU_C = 0.044715


def _gelu(x):
    th = jnp.tanh(_GELU_K * (x + _GELU_C * x * x * x))
    return 0.5 * x * (1.0 + th), th


def _gelu_grad(x, th):
    return 0.5 * (1.0 + th) + 0.5 * x * (1.0 - th * th) * (_GELU_K * (1.0 + 3.0 * _GELU_C * x * x))


def _dot(a, b):
    return jnp.dot(a, b, preferred_element_type=F32)


def _dot_nt(a, b):
    return lax.dot_general(a, b, (((1,), (1,)), ((), ())), preferred_element_type=F32)


def _dot_tn(a, b):
    return lax.dot_general(a, b, (((0,), (0,)), ((), ())), preferred_element_type=F32)


def _ffn_fwd(x, gain, w_in_ag, w_out_ag, tm, name):
    n = x.shape[0]

    def body(x_ref, g_ref, wg_ref, wu_ref, wo_ref, o_ref, xn_ref, gu_ref):
        j = pl.program_id(1)

        @pl.when(j == 0)
        def _():
            xv = x_ref[...]
            y, _, _ = _rms_fwd(xv, g_ref[...])
            xn_ref[...] = y.astype(BF16)
            o_ref[...] = xv

        xn = xn_ref[...]
        wo = wo_ref[...].reshape(FF_SHARD, D_MODEL)
        out = None
        for lo, hi in FF_PIECES:
            gg = _dot(xn, wg_ref[:, lo:hi])
            uu = _dot(xn, wu_ref[:, lo:hi])
            gu_ref[0, :, lo:hi] = gg.astype(BF16)
            gu_ref[1, :, lo:hi] = uu.astype(BF16)
            act = (gg * jax.nn.sigmoid(gg) * uu).astype(BF16)
            part = _dot(act, wo[lo:hi, :])
            out = part if out is None else out + part
        o_ref[...] += 0.5 * out

    return pl.pallas_call(
        body, name=name, grid=(n // tm, FF_CHUNKS),
        in_specs=[
            pl.BlockSpec((tm, D_MODEL), lambda i, j: (i, 0)),
            pl.BlockSpec((1, D_MODEL), lambda i, j: (0, 0)),
            pl.BlockSpec((None, D_MODEL, FF_SHARD), lambda i, j: (j, 0, 0)),
            pl.BlockSpec((None, D_MODEL, FF_SHARD), lambda i, j: (j + FF_CHUNKS, 0, 0)),
            pl.BlockSpec((2, FF_SHARD // 2, D_MODEL), lambda i, j: (j, 0, 0)),
        ],
        out_specs=[
            pl.BlockSpec((tm, D_MODEL), lambda i, j: (i, 0)),
            pl.BlockSpec((tm, D_MODEL), lambda i, j: (i, 0)),
            pl.BlockSpec((None, 2, tm, FF_SHARD), lambda i, j: (j, 0, i, 0)),
        ],
        out_shape=[
            jax.ShapeDtypeStruct((n, D_MODEL), F32),
            jax.ShapeDtypeStruct((n, D_MODEL), BF16),
            jax.ShapeDtypeStruct((FF_CHUNKS, 2, n, FF_SHARD), BF16),
        ],
        compiler_params=_cp("parallel", "arbitrary"),
    )(x, gain, w_in_ag, w_in_ag, w_out_ag)


def _ffn_bwd(x, gain, dy, gu, w_in_ag, w_out_ag, tm, name):
    n = x.shape[0]

    def body(x_ref, g_ref, dy_ref, gu_ref, wg_ref, wu_ref, wo_ref, dx_ref, dgu_ref, act_ref, dgain_ref, dyb_ref):
        i, j = pl.program_id(0), pl.program_id(1)

        @pl.when(jnp.logical_and(i == 0, j == 0))
        def _():
            dgain_ref[...] = jnp.zeros_like(dgain_ref)

        @pl.when(j == 0)
        def _():
            dyb_ref[...] = dy_ref[...].astype(BF16)
            dx_ref[...] = jnp.zeros_like(dx_ref)

        dyb = dyb_ref[...]
        wo = wo_ref[...].reshape(FF_SHARD, D_MODEL)
        for lo, hi in FF_PIECES:
            wg, wu = wg_ref[:, lo:hi], wu_ref[:, lo:hi]
            gg = gu_ref[0, :, lo:hi].astype(F32)
            uu = gu_ref[1, :, lo:hi].astype(F32)
            dact = 0.5 * _dot_nt(dyb, wo[lo:hi, :])
            sig = jax.nn.sigmoid(gg)
            silu = gg * sig
            act_ref[:, lo:hi] = (silu * uu).astype(BF16)
            du = (dact * silu).astype(BF16)
            dg = (dact * uu * (sig * (1.0 + gg * (1.0 - sig)))).astype(BF16)
            dgu_ref[0, :, lo:hi] = dg
            dgu_ref[1, :, lo:hi] = du
            dx_ref[...] += _dot_nt(dg, wg) + _dot_nt(du, wu)

        @pl.when(j == FF_CHUNKS - 1)
        def _():
            g = g_ref[...]
            _, xh, r = _rms_fwd(x_ref[...], g)
            dx, dgr = _rms_bwd(dx_ref[...], xh, r, g)
            dx_ref[...] = dy_ref[...] + dx
            dgain_ref[...] += _rows8(dgr)

    return pl.pallas_call(
        body, name=name, grid=(n // tm, FF_CHUNKS),
        in_specs=[
            pl.BlockSpec((tm, D_MODEL), lambda i, j: (i, 0)),
            pl.BlockSpec((1, D_MODEL), lambda i, j: (0, 0)),
            pl.BlockSpec((tm, D_MODEL), lambda i, j: (i, 0)),
            pl.BlockSpec((None, 2, tm, FF_SHARD), lambda i, j: (j, 0, i, 0)),
            pl.BlockSpec((None, D_MODEL, FF_SHARD), lambda i, j: (j, 0, 0)),
            pl.BlockSpec((None, D_MODEL, FF_SHARD), lambda i, j: (j + FF_CHUNKS, 0, 0)),
            pl.BlockSpec((2, FF_SHARD // 2, D_MODEL), lambda i, j: (j, 0, 0)),
        ],
        out_specs=[
            pl.BlockSpec((tm, D_MODEL), lambda i, j: (i, 0)),
            pl.BlockSpec((None, 2, tm, FF_SHARD), lambda i, j: (j, 0, i, 0)),
            pl.BlockSpec((None, tm, FF_SHARD), lambda i, j: (j, i, 0)),
            pl.BlockSpec((SUBLANES, D_MODEL), lambda i, j: (0, 0)),
        ],
        out_shape=[
            jax.ShapeDtypeStruct((n, D_MODEL), F32),
            jax.ShapeDtypeStruct((FF_CHUNKS, 2, n, FF_SHARD), BF16),
            jax.ShapeDtypeStruct((FF_CHUNKS, n, FF_SHARD), BF16),
            jax.ShapeDtypeStruct((SUBLANES, D_MODEL), F32),
        ],
        scratch_shapes=[pltpu.VMEM((tm, D_MODEL), BF16)],
        compiler_params=_cp("arbitrary", "arbitrary"),
    )(x, gain, dy, gu, w_in_ag, w_in_ag, w_out_ag)


def _ffn_dw_in(xn, dgu, tk, name, after=None):
    n = xn.shape[0]
    nk = n // tk
    deps = [] if after is None else [after]

    def body(a_ref, b_ref, *rest):
        o_ref, acc_ref = rest[-2:]
        k = pl.program_id(2)

        @pl.when(k == 0)
        def _():
            acc_ref[...] = jnp.zeros_like(acc_ref)

        acc_ref[...] += _dot_tn(b_ref[...], a_ref[...])

        @pl.when(k == nk - 1)
        def _():
            o_ref[...] = acc_ref[...].astype(BF16)

    return pl.pallas_call(
        body, name=name, grid=(FF_CHUNKS, 2, nk),
        in_specs=[
            pl.BlockSpec((tk, D_MODEL), lambda j, p, k: (k, 0)),
            pl.BlockSpec((None, None, tk, FF_SHARD), lambda j, p, k: (j, p, k, 0)),
        ] + [pl.BlockSpec(memory_space=pl.ANY)] * len(deps),
        out_specs=pl.BlockSpec((None, FF_SHARD, D_MODEL), lambda j, p, k: (FF_CHUNKS * p + j, 0, 0)),
        out_shape=jax.ShapeDtypeStruct((N_DEV, FF_SHARD, D_MODEL), BF16),
        scratch_shapes=[pltpu.VMEM((FF_SHARD, D_MODEL), F32)],
        compiler_params=_cp("parallel", "parallel", "arbitrary"),
    )(xn, dgu, *deps)


def _ffn_dw_out(act, dy, tk, name, after=None):
    n = act.shape[1]
    nk = n // tk
    deps = [] if after is None else [after]

    def body(a_ref, b_ref, *rest):
        o_ref, acc_ref = rest[-2:]
        k = pl.program_id(1)

        @pl.when(k == 0)
        def _():
            acc_ref[...] = jnp.zeros_like(acc_ref)

        acc_ref[...] += _dot_tn(a_ref[...], b_ref[...].astype(BF16))

        @pl.when(k == nk - 1)
        def _():
            o_ref[...] = (0.5 * acc_ref[...]).astype(BF16)

    return pl.pallas_call(
        body, name=name, grid=(FF_CHUNKS, nk),
        in_specs=[
            pl.BlockSpec((None, tk, FF_SHARD), lambda j, k: (j, k, 0)),
            pl.BlockSpec((tk, D_MODEL), lambda j, k: (k, 0)),
        ] + [pl.BlockSpec(memory_space=pl.ANY)] * len(deps),
        out_specs=pl.BlockSpec((None, FF_SHARD, D_MODEL), lambda j, k: (j, 0, 0)),
        out_shape=jax.ShapeDtypeStruct((FF_CHUNKS, FF_SHARD, D_MODEL), BF16),
        scratch_shapes=[pltpu.VMEM((FF_SHARD, D_MODEL), F32)],
        compiler_params=_cp("parallel", "arbitrary"),
    )(act, dy, *deps)


def _mix_in_fwd(x, gain, w, tm, name):
    n = x.shape[0]

    def body(x_ref, g_ref, w_ref, z_ref):
        y, _, _ = _rms_fwd(x_ref[...], g_ref[...])
        z_ref[...] = _dot(y.astype(BF16), w_ref[...])

    return pl.pallas_call(
        body, name=name, grid=(n // tm,),
        in_specs=[
            pl.BlockSpec((tm, D_MODEL), lambda i: (i, 0)),
            pl.BlockSpec((1, D_MODEL), lambda i: (0, 0)),
            pl.BlockSpec((D_MODEL, IN_COLS), lambda i: (0, 0)),
        ],
        out_specs=pl.BlockSpec((tm, IN_COLS), lambda i: (i, 0)),
        out_shape=jax.ShapeDtypeStruct((n, IN_COLS), F32),
        compiler_params=_cp("parallel"),
    )(x, gain, w)


def _mix_in_bwd(x, gain, du_ssm, du_gm, dv_gm, d_res, w, tm, name):
    n = x.shape[0]

    def body(x_ref, g_ref, d0_ref, d1_ref, d2_ref, dres_ref, w_ref, dx_ref, dw_ref, dgain_ref):
        i = pl.program_id(0)

        @pl.when(i == 0)
        def _():
            dw_ref[...] = jnp.zeros_like(dw_ref)
            dgain_ref[...] = jnp.zeros_like(dgain_ref)

        g = g_ref[...]
        y, xh, r = _rms_fwd(x_ref[...], g)
        xn = y.astype(BF16)
        dxn = jnp.zeros((tm, D_MODEL), F32)
        for k, d_ref in enumerate((d0_ref, d1_ref, d2_ref)):
            dz = d_ref[...].astype(BF16)
            cols = slice(k * SSM_WIDTH, (k + 1) * SSM_WIDTH)
            dxn += _dot_nt(dz, w_ref[:, cols])
            dw_ref[cols, :] += _dot_tn(dz, xn)
        dx, dgr = _rms_bwd(dxn, xh, r, g)
        dx_ref[...] = dres_ref[...] + dx
        dgain_ref[...] += _rows8(dgr)

    row = lambda i: (i, 0)
    fixed = lambda i: (0, 0)
    return pl.pallas_call(
        body, name=name, grid=(n // tm,),
        in_specs=[
            pl.BlockSpec((tm, D_MODEL), row),
            pl.BlockSpec((1, D_MODEL), fixed),
            pl.BlockSpec((tm, SSM_WIDTH), row),
            pl.BlockSpec((tm, GM_WIDTH), row),
            pl.BlockSpec((tm, GM_WIDTH), row),
            pl.BlockSpec((tm, D_MODEL), row),
            pl.BlockSpec((D_MODEL, IN_COLS), fixed),
        ],
        out_specs=[
            pl.BlockSpec((tm, D_MODEL), row),
            pl.BlockSpec((IN_COLS, D_MODEL), fixed),
            pl.BlockSpec((SUBLANES, D_MODEL), fixed),
        ],
        out_shape=[
            jax.ShapeDtypeStruct((n, D_MODEL), F32),
            jax.ShapeDtypeStruct((IN_COLS, D_MODEL), F32),
            jax.ShapeDtypeStruct((SUBLANES, D_MODEL), F32),
        ],
        compiler_params=_cp("arbitrary"),
    )(x, gain, du_ssm, du_gm, dv_gm, d_res, w)


def _mix_out_fwd(y_ssm, y_gm, g_ssm, g_gm, w, x, tm, name):
    n = x.shape[0]

    def body(ys_ref, yg_ref, gs_ref, gg_ref, w_ref, x_ref, o_ref):
        a, _, _ = _rms_fwd(ys_ref[...], gs_ref[...])
        b, _, _ = _rms_fwd(yg_ref[...], gg_ref[...])
        o_ref[...] = (x_ref[...] + _dot(a.astype(BF16), w_ref[0:SSM_WIDTH, :])
                      + _dot(b.astype(BF16), w_ref[SSM_WIDTH:D_MODEL, :]))

    row = lambda i: (i, 0)
    fixed = lambda i: (0, 0)
    return pl.pallas_call(
        body, name=name, grid=(n // tm,),
        in_specs=[
            pl.BlockSpec((tm, SSM_WIDTH), row), pl.BlockSpec((tm, GM_WIDTH), row),
            pl.BlockSpec((1, SSM_WIDTH), fixed), pl.BlockSpec((1, GM_WIDTH), fixed),
            pl.BlockSpec((D_MODEL, D_MODEL), fixed), pl.BlockSpec((tm, D_MODEL), row),
        ],
        out_specs=pl.BlockSpec((tm, D_MODEL), row),
        out_shape=jax.ShapeDtypeStruct((n, D_MODEL), F32),
        compiler_params=_cp("parallel"),
    )(y_ssm, y_gm, g_ssm, g_gm, w, x)


def _mix_out_bwd(y_ssm, y_gm, g_ssm, g_gm, w, dx, tm, name):
    n = dx.shape[0]

    def body(ys_ref, yg_ref, gs_ref, gg_ref, w_ref, dx_ref, dys_ref, dyg_ref, dw_ref, dgs_ref, dgg_ref):
        i = pl.program_id(0)

        @pl.when(i == 0)
        def _():
            dw_ref[...] = jnp.zeros_like(dw_ref)
            dgs_ref[...] = jnp.zeros_like(dgs_ref)
            dgg_ref[...] = jnp.zeros_like(dgg_ref)

        dxb = dx_ref[...].astype(BF16)
        parts = ((ys_ref, gs_ref, dys_ref, dgs_ref, 0), (yg_ref, gg_ref, dyg_ref, dgg_ref, SSM_WIDTH))
        for y_ref, g_ref, dy_ref, dg_ref, off in parts:
            g = g_ref[...]
            yn, xh, r = _rms_fwd(y_ref[...], g)
            rows = slice(off, off + SSM_WIDTH)
            dyn = _dot_nt(dxb, w_ref[rows, :])
            dw_ref[rows, :] += _dot_tn(yn.astype(BF16), dxb)
            dy, dgr = _rms_bwd(dyn, xh, r, g)
            dy_ref[...] = dy
            dg_ref[...] += _rows8(dgr)

    row = lambda i: (i, 0)
    fixed = lambda i: (0, 0)
    return pl.pallas_call(
        body, name=name, grid=(n // tm,),
        in_specs=[
            pl.BlockSpec((tm, SSM_WIDTH), row), pl.BlockSpec((tm, GM_WIDTH), row),
            pl.BlockSpec((1, SSM_WIDTH), fixed), pl.BlockSpec((1, GM_WIDTH), fixed),
            pl.BlockSpec((D_MODEL, D_MODEL), fixed), pl.BlockSpec((tm, D_MODEL), row),
        ],
        out_specs=[
            pl.BlockSpec((tm, SSM_WIDTH), row), pl.BlockSpec((tm, GM_WIDTH), row),
            pl.BlockSpec((D_MODEL, D_MODEL), fixed),
            pl.BlockSpec((SUBLANES, SSM_WIDTH), fixed), pl.BlockSpec((SUBLANES, GM_WIDTH), fixed),
        ],
        out_shape=[
            jax.ShapeDtypeStruct((n, SSM_WIDTH), F32), jax.ShapeDtypeStruct((n, GM_WIDTH), F32),
            jax.ShapeDtypeStruct((D_MODEL, D_MODEL), F32),
            jax.ShapeDtypeStruct((SUBLANES, SSM_WIDTH), F32), jax.ShapeDtypeStruct((SUBLANES, GM_WIDTH), F32),
        ],
        compiler_params=_cp("arbitrary"),
    )(y_ssm, y_gm, g_ssm, g_gm, w, dx)


SCAN_W = 512
SCAN_PIECES = HALF_ST // SCAN_W


def _scan_tiles(src_ref, dst_ref, dst_off, c_ref, half, carry_ref, n_tiles, reverse, extra=None):
    shifts = (1, 2, 4)
    carry_row = 0 if reverse else SUBLANES - 1

    def cols(piece, im):
        lo = im * HALF_ST + piece * SCAN_W
        return slice(lo, lo + SCAN_W)

    def step(t, state):
        carries, accs = state
        k = (n_tiles - 1 - t) if reverse else t
        rows = slice(k * SUBLANES, (k + 1) * SUBLANES)
        new_carries, new_accs = [], []
        for piece in range(SCAN_PIECES):
            cr, ci = carries[piece]
            xr0 = src_ref[rows, cols(piece, 0)]
            xi0 = src_ref[rows, cols(piece, 1)]
            xr, xi = xr0, xi0
            for si, s in enumerate(shifts):
                ar = c_ref[half, si, :, cols(piece, 0)]
                ai = c_ref[half, si, :, cols(piece, 1)]
                sh = (SUBLANES - s) if reverse else s
                sr = pltpu.roll(xr, sh, 0)
                sm = pltpu.roll(xi, sh, 0)
                xr, xi = xr + (ar * sr - ai * sm), xi + (ar * sm + ai * sr)
            pr = c_ref[half, 3, :, cols(piece, 0)]
            pi = c_ref[half, 3, :, cols(piece, 1)]
            hr = xr + (pr * cr - pi * ci)
            hi = xi + (pr * ci + pi * cr)
            dst_ref[rows, pl.ds(dst_off + piece * SCAN_W, SCAN_W)] = hr
            dst_ref[rows, pl.ds(dst_off + HALF_ST + piece * SCAN_W, SCAN_W)] = hi
            new_carries.append((jnp.broadcast_to(hr[carry_row:carry_row + 1, :], (SUBLANES, SCAN_W)),
                                jnp.broadcast_to(hi[carry_row:carry_row + 1, :], (SUBLANES, SCAN_W))))
            if extra is not None:
                new_accs.append(extra(rows, piece, (xr0, xi0), (hr, hi), accs[piece]))
        return tuple(new_carries), tuple(new_accs)

    base = half * 2 * HALF_ST
    carries0 = tuple((carry_ref[:, pl.ds(base + p * SCAN_W, SCAN_W)],
                      carry_ref[:, pl.ds(base + HALF_ST + p * SCAN_W, SCAN_W)]) for p in range(SCAN_PIECES))
    zero = jnp.zeros((SUBLANES, SCAN_W), F32)
    accs0 = tuple((zero, zero) for _ in range(SCAN_PIECES)) if extra is not None else ()
    state = (carries0, accs0)
    for t in range(n_tiles):
        state = step(t, state)
    carries, accs = state
    for p in range(SCAN_PIECES):
        carry_ref[:, pl.ds(base + p * SCAN_W, SCAN_W)] = carries[p][0]
        carry_ref[:, pl.ds(base + HALF_ST + p * SCAN_W, SCAN_W)] = carries[p][1]
    return accs


def _ssm_tail(hb, u, c_ref, glu_ref, glub_ref, dskip_ref):
    ypre = u * dskip_ref[...]
    parts = []
    for half in range(2):
        parts.append(_dot(hb[half], c_ref[half]))
    ypre = ypre + jnp.concatenate(parts, axis=1)
    yg, th = _gelu(ypre)
    zz = _dot(yg.astype(BF16), glu_ref[...]) + glub_ref[...]
    z1, z2 = zz[:, :SSM_WIDTH], zz[:, SSM_WIDTH:]
    sg = jax.nn.sigmoid(z2)
    return ypre, th, yg, z1, sg


def _ssm_fwd(z, bblk, cblk, glu, glub, dskip, fwdc, batch, t_chunk, name):
    n = z.shape[0]
    nk = n // batch // t_chunk
    n_tiles = t_chunk // SUBLANES

    def body(u_ref, b_ref, c_ref, glu_ref, glub_ref, dskip_ref, k_ref, y_ref, h_ref, bu_ref, carry_ref):
        @pl.when(pl.program_id(1) == 0)
        def _():
            carry_ref[...] = jnp.zeros_like(carry_ref)

        u = u_ref[...]
        ub = u.astype(BF16)
        for half in range(2):
            bu_ref[half] = _dot(ub[:, half * HALF_IN:(half + 1) * HALF_IN], b_ref[half])
            _scan_tiles(bu_ref.at[half], h_ref, half * 2 * HALF_ST, k_ref, half, carry_ref, n_tiles, False)
        hb = [h_ref[:, half * 2 * HALF_ST:(half + 1) * 2 * HALF_ST].astype(BF16) for half in range(2)]
        _, _, _, z1, sg = _ssm_tail(hb, u, c_ref, glu_ref, glub_ref, dskip_ref)
        y_ref[...] = z1 * sg

    fixed2 = lambda b, k: (0, 0)
    fixed3 = lambda b, k: (0, 0, 0)
    row = lambda b, k: (b * nk + k, 0)
    return pl.pallas_call(
        body, name=name, grid=(batch, nk),
        in_specs=[
            pl.BlockSpec((t_chunk, SSM_WIDTH), row),
            pl.BlockSpec((2, HALF_IN, 2 * HALF_ST), fixed3),
            pl.BlockSpec((2, 2 * HALF_ST, HALF_IN), fixed3),
            pl.BlockSpec((SSM_WIDTH, 2 * SSM_WIDTH), fixed2),
            pl.BlockSpec((1, 2 * SSM_WIDTH), fixed2),
            pl.BlockSpec((1, SSM_WIDTH), fixed2),
            pl.BlockSpec((2, 4, SUBLANES, 2 * HALF_ST), lambda b, k: (0, 0, 0, 0)),
        ],
        out_specs=[pl.BlockSpec((t_chunk, SSM_WIDTH), row), pl.BlockSpec((t_chunk, 4 * HALF_ST), row)],
        out_shape=[jax.ShapeDtypeStruct((n, SSM_WIDTH), F32), jax.ShapeDtypeStruct((n, 4 * HALF_ST), F32)],
        scratch_shapes=[pltpu.VMEM((2, t_chunk, 2 * HALF_ST), F32), pltpu.VMEM((SUBLANES, 4 * HALF_ST), F32)],
        compiler_params=_cp("parallel", "arbitrary"),
    )(z, bblk, cblk, glu, glub, dskip, fwdc)


def _ssm_bwd(z, h, dy, bblk, cblk, glu, glub, dskip, revc, batch, t_chunk, name):
    n = z.shape[0]
    nk = n // batch // t_chunk
    n_tiles = t_chunk // SUBLANES

    def body(u_ref, h_ref, dy_ref, b_ref, c_ref, glu_ref, glub_ref, dskip_ref, k_ref,
             du_ref, dglu_ref, dglub_ref, ddskip_ref, dct_ref, db_ref, q_ref, g_ref, carry_ref):
        first = jnp.logical_and(pl.program_id(0) == 0, pl.program_id(1) == 0)

        @pl.when(first)
        def _():
            for r in (dglu_ref, dglub_ref, ddskip_ref, dct_ref, db_ref, q_ref):
                r[...] = jnp.zeros_like(r)

        @pl.when(pl.program_id(1) == 0)
        def _():
            carry_ref[...] = jnp.zeros_like(carry_ref)

        u = u_ref[...]
        ub = u.astype(BF16)
        hb = [h_ref[:, half * 2 * HALF_ST:(half + 1) * 2 * HALF_ST].astype(BF16) for half in range(2)]
        ypre, th, yg, z1, sg = _ssm_tail(hb, u, c_ref, glu_ref, glub_ref, dskip_ref)
        dout = dy_ref[...]
        dz = jnp.concatenate([dout * sg, dout * z1 * sg * (1.0 - sg)], axis=1)
        dzb = dz.astype(BF16)
        dglu_ref[...] += _dot_tn(yg.astype(BF16), dzb)
        dglub_ref[...] += _rows8(dz)
        dypre = _dot_nt(dzb, glu_ref[...]) * _gelu_grad(ypre, th)
        ddskip_ref[...] += _rows8(dypre * u)
        dypb = dypre.astype(BF16)
        du_parts = []
        for half in range(2):
            dyp_h = dypb[:, half * HALF_IN:(half + 1) * HALF_IN]
            dct_ref[half] += _dot_tn(dyp_h, hb[half])
            g_ref[half] = _dot_nt(dyp_h, c_ref[half])

            def extra(rows, piece, x_in, g_out, acc, half=half):
                er, ei = g_out[0] - x_in[0], g_out[1] - x_in[1]
                base = half * 2 * HALF_ST + piece * SCAN_W
                hr = h_ref[rows, pl.ds(base, SCAN_W)]
                hi = h_ref[rows, pl.ds(base + HALF_ST, SCAN_W)]
                return acc[0] + (er * hr + ei * hi), acc[1] + (er * hi - ei * hr)

            accs = _scan_tiles(g_ref.at[half], g_ref.at[half], 0, k_ref, half, carry_ref, n_tiles, True, extra)
            for piece in range(SCAN_PIECES):
                base = half * 2 * HALF_ST + piece * SCAN_W
                q_ref[:, pl.ds(base, SCAN_W)] += accs[piece][0]
                q_ref[:, pl.ds(base + HALF_ST, SCAN_W)] += accs[piece][1]
            gb = g_ref[half].astype(BF16)
            db_ref[half] += _dot_tn(ub[:, half * HALF_IN:(half + 1) * HALF_IN], gb)
            du_parts.append(_dot_nt(gb, b_ref[half]))
        du_ref[...] = dypre * dskip_ref[...] + jnp.concatenate(du_parts, axis=1)

    fixed2 = lambda b, k: (0, 0)
    fixed3 = lambda b, k: (0, 0, 0)
    row = lambda b, k: (b * nk + (nk - 1 - k), 0)
    return pl.pallas_call(
        body, name=name, grid=(batch, nk),
        in_specs=[
            pl.BlockSpec((t_chunk, SSM_WIDTH), row),
            pl.BlockSpec((t_chunk, 4 * HALF_ST), row),
            pl.BlockSpec((t_chunk, SSM_WIDTH), row),
            pl.BlockSpec((2, HALF_IN, 2 * HALF_ST), fixed3),
            pl.BlockSpec((2, 2 * HALF_ST, HALF_IN), fixed3),
            pl.BlockSpec((SSM_WIDTH, 2 * SSM_WIDTH), fixed2),
            pl.BlockSpec((1, 2 * SSM_WIDTH), fixed2),
            pl.BlockSpec((1, SSM_WIDTH), fixed2),
            pl.BlockSpec((2, 4, SUBLANES, 2 * HALF_ST), lambda b, k: (0, 0, 0, 0)),
        ],
        out_specs=[
            pl.BlockSpec((t_chunk, SSM_WIDTH), row),
            pl.BlockSpec((SSM_WIDTH, 2 * SSM_WIDTH), fixed2),
            pl.BlockSpec((SUBLANES, 2 * SSM_WIDTH), fixed2),
            pl.BlockSpec((SUBLANES, SSM_WIDTH), fixed2),
            pl.BlockSpec((2, HALF_IN, 2 * HALF_ST), fixed3),
            pl.BlockSpec((2, HALF_IN, 2 * HALF_ST), fixed3),
            pl.BlockSpec((SUBLANES, 4 * HALF_ST), fixed2),
        ],
        out_shape=[
            jax.ShapeDtypeStruct((n, SSM_WIDTH), F32),
            jax.ShapeDtypeStruct((SSM_WIDTH, 2 * SSM_WIDTH), F32),
            jax.ShapeDtypeStruct((SUBLANES, 2 * SSM_WIDTH), F32),
            jax.ShapeDtypeStruct((SUBLANES, SSM_WIDTH), F32),
            jax.ShapeDtypeStruct((2, HALF_IN, 2 * HALF_ST), F32),
            jax.ShapeDtypeStruct((2, HALF_IN, 2 * HALF_ST), F32),
            jax.ShapeDtypeStruct((SUBLANES, 4 * HALF_ST), F32),
        ],
        scratch_shapes=[pltpu.VMEM((2, t_chunk, 2 * HALF_ST), F32), pltpu.VMEM((SUBLANES, 4 * HALF_ST), F32)],
        compiler_params=_cp("arbitrary", "arbitrary"),
    )(z, h, dy, bblk, cblk, glu, glub, dskip, revc)


def _gm_chunk_fwd(u, v, gain_ref, w_ref, bias_ref):
    ug, thu = _gelu(u)
    vg, thv = _gelu(v)
    rs, vns, ss = [], [], []
    for hh in range(GM_HEADS):
        cs = slice(hh * GM_HEAD_DIM, (hh + 1) * GM_HEAD_DIM)
        vn, _, r = _rms_fwd(vg[:, cs], gain_ref[:, cs])
        s = _dot(w_ref[hh], vn.astype(BF16)) + bias_ref[:, cs]
        rs.append(r)
        vns.append(vn)
        ss.append(s)
    return ug, thu, thv, vg, rs, vns, ss


def _gm_fwd(z, gain, w_tril, bias, rows, name):
    n = z.shape[0]
    chunks = rows // GM_CHUNK

    def body(u_ref, v_ref, gain_ref, w_ref, bias_ref, y_ref):
        for c in range(chunks):
            rs_ = slice(c * GM_CHUNK, (c + 1) * GM_CHUNK)
            ug, _, _, _, _, _, ss = _gm_chunk_fwd(u_ref[rs_, :], v_ref[rs_, :], gain_ref, w_ref, bias_ref)
            y_ref[rs_, :] = ug * jnp.concatenate(ss, axis=1)

    return pl.pallas_call(
        body, name=name, grid=(n // rows,),
        in_specs=[
            pl.BlockSpec((rows, GM_WIDTH), lambda i: (i, 1)),
            pl.BlockSpec((rows, GM_WIDTH), lambda i: (i, 2)),
            pl.BlockSpec((1, GM_WIDTH), lambda i: (0, 0)),
            pl.BlockSpec((GM_HEADS, GM_CHUNK, GM_CHUNK), lambda i: (0, 0, 0)),
            pl.BlockSpec((GM_CHUNK, GM_WIDTH), lambda i: (0, 0)),
        ],
        out_specs=pl.BlockSpec((rows, GM_WIDTH), lambda i: (i, 0)),
        out_shape=jax.ShapeDtypeStruct((n, GM_WIDTH), F32),
        compiler_params=_cp("parallel"),
    )(z, z, gain, w_tril, bias)


def _gm_bwd(z, dy, gain, w_tril, bias, rows, name):
    n = z.shape[0]
    chunks = rows // GM_CHUNK

    def body(u_ref, v_ref, dy_ref, gain_ref, w_ref, bias_ref, du_ref, dv_ref, dw_ref, dbias_ref, dgain_ref):
        @pl.when(pl.program_id(0) == 0)
        def _():
            dw_ref[...] = jnp.zeros_like(dw_ref)
            dbias_ref[...] = jnp.zeros_like(dbias_ref)
            dgain_ref[...] = jnp.zeros_like(dgain_ref)

        for c in range(chunks):
            rs_ = slice(c * GM_CHUNK, (c + 1) * GM_CHUNK)
            u, v = u_ref[rs_, :], v_ref[rs_, :]
            ug, thu, thv, vg, rs, vns, ss = _gm_chunk_fwd(u, v, gain_ref, w_ref, bias_ref)
            dout = dy_ref[rs_, :]
            ds = dout * ug
            du_ref[rs_, :] = dout * jnp.concatenate(ss, axis=1) * _gelu_grad(u, thu)
            dbias_ref[...] += ds
            dvg_parts, dgain_parts = [], []
            for hh in range(GM_HEADS):
                cs = slice(hh * GM_HEAD_DIM, (hh + 1) * GM_HEAD_DIM)
                dsb = ds[:, cs].astype(BF16)
                dvn = _dot_tn(w_ref[hh], dsb)
                dw_ref[hh] += _dot_nt(dsb, vns[hh].astype(BF16))
                g = gain_ref[:, cs]
                xh = vg[:, cs] * rs[hh]
                dvg, dgr = _rms_bwd(dvn, xh, rs[hh], g)
                dvg_parts.append(dvg)
                dgain_parts.append(dgr)
            dv_ref[rs_, :] = jnp.concatenate(dvg_parts, axis=1) * _gelu_grad(v, thv)
            dgain_ref[...] += _rows8(jnp.concatenate(dgain_parts, axis=1))

    row = lambda i: (i, 0)
    return pl.pallas_call(
        body, name=name, grid=(n // rows,),
        in_specs=[
            pl.BlockSpec((rows, GM_WIDTH), lambda i: (i, 1)),
            pl.BlockSpec((rows, GM_WIDTH), lambda i: (i, 2)),
            pl.BlockSpec((rows, GM_WIDTH), row),
            pl.BlockSpec((1, GM_WIDTH), lambda i: (0, 0)),
            pl.BlockSpec((GM_HEADS, GM_CHUNK, GM_CHUNK), lambda i: (0, 0, 0)),
            pl.BlockSpec((GM_CHUNK, GM_WIDTH), lambda i: (0, 0)),
        ],
        out_specs=[
            pl.BlockSpec((rows, GM_WIDTH), row), pl.BlockSpec((rows, GM_WIDTH), row),
            pl.BlockSpec((GM_HEADS, GM_CHUNK, GM_CHUNK), lambda i: (0, 0, 0)),
            pl.BlockSpec((GM_CHUNK, GM_WIDTH), lambda i: (0, 0)),
            pl.BlockSpec((SUBLANES, GM_WIDTH), lambda i: (0, 0)),
        ],
        out_shape=[
            jax.ShapeDtypeStruct((n, GM_WIDTH), F32), jax.ShapeDtypeStruct((n, GM_WIDTH), F32),
            jax.ShapeDtypeStruct((GM_HEADS, GM_CHUNK, GM_CHUNK), F32),
            jax.ShapeDtypeStruct((GM_CHUNK, GM_WIDTH), F32),
            jax.ShapeDtypeStruct((SUBLANES, GM_WIDTH), F32),
        ],
        compiler_params=_cp("arbitrary"),
    )(z, z, dy, gain, w_tril, bias)


def _loss_head(x, gain, target, tm, name):
    n = x.shape[0]

    def body(x_ref, g_ref, t_ref, dx_ref, sq_ref, dgain_ref):
        @pl.when(pl.program_id(0) == 0)
        def _():
            sq_ref[...] = jnp.zeros_like(sq_ref)
            dgain_ref[...] = jnp.zeros_like(dgain_ref)

        g = g_ref[...]
        y, xh, r = _rms_fwd(x_ref[...], g)
        err = y - t_ref[...]
        sq_ref[...] += _rows8(err * err)
        dx, dgr = _rms_bwd(err * (1.0 / D_MODEL), xh, r, g)
        dx_ref[...] = dx
        dgain_ref[...] += _rows8(dgr)

    row = lambda i: (i, 0)
    fixed = lambda i: (0, 0)
    return pl.pallas_call(
        body, name=name, grid=(n // tm,),
        in_specs=[pl.BlockSpec((tm, D_MODEL), row), pl.BlockSpec((1, D_MODEL), fixed), pl.BlockSpec((tm, D_MODEL), row)],
        out_specs=[pl.BlockSpec((tm, D_MODEL), row), pl.BlockSpec((SUBLANES, D_MODEL), fixed),
                   pl.BlockSpec((SUBLANES, D_MODEL), fixed)],
        out_shape=[jax.ShapeDtypeStruct((n, D_MODEL), F32), jax.ShapeDtypeStruct((SUBLANES, D_MODEL), F32),
                   jax.ShapeDtypeStruct((SUBLANES, D_MODEL), F32)],
        compiler_params=_cp("arbitrary"),
    )(x, gain, target)


def _adam_math(w, g, m, v):
    m2 = ADAM_B1 * m + (1.0 - ADAM_B1) * g
    v2 = ADAM_B2 * v + (1.0 - ADAM_B2) * (g * g)
    m_hat = m2 / (1.0 - ADAM_B1 ** ADAM_STEP)
    v_hat = v2 / (1.0 - ADAM_B2 ** ADAM_STEP)
    delta = -ADAM_LR * (m_hat / (jnp.sqrt(v_hat) + ADAM_EPS) + ADAM_WD * w)
    return delta, m2, v2


def _adam_sharded(parts, w, m, v, layer, earlier, name):
    depth, r, c = w.shape
    tr = max(t for t in range(16, 129, 16) if r % t == 0)

    def body(p_ref, w_ref, m_ref, v_ref, *rest):
        g_ref, d_ref, m2_ref, v2_ref = rest[-4:]
        g = p_ref[0].astype(F32)
        for s in range(1, N_DEV):
            g = g + p_ref[s].astype(F32)
        delta, m2, v2 = _adam_math(w_ref[...], g, m_ref[...], v_ref[...])
        g_ref[...] = g
        d_ref[...] = delta
        m2_ref[...] = m2
        v2_ref[...] = v2

    blk = pl.BlockSpec((None, tr, c), lambda i: (layer, i, 0))
    extra = [] if earlier is None else list(earlier)
    return pl.pallas_call(
        body, name=name, grid=(r // tr,),
        in_specs=[pl.BlockSpec((N_DEV, tr, c), lambda i: (0, i, 0)), blk, blk, blk]
        + [pl.BlockSpec(memory_space=pl.ANY)] * len(extra),
        out_specs=[blk, blk, blk, blk],
        out_shape=[jax.ShapeDtypeStruct((depth, r, c), F32)] * 4,
        input_output_aliases={4 + i: i for i in range(len(extra))},
        compiler_params=_cp("parallel"),
    )(parts, w, m, v, *extra)


def _adam_packed(g, w, m, v, name):
    r, c = g.shape

    def body(g_ref, w_ref, m_ref, v_ref, d_ref, m2_ref, v2_ref):
        delta, m2, v2 = _adam_math(w_ref[...], g_ref[...], m_ref[...], v_ref[...])
        d_ref[...] = delta
        m2_ref[...] = m2
        v2_ref[...] = v2

    blk = pl.BlockSpec((r, c), lambda i: (0, 0))
    return pl.pallas_call(
        body, name=name, grid=(1,),
        in_specs=[blk, blk, blk, blk], out_specs=[blk, blk, blk],
        out_shape=[jax.ShapeDtypeStruct((r, c), F32)] * 3,
        compiler_params=_cp("arbitrary"),
    )(g, w, m, v)


def _my_place():
    return lax.axis_index("x"), lax.axis_index("y"), lax.axis_index("c")


def _flip(place, rel):
    x, y, c = place
    return (1 - x if rel & 4 else x, 1 - y if rel & 2 else y, 1 - c if rel & 1 else c)


def _index(place):
    return 4 * place[0] + 2 * place[1] + place[2]


def _all_gather(shards, name):
    na = len(shards)

    def body(*refs):
        xs, outs = refs[:na], refs[na:2 * na]
        send_sems, recv_sems, local_sems = refs[2 * na:]
        me = _my_place()
        sibling = _flip(me, 1)
        chips = [_flip(me, 4), _flip(me, 2), _flip(me, 6)]

        def copy(a, k, block, to, src=None):
            slot = outs[a].at[_index(block)]
            return pltpu.make_async_remote_copy(
                src_ref=slot if src is None else src, dst_ref=slot,
                send_sem=send_sems.at[a, k], recv_sem=recv_sems.at[a, k],
                device_id=to, device_id_type=MESH)

        mine = [pltpu.make_async_copy(xs[a], outs[a].at[_index(me)], local_sems.at[a]) for a in range(na)]
        for cp in mine:
            cp.start()
        first = []
        for a in range(na):
            first.append(copy(a, 0, me, sibling, src=xs[a]))
            first += [copy(a, 1 + j, me, chip, src=xs[a]) for j, chip in enumerate(chips)]
        for cp in first:
            cp.start()
        passed = []
        for a in range(na):
            for j, chip in enumerate(chips):
                copy(a, 1 + j, chip, me).wait_recv()
                fwd = copy(a, 4 + j, chip, sibling)
                fwd.start()
                passed.append(fwd)
        for a in range(na):
            copy(a, 0, sibling, me).wait_recv()
            for j, chip in enumerate(chips):
                copy(a, 4 + j, _flip(chip, 1), me).wait_recv()
        for cp in first + passed:
            cp.wait_send()
        for cp in mine:
            cp.wait()

    hbm = pl.BlockSpec(memory_space=pl.ANY)
    return pl.pallas_call(
        body, name=name,
        in_specs=[hbm] * na, out_specs=[hbm] * na,
        out_shape=[jax.ShapeDtypeStruct((N_DEV,) + s.shape, s.dtype) for s in shards],
        scratch_shapes=[pltpu.SemaphoreType.DMA((na, 7)), pltpu.SemaphoreType.DMA((na, 7)),
                        pltpu.SemaphoreType.DMA((na,))],
    )(*shards)


_HBM = pl.BlockSpec(memory_space=pltpu.HBM)
_SEM = pl.BlockSpec(memory_space=pltpu.SEMAPHORE)
_EFFECT = pltpu.SideEffectType.DATAFLOW_SIDE_EFFECTING


def _exchange_copy(src_ref, land_ref, send_sems, recv_sems, a, rel, me, scatter, landed):
    peer = _flip(me, rel)
    src = src_ref.at[_index(peer)] if scatter else src_ref
    return pltpu.make_async_remote_copy(
        src_ref=src, dst_ref=land_ref.at[_index(peer if landed else me)],
        send_sem=send_sems.at[a * (N_DEV - 1) + rel - 1], recv_sem=recv_sems.at[a * (N_DEV - 1) + rel - 1],
        device_id=peer, device_id_type=MESH)


def _own_slot(data, me, scatter):
    if scatter:
        own = lax.dynamic_slice_in_dim(data, me, 1, axis=0)
        shape = data.shape
    else:
        own = data[None]
        shape = (N_DEV,) + data.shape
    start = (me,) + (0,) * (len(shape) - 1)
    return lax.dynamic_update_slice(lax.empty(shape, data.dtype), own, start)


def _exchange_start(groups, me, scatter, name, after=None):
    sizes = [len(g) for g in groups]
    srcs = [a for g in groups for a in g]
    lands = [_own_slot(a, me, scatter) for a in srcs]
    na, ng = len(srcs), len(groups)
    deps = [] if after is None else [after]

    def body(*refs):
        src_refs, land_refs = refs[:na], refs[na:2 * na]
        sems = refs[2 * na + len(deps):2 * na + len(deps) + 2 * ng]
        token = refs[-1]
        place = _my_place()
        a = 0
        for g, size in enumerate(sizes):
            for k in range(size):
                for rel in range(1, N_DEV):
                    _exchange_copy(src_refs[a], land_refs[a], sems[2 * g], sems[2 * g + 1], k, rel, place, scatter,
                                   False).start()
                a += 1
        token[...] = jnp.zeros_like(token)

    sem_shapes = [pltpu.SemaphoreType.DMA((size * (N_DEV - 1),)) for size in sizes for _ in range(2)]
    outs = pl.pallas_call(
        body, name=name,
        in_specs=[_HBM] * (2 * na) + [pl.BlockSpec(memory_space=pl.ANY)] * len(deps),
        out_specs=[_SEM] * (2 * ng) + [_HBM] * (2 * na) + [pl.BlockSpec(memory_space=pltpu.VMEM)],
        out_shape=sem_shapes + [pltpu.HBM(a.shape, a.dtype) for a in srcs + lands]
        + [jax.ShapeDtypeStruct((SUBLANES, LANES), F32)],
        input_output_aliases={i: 2 * ng + i for i in range(2 * na)},
        compiler_params=pltpu.CompilerParams(has_side_effects=_EFFECT),
    )(*[pltpu.with_memory_space_constraint(a, pltpu.HBM) for a in srcs + lands], *deps)
    sems, thru, token = outs[:2 * ng], outs[2 * ng:2 * ng + 2 * na], outs[-1]
    handles, a = [], 0
    for g, size in enumerate(sizes):
        handles.append((sems[2 * g], sems[2 * g + 1], thru[a:a + size], thru[na + a:na + a + size]))
        a += size
    return handles, token


def _exchange_wait(handle, after, scatter, name):
    send_sems, recv_sems, srcs, lands = handle
    na = len(srcs)

    def body(*refs):
        src_refs, land_refs = refs[:na], refs[na:2 * na]
        send_ref, recv_ref = refs[2 * na], refs[2 * na + 1]
        place = _my_place()
        for a in range(na):
            for rel in range(1, N_DEV):
                cp = _exchange_copy(src_refs[a], land_refs[a], send_ref, recv_ref, a, rel, place, scatter, True)
                cp.wait_send()
                cp.wait_recv()

    outs = pl.pallas_call(
        body, name=name,
        in_specs=[_HBM] * (2 * na) + [_SEM, _SEM, pl.BlockSpec(memory_space=pl.ANY)],
        out_specs=[_HBM] * (2 * na),
        out_shape=[pltpu.HBM(a.shape, a.dtype) for a in list(srcs) + list(lands)],
        input_output_aliases={i: i for i in range(2 * na)},
        compiler_params=pltpu.CompilerParams(has_side_effects=_EFFECT),
    )(*srcs, *lands, send_sems, recv_sems, after)
    return outs[na:]


def _behind(arr, token):
    return arr + token[0:1, 0:1]


def _all_reduce_small(g, name):
    _, r, c = g.shape

    def body(g_ref, o_ref, land_ref, red_ref, send1, recv1, send2, recv2):
        me = _my_place()
        idx = _index(me)

        def scatter(rel):
            peer = _flip(me, rel)
            return pltpu.make_async_remote_copy(
                src_ref=g_ref.at[_index(peer)], dst_ref=land_ref.at[idx],
                send_sem=send1.at[rel - 1], recv_sem=recv1.at[rel - 1], device_id=peer, device_id_type=MESH)

        def gather(rel):
            peer = _flip(me, rel)
            return pltpu.make_async_remote_copy(
                src_ref=red_ref, dst_ref=o_ref.at[idx],
                send_sem=send2.at[rel - 1], recv_sem=recv2.at[rel - 1], device_id=peer, device_id_type=MESH)

        for rel in range(1, N_DEV):
            scatter(rel).start()
        land_ref[idx] = g_ref[idx]
        for rel in range(1, N_DEV):
            scatter(rel).wait()
        acc = land_ref[0]
        for s in range(1, N_DEV):
            acc = acc + land_ref[s]
        red_ref[...] = acc
        for rel in range(1, N_DEV):
            gather(rel).start()
        o_ref[idx] = acc
        for rel in range(1, N_DEV):
            gather(rel).wait()

    vmem = pl.BlockSpec(memory_space=pltpu.VMEM)
    return pl.pallas_call(
        body, name=name,
        in_specs=[vmem], out_specs=vmem,
        out_shape=jax.ShapeDtypeStruct(g.shape, F32),
        scratch_shapes=[pltpu.VMEM(g.shape, F32), pltpu.VMEM((r, c), F32)]
        + [pltpu.SemaphoreType.DMA((N_DEV - 1,))] * 4,
        compiler_params=pltpu.CompilerParams(vmem_limit_bytes=VMEM_LIMIT),
    )(g)


def _ssm_discretize(a_re, a_im, log_dt, b_re, b_im):
    dt = jnp.exp(log_dt)[:, None]
    mag = jnp.exp(a_re * dt)
    lr, li = mag * jnp.cos(a_im * dt), mag * jnp.sin(a_im * dt)
    den = a_re * a_re + a_im * a_im
    qr = ((lr - 1.0) * a_re + li * a_im) / den
    qi = (li * a_re - (lr - 1.0) * a_im) / den
    bbr = qr[..., None] * b_re - qi[..., None] * b_im
    bbi = qr[..., None] * b_im + qi[..., None] * b_re
    return lr, li, bbr, bbi


def _halves(a):
    return a.reshape((2, HALF_GROUPS) + a.shape[1:])


def _block_diag(blocks):
    g, r, c = blocks.shape
    eye = jnp.eye(g, dtype=blocks.dtype)
    return jnp.einsum("grc,gh->grhc", blocks, eye).reshape(g * r, g * c)


def _block_diag_take(dense, g, r, c):
    return jnp.einsum("grhc,gh->grc", dense.reshape(g, r, g, c), jnp.eye(g, dtype=dense.dtype))


def _ssm_matrices(bbr, bbi, c_re, c_im, glu_w, glu_b, d_skip):
    bre, bim = _halves(jnp.swapaxes(bbr, 1, 2)), _halves(jnp.swapaxes(bbi, 1, 2))
    bblk = jnp.stack([jnp.concatenate([_block_diag(bre[h]), _block_diag(bim[h])], axis=1) for h in range(2)])
    cre, cim = _halves(jnp.swapaxes(c_re, 1, 2)), _halves(jnp.swapaxes(c_im, 1, 2))
    cblk = jnp.stack([jnp.concatenate([_block_diag(cre[h]), -_block_diag(cim[h])], axis=0) for h in range(2)])
    glu = jnp.concatenate([_block_diag(glu_w[:, :, :SSM_CH]), _block_diag(glu_w[:, :, SSM_CH:])], axis=1)
    glub = jnp.concatenate([glu_b[:, :SSM_CH].reshape(1, -1), glu_b[:, SSM_CH:].reshape(1, -1)], axis=1)
    return bblk.astype(BF16), cblk.astype(BF16), glu.astype(BF16), glub, d_skip.reshape(1, -1)


def _scan_constants(lr, li, reverse):
    if reverse:
        li = -li
    pows = [(lr, li)]
    for _ in range(SUBLANES - 1):
        pr, pi = pows[-1]
        pows.append((pr * lr - pi * li, pr * li + pi * lr))
    row = jnp.arange(SUBLANES)[:, None]

    def flat(a):
        return a.reshape(2, 1, HALF_ST)

    mats = []
    for s in (1, 2, 4):
        keep = (row + s <= SUBLANES - 1) if reverse else (row >= s)
        mats.append(tuple(jnp.where(keep[None], flat(p), 0.0) for p in pows[s - 1]))
    order = [SUBLANES - 1 - j for j in range(SUBLANES)] if reverse else list(range(SUBLANES))
    mats.append(tuple(jnp.concatenate([flat(pows[j][k]) for j in order], axis=1) for k in range(2)))
    return jnp.stack([jnp.concatenate([m[0], m[1]], axis=2) for m in mats], axis=1)


def _pack(arrs, rows):
    flat = jnp.concatenate([a.reshape(-1) for a in arrs])
    return jnp.pad(flat, (0, rows * LANES - flat.shape[0])).reshape(rows, LANES)


def _unpack(buf, like):
    flat = buf.reshape(-1)
    out, off = [], 0
    for a in like:
        out.append(flat[off:off + a.size].reshape(a.shape))
        off += a.size
    return out


SMALL = ("norm_ffn1", "norm_mix", "ssm_a_re", "ssm_a_im", "ssm_log_dt", "ssm_b_re", "ssm_b_im", "ssm_c_re",
         "ssm_c_im", "ssm_d", "ssm_glu_w", "ssm_glu_b", "gm_v_gain", "gm_w_s", "gm_b_s", "gain_ssm_out",
         "gain_gm_out", "norm_ffn2", "norm_final")
BIG = ("ffn1_w_in", "ffn1_w_out", "mix_w_in", "mix_w_out", "ffn2_w_in", "ffn2_w_out")
TRANSPOSED = ("ffn1_w_in", "mix_w_in", "ffn2_w_in")
WEIGHTS = ("norm_ffn1", "ffn1_w_in", "ffn1_w_out", "norm_mix", "mix_w_in", "ssm_a_re", "ssm_a_im", "ssm_log_dt",
           "ssm_b_re", "ssm_b_im", "ssm_c_re", "ssm_c_im", "ssm_d", "ssm_glu_w", "ssm_glu_b", "gm_v_gain", "gm_w_s",
           "gm_b_s", "gain_ssm_out", "gain_gm_out", "mix_w_out", "norm_ffn2", "ffn2_w_in", "ffn2_w_out", "norm_final")


def _step(x, target, w, m, v):
    batch, seq, _ = x.shape
    n = batch * seq
    depth = w["norm_ffn1"].shape[0]
    tm = min(512, n)
    tm_ffn = min(1024, n)
    tk = min(2048, n)
    t_chunk = min(256, seq)
    gm_rows = min(512, seq)
    x = x.reshape(n, D_MODEL)
    target = target.reshape(n, D_MODEL)

    assert depth == 2
    me = _index(_my_place())
    shard = lambda group, l: [w[f"{group}_w_in"][l].astype(BF16), w[f"{group}_w_out"][l].astype(BF16)]
    batches = ([("mix", 0), ("ffn2", 0)], [("ffn1", 1), ("mix", 1)], [("ffn2", 1)])
    gathered, pending = {("ffn1", 0): tuple(_all_gather(shard("ffn1", 0), "all_gather_first"))}, {}

    def gather_start(i, after):
        handles, tok = _exchange_start([shard(g, l) for g, l in batches[i]], me, False, f"all_gather_start_{i}",
                                       after)
        pending.update(zip(batches[i], handles))
        return tok

    def weights(group, l, after=None):
        if (group, l) not in gathered:
            w_in, w_out = _exchange_wait(pending[(group, l)], after, False, f"all_gather_wait_{group}_{l}")
            if group == "mix":
                w_in = jnp.transpose(w_in, (1, 0, 2)).reshape(D_MODEL, IN_COLS)
                w_out = w_out.reshape(D_MODEL, D_MODEL)
            gathered[(group, l)] = (w_in, w_out)
        return gathered[(group, l)]

    tril = jnp.tril(jnp.ones((GM_CHUNK, GM_CHUNK), bool))
    layers = []
    for l in range(depth):
        disc, disc_vjp = jax.vjp(_ssm_discretize, w["ssm_a_re"][l], w["ssm_a_im"][l], w["ssm_log_dt"][l],
                                 w["ssm_b_re"][l], w["ssm_b_im"][l])
        lr, li, bbr, bbi = disc
        bblk, cblk, glu, glub, dskip = _ssm_matrices(bbr, bbi, w["ssm_c_re"][l], w["ssm_c_im"][l],
                                                     w["ssm_glu_w"][l], w["ssm_glu_b"][l], w["ssm_d"][l])
        layers.append(dict(
            disc_vjp=disc_vjp, lr=lr, li=li, bblk=bblk, cblk=cblk, glu=glu, glub=glub, dskip=dskip,
            fwdc=_scan_constants(lr, li, False), revc=_scan_constants(lr, li, True),
            w_tril=jnp.where(tril[None], w["gm_w_s"][l], 0.0).astype(BF16),
            gm_bias=jnp.repeat(w["gm_b_s"][l].T, GM_HEAD_DIM, axis=1),
            g1=w["norm_ffn1"][l][None], gmix=w["norm_mix"][l][None], g2=w["norm_ffn2"][l][None],
            gv=w["gm_v_gain"][l][None], gs=w["gain_ssm_out"][l][None], gg=w["gain_gm_out"][l][None],
        ))

    saved = []
    for l in range(depth):
        p = layers[l]
        x0 = x
        g1, gmix, g2 = p["g1"], p["gmix"], p["g2"]
        w_in, w_out = weights("ffn1", l, x0)
        if l == 0:
            g1 = _behind(g1, gather_start(0, w_in))
        x1, xn1, gu1 = _ffn_fwd(x0, g1, w_in, w_out, tm_ffn, f"ffn1_fwd_{l}")
        if l == 0:
            gmix = _behind(gmix, gather_start(1, x1))
        mwi, mwo = weights("mix", l, x1)
        z = _mix_in_fwd(x1, gmix, mwi, tm, f"mix_in_fwd_{l}")
        y_ssm, h = _ssm_fwd(z, p["bblk"], p["cblk"], p["glu"], p["glub"], p["dskip"], p["fwdc"], batch, t_chunk,
                            f"ssm_fwd_{l}")
        y_gm = _gm_fwd(z, p["gv"], p["w_tril"], p["gm_bias"], gm_rows, f"gm_fwd_{l}")
        x2 = _mix_out_fwd(y_ssm, y_gm, p["gs"], p["gg"], mwo, x1, tm, f"mix_out_fwd_{l}")
        if l == 0:
            g2 = _behind(g2, gather_start(2, x2))
        x, xn2, gu2 = _ffn_fwd(x2, g2, *weights("ffn2", l, x2), tm_ffn, f"ffn2_fwd_{l}")
        saved.append((x0, x1, x2, z, h, y_ssm, y_gm, xn1, gu1, xn2, gu2))

    dx, sq, dnf = _loss_head(x, w["norm_final"][None], target, tm, "loss_head")
    loss = lax.psum((0.5 / D_MODEL) * jnp.sum(sq), AXES)

    small = {k: [None] * depth for k in SMALL if k != "norm_final"}
    sent = []

    def send(group, l, keys, parts):
        (handle,), tok = _exchange_start([parts], me, True, f"reduce_scatter_start_{group}_{l}")
        sent.append((group, l, keys, handle))
        return tok

    token = None
    for l in reversed(range(depth)):
        p = layers[l]
        x0, x1, x2, z, h, y_ssm, y_gm, xn1, gu1, xn2, gu2 = saved[l]
        mwi, mwo = weights("mix", l)
        dx_out = dx
        g2 = p["g2"] if token is None else _behind(p["g2"], token)
        dx, dgu, act, dgain = _ffn_bwd(x2, g2, dx_out, gu2, *weights("ffn2", l), tm, f"ffn2_bwd_{l}")
        dw_in = _ffn_dw_in(xn2, dgu, tk, f"ffn2_dw_in_{l}")
        dw_out = _ffn_dw_out(act, dx_out, tk, f"ffn2_dw_out_{l}").reshape(N_DEV, FF_SHARD // 2, D_MODEL)
        token = send("ffn2", l, ("ffn2_w_in", "ffn2_w_out"), [dw_in, dw_out])
        small["norm_ffn2"][l] = dgain.sum(0)

        dy_ssm, dy_gm, dwo, dgs, dgg = _mix_out_bwd(y_ssm, y_gm, _behind(p["gs"], token), p["gg"], mwo, dx, tm,
                                                    f"mix_out_bwd_{l}")
        dwo = dwo.astype(BF16).reshape(N_DEV, D_MODEL // N_DEV, D_MODEL)
        small["gain_ssm_out"][l] = dgs.sum(0)
        small["gain_gm_out"][l] = dgg.sum(0)

        du_ssm, dglu, dglub, ddskip, dct, db, q = _ssm_bwd(
            z, h, dy_ssm, p["bblk"], p["cblk"], p["glu"], p["glub"], p["dskip"], p["revc"], batch, t_chunk,
            f"ssm_bwd_{l}")
        du_gm, dv_gm, dws, dbias, dgv = _gm_bwd(z, dy_gm, p["gv"], p["w_tril"], p["gm_bias"], gm_rows, f"gm_bwd_{l}")

        q = q.sum(0).reshape(2, 2, HALF_GROUPS, SSM_STATE)
        qr, qi = q[:, 0].reshape(SSM_GROUPS, SSM_STATE), q[:, 1].reshape(SSM_GROUPS, SSM_STATE)
        den = p["lr"] * p["lr"] + p["li"] * p["li"]
        d_re = (qr * p["lr"] + qi * p["li"]) / den
        d_im = (qi * p["lr"] - qr * p["li"]) / den
        dbb = jnp.stack([_block_diag_take(db[hf, :, k * HALF_ST:(k + 1) * HALF_ST], HALF_GROUPS, SSM_CH, SSM_STATE)
                         for k in range(2) for hf in range(2)]).reshape(2, SSM_GROUPS, SSM_CH, SSM_STATE)
        dcc = jnp.stack([_block_diag_take(dct[hf, :, k * HALF_ST:(k + 1) * HALF_ST], HALF_GROUPS, SSM_CH, SSM_STATE)
                         for k in range(2) for hf in range(2)]).reshape(2, SSM_GROUPS, SSM_CH, SSM_STATE)
        da_re, da_im, dlog_dt, db_re, db_im = p["disc_vjp"](
            (d_re, -d_im, jnp.swapaxes(dbb[0], 1, 2), jnp.swapaxes(dbb[1], 1, 2)))
        small["ssm_a_re"][l], small["ssm_a_im"][l], small["ssm_log_dt"][l] = da_re, da_im, dlog_dt
        small["ssm_b_re"][l], small["ssm_b_im"][l] = db_re, db_im
        small["ssm_c_re"][l], small["ssm_c_im"][l] = dcc[0], -dcc[1]
        small["ssm_d"][l] = ddskip.sum(0).reshape(SSM_GROUPS, SSM_CH)
        small["ssm_glu_w"][l] = jnp.concatenate(
            [_block_diag_take(dglu[:, :SSM_WIDTH], SSM_GROUPS, SSM_CH, SSM_CH),
             _block_diag_take(dglu[:, SSM_WIDTH:], SSM_GROUPS, SSM_CH, SSM_CH)], axis=2)
        dglub = dglub.sum(0)
        small["ssm_glu_b"][l] = jnp.concatenate(
            [dglub[:SSM_WIDTH].reshape(SSM_GROUPS, SSM_CH), dglub[SSM_WIDTH:].reshape(SSM_GROUPS, SSM_CH)], axis=1)
        small["gm_v_gain"][l] = dgv.sum(0)
        small["gm_w_s"][l] = jnp.where(tril[None], dws, 0.0)
        small["gm_b_s"][l] = dbias.reshape(GM_CHUNK, GM_HEADS, GM_HEAD_DIM).sum(-1).T

        dx, dwi, dgain = _mix_in_bwd(x1, p["gmix"], du_ssm, du_gm, dv_gm, dx, mwi, tm, f"mix_in_bwd_{l}")
        dwi = dwi.astype(BF16).reshape(N_DEV, IN_COLS // N_DEV, D_MODEL)
        token = send("mix", l, ("mix_w_in", "mix_w_out"), [dwi, dwo])
        small["norm_mix"][l] = dgain.sum(0)

        dx_out = dx
        dx, dgu, act, dgain = _ffn_bwd(x0, _behind(p["g1"], token), dx_out, gu1, *weights("ffn1", l), tm,
                                       f"ffn1_bwd_{l}")
        small["norm_ffn1"][l] = dgain.sum(0)
        if l > 0:
            dw_in = _ffn_dw_in(xn1, dgu, tk, f"ffn1_dw_in_{l}")
            dw_out = _ffn_dw_out(act, dx_out, tk, f"ffn1_dw_out_{l}").reshape(N_DEV, FF_SHARD // 2, D_MODEL)
            token = send("ffn1", l, ("ffn1_w_in", "ffn1_w_out"), [dw_in, dw_out])
            continue
        small_g = [jnp.stack(small[k]) if k != "norm_final" else dnf.sum(0) for k in SMALL]
        total = sum(int(math.prod(w[k].shape)) for k in SMALL)
        rows = -(-total // (LANES * N_DEV * SUBLANES)) * N_DEV * SUBLANES
        g_all = _all_reduce_small(_pack(small_g, rows).reshape(N_DEV, rows // N_DEV, LANES), "all_reduce_small")
        dw_in = _ffn_dw_in(xn1, dgu, tk, f"ffn1_dw_in_{l}", after=g_all)
        token = send("ffn1_in", l, ("ffn1_w_in",), [dw_in])
        dw_out = _ffn_dw_out(act, dx_out, tk, f"ffn1_dw_out_{l}", after=token).reshape(
            N_DEV, FF_SHARD // 2, D_MODEL)
        token = send("ffn1_out", l, ("ffn1_w_out",), [dw_out])

    grad_x = dx.reshape(batch, seq, D_MODEL)
    grads, deltas, new_m, new_v = {}, {}, {}, {}

    g_all = _behind(g_all.reshape(rows, LANES), token)
    like = [w[k] for k in SMALL]
    d_p, m_p, v_p = _adam_packed(g_all, _pack(like, rows), _pack([m[k] for k in SMALL], rows),
                                 _pack([v[k] for k in SMALL], rows), "adam_small")
    for k, g_, d_, m_, v_ in zip(SMALL, _unpack(g_all, like), _unpack(d_p, like), _unpack(m_p, like),
                                 _unpack(v_p, like)):
        grads[k], deltas[k], new_m[k], new_v[k] = g_, d_, m_, v_

    results = {}
    after = d_p
    for group, l, keys, handle in sent:
        landed = _exchange_wait(handle, after, True, f"reduce_scatter_wait_{group}_{l}")
        for k, parts in zip(keys, landed):
            view = (lambda a: jnp.swapaxes(a, 1, 2)) if k in TRANSPOSED else (lambda a: a)
            results[k] = _adam_sharded(parts, view(w[k]), view(m[k]), view(v[k]), l, results.get(k),
                                       f"adam_{k}_{l}")
            after = results[k][0]
    for k in BIG:
        view = (lambda a: jnp.swapaxes(a, 1, 2)) if k in TRANSPOSED else (lambda a: a)
        grads[k], deltas[k], new_m[k], new_v[k] = [view(a) for a in results[k]]
    return loss, grad_x, grads, deltas, new_m, new_v


def kernel(x, norm_ffn1, ffn1_w_in, ffn1_w_out, norm_mix, mix_w_in, ssm_a_re, ssm_a_im, ssm_log_dt, ssm_b_re, ssm_b_im, ssm_c_re, ssm_c_im, ssm_d, ssm_glu_w, ssm_glu_b, gm_v_gain, gm_w_s, gm_b_s, gain_ssm_out, gain_gm_out, mix_w_out, norm_ffn2, ffn2_w_in, ffn2_w_out, norm_final, loss_target, m_norm_ffn1, m_ffn1_w_in, m_ffn1_w_out, m_norm_mix, m_mix_w_in, m_ssm_a_re, m_ssm_a_im, m_ssm_log_dt, m_ssm_b_re, m_ssm_b_im, m_ssm_c_re, m_ssm_c_im, m_ssm_d, m_ssm_glu_w, m_ssm_glu_b, m_gm_v_gain, m_gm_w_s, m_gm_b_s, m_gain_ssm_out, m_gain_gm_out, m_mix_w_out, m_norm_ffn2, m_ffn2_w_in, m_ffn2_w_out, m_norm_final, v_norm_ffn1, v_ffn1_w_in, v_ffn1_w_out, v_norm_mix, v_mix_w_in, v_ssm_a_re, v_ssm_a_im, v_ssm_log_dt, v_ssm_b_re, v_ssm_b_im, v_ssm_c_re, v_ssm_c_im, v_ssm_d, v_ssm_glu_w, v_ssm_glu_b, v_gm_v_gain, v_gm_w_s, v_gm_b_s, v_gain_ssm_out, v_gain_gm_out, v_mix_w_out, v_norm_ffn2, v_ffn2_w_in, v_ffn2_w_out, v_norm_final):
    args = locals()
    w = {k: args[k] for k in WEIGHTS}
    m = {k: args["m_" + k] for k in WEIGHTS}
    v = {k: args["v_" + k] for k in WEIGHTS}
    loss, grad_x, grads, deltas, new_m, new_v = _step(x, loss_target, w, m, v)
    return (loss, grad_x, *[grads[k] for k in WEIGHTS], *[deltas[k] for k in WEIGHTS],
            *[new_m[k] for k in WEIGHTS], *[new_v[k] for k in WEIGHTS])
```

```python
import functools
import math

import jax
import jax.numpy as jnp
from jax import lax
from jax.experimental import pallas as pl
from jax.experimental.pallas import tpu as pltpu

F32 = jnp.float32
BF16 = jnp.bfloat16
MESH = pl.DeviceIdType.MESH
AXES = ("x", "y", "c")

N_DEV = 8
D_MODEL = 1024
D_FF = 2816
FF_SHARD = 2 * D_FF // N_DEV
FF_CHUNKS = 4
MXU_DIM = 256
FF_PIECES = tuple((lo, min(lo + MXU_DIM, FF_SHARD)) for lo in range(0, FF_SHARD, MXU_DIM))
SSM_WIDTH = 512
SSM_CH = 16
SSM_GROUPS = 32
SSM_STATE = 64
HALF_GROUPS = 16
HALF_IN = HALF_GROUPS * SSM_CH
HALF_ST = HALF_GROUPS * SSM_STATE
GM_WIDTH = 512
GM_HEADS = 4
GM_HEAD_DIM = 128
GM_CHUNK = 128
IN_COLS = SSM_WIDTH + 2 * GM_WIDTH
EPS = 1e-6
SUBLANES = 8
LANES = 128

ADAM_LR = 0.001
ADAM_B1 = 0.9
ADAM_B2 = 0.999
ADAM_EPS = 1e-08
ADAM_WD = 0.01
ADAM_STEP = 10

VMEM_LIMIT = 46 * 1024 * 1024


def _cp(*sem):
    return pltpu.CompilerParams(dimension_semantics=sem, vmem_limit_bytes=VMEM_LIMIT)


def _rms_fwd(x, g):
    r = lax.rsqrt(jnp.mean(x * x, axis=-1, keepdims=True) + EPS)
    xh = x * r
    return xh * g, xh, r


def _rms_bwd(dy, xh, r, g):
    dxh = dy * g
    dx = r * (dxh - xh * jnp.mean(dxh * xh, axis=-1, keepdims=True))
    return dx, dy * xh


def _rows8(a):
    m, n = a.shape
    return a.reshape(m // SUBLANES, SUBLANES, n).sum(axis=0)


_GELU_K = math.sqrt(2.0 / math.pi)
_GELU_C = 0.044715


def _gelu(x):
    th = jnp.tanh(_GELU_K * (x + _GELU_C * x * x * x))
    return 0.5 * x * (1.0 + th), th


def _gelu_grad(x, th):
    return 0.5 * (1.0 + th) + 0.5 * x * (1.0 - th * th) * (_GELU_K * (1.0 + 3.0 * _GELU_C * x * x))


def _dot(a, b):
    return jnp.dot(a, b, preferred_element_type=F32)


def _dot_nt(a, b):
    return lax.dot_general(a, b, (((1,), (1,)), ((), ())), preferred_element_type=F32)


def _dot_tn(a, b):
    return lax.dot_general(a, b, (((0,), (0,)), ((), ())), preferred_element_type=F32)


def _ffn_fwd(x, gain, w_in_ag, w_out_ag, tm, name):
    n = x.shape[0]

    def body(x_ref, g_ref, wg_ref, wu_ref, wo_ref, o_ref, xn_ref, gu_ref):
        j = pl.program_id(1)

        @pl.when(j == 0)
        def _():
            xv = x_ref[...]
            y, _, _ = _rms_fwd(xv, g_ref[...])
            xn_ref[...] = y.astype(BF16)
            o_ref[...] = xv

        xn = xn_ref[...]
        wo = wo_ref[...].reshape(FF_SHARD, D_MODEL)
        out = None
        for lo, hi in FF_PIECES:
            gg = _dot(xn, wg_ref[:, lo:hi])
            uu = _dot(xn, wu_ref[:, lo:hi])
            gu_ref[0, :, lo:hi] = gg.astype(BF16)
            gu_ref[1, :, lo:hi] = uu.astype(BF16)
            act = (gg * jax.nn.sigmoid(gg) * uu).astype(BF16)
            part = _dot(act, wo[lo:hi, :])
            out = part if out is None else out + part
        o_ref[...] += 0.5 * out

    return pl.pallas_call(
        body, name=name, grid=(n // tm, FF_CHUNKS),
        in_specs=[
            pl.BlockSpec((tm, D_MODEL), lambda i, j: (i, 0)),
            pl.BlockSpec((1, D_MODEL), lambda i, j: (0, 0)),
            pl.BlockSpec((None, D_MODEL, FF_SHARD), lambda i, j: (j, 0, 0)),
            pl.BlockSpec((None, D_MODEL, FF_SHARD), lambda i, j: (j + FF_CHUNKS, 0, 0)),
            pl.BlockSpec((2, FF_SHARD // 2, D_MODEL), lambda i, j: (j, 0, 0)),
        ],
        out_specs=[
            pl.BlockSpec((tm, D_MODEL), lambda i, j: (i, 0)),
            pl.BlockSpec((tm, D_MODEL), lambda i, j: (i, 0)),
            pl.BlockSpec((None, 2, tm, FF_SHARD), lambda i, j: (j, 0, i, 0)),
        ],
        out_shape=[
            jax.ShapeDtypeStruct((n, D_MODEL), F32),
            jax.ShapeDtypeStruct((n, D_MODEL), BF16),
            jax.ShapeDtypeStruct((FF_CHUNKS, 2, n, FF_SHARD), BF16),
        ],
        compiler_params=_cp("parallel", "arbitrary"),
    )(x, gain, w_in_ag, w_in_ag, w_out_ag)


def _ffn_bwd(x, gain, dy, gu, w_in_ag, w_out_ag, tm, name):
    n = x.shape[0]
    steps = FF_CHUNKS + 1
    first = lambda j: jnp.minimum(j, FF_CHUNKS - 1)
    second = lambda j: jnp.maximum(j - 1, 0)

    def body(x_ref, g_ref, dy_ref, gu_ref, wg_ref, wu_ref, wo_ref, dx_ref, dgu_ref, act_ref, dgain_ref, dyb_ref,
             held_ref):
        i, j = pl.program_id(0), pl.program_id(1)

        @pl.when(jnp.logical_and(i == 0, j == 0))
        def _():
            dgain_ref[...] = jnp.zeros_like(dgain_ref)
            held_ref[...] = jnp.zeros_like(held_ref)

        @pl.when(j == 0)
        def _():
            dyb_ref[...] = (0.5 * dy_ref[...]).astype(BF16)
            dx_ref[...] = jnp.zeros_like(dx_ref)

        held = held_ref[1 - j % 2]
        part = _dot_nt(held[0], wg_ref[...]) + _dot_nt(held[1], wu_ref[...])
        dx_ref[...] += jnp.where(j > 0, part, 0.0)

        dyb = dyb_ref[...]
        wo = wo_ref[...].reshape(FF_SHARD, D_MODEL)
        slot = j % 2
        for lo, hi in FF_PIECES:
            gg = gu_ref[0, :, lo:hi].astype(F32)
            uu = gu_ref[1, :, lo:hi].astype(F32)
            dact = _dot_nt(dyb, wo[lo:hi, :])
            sig = jax.nn.sigmoid(gg)
            silu = gg * sig
            act_ref[:, lo:hi] = (silu * uu).astype(BF16)
            du = (dact * silu).astype(BF16)
            dg = (dact * uu * (sig * (1.0 + gg * (1.0 - sig)))).astype(BF16)
            dgu_ref[0, :, lo:hi] = dg
            dgu_ref[1, :, lo:hi] = du
            held_ref[slot, 0, :, lo:hi] = dg
            held_ref[slot, 1, :, lo:hi] = du

        @pl.when(j == steps - 1)
        def _():
            g = g_ref[...]
            _, xh, r = _rms_fwd(x_ref[...], g)
            dx, dgr = _rms_bwd(dx_ref[...], xh, r, g)
            dx_ref[...] = dy_ref[...] + dx
            dgain_ref[...] += _rows8(dgr)

    return pl.pallas_call(
        body, name=name, grid=(n // tm, steps),
        in_specs=[
            pl.BlockSpec((tm, D_MODEL), lambda i, j: (i, 0)),
            pl.BlockSpec((1, D_MODEL), lambda i, j: (0, 0)),
            pl.BlockSpec((tm, D_MODEL), lambda i, j: (i, 0)),
            pl.BlockSpec((None, 2, tm, FF_SHARD), lambda i, j: (first(j), 0, i, 0)),
            pl.BlockSpec((None, D_MODEL, FF_SHARD), lambda i, j: (second(j), 0, 0)),
            pl.BlockSpec((None, D_MODEL, FF_SHARD), lambda i, j: (second(j) + FF_CHUNKS, 0, 0)),
            pl.BlockSpec((2, FF_SHARD // 2, D_MODEL), lambda i, j: (first(j), 0, 0)),
        ],
        out_specs=[
            pl.BlockSpec((tm, D_MODEL), lambda i, j: (i, 0)),
            pl.BlockSpec((None, 2, tm, FF_SHARD), lambda i, j: (first(j), 0, i, 0)),
            pl.BlockSpec((None, tm, FF_SHARD), lambda i, j: (first(j), i, 0)),
            pl.BlockSpec((SUBLANES, D_MODEL), lambda i, j: (0, 0)),
        ],
        out_shape=[
            jax.ShapeDtypeStruct((n, D_MODEL), F32),
            jax.ShapeDtypeStruct((FF_CHUNKS, 2, n, FF_SHARD), BF16),
            jax.ShapeDtypeStruct((FF_CHUNKS, n, FF_SHARD), BF16),
            jax.ShapeDtypeStruct((SUBLANES, D_MODEL), F32),
        ],
        scratch_shapes=[pltpu.VMEM((tm, D_MODEL), BF16), pltpu.VMEM((2, 2, tm, FF_SHARD), BF16)],
        compiler_params=_cp("arbitrary", "arbitrary"),
    )(x, gain, dy, gu, w_in_ag, w_in_ag, w_out_ag)


def _ffn_dw_in(xn, dgu, tk, name, after=None):
    n = xn.shape[0]
    nk = n // tk
    deps = [] if after is None else [after]

    def body(a_ref, b_ref, *rest):
        o_ref, acc_ref = rest[-2:]
        k = pl.program_id(2)

        @pl.when(k == 0)
        def _():
            acc_ref[...] = jnp.zeros_like(acc_ref)

        acc_ref[...] += _dot_tn(b_ref[...], a_ref[...])

        @pl.when(k == nk - 1)
        def _():
            o_ref[...] = acc_ref[...].astype(BF16)

    return pl.pallas_call(
        body, name=name, grid=(FF_CHUNKS, 2, nk),
        in_specs=[
            pl.BlockSpec((tk, D_MODEL), lambda j, p, k: (k, 0)),
            pl.BlockSpec((None, None, tk, FF_SHARD), lambda j, p, k: (j, p, k, 0)),
        ] + [pl.BlockSpec(memory_space=pl.ANY)] * len(deps),
        out_specs=pl.BlockSpec((None, FF_SHARD, D_MODEL), lambda j, p, k: (FF_CHUNKS * p + j, 0, 0)),
        out_shape=jax.ShapeDtypeStruct((N_DEV, FF_SHARD, D_MODEL), BF16),
        scratch_shapes=[pltpu.VMEM((FF_SHARD, D_MODEL), F32)],
        compiler_params=_cp("parallel", "parallel", "arbitrary"),
    )(xn, dgu, *deps)


def _ffn_dw_out(act, dy, tk, name, after=None):
    n = act.shape[1]
    nk = n // tk
    deps = [] if after is None else [after]

    def body(a_ref, b_ref, *rest):
        o_ref, acc_ref = rest[-2:]
        k = pl.program_id(1)

        @pl.when(k == 0)
        def _():
            acc_ref[...] = jnp.zeros_like(acc_ref)

        acc_ref[...] += _dot_tn(a_ref[...], b_ref[...].astype(BF16))

        @pl.when(k == nk - 1)
        def _():
            o_ref[...] = (0.5 * acc_ref[...]).astype(BF16)

    return pl.pallas_call(
        body, name=name, grid=(FF_CHUNKS, nk),
        in_specs=[
            pl.BlockSpec((None, tk, FF_SHARD), lambda j, k: (j, k, 0)),
            pl.BlockSpec((tk, D_MODEL), lambda j, k: (k, 0)),
        ] + [pl.BlockSpec(memory_space=pl.ANY)] * len(deps),
        out_specs=pl.BlockSpec((None, FF_SHARD, D_MODEL), lambda j, k: (j, 0, 0)),
        out_shape=jax.ShapeDtypeStruct((FF_CHUNKS, FF_SHARD, D_MODEL), BF16),
        scratch_shapes=[pltpu.VMEM((FF_SHARD, D_MODEL), F32)],
        compiler_params=_cp("parallel", "arbitrary"),
    )(act, dy, *deps)


def _mix_in_fwd(x, gain, w, tm, name):
    n = x.shape[0]

    def body(x_ref, g_ref, w_ref, z_ref):
        y, _, _ = _rms_fwd(x_ref[...], g_ref[...])
        z_ref[...] = _dot(y.astype(BF16), w_ref[...])

    return pl.pallas_call(
        body, name=name, grid=(n // tm,),
        in_specs=[
            pl.BlockSpec((tm, D_MODEL), lambda i: (i, 0)),
            pl.BlockSpec((1, D_MODEL), lambda i: (0, 0)),
            pl.BlockSpec((D_MODEL, IN_COLS), lambda i: (0, 0)),
        ],
        out_specs=pl.BlockSpec((tm, IN_COLS), lambda i: (i, 0)),
        out_shape=jax.ShapeDtypeStruct((n, IN_COLS), F32),
        compiler_params=_cp("parallel"),
    )(x, gain, w)


def _mix_in_bwd(x, gain, du_ssm, du_gm, dv_gm, d_res, w, tm, name):
    n = x.shape[0]

    def body(x_ref, g_ref, d0_ref, d1_ref, d2_ref, dres_ref, w_ref, dx_ref, dw_ref, dgain_ref):
        i = pl.program_id(0)

        @pl.when(i == 0)
        def _():
            dw_ref[...] = jnp.zeros_like(dw_ref)
            dgain_ref[...] = jnp.zeros_like(dgain_ref)

        g = g_ref[...]
        y, xh, r = _rms_fwd(x_ref[...], g)
        xn = y.astype(BF16)
        dxn = jnp.zeros((tm, D_MODEL), F32)
        for k, d_ref in enumerate((d0_ref, d1_ref, d2_ref)):
            dz = d_ref[...].astype(BF16)
            cols = slice(k * SSM_WIDTH, (k + 1) * SSM_WIDTH)
            dxn += _dot_nt(dz, w_ref[:, cols])
            dw_ref[cols, :] += _dot_tn(dz, xn)
        dx, dgr = _rms_bwd(dxn, xh, r, g)
        dx_ref[...] = dres_ref[...] + dx
        dgain_ref[...] += _rows8(dgr)

    row = lambda i: (i, 0)
    fixed = lambda i: (0, 0)
    return pl.pallas_call(
        body, name=name, grid=(n // tm,),
        in_specs=[
            pl.BlockSpec((tm, D_MODEL), row),
            pl.BlockSpec((1, D_MODEL), fixed),
            pl.BlockSpec((tm, SSM_WIDTH), row),
            pl.BlockSpec((tm, GM_WIDTH), row),
            pl.BlockSpec((tm, GM_WIDTH), row),
            pl.BlockSpec((tm, D_MODEL), row),
            pl.BlockSpec((D_MODEL, IN_COLS), fixed),
        ],
        out_specs=[
            pl.BlockSpec((tm, D_MODEL), row),
            pl.BlockSpec((IN_COLS, D_MODEL), fixed),
            pl.BlockSpec((SUBLANES, D_MODEL), fixed),
        ],
        out_shape=[
            jax.ShapeDtypeStruct((n, D_MODEL), F32),
            jax.ShapeDtypeStruct((IN_COLS, D_MODEL), F32),
            jax.ShapeDtypeStruct((SUBLANES, D_MODEL), F32),
        ],
        compiler_params=_cp("arbitrary"),
    )(x, gain, du_ssm, du_gm, dv_gm, d_res, w)


def _mix_out_fwd(y_ssm, y_gm, g_ssm, g_gm, w, x, tm, name):
    n = x.shape[0]

    def body(ys_ref, yg_ref, gs_ref, gg_ref, w_ref, x_ref, o_ref):
        a, _, _ = _rms_fwd(ys_ref[...], gs_ref[...])
        b, _, _ = _rms_fwd(yg_ref[...], gg_ref[...])
        o_ref[...] = (x_ref[...] + _dot(a.astype(BF16), w_ref[0:SSM_WIDTH, :])
                      + _dot(b.astype(BF16), w_ref[SSM_WIDTH:D_MODEL, :]))

    row = lambda i: (i, 0)
    fixed = lambda i: (0, 0)
    return pl.pallas_call(
        body, name=name, grid=(n // tm,),
        in_specs=[
            pl.BlockSpec((tm, SSM_WIDTH), row), pl.BlockSpec((tm, GM_WIDTH), row),
            pl.BlockSpec((1, SSM_WIDTH), fixed), pl.BlockSpec((1, GM_WIDTH), fixed),
            pl.BlockSpec((D_MODEL, D_MODEL), fixed), pl.BlockSpec((tm, D_MODEL), row),
        ],
        out_specs=pl.BlockSpec((tm, D_MODEL), row),
        out_shape=jax.ShapeDtypeStruct((n, D_MODEL), F32),
        compiler_params=_cp("parallel"),
    )(y_ssm, y_gm, g_ssm, g_gm, w, x)


def _mix_out_bwd(y_ssm, y_gm, g_ssm, g_gm, w, dx, tm, name):
    n = dx.shape[0]

    def body(ys_ref, yg_ref, gs_ref, gg_ref, w_ref, dx_ref, dys_ref, dyg_ref, dw_ref, dgs_ref, dgg_ref):
        i = pl.program_id(0)

        @pl.when(i == 0)
        def _():
            dw_ref[...] = jnp.zeros_like(dw_ref)
            dgs_ref[...] = jnp.zeros_like(dgs_ref)
            dgg_ref[...] = jnp.zeros_like(dgg_ref)

        dxb = dx_ref[...].astype(BF16)
        parts = ((ys_ref, gs_ref, dys_ref, dgs_ref, 0), (yg_ref, gg_ref, dyg_ref, dgg_ref, SSM_WIDTH))
        for y_ref, g_ref, dy_ref, dg_ref, off in parts:
            g = g_ref[...]
            yn, xh, r = _rms_fwd(y_ref[...], g)
            rows = slice(off, off + SSM_WIDTH)
            dyn = _dot_nt(dxb, w_ref[rows, :])
            dw_ref[rows, :] += _dot_tn(yn.astype(BF16), dxb)
            dy, dgr = _rms_bwd(dyn, xh, r, g)
            dy_ref[...] = dy
            dg_ref[...] += _rows8(dgr)

    row = lambda i: (i, 0)
    fixed = lambda i: (0, 0)
    return pl.pallas_call(
        body, name=name, grid=(n // tm,),
        in_specs=[
            pl.BlockSpec((tm, SSM_WIDTH), row), pl.BlockSpec((tm, GM_WIDTH), row),
            pl.BlockSpec((1, SSM_WIDTH), fixed), pl.BlockSpec((1, GM_WIDTH), fixed),
            pl.BlockSpec((D_MODEL, D_MODEL), fixed), pl.BlockSpec((tm, D_MODEL), row),
        ],
        out_specs=[
            pl.BlockSpec((tm, SSM_WIDTH), row), pl.BlockSpec((tm, GM_WIDTH), row),
            pl.BlockSpec((D_MODEL, D_MODEL), fixed),
            pl.BlockSpec((SUBLANES, SSM_WIDTH), fixed), pl.BlockSpec((SUBLANES, GM_WIDTH), fixed),
        ],
        out_shape=[
            jax.ShapeDtypeStruct((n, SSM_WIDTH), F32), jax.ShapeDtypeStruct((n, GM_WIDTH), F32),
            jax.ShapeDtypeStruct((D_MODEL, D_MODEL), F32),
            jax.ShapeDtypeStruct((SUBLANES, SSM_WIDTH), F32), jax.ShapeDtypeStruct((SUBLANES, GM_WIDTH), F32),
        ],
        compiler_params=_cp("arbitrary"),
    )(y_ssm, y_gm, g_ssm, g_gm, w, dx)


SCAN_W = 512
SCAN_PIECES = HALF_ST // SCAN_W


def _scan_tiles(src_ref, dst_ref, dst_off, c_ref, half, carry_ref, n_tiles, reverse, extra=None):
    shifts = (1, 2, 4)
    carry_row = 0 if reverse else SUBLANES - 1

    def cols(piece, im):
        lo = im * HALF_ST + piece * SCAN_W
        return slice(lo, lo + SCAN_W)

    def step(t, state):
        carries, accs = state
        k = (n_tiles - 1 - t) if reverse else t
        rows = slice(k * SUBLANES, (k + 1) * SUBLANES)
        new_carries, new_accs = [], []
        for piece in range(SCAN_PIECES):
            cr, ci = carries[piece]
            xr0 = src_ref[rows, cols(piece, 0)]
            xi0 = src_ref[rows, cols(piece, 1)]
            xr, xi = xr0, xi0
            for si, s in enumerate(shifts):
                ar = c_ref[half, si, :, cols(piece, 0)]
                ai = c_ref[half, si, :, cols(piece, 1)]
                sh = (SUBLANES - s) if reverse else s
                sr = pltpu.roll(xr, sh, 0)
                sm = pltpu.roll(xi, sh, 0)
                xr, xi = xr + (ar * sr - ai * sm), xi + (ar * sm + ai * sr)
            pr = c_ref[half, 3, :, cols(piece, 0)]
            pi = c_ref[half, 3, :, cols(piece, 1)]
            hr = xr + (pr * cr - pi * ci)
            hi = xi + (pr * ci + pi * cr)
            dst_ref[rows, pl.ds(dst_off + piece * SCAN_W, SCAN_W)] = hr
            dst_ref[rows, pl.ds(dst_off + HALF_ST + piece * SCAN_W, SCAN_W)] = hi
            new_carries.append((jnp.broadcast_to(hr[carry_row:carry_row + 1, :], (SUBLANES, SCAN_W)),
                                jnp.broadcast_to(hi[carry_row:carry_row + 1, :], (SUBLANES, SCAN_W))))
            if extra is not None:
                new_accs.append(extra(rows, piece, (xr0, xi0), (hr, hi), accs[piece]))
        return tuple(new_carries), tuple(new_accs)

    base = half * 2 * HALF_ST
    carries0 = tuple((carry_ref[:, pl.ds(base + p * SCAN_W, SCAN_W)],
                      carry_ref[:, pl.ds(base + HALF_ST + p * SCAN_W, SCAN_W)]) for p in range(SCAN_PIECES))
    zero = jnp.zeros((SUBLANES, SCAN_W), F32)
    accs0 = tuple((zero, zero) for _ in range(SCAN_PIECES)) if extra is not None else ()
    state = (carries0, accs0)
    for t in range(n_tiles):
        state = step(t, state)
    carries, accs = state
    for p in range(SCAN_PIECES):
        carry_ref[:, pl.ds(base + p * SCAN_W, SCAN_W)] = carries[p][0]
        carry_ref[:, pl.ds(base + HALF_ST + p * SCAN_W, SCAN_W)] = carries[p][1]
    return accs


def _ssm_tail(hb, u, c_ref, glu_ref, glub_ref, dskip_ref):
    ypre = u * dskip_ref[...]
    parts = []
    for half in range(2):
        parts.append(_dot(hb[half], c_ref[half]))
    ypre = ypre + jnp.concatenate(parts, axis=1)
    yg, th = _gelu(ypre)
    zz = _dot(yg.astype(BF16), glu_ref[...]) + glub_ref[...]
    z1, z2 = zz[:, :SSM_WIDTH], zz[:, SSM_WIDTH:]
    sg = jax.nn.sigmoid(z2)
    return ypre, th, yg, z1, sg


def _ssm_fwd(z, bblk, cblk, glu, glub, dskip, fwdc, batch, t_chunk, name):
    n = z.shape[0]
    nk = n // batch // t_chunk
    n_tiles = t_chunk // SUBLANES

    def body(u_ref, b_ref, c_ref, glu_ref, glub_ref, dskip_ref, k_ref, y_ref, h_ref, bu_ref, carry_ref):
        @pl.when(pl.program_id(1) == 0)
        def _():
            carry_ref[...] = jnp.zeros_like(carry_ref)

        u = u_ref[...]
        ub = u.astype(BF16)
        for half in range(2):
            bu_ref[half] = _dot(ub[:, half * HALF_IN:(half + 1) * HALF_IN], b_ref[half])
            _scan_tiles(bu_ref.at[half], h_ref, half * 2 * HALF_ST, k_ref, half, carry_ref, n_tiles, False)
        hb = [h_ref[:, half * 2 * HALF_ST:(half + 1) * 2 * HALF_ST].astype(BF16) for half in range(2)]
        _, _, _, z1, sg = _ssm_tail(hb, u, c_ref, glu_ref, glub_ref, dskip_ref)
        y_ref[...] = z1 * sg

    fixed2 = lambda b, k: (0, 0)
    fixed3 = lambda b, k: (0, 0, 0)
    row = lambda b, k: (b * nk + k, 0)
    return pl.pallas_call(
        body, name=name, grid=(batch, nk),
        in_specs=[
            pl.BlockSpec((t_chunk, SSM_WIDTH), row),
            pl.BlockSpec((2, HALF_IN, 2 * HALF_ST), fixed3),
            pl.BlockSpec((2, 2 * HALF_ST, HALF_IN), fixed3),
            pl.BlockSpec((SSM_WIDTH, 2 * SSM_WIDTH), fixed2),
            pl.BlockSpec((1, 2 * SSM_WIDTH), fixed2),
            pl.BlockSpec((1, SSM_WIDTH), fixed2),
            pl.BlockSpec((2, 4, SUBLANES, 2 * HALF_ST), lambda b, k: (0, 0, 0, 0)),
        ],
        out_specs=[pl.BlockSpec((t_chunk, SSM_WIDTH), row), pl.BlockSpec((t_chunk, 4 * HALF_ST), row)],
        out_shape=[jax.ShapeDtypeStruct((n, SSM_WIDTH), F32), jax.ShapeDtypeStruct((n, 4 * HALF_ST), F32)],
        scratch_shapes=[pltpu.VMEM((2, t_chunk, 2 * HALF_ST), F32), pltpu.VMEM((SUBLANES, 4 * HALF_ST), F32)],
        compiler_params=_cp("parallel", "arbitrary"),
    )(z, bblk, cblk, glu, glub, dskip, fwdc)


def _ssm_bwd(z, h, dy, bblk, cblk, glu, glub, dskip, revc, batch, t_chunk, name):
    n = z.shape[0]
    nk = n // batch // t_chunk
    n_tiles = t_chunk // SUBLANES

    def body(u_ref, h_ref, dy_ref, b_ref, c_ref, glu_ref, glub_ref, dskip_ref, k_ref,
             du_ref, dglu_ref, dglub_ref, ddskip_ref, dct_ref, db_ref, q_ref, g_ref, carry_ref):
        first = jnp.logical_and(pl.program_id(0) == 0, pl.program_id(1) == 0)

        @pl.when(first)
        def _():
            for r in (dglu_ref, dglub_ref, ddskip_ref, dct_ref, db_ref, q_ref):
                r[...] = jnp.zeros_like(r)

        @pl.when(pl.program_id(1) == 0)
        def _():
            carry_ref[...] = jnp.zeros_like(carry_ref)

        u = u_ref[...]
        ub = u.astype(BF16)
        hb = [h_ref[:, half * 2 * HALF_ST:(half + 1) * 2 * HALF_ST].astype(BF16) for half in range(2)]
        ypre, th, yg, z1, sg = _ssm_tail(hb, u, c_ref, glu_ref, glub_ref, dskip_ref)
        dout = dy_ref[...]
        dz = jnp.concatenate([dout * sg, dout * z1 * sg * (1.0 - sg)], axis=1)
        dzb = dz.astype(BF16)
        dglu_ref[...] += _dot_tn(yg.astype(BF16), dzb)
        dglub_ref[...] += _rows8(dz)
        dypre = _dot_nt(dzb, glu_ref[...]) * _gelu_grad(ypre, th)
        ddskip_ref[...] += _rows8(dypre * u)
        dypb = dypre.astype(BF16)
        du_parts = []
        for half in range(2):
            dyp_h = dypb[:, half * HALF_IN:(half + 1) * HALF_IN]
            dct_ref[half] += _dot_tn(dyp_h, hb[half])
            g_ref[half] = _dot_nt(dyp_h, c_ref[half])

            def extra(rows, piece, x_in, g_out, acc, half=half):
                er, ei = g_out[0] - x_in[0], g_out[1] - x_in[1]
                base = half * 2 * HALF_ST + piece * SCAN_W
                hr = h_ref[rows, pl.ds(base, SCAN_W)]
                hi = h_ref[rows, pl.ds(base + HALF_ST, SCAN_W)]
                return acc[0] + (er * hr + ei * hi), acc[1] + (er * hi - ei * hr)

            accs = _scan_tiles(g_ref.at[half], g_ref.at[half], 0, k_ref, half, carry_ref, n_tiles, True, extra)
            for piece in range(SCAN_PIECES):
                base = half * 2 * HALF_ST + piece * SCAN_W
                q_ref[:, pl.ds(base, SCAN_W)] += accs[piece][0]
                q_ref[:, pl.ds(base + HALF_ST, SCAN_W)] += accs[piece][1]
            gb = g_ref[half].astype(BF16)
            db_ref[half] += _dot_tn(ub[:, half * HALF_IN:(half + 1) * HALF_IN], gb)
            du_parts.append(_dot_nt(gb, b_ref[half]))
        du_ref[...] = dypre * dskip_ref[...] + jnp.concatenate(du_parts, axis=1)

    fixed2 = lambda b, k: (0, 0)
    fixed3 = lambda b, k: (0, 0, 0)
    row = lambda b, k: (b * nk + (nk - 1 - k), 0)
    return pl.pallas_call(
        body, name=name, grid=(batch, nk),
        in_specs=[
            pl.BlockSpec((t_chunk, SSM_WIDTH), row),
            pl.BlockSpec((t_chunk, 4 * HALF_ST), row),
            pl.BlockSpec((t_chunk, SSM_WIDTH), row),
            pl.BlockSpec((2, HALF_IN, 2 * HALF_ST), fixed3),
            pl.BlockSpec((2, 2 * HALF_ST, HALF_IN), fixed3),
            pl.BlockSpec((SSM_WIDTH, 2 * SSM_WIDTH), fixed2),
            pl.BlockSpec((1, 2 * SSM_WIDTH), fixed2),
            pl.BlockSpec((1, SSM_WIDTH), fixed2),
            pl.BlockSpec((2, 4, SUBLANES, 2 * HALF_ST), lambda b, k: (0, 0, 0, 0)),
        ],
        out_specs=[
            pl.BlockSpec((t_chunk, SSM_WIDTH), row),
            pl.BlockSpec((SSM_WIDTH, 2 * SSM_WIDTH), fixed2),
            pl.BlockSpec((SUBLANES, 2 * SSM_WIDTH), fixed2),
            pl.BlockSpec((SUBLANES, SSM_WIDTH), fixed2),
            pl.BlockSpec((2, HALF_IN, 2 * HALF_ST), fixed3),
            pl.BlockSpec((2, HALF_IN, 2 * HALF_ST), fixed3),
            pl.BlockSpec((SUBLANES, 4 * HALF_ST), fixed2),
        ],
        out_shape=[
            jax.ShapeDtypeStruct((n, SSM_WIDTH), F32),
            jax.ShapeDtypeStruct((SSM_WIDTH, 2 * SSM_WIDTH), F32),
            jax.ShapeDtypeStruct((SUBLANES, 2 * SSM_WIDTH), F32),
            jax.ShapeDtypeStruct((SUBLANES, SSM_WIDTH), F32),
            jax.ShapeDtypeStruct((2, HALF_IN, 2 * HALF_ST), F32),
            jax.ShapeDtypeStruct((2, HALF_IN, 2 * HALF_ST), F32),
            jax.ShapeDtypeStruct((SUBLANES, 4 * HALF_ST), F32),
        ],
        scratch_shapes=[pltpu.VMEM((2, t_chunk, 2 * HALF_ST), F32), pltpu.VMEM((SUBLANES, 4 * HALF_ST), F32)],
        compiler_params=_cp("arbitrary", "arbitrary"),
    )(z, h, dy, bblk, cblk, glu, glub, dskip, revc)


def _gm_chunk_fwd(u, v, gain_ref, w_ref, bias_ref):
    ug, thu = _gelu(u)
    vg, thv = _gelu(v)
    rs, vns, ss = [], [], []
    for hh in range(GM_HEADS):
        cs = slice(hh * GM_HEAD_DIM, (hh + 1) * GM_HEAD_DIM)
        vn, _, r = _rms_fwd(vg[:, cs], gain_ref[:, cs])
        s = _dot(w_ref[hh], vn.astype(BF16)) + bias_ref[:, cs]
        rs.append(r)
        vns.append(vn)
        ss.append(s)
    return ug, thu, thv, vg, rs, vns, ss


def _gm_fwd(z, gain, w_tril, bias, rows, name):
    n = z.shape[0]
    chunks = rows // GM_CHUNK

    def body(u_ref, v_ref, gain_ref, w_ref, bias_ref, y_ref):
        for c in range(chunks):
            rs_ = slice(c * GM_CHUNK, (c + 1) * GM_CHUNK)
            ug, _, _, _, _, _, ss = _gm_chunk_fwd(u_ref[rs_, :], v_ref[rs_, :], gain_ref, w_ref, bias_ref)
            y_ref[rs_, :] = ug * jnp.concatenate(ss, axis=1)

    return pl.pallas_call(
        body, name=name, grid=(n // rows,),
        in_specs=[
            pl.BlockSpec((rows, GM_WIDTH), lambda i: (i, 1)),
            pl.BlockSpec((rows, GM_WIDTH), lambda i: (i, 2)),
            pl.BlockSpec((1, GM_WIDTH), lambda i: (0, 0)),
            pl.BlockSpec((GM_HEADS, GM_CHUNK, GM_CHUNK), lambda i: (0, 0, 0)),
            pl.BlockSpec((GM_CHUNK, GM_WIDTH), lambda i: (0, 0)),
        ],
        out_specs=pl.BlockSpec((rows, GM_WIDTH), lambda i: (i, 0)),
        out_shape=jax.ShapeDtypeStruct((n, GM_WIDTH), F32),
        compiler_params=_cp("parallel"),
    )(z, z, gain, w_tril, bias)


def _gm_bwd(z, dy, gain, w_tril, bias, rows, name):
    n = z.shape[0]
    chunks = rows // GM_CHUNK

    def body(u_ref, v_ref, dy_ref, gain_ref, w_ref, bias_ref, du_ref, dv_ref, dw_ref, dbias_ref, dgain_ref):
        @pl.when(pl.program_id(0) == 0)
        def _():
            dw_ref[...] = jnp.zeros_like(dw_ref)
            dbias_ref[...] = jnp.zeros_like(dbias_ref)
            dgain_ref[...] = jnp.zeros_like(dgain_ref)

        for c in range(chunks):
            rs_ = slice(c * GM_CHUNK, (c + 1) * GM_CHUNK)
            u, v = u_ref[rs_, :], v_ref[rs_, :]
            ug, thu, thv, vg, rs, vns, ss = _gm_chunk_fwd(u, v, gain_ref, w_ref, bias_ref)
            dout = dy_ref[rs_, :]
            ds = dout * ug
            du_ref[rs_, :] = dout * jnp.concatenate(ss, axis=1) * _gelu_grad(u, thu)
            dbias_ref[...] += ds
            dvg_parts, dgain_parts = [], []
            for hh in range(GM_HEADS):
                cs = slice(hh * GM_HEAD_DIM, (hh + 1) * GM_HEAD_DIM)
                dsb = ds[:, cs].astype(BF16)
                dvn = _dot_tn(w_ref[hh], dsb)
                dw_ref[hh] += _dot_nt(dsb, vns[hh].astype(BF16))
                g = gain_ref[:, cs]
                xh = vg[:, cs] * rs[hh]
                dvg, dgr = _rms_bwd(dvn, xh, rs[hh], g)
                dvg_parts.append(dvg)
                dgain_parts.append(dgr)
            dv_ref[rs_, :] = jnp.concatenate(dvg_parts, axis=1) * _gelu_grad(v, thv)
            dgain_ref[...] += _rows8(jnp.concatenate(dgain_parts, axis=1))

    row = lambda i: (i, 0)
    return pl.pallas_call(
        body, name=name, grid=(n // rows,),
        in_specs=[
            pl.BlockSpec((rows, GM_WIDTH), lambda i: (i, 1)),
            pl.BlockSpec((rows, GM_WIDTH), lambda i: (i, 2)),
            pl.BlockSpec((rows, GM_WIDTH), row),
            pl.BlockSpec((1, GM_WIDTH), lambda i: (0, 0)),
            pl.BlockSpec((GM_HEADS, GM_CHUNK, GM_CHUNK), lambda i: (0, 0, 0)),
            pl.BlockSpec((GM_CHUNK, GM_WIDTH), lambda i: (0, 0)),
        ],
        out_specs=[
            pl.BlockSpec((rows, GM_WIDTH), row), pl.BlockSpec((rows, GM_WIDTH), row),
            pl.BlockSpec((GM_HEADS, GM_CHUNK, GM_CHUNK), lambda i: (0, 0, 0)),
            pl.BlockSpec((GM_CHUNK, GM_WIDTH), lambda i: (0, 0)),
            pl.BlockSpec((SUBLANES, GM_WIDTH), lambda i: (0, 0)),
        ],
        out_shape=[
            jax.ShapeDtypeStruct((n, GM_WIDTH), F32), jax.ShapeDtypeStruct((n, GM_WIDTH), F32),
            jax.ShapeDtypeStruct((GM_HEADS, GM_CHUNK, GM_CHUNK), F32),
            jax.ShapeDtypeStruct((GM_CHUNK, GM_WIDTH), F32),
            jax.ShapeDtypeStruct((SUBLANES, GM_WIDTH), F32),
        ],
        compiler_params=_cp("arbitrary"),
    )(z, z, dy, gain, w_tril, bias)


def _loss_head(x, gain, target, tm, name):
    n = x.shape[0]

    def body(x_ref, g_ref, t_ref, dx_ref, sq_ref, dgain_ref):
        @pl.when(pl.program_id(0) == 0)
        def _():
            sq_ref[...] = jnp.zeros_like(sq_ref)
            dgain_ref[...] = jnp.zeros_like(dgain_ref)

        g = g_ref[...]
        y, xh, r = _rms_fwd(x_ref[...], g)
        err = y - t_ref[...]
        sq_ref[...] += _rows8(err * err)
        dx, dgr = _rms_bwd(err * (1.0 / D_MODEL), xh, r, g)
        dx_ref[...] = dx
        dgain_ref[...] += _rows8(dgr)

    row = lambda i: (i, 0)
    fixed = lambda i: (0, 0)
    return pl.pallas_call(
        body, name=name, grid=(n // tm,),
        in_specs=[pl.BlockSpec((tm, D_MODEL), row), pl.BlockSpec((1, D_MODEL), fixed), pl.BlockSpec((tm, D_MODEL), row)],
        out_specs=[pl.BlockSpec((tm, D_MODEL), row), pl.BlockSpec((SUBLANES, D_MODEL), fixed),
                   pl.BlockSpec((SUBLANES, D_MODEL), fixed)],
        out_shape=[jax.ShapeDtypeStruct((n, D_MODEL), F32), jax.ShapeDtypeStruct((SUBLANES, D_MODEL), F32),
                   jax.ShapeDtypeStruct((SUBLANES, D_MODEL), F32)],
        compiler_params=_cp("arbitrary"),
    )(x, gain, target)


def _adam_math(w, g, m, v):
    m2 = ADAM_B1 * m + (1.0 - ADAM_B1) * g
    v2 = ADAM_B2 * v + (1.0 - ADAM_B2) * (g * g)
    m_hat = m2 / (1.0 - ADAM_B1 ** ADAM_STEP)
    v_hat = v2 / (1.0 - ADAM_B2 ** ADAM_STEP)
    delta = -ADAM_LR * (m_hat / (jnp.sqrt(v_hat) + ADAM_EPS) + ADAM_WD * w)
    return delta, m2, v2


def _adam_sharded(parts, w, m, v, layer, earlier, name):
    depth, r, c = w.shape
    tr = max(t for t in range(16, 129, 16) if r % t == 0)

    def body(p_ref, w_ref, m_ref, v_ref, *rest):
        g_ref, d_ref, m2_ref, v2_ref = rest[-4:]
        g = p_ref[0].astype(F32)
        for s in range(1, N_DEV):
            g = g + p_ref[s].astype(F32)
        delta, m2, v2 = _adam_math(w_ref[...], g, m_ref[...], v_ref[...])
        g_ref[...] = g
        d_ref[...] = delta
        m2_ref[...] = m2
        v2_ref[...] = v2

    blk = pl.BlockSpec((None, tr, c), lambda i: (layer, i, 0))
    extra = [] if earlier is None else list(earlier)
    return pl.pallas_call(
        body, name=name, grid=(r // tr,),
        in_specs=[pl.BlockSpec((N_DEV, tr, c), lambda i: (0, i, 0)), blk, blk, blk]
        + [pl.BlockSpec(memory_space=pl.ANY)] * len(extra),
        out_specs=[blk, blk, blk, blk],
        out_shape=[jax.ShapeDtypeStruct((depth, r, c), F32)] * 4,
        input_output_aliases={4 + i: i for i in range(len(extra))},
        compiler_params=_cp("parallel"),
    )(parts, w, m, v, *extra)


def _adam_packed(g, w, m, v, name):
    r, c = g.shape

    def body(g_ref, w_ref, m_ref, v_ref, d_ref, m2_ref, v2_ref):
        delta, m2, v2 = _adam_math(w_ref[...], g_ref[...], m_ref[...], v_ref[...])
        d_ref[...] = delta
        m2_ref[...] = m2
        v2_ref[...] = v2

    blk = pl.BlockSpec((r, c), lambda i: (0, 0))
    return pl.pallas_call(
        body, name=name, grid=(1,),
        in_specs=[blk, blk, blk, blk], out_specs=[blk, blk, blk],
        out_shape=[jax.ShapeDtypeStruct((r, c), F32)] * 3,
        compiler_params=_cp("arbitrary"),
    )(g, w, m, v)


def _my_place():
    return lax.axis_index("x"), lax.axis_index("y"), lax.axis_index("c")


def _flip(place, rel):
    x, y, c = place
    return (1 - x if rel & 4 else x, 1 - y if rel & 2 else y, 1 - c if rel & 1 else c)


def _index(place):
    return 4 * place[0] + 2 * place[1] + place[2]


def _all_gather(shards, name):
    na = len(shards)

    def body(*refs):
        xs, outs = refs[:na], refs[na:2 * na]
        send_sems, recv_sems, local_sems = refs[2 * na:]
        me = _my_place()
        sibling = _flip(me, 1)
        chips = [_flip(me, 4), _flip(me, 2), _flip(me, 6)]

        def copy(a, k, block, to, src=None):
            slot = outs[a].at[_index(block)]
            return pltpu.make_async_remote_copy(
                src_ref=slot if src is None else src, dst_ref=slot,
                send_sem=send_sems.at[a, k], recv_sem=recv_sems.at[a, k],
                device_id=to, device_id_type=MESH)

        mine = [pltpu.make_async_copy(xs[a], outs[a].at[_index(me)], local_sems.at[a]) for a in range(na)]
        for cp in mine:
            cp.start()
        first = []
        for a in range(na):
            first.append(copy(a, 0, me, sibling, src=xs[a]))
            first += [copy(a, 1 + j, me, chip, src=xs[a]) for j, chip in enumerate(chips)]
        for cp in first:
            cp.start()
        passed = []
        for a in range(na):
            for j, chip in enumerate(chips):
                copy(a, 1 + j, chip, me).wait_recv()
                fwd = copy(a, 4 + j, chip, sibling)
                fwd.start()
                passed.append(fwd)
        for a in range(na):
            copy(a, 0, sibling, me).wait_recv()
            for j, chip in enumerate(chips):
                copy(a, 4 + j, _flip(chip, 1), me).wait_recv()
        for cp in first + passed:
            cp.wait_send()
        for cp in mine:
            cp.wait()

    hbm = pl.BlockSpec(memory_space=pl.ANY)
    return pl.pallas_call(
        body, name=name,
        in_specs=[hbm] * na, out_specs=[hbm] * na,
        out_shape=[jax.ShapeDtypeStruct((N_DEV,) + s.shape, s.dtype) for s in shards],
        scratch_shapes=[pltpu.SemaphoreType.DMA((na, 7)), pltpu.SemaphoreType.DMA((na, 7)),
                        pltpu.SemaphoreType.DMA((na,))],
    )(*shards)


_HBM = pl.BlockSpec(memory_space=pltpu.HBM)
_SEM = pl.BlockSpec(memory_space=pltpu.SEMAPHORE)
_EFFECT = pltpu.SideEffectType.DATAFLOW_SIDE_EFFECTING


def _exchange_copy(src_ref, land_ref, send_sems, recv_sems, a, rel, me, scatter, landed):
    peer = _flip(me, rel)
    src = src_ref.at[_index(peer)] if scatter else src_ref
    return pltpu.make_async_remote_copy(
        src_ref=src, dst_ref=land_ref.at[_index(peer if landed else me)],
        send_sem=send_sems.at[a * (N_DEV - 1) + rel - 1], recv_sem=recv_sems.at[a * (N_DEV - 1) + rel - 1],
        device_id=peer, device_id_type=MESH)


def _own_slot(data, me, scatter):
    if scatter:
        own = lax.dynamic_slice_in_dim(data, me, 1, axis=0)
        shape = data.shape
    else:
        own = data[None]
        shape = (N_DEV,) + data.shape
    start = (me,) + (0,) * (len(shape) - 1)
    return lax.dynamic_update_slice(lax.empty(shape, data.dtype), own, start)


def _exchange_start(groups, me, scatter, name, after=None):
    sizes = [len(g) for g in groups]
    srcs = [a for g in groups for a in g]
    lands = [_own_slot(a, me, scatter) for a in srcs]
    na, ng = len(srcs), len(groups)
    deps = [] if after is None else [after]

    def body(*refs):
        src_refs, land_refs = refs[:na], refs[na:2 * na]
        sems = refs[2 * na + len(deps):2 * na + len(deps) + 2 * ng]
        token = refs[-1]
        place = _my_place()
        a = 0
        for g, size in enumerate(sizes):
            for k in range(size):
                for rel in range(1, N_DEV):
                    _exchange_copy(src_refs[a], land_refs[a], sems[2 * g], sems[2 * g + 1], k, rel, place, scatter,
                                   False).start()
                a += 1
        token[...] = jnp.zeros_like(token)

    sem_shapes = [pltpu.SemaphoreType.DMA((size * (N_DEV - 1),)) for size in sizes for _ in range(2)]
    outs = pl.pallas_call(
        body, name=name,
        in_specs=[_HBM] * (2 * na) + [pl.BlockSpec(memory_space=pl.ANY)] * len(deps),
        out_specs=[_SEM] * (2 * ng) + [_HBM] * (2 * na) + [pl.BlockSpec(memory_space=pltpu.VMEM)],
        out_shape=sem_shapes + [pltpu.HBM(a.shape, a.dtype) for a in srcs + lands]
        + [jax.ShapeDtypeStruct((SUBLANES, LANES), F32)],
        input_output_aliases={i: 2 * ng + i for i in range(2 * na)},
        compiler_params=pltpu.CompilerParams(has_side_effects=_EFFECT),
    )(*[pltpu.with_memory_space_constraint(a, pltpu.HBM) for a in srcs + lands], *deps)
    sems, thru, token = outs[:2 * ng], outs[2 * ng:2 * ng + 2 * na], outs[-1]
    handles, a = [], 0
    for g, size in enumerate(sizes):
        handles.append((sems[2 * g], sems[2 * g + 1], thru[a:a + size], thru[na + a:na + a + size]))
        a += size
    return handles, token


def _exchange_wait(handle, after, scatter, name):
    send_sems, recv_sems, srcs, lands = handle
    na = len(srcs)

    def body(*refs):
        src_refs, land_refs = refs[:na], refs[na:2 * na]
        send_ref, recv_ref = refs[2 * na], refs[2 * na + 1]
        place = _my_place()
        for a in range(na):
            for rel in range(1, N_DEV):
                cp = _exchange_copy(src_refs[a], land_refs[a], send_ref, recv_ref, a, rel, place, scatter, True)
                cp.wait_send()
                cp.wait_recv()

    outs = pl.pallas_call(
        body, name=name,
        in_specs=[_HBM] * (2 * na) + [_SEM, _SEM, pl.BlockSpec(memory_space=pl.ANY)],
        out_specs=[_HBM] * (2 * na),
        out_shape=[pltpu.HBM(a.shape, a.dtype) for a in list(srcs) + list(lands)],
        input_output_aliases={i: i for i in range(2 * na)},
        compiler_params=pltpu.CompilerParams(has_side_effects=_EFFECT),
    )(*srcs, *lands, send_sems, recv_sems, after)
    return outs[na:]


def _behind(arr, token):
    return arr + token[0:1, 0:1]


def _all_reduce_small(g, name):
    _, r, c = g.shape

    def body(g_ref, o_ref, land_ref, red_ref, send1, recv1, send2, recv2):
        me = _my_place()
        idx = _index(me)

        def scatter(rel):
            peer = _flip(me, rel)
            return pltpu.make_async_remote_copy(
                src_ref=g_ref.at[_index(peer)], dst_ref=land_ref.at[idx],
                send_sem=send1.at[rel - 1], recv_sem=recv1.at[rel - 1], device_id=peer, device_id_type=MESH)

        def gather(rel):
            peer = _flip(me, rel)
            return pltpu.make_async_remote_copy(
                src_ref=red_ref, dst_ref=o_ref.at[idx],
                send_sem=send2.at[rel - 1], recv_sem=recv2.at[rel - 1], device_id=peer, device_id_type=MESH)

        for rel in range(1, N_DEV):
            scatter(rel).start()
        land_ref[idx] = g_ref[idx]
        for rel in range(1, N_DEV):
            scatter(rel).wait()
        acc = land_ref[0]
        for s in range(1, N_DEV):
            acc = acc + land_ref[s]
        red_ref[...] = acc
        for rel in range(1, N_DEV):
            gather(rel).start()
        o_ref[idx] = acc
        for rel in range(1, N_DEV):
            gather(rel).wait()

    vmem = pl.BlockSpec(memory_space=pltpu.VMEM)
    return pl.pallas_call(
        body, name=name,
        in_specs=[vmem], out_specs=vmem,
        out_shape=jax.ShapeDtypeStruct(g.shape, F32),
        scratch_shapes=[pltpu.VMEM(g.shape, F32), pltpu.VMEM((r, c), F32)]
        + [pltpu.SemaphoreType.DMA((N_DEV - 1,))] * 4,
        compiler_params=pltpu.CompilerParams(vmem_limit_bytes=VMEM_LIMIT),
    )(g)


def _ssm_discretize(a_re, a_im, log_dt, b_re, b_im):
    dt = jnp.exp(log_dt)[:, None]
    mag = jnp.exp(a_re * dt)
    lr, li = mag * jnp.cos(a_im * dt), mag * jnp.sin(a_im * dt)
    den = a_re * a_re + a_im * a_im
    qr = ((lr - 1.0) * a_re + li * a_im) / den
    qi = (li * a_re - (lr - 1.0) * a_im) / den
    bbr = qr[..., None] * b_re - qi[..., None] * b_im
    bbi = qr[..., None] * b_im + qi[..., None] * b_re
    return lr, li, bbr, bbi


def _halves(a):
    return a.reshape((2, HALF_GROUPS) + a.shape[1:])


def _block_diag(blocks):
    g, r, c = blocks.shape
    eye = jnp.eye(g, dtype=blocks.dtype)
    return jnp.einsum("grc,gh->grhc", blocks, eye).reshape(g * r, g * c)


def _block_diag_take(dense, g, r, c):
    return jnp.einsum("grhc,gh->grc", dense.reshape(g, r, g, c), jnp.eye(g, dtype=dense.dtype))


def _ssm_matrices(bbr, bbi, c_re, c_im, glu_w, glu_b, d_skip):
    bre, bim = _halves(jnp.swapaxes(bbr, 1, 2)), _halves(jnp.swapaxes(bbi, 1, 2))
    bblk = jnp.stack([jnp.concatenate([_block_diag(bre[h]), _block_diag(bim[h])], axis=1) for h in range(2)])
    cre, cim = _halves(jnp.swapaxes(c_re, 1, 2)), _halves(jnp.swapaxes(c_im, 1, 2))
    cblk = jnp.stack([jnp.concatenate([_block_diag(cre[h]), -_block_diag(cim[h])], axis=0) for h in range(2)])
    glu = jnp.concatenate([_block_diag(glu_w[:, :, :SSM_CH]), _block_diag(glu_w[:, :, SSM_CH:])], axis=1)
    glub = jnp.concatenate([glu_b[:, :SSM_CH].reshape(1, -1), glu_b[:, SSM_CH:].reshape(1, -1)], axis=1)
    return bblk.astype(BF16), cblk.astype(BF16), glu.astype(BF16), glub, d_skip.reshape(1, -1)


def _scan_constants(lr, li, reverse):
    if reverse:
        li = -li
    pows = [(lr, li)]
    for _ in range(SUBLANES - 1):
        pr, pi = pows[-1]
        pows.append((pr * lr - pi * li, pr * li + pi * lr))
    row = jnp.arange(SUBLANES)[:, None]

    def flat(a):
        return a.reshape(2, 1, HALF_ST)

    mats = []
    for s in (1, 2, 4):
        keep = (row + s <= SUBLANES - 1) if reverse else (row >= s)
        mats.append(tuple(jnp.where(keep[None], flat(p), 0.0) for p in pows[s - 1]))
    order = [SUBLANES - 1 - j for j in range(SUBLANES)] if reverse else list(range(SUBLANES))
    mats.append(tuple(jnp.concatenate([flat(pows[j][k]) for j in order], axis=1) for k in range(2)))
    return jnp.stack([jnp.concatenate([m[0], m[1]], axis=2) for m in mats], axis=1)


def _pack(arrs, rows):
    flat = jnp.concatenate([a.reshape(-1) for a in arrs])
    return jnp.pad(flat, (0, rows * LANES - flat.shape[0])).reshape(rows, LANES)


def _unpack(buf, like):
    flat = buf.reshape(-1)
    out, off = [], 0
    for a in like:
        out.append(flat[off:off + a.size].reshape(a.shape))
        off += a.size
    return out


SMALL = ("norm_ffn1", "norm_mix", "ssm_a_re", "ssm_a_im", "ssm_log_dt", "ssm_b_re", "ssm_b_im", "ssm_c_re",
         "ssm_c_im", "ssm_d", "ssm_glu_w", "ssm_glu_b", "gm_v_gain", "gm_w_s", "gm_b_s", "gain_ssm_out",
         "gain_gm_out", "norm_ffn2", "norm_final")
BIG = ("ffn1_w_in", "ffn1_w_out", "mix_w_in", "mix_w_out", "ffn2_w_in", "ffn2_w_out")
TRANSPOSED = ("ffn1_w_in", "mix_w_in", "ffn2_w_in")
WEIGHTS = ("norm_ffn1", "ffn1_w_in", "ffn1_w_out", "norm_mix", "mix_w_in", "ssm_a_re", "ssm_a_im", "ssm_log_dt",
           "ssm_b_re", "ssm_b_im", "ssm_c_re", "ssm_c_im", "ssm_d", "ssm_glu_w", "ssm_glu_b", "gm_v_gain", "gm_w_s",
           "gm_b_s", "gain_ssm_out", "gain_gm_out", "mix_w_out", "norm_ffn2", "ffn2_w_in", "ffn2_w_out", "norm_final")


def _step(x, target, w, m, v):
    batch, seq, _ = x.shape
    n = batch * seq
    depth = w["norm_ffn1"].shape[0]
    tm = min(512, n)
    tm_ffn = min(1024, n)
    tk = min(2048, n)
    t_chunk = min(256, seq)
    gm_rows = min(512, seq)
    x = x.reshape(n, D_MODEL)
    target = target.reshape(n, D_MODEL)

    assert depth == 2
    me = _index(_my_place())
    shard = lambda group, l: [w[f"{group}_w_in"][l].astype(BF16), w[f"{group}_w_out"][l].astype(BF16)]
    batches = ([("mix", 0), ("ffn2", 0)], [("ffn1", 1), ("mix", 1)], [("ffn2", 1)])
    gathered, pending = {("ffn1", 0): tuple(_all_gather(shard("ffn1", 0), "all_gather_first"))}, {}

    def gather_start(i, after):
        handles, tok = _exchange_start([shard(g, l) for g, l in batches[i]], me, False, f"all_gather_start_{i}",
                                       after)
        pending.update(zip(batches[i], handles))
        return tok

    def weights(group, l, after=None):
        if (group, l) not in gathered:
            w_in, w_out = _exchange_wait(pending[(group, l)], after, False, f"all_gather_wait_{group}_{l}")
            if group == "mix":
                w_in = jnp.transpose(w_in, (1, 0, 2)).reshape(D_MODEL, IN_COLS)
                w_out = w_out.reshape(D_MODEL, D_MODEL)
            gathered[(group, l)] = (w_in, w_out)
        return gathered[(group, l)]

    tril = jnp.tril(jnp.ones((GM_CHUNK, GM_CHUNK), bool))
    layers = []
    for l in range(depth):
        disc, disc_vjp = jax.vjp(_ssm_discretize, w["ssm_a_re"][l], w["ssm_a_im"][l], w["ssm_log_dt"][l],
                                 w["ssm_b_re"][l], w["ssm_b_im"][l])
        lr, li, bbr, bbi = disc
        bblk, cblk, glu, glub, dskip = _ssm_matrices(bbr, bbi, w["ssm_c_re"][l], w["ssm_c_im"][l],
                                                     w["ssm_glu_w"][l], w["ssm_glu_b"][l], w["ssm_d"][l])
        layers.append(dict(
            disc_vjp=disc_vjp, lr=lr, li=li, bblk=bblk, cblk=cblk, glu=glu, glub=glub, dskip=dskip,
            fwdc=_scan_constants(lr, li, False), revc=_scan_constants(lr, li, True),
            w_tril=jnp.where(tril[None], w["gm_w_s"][l], 0.0).astype(BF16),
            gm_bias=jnp.repeat(w["gm_b_s"][l].T, GM_HEAD_DIM, axis=1),
            g1=w["norm_ffn1"][l][None], gmix=w["norm_mix"][l][None], g2=w["norm_ffn2"][l][None],
            gv=w["gm_v_gain"][l][None], gs=w["gain_ssm_out"][l][None], gg=w["gain_gm_out"][l][None],
        ))

    saved = []
    for l in range(depth):
        p = layers[l]
        x0 = x
        g1, gmix, g2 = p["g1"], p["gmix"], p["g2"]
        w_in, w_out = weights("ffn1", l, x0)
        if l == 0:
            g1 = _behind(g1, gather_start(0, w_in))
        x1, xn1, gu1 = _ffn_fwd(x0, g1, w_in, w_out, tm_ffn, f"ffn1_fwd_{l}")
        if l == 0:
            gmix = _behind(gmix, gather_start(1, x1))
        mwi, mwo = weights("mix", l, x1)
        z = _mix_in_fwd(x1, gmix, mwi, tm, f"mix_in_fwd_{l}")
        y_ssm, h = _ssm_fwd(z, p["bblk"], p["cblk"], p["glu"], p["glub"], p["dskip"], p["fwdc"], batch, t_chunk,
                            f"ssm_fwd_{l}")
        y_gm = _gm_fwd(z, p["gv"], p["w_tril"], p["gm_bias"], gm_rows, f"gm_fwd_{l}")
        x2 = _mix_out_fwd(y_ssm, y_gm, p["gs"], p["gg"], mwo, x1, tm, f"mix_out_fwd_{l}")
        if l == 0:
            g2 = _behind(g2, gather_start(2, x2))
        x, xn2, gu2 = _ffn_fwd(x2, g2, *weights("ffn2", l, x2), tm_ffn, f"ffn2_fwd_{l}")
        saved.append((x0, x1, x2, z, h, y_ssm, y_gm, xn1, gu1, xn2, gu2))

    dx, sq, dnf = _loss_head(x, w["norm_final"][None], target, tm, "loss_head")
    loss = lax.psum((0.5 / D_MODEL) * jnp.sum(sq), AXES)

    small = {k: [None] * depth for k in SMALL if k != "norm_final"}
    sent = []

    def send(group, l, keys, parts):
        (handle,), tok = _exchange_start([parts], me, True, f"reduce_scatter_start_{group}_{l}")
        sent.append((group, l, keys, handle))
        return tok

    token = None
    for l in reversed(range(depth)):
        p = layers[l]
        x0, x1, x2, z, h, y_ssm, y_gm, xn1, gu1, xn2, gu2 = saved[l]
        mwi, mwo = weights("mix", l)
        dx_out = dx
        g2 = p["g2"] if token is None else _behind(p["g2"], token)
        dx, dgu, act, dgain = _ffn_bwd(x2, g2, dx_out, gu2, *weights("ffn2", l), tm, f"ffn2_bwd_{l}")
        dw_in = _ffn_dw_in(xn2, dgu, tk, f"ffn2_dw_in_{l}")
        dw_out = _ffn_dw_out(act, dx_out, tk, f"ffn2_dw_out_{l}").reshape(N_DEV, FF_SHARD // 2, D_MODEL)
        token = send("ffn2", l, ("ffn2_w_in", "ffn2_w_out"), [dw_in, dw_out])
        small["norm_ffn2"][l] = dgain.sum(0)

        dy_ssm, dy_gm, dwo, dgs, dgg = _mix_out_bwd(y_ssm, y_gm, _behind(p["gs"], token), p["gg"], mwo, dx, tm,
                                                    f"mix_out_bwd_{l}")
        dwo = dwo.astype(BF16).reshape(N_DEV, D_MODEL // N_DEV, D_MODEL)
        small["gain_ssm_out"][l] = dgs.sum(0)
        small["gain_gm_out"][l] = dgg.sum(0)

        du_ssm, dglu, dglub, ddskip, dct, db, q = _ssm_bwd(
            z, h, dy_ssm, p["bblk"], p["cblk"], p["glu"], p["glub"], p["dskip"], p["revc"], batch, t_chunk,
            f"ssm_bwd_{l}")
        du_gm, dv_gm, dws, dbias, dgv = _gm_bwd(z, dy_gm, p["gv"], p["w_tril"], p["gm_bias"], gm_rows, f"gm_bwd_{l}")

        q = q.sum(0).reshape(2, 2, HALF_GROUPS, SSM_STATE)
        qr, qi = q[:, 0].reshape(SSM_GROUPS, SSM_STATE), q[:, 1].reshape(SSM_GROUPS, SSM_STATE)
        den = p["lr"] * p["lr"] + p["li"] * p["li"]
        d_re = (qr * p["lr"] + qi * p["li"]) / den
        d_im = (qi * p["lr"] - qr * p["li"]) / den
        dbb = jnp.stack([_block_diag_take(db[hf, :, k * HALF_ST:(k + 1) * HALF_ST], HALF_GROUPS, SSM_CH, SSM_STATE)
                         for k in range(2) for hf in range(2)]).reshape(2, SSM_GROUPS, SSM_CH, SSM_STATE)
        dcc = jnp.stack([_block_diag_take(dct[hf, :, k * HALF_ST:(k + 1) * HALF_ST], HALF_GROUPS, SSM_CH, SSM_STATE)
                         for k in range(2) for hf in range(2)]).reshape(2, SSM_GROUPS, SSM_CH, SSM_STATE)
        da_re, da_im, dlog_dt, db_re, db_im = p["disc_vjp"](
            (d_re, -d_im, jnp.swapaxes(dbb[0], 1, 2), jnp.swapaxes(dbb[1], 1, 2)))
        small["ssm_a_re"][l], small["ssm_a_im"][l], small["ssm_log_dt"][l] = da_re, da_im, dlog_dt
        small["ssm_b_re"][l], small["ssm_b_im"][l] = db_re, db_im
        small["ssm_c_re"][l], small["ssm_c_im"][l] = dcc[0], -dcc[1]
        small["ssm_d"][l] = ddskip.sum(0).reshape(SSM_GROUPS, SSM_CH)
        small["ssm_glu_w"][l] = jnp.concatenate(
            [_block_diag_take(dglu[:, :SSM_WIDTH], SSM_GROUPS, SSM_CH, SSM_CH),
             _block_diag_take(dglu[:, SSM_WIDTH:], SSM_GROUPS, SSM_CH, SSM_CH)], axis=2)
        dglub = dglub.sum(0)
        small["ssm_glu_b"][l] = jnp.concatenate(
            [dglub[:SSM_WIDTH].reshape(SSM_GROUPS, SSM_CH), dglub[SSM_WIDTH:].reshape(SSM_GROUPS, SSM_CH)], axis=1)
        small["gm_v_gain"][l] = dgv.sum(0)
        small["gm_w_s"][l] = jnp.where(tril[None], dws, 0.0)
        small["gm_b_s"][l] = dbias.reshape(GM_CHUNK, GM_HEADS, GM_HEAD_DIM).sum(-1).T

        dx, dwi, dgain = _mix_in_bwd(x1, p["gmix"], du_ssm, du_gm, dv_gm, dx, mwi, tm, f"mix_in_bwd_{l}")
        dwi = dwi.astype(BF16).reshape(N_DEV, IN_COLS // N_DEV, D_MODEL)
        token = send("mix", l, ("mix_w_in", "mix_w_out"), [dwi, dwo])
        small["norm_mix"][l] = dgain.sum(0)

        dx_out = dx
        dx, dgu, act, dgain = _ffn_bwd(x0, _behind(p["g1"], token), dx_out, gu1, *weights("ffn1", l), tm,
                                       f"ffn1_bwd_{l}")
        small["norm_ffn1"][l] = dgain.sum(0)
        if l > 0:
            dw_in = _ffn_dw_in(xn1, dgu, tk, f"ffn1_dw_in_{l}")
            dw_out = _ffn_dw_out(act, dx_out, tk, f"ffn1_dw_out_{l}").reshape(N_DEV, FF_SHARD // 2, D_MODEL)
            token = send("ffn1", l, ("ffn1_w_in", "ffn1_w_out"), [dw_in, dw_out])
            continue
        small_g = [jnp.stack(small[k]) if k != "norm_final" else dnf.sum(0) for k in SMALL]
        total = sum(int(math.prod(w[k].shape)) for k in SMALL)
        rows = -(-total // (LANES * N_DEV * SUBLANES)) * N_DEV * SUBLANES
        g_all = _all_reduce_small(_pack(small_g, rows).reshape(N_DEV, rows // N_DEV, LANES), "all_reduce_small")
        dw_in = _ffn_dw_in(xn1, dgu, tk, f"ffn1_dw_in_{l}", after=g_all)
        token = send("ffn1_in", l, ("ffn1_w_in",), [dw_in])
        dw_out = _ffn_dw_out(act, dx_out, tk, f"ffn1_dw_out_{l}", after=token).reshape(
            N_DEV, FF_SHARD // 2, D_MODEL)
        token = send("ffn1_out", l, ("ffn1_w_out",), [dw_out])

    grad_x = dx.reshape(batch, seq, D_MODEL)
    grads, deltas, new_m, new_v = {}, {}, {}, {}

    g_all = _behind(g_all.reshape(rows, LANES), token)
    like = [w[k] for k in SMALL]
    d_p, m_p, v_p = _adam_packed(g_all, _pack(like, rows), _pack([m[k] for k in SMALL], rows),
                                 _pack([v[k] for k in SMALL], rows), "adam_small")
    for k, g_, d_, m_, v_ in zip(SMALL, _unpack(g_all, like), _unpack(d_p, like), _unpack(m_p, like),
                                 _unpack(v_p, like)):
        grads[k], deltas[k], new_m[k], new_v[k] = g_, d_, m_, v_

    results = {}
    after = d_p
    for group, l, keys, handle in sent:
        landed = _exchange_wait(handle, after, True, f"reduce_scatter_wait_{group}_{l}")
        for k, parts in zip(keys, landed):
            view = (lambda a: jnp.swapaxes(a, 1, 2)) if k in TRANSPOSED else (lambda a: a)
            results[k] = _adam_sharded(parts, view(w[k]), view(m[k]), view(v[k]), l, results.get(k),
                                       f"adam_{k}_{l}")
            after = results[k][0]
    for k in BIG:
        view = (lambda a: jnp.swapaxes(a, 1, 2)) if k in TRANSPOSED else (lambda a: a)
        grads[k], deltas[k], new_m[k], new_v[k] = [view(a) for a in results[k]]
    return loss, grad_x, grads, deltas, new_m, new_v


def kernel(x, norm_ffn1, ffn1_w_in, ffn1_w_out, norm_mix, mix_w_in, ssm_a_re, ssm_a_im, ssm_log_dt, ssm_b_re, ssm_b_im, ssm_c_re, ssm_c_im, ssm_d, ssm_glu_w, ssm_glu_b, gm_v_gain, gm_w_s, gm_b_s, gain_ssm_out, gain_gm_out, mix_w_out, norm_ffn2, ffn2_w_in, ffn2_w_out, norm_final, loss_target, m_norm_ffn1, m_ffn1_w_in, m_ffn1_w_out, m_norm_mix, m_mix_w_in, m_ssm_a_re, m_ssm_a_im, m_ssm_log_dt, m_ssm_b_re, m_ssm_b_im, m_ssm_c_re, m_ssm_c_im, m_ssm_d, m_ssm_glu_w, m_ssm_glu_b, m_gm_v_gain, m_gm_w_s, m_gm_b_s, m_gain_ssm_out, m_gain_gm_out, m_mix_w_out, m_norm_ffn2, m_ffn2_w_in, m_ffn2_w_out, m_norm_final, v_norm_ffn1, v_ffn1_w_in, v_ffn1_w_out, v_norm_mix, v_mix_w_in, v_ssm_a_re, v_ssm_a_im, v_ssm_log_dt, v_ssm_b_re, v_ssm_b_im, v_ssm_c_re, v_ssm_c_im, v_ssm_d, v_ssm_glu_w, v_ssm_glu_b, v_gm_v_gain, v_gm_w_s, v_gm_b_s, v_gain_ssm_out, v_gain_gm_out, v_mix_w_out, v_norm_ffn2, v_ffn2_w_in, v_ffn2_w_out, v_norm_final):
    args = locals()
    w = {k: args[k] for k in WEIGHTS}
    m = {k: args["m_" + k] for k in WEIGHTS}
    v = {k: args["v_" + k] for k in WEIGHTS}
    loss, grad_x, grads, deltas, new_m, new_v = _step(x, loss_target, w, m, v)
    return (loss, grad_x, *[grads[k] for k in WEIGHTS], *[deltas[k] for k in WEIGHTS],
            *[new_m[k] for k in WEIGHTS], *[new_v[k] for k in WEIGHTS])
```

```python
import functools
import math

import jax
import jax.numpy as jnp
from jax import lax
from jax.experimental import pallas as pl
from jax.experimental.pallas import tpu as pltpu

F32 = jnp.float32
BF16 = jnp.bfloat16
MESH = pl.DeviceIdType.MESH
AXES = ("x", "y", "c")

N_DEV = 8
D_MODEL = 1024
D_FF = 2816
FF_SHARD = 2 * D_FF // N_DEV
FF_CHUNKS = 4
MXU_DIM = 256
FF_PIECES = tuple((lo, min(lo + MXU_DIM, FF_SHARD)) for lo in range(0, FF_SHARD, MXU_DIM))
SSM_WIDTH = 512
SSM_CH = 16
SSM_GROUPS = 32
SSM_STATE = 64
HALF_GROUPS = 16
HALF_IN = HALF_GROUPS * SSM_CH
HALF_ST = HALF_GROUPS * SSM_STATE
GM_WIDTH = 512
GM_HEADS = 4
GM_HEAD_DIM = 128
GM_CHUNK = 128
IN_COLS = SSM_WIDTH + 2 * GM_WIDTH
EPS = 1e-6
SUBLANES = 8
LANES = 128

ADAM_LR = 0.001
ADAM_B1 = 0.9
ADAM_B2 = 0.999
ADAM_EPS = 1e-08
ADAM_WD = 0.01
ADAM_STEP = 10

VMEM_LIMIT = 46 * 1024 * 1024


def _cp(*sem):
    return pltpu.CompilerParams(dimension_semantics=sem, vmem_limit_bytes=VMEM_LIMIT)


def _rms_fwd(x, g):
    r = lax.rsqrt(jnp.mean(x * x, axis=-1, keepdims=True) + EPS)
    xh = x * r
    return xh * g, xh, r


def _rms_bwd(dy, xh, r, g):
    dxh = dy * g
    dx = r * (dxh - xh * jnp.mean(dxh * xh, axis=-1, keepdims=True))
    return dx, dy * xh


def _rows8(a):
    m, n = a.shape
    return a.reshape(m // SUBLANES, SUBLANES, n).sum(axis=0)


_GELU_K = math.sqrt(2.0 / math.pi)
_GELU_C = 0.044715


def _gelu(x):
    th = jnp.tanh(_GELU_K * (x + _GELU_C * x * x * x))
    return 0.5 * x * (1.0 + th), th


def _gelu_grad(x, th):
    return 0.5 * (1.0 + th) + 0.5 * x * (1.0 - th * th) * (_GELU_K * (1.0 + 3.0 * _GELU_C * x * x))


def _dot(a, b):
    return jnp.dot(a, b, preferred_element_type=F32)


def _dot_nt(a, b):
    return lax.dot_general(a, b, (((1,), (1,)), ((), ())), preferred_element_type=F32)


def _dot_tn(a, b):
    return lax.dot_general(a, b, (((0,), (0,)), ((), ())), preferred_element_type=F32)


def _ffn_fwd(x, gain, w_in_ag, w_out_ag, tm, name):
    n = x.shape[0]

    def body(x_ref, g_ref, wg_ref, wu_ref, wo_ref, o_ref, xn_ref, gu_ref):
        j = pl.program_id(1)

        @pl.when(j == 0)
        def _():
            xv = x_ref[...]
            y, _, _ = _rms_fwd(xv, g_ref[...])
            xn_ref[...] = y.astype(BF16)
            o_ref[...] = xv

        xn = xn_ref[...]
        wo = wo_ref[...].reshape(FF_SHARD, D_MODEL)
        out = None
        for lo, hi in FF_PIECES:
            gg = _dot(xn, wg_ref[:, lo:hi])
            uu = _dot(xn, wu_ref[:, lo:hi])
            gu_ref[0, :, lo:hi] = gg.astype(BF16)
            gu_ref[1, :, lo:hi] = uu.astype(BF16)
            act = (gg * jax.nn.sigmoid(gg) * uu).astype(BF16)
            part = _dot(act, wo[lo:hi, :])
            out = part if out is None else out + part
        o_ref[...] += 0.5 * out

    return pl.pallas_call(
        body, name=name, grid=(n // tm, FF_CHUNKS),
        in_specs=[
            pl.BlockSpec((tm, D_MODEL), lambda i, j: (i, 0)),
            pl.BlockSpec((1, D_MODEL), lambda i, j: (0, 0)),
            pl.BlockSpec((None, D_MODEL, FF_SHARD), lambda i, j: (j, 0, 0)),
            pl.BlockSpec((None, D_MODEL, FF_SHARD), lambda i, j: (j + FF_CHUNKS, 0, 0)),
            pl.BlockSpec((2, FF_SHARD // 2, D_MODEL), lambda i, j: (j, 0, 0)),
        ],
        out_specs=[
            pl.BlockSpec((tm, D_MODEL), lambda i, j: (i, 0)),
            pl.BlockSpec((tm, D_MODEL), lambda i, j: (i, 0)),
            pl.BlockSpec((None, 2, tm, FF_SHARD), lambda i, j: (j, 0, i, 0)),
        ],
        out_shape=[
            jax.ShapeDtypeStruct((n, D_MODEL), F32),
            jax.ShapeDtypeStruct((n, D_MODEL), BF16),
            jax.ShapeDtypeStruct((FF_CHUNKS, 2, n, FF_SHARD), BF16),
        ],
        compiler_params=_cp("parallel", "arbitrary"),
    )(x, gain, w_in_ag, w_in_ag, w_out_ag)


def _ffn_bwd(x, gain, dy, gu, w_in_ag, w_out_ag, tm, name):
    n = x.shape[0]
    steps = FF_CHUNKS + 1
    first = lambda j: jnp.minimum(j, FF_CHUNKS - 1)
    second = lambda j: jnp.maximum(j - 1, 0)

    def body(x_ref, g_ref, dy_ref, gu_ref, wg_ref, wu_ref, wo_ref, dx_ref, dgu_ref, act_ref, dgain_ref, dyb_ref,
             held_ref):
        i, j = pl.program_id(0), pl.program_id(1)

        @pl.when(jnp.logical_and(i == 0, j == 0))
        def _():
            dgain_ref[...] = jnp.zeros_like(dgain_ref)
            held_ref[...] = jnp.zeros_like(held_ref)

        @pl.when(j == 0)
        def _():
            dyb_ref[...] = (0.5 * dy_ref[...]).astype(BF16)
            dx_ref[...] = jnp.zeros_like(dx_ref)

        held = held_ref[1 - j % 2]
        part = _dot_nt(held[0], wg_ref[...]) + _dot_nt(held[1], wu_ref[...])
        dx_ref[...] += jnp.where(j > 0, part, 0.0)

        dyb = dyb_ref[...]
        wo = wo_ref[...].reshape(FF_SHARD, D_MODEL)
        slot = j % 2
        for lo, hi in FF_PIECES:
            gg = gu_ref[0, :, lo:hi].astype(F32)
            uu = gu_ref[1, :, lo:hi].astype(F32)
            dact = _dot_nt(dyb, wo[lo:hi, :])
            sig = jax.nn.sigmoid(gg)
            silu = gg * sig
            act_ref[:, lo:hi] = (silu * uu).astype(BF16)
            du = (dact * silu).astype(BF16)
            dg = (dact * uu * (sig * (1.0 + gg * (1.0 - sig)))).astype(BF16)
            dgu_ref[0, :, lo:hi] = dg
            dgu_ref[1, :, lo:hi] = du
            held_ref[slot, 0, :, lo:hi] = dg
            held_ref[slot, 1, :, lo:hi] = du

        @pl.when(j == steps - 1)
        def _():
            g = g_ref[...]
            _, xh, r = _rms_fwd(x_ref[...], g)
            dx, dgr = _rms_bwd(dx_ref[...], xh, r, g)
            dx_ref[...] = dy_ref[...] + dx
            dgain_ref[...] += _rows8(dgr)

    return pl.pallas_call(
        body, name=name, grid=(n // tm, steps),
        in_specs=[
            pl.BlockSpec((tm, D_MODEL), lambda i, j: (i, 0)),
            pl.BlockSpec((1, D_MODEL), lambda i, j: (0, 0)),
            pl.BlockSpec((tm, D_MODEL), lambda i, j: (i, 0)),
            pl.BlockSpec((None, 2, tm, FF_SHARD), lambda i, j: (first(j), 0, i, 0)),
            pl.BlockSpec((None, D_MODEL, FF_SHARD), lambda i, j: (second(j), 0, 0)),
            pl.BlockSpec((None, D_MODEL, FF_SHARD), lambda i, j: (second(j) + FF_CHUNKS, 0, 0)),
            pl.BlockSpec((2, FF_SHARD // 2, D_MODEL), lambda i, j: (first(j), 0, 0)),
        ],
        out_specs=[
            pl.BlockSpec((tm, D_MODEL), lambda i, j: (i, 0)),
            pl.BlockSpec((None, 2, tm, FF_SHARD), lambda i, j: (first(j), 0, i, 0)),
            pl.BlockSpec((None, tm, FF_SHARD), lambda i, j: (first(j), i, 0)),
            pl.BlockSpec((SUBLANES, D_MODEL), lambda i, j: (0, 0)),
        ],
        out_shape=[
            jax.ShapeDtypeStruct((n, D_MODEL), F32),
            jax.ShapeDtypeStruct((FF_CHUNKS, 2, n, FF_SHARD), BF16),
            jax.ShapeDtypeStruct((FF_CHUNKS, n, FF_SHARD), BF16),
            jax.ShapeDtypeStruct((SUBLANES, D_MODEL), F32),
        ],
        scratch_shapes=[pltpu.VMEM((tm, D_MODEL), BF16), pltpu.VMEM((2, 2, tm, FF_SHARD), BF16)],
        compiler_params=_cp("arbitrary", "arbitrary"),
    )(x, gain, dy, gu, w_in_ag, w_in_ag, w_out_ag)


def _ffn_dw_in(xn, dgu, tk, name, after=None):
    n = xn.shape[0]
    nk = n // tk
    deps = [] if after is None else [after]

    def body(a_ref, b_ref, *rest):
        o_ref, acc_ref = rest[-2:]
        k = pl.program_id(2)

        @pl.when(k == 0)
        def _():
            acc_ref[...] = jnp.zeros_like(acc_ref)

        acc_ref[...] += _dot_tn(b_ref[...], a_ref[...])

        @pl.when(k == nk - 1)
        def _():
            o_ref[...] = acc_ref[...].astype(BF16)

    return pl.pallas_call(
        body, name=name, grid=(FF_CHUNKS, 2, nk),
        in_specs=[
            pl.BlockSpec((tk, D_MODEL), lambda j, p, k: (k, 0)),
            pl.BlockSpec((None, None, tk, FF_SHARD), lambda j, p, k: (j, p, k, 0)),
        ] + [pl.BlockSpec(memory_space=pl.ANY)] * len(deps),
        out_specs=pl.BlockSpec((None, FF_SHARD, D_MODEL), lambda j, p, k: (FF_CHUNKS * p + j, 0, 0)),
        out_shape=jax.ShapeDtypeStruct((N_DEV, FF_SHARD, D_MODEL), BF16),
        scratch_shapes=[pltpu.VMEM((FF_SHARD, D_MODEL), F32)],
        compiler_params=_cp("parallel", "parallel", "arbitrary"),
    )(xn, dgu, *deps)


def _ffn_dw_out(act, dy, tk, name, after=None):
    n = act.shape[1]
    nk = n // tk
    deps = [] if after is None else [after]

    def body(a_ref, b_ref, *rest):
        o_ref, acc_ref = rest[-2:]
        k = pl.program_id(1)

        @pl.when(k == 0)
        def _():
            acc_ref[...] = jnp.zeros_like(acc_ref)

        acc_ref[...] += _dot_tn(a_ref[...], b_ref[...].astype(BF16))

        @pl.when(k == nk - 1)
        def _():
            o_ref[...] = (0.5 * acc_ref[...]).astype(BF16)

    return pl.pallas_call(
        body, name=name, grid=(FF_CHUNKS, nk),
        in_specs=[
            pl.BlockSpec((None, tk, FF_SHARD), lambda j, k: (j, k, 0)),
            pl.BlockSpec((tk, D_MODEL), lambda j, k: (k, 0)),
        ] + [pl.BlockSpec(memory_space=pl.ANY)] * len(deps),
        out_specs=pl.BlockSpec((None, FF_SHARD, D_MODEL), lambda j, k: (j, 0, 0)),
        out_shape=jax.ShapeDtypeStruct((FF_CHUNKS, FF_SHARD, D_MODEL), BF16),
        scratch_shapes=[pltpu.VMEM((FF_SHARD, D_MODEL), F32)],
        compiler_params=_cp("parallel", "arbitrary"),
    )(act, dy, *deps)


def _mix_in_fwd(x, gain, w, tm, name):
    n = x.shape[0]

    def body(x_ref, g_ref, w_ref, z_ref):
        y, _, _ = _rms_fwd(x_ref[...], g_ref[...])
        z_ref[...] = _dot(y.astype(BF16), w_ref[...])

    return pl.pallas_call(
        body, name=name, grid=(n // tm,),
        in_specs=[
            pl.BlockSpec((tm, D_MODEL), lambda i: (i, 0)),
            pl.BlockSpec((1, D_MODEL), lambda i: (0, 0)),
            pl.BlockSpec((D_MODEL, IN_COLS), lambda i: (0, 0)),
        ],
        out_specs=pl.BlockSpec((tm, IN_COLS), lambda i: (i, 0)),
        out_shape=jax.ShapeDtypeStruct((n, IN_COLS), F32),
        compiler_params=_cp("parallel"),
    )(x, gain, w)


def _mix_in_bwd(x, gain, du_ssm, du_gm, dv_gm, d_res, w, tm, name):
    n = x.shape[0]

    def body(x_ref, g_ref, d0_ref, d1_ref, d2_ref, dres_ref, w_ref, dx_ref, dw_ref, dgain_ref):
        i = pl.program_id(0)

        @pl.when(i == 0)
        def _():
            dw_ref[...] = jnp.zeros_like(dw_ref)
            dgain_ref[...] = jnp.zeros_like(dgain_ref)

        g = g_ref[...]
        y, xh, r = _rms_fwd(x_ref[...], g)
        xn = y.astype(BF16)
        dxn = jnp.zeros((tm, D_MODEL), F32)
        for k, d_ref in enumerate((d0_ref, d1_ref, d2_ref)):
            dz = d_ref[...].astype(BF16)
            cols = slice(k * SSM_WIDTH, (k + 1) * SSM_WIDTH)
            dxn += _dot_nt(dz, w_ref[:, cols])
            dw_ref[cols, :] += _dot_tn(dz, xn)
        dx, dgr = _rms_bwd(dxn, xh, r, g)
        dx_ref[...] = dres_ref[...] + dx
        dgain_ref[...] += _rows8(dgr)

    row = lambda i: (i, 0)
    fixed = lambda i: (0, 0)
    return pl.pallas_call(
        body, name=name, grid=(n // tm,),
        in_specs=[
            pl.BlockSpec((tm, D_MODEL), row),
            pl.BlockSpec((1, D_MODEL), fixed),
            pl.BlockSpec((tm, SSM_WIDTH), row),
            pl.BlockSpec((tm, GM_WIDTH), row),
            pl.BlockSpec((tm, GM_WIDTH), row),
            pl.BlockSpec((tm, D_MODEL), row),
            pl.BlockSpec((D_MODEL, IN_COLS), fixed),
        ],
        out_specs=[
            pl.BlockSpec((tm, D_MODEL), row),
            pl.BlockSpec((IN_COLS, D_MODEL), fixed),
            pl.BlockSpec((SUBLANES, D_MODEL), fixed),
        ],
        out_shape=[
            jax.ShapeDtypeStruct((n, D_MODEL), F32),
            jax.ShapeDtypeStruct((IN_COLS, D_MODEL), F32),
            jax.ShapeDtypeStruct((SUBLANES, D_MODEL), F32),
        ],
        compiler_params=_cp("arbitrary"),
    )(x, gain, du_ssm, du_gm, dv_gm, d_res, w)


def _mix_out_fwd(y_ssm, y_gm, g_ssm, g_gm, w, x, tm, name):
    n = x.shape[0]

    def body(ys_ref, yg_ref, gs_ref, gg_ref, w_ref, x_ref, o_ref):
        a, _, _ = _rms_fwd(ys_ref[...], gs_ref[...])
        b, _, _ = _rms_fwd(yg_ref[...], gg_ref[...])
        o_ref[...] = (x_ref[...] + _dot(a.astype(BF16), w_ref[0:SSM_WIDTH, :])
                      + _dot(b.astype(BF16), w_ref[SSM_WIDTH:D_MODEL, :]))

    row = lambda i: (i, 0)
    fixed = lambda i: (0, 0)
    return pl.pallas_call(
        body, name=name, grid=(n // tm,),
        in_specs=[
            pl.BlockSpec((tm, SSM_WIDTH), row), pl.BlockSpec((tm, GM_WIDTH), row),
            pl.BlockSpec((1, SSM_WIDTH), fixed), pl.BlockSpec((1, GM_WIDTH), fixed),
            pl.BlockSpec((D_MODEL, D_MODEL), fixed), pl.BlockSpec((tm, D_MODEL), row),
        ],
        out_specs=pl.BlockSpec((tm, D_MODEL), row),
        out_shape=jax.ShapeDtypeStruct((n, D_MODEL), F32),
        compiler_params=_cp("parallel"),
    )(y_ssm, y_gm, g_ssm, g_gm, w, x)


def _mix_out_bwd(y_ssm, y_gm, g_ssm, g_gm, w, dx, tm, name):
    n = dx.shape[0]

    def body(ys_ref, yg_ref, gs_ref, gg_ref, w_ref, dx_ref, dys_ref, dyg_ref, dw_ref, dgs_ref, dgg_ref):
        i = pl.program_id(0)

        @pl.when(i == 0)
        def _():
            dw_ref[...] = jnp.zeros_like(dw_ref)
            dgs_ref[...] = jnp.zeros_like(dgs_ref)
            dgg_ref[...] = jnp.zeros_like(dgg_ref)

        dxb = dx_ref[...].astype(BF16)
        parts = ((ys_ref, gs_ref, dys_ref, dgs_ref, 0), (yg_ref, gg_ref, dyg_ref, dgg_ref, SSM_WIDTH))
        for y_ref, g_ref, dy_ref, dg_ref, off in parts:
            g = g_ref[...]
            yn, xh, r = _rms_fwd(y_ref[...], g)
            rows = slice(off, off + SSM_WIDTH)
            dyn = _dot_nt(dxb, w_ref[rows, :])
            dw_ref[rows, :] += _dot_tn(yn.astype(BF16), dxb)
            dy, dgr = _rms_bwd(dyn, xh, r, g)
            dy_ref[...] = dy
            dg_ref[...] += _rows8(dgr)

    row = lambda i: (i, 0)
    fixed = lambda i: (0, 0)
    return pl.pallas_call(
        body, name=name, grid=(n // tm,),
        in_specs=[
            pl.BlockSpec((tm, SSM_WIDTH), row), pl.BlockSpec((tm, GM_WIDTH), row),
            pl.BlockSpec((1, SSM_WIDTH), fixed), pl.BlockSpec((1, GM_WIDTH), fixed),
            pl.BlockSpec((D_MODEL, D_MODEL), fixed), pl.BlockSpec((tm, D_MODEL), row),
        ],
        out_specs=[
            pl.BlockSpec((tm, SSM_WIDTH), row), pl.BlockSpec((tm, GM_WIDTH), row),
            pl.BlockSpec((D_MODEL, D_MODEL), fixed),
            pl.BlockSpec((SUBLANES, SSM_WIDTH), fixed), pl.BlockSpec((SUBLANES, GM_WIDTH), fixed),
        ],
        out_shape=[
            jax.ShapeDtypeStruct((n, SSM_WIDTH), F32), jax.ShapeDtypeStruct((n, GM_WIDTH), F32),
            jax.ShapeDtypeStruct((D_MODEL, D_MODEL), F32),
            jax.ShapeDtypeStruct((SUBLANES, SSM_WIDTH), F32), jax.ShapeDtypeStruct((SUBLANES, GM_WIDTH), F32),
        ],
        compiler_params=_cp("arbitrary"),
    )(y_ssm, y_gm, g_ssm, g_gm, w, dx)


SCAN_W = 512
SCAN_PIECES = HALF_ST // SCAN_W


def _scan_tiles(src_ref, dst_ref, dst_off, c_ref, half, carry_ref, n_tiles, reverse, extra=None):
    shifts = (1, 2, 4)
    carry_row = 0 if reverse else SUBLANES - 1

    def cols(piece, im):
        lo = im * HALF_ST + piece * SCAN_W
        return slice(lo, lo + SCAN_W)

    def step(t, state):
        carries, accs = state
        k = (n_tiles - 1 - t) if reverse else t
        rows = slice(k * SUBLANES, (k + 1) * SUBLANES)
        new_carries, new_accs = [], []
        for piece in range(SCAN_PIECES):
            cr, ci = carries[piece]
            xr0 = src_ref[rows, cols(piece, 0)]
            xi0 = src_ref[rows, cols(piece, 1)]
            xr, xi = xr0, xi0
            for si, s in enumerate(shifts):
                ar = c_ref[half, si, :, cols(piece, 0)]
                ai = c_ref[half, si, :, cols(piece, 1)]
                sh = (SUBLANES - s) if reverse else s
                sr = pltpu.roll(xr, sh, 0)
                sm = pltpu.roll(xi, sh, 0)
                xr, xi = xr + (ar * sr - ai * sm), xi + (ar * sm + ai * sr)
            pr = c_ref[half, 3, :, cols(piece, 0)]
            pi = c_ref[half, 3, :, cols(piece, 1)]
            hr = xr + (pr * cr - pi * ci)
            hi = xi + (pr * ci + pi * cr)
            dst_ref[rows, pl.ds(dst_off + piece * SCAN_W, SCAN_W)] = hr
            dst_ref[rows, pl.ds(dst_off + HALF_ST + piece * SCAN_W, SCAN_W)] = hi
            new_carries.append((jnp.broadcast_to(hr[carry_row:carry_row + 1, :], (SUBLANES, SCAN_W)),
                                jnp.broadcast_to(hi[carry_row:carry_row + 1, :], (SUBLANES, SCAN_W))))
            if extra is not None:
                new_accs.append(extra(rows, piece, (xr0, xi0), (hr, hi), accs[piece]))
        return tuple(new_carries), tuple(new_accs)

    base = half * 2 * HALF_ST
    carries0 = tuple((carry_ref[:, pl.ds(base + p * SCAN_W, SCAN_W)],
                      carry_ref[:, pl.ds(base + HALF_ST + p * SCAN_W, SCAN_W)]) for p in range(SCAN_PIECES))
    zero = jnp.zeros((SUBLANES, SCAN_W), F32)
    accs0 = tuple((zero, zero) for _ in range(SCAN_PIECES)) if extra is not None else ()
    state = (carries0, accs0)
    for t in range(n_tiles):
        state = step(t, state)
    carries, accs = state
    for p in range(SCAN_PIECES):
        carry_ref[:, pl.ds(base + p * SCAN_W, SCAN_W)] = carries[p][0]
        carry_ref[:, pl.ds(base + HALF_ST + p * SCAN_W, SCAN_W)] = carries[p][1]
    return accs


def _ssm_tail(hb, u, c_ref, glu_ref, glub_ref, dskip_ref):
    ypre = u * dskip_ref[...]
    parts = []
    for half in range(2):
        parts.append(_dot(hb[half], c_ref[half]))
    ypre = ypre + jnp.concatenate(parts, axis=1)
    yg, th = _gelu(ypre)
    zz = _dot(yg.astype(BF16), glu_ref[...]) + glub_ref[...]
    z1, z2 = zz[:, :SSM_WIDTH], zz[:, SSM_WIDTH:]
    sg = jax.nn.sigmoid(z2)
    return ypre, th, yg, z1, sg


def _ssm_fwd(z, bblk, cblk, glu, glub, dskip, fwdc, batch, t_chunk, name):
    n = z.shape[0]
    nk = n // batch // t_chunk
    n_tiles = t_chunk // SUBLANES

    def body(u_ref, b_ref, c_ref, glu_ref, glub_ref, dskip_ref, k_ref, y_ref, h_ref, bu_ref, carry_ref):
        @pl.when(pl.program_id(1) == 0)
        def _():
            carry_ref[...] = jnp.zeros_like(carry_ref)

        u = u_ref[...]
        ub = u.astype(BF16)
        for half in range(2):
            bu_ref[half] = _dot(ub[:, half * HALF_IN:(half + 1) * HALF_IN], b_ref[half])
            _scan_tiles(bu_ref.at[half], h_ref, half * 2 * HALF_ST, k_ref, half, carry_ref, n_tiles, False)
        hb = [h_ref[:, half * 2 * HALF_ST:(half + 1) * 2 * HALF_ST].astype(BF16) for half in range(2)]
        _, _, _, z1, sg = _ssm_tail(hb, u, c_ref, glu_ref, glub_ref, dskip_ref)
        y_ref[...] = z1 * sg

    fixed2 = lambda b, k: (0, 0)
    fixed3 = lambda b, k: (0, 0, 0)
    row = lambda b, k: (b * nk + k, 0)
    return pl.pallas_call(
        body, name=name, grid=(batch, nk),
        in_specs=[
            pl.BlockSpec((t_chunk, SSM_WIDTH), row),
            pl.BlockSpec((2, HALF_IN, 2 * HALF_ST), fixed3),
            pl.BlockSpec((2, 2 * HALF_ST, HALF_IN), fixed3),
            pl.BlockSpec((SSM_WIDTH, 2 * SSM_WIDTH), fixed2),
            pl.BlockSpec((1, 2 * SSM_WIDTH), fixed2),
            pl.BlockSpec((1, SSM_WIDTH), fixed2),
            pl.BlockSpec((2, 4, SUBLANES, 2 * HALF_ST), lambda b, k: (0, 0, 0, 0)),
        ],
        out_specs=[pl.BlockSpec((t_chunk, SSM_WIDTH), row), pl.BlockSpec((t_chunk, 4 * HALF_ST), row)],
        out_shape=[jax.ShapeDtypeStruct((n, SSM_WIDTH), F32), jax.ShapeDtypeStruct((n, 4 * HALF_ST), F32)],
        scratch_shapes=[pltpu.VMEM((2, t_chunk, 2 * HALF_ST), F32), pltpu.VMEM((SUBLANES, 4 * HALF_ST), F32)],
        compiler_params=_cp("parallel", "arbitrary"),
    )(z, bblk, cblk, glu, glub, dskip, fwdc)


def _ssm_bwd(z, h, dy, bblk, cblk, glu, glub, dskip, revc, batch, t_chunk, name):
    n = z.shape[0]
    nk = n // batch // t_chunk
    n_tiles = t_chunk // SUBLANES

    def body(u_ref, h_ref, dy_ref, b_ref, c_ref, glu_ref, glub_ref, dskip_ref, k_ref,
             du_ref, dglu_ref, dglub_ref, ddskip_ref, dct_ref, db_ref, q_ref, g_ref, carry_ref):
        first = jnp.logical_and(pl.program_id(0) == 0, pl.program_id(1) == 0)

        @pl.when(first)
        def _():
            for r in (dglu_ref, dglub_ref, ddskip_ref, dct_ref, db_ref, q_ref):
                r[...] = jnp.zeros_like(r)

        @pl.when(pl.program_id(1) == 0)
        def _():
            carry_ref[...] = jnp.zeros_like(carry_ref)

        u = u_ref[...]
        ub = u.astype(BF16)
        hb = [h_ref[:, half * 2 * HALF_ST:(half + 1) * 2 * HALF_ST].astype(BF16) for half in range(2)]
        ypre, th, yg, z1, sg = _ssm_tail(hb, u, c_ref, glu_ref, glub_ref, dskip_ref)
        dout = dy_ref[...]
        dz = jnp.concatenate([dout * sg, dout * z1 * sg * (1.0 - sg)], axis=1)
        dzb = dz.astype(BF16)
        dglu_ref[...] += _dot_tn(yg.astype(BF16), dzb)
        dglub_ref[...] += _rows8(dz)
        dypre = _dot_nt(dzb, glu_ref[...]) * _gelu_grad(ypre, th)
        ddskip_ref[...] += _rows8(dypre * u)
        dypb = dypre.astype(BF16)
        du_parts = []
        for half in range(2):
            dyp_h = dypb[:, half * HALF_IN:(half + 1) * HALF_IN]
            dct_ref[half] += _dot_tn(dyp_h, hb[half])
            g_ref[half] = _dot_nt(dyp_h, c_ref[half])

            def extra(rows, piece, x_in, g_out, acc, half=half):
                er, ei = g_out[0] - x_in[0], g_out[1] - x_in[1]
                base = half * 2 * HALF_ST + piece * SCAN_W
                hr = h_ref[rows, pl.ds(base, SCAN_W)]
                hi = h_ref[rows, pl.ds(base + HALF_ST, SCAN_W)]
                return acc[0] + (er * hr + ei * hi), acc[1] + (er * hi - ei * hr)

            accs = _scan_tiles(g_ref.at[half], g_ref.at[half], 0, k_ref, half, carry_ref, n_tiles, True, extra)
            for piece in range(SCAN_PIECES):
                base = half * 2 * HALF_ST + piece * SCAN_W
                q_ref[:, pl.ds(base, SCAN_W)] += accs[piece][0]
                q_ref[:, pl.ds(base + HALF_ST, SCAN_W)] += accs[piece][1]
            gb = g_ref[half].astype(BF16)
            db_ref[half] += _dot_tn(ub[:, half * HALF_IN:(half + 1) * HALF_IN], gb)
            du_parts.append(_dot_nt(gb, b_ref[half]))
        du_ref[...] = dypre * dskip_ref[...] + jnp.concatenate(du_parts, axis=1)

    fixed2 = lambda b, k: (0, 0)
    fixed3 = lambda b, k: (0, 0, 0)
    row = lambda b, k: (b * nk + (nk - 1 - k), 0)
    return pl.pallas_call(
        body, name=name, grid=(batch, nk),
        in_specs=[
            pl.BlockSpec((t_chunk, SSM_WIDTH), row),
            pl.BlockSpec((t_chunk, 4 * HALF_ST), row),
            pl.BlockSpec((t_chunk, SSM_WIDTH), row),
            pl.BlockSpec((2, HALF_IN, 2 * HALF_ST), fixed3),
            pl.BlockSpec((2, 2 * HALF_ST, HALF_IN), fixed3),
            pl.BlockSpec((SSM_WIDTH, 2 * SSM_WIDTH), fixed2),
            pl.BlockSpec((1, 2 * SSM_WIDTH), fixed2),
            pl.BlockSpec((1, SSM_WIDTH), fixed2),
            pl.BlockSpec((2, 4, SUBLANES, 2 * HALF_ST), lambda b, k: (0, 0, 0, 0)),
        ],
        out_specs=[
            pl.BlockSpec((t_chunk, SSM_WIDTH), row),
            pl.BlockSpec((SSM_WIDTH, 2 * SSM_WIDTH), fixed2),
            pl.BlockSpec((SUBLANES, 2 * SSM_WIDTH), fixed2),
            pl.BlockSpec((SUBLANES, SSM_WIDTH), fixed2),
            pl.BlockSpec((2, HALF_IN, 2 * HALF_ST), fixed3),
            pl.BlockSpec((2, HALF_IN, 2 * HALF_ST), fixed3),
            pl.BlockSpec((SUBLANES, 4 * HALF_ST), fixed2),
        ],
        out_shape=[
            jax.ShapeDtypeStruct((n, SSM_WIDTH), F32),
            jax.ShapeDtypeStruct((SSM_WIDTH, 2 * SSM_WIDTH), F32),
            jax.ShapeDtypeStruct((SUBLANES, 2 * SSM_WIDTH), F32),
            jax.ShapeDtypeStruct((SUBLANES, SSM_WIDTH), F32),
            jax.ShapeDtypeStruct((2, HALF_IN, 2 * HALF_ST), F32),
            jax.ShapeDtypeStruct((2, HALF_IN, 2 * HALF_ST), F32),
            jax.ShapeDtypeStruct((SUBLANES, 4 * HALF_ST), F32),
        ],
        scratch_shapes=[pltpu.VMEM((2, t_chunk, 2 * HALF_ST), F32), pltpu.VMEM((SUBLANES, 4 * HALF_ST), F32)],
        compiler_params=_cp("arbitrary", "arbitrary"),
    )(z, h, dy, bblk, cblk, glu, glub, dskip, revc)


def _gm_chunk_fwd(u, v, gain_ref, w_ref, bias_ref):
    ug, thu = _gelu(u)
    vg, thv = _gelu(v)
    rs, vns, ss = [], [], []
    for hh in range(GM_HEADS):
        cs = slice(hh * GM_HEAD_DIM, (hh + 1) * GM_HEAD_DIM)
        vn, _, r = _rms_fwd(vg[:, cs], gain_ref[:, cs])
        s = _dot(w_ref[hh], vn.astype(BF16)) + bias_ref[:, cs]
        rs.append(r)
        vns.append(vn)
        ss.append(s)
    return ug, thu, thv, vg, rs, vns, ss


def _gm_fwd(z, gain, w_tril, bias, rows, name):
    n = z.shape[0]
    chunks = rows // GM_CHUNK

    def body(u_ref, v_ref, gain_ref, w_ref, bias_ref, y_ref):
        for c in range(chunks):
            rs_ = slice(c * GM_CHUNK, (c + 1) * GM_CHUNK)
            ug, _, _, _, _, _, ss = _gm_chunk_fwd(u_ref[rs_, :], v_ref[rs_, :], gain_ref, w_ref, bias_ref)
            y_ref[rs_, :] = ug * jnp.concatenate(ss, axis=1)

    return pl.pallas_call(
        body, name=name, grid=(n // rows,),
        in_specs=[
            pl.BlockSpec((rows, GM_WIDTH), lambda i: (i, 1)),
            pl.BlockSpec((rows, GM_WIDTH), lambda i: (i, 2)),
            pl.BlockSpec((1, GM_WIDTH), lambda i: (0, 0)),
            pl.BlockSpec((GM_HEADS, GM_CHUNK, GM_CHUNK), lambda i: (0, 0, 0)),
            pl.BlockSpec((GM_CHUNK, GM_WIDTH), lambda i: (0, 0)),
        ],
        out_specs=pl.BlockSpec((rows, GM_WIDTH), lambda i: (i, 0)),
        out_shape=jax.ShapeDtypeStruct((n, GM_WIDTH), F32),
        compiler_params=_cp("parallel"),
    )(z, z, gain, w_tril, bias)


def _gm_bwd(z, dy, gain, w_tril, bias, rows, name):
    n = z.shape[0]
    chunks = rows // GM_CHUNK

    def body(u_ref, v_ref, dy_ref, gain_ref, w_ref, bias_ref, du_ref, dv_ref, dw_ref, dbias_ref, dgain_ref):
        @pl.when(pl.program_id(0) == 0)
        def _():
            dw_ref[...] = jnp.zeros_like(dw_ref)
            dbias_ref[...] = jnp.zeros_like(dbias_ref)
            dgain_ref[...] = jnp.zeros_like(dgain_ref)

        for c in range(chunks):
            rs_ = slice(c * GM_CHUNK, (c + 1) * GM_CHUNK)
            u, v = u_ref[rs_, :], v_ref[rs_, :]
            ug, thu, thv, vg, rs, vns, ss = _gm_chunk_fwd(u, v, gain_ref, w_ref, bias_ref)
            dout = dy_ref[rs_, :]
            ds = dout * ug
            du_ref[rs_, :] = dout * jnp.concatenate(ss, axis=1) * _gelu_grad(u, thu)
            dbias_ref[...] += ds
            dvg_parts, dgain_parts = [], []
            for hh in range(GM_HEADS):
                cs = slice(hh * GM_HEAD_DIM, (hh + 1) * GM_HEAD_DIM)
                dsb = ds[:, cs].astype(BF16)
                dvn = _dot_tn(w_ref[hh], dsb)
                dw_ref[hh] += _dot_nt(dsb, vns[hh].astype(BF16))
                g = gain_ref[:, cs]
                xh = vg[:, cs] * rs[hh]
                dvg, dgr = _rms_bwd(dvn, xh, rs[hh], g)
                dvg_parts.append(dvg)
                dgain_parts.append(dgr)
            dv_ref[rs_, :] = jnp.concatenate(dvg_parts, axis=1) * _gelu_grad(v, thv)
            dgain_ref[...] += _rows8(jnp.concatenate(dgain_parts, axis=1))

    row = lambda i: (i, 0)
    return pl.pallas_call(
        body, name=name, grid=(n // rows,),
        in_specs=[
            pl.BlockSpec((rows, GM_WIDTH), lambda i: (i, 1)),
            pl.BlockSpec((rows, GM_WIDTH), lambda i: (i, 2)),
            pl.BlockSpec((rows, GM_WIDTH), row),
            pl.BlockSpec((1, GM_WIDTH), lambda i: (0, 0)),
            pl.BlockSpec((GM_HEADS, GM_CHUNK, GM_CHUNK), lambda i: (0, 0, 0)),
            pl.BlockSpec((GM_CHUNK, GM_WIDTH), lambda i: (0, 0)),
        ],
        out_specs=[
            pl.BlockSpec((rows, GM_WIDTH), row), pl.BlockSpec((rows, GM_WIDTH), row),
            pl.BlockSpec((GM_HEADS, GM_CHUNK, GM_CHUNK), lambda i: (0, 0, 0)),
            pl.BlockSpec((GM_CHUNK, GM_WIDTH), lambda i: (0, 0)),
            pl.BlockSpec((SUBLANES, GM_WIDTH), lambda i: (0, 0)),
        ],
        out_shape=[
            jax.ShapeDtypeStruct((n, GM_WIDTH), F32), jax.ShapeDtypeStruct((n, GM_WIDTH), F32),
            jax.ShapeDtypeStruct((GM_HEADS, GM_CHUNK, GM_CHUNK), F32),
            jax.ShapeDtypeStruct((GM_CHUNK, GM_WIDTH), F32),
            jax.ShapeDtypeStruct((SUBLANES, GM_WIDTH), F32),
        ],
        compiler_params=_cp("arbitrary"),
    )(z, z, dy, gain, w_tril, bias)


def _loss_head(x, gain, target, tm, name):
    n = x.shape[0]

    def body(x_ref, g_ref, t_ref, dx_ref, sq_ref, dgain_ref):
        @pl.when(pl.program_id(0) == 0)
        def _():
            sq_ref[...] = jnp.zeros_like(sq_ref)
            dgain_ref[...] = jnp.zeros_like(dgain_ref)

        g = g_ref[...]
        y, xh, r = _rms_fwd(x_ref[...], g)
        err = y - t_ref[...]
        sq_ref[...] += _rows8(err * err)
        dx, dgr = _rms_bwd(err * (1.0 / D_MODEL), xh, r, g)
        dx_ref[...] = dx
        dgain_ref[...] += _rows8(dgr)

    row = lambda i: (i, 0)
    fixed = lambda i: (0, 0)
    return pl.pallas_call(
        body, name=name, grid=(n // tm,),
        in_specs=[pl.BlockSpec((tm, D_MODEL), row), pl.BlockSpec((1, D_MODEL), fixed), pl.BlockSpec((tm, D_MODEL), row)],
        out_specs=[pl.BlockSpec((tm, D_MODEL), row), pl.BlockSpec((SUBLANES, D_MODEL), fixed),
                   pl.BlockSpec((SUBLANES, D_MODEL), fixed)],
        out_shape=[jax.ShapeDtypeStruct((n, D_MODEL), F32), jax.ShapeDtypeStruct((SUBLANES, D_MODEL), F32),
                   jax.ShapeDtypeStruct((SUBLANES, D_MODEL), F32)],
        compiler_params=_cp("arbitrary"),
    )(x, gain, target)


def _adam_math(w, g, m, v):
    m2 = ADAM_B1 * m + (1.0 - ADAM_B1) * g
    v2 = ADAM_B2 * v + (1.0 - ADAM_B2) * (g * g)
    m_hat = m2 / (1.0 - ADAM_B1 ** ADAM_STEP)
    v_hat = v2 / (1.0 - ADAM_B2 ** ADAM_STEP)
    delta = -ADAM_LR * (m_hat / (jnp.sqrt(v_hat) + ADAM_EPS) + ADAM_WD * w)
    return delta, m2, v2


def _adam_sharded(parts, w, m, v, layer, earlier, name):
    depth, r, c = w.shape
    tr = max(t for t in range(16, 129, 16) if r % t == 0)

    def body(p_ref, w_ref, m_ref, v_ref, *rest):
        g_ref, d_ref, m2_ref, v2_ref = rest[-4:]
        g = p_ref[0].astype(F32)
        for s in range(1, N_DEV):
            g = g + p_ref[s].astype(F32)
        delta, m2, v2 = _adam_math(w_ref[...], g, m_ref[...], v_ref[...])
        g_ref[...] = g
        d_ref[...] = delta
        m2_ref[...] = m2
        v2_ref[...] = v2

    blk = pl.BlockSpec((None, tr, c), lambda i: (layer, i, 0))
    extra = [] if earlier is None else list(earlier)
    return pl.pallas_call(
        body, name=name, grid=(r // tr,),
        in_specs=[pl.BlockSpec((N_DEV, tr, c), lambda i: (0, i, 0)), blk, blk, blk]
        + [pl.BlockSpec(memory_space=pl.ANY)] * len(extra),
        out_specs=[blk, blk, blk, blk],
        out_shape=[jax.ShapeDtypeStruct((depth, r, c), F32)] * 4,
        input_output_aliases={4 + i: i for i in range(len(extra))},
        compiler_params=_cp("parallel"),
    )(parts, w, m, v, *extra)


def _adam_packed(g, w, m, v, name):
    r, c = g.shape

    def body(g_ref, w_ref, m_ref, v_ref, d_ref, m2_ref, v2_ref):
        delta, m2, v2 = _adam_math(w_ref[...], g_ref[...], m_ref[...], v_ref[...])
        d_ref[...] = delta
        m2_ref[...] = m2
        v2_ref[...] = v2

    blk = pl.BlockSpec((r, c), lambda i: (0, 0))
    return pl.pallas_call(
        body, name=name, grid=(1,),
        in_specs=[blk, blk, blk, blk], out_specs=[blk, blk, blk],
        out_shape=[jax.ShapeDtypeStruct((r, c), F32)] * 3,
        compiler_params=_cp("arbitrary"),
    )(g, w, m, v)


def _my_place():
    return lax.axis_index("x"), lax.axis_index("y"), lax.axis_index("c")


def _flip(place, rel):
    x, y, c = place
    return (1 - x if rel & 4 else x, 1 - y if rel & 2 else y, 1 - c if rel & 1 else c)


def _index(place):
    return 4 * place[0] + 2 * place[1] + place[2]


def _all_gather(shards, name):
    na = len(shards)

    def body(*refs):
        xs, outs = refs[:na], refs[na:2 * na]
        send_sems, recv_sems, local_sems = refs[2 * na:]
        me = _my_place()
        sibling = _flip(me, 1)
        chips = [_flip(me, 4), _flip(me, 2), _flip(me, 6)]

        def copy(a, k, block, to, src=None):
            slot = outs[a].at[_index(block)]
            return pltpu.make_async_remote_copy(
                src_ref=slot if src is None else src, dst_ref=slot,
                send_sem=send_sems.at[a, k], recv_sem=recv_sems.at[a, k],
                device_id=to, device_id_type=MESH)

        mine = [pltpu.make_async_copy(xs[a], outs[a].at[_index(me)], local_sems.at[a]) for a in range(na)]
        for cp in mine:
            cp.start()
        first = []
        for a in range(na):
            first.append(copy(a, 0, me, sibling, src=xs[a]))
            first += [copy(a, 1 + j, me, chip, src=xs[a]) for j, chip in enumerate(chips)]
        for cp in first:
            cp.start()
        passed = []
        for a in range(na):
            for j, chip in enumerate(chips):
                copy(a, 1 + j, chip, me).wait_recv()
                fwd = copy(a, 4 + j, chip, sibling)
                fwd.start()
                passed.append(fwd)
        for a in range(na):
            copy(a, 0, sibling, me).wait_recv()
            for j, chip in enumerate(chips):
                copy(a, 4 + j, _flip(chip, 1), me).wait_recv()
        for cp in first + passed:
            cp.wait_send()
        for cp in mine:
            cp.wait()

    hbm = pl.BlockSpec(memory_space=pl.ANY)
    return pl.pallas_call(
        body, name=name,
        in_specs=[hbm] * na, out_specs=[hbm] * na,
        out_shape=[jax.ShapeDtypeStruct((N_DEV,) + s.shape, s.dtype) for s in shards],
        scratch_shapes=[pltpu.SemaphoreType.DMA((na, 7)), pltpu.SemaphoreType.DMA((na, 7)),
                        pltpu.SemaphoreType.DMA((na,))],
    )(*shards)


_HBM = pl.BlockSpec(memory_space=pltpu.HBM)
_SEM = pl.BlockSpec(memory_space=pltpu.SEMAPHORE)
_EFFECT = pltpu.SideEffectType.DATAFLOW_SIDE_EFFECTING


def _exchange_copy(src_ref, land_ref, send_sems, recv_sems, a, rel, me, scatter, landed):
    peer = _flip(me, rel)
    src = src_ref.at[_index(peer)] if scatter else src_ref
    return pltpu.make_async_remote_copy(
        src_ref=src, dst_ref=land_ref.at[_index(peer if landed else me)],
        send_sem=send_sems.at[a * (N_DEV - 1) + rel - 1], recv_sem=recv_sems.at[a * (N_DEV - 1) + rel - 1],
        device_id=peer, device_id_type=MESH)


def _own_slot(data, me, scatter):
    if scatter:
        own = lax.dynamic_slice_in_dim(data, me, 1, axis=0)
        shape = data.shape
    else:
        own = data[None]
        shape = (N_DEV,) + data.shape
    start = (me,) + (0,) * (len(shape) - 1)
    return lax.dynamic_update_slice(lax.empty(shape, data.dtype), own, start)


def _exchange_start(groups, me, scatter, name, after=None):
    sizes = [len(g) for g in groups]
    srcs = [a for g in groups for a in g]
    lands = [_own_slot(a, me, scatter) for a in srcs]
    na, ng = len(srcs), len(groups)
    deps = [] if after is None else [after]

    def body(*refs):
        src_refs, land_refs = refs[:na], refs[na:2 * na]
        sems = refs[2 * na + len(deps):2 * na + len(deps) + 2 * ng]
        token = refs[-1]
        place = _my_place()
        a = 0
        for g, size in enumerate(sizes):
            for k in range(size):
                for rel in range(1, N_DEV):
                    _exchange_copy(src_refs[a], land_refs[a], sems[2 * g], sems[2 * g + 1], k, rel, place, scatter,
                                   False).start()
                a += 1
        token[...] = jnp.zeros_like(token)

    sem_shapes = [pltpu.SemaphoreType.DMA((size * (N_DEV - 1),)) for size in sizes for _ in range(2)]
    outs = pl.pallas_call(
        body, name=name,
        in_specs=[_HBM] * (2 * na) + [pl.BlockSpec(memory_space=pl.ANY)] * len(deps),
        out_specs=[_SEM] * (2 * ng) + [_HBM] * (2 * na) + [pl.BlockSpec(memory_space=pltpu.VMEM)],
        out_shape=sem_shapes + [pltpu.HBM(a.shape, a.dtype) for a in srcs + lands]
        + [jax.ShapeDtypeStruct((SUBLANES, LANES), F32)],
        input_output_aliases={i: 2 * ng + i for i in range(2 * na)},
        compiler_params=pltpu.CompilerParams(has_side_effects=_EFFECT),
    )(*[pltpu.with_memory_space_constraint(a, pltpu.HBM) for a in srcs + lands], *deps)
    sems, thru, token = outs[:2 * ng], outs[2 * ng:2 * ng + 2 * na], outs[-1]
    handles, a = [], 0
    for g, size in enumerate(sizes):
        handles.append((sems[2 * g], sems[2 * g + 1], thru[a:a + size], thru[na + a:na + a + size]))
        a += size
    return handles, token


def _exchange_wait(handle, after, scatter, name):
    send_sems, recv_sems, srcs, lands = handle
    na = len(srcs)

    def body(*refs):
        src_refs, land_refs = refs[:na], refs[na:2 * na]
        send_ref, recv_ref = refs[2 * na], refs[2 * na + 1]
        place = _my_place()
        for a in range(na):
            for rel in range(1, N_DEV):
                cp = _exchange_copy(src_refs[a], land_refs[a], send_ref, recv_ref, a, rel, place, scatter, True)
                cp.wait_send()
                cp.wait_recv()

    outs = pl.pallas_call(
        body, name=name,
        in_specs=[_HBM] * (2 * na) + [_SEM, _SEM, pl.BlockSpec(memory_space=pl.ANY)],
        out_specs=[_HBM] * (2 * na),
        out_shape=[pltpu.HBM(a.shape, a.dtype) for a in list(srcs) + list(lands)],
        input_output_aliases={i: i for i in range(2 * na)},
        compiler_params=pltpu.CompilerParams(has_side_effects=_EFFECT),
    )(*srcs, *lands, send_sems, recv_sems, after)
    return outs[na:]


def _behind(arr, token):
    return arr + token[0:1, 0:1]


def _all_reduce_small(g, name):
    _, r, c = g.shape

    def body(g_ref, o_ref, land_ref, red_ref, send1, recv1, send2, recv2):
        me = _my_place()
        idx = _index(me)

        def scatter(rel):
            peer = _flip(me, rel)
            return pltpu.make_async_remote_copy(
                src_ref=g_ref.at[_index(peer)], dst_ref=land_ref.at[idx],
                send_sem=send1.at[rel - 1], recv_sem=recv1.at[rel - 1], device_id=peer, device_id_type=MESH)

        def gather(rel):
            peer = _flip(me, rel)
            return pltpu.make_async_remote_copy(
                src_ref=red_ref, dst_ref=o_ref.at[idx],
                send_sem=send2.at[rel - 1], recv_sem=recv2.at[rel - 1], device_id=peer, device_id_type=MESH)

        for rel in range(1, N_DEV):
            scatter(rel).start()
        land_ref[idx] = g_ref[idx]
        for rel in range(1, N_DEV):
            scatter(rel).wait()
        acc = land_ref[0]
        for s in range(1, N_DEV):
            acc = acc + land_ref[s]
        red_ref[...] = acc
        for rel in range(1, N_DEV):
            gather(rel).start()
        o_ref[idx] = acc
        for rel in range(1, N_DEV):
            gather(rel).wait()

    vmem = pl.BlockSpec(memory_space=pltpu.VMEM)
    return pl.pallas_call(
        body, name=name,
        in_specs=[vmem], out_specs=vmem,
        out_shape=jax.ShapeDtypeStruct(g.shape, F32),
        scratch_shapes=[pltpu.VMEM(g.shape, F32), pltpu.VMEM((r, c), F32)]
        + [pltpu.SemaphoreType.DMA((N_DEV - 1,))] * 4,
        compiler_params=pltpu.CompilerParams(vmem_limit_bytes=VMEM_LIMIT),
    )(g)


def _ssm_discretize(a_re, a_im, log_dt, b_re, b_im):
    dt = jnp.exp(log_dt)[:, None]
    mag = jnp.exp(a_re * dt)
    lr, li = mag * jnp.cos(a_im * dt), mag * jnp.sin(a_im * dt)
    den = a_re * a_re + a_im * a_im
    qr = ((lr - 1.0) * a_re + li * a_im) / den
    qi = (li * a_re - (lr - 1.0) * a_im) / den
    bbr = qr[..., None] * b_re - qi[..., None] * b_im
    bbi = qr[..., None] * b_im + qi[..., None] * b_re
    return lr, li, bbr, bbi


def _halves(a):
    return a.reshape((2, HALF_GROUPS) + a.shape[1:])


def _block_diag_mask(g, r, c):
    rows = lax.broadcasted_iota(jnp.int32, (g * r, g * c), 0) // r
    cols = lax.broadcasted_iota(jnp.int32, (g * r, g * c), 1) // c
    return rows == cols


def _block_diag(blocks):
    g, r, c = blocks.shape
    spread = jnp.tile(jnp.eye(c, dtype=blocks.dtype), (1, g))
    full = jnp.dot(blocks.reshape(g * r, c), spread, precision=lax.Precision.HIGHEST)
    return jnp.where(_block_diag_mask(g, r, c), full, 0.0)


def _block_diag_take(dense, g, r, c):
    gather = jnp.tile(jnp.eye(c, dtype=dense.dtype), (g, 1))
    kept = jnp.where(_block_diag_mask(g, r, c), dense, 0.0)
    return jnp.dot(kept, gather, precision=lax.Precision.HIGHEST).reshape(g, r, c)


def _ssm_matrices(bbr, bbi, c_re, c_im, glu_w, glu_b, d_skip):
    bre, bim = _halves(jnp.swapaxes(bbr, 1, 2)), _halves(jnp.swapaxes(bbi, 1, 2))
    bblk = jnp.stack([jnp.concatenate([_block_diag(bre[h]), _block_diag(bim[h])], axis=1) for h in range(2)])
    cre, cim = _halves(jnp.swapaxes(c_re, 1, 2)), _halves(jnp.swapaxes(c_im, 1, 2))
    cblk = jnp.stack([jnp.concatenate([_block_diag(cre[h]), -_block_diag(cim[h])], axis=0) for h in range(2)])
    glu = jnp.concatenate([_block_diag(glu_w[:, :, :SSM_CH]), _block_diag(glu_w[:, :, SSM_CH:])], axis=1)
    glub = jnp.concatenate([glu_b[:, :SSM_CH].reshape(1, -1), glu_b[:, SSM_CH:].reshape(1, -1)], axis=1)
    return bblk.astype(BF16), cblk.astype(BF16), glu.astype(BF16), glub, d_skip.reshape(1, -1)


def _scan_constants(lr, li, reverse):
    if reverse:
        li = -li
    pows = [(lr, li)]
    for _ in range(SUBLANES - 1):
        pr, pi = pows[-1]
        pows.append((pr * lr - pi * li, pr * li + pi * lr))
    row = jnp.arange(SUBLANES)[:, None]

    def flat(a):
        return a.reshape(2, 1, HALF_ST)

    mats = []
    for s in (1, 2, 4):
        keep = (row + s <= SUBLANES - 1) if reverse else (row >= s)
        mats.append(tuple(jnp.where(keep[None], flat(p), 0.0) for p in pows[s - 1]))
    order = [SUBLANES - 1 - j for j in range(SUBLANES)] if reverse else list(range(SUBLANES))
    mats.append(tuple(jnp.concatenate([flat(pows[j][k]) for j in order], axis=1) for k in range(2)))
    return jnp.stack([jnp.concatenate([m[0], m[1]], axis=2) for m in mats], axis=1)


def _pack(arrs, rows):
    flat = jnp.concatenate([a.reshape(-1) for a in arrs])
    return jnp.pad(flat, (0, rows * LANES - flat.shape[0])).reshape(rows, LANES)


def _unpack(buf, like):
    flat = buf.reshape(-1)
    out, off = [], 0
    for a in like:
        out.append(flat[off:off + a.size].reshape(a.shape))
        off += a.size
    return out


SMALL = ("norm_ffn1", "norm_mix", "ssm_a_re", "ssm_a_im", "ssm_log_dt", "ssm_b_re", "ssm_b_im", "ssm_c_re",
         "ssm_c_im", "ssm_d", "ssm_glu_w", "ssm_glu_b", "gm_v_gain", "gm_w_s", "gm_b_s", "gain_ssm_out",
         "gain_gm_out", "norm_ffn2", "norm_final")
BIG = ("ffn1_w_in", "ffn1_w_out", "mix_w_in", "mix_w_out", "ffn2_w_in", "ffn2_w_out")
TRANSPOSED = ("ffn1_w_in", "mix_w_in", "ffn2_w_in")
WEIGHTS = ("norm_ffn1", "ffn1_w_in", "ffn1_w_out", "norm_mix", "mix_w_in", "ssm_a_re", "ssm_a_im", "ssm_log_dt",
           "ssm_b_re", "ssm_b_im", "ssm_c_re", "ssm_c_im", "ssm_d", "ssm_glu_w", "ssm_glu_b", "gm_v_gain", "gm_w_s",
           "gm_b_s", "gain_ssm_out", "gain_gm_out", "mix_w_out", "norm_ffn2", "ffn2_w_in", "ffn2_w_out", "norm_final")


def _step(x, target, w, m, v):
    batch, seq, _ = x.shape
    n = batch * seq
    depth = w["norm_ffn1"].shape[0]
    tm = min(512, n)
    tm_ffn = min(1024, n)
    tk = min(2048, n)
    t_chunk = min(256, seq)
    gm_rows = min(512, seq)
    x = x.reshape(n, D_MODEL)
    target = target.reshape(n, D_MODEL)

    assert depth == 2
    me = _index(_my_place())
    shard = lambda group, l: [w[f"{group}_w_in"][l].astype(BF16), w[f"{group}_w_out"][l].astype(BF16)]
    batches = ([("mix", 0), ("ffn2", 0)], [("ffn1", 1), ("mix", 1)], [("ffn2", 1)])
    gathered, pending = {("ffn1", 0): tuple(_all_gather(shard("ffn1", 0), "all_gather_first"))}, {}

    def gather_start(i, after):
        handles, tok = _exchange_start([shard(g, l) for g, l in batches[i]], me, False, f"all_gather_start_{i}",
                                       after)
        pending.update(zip(batches[i], handles))
        return tok

    def weights(group, l, after=None):
        if (group, l) not in gathered:
            w_in, w_out = _exchange_wait(pending[(group, l)], after, False, f"all_gather_wait_{group}_{l}")
            if group == "mix":
                w_in = jnp.transpose(w_in, (1, 0, 2)).reshape(D_MODEL, IN_COLS)
                w_out = w_out.reshape(D_MODEL, D_MODEL)
            gathered[(group, l)] = (w_in, w_out)
        return gathered[(group, l)]

    tril = jnp.tril(jnp.ones((GM_CHUNK, GM_CHUNK), bool))
    layers = []
    for l in range(depth):
        disc, disc_vjp = jax.vjp(_ssm_discretize, w["ssm_a_re"][l], w["ssm_a_im"][l], w["ssm_log_dt"][l],
                                 w["ssm_b_re"][l], w["ssm_b_im"][l])
        lr, li, bbr, bbi = disc
        bblk, cblk, glu, glub, dskip = _ssm_matrices(bbr, bbi, w["ssm_c_re"][l], w["ssm_c_im"][l],
                                                     w["ssm_glu_w"][l], w["ssm_glu_b"][l], w["ssm_d"][l])
        layers.append(dict(
            disc_vjp=disc_vjp, lr=lr, li=li, bblk=bblk, cblk=cblk, glu=glu, glub=glub, dskip=dskip,
            fwdc=_scan_constants(lr, li, False), revc=_scan_constants(lr, li, True),
            w_tril=jnp.where(tril[None], w["gm_w_s"][l], 0.0).astype(BF16),
            gm_bias=jnp.repeat(w["gm_b_s"][l].T, GM_HEAD_DIM, axis=1),
            g1=w["norm_ffn1"][l][None], gmix=w["norm_mix"][l][None], g2=w["norm_ffn2"][l][None],
            gv=w["gm_v_gain"][l][None], gs=w["gain_ssm_out"][l][None], gg=w["gain_gm_out"][l][None],
        ))

    saved = []
    for l in range(depth):
        p = layers[l]
        x0 = x
        g1, gmix, g2 = p["g1"], p["gmix"], p["g2"]
        w_in, w_out = weights("ffn1", l, x0)
        if l == 0:
            g1 = _behind(g1, gather_start(0, w_in))
        x1, xn1, gu1 = _ffn_fwd(x0, g1, w_in, w_out, tm_ffn, f"ffn1_fwd_{l}")
        if l == 0:
            gmix = _behind(gmix, gather_start(1, x1))
        mwi, mwo = weights("mix", l, x1)
        z = _mix_in_fwd(x1, gmix, mwi, tm, f"mix_in_fwd_{l}")
        y_ssm, h = _ssm_fwd(z, p["bblk"], p["cblk"], p["glu"], p["glub"], p["dskip"], p["fwdc"], batch, t_chunk,
                            f"ssm_fwd_{l}")
        y_gm = _gm_fwd(z, p["gv"], p["w_tril"], p["gm_bias"], gm_rows, f"gm_fwd_{l}")
        x2 = _mix_out_fwd(y_ssm, y_gm, p["gs"], p["gg"], mwo, x1, tm, f"mix_out_fwd_{l}")
        if l == 0:
            g2 = _behind(g2, gather_start(2, x2))
        x, xn2, gu2 = _ffn_fwd(x2, g2, *weights("ffn2", l, x2), tm_ffn, f"ffn2_fwd_{l}")
        saved.append((x0, x1, x2, z, h, y_ssm, y_gm, xn1, gu1, xn2, gu2))

    dx, sq, dnf = _loss_head(x, w["norm_final"][None], target, tm, "loss_head")
    loss = lax.psum((0.5 / D_MODEL) * jnp.sum(sq), AXES)

    small = {k: [None] * depth for k in SMALL if k != "norm_final"}
    sent = []

    def send(group, l, keys, parts):
        (handle,), tok = _exchange_start([parts], me, True, f"reduce_scatter_start_{group}_{l}")
        sent.append((group, l, keys, handle))
        return tok

    token = None
    for l in reversed(range(depth)):
        p = layers[l]
        x0, x1, x2, z, h, y_ssm, y_gm, xn1, gu1, xn2, gu2 = saved[l]
        mwi, mwo = weights("mix", l)
        dx_out = dx
        g2 = p["g2"] if token is None else _behind(p["g2"], token)
        dx, dgu, act, dgain = _ffn_bwd(x2, g2, dx_out, gu2, *weights("ffn2", l), tm, f"ffn2_bwd_{l}")
        dw_in = _ffn_dw_in(xn2, dgu, tk, f"ffn2_dw_in_{l}")
        dw_out = _ffn_dw_out(act, dx_out, tk, f"ffn2_dw_out_{l}").reshape(N_DEV, FF_SHARD // 2, D_MODEL)
        token = send("ffn2", l, ("ffn2_w_in", "ffn2_w_out"), [dw_in, dw_out])
        small["norm_ffn2"][l] = dgain.sum(0)

        dy_ssm, dy_gm, dwo, dgs, dgg = _mix_out_bwd(y_ssm, y_gm, _behind(p["gs"], token), p["gg"], mwo, dx, tm,
                                                    f"mix_out_bwd_{l}")
        dwo = dwo.astype(BF16).reshape(N_DEV, D_MODEL // N_DEV, D_MODEL)
        small["gain_ssm_out"][l] = dgs.sum(0)
        small["gain_gm_out"][l] = dgg.sum(0)

        du_ssm, dglu, dglub, ddskip, dct, db, q = _ssm_bwd(
            z, h, dy_ssm, p["bblk"], p["cblk"], p["glu"], p["glub"], p["dskip"], p["revc"], batch, t_chunk,
            f"ssm_bwd_{l}")
        du_gm, dv_gm, dws, dbias, dgv = _gm_bwd(z, dy_gm, p["gv"], p["w_tril"], p["gm_bias"], gm_rows, f"gm_bwd_{l}")

        q = q.sum(0).reshape(2, 2, HALF_GROUPS, SSM_STATE)
        qr, qi = q[:, 0].reshape(SSM_GROUPS, SSM_STATE), q[:, 1].reshape(SSM_GROUPS, SSM_STATE)
        den = p["lr"] * p["lr"] + p["li"] * p["li"]
        d_re = (qr * p["lr"] + qi * p["li"]) / den
        d_im = (qi * p["lr"] - qr * p["li"]) / den
        dbb = jnp.stack([_block_diag_take(db[hf, :, k * HALF_ST:(k + 1) * HALF_ST], HALF_GROUPS, SSM_CH, SSM_STATE)
                         for k in range(2) for hf in range(2)]).reshape(2, SSM_GROUPS, SSM_CH, SSM_STATE)
        dcc = jnp.stack([_block_diag_take(dct[hf, :, k * HALF_ST:(k + 1) * HALF_ST], HALF_GROUPS, SSM_CH, SSM_STATE)
                         for k in range(2) for hf in range(2)]).reshape(2, SSM_GROUPS, SSM_CH, SSM_STATE)
        da_re, da_im, dlog_dt, db_re, db_im = p["disc_vjp"](
            (d_re, -d_im, jnp.swapaxes(dbb[0], 1, 2), jnp.swapaxes(dbb[1], 1, 2)))
        small["ssm_a_re"][l], small["ssm_a_im"][l], small["ssm_log_dt"][l] = da_re, da_im, dlog_dt
        small["ssm_b_re"][l], small["ssm_b_im"][l] = db_re, db_im
        small["ssm_c_re"][l], small["ssm_c_im"][l] = dcc[0], -dcc[1]
        small["ssm_d"][l] = ddskip.sum(0).reshape(SSM_GROUPS, SSM_CH)
        small["ssm_glu_w"][l] = jnp.concatenate(
            [_block_diag_take(dglu[:, :SSM_WIDTH], SSM_GROUPS, SSM_CH, SSM_CH),
             _block_diag_take(dglu[:, SSM_WIDTH:], SSM_GROUPS, SSM_CH, SSM_CH)], axis=2)
        dglub = dglub.sum(0)
        small["ssm_glu_b"][l] = jnp.concatenate(
            [dglub[:SSM_WIDTH].reshape(SSM_GROUPS, SSM_CH), dglub[SSM_WIDTH:].reshape(SSM_GROUPS, SSM_CH)], axis=1)
        small["gm_v_gain"][l] = dgv.sum(0)
        small["gm_w_s"][l] = jnp.where(tril[None], dws, 0.0)
        small["gm_b_s"][l] = dbias.reshape(GM_CHUNK, GM_HEADS, GM_HEAD_DIM).sum(-1).T

        dx, dwi, dgain = _mix_in_bwd(x1, p["gmix"], du_ssm, du_gm, dv_gm, dx, mwi, tm, f"mix_in_bwd_{l}")
        dwi = dwi.astype(BF16).reshape(N_DEV, IN_COLS // N_DEV, D_MODEL)
        token = send("mix", l, ("mix_w_in", "mix_w_out"), [dwi, dwo])
        small["norm_mix"][l] = dgain.sum(0)

        dx_out = dx
        dx, dgu, act, dgain = _ffn_bwd(x0, _behind(p["g1"], token), dx_out, gu1, *weights("ffn1", l), tm,
                                       f"ffn1_bwd_{l}")
        small["norm_ffn1"][l] = dgain.sum(0)
        if l > 0:
            dw_in = _ffn_dw_in(xn1, dgu, tk, f"ffn1_dw_in_{l}")
            dw_out = _ffn_dw_out(act, dx_out, tk, f"ffn1_dw_out_{l}").reshape(N_DEV, FF_SHARD // 2, D_MODEL)
            token = send("ffn1", l, ("ffn1_w_in", "ffn1_w_out"), [dw_in, dw_out])
            continue
        small_g = [jnp.stack(small[k]) if k != "norm_final" else dnf.sum(0) for k in SMALL]
        total = sum(int(math.prod(w[k].shape)) for k in SMALL)
        rows = -(-total // (LANES * N_DEV * SUBLANES)) * N_DEV * SUBLANES
        g_all = _all_reduce_small(_pack(small_g, rows).reshape(N_DEV, rows // N_DEV, LANES), "all_reduce_small")
        dw_in = _ffn_dw_in(xn1, dgu, tk, f"ffn1_dw_in_{l}", after=g_all)
        token = send("ffn1_in", l, ("ffn1_w_in",), [dw_in])
        dw_out = _ffn_dw_out(act, dx_out, tk, f"ffn1_dw_out_{l}", after=token).reshape(
            N_DEV, FF_SHARD // 2, D_MODEL)
        token = send("ffn1_out", l, ("ffn1_w_out",), [dw_out])

    grad_x = dx.reshape(batch, seq, D_MODEL)
    grads, deltas, new_m, new_v = {}, {}, {}, {}

    g_all = _behind(g_all.reshape(rows, LANES), token)
    like = [w[k] for k in SMALL]
    d_p, m_p, v_p = _adam_packed(g_all, _pack(like, rows), _pack([m[k] for k in SMALL], rows),
                                 _pack([v[k] for k in SMALL], rows), "adam_small")
    for k, g_, d_, m_, v_ in zip(SMALL, _unpack(g_all, like), _unpack(d_p, like), _unpack(m_p, like),
                                 _unpack(v_p, like)):
        grads[k], deltas[k], new_m[k], new_v[k] = g_, d_, m_, v_

    results = {}
    after = d_p
    for group, l, keys, handle in sent:
        landed = _exchange_wait(handle, after, True, f"reduce_scatter_wait_{group}_{l}")
        for k, parts in zip(keys, landed):
            view = (lambda a: jnp.swapaxes(a, 1, 2)) if k in TRANSPOSED else (lambda a: a)
            results[k] = _adam_sharded(parts, view(w[k]), view(m[k]), view(v[k]), l, results.get(k),
                                       f"adam_{k}_{l}")
            after = results[k][0]
    for k in BIG:
        view = (lambda a: jnp.swapaxes(a, 1, 2)) if k in TRANSPOSED else (lambda a: a)
        grads[k], deltas[k], new_m[k], new_v[k] = [view(a) for a in results[k]]
    return loss, grad_x, grads, deltas, new_m, new_v


def kernel(x, norm_ffn1, ffn1_w_in, ffn1_w_out, norm_mix, mix_w_in, ssm_a_re, ssm_a_im, ssm_log_dt, ssm_b_re, ssm_b_im, ssm_c_re, ssm_c_im, ssm_d, ssm_glu_w, ssm_glu_b, gm_v_gain, gm_w_s, gm_b_s, gain_ssm_out, gain_gm_out, mix_w_out, norm_ffn2, ffn2_w_in, ffn2_w_out, norm_final, loss_target, m_norm_ffn1, m_ffn1_w_in, m_ffn1_w_out, m_norm_mix, m_mix_w_in, m_ssm_a_re, m_ssm_a_im, m_ssm_log_dt, m_ssm_b_re, m_ssm_b_im, m_ssm_c_re, m_ssm_c_im, m_ssm_d, m_ssm_glu_w, m_ssm_glu_b, m_gm_v_gain, m_gm_w_s, m_gm_b_s, m_gain_ssm_out, m_gain_gm_out, m_mix_w_out, m_norm_ffn2, m_ffn2_w_in, m_ffn2_w_out, m_norm_final, v_norm_ffn1, v_ffn1_w_in, v_ffn1_w_out, v_norm_mix, v_mix_w_in, v_ssm_a_re, v_ssm_a_im, v_ssm_log_dt, v_ssm_b_re, v_ssm_b_im, v_ssm_c_re, v_ssm_c_im, v_ssm_d, v_ssm_glu_w, v_ssm_glu_b, v_gm_v_gain, v_gm_w_s, v_gm_b_s, v_gain_ssm_out, v_gain_gm_out, v_mix_w_out, v_norm_ffn2, v_ffn2_w_in, v_ffn2_w_out, v_norm_final):
    args = locals()
    w = {k: args[k] for k in WEIGHTS}
    m = {k: args["m_" + k] for k in WEIGHTS}
    v = {k: args["v_" + k] for k in WEIGHTS}
    loss, grad_x, grads, deltas, new_m, new_v = _step(x, loss_target, w, m, v)
    return (loss, grad_x, *[grads[k] for k in WEIGHTS], *[deltas[k] for k in WEIGHTS],
            *[new_m[k] for k in WEIGHTS], *[new_v[k] for k in WEIGHTS])
```

```python
import functools
import math

import jax
import jax.numpy as jnp
from jax import lax
from jax.experimental import pallas as pl
from jax.experimental.pallas import tpu as pltpu

F32 = jnp.float32
BF16 = jnp.bfloat16
MESH = pl.DeviceIdType.MESH
AXES = ("x", "y", "c")

N_DEV = 8
D_MODEL = 1024
D_FF = 2816
FF_SHARD = 2 * D_FF // N_DEV
FF_CHUNKS = 4
MXU_DIM = 256
FF_PIECES = tuple((lo, min(lo + MXU_DIM, FF_SHARD)) for lo in range(0, FF_SHARD, MXU_DIM))
SSM_WIDTH = 512
SSM_CH = 16
SSM_GROUPS = 32
SSM_STATE = 64
HALF_GROUPS = 16
HALF_IN = HALF_GROUPS * SSM_CH
HALF_ST = HALF_GROUPS * SSM_STATE
GM_WIDTH = 512
GM_HEADS = 4
GM_HEAD_DIM = 128
GM_CHUNK = 128
IN_COLS = SSM_WIDTH + 2 * GM_WIDTH
EPS = 1e-6
SUBLANES = 8
LANES = 128

ADAM_LR = 0.001
ADAM_B1 = 0.9
ADAM_B2 = 0.999
ADAM_EPS = 1e-08
ADAM_WD = 0.01
ADAM_STEP = 10

VMEM_LIMIT = 46 * 1024 * 1024


def _cp(*sem):
    return pltpu.CompilerParams(dimension_semantics=sem, vmem_limit_bytes=VMEM_LIMIT)


def _rms_fwd(x, g):
    r = lax.rsqrt(jnp.mean(x * x, axis=-1, keepdims=True) + EPS)
    xh = x * r
    return xh * g, xh, r


def _rms_bwd(dy, xh, r, g):
    dxh = dy * g
    dx = r * (dxh - xh * jnp.mean(dxh * xh, axis=-1, keepdims=True))
    return dx, dy * xh


def _rows8(a):
    m, n = a.shape
    return a.reshape(m // SUBLANES, SUBLANES, n).sum(axis=0)


_GELU_K = math.sqrt(2.0 / math.pi)
_GELU_C = 0.044715


def _gelu(x):
    th = jnp.tanh(_GELU_K * (x + _GELU_C * x * x * x))
    return 0.5 * x * (1.0 + th), th


def _gelu_grad(x, th):
    return 0.5 * (1.0 + th) + 0.5 * x * (1.0 - th * th) * (_GELU_K * (1.0 + 3.0 * _GELU_C * x * x))


def _dot(a, b):
    return jnp.dot(a, b, preferred_element_type=F32)


def _dot_nt(a, b):
    return lax.dot_general(a, b, (((1,), (1,)), ((), ())), preferred_element_type=F32)


def _dot_tn(a, b):
    return lax.dot_general(a, b, (((0,), (0,)), ((), ())), preferred_element_type=F32)


def _ffn_fwd(x, gain, w_in_ag, w_out_ag, tm, name):
    n = x.shape[0]

    def body(x_ref, g_ref, wg_ref, wu_ref, wo_ref, o_ref, xn_ref, gu_ref):
        j = pl.program_id(1)

        @pl.when(j == 0)
        def _():
            xv = x_ref[...]
            y, _, _ = _rms_fwd(xv, g_ref[...])
            xn_ref[...] = y.astype(BF16)
            o_ref[...] = xv

        xn = xn_ref[...]
        wo = wo_ref[...].reshape(FF_SHARD, D_MODEL)
        out = None
        for lo, hi in FF_PIECES:
            gg = _dot(xn, wg_ref[:, lo:hi])
            uu = _dot(xn, wu_ref[:, lo:hi])
            gu_ref[0, :, lo:hi] = gg.astype(BF16)
            gu_ref[1, :, lo:hi] = uu.astype(BF16)
            act = (gg * jax.nn.sigmoid(gg) * uu).astype(BF16)
            part = _dot(act, wo[lo:hi, :])
            out = part if out is None else out + part
        o_ref[...] += 0.5 * out

    return pl.pallas_call(
        body, name=name, grid=(n // tm, FF_CHUNKS),
        in_specs=[
            pl.BlockSpec((tm, D_MODEL), lambda i, j: (i, 0)),
            pl.BlockSpec((1, D_MODEL), lambda i, j: (0, 0)),
            pl.BlockSpec((None, D_MODEL, FF_SHARD), lambda i, j: (j, 0, 0)),
            pl.BlockSpec((None, D_MODEL, FF_SHARD), lambda i, j: (j + FF_CHUNKS, 0, 0)),
            pl.BlockSpec((2, FF_SHARD // 2, D_MODEL), lambda i, j: (j, 0, 0)),
        ],
        out_specs=[
            pl.BlockSpec((tm, D_MODEL), lambda i, j: (i, 0)),
            pl.BlockSpec((tm, D_MODEL), lambda i, j: (i, 0)),
            pl.BlockSpec((None, 2, tm, FF_SHARD), lambda i, j: (j, 0, i, 0)),
        ],
        out_shape=[
            jax.ShapeDtypeStruct((n, D_MODEL), F32),
            jax.ShapeDtypeStruct((n, D_MODEL), BF16),
            jax.ShapeDtypeStruct((FF_CHUNKS, 2, n, FF_SHARD), BF16),
        ],
        compiler_params=_cp("parallel", "arbitrary"),
    )(x, gain, w_in_ag, w_in_ag, w_out_ag)


def _ffn_bwd(x, gain, dy, gu, w_in_ag, w_out_ag, tm, name):
    n = x.shape[0]
    steps = FF_CHUNKS + 1
    first = lambda j: jnp.minimum(j, FF_CHUNKS - 1)
    second = lambda j: jnp.maximum(j - 1, 0)

    def body(x_ref, g_ref, dy_ref, gu_ref, wg_ref, wu_ref, wo_ref, dx_ref, dgu_ref, act_ref, dgain_ref, dyb_ref,
             held_ref):
        i, j = pl.program_id(0), pl.program_id(1)

        @pl.when(jnp.logical_and(i == 0, j == 0))
        def _():
            dgain_ref[...] = jnp.zeros_like(dgain_ref)
            held_ref[...] = jnp.zeros_like(held_ref)

        @pl.when(j == 0)
        def _():
            dyb_ref[...] = (0.5 * dy_ref[...]).astype(BF16)
            dx_ref[...] = jnp.zeros_like(dx_ref)

        held = held_ref[1 - j % 2]
        part = _dot_nt(held[0], wg_ref[...]) + _dot_nt(held[1], wu_ref[...])
        dx_ref[...] += jnp.where(j > 0, part, 0.0)

        dyb = dyb_ref[...]
        wo = wo_ref[...].reshape(FF_SHARD, D_MODEL)
        slot = j % 2
        for lo, hi in FF_PIECES:
            gg = gu_ref[0, :, lo:hi].astype(F32)
            uu = gu_ref[1, :, lo:hi].astype(F32)
            dact = _dot_nt(dyb, wo[lo:hi, :])
            sig = jax.nn.sigmoid(gg)
            silu = gg * sig
            act_ref[:, lo:hi] = (silu * uu).astype(BF16)
            du = (dact * silu).astype(BF16)
            dg = (dact * uu * (sig * (1.0 + gg * (1.0 - sig)))).astype(BF16)
            dgu_ref[0, :, lo:hi] = dg
            dgu_ref[1, :, lo:hi] = du
            held_ref[slot, 0, :, lo:hi] = dg
            held_ref[slot, 1, :, lo:hi] = du

        @pl.when(j == steps - 1)
        def _():
            g = g_ref[...]
            _, xh, r = _rms_fwd(x_ref[...], g)
            dx, dgr = _rms_bwd(dx_ref[...], xh, r, g)
            dx_ref[...] = dy_ref[...] + dx
            dgain_ref[...] += _rows8(dgr)

    return pl.pallas_call(
        body, name=name, grid=(n // tm, steps),
        in_specs=[
            pl.BlockSpec((tm, D_MODEL), lambda i, j: (i, 0)),
            pl.BlockSpec((1, D_MODEL), lambda i, j: (0, 0)),
            pl.BlockSpec((tm, D_MODEL), lambda i, j: (i, 0)),
            pl.BlockSpec((None, 2, tm, FF_SHARD), lambda i, j: (first(j), 0, i, 0)),
            pl.BlockSpec((None, D_MODEL, FF_SHARD), lambda i, j: (second(j), 0, 0)),
            pl.BlockSpec((None, D_MODEL, FF_SHARD), lambda i, j: (second(j) + FF_CHUNKS, 0, 0)),
            pl.BlockSpec((2, FF_SHARD // 2, D_MODEL), lambda i, j: (first(j), 0, 0)),
        ],
        out_specs=[
            pl.BlockSpec((tm, D_MODEL), lambda i, j: (i, 0)),
            pl.BlockSpec((None, 2, tm, FF_SHARD), lambda i, j: (first(j), 0, i, 0)),
            pl.BlockSpec((None, tm, FF_SHARD), lambda i, j: (first(j), i, 0)),
            pl.BlockSpec((SUBLANES, D_MODEL), lambda i, j: (0, 0)),
            pl.BlockSpec((tm, D_MODEL), lambda i, j: (i, 0)),
        ],
        out_shape=[
            jax.ShapeDtypeStruct((n, D_MODEL), F32),
            jax.ShapeDtypeStruct((FF_CHUNKS, 2, n, FF_SHARD), BF16),
            jax.ShapeDtypeStruct((FF_CHUNKS, n, FF_SHARD), BF16),
            jax.ShapeDtypeStruct((SUBLANES, D_MODEL), F32),
            jax.ShapeDtypeStruct((n, D_MODEL), BF16),
        ],
        scratch_shapes=[pltpu.VMEM((2, 2, tm, FF_SHARD), BF16)],
        compiler_params=_cp("arbitrary", "arbitrary"),
    )(x, gain, dy, gu, w_in_ag, w_in_ag, w_out_ag)


def _ffn_dw_in(xn, dgu, tk, name, after=None):
    n = xn.shape[0]
    nk = n // tk
    deps = [] if after is None else [after]

    def body(a_ref, b_ref, *rest):
        o_ref, acc_ref = rest[-2:]
        k = pl.program_id(2)

        @pl.when(k == 0)
        def _():
            acc_ref[...] = jnp.zeros_like(acc_ref)

        acc_ref[...] += _dot_tn(b_ref[...], a_ref[...])

        @pl.when(k == nk - 1)
        def _():
            o_ref[...] = acc_ref[...].astype(BF16)

    return pl.pallas_call(
        body, name=name, grid=(FF_CHUNKS, 2, nk),
        in_specs=[
            pl.BlockSpec((tk, D_MODEL), lambda j, p, k: (k, 0)),
            pl.BlockSpec((None, None, tk, FF_SHARD), lambda j, p, k: (j, p, k, 0)),
        ] + [pl.BlockSpec(memory_space=pl.ANY)] * len(deps),
        out_specs=pl.BlockSpec((None, FF_SHARD, D_MODEL), lambda j, p, k: (FF_CHUNKS * p + j, 0, 0)),
        out_shape=jax.ShapeDtypeStruct((N_DEV, FF_SHARD, D_MODEL), BF16),
        scratch_shapes=[pltpu.VMEM((FF_SHARD, D_MODEL), F32)],
        compiler_params=_cp("parallel", "parallel", "arbitrary"),
    )(xn, dgu, *deps)


def _ffn_dw_out(act, dyb, tk, name, after=None):
    n = act.shape[1]
    nk = n // tk
    deps = [] if after is None else [after]

    def body(a_ref, b_ref, *rest):
        o_ref, acc_ref = rest[-2:]
        k = pl.program_id(1)

        @pl.when(k == 0)
        def _():
            acc_ref[...] = jnp.zeros_like(acc_ref)

        acc_ref[...] += _dot_tn(a_ref[...], b_ref[...])

        @pl.when(k == nk - 1)
        def _():
            o_ref[...] = acc_ref[...].astype(BF16)

    return pl.pallas_call(
        body, name=name, grid=(FF_CHUNKS, nk),
        in_specs=[
            pl.BlockSpec((None, tk, FF_SHARD), lambda j, k: (j, k, 0)),
            pl.BlockSpec((tk, D_MODEL), lambda j, k: (k, 0)),
        ] + [pl.BlockSpec(memory_space=pl.ANY)] * len(deps),
        out_specs=pl.BlockSpec((None, FF_SHARD, D_MODEL), lambda j, k: (j, 0, 0)),
        out_shape=jax.ShapeDtypeStruct((FF_CHUNKS, FF_SHARD, D_MODEL), BF16),
        scratch_shapes=[pltpu.VMEM((FF_SHARD, D_MODEL), F32)],
        compiler_params=_cp("parallel", "arbitrary"),
    )(act, dyb, *deps)


def _mix_in_fwd(x, gain, w, tm, name):
    n = x.shape[0]

    def body(x_ref, g_ref, w_ref, z_ref):
        y, _, _ = _rms_fwd(x_ref[...], g_ref[...])
        z_ref[...] = _dot(y.astype(BF16), w_ref[...])

    return pl.pallas_call(
        body, name=name, grid=(n // tm,),
        in_specs=[
            pl.BlockSpec((tm, D_MODEL), lambda i: (i, 0)),
            pl.BlockSpec((1, D_MODEL), lambda i: (0, 0)),
            pl.BlockSpec((D_MODEL, IN_COLS), lambda i: (0, 0)),
        ],
        out_specs=pl.BlockSpec((tm, IN_COLS), lambda i: (i, 0)),
        out_shape=jax.ShapeDtypeStruct((n, IN_COLS), F32),
        compiler_params=_cp("parallel"),
    )(x, gain, w)


def _mix_in_bwd(x, gain, du_ssm, du_gm, dv_gm, d_res, w, tm, name):
    n = x.shape[0]

    def body(x_ref, g_ref, d0_ref, d1_ref, d2_ref, dres_ref, w_ref, dx_ref, dw_ref, dgain_ref):
        i = pl.program_id(0)

        @pl.when(i == 0)
        def _():
            dw_ref[...] = jnp.zeros_like(dw_ref)
            dgain_ref[...] = jnp.zeros_like(dgain_ref)

        g = g_ref[...]
        y, xh, r = _rms_fwd(x_ref[...], g)
        xn = y.astype(BF16)
        dxn = jnp.zeros((tm, D_MODEL), F32)
        for k, d_ref in enumerate((d0_ref, d1_ref, d2_ref)):
            dz = d_ref[...].astype(BF16)
            cols = slice(k * SSM_WIDTH, (k + 1) * SSM_WIDTH)
            dxn += _dot_nt(dz, w_ref[:, cols])
            dw_ref[cols, :] += _dot_tn(dz, xn)
        dx, dgr = _rms_bwd(dxn, xh, r, g)
        dx_ref[...] = dres_ref[...] + dx
        dgain_ref[...] += _rows8(dgr)

    row = lambda i: (i, 0)
    fixed = lambda i: (0, 0)
    return pl.pallas_call(
        body, name=name, grid=(n // tm,),
        in_specs=[
            pl.BlockSpec((tm, D_MODEL), row),
            pl.BlockSpec((1, D_MODEL), fixed),
            pl.BlockSpec((tm, SSM_WIDTH), row),
            pl.BlockSpec((tm, GM_WIDTH), row),
            pl.BlockSpec((tm, GM_WIDTH), row),
            pl.BlockSpec((tm, D_MODEL), row),
            pl.BlockSpec((D_MODEL, IN_COLS), fixed),
        ],
        out_specs=[
            pl.BlockSpec((tm, D_MODEL), row),
            pl.BlockSpec((IN_COLS, D_MODEL), fixed),
            pl.BlockSpec((SUBLANES, D_MODEL), fixed),
        ],
        out_shape=[
            jax.ShapeDtypeStruct((n, D_MODEL), F32),
            jax.ShapeDtypeStruct((IN_COLS, D_MODEL), F32),
            jax.ShapeDtypeStruct((SUBLANES, D_MODEL), F32),
        ],
        compiler_params=_cp("arbitrary"),
    )(x, gain, du_ssm, du_gm, dv_gm, d_res, w)


def _mix_out_fwd(y_ssm, y_gm, g_ssm, g_gm, w, x, tm, name):
    n = x.shape[0]

    def body(ys_ref, yg_ref, gs_ref, gg_ref, w_ref, x_ref, o_ref):
        a, _, _ = _rms_fwd(ys_ref[...], gs_ref[...])
        b, _, _ = _rms_fwd(yg_ref[...], gg_ref[...])
        o_ref[...] = (x_ref[...] + _dot(a.astype(BF16), w_ref[0:SSM_WIDTH, :])
                      + _dot(b.astype(BF16), w_ref[SSM_WIDTH:D_MODEL, :]))

    row = lambda i: (i, 0)
    fixed = lambda i: (0, 0)
    return pl.pallas_call(
        body, name=name, grid=(n // tm,),
        in_specs=[
            pl.BlockSpec((tm, SSM_WIDTH), row), pl.BlockSpec((tm, GM_WIDTH), row),
            pl.BlockSpec((1, SSM_WIDTH), fixed), pl.BlockSpec((1, GM_WIDTH), fixed),
            pl.BlockSpec((D_MODEL, D_MODEL), fixed), pl.BlockSpec((tm, D_MODEL), row),
        ],
        out_specs=pl.BlockSpec((tm, D_MODEL), row),
        out_shape=jax.ShapeDtypeStruct((n, D_MODEL), F32),
        compiler_params=_cp("parallel"),
    )(y_ssm, y_gm, g_ssm, g_gm, w, x)


def _mix_out_bwd(y_ssm, y_gm, g_ssm, g_gm, w, dx, tm, name):
    n = dx.shape[0]

    def body(ys_ref, yg_ref, gs_ref, gg_ref, w_ref, dx_ref, dys_ref, dyg_ref, dw_ref, dgs_ref, dgg_ref):
        i = pl.program_id(0)

        @pl.when(i == 0)
        def _():
            dw_ref[...] = jnp.zeros_like(dw_ref)
            dgs_ref[...] = jnp.zeros_like(dgs_ref)
            dgg_ref[...] = jnp.zeros_like(dgg_ref)

        dxb = dx_ref[...].astype(BF16)
        parts = ((ys_ref, gs_ref, dys_ref, dgs_ref, 0), (yg_ref, gg_ref, dyg_ref, dgg_ref, SSM_WIDTH))
        for y_ref, g_ref, dy_ref, dg_ref, off in parts:
            g = g_ref[...]
            yn, xh, r = _rms_fwd(y_ref[...], g)
            rows = slice(off, off + SSM_WIDTH)
            dyn = _dot_nt(dxb, w_ref[rows, :])
            dw_ref[rows, :] += _dot_tn(yn.astype(BF16), dxb)
            dy, dgr = _rms_bwd(dyn, xh, r, g)
            dy_ref[...] = dy
            dg_ref[...] += _rows8(dgr)

    row = lambda i: (i, 0)
    fixed = lambda i: (0, 0)
    return pl.pallas_call(
        body, name=name, grid=(n // tm,),
        in_specs=[
            pl.BlockSpec((tm, SSM_WIDTH), row), pl.BlockSpec((tm, GM_WIDTH), row),
            pl.BlockSpec((1, SSM_WIDTH), fixed), pl.BlockSpec((1, GM_WIDTH), fixed),
            pl.BlockSpec((D_MODEL, D_MODEL), fixed), pl.BlockSpec((tm, D_MODEL), row),
        ],
        out_specs=[
            pl.BlockSpec((tm, SSM_WIDTH), row), pl.BlockSpec((tm, GM_WIDTH), row),
            pl.BlockSpec((D_MODEL, D_MODEL), fixed),
            pl.BlockSpec((SUBLANES, SSM_WIDTH), fixed), pl.BlockSpec((SUBLANES, GM_WIDTH), fixed),
        ],
        out_shape=[
            jax.ShapeDtypeStruct((n, SSM_WIDTH), F32), jax.ShapeDtypeStruct((n, GM_WIDTH), F32),
            jax.ShapeDtypeStruct((D_MODEL, D_MODEL), F32),
            jax.ShapeDtypeStruct((SUBLANES, SSM_WIDTH), F32), jax.ShapeDtypeStruct((SUBLANES, GM_WIDTH), F32),
        ],
        compiler_params=_cp("arbitrary"),
    )(y_ssm, y_gm, g_ssm, g_gm, w, dx)


SCAN_W = 512
SCAN_PIECES = HALF_ST // SCAN_W


def _scan_tiles(src_ref, dst_ref, dst_off, c_ref, half, carry_ref, n_tiles, reverse, extra=None):
    shifts = (1, 2, 4)
    carry_row = 0 if reverse else SUBLANES - 1

    def cols(piece, im):
        lo = im * HALF_ST + piece * SCAN_W
        return slice(lo, lo + SCAN_W)

    def step(t, state):
        carries, accs = state
        k = (n_tiles - 1 - t) if reverse else t
        rows = slice(k * SUBLANES, (k + 1) * SUBLANES)
        new_carries, new_accs = [], []
        for piece in range(SCAN_PIECES):
            cr, ci = carries[piece]
            xr0 = src_ref[rows, cols(piece, 0)]
            xi0 = src_ref[rows, cols(piece, 1)]
            xr, xi = xr0, xi0
            for si, s in enumerate(shifts):
                ar = c_ref[half, si, :, cols(piece, 0)]
                ai = c_ref[half, si, :, cols(piece, 1)]
                sh = (SUBLANES - s) if reverse else s
                sr = pltpu.roll(xr, sh, 0)
                sm = pltpu.roll(xi, sh, 0)
                xr, xi = xr + (ar * sr - ai * sm), xi + (ar * sm + ai * sr)
            pr = c_ref[half, 3, :, cols(piece, 0)]
            pi = c_ref[half, 3, :, cols(piece, 1)]
            hr = xr + (pr * cr - pi * ci)
            hi = xi + (pr * ci + pi * cr)
            dst_ref[rows, pl.ds(dst_off + piece * SCAN_W, SCAN_W)] = hr
            dst_ref[rows, pl.ds(dst_off + HALF_ST + piece * SCAN_W, SCAN_W)] = hi
            new_carries.append((jnp.broadcast_to(hr[carry_row:carry_row + 1, :], (SUBLANES, SCAN_W)),
                                jnp.broadcast_to(hi[carry_row:carry_row + 1, :], (SUBLANES, SCAN_W))))
            if extra is not None:
                new_accs.append(extra(rows, piece, (xr0, xi0), (hr, hi), accs[piece]))
        return tuple(new_carries), tuple(new_accs)

    base = half * 2 * HALF_ST
    carries0 = tuple((carry_ref[:, pl.ds(base + p * SCAN_W, SCAN_W)],
                      carry_ref[:, pl.ds(base + HALF_ST + p * SCAN_W, SCAN_W)]) for p in range(SCAN_PIECES))
    zero = jnp.zeros((SUBLANES, SCAN_W), F32)
    accs0 = tuple((zero, zero) for _ in range(SCAN_PIECES)) if extra is not None else ()
    state = (carries0, accs0)
    for t in range(n_tiles):
        state = step(t, state)
    carries, accs = state
    for p in range(SCAN_PIECES):
        carry_ref[:, pl.ds(base + p * SCAN_W, SCAN_W)] = carries[p][0]
        carry_ref[:, pl.ds(base + HALF_ST + p * SCAN_W, SCAN_W)] = carries[p][1]
    return accs


def _ssm_tail(hb, u, c_ref, glu_ref, glub_ref, dskip_ref):
    ypre = u * dskip_ref[...]
    parts = []
    for half in range(2):
        parts.append(_dot(hb[half], c_ref[half]))
    ypre = ypre + jnp.concatenate(parts, axis=1)
    yg, th = _gelu(ypre)
    zz = _dot(yg.astype(BF16), glu_ref[...]) + glub_ref[...]
    z1, z2 = zz[:, :SSM_WIDTH], zz[:, SSM_WIDTH:]
    sg = jax.nn.sigmoid(z2)
    return ypre, th, yg, z1, sg


def _ssm_fwd(z, bblk, cblk, glu, glub, dskip, fwdc, batch, t_chunk, name):
    n = z.shape[0]
    nk = n // batch // t_chunk
    n_tiles = t_chunk // SUBLANES

    def body(u_ref, b_ref, c_ref, glu_ref, glub_ref, dskip_ref, k_ref, y_ref, h_ref, bu_ref, carry_ref):
        @pl.when(pl.program_id(1) == 0)
        def _():
            carry_ref[...] = jnp.zeros_like(carry_ref)

        u = u_ref[...]
        ub = u.astype(BF16)
        for half in range(2):
            bu_ref[half] = _dot(ub[:, half * HALF_IN:(half + 1) * HALF_IN], b_ref[half])
            _scan_tiles(bu_ref.at[half], h_ref, half * 2 * HALF_ST, k_ref, half, carry_ref, n_tiles, False)
        hb = [h_ref[:, half * 2 * HALF_ST:(half + 1) * 2 * HALF_ST].astype(BF16) for half in range(2)]
        _, _, _, z1, sg = _ssm_tail(hb, u, c_ref, glu_ref, glub_ref, dskip_ref)
        y_ref[...] = z1 * sg

    fixed2 = lambda b, k: (0, 0)
    fixed3 = lambda b, k: (0, 0, 0)
    row = lambda b, k: (b * nk + k, 0)
    return pl.pallas_call(
        body, name=name, grid=(batch, nk),
        in_specs=[
            pl.BlockSpec((t_chunk, SSM_WIDTH), row),
            pl.BlockSpec((2, HALF_IN, 2 * HALF_ST), fixed3),
            pl.BlockSpec((2, 2 * HALF_ST, HALF_IN), fixed3),
            pl.BlockSpec((SSM_WIDTH, 2 * SSM_WIDTH), fixed2),
            pl.BlockSpec((1, 2 * SSM_WIDTH), fixed2),
            pl.BlockSpec((1, SSM_WIDTH), fixed2),
            pl.BlockSpec((2, 4, SUBLANES, 2 * HALF_ST), lambda b, k: (0, 0, 0, 0)),
        ],
        out_specs=[pl.BlockSpec((t_chunk, SSM_WIDTH), row), pl.BlockSpec((t_chunk, 4 * HALF_ST), row)],
        out_shape=[jax.ShapeDtypeStruct((n, SSM_WIDTH), F32), jax.ShapeDtypeStruct((n, 4 * HALF_ST), F32)],
        scratch_shapes=[pltpu.VMEM((2, t_chunk, 2 * HALF_ST), F32), pltpu.VMEM((SUBLANES, 4 * HALF_ST), F32)],
        compiler_params=_cp("parallel", "arbitrary"),
    )(z, bblk, cblk, glu, glub, dskip, fwdc)


def _ssm_bwd(z, h, dy, bblk, cblk, glu, glub, dskip, revc, batch, t_chunk, name):
    n = z.shape[0]
    nk = n // batch // t_chunk
    n_tiles = t_chunk // SUBLANES

    def body(u_ref, h_ref, dy_ref, b_ref, c_ref, glu_ref, glub_ref, dskip_ref, k_ref,
             du_ref, dglu_ref, dglub_ref, ddskip_ref, dct_ref, db_ref, q_ref, g_ref, carry_ref):
        first = jnp.logical_and(pl.program_id(0) == 0, pl.program_id(1) == 0)

        @pl.when(first)
        def _():
            for r in (dglu_ref, dglub_ref, ddskip_ref, dct_ref, db_ref, q_ref):
                r[...] = jnp.zeros_like(r)

        @pl.when(pl.program_id(1) == 0)
        def _():
            carry_ref[...] = jnp.zeros_like(carry_ref)

        u = u_ref[...]
        ub = u.astype(BF16)
        hb = [h_ref[:, half * 2 * HALF_ST:(half + 1) * 2 * HALF_ST].astype(BF16) for half in range(2)]
        ypre, th, yg, z1, sg = _ssm_tail(hb, u, c_ref, glu_ref, glub_ref, dskip_ref)
        dout = dy_ref[...]
        dz = jnp.concatenate([dout * sg, dout * z1 * sg * (1.0 - sg)], axis=1)
        dzb = dz.astype(BF16)
        dglu_ref[...] += _dot_tn(yg.astype(BF16), dzb)
        dglub_ref[...] += _rows8(dz)
        dypre = _dot_nt(dzb, glu_ref[...]) * _gelu_grad(ypre, th)
        ddskip_ref[...] += _rows8(dypre * u)
        dypb = dypre.astype(BF16)
        du_parts = []
        for half in range(2):
            dyp_h = dypb[:, half * HALF_IN:(half + 1) * HALF_IN]
            dct_ref[half] += _dot_tn(dyp_h, hb[half])
            g_ref[half] = _dot_nt(dyp_h, c_ref[half])

            def extra(rows, piece, x_in, g_out, acc, half=half):
                er, ei = g_out[0] - x_in[0], g_out[1] - x_in[1]
                base = half * 2 * HALF_ST + piece * SCAN_W
                hr = h_ref[rows, pl.ds(base, SCAN_W)]
                hi = h_ref[rows, pl.ds(base + HALF_ST, SCAN_W)]
                return acc[0] + (er * hr + ei * hi), acc[1] + (er * hi - ei * hr)

            accs = _scan_tiles(g_ref.at[half], g_ref.at[half], 0, k_ref, half, carry_ref, n_tiles, True, extra)
            for piece in range(SCAN_PIECES):
                base = half * 2 * HALF_ST + piece * SCAN_W
                q_ref[:, pl.ds(base, SCAN_W)] += accs[piece][0]
                q_ref[:, pl.ds(base + HALF_ST, SCAN_W)] += accs[piece][1]
            gb = g_ref[half].astype(BF16)
            db_ref[half] += _dot_tn(ub[:, half * HALF_IN:(half + 1) * HALF_IN], gb)
            du_parts.append(_dot_nt(gb, b_ref[half]))
        du_ref[...] = dypre * dskip_ref[...] + jnp.concatenate(du_parts, axis=1)

    fixed2 = lambda b, k: (0, 0)
    fixed3 = lambda b, k: (0, 0, 0)
    row = lambda b, k: (b * nk + (nk - 1 - k), 0)
    return pl.pallas_call(
        body, name=name, grid=(batch, nk),
        in_specs=[
            pl.BlockSpec((t_chunk, SSM_WIDTH), row),
            pl.BlockSpec((t_chunk, 4 * HALF_ST), row),
            pl.BlockSpec((t_chunk, SSM_WIDTH), row),
            pl.BlockSpec((2, HALF_IN, 2 * HALF_ST), fixed3),
            pl.BlockSpec((2, 2 * HALF_ST, HALF_IN), fixed3),
            pl.BlockSpec((SSM_WIDTH, 2 * SSM_WIDTH), fixed2),
            pl.BlockSpec((1, 2 * SSM_WIDTH), fixed2),
            pl.BlockSpec((1, SSM_WIDTH), fixed2),
            pl.BlockSpec((2, 4, SUBLANES, 2 * HALF_ST), lambda b, k: (0, 0, 0, 0)),
        ],
        out_specs=[
            pl.BlockSpec((t_chunk, SSM_WIDTH), row),
            pl.BlockSpec((SSM_WIDTH, 2 * SSM_WIDTH), fixed2),
            pl.BlockSpec((SUBLANES, 2 * SSM_WIDTH), fixed2),
            pl.BlockSpec((SUBLANES, SSM_WIDTH), fixed2),
            pl.BlockSpec((2, HALF_IN, 2 * HALF_ST), fixed3),
            pl.BlockSpec((2, HALF_IN, 2 * HALF_ST), fixed3),
            pl.BlockSpec((SUBLANES, 4 * HALF_ST), fixed2),
        ],
        out_shape=[
            jax.ShapeDtypeStruct((n, SSM_WIDTH), F32),
            jax.ShapeDtypeStruct((SSM_WIDTH, 2 * SSM_WIDTH), F32),
            jax.ShapeDtypeStruct((SUBLANES, 2 * SSM_WIDTH), F32),
            jax.ShapeDtypeStruct((SUBLANES, SSM_WIDTH), F32),
            jax.ShapeDtypeStruct((2, HALF_IN, 2 * HALF_ST), F32),
            jax.ShapeDtypeStruct((2, HALF_IN, 2 * HALF_ST), F32),
            jax.ShapeDtypeStruct((SUBLANES, 4 * HALF_ST), F32),
        ],
        scratch_shapes=[pltpu.VMEM((2, t_chunk, 2 * HALF_ST), F32), pltpu.VMEM((SUBLANES, 4 * HALF_ST), F32)],
        compiler_params=_cp("arbitrary", "arbitrary"),
    )(z, h, dy, bblk, cblk, glu, glub, dskip, revc)


def _gm_chunk_fwd(u, v, gain_ref, w_ref, bias_ref):
    ug, thu = _gelu(u)
    vg, thv = _gelu(v)
    rs, vns, ss = [], [], []
    for hh in range(GM_HEADS):
        cs = slice(hh * GM_HEAD_DIM, (hh + 1) * GM_HEAD_DIM)
        vn, _, r = _rms_fwd(vg[:, cs], gain_ref[:, cs])
        s = _dot(w_ref[hh], vn.astype(BF16)) + bias_ref[:, cs]
        rs.append(r)
        vns.append(vn)
        ss.append(s)
    return ug, thu, thv, vg, rs, vns, ss


def _gm_fwd(z, gain, w_tril, bias, rows, name):
    n = z.shape[0]
    chunks = rows // GM_CHUNK

    def body(u_ref, v_ref, gain_ref, w_ref, bias_ref, y_ref):
        for c in range(chunks):
            rs_ = slice(c * GM_CHUNK, (c + 1) * GM_CHUNK)
            ug, _, _, _, _, _, ss = _gm_chunk_fwd(u_ref[rs_, :], v_ref[rs_, :], gain_ref, w_ref, bias_ref)
            y_ref[rs_, :] = ug * jnp.concatenate(ss, axis=1)

    return pl.pallas_call(
        body, name=name, grid=(n // rows,),
        in_specs=[
            pl.BlockSpec((rows, GM_WIDTH), lambda i: (i, 1)),
            pl.BlockSpec((rows, GM_WIDTH), lambda i: (i, 2)),
            pl.BlockSpec((1, GM_WIDTH), lambda i: (0, 0)),
            pl.BlockSpec((GM_HEADS, GM_CHUNK, GM_CHUNK), lambda i: (0, 0, 0)),
            pl.BlockSpec((GM_CHUNK, GM_WIDTH), lambda i: (0, 0)),
        ],
        out_specs=pl.BlockSpec((rows, GM_WIDTH), lambda i: (i, 0)),
        out_shape=jax.ShapeDtypeStruct((n, GM_WIDTH), F32),
        compiler_params=_cp("parallel"),
    )(z, z, gain, w_tril, bias)


def _gm_bwd(z, dy, gain, w_tril, bias, rows, name):
    n = z.shape[0]
    chunks = rows // GM_CHUNK

    def body(u_ref, v_ref, dy_ref, gain_ref, w_ref, bias_ref, du_ref, dv_ref, dw_ref, dbias_ref, dgain_ref):
        @pl.when(pl.program_id(0) == 0)
        def _():
            dw_ref[...] = jnp.zeros_like(dw_ref)
            dbias_ref[...] = jnp.zeros_like(dbias_ref)
            dgain_ref[...] = jnp.zeros_like(dgain_ref)

        for c in range(chunks):
            rs_ = slice(c * GM_CHUNK, (c + 1) * GM_CHUNK)
            u, v = u_ref[rs_, :], v_ref[rs_, :]
            ug, thu, thv, vg, rs, vns, ss = _gm_chunk_fwd(u, v, gain_ref, w_ref, bias_ref)
            dout = dy_ref[rs_, :]
            ds = dout * ug
            du_ref[rs_, :] = dout * jnp.concatenate(ss, axis=1) * _gelu_grad(u, thu)
            dbias_ref[...] += ds
            dvg_parts, dgain_parts = [], []
            for hh in range(GM_HEADS):
                cs = slice(hh * GM_HEAD_DIM, (hh + 1) * GM_HEAD_DIM)
                dsb = ds[:, cs].astype(BF16)
                dvn = _dot_tn(w_ref[hh], dsb)
                dw_ref[hh] += _dot_nt(dsb, vns[hh].astype(BF16))
                g = gain_ref[:, cs]
                xh = vg[:, cs] * rs[hh]
                dvg, dgr = _rms_bwd(dvn, xh, rs[hh], g)
                dvg_parts.append(dvg)
                dgain_parts.append(dgr)
            dv_ref[rs_, :] = jnp.concatenate(dvg_parts, axis=1) * _gelu_grad(v, thv)
            dgain_ref[...] += _rows8(jnp.concatenate(dgain_parts, axis=1))

    row = lambda i: (i, 0)
    return pl.pallas_call(
        body, name=name, grid=(n // rows,),
        in_specs=[
            pl.BlockSpec((rows, GM_WIDTH), lambda i: (i, 1)),
            pl.BlockSpec((rows, GM_WIDTH), lambda i: (i, 2)),
            pl.BlockSpec((rows, GM_WIDTH), row),
            pl.BlockSpec((1, GM_WIDTH), lambda i: (0, 0)),
            pl.BlockSpec((GM_HEADS, GM_CHUNK, GM_CHUNK), lambda i: (0, 0, 0)),
            pl.BlockSpec((GM_CHUNK, GM_WIDTH), lambda i: (0, 0)),
        ],
        out_specs=[
            pl.BlockSpec((rows, GM_WIDTH), row), pl.BlockSpec((rows, GM_WIDTH), row),
            pl.BlockSpec((GM_HEADS, GM_CHUNK, GM_CHUNK), lambda i: (0, 0, 0)),
            pl.BlockSpec((GM_CHUNK, GM_WIDTH), lambda i: (0, 0)),
            pl.BlockSpec((SUBLANES, GM_WIDTH), lambda i: (0, 0)),
        ],
        out_shape=[
            jax.ShapeDtypeStruct((n, GM_WIDTH), F32), jax.ShapeDtypeStruct((n, GM_WIDTH), F32),
            jax.ShapeDtypeStruct((GM_HEADS, GM_CHUNK, GM_CHUNK), F32),
            jax.ShapeDtypeStruct((GM_CHUNK, GM_WIDTH), F32),
            jax.ShapeDtypeStruct((SUBLANES, GM_WIDTH), F32),
        ],
        compiler_params=_cp("arbitrary"),
    )(z, z, dy, gain, w_tril, bias)


def _loss_head(x, gain, target, tm, name):
    n = x.shape[0]

    def body(x_ref, g_ref, t_ref, dx_ref, sq_ref, dgain_ref):
        @pl.when(pl.program_id(0) == 0)
        def _():
            sq_ref[...] = jnp.zeros_like(sq_ref)
            dgain_ref[...] = jnp.zeros_like(dgain_ref)

        g = g_ref[...]
        y, xh, r = _rms_fwd(x_ref[...], g)
        err = y - t_ref[...]
        sq_ref[...] += _rows8(err * err)
        dx, dgr = _rms_bwd(err * (1.0 / D_MODEL), xh, r, g)
        dx_ref[...] = dx
        dgain_ref[...] += _rows8(dgr)

    row = lambda i: (i, 0)
    fixed = lambda i: (0, 0)
    return pl.pallas_call(
        body, name=name, grid=(n // tm,),
        in_specs=[pl.BlockSpec((tm, D_MODEL), row), pl.BlockSpec((1, D_MODEL), fixed), pl.BlockSpec((tm, D_MODEL), row)],
        out_specs=[pl.BlockSpec((tm, D_MODEL), row), pl.BlockSpec((SUBLANES, D_MODEL), fixed),
                   pl.BlockSpec((SUBLANES, D_MODEL), fixed)],
        out_shape=[jax.ShapeDtypeStruct((n, D_MODEL), F32), jax.ShapeDtypeStruct((SUBLANES, D_MODEL), F32),
                   jax.ShapeDtypeStruct((SUBLANES, D_MODEL), F32)],
        compiler_params=_cp("arbitrary"),
    )(x, gain, target)


def _adam_math(w, g, m, v):
    m2 = ADAM_B1 * m + (1.0 - ADAM_B1) * g
    v2 = ADAM_B2 * v + (1.0 - ADAM_B2) * (g * g)
    m_hat = m2 / (1.0 - ADAM_B1 ** ADAM_STEP)
    v_hat = v2 / (1.0 - ADAM_B2 ** ADAM_STEP)
    delta = -ADAM_LR * (m_hat / (jnp.sqrt(v_hat) + ADAM_EPS) + ADAM_WD * w)
    return delta, m2, v2


def _adam_sharded(parts, w, m, v, layer, earlier, name):
    depth, r, c = w.shape
    tr = max(t for t in range(16, 129, 16) if r % t == 0)

    def body(p_ref, w_ref, m_ref, v_ref, *rest):
        g_ref, d_ref, m2_ref, v2_ref = rest[-4:]
        g = p_ref[0].astype(F32)
        for s in range(1, N_DEV):
            g = g + p_ref[s].astype(F32)
        delta, m2, v2 = _adam_math(w_ref[...], g, m_ref[...], v_ref[...])
        g_ref[...] = g
        d_ref[...] = delta
        m2_ref[...] = m2
        v2_ref[...] = v2

    blk = pl.BlockSpec((None, tr, c), lambda i: (layer, i, 0))
    extra = [] if earlier is None else list(earlier)
    return pl.pallas_call(
        body, name=name, grid=(r // tr,),
        in_specs=[pl.BlockSpec((N_DEV, tr, c), lambda i: (0, i, 0)), blk, blk, blk]
        + [pl.BlockSpec(memory_space=pl.ANY)] * len(extra),
        out_specs=[blk, blk, blk, blk],
        out_shape=[jax.ShapeDtypeStruct((depth, r, c), F32)] * 4,
        input_output_aliases={4 + i: i for i in range(len(extra))},
        compiler_params=_cp("parallel"),
    )(parts, w, m, v, *extra)


def _adam_packed(g, w, m, v, name):
    r, c = g.shape

    def body(g_ref, w_ref, m_ref, v_ref, d_ref, m2_ref, v2_ref):
        delta, m2, v2 = _adam_math(w_ref[...], g_ref[...], m_ref[...], v_ref[...])
        d_ref[...] = delta
        m2_ref[...] = m2
        v2_ref[...] = v2

    blk = pl.BlockSpec((r, c), lambda i: (0, 0))
    return pl.pallas_call(
        body, name=name, grid=(1,),
        in_specs=[blk, blk, blk, blk], out_specs=[blk, blk, blk],
        out_shape=[jax.ShapeDtypeStruct((r, c), F32)] * 3,
        compiler_params=_cp("arbitrary"),
    )(g, w, m, v)


def _my_place():
    return lax.axis_index("x"), lax.axis_index("y"), lax.axis_index("c")


def _flip(place, rel):
    x, y, c = place
    return (1 - x if rel & 4 else x, 1 - y if rel & 2 else y, 1 - c if rel & 1 else c)


def _index(place):
    return 4 * place[0] + 2 * place[1] + place[2]


def _all_gather(shards, name):
    na = len(shards)

    def body(*refs):
        xs, outs = refs[:na], refs[na:2 * na]
        send_sems, recv_sems, local_sems = refs[2 * na:]
        me = _my_place()
        sibling = _flip(me, 1)
        chips = [_flip(me, 4), _flip(me, 2), _flip(me, 6)]

        def copy(a, k, block, to, src=None):
            slot = outs[a].at[_index(block)]
            return pltpu.make_async_remote_copy(
                src_ref=slot if src is None else src, dst_ref=slot,
                send_sem=send_sems.at[a, k], recv_sem=recv_sems.at[a, k],
                device_id=to, device_id_type=MESH)

        mine = [pltpu.make_async_copy(xs[a], outs[a].at[_index(me)], local_sems.at[a]) for a in range(na)]
        for cp in mine:
            cp.start()
        first = []
        for a in range(na):
            first.append(copy(a, 0, me, sibling, src=xs[a]))
            first += [copy(a, 1 + j, me, chip, src=xs[a]) for j, chip in enumerate(chips)]
        for cp in first:
            cp.start()
        passed = []
        for a in range(na):
            for j, chip in enumerate(chips):
                copy(a, 1 + j, chip, me).wait_recv()
                fwd = copy(a, 4 + j, chip, sibling)
                fwd.start()
                passed.append(fwd)
        for a in range(na):
            copy(a, 0, sibling, me).wait_recv()
            for j, chip in enumerate(chips):
                copy(a, 4 + j, _flip(chip, 1), me).wait_recv()
        for cp in first + passed:
            cp.wait_send()
        for cp in mine:
            cp.wait()

    hbm = pl.BlockSpec(memory_space=pl.ANY)
    return pl.pallas_call(
        body, name=name,
        in_specs=[hbm] * na, out_specs=[hbm] * na,
        out_shape=[jax.ShapeDtypeStruct((N_DEV,) + s.shape, s.dtype) for s in shards],
        scratch_shapes=[pltpu.SemaphoreType.DMA((na, 7)), pltpu.SemaphoreType.DMA((na, 7)),
                        pltpu.SemaphoreType.DMA((na,))],
    )(*shards)


_HBM = pl.BlockSpec(memory_space=pltpu.HBM)
_SEM = pl.BlockSpec(memory_space=pltpu.SEMAPHORE)
_EFFECT = pltpu.SideEffectType.DATAFLOW_SIDE_EFFECTING


def _exchange_copy(src_ref, land_ref, send_sems, recv_sems, a, rel, me, scatter, landed):
    peer = _flip(me, rel)
    src = src_ref.at[_index(peer)] if scatter else src_ref
    return pltpu.make_async_remote_copy(
        src_ref=src, dst_ref=land_ref.at[_index(peer if landed else me)],
        send_sem=send_sems.at[a * (N_DEV - 1) + rel - 1], recv_sem=recv_sems.at[a * (N_DEV - 1) + rel - 1],
        device_id=peer, device_id_type=MESH)


def _own_slot(data, me, scatter):
    if scatter:
        own = lax.dynamic_slice_in_dim(data, me, 1, axis=0)
        shape = data.shape
    else:
        own = data[None]
        shape = (N_DEV,) + data.shape
    start = (me,) + (0,) * (len(shape) - 1)
    return lax.dynamic_update_slice(lax.empty(shape, data.dtype), own, start)


def _exchange_start(groups, me, scatter, name, after=None):
    sizes = [len(g) for g in groups]
    srcs = [a for g in groups for a in g]
    lands = [_own_slot(a, me, scatter) for a in srcs]
    na, ng = len(srcs), len(groups)
    deps = [] if after is None else [after]

    def body(*refs):
        src_refs, land_refs = refs[:na], refs[na:2 * na]
        sems = refs[2 * na + len(deps):2 * na + len(deps) + 2 * ng]
        token = refs[-1]
        place = _my_place()
        a = 0
        for g, size in enumerate(sizes):
            for k in range(size):
                for rel in range(1, N_DEV):
                    _exchange_copy(src_refs[a], land_refs[a], sems[2 * g], sems[2 * g + 1], k, rel, place, scatter,
                                   False).start()
                a += 1
        token[...] = jnp.zeros_like(token)

    sem_shapes = [pltpu.SemaphoreType.DMA((size * (N_DEV - 1),)) for size in sizes for _ in range(2)]
    outs = pl.pallas_call(
        body, name=name,
        in_specs=[_HBM] * (2 * na) + [pl.BlockSpec(memory_space=pl.ANY)] * len(deps),
        out_specs=[_SEM] * (2 * ng) + [_HBM] * (2 * na) + [pl.BlockSpec(memory_space=pltpu.VMEM)],
        out_shape=sem_shapes + [pltpu.HBM(a.shape, a.dtype) for a in srcs + lands]
        + [jax.ShapeDtypeStruct((SUBLANES, LANES), F32)],
        input_output_aliases={i: 2 * ng + i for i in range(2 * na)},
        compiler_params=pltpu.CompilerParams(has_side_effects=_EFFECT),
    )(*[pltpu.with_memory_space_constraint(a, pltpu.HBM) for a in srcs + lands], *deps)
    sems, thru, token = outs[:2 * ng], outs[2 * ng:2 * ng + 2 * na], outs[-1]
    handles, a = [], 0
    for g, size in enumerate(sizes):
        handles.append((sems[2 * g], sems[2 * g + 1], thru[a:a + size], thru[na + a:na + a + size]))
        a += size
    return handles, token


def _exchange_wait(handle, after, scatter, name):
    send_sems, recv_sems, srcs, lands = handle
    na = len(srcs)

    def body(*refs):
        src_refs, land_refs = refs[:na], refs[na:2 * na]
        send_ref, recv_ref = refs[2 * na], refs[2 * na + 1]
        place = _my_place()
        for a in range(na):
            for rel in range(1, N_DEV):
                cp = _exchange_copy(src_refs[a], land_refs[a], send_ref, recv_ref, a, rel, place, scatter, True)
                cp.wait_send()
                cp.wait_recv()

    outs = pl.pallas_call(
        body, name=name,
        in_specs=[_HBM] * (2 * na) + [_SEM, _SEM, pl.BlockSpec(memory_space=pl.ANY)],
        out_specs=[_HBM] * (2 * na),
        out_shape=[pltpu.HBM(a.shape, a.dtype) for a in list(srcs) + list(lands)],
        input_output_aliases={i: i for i in range(2 * na)},
        compiler_params=pltpu.CompilerParams(has_side_effects=_EFFECT),
    )(*srcs, *lands, send_sems, recv_sems, after)
    return outs[na:]


def _behind(arr, token):
    return arr + token[0:1, 0:1]


def _all_reduce_small(g, name):
    _, r, c = g.shape

    def body(g_ref, o_ref, land_ref, red_ref, send1, recv1, send2, recv2):
        me = _my_place()
        idx = _index(me)

        def scatter(rel):
            peer = _flip(me, rel)
            return pltpu.make_async_remote_copy(
                src_ref=g_ref.at[_index(peer)], dst_ref=land_ref.at[idx],
                send_sem=send1.at[rel - 1], recv_sem=recv1.at[rel - 1], device_id=peer, device_id_type=MESH)

        def gather(rel):
            peer = _flip(me, rel)
            return pltpu.make_async_remote_copy(
                src_ref=red_ref, dst_ref=o_ref.at[idx],
                send_sem=send2.at[rel - 1], recv_sem=recv2.at[rel - 1], device_id=peer, device_id_type=MESH)

        for rel in range(1, N_DEV):
            scatter(rel).start()
        land_ref[idx] = g_ref[idx]
        for rel in range(1, N_DEV):
            scatter(rel).wait()
        acc = land_ref[0]
        for s in range(1, N_DEV):
            acc = acc + land_ref[s]
        red_ref[...] = acc
        for rel in range(1, N_DEV):
            gather(rel).start()
        o_ref[idx] = acc
        for rel in range(1, N_DEV):
            gather(rel).wait()

    vmem = pl.BlockSpec(memory_space=pltpu.VMEM)
    return pl.pallas_call(
        body, name=name,
        in_specs=[vmem], out_specs=vmem,
        out_shape=jax.ShapeDtypeStruct(g.shape, F32),
        scratch_shapes=[pltpu.VMEM(g.shape, F32), pltpu.VMEM((r, c), F32)]
        + [pltpu.SemaphoreType.DMA((N_DEV - 1,))] * 4,
        compiler_params=pltpu.CompilerParams(vmem_limit_bytes=VMEM_LIMIT),
    )(g)


def _ssm_discretize(a_re, a_im, log_dt, b_re, b_im):
    dt = jnp.exp(log_dt)[:, None]
    mag = jnp.exp(a_re * dt)
    lr, li = mag * jnp.cos(a_im * dt), mag * jnp.sin(a_im * dt)
    den = a_re * a_re + a_im * a_im
    qr = ((lr - 1.0) * a_re + li * a_im) / den
    qi = (li * a_re - (lr - 1.0) * a_im) / den
    bbr = qr[..., None] * b_re - qi[..., None] * b_im
    bbi = qr[..., None] * b_im + qi[..., None] * b_re
    return lr, li, bbr, bbi


def _halves(a):
    return a.reshape((2, HALF_GROUPS) + a.shape[1:])


def _block_diag_mask(g, r, c):
    rows = lax.broadcasted_iota(jnp.int32, (g * r, g * c), 0) // r
    cols = lax.broadcasted_iota(jnp.int32, (g * r, g * c), 1) // c
    return rows == cols


def _block_diag(blocks):
    g, r, c = blocks.shape
    spread = jnp.tile(jnp.eye(c, dtype=blocks.dtype), (1, g))
    full = jnp.dot(blocks.reshape(g * r, c), spread, precision=lax.Precision.HIGHEST)
    return jnp.where(_block_diag_mask(g, r, c), full, 0.0)


def _block_diag_take(dense, g, r, c):
    gather = jnp.tile(jnp.eye(c, dtype=dense.dtype), (g, 1))
    kept = jnp.where(_block_diag_mask(g, r, c), dense, 0.0)
    return jnp.dot(kept, gather, precision=lax.Precision.HIGHEST).reshape(g, r, c)


def _ssm_matrices(bbr, bbi, c_re, c_im, glu_w, glu_b, d_skip):
    bre, bim = _halves(jnp.swapaxes(bbr, 1, 2)), _halves(jnp.swapaxes(bbi, 1, 2))
    bblk = jnp.stack([jnp.concatenate([_block_diag(bre[h]), _block_diag(bim[h])], axis=1) for h in range(2)])
    cre, cim = _halves(jnp.swapaxes(c_re, 1, 2)), _halves(jnp.swapaxes(c_im, 1, 2))
    cblk = jnp.stack([jnp.concatenate([_block_diag(cre[h]), -_block_diag(cim[h])], axis=0) for h in range(2)])
    glu = jnp.concatenate([_block_diag(glu_w[:, :, :SSM_CH]), _block_diag(glu_w[:, :, SSM_CH:])], axis=1)
    glub = jnp.concatenate([glu_b[:, :SSM_CH].reshape(1, -1), glu_b[:, SSM_CH:].reshape(1, -1)], axis=1)
    return bblk.astype(BF16), cblk.astype(BF16), glu.astype(BF16), glub, d_skip.reshape(1, -1)


def _scan_constants(lr, li, reverse):
    if reverse:
        li = -li
    pows = [(lr, li)]
    for _ in range(SUBLANES - 1):
        pr, pi = pows[-1]
        pows.append((pr * lr - pi * li, pr * li + pi * lr))
    row = jnp.arange(SUBLANES)[:, None]

    def flat(a):
        return a.reshape(2, 1, HALF_ST)

    mats = []
    for s in (1, 2, 4):
        keep = (row + s <= SUBLANES - 1) if reverse else (row >= s)
        mats.append(tuple(jnp.where(keep[None], flat(p), 0.0) for p in pows[s - 1]))
    order = [SUBLANES - 1 - j for j in range(SUBLANES)] if reverse else list(range(SUBLANES))
    mats.append(tuple(jnp.concatenate([flat(pows[j][k]) for j in order], axis=1) for k in range(2)))
    return jnp.stack([jnp.concatenate([m[0], m[1]], axis=2) for m in mats], axis=1)


def _pack(arrs, rows):
    flat = jnp.concatenate([a.reshape(-1) for a in arrs])
    return jnp.pad(flat, (0, rows * LANES - flat.shape[0])).reshape(rows, LANES)


def _unpack(buf, like):
    flat = buf.reshape(-1)
    out, off = [], 0
    for a in like:
        out.append(flat[off:off + a.size].reshape(a.shape))
        off += a.size
    return out


SMALL = ("norm_ffn1", "norm_mix", "ssm_a_re", "ssm_a_im", "ssm_log_dt", "ssm_b_re", "ssm_b_im", "ssm_c_re",
         "ssm_c_im", "ssm_d", "ssm_glu_w", "ssm_glu_b", "gm_v_gain", "gm_w_s", "gm_b_s", "gain_ssm_out",
         "gain_gm_out", "norm_ffn2", "norm_final")
BIG = ("ffn1_w_in", "ffn1_w_out", "mix_w_in", "mix_w_out", "ffn2_w_in", "ffn2_w_out")
TRANSPOSED = ("ffn1_w_in", "mix_w_in", "ffn2_w_in")
WEIGHTS = ("norm_ffn1", "ffn1_w_in", "ffn1_w_out", "norm_mix", "mix_w_in", "ssm_a_re", "ssm_a_im", "ssm_log_dt",
           "ssm_b_re", "ssm_b_im", "ssm_c_re", "ssm_c_im", "ssm_d", "ssm_glu_w", "ssm_glu_b", "gm_v_gain", "gm_w_s",
           "gm_b_s", "gain_ssm_out", "gain_gm_out", "mix_w_out", "norm_ffn2", "ffn2_w_in", "ffn2_w_out", "norm_final")


def _step(x, target, w, m, v):
    batch, seq, _ = x.shape
    n = batch * seq
    depth = w["norm_ffn1"].shape[0]
    tm = min(512, n)
    tm_ffn = min(1024, n)
    tk = min(4096, n)
    t_chunk = min(256, seq)
    gm_rows = min(512, seq)
    x = x.reshape(n, D_MODEL)
    target = target.reshape(n, D_MODEL)

    assert depth == 2
    me = _index(_my_place())
    shard = lambda group, l: [w[f"{group}_w_in"][l].astype(BF16), w[f"{group}_w_out"][l].astype(BF16)]
    batches = ([("mix", 0), ("ffn2", 0)], [("ffn1", 1), ("mix", 1)], [("ffn2", 1)])
    gathered, pending = {("ffn1", 0): tuple(_all_gather(shard("ffn1", 0), "all_gather_first"))}, {}

    def gather_start(i, after):
        handles, tok = _exchange_start([shard(g, l) for g, l in batches[i]], me, False, f"all_gather_start_{i}",
                                       after)
        pending.update(zip(batches[i], handles))
        return tok

    def weights(group, l, after=None):
        if (group, l) not in gathered:
            w_in, w_out = _exchange_wait(pending[(group, l)], after, False, f"all_gather_wait_{group}_{l}")
            if group == "mix":
                w_in = jnp.transpose(w_in, (1, 0, 2)).reshape(D_MODEL, IN_COLS)
                w_out = w_out.reshape(D_MODEL, D_MODEL)
            gathered[(group, l)] = (w_in, w_out)
        return gathered[(group, l)]

    tril = jnp.tril(jnp.ones((GM_CHUNK, GM_CHUNK), bool))
    layers = []
    for l in range(depth):
        disc, disc_vjp = jax.vjp(_ssm_discretize, w["ssm_a_re"][l], w["ssm_a_im"][l], w["ssm_log_dt"][l],
                                 w["ssm_b_re"][l], w["ssm_b_im"][l])
        lr, li, bbr, bbi = disc
        bblk, cblk, glu, glub, dskip = _ssm_matrices(bbr, bbi, w["ssm_c_re"][l], w["ssm_c_im"][l],
                                                     w["ssm_glu_w"][l], w["ssm_glu_b"][l], w["ssm_d"][l])
        layers.append(dict(
            disc_vjp=disc_vjp, lr=lr, li=li, bblk=bblk, cblk=cblk, glu=glu, glub=glub, dskip=dskip,
            fwdc=_scan_constants(lr, li, False), revc=_scan_constants(lr, li, True),
            w_tril=jnp.where(tril[None], w["gm_w_s"][l], 0.0).astype(BF16),
            gm_bias=jnp.repeat(w["gm_b_s"][l].T, GM_HEAD_DIM, axis=1),
            g1=w["norm_ffn1"][l][None], gmix=w["norm_mix"][l][None], g2=w["norm_ffn2"][l][None],
            gv=w["gm_v_gain"][l][None], gs=w["gain_ssm_out"][l][None], gg=w["gain_gm_out"][l][None],
        ))

    saved = []
    for l in range(depth):
        p = layers[l]
        x0 = x
        g1, gmix, g2 = p["g1"], p["gmix"], p["g2"]
        w_in, w_out = weights("ffn1", l, x0)
        if l == 0:
            g1 = _behind(g1, gather_start(0, w_in))
        x1, xn1, gu1 = _ffn_fwd(x0, g1, w_in, w_out, tm_ffn, f"ffn1_fwd_{l}")
        if l == 0:
            gmix = _behind(gmix, gather_start(1, x1))
        mwi, mwo = weights("mix", l, x1)
        z = _mix_in_fwd(x1, gmix, mwi, tm, f"mix_in_fwd_{l}")
        y_ssm, h = _ssm_fwd(z, p["bblk"], p["cblk"], p["glu"], p["glub"], p["dskip"], p["fwdc"], batch, t_chunk,
                            f"ssm_fwd_{l}")
        y_gm = _gm_fwd(z, p["gv"], p["w_tril"], p["gm_bias"], gm_rows, f"gm_fwd_{l}")
        x2 = _mix_out_fwd(y_ssm, y_gm, p["gs"], p["gg"], mwo, x1, tm, f"mix_out_fwd_{l}")
        if l == 0:
            g2 = _behind(g2, gather_start(2, x2))
        x, xn2, gu2 = _ffn_fwd(x2, g2, *weights("ffn2", l, x2), tm_ffn, f"ffn2_fwd_{l}")
        saved.append((x0, x1, x2, z, h, y_ssm, y_gm, xn1, gu1, xn2, gu2))

    dx, sq, dnf = _loss_head(x, w["norm_final"][None], target, tm, "loss_head")
    loss = lax.psum((0.5 / D_MODEL) * jnp.sum(sq), AXES)

    small = {k: [None] * depth for k in SMALL if k != "norm_final"}
    sent = []

    def send(group, l, keys, parts):
        (handle,), tok = _exchange_start([parts], me, True, f"reduce_scatter_start_{group}_{l}")
        sent.append((group, l, keys, handle))
        return tok

    token = None
    for l in reversed(range(depth)):
        p = layers[l]
        x0, x1, x2, z, h, y_ssm, y_gm, xn1, gu1, xn2, gu2 = saved[l]
        mwi, mwo = weights("mix", l)
        dx_out = dx
        g2 = p["g2"] if token is None else _behind(p["g2"], token)
        dx, dgu, act, dgain, dyb = _ffn_bwd(x2, g2, dx_out, gu2, *weights("ffn2", l), tm, f"ffn2_bwd_{l}")
        dw_in = _ffn_dw_in(xn2, dgu, tk, f"ffn2_dw_in_{l}")
        dw_out = _ffn_dw_out(act, dyb, tk, f"ffn2_dw_out_{l}").reshape(N_DEV, FF_SHARD // 2, D_MODEL)
        token = send("ffn2", l, ("ffn2_w_in", "ffn2_w_out"), [dw_in, dw_out])
        small["norm_ffn2"][l] = dgain.sum(0)

        dy_ssm, dy_gm, dwo, dgs, dgg = _mix_out_bwd(y_ssm, y_gm, _behind(p["gs"], token), p["gg"], mwo, dx, tm,
                                                    f"mix_out_bwd_{l}")
        dwo = dwo.astype(BF16).reshape(N_DEV, D_MODEL // N_DEV, D_MODEL)
        small["gain_ssm_out"][l] = dgs.sum(0)
        small["gain_gm_out"][l] = dgg.sum(0)

        du_ssm, dglu, dglub, ddskip, dct, db, q = _ssm_bwd(
            z, h, dy_ssm, p["bblk"], p["cblk"], p["glu"], p["glub"], p["dskip"], p["revc"], batch, t_chunk,
            f"ssm_bwd_{l}")
        du_gm, dv_gm, dws, dbias, dgv = _gm_bwd(z, dy_gm, p["gv"], p["w_tril"], p["gm_bias"], gm_rows, f"gm_bwd_{l}")

        q = q.sum(0).reshape(2, 2, HALF_GROUPS, SSM_STATE)
        qr, qi = q[:, 0].reshape(SSM_GROUPS, SSM_STATE), q[:, 1].reshape(SSM_GROUPS, SSM_STATE)
        den = p["lr"] * p["lr"] + p["li"] * p["li"]
        d_re = (qr * p["lr"] + qi * p["li"]) / den
        d_im = (qi * p["lr"] - qr * p["li"]) / den
        dbb = jnp.stack([_block_diag_take(db[hf, :, k * HALF_ST:(k + 1) * HALF_ST], HALF_GROUPS, SSM_CH, SSM_STATE)
                         for k in range(2) for hf in range(2)]).reshape(2, SSM_GROUPS, SSM_CH, SSM_STATE)
        dcc = jnp.stack([_block_diag_take(dct[hf, :, k * HALF_ST:(k + 1) * HALF_ST], HALF_GROUPS, SSM_CH, SSM_STATE)
                         for k in range(2) for hf in range(2)]).reshape(2, SSM_GROUPS, SSM_CH, SSM_STATE)
        da_re, da_im, dlog_dt, db_re, db_im = p["disc_vjp"](
            (d_re, -d_im, jnp.swapaxes(dbb[0], 1, 2), jnp.swapaxes(dbb[1], 1, 2)))
        small["ssm_a_re"][l], small["ssm_a_im"][l], small["ssm_log_dt"][l] = da_re, da_im, dlog_dt
        small["ssm_b_re"][l], small["ssm_b_im"][l] = db_re, db_im
        small["ssm_c_re"][l], small["ssm_c_im"][l] = dcc[0], -dcc[1]
        small["ssm_d"][l] = ddskip.sum(0).reshape(SSM_GROUPS, SSM_CH)
        small["ssm_glu_w"][l] = jnp.concatenate(
            [_block_diag_take(dglu[:, :SSM_WIDTH], SSM_GROUPS, SSM_CH, SSM_CH),
             _block_diag_take(dglu[:, SSM_WIDTH:], SSM_GROUPS, SSM_CH, SSM_CH)], axis=2)
        dglub = dglub.sum(0)
        small["ssm_glu_b"][l] = jnp.concatenate(
            [dglub[:SSM_WIDTH].reshape(SSM_GROUPS, SSM_CH), dglub[SSM_WIDTH:].reshape(SSM_GROUPS, SSM_CH)], axis=1)
        small["gm_v_gain"][l] = dgv.sum(0)
        small["gm_w_s"][l] = jnp.where(tril[None], dws, 0.0)
        small["gm_b_s"][l] = dbias.reshape(GM_CHUNK, GM_HEADS, GM_HEAD_DIM).sum(-1).T

        dx, dwi, dgain = _mix_in_bwd(x1, p["gmix"], du_ssm, du_gm, dv_gm, dx, mwi, tm, f"mix_in_bwd_{l}")
        dwi = dwi.astype(BF16).reshape(N_DEV, IN_COLS // N_DEV, D_MODEL)
        token = send("mix", l, ("mix_w_in", "mix_w_out"), [dwi, dwo])
        small["norm_mix"][l] = dgain.sum(0)

        dx_out = dx
        dx, dgu, act, dgain, dyb = _ffn_bwd(x0, _behind(p["g1"], token), dx_out, gu1, *weights("ffn1", l), tm,
                                       f"ffn1_bwd_{l}")
        small["norm_ffn1"][l] = dgain.sum(0)
        if l > 0:
            dw_in = _ffn_dw_in(xn1, dgu, tk, f"ffn1_dw_in_{l}")
            dw_out = _ffn_dw_out(act, dyb, tk, f"ffn1_dw_out_{l}").reshape(N_DEV, FF_SHARD // 2, D_MODEL)
            token = send("ffn1", l, ("ffn1_w_in", "ffn1_w_out"), [dw_in, dw_out])
            continue
        small_g = [jnp.stack(small[k]) if k != "norm_final" else dnf.sum(0) for k in SMALL]
        total = sum(int(math.prod(w[k].shape)) for k in SMALL)
        rows = -(-total // (LANES * N_DEV * SUBLANES)) * N_DEV * SUBLANES
        g_all = _all_reduce_small(_pack(small_g, rows).reshape(N_DEV, rows // N_DEV, LANES), "all_reduce_small")
        dw_in = _ffn_dw_in(xn1, dgu, tk, f"ffn1_dw_in_{l}", after=g_all)
        token = send("ffn1_in", l, ("ffn1_w_in",), [dw_in])
        dw_out = _ffn_dw_out(act, dyb, tk, f"ffn1_dw_out_{l}", after=token).reshape(
            N_DEV, FF_SHARD // 2, D_MODEL)
        token = send("ffn1_out", l, ("ffn1_w_out",), [dw_out])

    grad_x = dx.reshape(batch, seq, D_MODEL)
    grads, deltas, new_m, new_v = {}, {}, {}, {}

    g_all = _behind(g_all.reshape(rows, LANES), token)
    like = [w[k] for k in SMALL]
    d_p, m_p, v_p = _adam_packed(g_all, _pack(like, rows), _pack([m[k] for k in SMALL], rows),
                                 _pack([v[k] for k in SMALL], rows), "adam_small")
    for k, g_, d_, m_, v_ in zip(SMALL, _unpack(g_all, like), _unpack(d_p, like), _unpack(m_p, like),
                                 _unpack(v_p, like)):
        grads[k], deltas[k], new_m[k], new_v[k] = g_, d_, m_, v_

    results = {}
    after = d_p
    for group, l, keys, handle in sent:
        landed = _exchange_wait(handle, after, True, f"reduce_scatter_wait_{group}_{l}")
        for k, parts in zip(keys, landed):
            view = (lambda a: jnp.swapaxes(a, 1, 2)) if k in TRANSPOSED else (lambda a: a)
            results[k] = _adam_sharded(parts, view(w[k]), view(m[k]), view(v[k]), l, results.get(k),
                                       f"adam_{k}_{l}")
            after = results[k][0]
    for k in BIG:
        view = (lambda a: jnp.swapaxes(a, 1, 2)) if k in TRANSPOSED else (lambda a: a)
        grads[k], deltas[k], new_m[k], new_v[k] = [view(a) for a in results[k]]
    return loss, grad_x, grads, deltas, new_m, new_v


def kernel(x, norm_ffn1, ffn1_w_in, ffn1_w_out, norm_mix, mix_w_in, ssm_a_re, ssm_a_im, ssm_log_dt, ssm_b_re, ssm_b_im, ssm_c_re, ssm_c_im, ssm_d, ssm_glu_w, ssm_glu_b, gm_v_gain, gm_w_s, gm_b_s, gain_ssm_out, gain_gm_out, mix_w_out, norm_ffn2, ffn2_w_in, ffn2_w_out, norm_final, loss_target, m_norm_ffn1, m_ffn1_w_in, m_ffn1_w_out, m_norm_mix, m_mix_w_in, m_ssm_a_re, m_ssm_a_im, m_ssm_log_dt, m_ssm_b_re, m_ssm_b_im, m_ssm_c_re, m_ssm_c_im, m_ssm_d, m_ssm_glu_w, m_ssm_glu_b, m_gm_v_gain, m_gm_w_s, m_gm_b_s, m_gain_ssm_out, m_gain_gm_out, m_mix_w_out, m_norm_ffn2, m_ffn2_w_in, m_ffn2_w_out, m_norm_final, v_norm_ffn1, v_ffn1_w_in, v_ffn1_w_out, v_norm_mix, v_mix_w_in, v_ssm_a_re, v_ssm_a_im, v_ssm_log_dt, v_ssm_b_re, v_ssm_b_im, v_ssm_c_re, v_ssm_c_im, v_ssm_d, v_ssm_glu_w, v_ssm_glu_b, v_gm_v_gain, v_gm_w_s, v_gm_b_s, v_gain_ssm_out, v_gain_gm_out, v_mix_w_out, v_norm_ffn2, v_ffn2_w_in, v_ffn2_w_out, v_norm_final):
    args = locals()
    w = {k: args[k] for k in WEIGHTS}
    m = {k: args["m_" + k] for k in WEIGHTS}
    v = {k: args["v_" + k] for k in WEIGHTS}
    loss, grad_x, grads, deltas, new_m, new_v = _step(x, loss_target, w, m, v)
    return (loss, grad_x, *[grads[k] for k in WEIGHTS], *[deltas[k] for k in WEIGHTS],
            *[new_m[k] for k in WEIGHTS], *[new_v[k] for k in WEIGHTS])
```

```python
import functools
import math

import jax
import jax.numpy as jnp
from jax import lax
from jax.experimental import pallas as pl
from jax.experimental.pallas import tpu as pltpu

F32 = jnp.float32
BF16 = jnp.bfloat16
MESH = pl.DeviceIdType.MESH
AXES = ("x", "y", "c")

N_DEV = 8
D_MODEL = 1024
D_FF = 2816
FF_SHARD = 2 * D_FF // N_DEV
FF_CHUNKS = 4
MXU_DIM = 256
FF_PIECES = tuple((lo, min(lo + MXU_DIM, FF_SHARD)) for lo in range(0, FF_SHARD, MXU_DIM))
SSM_WIDTH = 512
SSM_CH = 16
SSM_GROUPS = 32
SSM_STATE = 64
HALF_GROUPS = 16
HALF_IN = HALF_GROUPS * SSM_CH
HALF_ST = HALF_GROUPS * SSM_STATE
GM_WIDTH = 512
GM_HEADS = 4
GM_HEAD_DIM = 128
GM_CHUNK = 128
IN_COLS = SSM_WIDTH + 2 * GM_WIDTH
EPS = 1e-6
SUBLANES = 8
LANES = 128

ADAM_LR = 0.001
ADAM_B1 = 0.9
ADAM_B2 = 0.999
ADAM_EPS = 1e-08
ADAM_WD = 0.01
ADAM_STEP = 10

VMEM_LIMIT = 46 * 1024 * 1024


def _cp(*sem):
    return pltpu.CompilerParams(dimension_semantics=sem, vmem_limit_bytes=VMEM_LIMIT)


def _rms_fwd(x, g):
    r = lax.rsqrt(jnp.mean(x * x, axis=-1, keepdims=True) + EPS)
    xh = x * r
    return xh * g, xh, r


def _rms_bwd(dy, xh, r, g):
    dxh = dy * g
    dx = r * (dxh - xh * jnp.mean(dxh * xh, axis=-1, keepdims=True))
    return dx, dy * xh


def _rows8(a):
    m, n = a.shape
    return a.reshape(m // SUBLANES, SUBLANES, n).sum(axis=0)


_GELU_K = math.sqrt(2.0 / math.pi)
_GELU_C = 0.044715


def _gelu(x):
    th = jnp.tanh(_GELU_K * (x + _GELU_C * x * x * x))
    return 0.5 * x * (1.0 + th), th


def _gelu_grad(x, th):
    return 0.5 * (1.0 + th) + 0.5 * x * (1.0 - th * th) * (_GELU_K * (1.0 + 3.0 * _GELU_C * x * x))


def _dot(a, b):
    return jnp.dot(a, b, preferred_element_type=F32)


def _dot_nt(a, b):
    return lax.dot_general(a, b, (((1,), (1,)), ((), ())), preferred_element_type=F32)


def _dot_tn(a, b):
    return lax.dot_general(a, b, (((0,), (0,)), ((), ())), preferred_element_type=F32)


def _ffn_fwd(x, gain, w_in_ag, w_out_ag, tm, name):
    n = x.shape[0]

    def body(x_ref, g_ref, wg_ref, wu_ref, wo_ref, o_ref, xn_ref, gu_ref):
        j = pl.program_id(1)

        @pl.when(j == 0)
        def _():
            xv = x_ref[...]
            y, _, _ = _rms_fwd(xv, g_ref[...])
            xn_ref[...] = y.astype(BF16)
            o_ref[...] = xv

        xn = xn_ref[...]
        wo = wo_ref[...].reshape(FF_SHARD, D_MODEL)
        out = None
        for lo, hi in FF_PIECES:
            gg = _dot(xn, wg_ref[:, lo:hi])
            uu = _dot(xn, wu_ref[:, lo:hi])
            gu_ref[0, :, lo:hi] = gg.astype(BF16)
            gu_ref[1, :, lo:hi] = uu.astype(BF16)
            act = (gg * jax.nn.sigmoid(gg) * uu).astype(BF16)
            part = _dot(act, wo[lo:hi, :])
            out = part if out is None else out + part
        o_ref[...] += 0.5 * out

    return pl.pallas_call(
        body, name=name, grid=(n // tm, FF_CHUNKS),
        in_specs=[
            pl.BlockSpec((tm, D_MODEL), lambda i, j: (i, 0)),
            pl.BlockSpec((1, D_MODEL), lambda i, j: (0, 0)),
            pl.BlockSpec((None, D_MODEL, FF_SHARD), lambda i, j: (j, 0, 0)),
            pl.BlockSpec((None, D_MODEL, FF_SHARD), lambda i, j: (j + FF_CHUNKS, 0, 0)),
            pl.BlockSpec((2, FF_SHARD // 2, D_MODEL), lambda i, j: (j, 0, 0)),
        ],
        out_specs=[
            pl.BlockSpec((tm, D_MODEL), lambda i, j: (i, 0)),
            pl.BlockSpec((tm, D_MODEL), lambda i, j: (i, 0)),
            pl.BlockSpec((None, 2, tm, FF_SHARD), lambda i, j: (j, 0, i, 0)),
        ],
        out_shape=[
            jax.ShapeDtypeStruct((n, D_MODEL), F32),
            jax.ShapeDtypeStruct((n, D_MODEL), BF16),
            jax.ShapeDtypeStruct((FF_CHUNKS, 2, n, FF_SHARD), BF16),
        ],
        compiler_params=_cp("parallel", "arbitrary"),
    )(x, gain, w_in_ag, w_in_ag, w_out_ag)


def _ffn_bwd(x, gain, dy, gu, w_in_ag, w_out_ag, tm, name):
    n = x.shape[0]
    work = (n // tm) * FF_CHUNKS
    steps = work + 1
    first = lambda s: jnp.minimum(s, work - 1)
    second = lambda s: jnp.maximum(s - 1, 0)
    tile1, chunk1 = (lambda s: first(s) // FF_CHUNKS), (lambda s: first(s) % FF_CHUNKS)
    tile2, chunk2 = (lambda s: second(s) // FF_CHUNKS), (lambda s: second(s) % FF_CHUNKS)

    def body(x_ref, g_ref, dy_ref, dy1_ref, gu_ref, wg_ref, wu_ref, wo_ref, dx_ref, dgu_ref, act_ref, dgain_ref,
             dyb_ref, held_ref):
        s = pl.program_id(0)

        @pl.when(s == 0)
        def _():
            dgain_ref[...] = jnp.zeros_like(dgain_ref)
            held_ref[...] = jnp.zeros_like(held_ref)

        @pl.when(jnp.logical_and(chunk1(s) == 0, s < work))
        def _():
            dyb_ref[...] = (0.5 * dy1_ref[...]).astype(BF16)

        @pl.when(chunk2(s) == 0)
        def _():
            dx_ref[...] = jnp.zeros_like(dx_ref)

        held = held_ref[1 - s % 2]
        part = _dot_nt(held[0], wg_ref[...]) + _dot_nt(held[1], wu_ref[...])
        dx_ref[...] += jnp.where(s > 0, part, 0.0)

        dyb = dyb_ref[...]
        wo = wo_ref[...].reshape(FF_SHARD, D_MODEL)
        slot = s % 2
        for lo, hi in FF_PIECES:
            gg = gu_ref[0, :, lo:hi].astype(F32)
            uu = gu_ref[1, :, lo:hi].astype(F32)
            dact = _dot_nt(dyb, wo[lo:hi, :])
            sig = jax.nn.sigmoid(gg)
            silu = gg * sig
            act_ref[:, lo:hi] = (silu * uu).astype(BF16)
            du = (dact * silu).astype(BF16)
            dg = (dact * uu * (sig * (1.0 + gg * (1.0 - sig)))).astype(BF16)
            dgu_ref[0, :, lo:hi] = dg
            dgu_ref[1, :, lo:hi] = du
            held_ref[slot, 0, :, lo:hi] = dg
            held_ref[slot, 1, :, lo:hi] = du

        @pl.when(jnp.logical_and(chunk2(s) == FF_CHUNKS - 1, s > 0))
        def _():
            g = g_ref[...]
            _, xh, r = _rms_fwd(x_ref[...], g)
            dx, dgr = _rms_bwd(dx_ref[...], xh, r, g)
            dx_ref[...] = dy_ref[...] + dx
            dgain_ref[...] += _rows8(dgr)

    return pl.pallas_call(
        body, name=name, grid=(steps,),
        in_specs=[
            pl.BlockSpec((tm, D_MODEL), lambda s: (tile2(s), 0)),
            pl.BlockSpec((1, D_MODEL), lambda s: (0, 0)),
            pl.BlockSpec((tm, D_MODEL), lambda s: (tile2(s), 0)),
            pl.BlockSpec((tm, D_MODEL), lambda s: (tile1(s), 0)),
            pl.BlockSpec((None, 2, tm, FF_SHARD), lambda s: (chunk1(s), 0, tile1(s), 0)),
            pl.BlockSpec((None, D_MODEL, FF_SHARD), lambda s: (chunk2(s), 0, 0)),
            pl.BlockSpec((None, D_MODEL, FF_SHARD), lambda s: (chunk2(s) + FF_CHUNKS, 0, 0)),
            pl.BlockSpec((2, FF_SHARD // 2, D_MODEL), lambda s: (chunk1(s), 0, 0)),
        ],
        out_specs=[
            pl.BlockSpec((tm, D_MODEL), lambda s: (tile2(s), 0)),
            pl.BlockSpec((None, 2, tm, FF_SHARD), lambda s: (chunk1(s), 0, tile1(s), 0)),
            pl.BlockSpec((None, tm, FF_SHARD), lambda s: (chunk1(s), tile1(s), 0)),
            pl.BlockSpec((SUBLANES, D_MODEL), lambda s: (0, 0)),
            pl.BlockSpec((tm, D_MODEL), lambda s: (tile1(s), 0)),
        ],
        out_shape=[
            jax.ShapeDtypeStruct((n, D_MODEL), F32),
            jax.ShapeDtypeStruct((FF_CHUNKS, 2, n, FF_SHARD), BF16),
            jax.ShapeDtypeStruct((FF_CHUNKS, n, FF_SHARD), BF16),
            jax.ShapeDtypeStruct((SUBLANES, D_MODEL), F32),
            jax.ShapeDtypeStruct((n, D_MODEL), BF16),
        ],
        scratch_shapes=[pltpu.VMEM((2, 2, tm, FF_SHARD), BF16)],
        compiler_params=_cp("arbitrary"),
    )(x, gain, dy, dy, gu, w_in_ag, w_in_ag, w_out_ag)


def _ffn_dw_in(xn, dgu, tk, name, after=None):
    n = xn.shape[0]
    nk = n // tk
    deps = [] if after is None else [after]

    def body(a_ref, b_ref, *rest):
        o_ref, acc_ref = rest[-2:]
        k = pl.program_id(2)

        @pl.when(k == 0)
        def _():
            acc_ref[...] = jnp.zeros_like(acc_ref)

        acc_ref[...] += _dot_tn(b_ref[...], a_ref[...])

        @pl.when(k == nk - 1)
        def _():
            o_ref[...] = acc_ref[...].astype(BF16)

    return pl.pallas_call(
        body, name=name, grid=(FF_CHUNKS, 2, nk),
        in_specs=[
            pl.BlockSpec((tk, D_MODEL), lambda j, p, k: (k, 0)),
            pl.BlockSpec((None, None, tk, FF_SHARD), lambda j, p, k: (j, p, k, 0)),
        ] + [pl.BlockSpec(memory_space=pl.ANY)] * len(deps),
        out_specs=pl.BlockSpec((None, FF_SHARD, D_MODEL), lambda j, p, k: (FF_CHUNKS * p + j, 0, 0)),
        out_shape=jax.ShapeDtypeStruct((N_DEV, FF_SHARD, D_MODEL), BF16),
        scratch_shapes=[pltpu.VMEM((FF_SHARD, D_MODEL), F32)],
        compiler_params=_cp("parallel", "parallel", "arbitrary"),
    )(xn, dgu, *deps)


def _ffn_dw_out(act, dyb, tk, name, after=None):
    n = act.shape[1]
    nk = n // tk
    deps = [] if after is None else [after]

    def body(a_ref, b_ref, *rest):
        o_ref, acc_ref = rest[-2:]
        k = pl.program_id(1)

        @pl.when(k == 0)
        def _():
            acc_ref[...] = jnp.zeros_like(acc_ref)

        acc_ref[...] += _dot_tn(a_ref[...], b_ref[...])

        @pl.when(k == nk - 1)
        def _():
            o_ref[...] = acc_ref[...].astype(BF16)

    return pl.pallas_call(
        body, name=name, grid=(FF_CHUNKS, nk),
        in_specs=[
            pl.BlockSpec((None, tk, FF_SHARD), lambda j, k: (j, k, 0)),
            pl.BlockSpec((tk, D_MODEL), lambda j, k: (k, 0)),
        ] + [pl.BlockSpec(memory_space=pl.ANY)] * len(deps),
        out_specs=pl.BlockSpec((None, FF_SHARD, D_MODEL), lambda j, k: (j, 0, 0)),
        out_shape=jax.ShapeDtypeStruct((FF_CHUNKS, FF_SHARD, D_MODEL), BF16),
        scratch_shapes=[pltpu.VMEM((FF_SHARD, D_MODEL), F32)],
        compiler_params=_cp("parallel", "arbitrary"),
    )(act, dyb, *deps)


def _mix_in_fwd(x, gain, w, tm, name):
    n = x.shape[0]

    def body(x_ref, g_ref, w_ref, z_ref):
        y, _, _ = _rms_fwd(x_ref[...], g_ref[...])
        z_ref[...] = _dot(y.astype(BF16), w_ref[...])

    return pl.pallas_call(
        body, name=name, grid=(n // tm,),
        in_specs=[
            pl.BlockSpec((tm, D_MODEL), lambda i: (i, 0)),
            pl.BlockSpec((1, D_MODEL), lambda i: (0, 0)),
            pl.BlockSpec((D_MODEL, IN_COLS), lambda i: (0, 0)),
        ],
        out_specs=pl.BlockSpec((tm, IN_COLS), lambda i: (i, 0)),
        out_shape=jax.ShapeDtypeStruct((n, IN_COLS), F32),
        compiler_params=_cp("parallel"),
    )(x, gain, w)


def _mix_in_bwd(x, gain, du_ssm, du_gm, dv_gm, d_res, w, tm, name):
    n = x.shape[0]

    def body(x_ref, g_ref, d0_ref, d1_ref, d2_ref, dres_ref, w_ref, dx_ref, dw_ref, dgain_ref):
        i = pl.program_id(0)

        @pl.when(i == 0)
        def _():
            dw_ref[...] = jnp.zeros_like(dw_ref)
            dgain_ref[...] = jnp.zeros_like(dgain_ref)

        g = g_ref[...]
        y, xh, r = _rms_fwd(x_ref[...], g)
        xn = y.astype(BF16)
        dxn = jnp.zeros((tm, D_MODEL), F32)
        for k, d_ref in enumerate((d0_ref, d1_ref, d2_ref)):
            dz = d_ref[...].astype(BF16)
            cols = slice(k * SSM_WIDTH, (k + 1) * SSM_WIDTH)
            dxn += _dot_nt(dz, w_ref[:, cols])
            dw_ref[cols, :] += _dot_tn(dz, xn)
        dx, dgr = _rms_bwd(dxn, xh, r, g)
        dx_ref[...] = dres_ref[...] + dx
        dgain_ref[...] += _rows8(dgr)

    row = lambda i: (i, 0)
    fixed = lambda i: (0, 0)
    return pl.pallas_call(
        body, name=name, grid=(n // tm,),
        in_specs=[
            pl.BlockSpec((tm, D_MODEL), row),
            pl.BlockSpec((1, D_MODEL), fixed),
            pl.BlockSpec((tm, SSM_WIDTH), row),
            pl.BlockSpec((tm, GM_WIDTH), row),
            pl.BlockSpec((tm, GM_WIDTH), row),
            pl.BlockSpec((tm, D_MODEL), row),
            pl.BlockSpec((D_MODEL, IN_COLS), fixed),
        ],
        out_specs=[
            pl.BlockSpec((tm, D_MODEL), row),
            pl.BlockSpec((IN_COLS, D_MODEL), fixed),
            pl.BlockSpec((SUBLANES, D_MODEL), fixed),
        ],
        out_shape=[
            jax.ShapeDtypeStruct((n, D_MODEL), F32),
            jax.ShapeDtypeStruct((IN_COLS, D_MODEL), F32),
            jax.ShapeDtypeStruct((SUBLANES, D_MODEL), F32),
        ],
        compiler_params=_cp("arbitrary"),
    )(x, gain, du_ssm, du_gm, dv_gm, d_res, w)


def _mix_out_fwd(y_ssm, y_gm, g_ssm, g_gm, w, x, tm, name):
    n = x.shape[0]

    def body(ys_ref, yg_ref, gs_ref, gg_ref, w_ref, x_ref, o_ref):
        a, _, _ = _rms_fwd(ys_ref[...], gs_ref[...])
        b, _, _ = _rms_fwd(yg_ref[...], gg_ref[...])
        o_ref[...] = (x_ref[...] + _dot(a.astype(BF16), w_ref[0:SSM_WIDTH, :])
                      + _dot(b.astype(BF16), w_ref[SSM_WIDTH:D_MODEL, :]))

    row = lambda i: (i, 0)
    fixed = lambda i: (0, 0)
    return pl.pallas_call(
        body, name=name, grid=(n // tm,),
        in_specs=[
            pl.BlockSpec((tm, SSM_WIDTH), row), pl.BlockSpec((tm, GM_WIDTH), row),
            pl.BlockSpec((1, SSM_WIDTH), fixed), pl.BlockSpec((1, GM_WIDTH), fixed),
            pl.BlockSpec((D_MODEL, D_MODEL), fixed), pl.BlockSpec((tm, D_MODEL), row),
        ],
        out_specs=pl.BlockSpec((tm, D_MODEL), row),
        out_shape=jax.ShapeDtypeStruct((n, D_MODEL), F32),
        compiler_params=_cp("parallel"),
    )(y_ssm, y_gm, g_ssm, g_gm, w, x)


def _mix_out_bwd(y_ssm, y_gm, g_ssm, g_gm, w, dx, tm, name):
    n = dx.shape[0]

    def body(ys_ref, yg_ref, gs_ref, gg_ref, w_ref, dx_ref, dys_ref, dyg_ref, dw_ref, dgs_ref, dgg_ref):
        i = pl.program_id(0)

        @pl.when(i == 0)
        def _():
            dw_ref[...] = jnp.zeros_like(dw_ref)
            dgs_ref[...] = jnp.zeros_like(dgs_ref)
            dgg_ref[...] = jnp.zeros_like(dgg_ref)

        dxb = dx_ref[...].astype(BF16)
        parts = ((ys_ref, gs_ref, dys_ref, dgs_ref, 0), (yg_ref, gg_ref, dyg_ref, dgg_ref, SSM_WIDTH))
        for y_ref, g_ref, dy_ref, dg_ref, off in parts:
            g = g_ref[...]
            yn, xh, r = _rms_fwd(y_ref[...], g)
            rows = slice(off, off + SSM_WIDTH)
            dyn = _dot_nt(dxb, w_ref[rows, :])
            dw_ref[rows, :] += _dot_tn(yn.astype(BF16), dxb)
            dy, dgr = _rms_bwd(dyn, xh, r, g)
            dy_ref[...] = dy
            dg_ref[...] += _rows8(dgr)

    row = lambda i: (i, 0)
    fixed = lambda i: (0, 0)
    return pl.pallas_call(
        body, name=name, grid=(n // tm,),
        in_specs=[
            pl.BlockSpec((tm, SSM_WIDTH), row), pl.BlockSpec((tm, GM_WIDTH), row),
            pl.BlockSpec((1, SSM_WIDTH), fixed), pl.BlockSpec((1, GM_WIDTH), fixed),
            pl.BlockSpec((D_MODEL, D_MODEL), fixed), pl.BlockSpec((tm, D_MODEL), row),
        ],
        out_specs=[
            pl.BlockSpec((tm, SSM_WIDTH), row), pl.BlockSpec((tm, GM_WIDTH), row),
            pl.BlockSpec((D_MODEL, D_MODEL), fixed),
            pl.BlockSpec((SUBLANES, SSM_WIDTH), fixed), pl.BlockSpec((SUBLANES, GM_WIDTH), fixed),
        ],
        out_shape=[
            jax.ShapeDtypeStruct((n, SSM_WIDTH), F32), jax.ShapeDtypeStruct((n, GM_WIDTH), F32),
            jax.ShapeDtypeStruct((D_MODEL, D_MODEL), F32),
            jax.ShapeDtypeStruct((SUBLANES, SSM_WIDTH), F32), jax.ShapeDtypeStruct((SUBLANES, GM_WIDTH), F32),
        ],
        compiler_params=_cp("arbitrary"),
    )(y_ssm, y_gm, g_ssm, g_gm, w, dx)


SCAN_W = 512
SCAN_PIECES = HALF_ST // SCAN_W


def _scan_tiles(src_ref, dst_ref, dst_off, c_ref, half, carry_ref, n_tiles, reverse, extra=None):
    shifts = (1, 2, 4)
    carry_row = 0 if reverse else SUBLANES - 1

    def cols(piece, im):
        lo = im * HALF_ST + piece * SCAN_W
        return slice(lo, lo + SCAN_W)

    def step(t, state):
        carries, accs = state
        k = (n_tiles - 1 - t) if reverse else t
        rows = slice(k * SUBLANES, (k + 1) * SUBLANES)
        new_carries, new_accs = [], []
        for piece in range(SCAN_PIECES):
            cr, ci = carries[piece]
            xr0 = src_ref[rows, cols(piece, 0)]
            xi0 = src_ref[rows, cols(piece, 1)]
            xr, xi = xr0, xi0
            for si, s in enumerate(shifts):
                ar = c_ref[half, si, :, cols(piece, 0)]
                ai = c_ref[half, si, :, cols(piece, 1)]
                sh = (SUBLANES - s) if reverse else s
                sr = pltpu.roll(xr, sh, 0)
                sm = pltpu.roll(xi, sh, 0)
                xr, xi = xr + (ar * sr - ai * sm), xi + (ar * sm + ai * sr)
            pr = c_ref[half, 3, :, cols(piece, 0)]
            pi = c_ref[half, 3, :, cols(piece, 1)]
            hr = xr + (pr * cr - pi * ci)
            hi = xi + (pr * ci + pi * cr)
            dst_ref[rows, pl.ds(dst_off + piece * SCAN_W, SCAN_W)] = hr
            dst_ref[rows, pl.ds(dst_off + HALF_ST + piece * SCAN_W, SCAN_W)] = hi
            new_carries.append((jnp.broadcast_to(hr[carry_row:carry_row + 1, :], (SUBLANES, SCAN_W)),
                                jnp.broadcast_to(hi[carry_row:carry_row + 1, :], (SUBLANES, SCAN_W))))
            if extra is not None:
                new_accs.append(extra(rows, piece, (xr0, xi0), (hr, hi), accs[piece]))
        return tuple(new_carries), tuple(new_accs)

    base = half * 2 * HALF_ST
    carries0 = tuple((carry_ref[:, pl.ds(base + p * SCAN_W, SCAN_W)],
                      carry_ref[:, pl.ds(base + HALF_ST + p * SCAN_W, SCAN_W)]) for p in range(SCAN_PIECES))
    zero = jnp.zeros((SUBLANES, SCAN_W), F32)
    accs0 = tuple((zero, zero) for _ in range(SCAN_PIECES)) if extra is not None else ()
    state = (carries0, accs0)
    for t in range(n_tiles):
        state = step(t, state)
    carries, accs = state
    for p in range(SCAN_PIECES):
        carry_ref[:, pl.ds(base + p * SCAN_W, SCAN_W)] = carries[p][0]
        carry_ref[:, pl.ds(base + HALF_ST + p * SCAN_W, SCAN_W)] = carries[p][1]
    return accs


def _ssm_tail(hb, u, c_ref, glu_ref, glub_ref, dskip_ref):
    ypre = u * dskip_ref[...]
    parts = []
    for half in range(2):
        parts.append(_dot(hb[half], c_ref[half]))
    ypre = ypre + jnp.concatenate(parts, axis=1)
    yg, th = _gelu(ypre)
    zz = _dot(yg.astype(BF16), glu_ref[...]) + glub_ref[...]
    z1, z2 = zz[:, :SSM_WIDTH], zz[:, SSM_WIDTH:]
    sg = jax.nn.sigmoid(z2)
    return ypre, th, yg, z1, sg


def _ssm_fwd(z, bblk, cblk, glu, glub, dskip, fwdc, batch, t_chunk, name):
    n = z.shape[0]
    nk = n // batch // t_chunk
    n_tiles = t_chunk // SUBLANES

    def body(u_ref, b_ref, c_ref, glu_ref, glub_ref, dskip_ref, k_ref, y_ref, h_ref, bu_ref, carry_ref):
        @pl.when(pl.program_id(1) == 0)
        def _():
            carry_ref[...] = jnp.zeros_like(carry_ref)

        u = u_ref[...]
        ub = u.astype(BF16)
        for half in range(2):
            bu_ref[half] = _dot(ub[:, half * HALF_IN:(half + 1) * HALF_IN], b_ref[half])
            _scan_tiles(bu_ref.at[half], h_ref, half * 2 * HALF_ST, k_ref, half, carry_ref, n_tiles, False)
        hb = [h_ref[:, half * 2 * HALF_ST:(half + 1) * 2 * HALF_ST].astype(BF16) for half in range(2)]
        _, _, _, z1, sg = _ssm_tail(hb, u, c_ref, glu_ref, glub_ref, dskip_ref)
        y_ref[...] = z1 * sg

    fixed2 = lambda b, k: (0, 0)
    fixed3 = lambda b, k: (0, 0, 0)
    row = lambda b, k: (b * nk + k, 0)
    return pl.pallas_call(
        body, name=name, grid=(batch, nk),
        in_specs=[
            pl.BlockSpec((t_chunk, SSM_WIDTH), row),
            pl.BlockSpec((2, HALF_IN, 2 * HALF_ST), fixed3),
            pl.BlockSpec((2, 2 * HALF_ST, HALF_IN), fixed3),
            pl.BlockSpec((SSM_WIDTH, 2 * SSM_WIDTH), fixed2),
            pl.BlockSpec((1, 2 * SSM_WIDTH), fixed2),
            pl.BlockSpec((1, SSM_WIDTH), fixed2),
            pl.BlockSpec((2, 4, SUBLANES, 2 * HALF_ST), lambda b, k: (0, 0, 0, 0)),
        ],
        out_specs=[pl.BlockSpec((t_chunk, SSM_WIDTH), row), pl.BlockSpec((t_chunk, 4 * HALF_ST), row)],
        out_shape=[jax.ShapeDtypeStruct((n, SSM_WIDTH), F32), jax.ShapeDtypeStruct((n, 4 * HALF_ST), F32)],
        scratch_shapes=[pltpu.VMEM((2, t_chunk, 2 * HALF_ST), F32), pltpu.VMEM((SUBLANES, 4 * HALF_ST), F32)],
        compiler_params=_cp("parallel", "arbitrary"),
    )(z, bblk, cblk, glu, glub, dskip, fwdc)


def _ssm_bwd(z, h, dy, bblk, cblk, glu, glub, dskip, revc, batch, t_chunk, name):
    n = z.shape[0]
    nk = n // batch // t_chunk
    n_tiles = t_chunk // SUBLANES

    def body(u_ref, h_ref, dy_ref, b_ref, c_ref, glu_ref, glub_ref, dskip_ref, k_ref,
             du_ref, dglu_ref, dglub_ref, ddskip_ref, dct_ref, db_ref, q_ref, g_ref, carry_ref):
        first = jnp.logical_and(pl.program_id(0) == 0, pl.program_id(1) == 0)

        @pl.when(first)
        def _():
            for r in (dglu_ref, dglub_ref, ddskip_ref, dct_ref, db_ref, q_ref):
                r[...] = jnp.zeros_like(r)

        @pl.when(pl.program_id(1) == 0)
        def _():
            carry_ref[...] = jnp.zeros_like(carry_ref)

        u = u_ref[...]
        ub = u.astype(BF16)
        hb = [h_ref[:, half * 2 * HALF_ST:(half + 1) * 2 * HALF_ST].astype(BF16) for half in range(2)]
        ypre, th, yg, z1, sg = _ssm_tail(hb, u, c_ref, glu_ref, glub_ref, dskip_ref)
        dout = dy_ref[...]
        dz = jnp.concatenate([dout * sg, dout * z1 * sg * (1.0 - sg)], axis=1)
        dzb = dz.astype(BF16)
        dglu_ref[...] += _dot_tn(yg.astype(BF16), dzb)
        dglub_ref[...] += _rows8(dz)
        dypre = _dot_nt(dzb, glu_ref[...]) * _gelu_grad(ypre, th)
        ddskip_ref[...] += _rows8(dypre * u)
        dypb = dypre.astype(BF16)
        du_parts = []
        for half in range(2):
            dyp_h = dypb[:, half * HALF_IN:(half + 1) * HALF_IN]
            dct_ref[half] += _dot_tn(dyp_h, hb[half])
            g_ref[half] = _dot_nt(dyp_h, c_ref[half])

            def extra(rows, piece, x_in, g_out, acc, half=half):
                er, ei = g_out[0] - x_in[0], g_out[1] - x_in[1]
                base = half * 2 * HALF_ST + piece * SCAN_W
                hr = h_ref[rows, pl.ds(base, SCAN_W)]
                hi = h_ref[rows, pl.ds(base + HALF_ST, SCAN_W)]
                return acc[0] + (er * hr + ei * hi), acc[1] + (er * hi - ei * hr)

            accs = _scan_tiles(g_ref.at[half], g_ref.at[half], 0, k_ref, half, carry_ref, n_tiles, True, extra)
            for piece in range(SCAN_PIECES):
                base = half * 2 * HALF_ST + piece * SCAN_W
                q_ref[:, pl.ds(base, SCAN_W)] += accs[piece][0]
                q_ref[:, pl.ds(base + HALF_ST, SCAN_W)] += accs[piece][1]
            gb = g_ref[half].astype(BF16)
            db_ref[half] += _dot_tn(ub[:, half * HALF_IN:(half + 1) * HALF_IN], gb)
            du_parts.append(_dot_nt(gb, b_ref[half]))
        du_ref[...] = dypre * dskip_ref[...] + jnp.concatenate(du_parts, axis=1)

    fixed2 = lambda b, k: (0, 0)
    fixed3 = lambda b, k: (0, 0, 0)
    row = lambda b, k: (b * nk + (nk - 1 - k), 0)
    return pl.pallas_call(
        body, name=name, grid=(batch, nk),
        in_specs=[
            pl.BlockSpec((t_chunk, SSM_WIDTH), row),
            pl.BlockSpec((t_chunk, 4 * HALF_ST), row),
            pl.BlockSpec((t_chunk, SSM_WIDTH), row),
            pl.BlockSpec((2, HALF_IN, 2 * HALF_ST), fixed3),
            pl.BlockSpec((2, 2 * HALF_ST, HALF_IN), fixed3),
            pl.BlockSpec((SSM_WIDTH, 2 * SSM_WIDTH), fixed2),
            pl.BlockSpec((1, 2 * SSM_WIDTH), fixed2),
            pl.BlockSpec((1, SSM_WIDTH), fixed2),
            pl.BlockSpec((2, 4, SUBLANES, 2 * HALF_ST), lambda b, k: (0, 0, 0, 0)),
        ],
        out_specs=[
            pl.BlockSpec((t_chunk, SSM_WIDTH), row),
            pl.BlockSpec((SSM_WIDTH, 2 * SSM_WIDTH), fixed2),
            pl.BlockSpec((SUBLANES, 2 * SSM_WIDTH), fixed2),
            pl.BlockSpec((SUBLANES, SSM_WIDTH), fixed2),
            pl.BlockSpec((2, HALF_IN, 2 * HALF_ST), fixed3),
            pl.BlockSpec((2, HALF_IN, 2 * HALF_ST), fixed3),
            pl.BlockSpec((SUBLANES, 4 * HALF_ST), fixed2),
        ],
        out_shape=[
            jax.ShapeDtypeStruct((n, SSM_WIDTH), F32),
            jax.ShapeDtypeStruct((SSM_WIDTH, 2 * SSM_WIDTH), F32),
            jax.ShapeDtypeStruct((SUBLANES, 2 * SSM_WIDTH), F32),
            jax.ShapeDtypeStruct((SUBLANES, SSM_WIDTH), F32),
            jax.ShapeDtypeStruct((2, HALF_IN, 2 * HALF_ST), F32),
            jax.ShapeDtypeStruct((2, HALF_IN, 2 * HALF_ST), F32),
            jax.ShapeDtypeStruct((SUBLANES, 4 * HALF_ST), F32),
        ],
        scratch_shapes=[pltpu.VMEM((2, t_chunk, 2 * HALF_ST), F32), pltpu.VMEM((SUBLANES, 4 * HALF_ST), F32)],
        compiler_params=_cp("arbitrary", "arbitrary"),
    )(z, h, dy, bblk, cblk, glu, glub, dskip, revc)


def _gm_chunk_fwd(u, v, gain_ref, w_ref, bias_ref):
    ug, thu = _gelu(u)
    vg, thv = _gelu(v)
    rs, vns, ss = [], [], []
    for hh in range(GM_HEADS):
        cs = slice(hh * GM_HEAD_DIM, (hh + 1) * GM_HEAD_DIM)
        vn, _, r = _rms_fwd(vg[:, cs], gain_ref[:, cs])
        s = _dot(w_ref[hh], vn.astype(BF16)) + bias_ref[:, cs]
        rs.append(r)
        vns.append(vn)
        ss.append(s)
    return ug, thu, thv, vg, rs, vns, ss


def _gm_fwd(z, gain, w_tril, bias, rows, name):
    n = z.shape[0]
    chunks = rows // GM_CHUNK

    def body(u_ref, v_ref, gain_ref, w_ref, bias_ref, y_ref):
        for c in range(chunks):
            rs_ = slice(c * GM_CHUNK, (c + 1) * GM_CHUNK)
            ug, _, _, _, _, _, ss = _gm_chunk_fwd(u_ref[rs_, :], v_ref[rs_, :], gain_ref, w_ref, bias_ref)
            y_ref[rs_, :] = ug * jnp.concatenate(ss, axis=1)

    return pl.pallas_call(
        body, name=name, grid=(n // rows,),
        in_specs=[
            pl.BlockSpec((rows, GM_WIDTH), lambda i: (i, 1)),
            pl.BlockSpec((rows, GM_WIDTH), lambda i: (i, 2)),
            pl.BlockSpec((1, GM_WIDTH), lambda i: (0, 0)),
            pl.BlockSpec((GM_HEADS, GM_CHUNK, GM_CHUNK), lambda i: (0, 0, 0)),
            pl.BlockSpec((GM_CHUNK, GM_WIDTH), lambda i: (0, 0)),
        ],
        out_specs=pl.BlockSpec((rows, GM_WIDTH), lambda i: (i, 0)),
        out_shape=jax.ShapeDtypeStruct((n, GM_WIDTH), F32),
        compiler_params=_cp("parallel"),
    )(z, z, gain, w_tril, bias)


def _gm_bwd(z, dy, gain, w_tril, bias, rows, name):
    n = z.shape[0]
    chunks = rows // GM_CHUNK

    def body(u_ref, v_ref, dy_ref, gain_ref, w_ref, bias_ref, du_ref, dv_ref, dw_ref, dbias_ref, dgain_ref):
        @pl.when(pl.program_id(0) == 0)
        def _():
            dw_ref[...] = jnp.zeros_like(dw_ref)
            dbias_ref[...] = jnp.zeros_like(dbias_ref)
            dgain_ref[...] = jnp.zeros_like(dgain_ref)

        for c in range(chunks):
            rs_ = slice(c * GM_CHUNK, (c + 1) * GM_CHUNK)
            u, v = u_ref[rs_, :], v_ref[rs_, :]
            ug, thu, thv, vg, rs, vns, ss = _gm_chunk_fwd(u, v, gain_ref, w_ref, bias_ref)
            dout = dy_ref[rs_, :]
            ds = dout * ug
            du_ref[rs_, :] = dout * jnp.concatenate(ss, axis=1) * _gelu_grad(u, thu)
            dbias_ref[...] += ds
            dvg_parts, dgain_parts = [], []
            for hh in range(GM_HEADS):
                cs = slice(hh * GM_HEAD_DIM, (hh + 1) * GM_HEAD_DIM)
                dsb = ds[:, cs].astype(BF16)
                dvn = _dot_tn(w_ref[hh], dsb)
                dw_ref[hh] += _dot_nt(dsb, vns[hh].astype(BF16))
                g = gain_ref[:, cs]
                xh = vg[:, cs] * rs[hh]
                dvg, dgr = _rms_bwd(dvn, xh, rs[hh], g)
                dvg_parts.append(dvg)
                dgain_parts.append(dgr)
            dv_ref[rs_, :] = jnp.concatenate(dvg_parts, axis=1) * _gelu_grad(v, thv)
            dgain_ref[...] += _rows8(jnp.concatenate(dgain_parts, axis=1))

    row = lambda i: (i, 0)
    return pl.pallas_call(
        body, name=name, grid=(n // rows,),
        in_specs=[
            pl.BlockSpec((rows, GM_WIDTH), lambda i: (i, 1)),
            pl.BlockSpec((rows, GM_WIDTH), lambda i: (i, 2)),
            pl.BlockSpec((rows, GM_WIDTH), row),
            pl.BlockSpec((1, GM_WIDTH), lambda i: (0, 0)),
            pl.BlockSpec((GM_HEADS, GM_CHUNK, GM_CHUNK), lambda i: (0, 0, 0)),
            pl.BlockSpec((GM_CHUNK, GM_WIDTH), lambda i: (0, 0)),
        ],
        out_specs=[
            pl.BlockSpec((rows, GM_WIDTH), row), pl.BlockSpec((rows, GM_WIDTH), row),
            pl.BlockSpec((GM_HEADS, GM_CHUNK, GM_CHUNK), lambda i: (0, 0, 0)),
            pl.BlockSpec((GM_CHUNK, GM_WIDTH), lambda i: (0, 0)),
            pl.BlockSpec((SUBLANES, GM_WIDTH), lambda i: (0, 0)),
        ],
        out_shape=[
            jax.ShapeDtypeStruct((n, GM_WIDTH), F32), jax.ShapeDtypeStruct((n, GM_WIDTH), F32),
            jax.ShapeDtypeStruct((GM_HEADS, GM_CHUNK, GM_CHUNK), F32),
            jax.ShapeDtypeStruct((GM_CHUNK, GM_WIDTH), F32),
            jax.ShapeDtypeStruct((SUBLANES, GM_WIDTH), F32),
        ],
        compiler_params=_cp("arbitrary"),
    )(z, z, dy, gain, w_tril, bias)


def _loss_head(x, gain, target, tm, name):
    n = x.shape[0]

    def body(x_ref, g_ref, t_ref, dx_ref, sq_ref, dgain_ref):
        @pl.when(pl.program_id(0) == 0)
        def _():
            sq_ref[...] = jnp.zeros_like(sq_ref)
            dgain_ref[...] = jnp.zeros_like(dgain_ref)

        g = g_ref[...]
        y, xh, r = _rms_fwd(x_ref[...], g)
        err = y - t_ref[...]
        sq_ref[...] += _rows8(err * err)
        dx, dgr = _rms_bwd(err * (1.0 / D_MODEL), xh, r, g)
        dx_ref[...] = dx
        dgain_ref[...] += _rows8(dgr)

    row = lambda i: (i, 0)
    fixed = lambda i: (0, 0)
    return pl.pallas_call(
        body, name=name, grid=(n // tm,),
        in_specs=[pl.BlockSpec((tm, D_MODEL), row), pl.BlockSpec((1, D_MODEL), fixed), pl.BlockSpec((tm, D_MODEL), row)],
        out_specs=[pl.BlockSpec((tm, D_MODEL), row), pl.BlockSpec((SUBLANES, D_MODEL), fixed),
                   pl.BlockSpec((SUBLANES, D_MODEL), fixed)],
        out_shape=[jax.ShapeDtypeStruct((n, D_MODEL), F32), jax.ShapeDtypeStruct((SUBLANES, D_MODEL), F32),
                   jax.ShapeDtypeStruct((SUBLANES, D_MODEL), F32)],
        compiler_params=_cp("arbitrary"),
    )(x, gain, target)


def _adam_math(w, g, m, v):
    m2 = ADAM_B1 * m + (1.0 - ADAM_B1) * g
    v2 = ADAM_B2 * v + (1.0 - ADAM_B2) * (g * g)
    m_hat = m2 / (1.0 - ADAM_B1 ** ADAM_STEP)
    v_hat = v2 / (1.0 - ADAM_B2 ** ADAM_STEP)
    delta = -ADAM_LR * (m_hat / (jnp.sqrt(v_hat) + ADAM_EPS) + ADAM_WD * w)
    return delta, m2, v2


def _adam_sharded(parts, w, m, v, layer, earlier, name):
    depth, r, c = w.shape
    tr = max(t for t in range(16, 129, 16) if r % t == 0)

    def body(p_ref, w_ref, m_ref, v_ref, *rest):
        g_ref, d_ref, m2_ref, v2_ref = rest[-4:]
        g = p_ref[0].astype(F32)
        for s in range(1, N_DEV):
            g = g + p_ref[s].astype(F32)
        delta, m2, v2 = _adam_math(w_ref[...], g, m_ref[...], v_ref[...])
        g_ref[...] = g
        d_ref[...] = delta
        m2_ref[...] = m2
        v2_ref[...] = v2

    blk = pl.BlockSpec((None, tr, c), lambda i: (layer, i, 0))
    extra = [] if earlier is None else list(earlier)
    return pl.pallas_call(
        body, name=name, grid=(r // tr,),
        in_specs=[pl.BlockSpec((N_DEV, tr, c), lambda i: (0, i, 0)), blk, blk, blk]
        + [pl.BlockSpec(memory_space=pl.ANY)] * len(extra),
        out_specs=[blk, blk, blk, blk],
        out_shape=[jax.ShapeDtypeStruct((depth, r, c), F32)] * 4,
        input_output_aliases={4 + i: i for i in range(len(extra))},
        compiler_params=_cp("parallel"),
    )(parts, w, m, v, *extra)


def _adam_packed(g, w, m, v, name):
    r, c = g.shape

    def body(g_ref, w_ref, m_ref, v_ref, d_ref, m2_ref, v2_ref):
        delta, m2, v2 = _adam_math(w_ref[...], g_ref[...], m_ref[...], v_ref[...])
        d_ref[...] = delta
        m2_ref[...] = m2
        v2_ref[...] = v2

    blk = pl.BlockSpec((r, c), lambda i: (0, 0))
    return pl.pallas_call(
        body, name=name, grid=(1,),
        in_specs=[blk, blk, blk, blk], out_specs=[blk, blk, blk],
        out_shape=[jax.ShapeDtypeStruct((r, c), F32)] * 3,
        compiler_params=_cp("arbitrary"),
    )(g, w, m, v)


def _my_place():
    return lax.axis_index("x"), lax.axis_index("y"), lax.axis_index("c")


def _flip(place, rel):
    x, y, c = place
    return (1 - x if rel & 4 else x, 1 - y if rel & 2 else y, 1 - c if rel & 1 else c)


def _index(place):
    return 4 * place[0] + 2 * place[1] + place[2]


def _all_gather(shards, name):
    na = len(shards)

    def body(*refs):
        xs, outs = refs[:na], refs[na:2 * na]
        send_sems, recv_sems, local_sems = refs[2 * na:]
        me = _my_place()
        sibling = _flip(me, 1)
        chips = [_flip(me, 4), _flip(me, 2), _flip(me, 6)]

        def copy(a, k, block, to, src=None):
            slot = outs[a].at[_index(block)]
            return pltpu.make_async_remote_copy(
                src_ref=slot if src is None else src, dst_ref=slot,
                send_sem=send_sems.at[a, k], recv_sem=recv_sems.at[a, k],
                device_id=to, device_id_type=MESH)

        mine = [pltpu.make_async_copy(xs[a], outs[a].at[_index(me)], local_sems.at[a]) for a in range(na)]
        for cp in mine:
            cp.start()
        first = []
        for a in range(na):
            first.append(copy(a, 0, me, sibling, src=xs[a]))
            first += [copy(a, 1 + j, me, chip, src=xs[a]) for j, chip in enumerate(chips)]
        for cp in first:
            cp.start()
        passed = []
        for a in range(na):
            for j, chip in enumerate(chips):
                copy(a, 1 + j, chip, me).wait_recv()
                fwd = copy(a, 4 + j, chip, sibling)
                fwd.start()
                passed.append(fwd)
        for a in range(na):
            copy(a, 0, sibling, me).wait_recv()
            for j, chip in enumerate(chips):
                copy(a, 4 + j, _flip(chip, 1), me).wait_recv()
        for cp in first + passed:
            cp.wait_send()
        for cp in mine:
            cp.wait()

    hbm = pl.BlockSpec(memory_space=pl.ANY)
    return pl.pallas_call(
        body, name=name,
        in_specs=[hbm] * na, out_specs=[hbm] * na,
        out_shape=[jax.ShapeDtypeStruct((N_DEV,) + s.shape, s.dtype) for s in shards],
        scratch_shapes=[pltpu.SemaphoreType.DMA((na, 7)), pltpu.SemaphoreType.DMA((na, 7)),
                        pltpu.SemaphoreType.DMA((na,))],
    )(*shards)


_HBM = pl.BlockSpec(memory_space=pltpu.HBM)
_SEM = pl.BlockSpec(memory_space=pltpu.SEMAPHORE)
_EFFECT = pltpu.SideEffectType.DATAFLOW_SIDE_EFFECTING


def _exchange_copy(src_ref, land_ref, send_sems, recv_sems, a, rel, me, scatter, landed):
    peer = _flip(me, rel)
    src = src_ref.at[_index(peer)] if scatter else src_ref
    return pltpu.make_async_remote_copy(
        src_ref=src, dst_ref=land_ref.at[_index(peer if landed else me)],
        send_sem=send_sems.at[a * (N_DEV - 1) + rel - 1], recv_sem=recv_sems.at[a * (N_DEV - 1) + rel - 1],
        device_id=peer, device_id_type=MESH)


def _own_slot(data, me, scatter):
    if scatter:
        own = lax.dynamic_slice_in_dim(data, me, 1, axis=0)
        shape = data.shape
    else:
        own = data[None]
        shape = (N_DEV,) + data.shape
    start = (me,) + (0,) * (len(shape) - 1)
    return lax.dynamic_update_slice(lax.empty(shape, data.dtype), own, start)


def _exchange_start(groups, me, scatter, name, after=None):
    sizes = [len(g) for g in groups]
    srcs = [a for g in groups for a in g]
    lands = [_own_slot(a, me, scatter) for a in srcs]
    na, ng = len(srcs), len(groups)
    deps = [] if after is None else [after]

    def body(*refs):
        src_refs, land_refs = refs[:na], refs[na:2 * na]
        sems = refs[2 * na + len(deps):2 * na + len(deps) + 2 * ng]
        token = refs[-1]
        place = _my_place()
        a = 0
        for g, size in enumerate(sizes):
            for k in range(size):
                for rel in range(1, N_DEV):
                    _exchange_copy(src_refs[a], land_refs[a], sems[2 * g], sems[2 * g + 1], k, rel, place, scatter,
                                   False).start()
                a += 1
        token[...] = jnp.zeros_like(token)

    sem_shapes = [pltpu.SemaphoreType.DMA((size * (N_DEV - 1),)) for size in sizes for _ in range(2)]
    outs = pl.pallas_call(
        body, name=name,
        in_specs=[_HBM] * (2 * na) + [pl.BlockSpec(memory_space=pl.ANY)] * len(deps),
        out_specs=[_SEM] * (2 * ng) + [_HBM] * (2 * na) + [pl.BlockSpec(memory_space=pltpu.VMEM)],
        out_shape=sem_shapes + [pltpu.HBM(a.shape, a.dtype) for a in srcs + lands]
        + [jax.ShapeDtypeStruct((SUBLANES, LANES), F32)],
        input_output_aliases={i: 2 * ng + i for i in range(2 * na)},
        compiler_params=pltpu.CompilerParams(has_side_effects=_EFFECT),
    )(*[pltpu.with_memory_space_constraint(a, pltpu.HBM) for a in srcs + lands], *deps)
    sems, thru, token = outs[:2 * ng], outs[2 * ng:2 * ng + 2 * na], outs[-1]
    handles, a = [], 0
    for g, size in enumerate(sizes):
        handles.append((sems[2 * g], sems[2 * g + 1], thru[a:a + size], thru[na + a:na + a + size]))
        a += size
    return handles, token


def _exchange_wait(handle, after, scatter, name):
    send_sems, recv_sems, srcs, lands = handle
    na = len(srcs)

    def body(*refs):
        src_refs, land_refs = refs[:na], refs[na:2 * na]
        send_ref, recv_ref = refs[2 * na], refs[2 * na + 1]
        place = _my_place()
        for a in range(na):
            for rel in range(1, N_DEV):
                cp = _exchange_copy(src_refs[a], land_refs[a], send_ref, recv_ref, a, rel, place, scatter, True)
                cp.wait_send()
                cp.wait_recv()

    outs = pl.pallas_call(
        body, name=name,
        in_specs=[_HBM] * (2 * na) + [_SEM, _SEM, pl.BlockSpec(memory_space=pl.ANY)],
        out_specs=[_HBM] * (2 * na),
        out_shape=[pltpu.HBM(a.shape, a.dtype) for a in list(srcs) + list(lands)],
        input_output_aliases={i: i for i in range(2 * na)},
        compiler_params=pltpu.CompilerParams(has_side_effects=_EFFECT),
    )(*srcs, *lands, send_sems, recv_sems, after)
    return outs[na:]


def _behind(arr, token):
    return arr + token[0:1, 0:1]


def _all_reduce_small(g, name):
    _, r, c = g.shape

    def body(g_ref, o_ref, land_ref, red_ref, send1, recv1, send2, recv2):
        me = _my_place()
        idx = _index(me)

        def scatter(rel):
            peer = _flip(me, rel)
            return pltpu.make_async_remote_copy(
                src_ref=g_ref.at[_index(peer)], dst_ref=land_ref.at[idx],
                send_sem=send1.at[rel - 1], recv_sem=recv1.at[rel - 1], device_id=peer, device_id_type=MESH)

        def gather(rel):
            peer = _flip(me, rel)
            return pltpu.make_async_remote_copy(
                src_ref=red_ref, dst_ref=o_ref.at[idx],
                send_sem=send2.at[rel - 1], recv_sem=recv2.at[rel - 1], device_id=peer, device_id_type=MESH)

        for rel in range(1, N_DEV):
            scatter(rel).start()
        land_ref[idx] = g_ref[idx]
        for rel in range(1, N_DEV):
            scatter(rel).wait()
        acc = land_ref[0]
        for s in range(1, N_DEV):
            acc = acc + land_ref[s]
        red_ref[...] = acc
        for rel in range(1, N_DEV):
            gather(rel).start()
        o_ref[idx] = acc
        for rel in range(1, N_DEV):
            gather(rel).wait()

    vmem = pl.BlockSpec(memory_space=pltpu.VMEM)
    return pl.pallas_call(
        body, name=name,
        in_specs=[vmem], out_specs=vmem,
        out_shape=jax.ShapeDtypeStruct(g.shape, F32),
        scratch_shapes=[pltpu.VMEM(g.shape, F32), pltpu.VMEM((r, c), F32)]
        + [pltpu.SemaphoreType.DMA((N_DEV - 1,))] * 4,
        compiler_params=pltpu.CompilerParams(vmem_limit_bytes=VMEM_LIMIT),
    )(g)


def _ssm_discretize(a_re, a_im, log_dt, b_re, b_im):
    dt = jnp.exp(log_dt)[:, None]
    mag = jnp.exp(a_re * dt)
    lr, li = mag * jnp.cos(a_im * dt), mag * jnp.sin(a_im * dt)
    den = a_re * a_re + a_im * a_im
    qr = ((lr - 1.0) * a_re + li * a_im) / den
    qi = (li * a_re - (lr - 1.0) * a_im) / den
    bbr = qr[..., None] * b_re - qi[..., None] * b_im
    bbi = qr[..., None] * b_im + qi[..., None] * b_re
    return lr, li, bbr, bbi


def _halves(a):
    return a.reshape((2, HALF_GROUPS) + a.shape[1:])


def _block_diag_mask(g, r, c):
    rows = lax.broadcasted_iota(jnp.int32, (g * r, g * c), 0) // r
    cols = lax.broadcasted_iota(jnp.int32, (g * r, g * c), 1) // c
    return rows == cols


def _block_diag(blocks):
    g, r, c = blocks.shape
    spread = jnp.tile(jnp.eye(c, dtype=blocks.dtype), (1, g))
    full = jnp.dot(blocks.reshape(g * r, c), spread, precision=lax.Precision.HIGHEST)
    return jnp.where(_block_diag_mask(g, r, c), full, 0.0)


def _block_diag_take(dense, g, r, c):
    gather = jnp.tile(jnp.eye(c, dtype=dense.dtype), (g, 1))
    kept = jnp.where(_block_diag_mask(g, r, c), dense, 0.0)
    return jnp.dot(kept, gather, precision=lax.Precision.HIGHEST).reshape(g, r, c)


def _ssm_matrices(bbr, bbi, c_re, c_im, glu_w, glu_b, d_skip):
    bre, bim = _halves(jnp.swapaxes(bbr, 1, 2)), _halves(jnp.swapaxes(bbi, 1, 2))
    bblk = jnp.stack([jnp.concatenate([_block_diag(bre[h]), _block_diag(bim[h])], axis=1) for h in range(2)])
    cre, cim = _halves(jnp.swapaxes(c_re, 1, 2)), _halves(jnp.swapaxes(c_im, 1, 2))
    cblk = jnp.stack([jnp.concatenate([_block_diag(cre[h]), -_block_diag(cim[h])], axis=0) for h in range(2)])
    glu = jnp.concatenate([_block_diag(glu_w[:, :, :SSM_CH]), _block_diag(glu_w[:, :, SSM_CH:])], axis=1)
    glub = jnp.concatenate([glu_b[:, :SSM_CH].reshape(1, -1), glu_b[:, SSM_CH:].reshape(1, -1)], axis=1)
    return bblk.astype(BF16), cblk.astype(BF16), glu.astype(BF16), glub, d_skip.reshape(1, -1)


def _scan_constants(lr, li, reverse):
    if reverse:
        li = -li
    pows = [(lr, li)]
    for _ in range(SUBLANES - 1):
        pr, pi = pows[-1]
        pows.append((pr * lr - pi * li, pr * li + pi * lr))
    row = jnp.arange(SUBLANES)[:, None]

    def flat(a):
        return a.reshape(2, 1, HALF_ST)

    mats = []
    for s in (1, 2, 4):
        keep = (row + s <= SUBLANES - 1) if reverse else (row >= s)
        mats.append(tuple(jnp.where(keep[None], flat(p), 0.0) for p in pows[s - 1]))
    order = [SUBLANES - 1 - j for j in range(SUBLANES)] if reverse else list(range(SUBLANES))
    mats.append(tuple(jnp.concatenate([flat(pows[j][k]) for j in order], axis=1) for k in range(2)))
    return jnp.stack([jnp.concatenate([m[0], m[1]], axis=2) for m in mats], axis=1)


def _pack(arrs, rows):
    flat = jnp.concatenate([a.reshape(-1) for a in arrs])
    return jnp.pad(flat, (0, rows * LANES - flat.shape[0])).reshape(rows, LANES)


def _unpack(buf, like):
    flat = buf.reshape(-1)
    out, off = [], 0
    for a in like:
        out.append(flat[off:off + a.size].reshape(a.shape))
        off += a.size
    return out


SMALL = ("norm_ffn1", "norm_mix", "ssm_a_re", "ssm_a_im", "ssm_log_dt", "ssm_b_re", "ssm_b_im", "ssm_c_re",
         "ssm_c_im", "ssm_d", "ssm_glu_w", "ssm_glu_b", "gm_v_gain", "gm_w_s", "gm_b_s", "gain_ssm_out",
         "gain_gm_out", "norm_ffn2", "norm_final")
BIG = ("ffn1_w_in", "ffn1_w_out", "mix_w_in", "mix_w_out", "ffn2_w_in", "ffn2_w_out")
TRANSPOSED = ("ffn1_w_in", "mix_w_in", "ffn2_w_in")
WEIGHTS = ("norm_ffn1", "ffn1_w_in", "ffn1_w_out", "norm_mix", "mix_w_in", "ssm_a_re", "ssm_a_im", "ssm_log_dt",
           "ssm_b_re", "ssm_b_im", "ssm_c_re", "ssm_c_im", "ssm_d", "ssm_glu_w", "ssm_glu_b", "gm_v_gain", "gm_w_s",
           "gm_b_s", "gain_ssm_out", "gain_gm_out", "mix_w_out", "norm_ffn2", "ffn2_w_in", "ffn2_w_out", "norm_final")


def _step(x, target, w, m, v):
    batch, seq, _ = x.shape
    n = batch * seq
    depth = w["norm_ffn1"].shape[0]
    tm = min(512, n)
    tm_ffn = min(1024, n)
    tk = min(4096, n)
    t_chunk = min(256, seq)
    gm_rows = min(512, seq)
    x = x.reshape(n, D_MODEL)
    target = target.reshape(n, D_MODEL)

    assert depth == 2
    me = _index(_my_place())
    shard = lambda group, l: [w[f"{group}_w_in"][l].astype(BF16), w[f"{group}_w_out"][l].astype(BF16)]
    batches = ([("mix", 0), ("ffn2", 0)], [("ffn1", 1), ("mix", 1)], [("ffn2", 1)])
    gathered, pending = {("ffn1", 0): tuple(_all_gather(shard("ffn1", 0), "all_gather_first"))}, {}

    def gather_start(i, after):
        handles, tok = _exchange_start([shard(g, l) for g, l in batches[i]], me, False, f"all_gather_start_{i}",
                                       after)
        pending.update(zip(batches[i], handles))
        return tok

    def weights(group, l, after=None):
        if (group, l) not in gathered:
            w_in, w_out = _exchange_wait(pending[(group, l)], after, False, f"all_gather_wait_{group}_{l}")
            if group == "mix":
                w_in = jnp.transpose(w_in, (1, 0, 2)).reshape(D_MODEL, IN_COLS)
                w_out = w_out.reshape(D_MODEL, D_MODEL)
            gathered[(group, l)] = (w_in, w_out)
        return gathered[(group, l)]

    tril = jnp.tril(jnp.ones((GM_CHUNK, GM_CHUNK), bool))
    layers = []
    for l in range(depth):
        disc, disc_vjp = jax.vjp(_ssm_discretize, w["ssm_a_re"][l], w["ssm_a_im"][l], w["ssm_log_dt"][l],
                                 w["ssm_b_re"][l], w["ssm_b_im"][l])
        lr, li, bbr, bbi = disc
        bblk, cblk, glu, glub, dskip = _ssm_matrices(bbr, bbi, w["ssm_c_re"][l], w["ssm_c_im"][l],
                                                     w["ssm_glu_w"][l], w["ssm_glu_b"][l], w["ssm_d"][l])
        layers.append(dict(
            disc_vjp=disc_vjp, lr=lr, li=li, bblk=bblk, cblk=cblk, glu=glu, glub=glub, dskip=dskip,
            fwdc=_scan_constants(lr, li, False), revc=_scan_constants(lr, li, True),
            w_tril=jnp.where(tril[None], w["gm_w_s"][l], 0.0).astype(BF16),
            gm_bias=jnp.repeat(w["gm_b_s"][l].T, GM_HEAD_DIM, axis=1),
            g1=w["norm_ffn1"][l][None], gmix=w["norm_mix"][l][None], g2=w["norm_ffn2"][l][None],
            gv=w["gm_v_gain"][l][None], gs=w["gain_ssm_out"][l][None], gg=w["gain_gm_out"][l][None],
        ))

    saved = []
    for l in range(depth):
        p = layers[l]
        x0 = x
        g1, gmix, g2 = p["g1"], p["gmix"], p["g2"]
        w_in, w_out = weights("ffn1", l, x0)
        if l == 0:
            g1 = _behind(g1, gather_start(0, w_in))
        x1, xn1, gu1 = _ffn_fwd(x0, g1, w_in, w_out, tm_ffn, f"ffn1_fwd_{l}")
        if l == 0:
            gmix = _behind(gmix, gather_start(1, x1))
        mwi, mwo = weights("mix", l, x1)
        z = _mix_in_fwd(x1, gmix, mwi, tm, f"mix_in_fwd_{l}")
        y_ssm, h = _ssm_fwd(z, p["bblk"], p["cblk"], p["glu"], p["glub"], p["dskip"], p["fwdc"], batch, t_chunk,
                            f"ssm_fwd_{l}")
        y_gm = _gm_fwd(z, p["gv"], p["w_tril"], p["gm_bias"], gm_rows, f"gm_fwd_{l}")
        x2 = _mix_out_fwd(y_ssm, y_gm, p["gs"], p["gg"], mwo, x1, tm, f"mix_out_fwd_{l}")
        if l == 0:
            g2 = _behind(g2, gather_start(2, x2))
        x, xn2, gu2 = _ffn_fwd(x2, g2, *weights("ffn2", l, x2), tm_ffn, f"ffn2_fwd_{l}")
        saved.append((x0, x1, x2, z, h, y_ssm, y_gm, xn1, gu1, xn2, gu2))

    dx, sq, dnf = _loss_head(x, w["norm_final"][None], target, tm, "loss_head")
    loss = lax.psum((0.5 / D_MODEL) * jnp.sum(sq), AXES)

    small = {k: [None] * depth for k in SMALL if k != "norm_final"}
    sent = []

    def send(group, l, keys, parts):
        (handle,), tok = _exchange_start([parts], me, True, f"reduce_scatter_start_{group}_{l}")
        sent.append((group, l, keys, handle))
        return tok

    token = None
    for l in reversed(range(depth)):
        p = layers[l]
        x0, x1, x2, z, h, y_ssm, y_gm, xn1, gu1, xn2, gu2 = saved[l]
        mwi, mwo = weights("mix", l)
        dx_out = dx
        g2 = p["g2"] if token is None else _behind(p["g2"], token)
        dx, dgu, act, dgain, dyb = _ffn_bwd(x2, g2, dx_out, gu2, *weights("ffn2", l), tm, f"ffn2_bwd_{l}")
        dw_in = _ffn_dw_in(xn2, dgu, tk, f"ffn2_dw_in_{l}")
        dw_out = _ffn_dw_out(act, dyb, tk, f"ffn2_dw_out_{l}").reshape(N_DEV, FF_SHARD // 2, D_MODEL)
        token = send("ffn2", l, ("ffn2_w_in", "ffn2_w_out"), [dw_in, dw_out])
        small["norm_ffn2"][l] = dgain.sum(0)

        dy_ssm, dy_gm, dwo, dgs, dgg = _mix_out_bwd(y_ssm, y_gm, _behind(p["gs"], token), p["gg"], mwo, dx, tm,
                                                    f"mix_out_bwd_{l}")
        dwo = dwo.astype(BF16).reshape(N_DEV, D_MODEL // N_DEV, D_MODEL)
        small["gain_ssm_out"][l] = dgs.sum(0)
        small["gain_gm_out"][l] = dgg.sum(0)

        du_ssm, dglu, dglub, ddskip, dct, db, q = _ssm_bwd(
            z, h, dy_ssm, p["bblk"], p["cblk"], p["glu"], p["glub"], p["dskip"], p["revc"], batch, t_chunk,
            f"ssm_bwd_{l}")
        du_gm, dv_gm, dws, dbias, dgv = _gm_bwd(z, dy_gm, p["gv"], p["w_tril"], p["gm_bias"], gm_rows, f"gm_bwd_{l}")

        q = q.sum(0).reshape(2, 2, HALF_GROUPS, SSM_STATE)
        qr, qi = q[:, 0].reshape(SSM_GROUPS, SSM_STATE), q[:, 1].reshape(SSM_GROUPS, SSM_STATE)
        den = p["lr"] * p["lr"] + p["li"] * p["li"]
        d_re = (qr * p["lr"] + qi * p["li"]) / den
        d_im = (qi * p["lr"] - qr * p["li"]) / den
        dbb = jnp.stack([_block_diag_take(db[hf, :, k * HALF_ST:(k + 1) * HALF_ST], HALF_GROUPS, SSM_CH, SSM_STATE)
                         for k in range(2) for hf in range(2)]).reshape(2, SSM_GROUPS, SSM_CH, SSM_STATE)
        dcc = jnp.stack([_block_diag_take(dct[hf, :, k * HALF_ST:(k + 1) * HALF_ST], HALF_GROUPS, SSM_CH, SSM_STATE)
                         for k in range(2) for hf in range(2)]).reshape(2, SSM_GROUPS, SSM_CH, SSM_STATE)
        da_re, da_im, dlog_dt, db_re, db_im = p["disc_vjp"](
            (d_re, -d_im, jnp.swapaxes(dbb[0], 1, 2), jnp.swapaxes(dbb[1], 1, 2)))
        small["ssm_a_re"][l], small["ssm_a_im"][l], small["ssm_log_dt"][l] = da_re, da_im, dlog_dt
        small["ssm_b_re"][l], small["ssm_b_im"][l] = db_re, db_im
        small["ssm_c_re"][l], small["ssm_c_im"][l] = dcc[0], -dcc[1]
        small["ssm_d"][l] = ddskip.sum(0).reshape(SSM_GROUPS, SSM_CH)
        small["ssm_glu_w"][l] = jnp.concatenate(
            [_block_diag_take(dglu[:, :SSM_WIDTH], SSM_GROUPS, SSM_CH, SSM_CH),
             _block_diag_take(dglu[:, SSM_WIDTH:], SSM_GROUPS, SSM_CH, SSM_CH)], axis=2)
        dglub = dglub.sum(0)
        small["ssm_glu_b"][l] = jnp.concatenate(
            [dglub[:SSM_WIDTH].reshape(SSM_GROUPS, SSM_CH), dglub[SSM_WIDTH:].reshape(SSM_GROUPS, SSM_CH)], axis=1)
        small["gm_v_gain"][l] = dgv.sum(0)
        small["gm_w_s"][l] = jnp.where(tril[None], dws, 0.0)
        small["gm_b_s"][l] = dbias.reshape(GM_CHUNK, GM_HEADS, GM_HEAD_DIM).sum(-1).T

        dx, dwi, dgain = _mix_in_bwd(x1, p["gmix"], du_ssm, du_gm, dv_gm, dx, mwi, tm, f"mix_in_bwd_{l}")
        dwi = dwi.astype(BF16).reshape(N_DEV, IN_COLS // N_DEV, D_MODEL)
        token = send("mix", l, ("mix_w_in", "mix_w_out"), [dwi, dwo])
        small["norm_mix"][l] = dgain.sum(0)

        dx_out = dx
        dx, dgu, act, dgain, dyb = _ffn_bwd(x0, _behind(p["g1"], token), dx_out, gu1, *weights("ffn1", l), tm,
                                       f"ffn1_bwd_{l}")
        small["norm_ffn1"][l] = dgain.sum(0)
        if l > 0:
            dw_in = _ffn_dw_in(xn1, dgu, tk, f"ffn1_dw_in_{l}")
            dw_out = _ffn_dw_out(act, dyb, tk, f"ffn1_dw_out_{l}").reshape(N_DEV, FF_SHARD // 2, D_MODEL)
            token = send("ffn1", l, ("ffn1_w_in", "ffn1_w_out"), [dw_in, dw_out])
            continue
        small_g = [jnp.stack(small[k]) if k != "norm_final" else dnf.sum(0) for k in SMALL]
        total = sum(int(math.prod(w[k].shape)) for k in SMALL)
        rows = -(-total // (LANES * N_DEV * SUBLANES)) * N_DEV * SUBLANES
        g_all = _all_reduce_small(_pack(small_g, rows).reshape(N_DEV, rows // N_DEV, LANES), "all_reduce_small")
        dw_in = _ffn_dw_in(xn1, dgu, tk, f"ffn1_dw_in_{l}", after=g_all)
        token = send("ffn1_in", l, ("ffn1_w_in",), [dw_in])
        dw_out = _ffn_dw_out(act, dyb, tk, f"ffn1_dw_out_{l}", after=token).reshape(
            N_DEV, FF_SHARD // 2, D_MODEL)
        token = send("ffn1_out", l, ("ffn1_w_out",), [dw_out])

    grad_x = dx.reshape(batch, seq, D_MODEL)
    grads, deltas, new_m, new_v = {}, {}, {}, {}

    g_all = _behind(g_all.reshape(rows, LANES), token)
    like = [w[k] for k in SMALL]
    d_p, m_p, v_p = _adam_packed(g_all, _pack(like, rows), _pack([m[k] for k in SMALL], rows),
                                 _pack([v[k] for k in SMALL], rows), "adam_small")
    for k, g_, d_, m_, v_ in zip(SMALL, _unpack(g_all, like), _unpack(d_p, like), _unpack(m_p, like),
                                 _unpack(v_p, like)):
        grads[k], deltas[k], new_m[k], new_v[k] = g_, d_, m_, v_

    results = {}
    after = d_p
    for group, l, keys, handle in sent:
        landed = _exchange_wait(handle, after, True, f"reduce_scatter_wait_{group}_{l}")
        for k, parts in zip(keys, landed):
            view = (lambda a: jnp.swapaxes(a, 1, 2)) if k in TRANSPOSED else (lambda a: a)
            results[k] = _adam_sharded(parts, view(w[k]), view(m[k]), view(v[k]), l, results.get(k),
                                       f"adam_{k}_{l}")
            after = results[k][0]
    for k in BIG:
        view = (lambda a: jnp.swapaxes(a, 1, 2)) if k in TRANSPOSED else (lambda a: a)
        grads[k], deltas[k], new_m[k], new_v[k] = [view(a) for a in results[k]]
    return loss, grad_x, grads, deltas, new_m, new_v


def kernel(x, norm_ffn1, ffn1_w_in, ffn1_w_out, norm_mix, mix_w_in, ssm_a_re, ssm_a_im, ssm_log_dt, ssm_b_re, ssm_b_im, ssm_c_re, ssm_c_im, ssm_d, ssm_glu_w, ssm_glu_b, gm_v_gain, gm_w_s, gm_b_s, gain_ssm_out, gain_gm_out, mix_w_out, norm_ffn2, ffn2_w_in, ffn2_w_out, norm_final, loss_target, m_norm_ffn1, m_ffn1_w_in, m_ffn1_w_out, m_norm_mix, m_mix_w_in, m_ssm_a_re, m_ssm_a_im, m_ssm_log_dt, m_ssm_b_re, m_ssm_b_im, m_ssm_c_re, m_ssm_c_im, m_ssm_d, m_ssm_glu_w, m_ssm_glu_b, m_gm_v_gain, m_gm_w_s, m_gm_b_s, m_gain_ssm_out, m_gain_gm_out, m_mix_w_out, m_norm_ffn2, m_ffn2_w_in, m_ffn2_w_out, m_norm_final, v_norm_ffn1, v_ffn1_w_in, v_ffn1_w_out, v_norm_mix, v_mix_w_in, v_ssm_a_re, v_ssm_a_im, v_ssm_log_dt, v_ssm_b_re, v_ssm_b_im, v_ssm_c_re, v_ssm_c_im, v_ssm_d, v_ssm_glu_w, v_ssm_glu_b, v_gm_v_gain, v_gm_w_s, v_gm_b_s, v_gain_ssm_out, v_gain_gm_out, v_mix_w_out, v_norm_ffn2, v_ffn2_w_in, v_ffn2_w_out, v_norm_final):
    args = locals()
    w = {k: args[k] for k in WEIGHTS}
    m = {k: args["m_" + k] for k in WEIGHTS}
    v = {k: args["v_" + k] for k in WEIGHTS}
    loss, grad_x, grads, deltas, new_m, new_v = _step(x, loss_target, w, m, v)
    return (loss, grad_x, *[grads[k] for k in WEIGHTS], *[deltas[k] for k in WEIGHTS],
            *[new_m[k] for k in WEIGHTS], *[new_v[k] for k in WEIGHTS])
```

```python
import functools
import math

import jax
import jax.numpy as jnp
from jax import lax
from jax.experimental import pallas as pl
from jax.experimental.pallas import tpu as pltpu

F32 = jnp.float32
BF16 = jnp.bfloat16
MESH = pl.DeviceIdType.MESH
AXES = ("x", "y", "c")

N_DEV = 8
D_MODEL = 1024
D_FF = 2816
FF_SHARD = 2 * D_FF // N_DEV
FF_CHUNKS = 4
MXU_DIM = 256
FF_PIECES = tuple((lo, min(lo + MXU_DIM, FF_SHARD)) for lo in range(0, FF_SHARD, MXU_DIM))
SSM_WIDTH = 512
SSM_CH = 16
SSM_GROUPS = 32
SSM_STATE = 64
HALF_GROUPS = 16
HALF_IN = HALF_GROUPS * SSM_CH
HALF_ST = HALF_GROUPS * SSM_STATE
GM_WIDTH = 512
GM_HEADS = 4
GM_HEAD_DIM = 128
GM_CHUNK = 128
IN_COLS = SSM_WIDTH + 2 * GM_WIDTH
EPS = 1e-6
SUBLANES = 8
LANES = 128

ADAM_LR = 0.001
ADAM_B1 = 0.9
ADAM_B2 = 0.999
ADAM_EPS = 1e-08
ADAM_WD = 0.01
ADAM_STEP = 10

VMEM_LIMIT = 46 * 1024 * 1024


def _cp(*sem):
    return pltpu.CompilerParams(dimension_semantics=sem, vmem_limit_bytes=VMEM_LIMIT)


def _rms_fwd(x, g):
    r = lax.rsqrt(jnp.mean(x * x, axis=-1, keepdims=True) + EPS)
    xh = x * r
    return xh * g, xh, r


def _rms_bwd(dy, xh, r, g):
    dxh = dy * g
    dx = r * (dxh - xh * jnp.mean(dxh * xh, axis=-1, keepdims=True))
    return dx, dy * xh


def _rows8(a):
    m, n = a.shape
    return a.reshape(m // SUBLANES, SUBLANES, n).sum(axis=0)


_GELU_K = math.sqrt(2.0 / math.pi)
_GELU_C = 0.044715


def _gelu(x):
    th = jnp.tanh(_GELU_K * (x + _GELU_C * x * x * x))
    return 0.5 * x * (1.0 + th), th


def _gelu_grad(x, th):
    return 0.5 * (1.0 + th) + 0.5 * x * (1.0 - th * th) * (_GELU_K * (1.0 + 3.0 * _GELU_C * x * x))


def _dot(a, b):
    return jnp.dot(a, b, preferred_element_type=F32)


def _dot_nt(a, b):
    return lax.dot_general(a, b, (((1,), (1,)), ((), ())), preferred_element_type=F32)


def _dot_tn(a, b):
    return lax.dot_general(a, b, (((0,), (0,)), ((), ())), preferred_element_type=F32)


def _ffn_fwd(x, gain, w_in_ag, w_out_ag, tm, name):
    n = x.shape[0]

    def body(x_ref, g_ref, wg_ref, wu_ref, wo_ref, o_ref, xn_ref, gu_ref):
        j = pl.program_id(1)

        @pl.when(j == 0)
        def _():
            xv = x_ref[...]
            y, _, _ = _rms_fwd(xv, g_ref[...])
            xn_ref[...] = y.astype(BF16)
            o_ref[...] = xv

        xn = xn_ref[...]
        wo = wo_ref[...].reshape(FF_SHARD, D_MODEL)
        out = None
        for lo, hi in FF_PIECES:
            gg = _dot(xn, wg_ref[:, lo:hi])
            uu = _dot(xn, wu_ref[:, lo:hi])
            gu_ref[0, :, lo:hi] = gg.astype(BF16)
            gu_ref[1, :, lo:hi] = uu.astype(BF16)
            act = (gg * jax.nn.sigmoid(gg) * uu).astype(BF16)
            part = _dot(act, wo[lo:hi, :])
            out = part if out is None else out + part
        o_ref[...] += 0.5 * out

    return pl.pallas_call(
        body, name=name, grid=(n // tm, FF_CHUNKS),
        in_specs=[
            pl.BlockSpec((tm, D_MODEL), lambda i, j: (i, 0)),
            pl.BlockSpec((1, D_MODEL), lambda i, j: (0, 0)),
            pl.BlockSpec((None, D_MODEL, FF_SHARD), lambda i, j: (j, 0, 0)),
            pl.BlockSpec((None, D_MODEL, FF_SHARD), lambda i, j: (j + FF_CHUNKS, 0, 0)),
            pl.BlockSpec((2, FF_SHARD // 2, D_MODEL), lambda i, j: (j, 0, 0)),
        ],
        out_specs=[
            pl.BlockSpec((tm, D_MODEL), lambda i, j: (i, 0)),
            pl.BlockSpec((tm, D_MODEL), lambda i, j: (i, 0)),
            pl.BlockSpec((None, 2, tm, FF_SHARD), lambda i, j: (j, 0, i, 0)),
        ],
        out_shape=[
            jax.ShapeDtypeStruct((n, D_MODEL), F32),
            jax.ShapeDtypeStruct((n, D_MODEL), BF16),
            jax.ShapeDtypeStruct((FF_CHUNKS, 2, n, FF_SHARD), BF16),
        ],
        compiler_params=_cp("parallel", "arbitrary"),
    )(x, gain, w_in_ag, w_in_ag, w_out_ag)


def _ffn_bwd(x, gain, dy, gu, w_in_ag, w_out_ag, tm, name):
    n = x.shape[0]
    work = (n // tm) * FF_CHUNKS
    steps = work + 1
    first = lambda s: jnp.minimum(s, work - 1)
    second = lambda s: jnp.maximum(s - 1, 0)
    tile1, chunk1 = (lambda s: first(s) // FF_CHUNKS), (lambda s: first(s) % FF_CHUNKS)
    tile2, chunk2 = (lambda s: second(s) // FF_CHUNKS), (lambda s: second(s) % FF_CHUNKS)

    def body(x_ref, g_ref, dy_ref, dy1_ref, gu_ref, wg_ref, wu_ref, wo_ref, dx_ref, dgu_ref, act_ref, dgain_ref,
             dyb_ref, held_ref):
        s = pl.program_id(0)

        @pl.when(s == 0)
        def _():
            dgain_ref[...] = jnp.zeros_like(dgain_ref)
            held_ref[...] = jnp.zeros_like(held_ref)

        @pl.when(jnp.logical_and(chunk1(s) == 0, s < work))
        def _():
            dyb_ref[...] = (0.5 * dy1_ref[...]).astype(BF16)

        @pl.when(chunk2(s) == 0)
        def _():
            dx_ref[...] = jnp.zeros_like(dx_ref)

        held = held_ref[1 - s % 2]
        dx_ref[...] += _dot_nt(held[0], wg_ref[...]) + _dot_nt(held[1], wu_ref[...])

        dyb = dyb_ref[...]
        wo = wo_ref[...].reshape(FF_SHARD, D_MODEL)
        slot = s % 2
        for lo, hi in FF_PIECES:
            gg = gu_ref[0, :, lo:hi].astype(F32)
            uu = gu_ref[1, :, lo:hi].astype(F32)
            dact = _dot_nt(dyb, wo[lo:hi, :])
            sig = jax.nn.sigmoid(gg)
            silu = gg * sig
            act_ref[:, lo:hi] = (silu * uu).astype(BF16)
            du = (dact * silu).astype(BF16)
            dg = (dact * uu * (sig * (1.0 + gg * (1.0 - sig)))).astype(BF16)
            dgu_ref[0, :, lo:hi] = dg
            dgu_ref[1, :, lo:hi] = du
            held_ref[slot, 0, :, lo:hi] = dg
            held_ref[slot, 1, :, lo:hi] = du

        @pl.when(jnp.logical_and(chunk2(s) == FF_CHUNKS - 1, s > 0))
        def _():
            g = g_ref[...]
            _, xh, r = _rms_fwd(x_ref[...], g)
            dx, dgr = _rms_bwd(dx_ref[...], xh, r, g)
            dx_ref[...] = dy_ref[...] + dx
            dgain_ref[...] += _rows8(dgr)

    return pl.pallas_call(
        body, name=name, grid=(steps,),
        in_specs=[
            pl.BlockSpec((tm, D_MODEL), lambda s: (tile2(s), 0)),
            pl.BlockSpec((1, D_MODEL), lambda s: (0, 0)),
            pl.BlockSpec((tm, D_MODEL), lambda s: (tile2(s), 0)),
            pl.BlockSpec((tm, D_MODEL), lambda s: (tile1(s), 0)),
            pl.BlockSpec((None, 2, tm, FF_SHARD), lambda s: (chunk1(s), 0, tile1(s), 0)),
            pl.BlockSpec((None, D_MODEL, FF_SHARD), lambda s: (chunk2(s), 0, 0)),
            pl.BlockSpec((None, D_MODEL, FF_SHARD), lambda s: (chunk2(s) + FF_CHUNKS, 0, 0)),
            pl.BlockSpec((2, FF_SHARD // 2, D_MODEL), lambda s: (chunk1(s), 0, 0)),
        ],
        out_specs=[
            pl.BlockSpec((tm, D_MODEL), lambda s: (tile2(s), 0)),
            pl.BlockSpec((None, 2, tm, FF_SHARD), lambda s: (chunk1(s), 0, tile1(s), 0)),
            pl.BlockSpec((None, tm, FF_SHARD), lambda s: (chunk1(s), tile1(s), 0)),
            pl.BlockSpec((SUBLANES, D_MODEL), lambda s: (0, 0)),
            pl.BlockSpec((tm, D_MODEL), lambda s: (tile1(s), 0)),
        ],
        out_shape=[
            jax.ShapeDtypeStruct((n, D_MODEL), F32),
            jax.ShapeDtypeStruct((FF_CHUNKS, 2, n, FF_SHARD), BF16),
            jax.ShapeDtypeStruct((FF_CHUNKS, n, FF_SHARD), BF16),
            jax.ShapeDtypeStruct((SUBLANES, D_MODEL), F32),
            jax.ShapeDtypeStruct((n, D_MODEL), BF16),
        ],
        scratch_shapes=[pltpu.VMEM((2, 2, tm, FF_SHARD), BF16)],
        compiler_params=_cp("arbitrary"),
    )(x, gain, dy, dy, gu, w_in_ag, w_in_ag, w_out_ag)


def _ffn_dw_in(xn, dgu, tk, name, after=None):
    n = xn.shape[0]
    nk = n // tk
    deps = [] if after is None else [after]

    def body(a_ref, b_ref, *rest):
        o_ref, acc_ref = rest[-2:]
        k = pl.program_id(2)

        @pl.when(k == 0)
        def _():
            acc_ref[...] = jnp.zeros_like(acc_ref)

        acc_ref[...] += _dot_tn(b_ref[...], a_ref[...])

        @pl.when(k == nk - 1)
        def _():
            o_ref[...] = acc_ref[...].astype(BF16)

    return pl.pallas_call(
        body, name=name, grid=(FF_CHUNKS, 2, nk),
        in_specs=[
            pl.BlockSpec((tk, D_MODEL), lambda j, p, k: (k, 0)),
            pl.BlockSpec((None, None, tk, FF_SHARD), lambda j, p, k: (j, p, k, 0)),
        ] + [pl.BlockSpec(memory_space=pl.ANY)] * len(deps),
        out_specs=pl.BlockSpec((None, FF_SHARD, D_MODEL), lambda j, p, k: (FF_CHUNKS * p + j, 0, 0)),
        out_shape=jax.ShapeDtypeStruct((N_DEV, FF_SHARD, D_MODEL), BF16),
        scratch_shapes=[pltpu.VMEM((FF_SHARD, D_MODEL), F32)],
        compiler_params=_cp("parallel", "parallel", "arbitrary"),
    )(xn, dgu, *deps)


def _ffn_dw_out(act, dyb, tk, name, after=None):
    n = act.shape[1]
    nk = n // tk
    deps = [] if after is None else [after]

    def body(a_ref, b_ref, *rest):
        o_ref, acc_ref = rest[-2:]
        k = pl.program_id(1)

        @pl.when(k == 0)
        def _():
            acc_ref[...] = jnp.zeros_like(acc_ref)

        acc_ref[...] += _dot_tn(a_ref[...], b_ref[...])

        @pl.when(k == nk - 1)
        def _():
            o_ref[...] = acc_ref[...].astype(BF16)

    return pl.pallas_call(
        body, name=name, grid=(FF_CHUNKS, nk),
        in_specs=[
            pl.BlockSpec((None, tk, FF_SHARD), lambda j, k: (j, k, 0)),
            pl.BlockSpec((tk, D_MODEL), lambda j, k: (k, 0)),
        ] + [pl.BlockSpec(memory_space=pl.ANY)] * len(deps),
        out_specs=pl.BlockSpec((None, FF_SHARD, D_MODEL), lambda j, k: (j, 0, 0)),
        out_shape=jax.ShapeDtypeStruct((FF_CHUNKS, FF_SHARD, D_MODEL), BF16),
        scratch_shapes=[pltpu.VMEM((FF_SHARD, D_MODEL), F32)],
        compiler_params=_cp("parallel", "arbitrary"),
    )(act, dyb, *deps)


def _mix_in_fwd(x, gain, w, tm, name):
    n = x.shape[0]

    def body(x_ref, g_ref, w_ref, z_ref):
        y, _, _ = _rms_fwd(x_ref[...], g_ref[...])
        z_ref[...] = _dot(y.astype(BF16), w_ref[...])

    return pl.pallas_call(
        body, name=name, grid=(n // tm,),
        in_specs=[
            pl.BlockSpec((tm, D_MODEL), lambda i: (i, 0)),
            pl.BlockSpec((1, D_MODEL), lambda i: (0, 0)),
            pl.BlockSpec((D_MODEL, IN_COLS), lambda i: (0, 0)),
        ],
        out_specs=pl.BlockSpec((tm, IN_COLS), lambda i: (i, 0)),
        out_shape=jax.ShapeDtypeStruct((n, IN_COLS), F32),
        compiler_params=_cp("parallel"),
    )(x, gain, w)


def _mix_in_bwd(x, gain, du_ssm, du_gm, dv_gm, d_res, w, tm, name):
    n = x.shape[0]

    def body(x_ref, g_ref, d0_ref, d1_ref, d2_ref, dres_ref, w_ref, dx_ref, dw_ref, dgain_ref):
        i = pl.program_id(0)

        @pl.when(i == 0)
        def _():
            dw_ref[...] = jnp.zeros_like(dw_ref)
            dgain_ref[...] = jnp.zeros_like(dgain_ref)

        g = g_ref[...]
        y, xh, r = _rms_fwd(x_ref[...], g)
        xn = y.astype(BF16)
        dxn = jnp.zeros((tm, D_MODEL), F32)
        for k, d_ref in enumerate((d0_ref, d1_ref, d2_ref)):
            dz = d_ref[...].astype(BF16)
            cols = slice(k * SSM_WIDTH, (k + 1) * SSM_WIDTH)
            dxn += _dot_nt(dz, w_ref[:, cols])
            dw_ref[cols, :] += _dot_tn(dz, xn)
        dx, dgr = _rms_bwd(dxn, xh, r, g)
        dx_ref[...] = dres_ref[...] + dx
        dgain_ref[...] += _rows8(dgr)

    row = lambda i: (i, 0)
    fixed = lambda i: (0, 0)
    return pl.pallas_call(
        body, name=name, grid=(n // tm,),
        in_specs=[
            pl.BlockSpec((tm, D_MODEL), row),
            pl.BlockSpec((1, D_MODEL), fixed),
            pl.BlockSpec((tm, SSM_WIDTH), row),
            pl.BlockSpec((tm, GM_WIDTH), row),
            pl.BlockSpec((tm, GM_WIDTH), row),
            pl.BlockSpec((tm, D_MODEL), row),
            pl.BlockSpec((D_MODEL, IN_COLS), fixed),
        ],
        out_specs=[
            pl.BlockSpec((tm, D_MODEL), row),
            pl.BlockSpec((IN_COLS, D_MODEL), fixed),
            pl.BlockSpec((SUBLANES, D_MODEL), fixed),
        ],
        out_shape=[
            jax.ShapeDtypeStruct((n, D_MODEL), F32),
            jax.ShapeDtypeStruct((IN_COLS, D_MODEL), F32),
            jax.ShapeDtypeStruct((SUBLANES, D_MODEL), F32),
        ],
        compiler_params=_cp("arbitrary"),
    )(x, gain, du_ssm, du_gm, dv_gm, d_res, w)


def _mix_out_fwd(y_ssm, y_gm, g_ssm, g_gm, w, x, tm, name):
    n = x.shape[0]

    def body(ys_ref, yg_ref, gs_ref, gg_ref, w_ref, x_ref, o_ref):
        a, _, _ = _rms_fwd(ys_ref[...], gs_ref[...])
        b, _, _ = _rms_fwd(yg_ref[...], gg_ref[...])
        o_ref[...] = (x_ref[...] + _dot(a.astype(BF16), w_ref[0:SSM_WIDTH, :])
                      + _dot(b.astype(BF16), w_ref[SSM_WIDTH:D_MODEL, :]))

    row = lambda i: (i, 0)
    fixed = lambda i: (0, 0)
    return pl.pallas_call(
        body, name=name, grid=(n // tm,),
        in_specs=[
            pl.BlockSpec((tm, SSM_WIDTH), row), pl.BlockSpec((tm, GM_WIDTH), row),
            pl.BlockSpec((1, SSM_WIDTH), fixed), pl.BlockSpec((1, GM_WIDTH), fixed),
            pl.BlockSpec((D_MODEL, D_MODEL), fixed), pl.BlockSpec((tm, D_MODEL), row),
        ],
        out_specs=pl.BlockSpec((tm, D_MODEL), row),
        out_shape=jax.ShapeDtypeStruct((n, D_MODEL), F32),
        compiler_params=_cp("parallel"),
    )(y_ssm, y_gm, g_ssm, g_gm, w, x)


def _mix_out_bwd(y_ssm, y_gm, g_ssm, g_gm, w, dx, tm, name):
    n = dx.shape[0]

    def body(ys_ref, yg_ref, gs_ref, gg_ref, w_ref, dx_ref, dys_ref, dyg_ref, dw_ref, dgs_ref, dgg_ref):
        i = pl.program_id(0)

        @pl.when(i == 0)
        def _():
            dw_ref[...] = jnp.zeros_like(dw_ref)
            dgs_ref[...] = jnp.zeros_like(dgs_ref)
            dgg_ref[...] = jnp.zeros_like(dgg_ref)

        dxb = dx_ref[...].astype(BF16)
        parts = ((ys_ref, gs_ref, dys_ref, dgs_ref, 0), (yg_ref, gg_ref, dyg_ref, dgg_ref, SSM_WIDTH))
        for y_ref, g_ref, dy_ref, dg_ref, off in parts:
            g = g_ref[...]
            yn, xh, r = _rms_fwd(y_ref[...], g)
            rows = slice(off, off + SSM_WIDTH)
            dyn = _dot_nt(dxb, w_ref[rows, :])
            dw_ref[rows, :] += _dot_tn(yn.astype(BF16), dxb)
            dy, dgr = _rms_bwd(dyn, xh, r, g)
            dy_ref[...] = dy
            dg_ref[...] += _rows8(dgr)

    row = lambda i: (i, 0)
    fixed = lambda i: (0, 0)
    return pl.pallas_call(
        body, name=name, grid=(n // tm,),
        in_specs=[
            pl.BlockSpec((tm, SSM_WIDTH), row), pl.BlockSpec((tm, GM_WIDTH), row),
            pl.BlockSpec((1, SSM_WIDTH), fixed), pl.BlockSpec((1, GM_WIDTH), fixed),
            pl.BlockSpec((D_MODEL, D_MODEL), fixed), pl.BlockSpec((tm, D_MODEL), row),
        ],
        out_specs=[
            pl.BlockSpec((tm, SSM_WIDTH), row), pl.BlockSpec((tm, GM_WIDTH), row),
            pl.BlockSpec((D_MODEL, D_MODEL), fixed),
            pl.BlockSpec((SUBLANES, SSM_WIDTH), fixed), pl.BlockSpec((SUBLANES, GM_WIDTH), fixed),
        ],
        out_shape=[
            jax.ShapeDtypeStruct((n, SSM_WIDTH), F32), jax.ShapeDtypeStruct((n, GM_WIDTH), F32),
            jax.ShapeDtypeStruct((D_MODEL, D_MODEL), F32),
            jax.ShapeDtypeStruct((SUBLANES, SSM_WIDTH), F32), jax.ShapeDtypeStruct((SUBLANES, GM_WIDTH), F32),
        ],
        compiler_params=_cp("arbitrary"),
    )(y_ssm, y_gm, g_ssm, g_gm, w, dx)


SCAN_W = 512
SCAN_PIECES = HALF_ST // SCAN_W


def _scan_tiles(src_ref, dst_ref, dst_off, c_ref, half, carry_ref, n_tiles, reverse, extra=None):
    shifts = (1, 2, 4)
    carry_row = 0 if reverse else SUBLANES - 1

    def cols(piece, im):
        lo = im * HALF_ST + piece * SCAN_W
        return slice(lo, lo + SCAN_W)

    def step(t, state):
        carries, accs = state
        k = (n_tiles - 1 - t) if reverse else t
        rows = slice(k * SUBLANES, (k + 1) * SUBLANES)
        new_carries, new_accs = [], []
        for piece in range(SCAN_PIECES):
            cr, ci = carries[piece]
            xr0 = src_ref[rows, cols(piece, 0)]
            xi0 = src_ref[rows, cols(piece, 1)]
            xr, xi = xr0, xi0
            for si, s in enumerate(shifts):
                ar = c_ref[half, si, :, cols(piece, 0)]
                ai = c_ref[half, si, :, cols(piece, 1)]
                sh = (SUBLANES - s) if reverse else s
                sr = pltpu.roll(xr, sh, 0)
                sm = pltpu.roll(xi, sh, 0)
                xr, xi = xr + (ar * sr - ai * sm), xi + (ar * sm + ai * sr)
            pr = c_ref[half, 3, :, cols(piece, 0)]
            pi = c_ref[half, 3, :, cols(piece, 1)]
            hr = xr + (pr * cr - pi * ci)
            hi = xi + (pr * ci + pi * cr)
            dst_ref[rows, pl.ds(dst_off + piece * SCAN_W, SCAN_W)] = hr
            dst_ref[rows, pl.ds(dst_off + HALF_ST + piece * SCAN_W, SCAN_W)] = hi
            new_carries.append((jnp.broadcast_to(hr[carry_row:carry_row + 1, :], (SUBLANES, SCAN_W)),
                                jnp.broadcast_to(hi[carry_row:carry_row + 1, :], (SUBLANES, SCAN_W))))
            if extra is not None:
                new_accs.append(extra(rows, piece, (xr0, xi0), (hr, hi), accs[piece]))
        return tuple(new_carries), tuple(new_accs)

    base = half * 2 * HALF_ST
    carries0 = tuple((carry_ref[:, pl.ds(base + p * SCAN_W, SCAN_W)],
                      carry_ref[:, pl.ds(base + HALF_ST + p * SCAN_W, SCAN_W)]) for p in range(SCAN_PIECES))
    zero = jnp.zeros((SUBLANES, SCAN_W), F32)
    accs0 = tuple((zero, zero) for _ in range(SCAN_PIECES)) if extra is not None else ()
    state = (carries0, accs0)
    for t in range(n_tiles):
        state = step(t, state)
    carries, accs = state
    for p in range(SCAN_PIECES):
        carry_ref[:, pl.ds(base + p * SCAN_W, SCAN_W)] = carries[p][0]
        carry_ref[:, pl.ds(base + HALF_ST + p * SCAN_W, SCAN_W)] = carries[p][1]
    return accs


def _ssm_tail(hb, u, c_ref, glu_ref, glub_ref, dskip_ref):
    ypre = u * dskip_ref[...]
    parts = []
    for half in range(2):
        parts.append(_dot(hb[half], c_ref[half]))
    ypre = ypre + jnp.concatenate(parts, axis=1)
    yg, th = _gelu(ypre)
    zz = _dot(yg.astype(BF16), glu_ref[...]) + glub_ref[...]
    z1, z2 = zz[:, :SSM_WIDTH], zz[:, SSM_WIDTH:]
    sg = jax.nn.sigmoid(z2)
    return ypre, th, yg, z1, sg


def _ssm_fwd(z, bblk, cblk, glu, glub, dskip, fwdc, batch, t_chunk, name):
    n = z.shape[0]
    nk = n // batch // t_chunk
    n_tiles = t_chunk // SUBLANES

    def body(u_ref, b_ref, c_ref, glu_ref, glub_ref, dskip_ref, k_ref, y_ref, h_ref, bu_ref, carry_ref):
        @pl.when(pl.program_id(1) == 0)
        def _():
            carry_ref[...] = jnp.zeros_like(carry_ref)

        u = u_ref[...]
        ub = u.astype(BF16)
        for half in range(2):
            bu_ref[half] = _dot(ub[:, half * HALF_IN:(half + 1) * HALF_IN], b_ref[half])
            _scan_tiles(bu_ref.at[half], h_ref, half * 2 * HALF_ST, k_ref, half, carry_ref, n_tiles, False)
        hb = [h_ref[:, half * 2 * HALF_ST:(half + 1) * 2 * HALF_ST].astype(BF16) for half in range(2)]
        _, _, _, z1, sg = _ssm_tail(hb, u, c_ref, glu_ref, glub_ref, dskip_ref)
        y_ref[...] = z1 * sg

    fixed2 = lambda b, k: (0, 0)
    fixed3 = lambda b, k: (0, 0, 0)
    row = lambda b, k: (b * nk + k, 0)
    return pl.pallas_call(
        body, name=name, grid=(batch, nk),
        in_specs=[
            pl.BlockSpec((t_chunk, SSM_WIDTH), row),
            pl.BlockSpec((2, HALF_IN, 2 * HALF_ST), fixed3),
            pl.BlockSpec((2, 2 * HALF_ST, HALF_IN), fixed3),
            pl.BlockSpec((SSM_WIDTH, 2 * SSM_WIDTH), fixed2),
            pl.BlockSpec((1, 2 * SSM_WIDTH), fixed2),
            pl.BlockSpec((1, SSM_WIDTH), fixed2),
            pl.BlockSpec((2, 4, SUBLANES, 2 * HALF_ST), lambda b, k: (0, 0, 0, 0)),
        ],
        out_specs=[pl.BlockSpec((t_chunk, SSM_WIDTH), row), pl.BlockSpec((t_chunk, 4 * HALF_ST), row)],
        out_shape=[jax.ShapeDtypeStruct((n, SSM_WIDTH), F32), jax.ShapeDtypeStruct((n, 4 * HALF_ST), F32)],
        scratch_shapes=[pltpu.VMEM((2, t_chunk, 2 * HALF_ST), F32), pltpu.VMEM((SUBLANES, 4 * HALF_ST), F32)],
        compiler_params=_cp("parallel", "arbitrary"),
    )(z, bblk, cblk, glu, glub, dskip, fwdc)


def _ssm_bwd(z, h, dy, bblk, cblk, glu, glub, dskip, revc, batch, t_chunk, name):
    n = z.shape[0]
    nk = n // batch // t_chunk
    n_tiles = t_chunk // SUBLANES

    def body(u_ref, h_ref, dy_ref, b_ref, c_ref, glu_ref, glub_ref, dskip_ref, k_ref,
             du_ref, dglu_ref, dglub_ref, ddskip_ref, dct_ref, db_ref, q_ref, g_ref, carry_ref):
        first = jnp.logical_and(pl.program_id(0) == 0, pl.program_id(1) == 0)

        @pl.when(first)
        def _():
            for r in (dglu_ref, dglub_ref, ddskip_ref, dct_ref, db_ref, q_ref):
                r[...] = jnp.zeros_like(r)

        @pl.when(pl.program_id(1) == 0)
        def _():
            carry_ref[...] = jnp.zeros_like(carry_ref)

        u = u_ref[...]
        ub = u.astype(BF16)
        hb = [h_ref[:, half * 2 * HALF_ST:(half + 1) * 2 * HALF_ST].astype(BF16) for half in range(2)]
        ypre, th, yg, z1, sg = _ssm_tail(hb, u, c_ref, glu_ref, glub_ref, dskip_ref)
        dout = dy_ref[...]
        dz = jnp.concatenate([dout * sg, dout * z1 * sg * (1.0 - sg)], axis=1)
        dzb = dz.astype(BF16)
        dglu_ref[...] += _dot_tn(yg.astype(BF16), dzb)
        dglub_ref[...] += _rows8(dz)
        dypre = _dot_nt(dzb, glu_ref[...]) * _gelu_grad(ypre, th)
        ddskip_ref[...] += _rows8(dypre * u)
        dypb = dypre.astype(BF16)
        du_parts = []
        for half in range(2):
            dyp_h = dypb[:, half * HALF_IN:(half + 1) * HALF_IN]
            dct_ref[half] += _dot_tn(dyp_h, hb[half])
            g_ref[half] = _dot_nt(dyp_h, c_ref[half])

            def extra(rows, piece, x_in, g_out, acc, half=half):
                er, ei = g_out[0] - x_in[0], g_out[1] - x_in[1]
                base = half * 2 * HALF_ST + piece * SCAN_W
                hr = h_ref[rows, pl.ds(base, SCAN_W)]
                hi = h_ref[rows, pl.ds(base + HALF_ST, SCAN_W)]
                return acc[0] + (er * hr + ei * hi), acc[1] + (er * hi - ei * hr)

            accs = _scan_tiles(g_ref.at[half], g_ref.at[half], 0, k_ref, half, carry_ref, n_tiles, True, extra)
            for piece in range(SCAN_PIECES):
                base = half * 2 * HALF_ST + piece * SCAN_W
                q_ref[:, pl.ds(base, SCAN_W)] += accs[piece][0]
                q_ref[:, pl.ds(base + HALF_ST, SCAN_W)] += accs[piece][1]
            gb = g_ref[half].astype(BF16)
            db_ref[half] += _dot_tn(ub[:, half * HALF_IN:(half + 1) * HALF_IN], gb)
            du_parts.append(_dot_nt(gb, b_ref[half]))
        du_ref[...] = dypre * dskip_ref[...] + jnp.concatenate(du_parts, axis=1)

    fixed2 = lambda b, k: (0, 0)
    fixed3 = lambda b, k: (0, 0, 0)
    row = lambda b, k: (b * nk + (nk - 1 - k), 0)
    return pl.pallas_call(
        body, name=name, grid=(batch, nk),
        in_specs=[
            pl.BlockSpec((t_chunk, SSM_WIDTH), row),
            pl.BlockSpec((t_chunk, 4 * HALF_ST), row),
            pl.BlockSpec((t_chunk, SSM_WIDTH), row),
            pl.BlockSpec((2, HALF_IN, 2 * HALF_ST), fixed3),
            pl.BlockSpec((2, 2 * HALF_ST, HALF_IN), fixed3),
            pl.BlockSpec((SSM_WIDTH, 2 * SSM_WIDTH), fixed2),
            pl.BlockSpec((1, 2 * SSM_WIDTH), fixed2),
            pl.BlockSpec((1, SSM_WIDTH), fixed2),
            pl.BlockSpec((2, 4, SUBLANES, 2 * HALF_ST), lambda b, k: (0, 0, 0, 0)),
        ],
        out_specs=[
            pl.BlockSpec((t_chunk, SSM_WIDTH), row),
            pl.BlockSpec((SSM_WIDTH, 2 * SSM_WIDTH), fixed2),
            pl.BlockSpec((SUBLANES, 2 * SSM_WIDTH), fixed2),
            pl.BlockSpec((SUBLANES, SSM_WIDTH), fixed2),
            pl.BlockSpec((2, HALF_IN, 2 * HALF_ST), fixed3),
            pl.BlockSpec((2, HALF_IN, 2 * HALF_ST), fixed3),
            pl.BlockSpec((SUBLANES, 4 * HALF_ST), fixed2),
        ],
        out_shape=[
            jax.ShapeDtypeStruct((n, SSM_WIDTH), F32),
            jax.ShapeDtypeStruct((SSM_WIDTH, 2 * SSM_WIDTH), F32),
            jax.ShapeDtypeStruct((SUBLANES, 2 * SSM_WIDTH), F32),
            jax.ShapeDtypeStruct((SUBLANES, SSM_WIDTH), F32),
            jax.ShapeDtypeStruct((2, HALF_IN, 2 * HALF_ST), F32),
            jax.ShapeDtypeStruct((2, HALF_IN, 2 * HALF_ST), F32),
            jax.ShapeDtypeStruct((SUBLANES, 4 * HALF_ST), F32),
        ],
        scratch_shapes=[pltpu.VMEM((2, t_chunk, 2 * HALF_ST), F32), pltpu.VMEM((SUBLANES, 4 * HALF_ST), F32)],
        compiler_params=_cp("arbitrary", "arbitrary"),
    )(z, h, dy, bblk, cblk, glu, glub, dskip, revc)


def _gm_chunk_fwd(u, v, gain_ref, w_ref, bias_ref):
    ug, thu = _gelu(u)
    vg, thv = _gelu(v)
    rs, vns, ss = [], [], []
    for hh in range(GM_HEADS):
        cs = slice(hh * GM_HEAD_DIM, (hh + 1) * GM_HEAD_DIM)
        vn, _, r = _rms_fwd(vg[:, cs], gain_ref[:, cs])
        s = _dot(w_ref[hh], vn.astype(BF16)) + bias_ref[:, cs]
        rs.append(r)
        vns.append(vn)
        ss.append(s)
    return ug, thu, thv, vg, rs, vns, ss


def _gm_fwd(z, gain, w_tril, bias, rows, name):
    n = z.shape[0]
    chunks = rows // GM_CHUNK

    def body(u_ref, v_ref, gain_ref, w_ref, bias_ref, y_ref):
        for c in range(chunks):
            rs_ = slice(c * GM_CHUNK, (c + 1) * GM_CHUNK)
            ug, _, _, _, _, _, ss = _gm_chunk_fwd(u_ref[rs_, :], v_ref[rs_, :], gain_ref, w_ref, bias_ref)
            y_ref[rs_, :] = ug * jnp.concatenate(ss, axis=1)

    return pl.pallas_call(
        body, name=name, grid=(n // rows,),
        in_specs=[
            pl.BlockSpec((rows, GM_WIDTH), lambda i: (i, 1)),
            pl.BlockSpec((rows, GM_WIDTH), lambda i: (i, 2)),
            pl.BlockSpec((1, GM_WIDTH), lambda i: (0, 0)),
            pl.BlockSpec((GM_HEADS, GM_CHUNK, GM_CHUNK), lambda i: (0, 0, 0)),
            pl.BlockSpec((GM_CHUNK, GM_WIDTH), lambda i: (0, 0)),
        ],
        out_specs=pl.BlockSpec((rows, GM_WIDTH), lambda i: (i, 0)),
        out_shape=jax.ShapeDtypeStruct((n, GM_WIDTH), F32),
        compiler_params=_cp("parallel"),
    )(z, z, gain, w_tril, bias)


def _gm_bwd(z, dy, gain, w_tril, bias, rows, name):
    n = z.shape[0]
    chunks = rows // GM_CHUNK

    def body(u_ref, v_ref, dy_ref, gain_ref, w_ref, bias_ref, du_ref, dv_ref, dw_ref, dbias_ref, dgain_ref):
        @pl.when(pl.program_id(0) == 0)
        def _():
            dw_ref[...] = jnp.zeros_like(dw_ref)
            dbias_ref[...] = jnp.zeros_like(dbias_ref)
            dgain_ref[...] = jnp.zeros_like(dgain_ref)

        for c in range(chunks):
            rs_ = slice(c * GM_CHUNK, (c + 1) * GM_CHUNK)
            u, v = u_ref[rs_, :], v_ref[rs_, :]
            ug, thu, thv, vg, rs, vns, ss = _gm_chunk_fwd(u, v, gain_ref, w_ref, bias_ref)
            dout = dy_ref[rs_, :]
            ds = dout * ug
            du_ref[rs_, :] = dout * jnp.concatenate(ss, axis=1) * _gelu_grad(u, thu)
            dbias_ref[...] += ds
            dvg_parts, dgain_parts = [], []
            for hh in range(GM_HEADS):
                cs = slice(hh * GM_HEAD_DIM, (hh + 1) * GM_HEAD_DIM)
                dsb = ds[:, cs].astype(BF16)
                dvn = _dot_tn(w_ref[hh], dsb)
                dw_ref[hh] += _dot_nt(dsb, vns[hh].astype(BF16))
                g = gain_ref[:, cs]
                xh = vg[:, cs] * rs[hh]
                dvg, dgr = _rms_bwd(dvn, xh, rs[hh], g)
                dvg_parts.append(dvg)
                dgain_parts.append(dgr)
            dv_ref[rs_, :] = jnp.concatenate(dvg_parts, axis=1) * _gelu_grad(v, thv)
            dgain_ref[...] += _rows8(jnp.concatenate(dgain_parts, axis=1))

    row = lambda i: (i, 0)
    return pl.pallas_call(
        body, name=name, grid=(n // rows,),
        in_specs=[
            pl.BlockSpec((rows, GM_WIDTH), lambda i: (i, 1)),
            pl.BlockSpec((rows, GM_WIDTH), lambda i: (i, 2)),
            pl.BlockSpec((rows, GM_WIDTH), row),
            pl.BlockSpec((1, GM_WIDTH), lambda i: (0, 0)),
            pl.BlockSpec((GM_HEADS, GM_CHUNK, GM_CHUNK), lambda i: (0, 0, 0)),
            pl.BlockSpec((GM_CHUNK, GM_WIDTH), lambda i: (0, 0)),
        ],
        out_specs=[
            pl.BlockSpec((rows, GM_WIDTH), row), pl.BlockSpec((rows, GM_WIDTH), row),
            pl.BlockSpec((GM_HEADS, GM_CHUNK, GM_CHUNK), lambda i: (0, 0, 0)),
            pl.BlockSpec((GM_CHUNK, GM_WIDTH), lambda i: (0, 0)),
            pl.BlockSpec((SUBLANES, GM_WIDTH), lambda i: (0, 0)),
        ],
        out_shape=[
            jax.ShapeDtypeStruct((n, GM_WIDTH), F32), jax.ShapeDtypeStruct((n, GM_WIDTH), F32),
            jax.ShapeDtypeStruct((GM_HEADS, GM_CHUNK, GM_CHUNK), F32),
            jax.ShapeDtypeStruct((GM_CHUNK, GM_WIDTH), F32),
            jax.ShapeDtypeStruct((SUBLANES, GM_WIDTH), F32),
        ],
        compiler_params=_cp("arbitrary"),
    )(z, z, dy, gain, w_tril, bias)


def _loss_head(x, gain, target, tm, name):
    n = x.shape[0]

    def body(x_ref, g_ref, t_ref, dx_ref, sq_ref, dgain_ref):
        @pl.when(pl.program_id(0) == 0)
        def _():
            sq_ref[...] = jnp.zeros_like(sq_ref)
            dgain_ref[...] = jnp.zeros_like(dgain_ref)

        g = g_ref[...]
        y, xh, r = _rms_fwd(x_ref[...], g)
        err = y - t_ref[...]
        sq_ref[...] += _rows8(err * err)
        dx, dgr = _rms_bwd(err * (1.0 / D_MODEL), xh, r, g)
        dx_ref[...] = dx
        dgain_ref[...] += _rows8(dgr)

    row = lambda i: (i, 0)
    fixed = lambda i: (0, 0)
    return pl.pallas_call(
        body, name=name, grid=(n // tm,),
        in_specs=[pl.BlockSpec((tm, D_MODEL), row), pl.BlockSpec((1, D_MODEL), fixed), pl.BlockSpec((tm, D_MODEL), row)],
        out_specs=[pl.BlockSpec((tm, D_MODEL), row), pl.BlockSpec((SUBLANES, D_MODEL), fixed),
                   pl.BlockSpec((SUBLANES, D_MODEL), fixed)],
        out_shape=[jax.ShapeDtypeStruct((n, D_MODEL), F32), jax.ShapeDtypeStruct((SUBLANES, D_MODEL), F32),
                   jax.ShapeDtypeStruct((SUBLANES, D_MODEL), F32)],
        compiler_params=_cp("arbitrary"),
    )(x, gain, target)


def _adam_math(w, g, m, v):
    m2 = ADAM_B1 * m + (1.0 - ADAM_B1) * g
    v2 = ADAM_B2 * v + (1.0 - ADAM_B2) * (g * g)
    m_hat = m2 / (1.0 - ADAM_B1 ** ADAM_STEP)
    v_hat = v2 / (1.0 - ADAM_B2 ** ADAM_STEP)
    delta = -ADAM_LR * (m_hat / (jnp.sqrt(v_hat) + ADAM_EPS) + ADAM_WD * w)
    return delta, m2, v2


def _adam_sharded(parts, w, m, v, layer, earlier, name):
    depth, r, c = w.shape
    tr = max(t for t in range(16, 129, 16) if r % t == 0)

    def body(p_ref, w_ref, m_ref, v_ref, *rest):
        g_ref, d_ref, m2_ref, v2_ref = rest[-4:]
        g = p_ref[0].astype(F32)
        for s in range(1, N_DEV):
            g = g + p_ref[s].astype(F32)
        delta, m2, v2 = _adam_math(w_ref[...], g, m_ref[...], v_ref[...])
        g_ref[...] = g
        d_ref[...] = delta
        m2_ref[...] = m2
        v2_ref[...] = v2

    blk = pl.BlockSpec((None, tr, c), lambda i: (layer, i, 0))
    extra = [] if earlier is None else list(earlier)
    return pl.pallas_call(
        body, name=name, grid=(r // tr,),
        in_specs=[pl.BlockSpec((N_DEV, tr, c), lambda i: (0, i, 0)), blk, blk, blk]
        + [pl.BlockSpec(memory_space=pl.ANY)] * len(extra),
        out_specs=[blk, blk, blk, blk],
        out_shape=[jax.ShapeDtypeStruct((depth, r, c), F32)] * 4,
        input_output_aliases={4 + i: i for i in range(len(extra))},
        compiler_params=_cp("parallel"),
    )(parts, w, m, v, *extra)


def _adam_packed(g, w, m, v, name):
    r, c = g.shape

    def body(g_ref, w_ref, m_ref, v_ref, d_ref, m2_ref, v2_ref):
        delta, m2, v2 = _adam_math(w_ref[...], g_ref[...], m_ref[...], v_ref[...])
        d_ref[...] = delta
        m2_ref[...] = m2
        v2_ref[...] = v2

    blk = pl.BlockSpec((r, c), lambda i: (0, 0))
    return pl.pallas_call(
        body, name=name, grid=(1,),
        in_specs=[blk, blk, blk, blk], out_specs=[blk, blk, blk],
        out_shape=[jax.ShapeDtypeStruct((r, c), F32)] * 3,
        compiler_params=_cp("arbitrary"),
    )(g, w, m, v)


def _my_place():
    return lax.axis_index("x"), lax.axis_index("y"), lax.axis_index("c")


def _flip(place, rel):
    x, y, c = place
    return (1 - x if rel & 4 else x, 1 - y if rel & 2 else y, 1 - c if rel & 1 else c)


def _index(place):
    return 4 * place[0] + 2 * place[1] + place[2]


def _all_gather(shards, name):
    na = len(shards)

    def body(*refs):
        xs, outs = refs[:na], refs[na:2 * na]
        send_sems, recv_sems, local_sems = refs[2 * na:]
        me = _my_place()
        sibling = _flip(me, 1)
        chips = [_flip(me, 4), _flip(me, 2), _flip(me, 6)]

        def copy(a, k, block, to, src=None):
            slot = outs[a].at[_index(block)]
            return pltpu.make_async_remote_copy(
                src_ref=slot if src is None else src, dst_ref=slot,
                send_sem=send_sems.at[a, k], recv_sem=recv_sems.at[a, k],
                device_id=to, device_id_type=MESH)

        mine = [pltpu.make_async_copy(xs[a], outs[a].at[_index(me)], local_sems.at[a]) for a in range(na)]
        for cp in mine:
            cp.start()
        first = []
        for a in range(na):
            first.append(copy(a, 0, me, sibling, src=xs[a]))
            first += [copy(a, 1 + j, me, chip, src=xs[a]) for j, chip in enumerate(chips)]
        for cp in first:
            cp.start()
        passed = []
        for a in range(na):
            for j, chip in enumerate(chips):
                copy(a, 1 + j, chip, me).wait_recv()
                fwd = copy(a, 4 + j, chip, sibling)
                fwd.start()
                passed.append(fwd)
        for a in range(na):
            copy(a, 0, sibling, me).wait_recv()
            for j, chip in enumerate(chips):
                copy(a, 4 + j, _flip(chip, 1), me).wait_recv()
        for cp in first + passed:
            cp.wait_send()
        for cp in mine:
            cp.wait()

    hbm = pl.BlockSpec(memory_space=pl.ANY)
    return pl.pallas_call(
        body, name=name,
        in_specs=[hbm] * na, out_specs=[hbm] * na,
        out_shape=[jax.ShapeDtypeStruct((N_DEV,) + s.shape, s.dtype) for s in shards],
        scratch_shapes=[pltpu.SemaphoreType.DMA((na, 7)), pltpu.SemaphoreType.DMA((na, 7)),
                        pltpu.SemaphoreType.DMA((na,))],
    )(*shards)


_HBM = pl.BlockSpec(memory_space=pltpu.HBM)
_SEM = pl.BlockSpec(memory_space=pltpu.SEMAPHORE)
_EFFECT = pltpu.SideEffectType.DATAFLOW_SIDE_EFFECTING


def _exchange_copy(src_ref, land_ref, send_sems, recv_sems, a, rel, me, scatter, landed):
    peer = _flip(me, rel)
    src = src_ref.at[_index(peer)] if scatter else src_ref
    return pltpu.make_async_remote_copy(
        src_ref=src, dst_ref=land_ref.at[_index(peer if landed else me)],
        send_sem=send_sems.at[a * (N_DEV - 1) + rel - 1], recv_sem=recv_sems.at[a * (N_DEV - 1) + rel - 1],
        device_id=peer, device_id_type=MESH)


def _own_slot(data, me, scatter):
    if scatter:
        own = lax.dynamic_slice_in_dim(data, me, 1, axis=0)
        shape = data.shape
    else:
        own = data[None]
        shape = (N_DEV,) + data.shape
    start = (me,) + (0,) * (len(shape) - 1)
    return lax.dynamic_update_slice(lax.empty(shape, data.dtype), own, start)


def _exchange_start(groups, me, scatter, name, after=None):
    sizes = [len(g) for g in groups]
    srcs = [a for g in groups for a in g]
    lands = [_own_slot(a, me, scatter) for a in srcs]
    na, ng = len(srcs), len(groups)
    deps = [] if after is None else [after]

    def body(*refs):
        src_refs, land_refs = refs[:na], refs[na:2 * na]
        sems = refs[2 * na + len(deps):2 * na + len(deps) + 2 * ng]
        token = refs[-1]
        place = _my_place()
        a = 0
        for g, size in enumerate(sizes):
            for k in range(size):
                for rel in range(1, N_DEV):
                    _exchange_copy(src_refs[a], land_refs[a], sems[2 * g], sems[2 * g + 1], k, rel, place, scatter,
                                   False).start()
                a += 1
        token[...] = jnp.zeros_like(token)

    sem_shapes = [pltpu.SemaphoreType.DMA((size * (N_DEV - 1),)) for size in sizes for _ in range(2)]
    outs = pl.pallas_call(
        body, name=name,
        in_specs=[_HBM] * (2 * na) + [pl.BlockSpec(memory_space=pl.ANY)] * len(deps),
        out_specs=[_SEM] * (2 * ng) + [_HBM] * (2 * na) + [pl.BlockSpec(memory_space=pltpu.VMEM)],
        out_shape=sem_shapes + [pltpu.HBM(a.shape, a.dtype) for a in srcs + lands]
        + [jax.ShapeDtypeStruct((SUBLANES, LANES), F32)],
        input_output_aliases={i: 2 * ng + i for i in range(2 * na)},
        compiler_params=pltpu.CompilerParams(has_side_effects=_EFFECT),
    )(*[pltpu.with_memory_space_constraint(a, pltpu.HBM) for a in srcs + lands], *deps)
    sems, thru, token = outs[:2 * ng], outs[2 * ng:2 * ng + 2 * na], outs[-1]
    handles, a = [], 0
    for g, size in enumerate(sizes):
        handles.append((sems[2 * g], sems[2 * g + 1], thru[a:a + size], thru[na + a:na + a + size]))
        a += size
    return handles, token


def _exchange_wait(handle, after, scatter, name):
    send_sems, recv_sems, srcs, lands = handle
    na = len(srcs)

    def body(*refs):
        src_refs, land_refs = refs[:na], refs[na:2 * na]
        send_ref, recv_ref = refs[2 * na], refs[2 * na + 1]
        place = _my_place()
        for a in range(na):
            for rel in range(1, N_DEV):
                cp = _exchange_copy(src_refs[a], land_refs[a], send_ref, recv_ref, a, rel, place, scatter, True)
                cp.wait_send()
                cp.wait_recv()

    outs = pl.pallas_call(
        body, name=name,
        in_specs=[_HBM] * (2 * na) + [_SEM, _SEM, pl.BlockSpec(memory_space=pl.ANY)],
        out_specs=[_HBM] * (2 * na),
        out_shape=[pltpu.HBM(a.shape, a.dtype) for a in list(srcs) + list(lands)],
        input_output_aliases={i: i for i in range(2 * na)},
        compiler_params=pltpu.CompilerParams(has_side_effects=_EFFECT),
    )(*srcs, *lands, send_sems, recv_sems, after)
    return outs[na:]


def _behind(arr, token):
    return arr + token[0:1, 0:1]


def _all_reduce_small(g, name):
    _, r, c = g.shape

    def body(g_ref, o_ref, land_ref, red_ref, send1, recv1, send2, recv2):
        me = _my_place()
        idx = _index(me)

        def scatter(rel):
            peer = _flip(me, rel)
            return pltpu.make_async_remote_copy(
                src_ref=g_ref.at[_index(peer)], dst_ref=land_ref.at[idx],
                send_sem=send1.at[rel - 1], recv_sem=recv1.at[rel - 1], device_id=peer, device_id_type=MESH)

        def gather(rel):
            peer = _flip(me, rel)
            return pltpu.make_async_remote_copy(
                src_ref=red_ref, dst_ref=o_ref.at[idx],
                send_sem=send2.at[rel - 1], recv_sem=recv2.at[rel - 1], device_id=peer, device_id_type=MESH)

        for rel in range(1, N_DEV):
            scatter(rel).start()
        land_ref[idx] = g_ref[idx]
        for rel in range(1, N_DEV):
            scatter(rel).wait()
        acc = land_ref[0]
        for s in range(1, N_DEV):
            acc = acc + land_ref[s]
        red_ref[...] = acc
        for rel in range(1, N_DEV):
            gather(rel).start()
        o_ref[idx] = acc
        for rel in range(1, N_DEV):
            gather(rel).wait()

    vmem = pl.BlockSpec(memory_space=pltpu.VMEM)
    return pl.pallas_call(
        body, name=name,
        in_specs=[vmem], out_specs=vmem,
        out_shape=jax.ShapeDtypeStruct(g.shape, F32),
        scratch_shapes=[pltpu.VMEM(g.shape, F32), pltpu.VMEM((r, c), F32)]
        + [pltpu.SemaphoreType.DMA((N_DEV - 1,))] * 4,
        compiler_params=pltpu.CompilerParams(vmem_limit_bytes=VMEM_LIMIT),
    )(g)


def _ssm_discretize(a_re, a_im, log_dt, b_re, b_im):
    dt = jnp.exp(log_dt)[:, None]
    mag = jnp.exp(a_re * dt)
    lr, li = mag * jnp.cos(a_im * dt), mag * jnp.sin(a_im * dt)
    den = a_re * a_re + a_im * a_im
    qr = ((lr - 1.0) * a_re + li * a_im) / den
    qi = (li * a_re - (lr - 1.0) * a_im) / den
    bbr = qr[..., None] * b_re - qi[..., None] * b_im
    bbi = qr[..., None] * b_im + qi[..., None] * b_re
    return lr, li, bbr, bbi


def _halves(a):
    return a.reshape((2, HALF_GROUPS) + a.shape[1:])


def _block_diag_mask(g, r, c):
    rows = lax.broadcasted_iota(jnp.int32, (g * r, g * c), 0) // r
    cols = lax.broadcasted_iota(jnp.int32, (g * r, g * c), 1) // c
    return rows == cols


def _block_diag(blocks):
    g, r, c = blocks.shape
    spread = jnp.tile(jnp.eye(c, dtype=blocks.dtype), (1, g))
    full = jnp.dot(blocks.reshape(g * r, c), spread, precision=lax.Precision.HIGHEST)
    return jnp.where(_block_diag_mask(g, r, c), full, 0.0)


def _block_diag_take(dense, g, r, c):
    gather = jnp.tile(jnp.eye(c, dtype=dense.dtype), (g, 1))
    kept = jnp.where(_block_diag_mask(g, r, c), dense, 0.0)
    return jnp.dot(kept, gather, precision=lax.Precision.HIGHEST).reshape(g, r, c)


def _ssm_matrices(bbr, bbi, c_re, c_im, glu_w, glu_b, d_skip):
    bre, bim = _halves(jnp.swapaxes(bbr, 1, 2)), _halves(jnp.swapaxes(bbi, 1, 2))
    bblk = jnp.stack([jnp.concatenate([_block_diag(bre[h]), _block_diag(bim[h])], axis=1) for h in range(2)])
    cre, cim = _halves(jnp.swapaxes(c_re, 1, 2)), _halves(jnp.swapaxes(c_im, 1, 2))
    cblk = jnp.stack([jnp.concatenate([_block_diag(cre[h]), -_block_diag(cim[h])], axis=0) for h in range(2)])
    glu = jnp.concatenate([_block_diag(glu_w[:, :, :SSM_CH]), _block_diag(glu_w[:, :, SSM_CH:])], axis=1)
    glub = jnp.concatenate([glu_b[:, :SSM_CH].reshape(1, -1), glu_b[:, SSM_CH:].reshape(1, -1)], axis=1)
    return bblk.astype(BF16), cblk.astype(BF16), glu.astype(BF16), glub, d_skip.reshape(1, -1)


def _scan_constants(lr, li, reverse):
    if reverse:
        li = -li
    pows = [(lr, li)]
    for _ in range(SUBLANES - 1):
        pr, pi = pows[-1]
        pows.append((pr * lr - pi * li, pr * li + pi * lr))
    row = jnp.arange(SUBLANES)[:, None]

    def flat(a):
        return a.reshape(2, 1, HALF_ST)

    mats = []
    for s in (1, 2, 4):
        keep = (row + s <= SUBLANES - 1) if reverse else (row >= s)
        mats.append(tuple(jnp.where(keep[None], flat(p), 0.0) for p in pows[s - 1]))
    order = [SUBLANES - 1 - j for j in range(SUBLANES)] if reverse else list(range(SUBLANES))
    mats.append(tuple(jnp.concatenate([flat(pows[j][k]) for j in order], axis=1) for k in range(2)))
    return jnp.stack([jnp.concatenate([m[0], m[1]], axis=2) for m in mats], axis=1)


def _pack(arrs, rows):
    flat = jnp.concatenate([a.reshape(-1) for a in arrs])
    return jnp.pad(flat, (0, rows * LANES - flat.shape[0])).reshape(rows, LANES)


def _unpack(buf, like):
    flat = buf.reshape(-1)
    out, off = [], 0
    for a in like:
        out.append(flat[off:off + a.size].reshape(a.shape))
        off += a.size
    return out


SMALL = ("norm_ffn1", "norm_mix", "ssm_a_re", "ssm_a_im", "ssm_log_dt", "ssm_b_re", "ssm_b_im", "ssm_c_re",
         "ssm_c_im", "ssm_d", "ssm_glu_w", "ssm_glu_b", "gm_v_gain", "gm_w_s", "gm_b_s", "gain_ssm_out",
         "gain_gm_out", "norm_ffn2", "norm_final")
BIG = ("ffn1_w_in", "ffn1_w_out", "mix_w_in", "mix_w_out", "ffn2_w_in", "ffn2_w_out")
TRANSPOSED = ("ffn1_w_in", "mix_w_in", "ffn2_w_in")
WEIGHTS = ("norm_ffn1", "ffn1_w_in", "ffn1_w_out", "norm_mix", "mix_w_in", "ssm_a_re", "ssm_a_im", "ssm_log_dt",
           "ssm_b_re", "ssm_b_im", "ssm_c_re", "ssm_c_im", "ssm_d", "ssm_glu_w", "ssm_glu_b", "gm_v_gain", "gm_w_s",
           "gm_b_s", "gain_ssm_out", "gain_gm_out", "mix_w_out", "norm_ffn2", "ffn2_w_in", "ffn2_w_out", "norm_final")


def _step(x, target, w, m, v):
    batch, seq, _ = x.shape
    n = batch * seq
    depth = w["norm_ffn1"].shape[0]
    tm = min(512, n)
    tm_ffn = min(1024, n)
    tk = min(4096, n)
    t_chunk = min(256, seq)
    gm_rows = min(512, seq)
    x = x.reshape(n, D_MODEL)
    target = target.reshape(n, D_MODEL)

    assert depth == 2
    me = _index(_my_place())
    groups = [(g, l) for l in range(depth) for g in ("ffn1", "mix", "ffn2")]
    shards = {(g, l): [w[f"{g}_w_in"][l].astype(BF16), w[f"{g}_w_out"][l].astype(BF16)] for g, l in groups}
    batches = ([("ffn1", 0)], [("mix", 0), ("ffn2", 0)], [("ffn1", 1), ("mix", 1)], [("ffn2", 1)])
    gathered, pending = {}, {}

    def gather_start(i, after):
        handles, tok = _exchange_start([shards[k] for k in batches[i]], me, False, f"all_gather_start_{i}", after)
        pending.update(zip(batches[i], handles))
        return tok

    gather_start(0, None)

    def weights(group, l, after=None):
        if (group, l) not in gathered:
            w_in, w_out = _exchange_wait(pending[(group, l)], after, False, f"all_gather_wait_{group}_{l}")
            if group == "mix":
                w_in = jnp.transpose(w_in, (1, 0, 2)).reshape(D_MODEL, IN_COLS)
                w_out = w_out.reshape(D_MODEL, D_MODEL)
            gathered[(group, l)] = (w_in, w_out)
        return gathered[(group, l)]

    tril = jnp.tril(jnp.ones((GM_CHUNK, GM_CHUNK), bool))
    layers = []
    for l in range(depth):
        disc, disc_vjp = jax.vjp(_ssm_discretize, w["ssm_a_re"][l], w["ssm_a_im"][l], w["ssm_log_dt"][l],
                                 w["ssm_b_re"][l], w["ssm_b_im"][l])
        lr, li, bbr, bbi = disc
        bblk, cblk, glu, glub, dskip = _ssm_matrices(bbr, bbi, w["ssm_c_re"][l], w["ssm_c_im"][l],
                                                     w["ssm_glu_w"][l], w["ssm_glu_b"][l], w["ssm_d"][l])
        layers.append(dict(
            disc_vjp=disc_vjp, lr=lr, li=li, bblk=bblk, cblk=cblk, glu=glu, glub=glub, dskip=dskip,
            fwdc=_scan_constants(lr, li, False), revc=_scan_constants(lr, li, True),
            w_tril=jnp.where(tril[None], w["gm_w_s"][l], 0.0).astype(BF16),
            gm_bias=jnp.repeat(w["gm_b_s"][l].T, GM_HEAD_DIM, axis=1),
            g1=w["norm_ffn1"][l][None], gmix=w["norm_mix"][l][None], g2=w["norm_ffn2"][l][None],
            gv=w["gm_v_gain"][l][None], gs=w["gain_ssm_out"][l][None], gg=w["gain_gm_out"][l][None],
        ))

    probe = lambda a: a[(0,) * a.ndim].astype(F32)
    early = [a for k in groups[1:] for a in shards[k]]
    early += [layers[l][k] for l in range(depth) for k in ("bblk", "cblk", "glu", "fwdc", "revc", "w_tril", "gm_bias")]
    ready = jnp.stack([probe(a) for a in early]).sum().reshape(1)

    saved = []
    for l in range(depth):
        p = layers[l]
        x0 = x
        g1, gmix, g2 = p["g1"], p["gmix"], p["g2"]
        w_in, w_out = weights("ffn1", l, ready if l == 0 else x0)
        if l == 0:
            g1 = _behind(g1, gather_start(1, w_in))
        x1, xn1, gu1 = _ffn_fwd(x0, g1, w_in, w_out, tm_ffn, f"ffn1_fwd_{l}")
        if l == 0:
            gmix = _behind(gmix, gather_start(2, x1))
        mwi, mwo = weights("mix", l, x1)
        z = _mix_in_fwd(x1, gmix, mwi, tm, f"mix_in_fwd_{l}")
        y_ssm, h = _ssm_fwd(z, p["bblk"], p["cblk"], p["glu"], p["glub"], p["dskip"], p["fwdc"], batch, t_chunk,
                            f"ssm_fwd_{l}")
        y_gm = _gm_fwd(z, p["gv"], p["w_tril"], p["gm_bias"], gm_rows, f"gm_fwd_{l}")
        x2 = _mix_out_fwd(y_ssm, y_gm, p["gs"], p["gg"], mwo, x1, tm, f"mix_out_fwd_{l}")
        if l == 0:
            g2 = _behind(g2, gather_start(3, x2))
        x, xn2, gu2 = _ffn_fwd(x2, g2, *weights("ffn2", l, x2), tm_ffn, f"ffn2_fwd_{l}")
        saved.append((x0, x1, x2, z, h, y_ssm, y_gm, xn1, gu1, xn2, gu2))

    dx, sq, dnf = _loss_head(x, w["norm_final"][None], target, tm, "loss_head")
    loss = lax.psum((0.5 / D_MODEL) * jnp.sum(sq), AXES)

    small = {k: [None] * depth for k in SMALL if k != "norm_final"}
    sent = []

    def send(group, l, keys, parts):
        (handle,), tok = _exchange_start([parts], me, True, f"reduce_scatter_start_{group}_{l}")
        sent.append((group, l, keys, handle))
        return tok

    token = None
    for l in reversed(range(depth)):
        p = layers[l]
        x0, x1, x2, z, h, y_ssm, y_gm, xn1, gu1, xn2, gu2 = saved[l]
        mwi, mwo = weights("mix", l)
        dx_out = dx
        g2 = p["g2"] if token is None else _behind(p["g2"], token)
        dx, dgu, act, dgain, dyb = _ffn_bwd(x2, g2, dx_out, gu2, *weights("ffn2", l), tm, f"ffn2_bwd_{l}")
        dw_in = _ffn_dw_in(xn2, dgu, tk, f"ffn2_dw_in_{l}")
        dw_out = _ffn_dw_out(act, dyb, tk, f"ffn2_dw_out_{l}").reshape(N_DEV, FF_SHARD // 2, D_MODEL)
        token = send("ffn2", l, ("ffn2_w_in", "ffn2_w_out"), [dw_in, dw_out])
        small["norm_ffn2"][l] = dgain.sum(0)

        dy_ssm, dy_gm, dwo, dgs, dgg = _mix_out_bwd(y_ssm, y_gm, _behind(p["gs"], token), p["gg"], mwo, dx, tm,
                                                    f"mix_out_bwd_{l}")
        dwo = dwo.astype(BF16).reshape(N_DEV, D_MODEL // N_DEV, D_MODEL)
        small["gain_ssm_out"][l] = dgs.sum(0)
        small["gain_gm_out"][l] = dgg.sum(0)

        du_ssm, dglu, dglub, ddskip, dct, db, q = _ssm_bwd(
            z, h, dy_ssm, p["bblk"], p["cblk"], p["glu"], p["glub"], p["dskip"], p["revc"], batch, t_chunk,
            f"ssm_bwd_{l}")
        du_gm, dv_gm, dws, dbias, dgv = _gm_bwd(z, dy_gm, p["gv"], p["w_tril"], p["gm_bias"], gm_rows, f"gm_bwd_{l}")

        q = q.sum(0).reshape(2, 2, HALF_GROUPS, SSM_STATE)
        qr, qi = q[:, 0].reshape(SSM_GROUPS, SSM_STATE), q[:, 1].reshape(SSM_GROUPS, SSM_STATE)
        den = p["lr"] * p["lr"] + p["li"] * p["li"]
        d_re = (qr * p["lr"] + qi * p["li"]) / den
        d_im = (qi * p["lr"] - qr * p["li"]) / den
        dbb = jnp.stack([_block_diag_take(db[hf, :, k * HALF_ST:(k + 1) * HALF_ST], HALF_GROUPS, SSM_CH, SSM_STATE)
                         for k in range(2) for hf in range(2)]).reshape(2, SSM_GROUPS, SSM_CH, SSM_STATE)
        dcc = jnp.stack([_block_diag_take(dct[hf, :, k * HALF_ST:(k + 1) * HALF_ST], HALF_GROUPS, SSM_CH, SSM_STATE)
                         for k in range(2) for hf in range(2)]).reshape(2, SSM_GROUPS, SSM_CH, SSM_STATE)
        da_re, da_im, dlog_dt, db_re, db_im = p["disc_vjp"](
            (d_re, -d_im, jnp.swapaxes(dbb[0], 1, 2), jnp.swapaxes(dbb[1], 1, 2)))
        small["ssm_a_re"][l], small["ssm_a_im"][l], small["ssm_log_dt"][l] = da_re, da_im, dlog_dt
        small["ssm_b_re"][l], small["ssm_b_im"][l] = db_re, db_im
        small["ssm_c_re"][l], small["ssm_c_im"][l] = dcc[0], -dcc[1]
        small["ssm_d"][l] = ddskip.sum(0).reshape(SSM_GROUPS, SSM_CH)
        small["ssm_glu_w"][l] = jnp.concatenate(
            [_block_diag_take(dglu[:, :SSM_WIDTH], SSM_GROUPS, SSM_CH, SSM_CH),
             _block_diag_take(dglu[:, SSM_WIDTH:], SSM_GROUPS, SSM_CH, SSM_CH)], axis=2)
        dglub = dglub.sum(0)
        small["ssm_glu_b"][l] = jnp.concatenate(
            [dglub[:SSM_WIDTH].reshape(SSM_GROUPS, SSM_CH), dglub[SSM_WIDTH:].reshape(SSM_GROUPS, SSM_CH)], axis=1)
        small["gm_v_gain"][l] = dgv.sum(0)
        small["gm_w_s"][l] = jnp.where(tril[None], dws, 0.0)
        small["gm_b_s"][l] = dbias.reshape(GM_CHUNK, GM_HEADS, GM_HEAD_DIM).sum(-1).T

        dx, dwi, dgain = _mix_in_bwd(x1, p["gmix"], du_ssm, du_gm, dv_gm, dx, mwi, tm, f"mix_in_bwd_{l}")
        dwi = dwi.astype(BF16).reshape(N_DEV, IN_COLS // N_DEV, D_MODEL)
        token = send("mix", l, ("mix_w_in", "mix_w_out"), [dwi, dwo])
        small["norm_mix"][l] = dgain.sum(0)

        dx_out = dx
        dx, dgu, act, dgain, dyb = _ffn_bwd(x0, _behind(p["g1"], token), dx_out, gu1, *weights("ffn1", l), tm,
                                       f"ffn1_bwd_{l}")
        small["norm_ffn1"][l] = dgain.sum(0)
        if l > 0:
            dw_in = _ffn_dw_in(xn1, dgu, tk, f"ffn1_dw_in_{l}")
            dw_out = _ffn_dw_out(act, dyb, tk, f"ffn1_dw_out_{l}").reshape(N_DEV, FF_SHARD // 2, D_MODEL)
            token = send("ffn1", l, ("ffn1_w_in", "ffn1_w_out"), [dw_in, dw_out])
            continue
        small_g = [jnp.stack(small[k]) if k != "norm_final" else dnf.sum(0) for k in SMALL]
        total = sum(int(math.prod(w[k].shape)) for k in SMALL)
        rows = -(-total // (LANES * N_DEV * SUBLANES)) * N_DEV * SUBLANES
        g_all = _all_reduce_small(_pack(small_g, rows).reshape(N_DEV, rows // N_DEV, LANES), "all_reduce_small")
        dw_in = _ffn_dw_in(xn1, dgu, tk, f"ffn1_dw_in_{l}", after=g_all)
        token = send("ffn1_in", l, ("ffn1_w_in",), [dw_in])
        dw_out = _ffn_dw_out(act, dyb, tk, f"ffn1_dw_out_{l}", after=token).reshape(
            N_DEV, FF_SHARD // 2, D_MODEL)
        token = send("ffn1_out", l, ("ffn1_w_out",), [dw_out])

    grad_x = dx.reshape(batch, seq, D_MODEL)
    grads, deltas, new_m, new_v = {}, {}, {}, {}

    g_all = _behind(g_all.reshape(rows, LANES), token)
    like = [w[k] for k in SMALL]
    d_p, m_p, v_p = _adam_packed(g_all, _pack(like, rows), _pack([m[k] for k in SMALL], rows),
                                 _pack([v[k] for k in SMALL], rows), "adam_small")
    for k, g_, d_, m_, v_ in zip(SMALL, _unpack(g_all, like), _unpack(d_p, like), _unpack(m_p, like),
                                 _unpack(v_p, like)):
        grads[k], deltas[k], new_m[k], new_v[k] = g_, d_, m_, v_

    results = {}
    after = d_p
    for group, l, keys, handle in sent:
        landed = _exchange_wait(handle, after, True, f"reduce_scatter_wait_{group}_{l}")
        for k, parts in zip(keys, landed):
            view = (lambda a: jnp.swapaxes(a, 1, 2)) if k in TRANSPOSED else (lambda a: a)
            results[k] = _adam_sharded(parts, view(w[k]), view(m[k]), view(v[k]), l, results.get(k),
                                       f"adam_{k}_{l}")
            after = results[k][0]
    for k in BIG:
        view = (lambda a: jnp.swapaxes(a, 1, 2)) if k in TRANSPOSED else (lambda a: a)
        grads[k], deltas[k], new_m[k], new_v[k] = [view(a) for a in results[k]]
    return loss, grad_x, grads, deltas, new_m, new_v


def kernel(x, norm_ffn1, ffn1_w_in, ffn1_w_out, norm_mix, mix_w_in, ssm_a_re, ssm_a_im, ssm_log_dt, ssm_b_re, ssm_b_im, ssm_c_re, ssm_c_im, ssm_d, ssm_glu_w, ssm_glu_b, gm_v_gain, gm_w_s, gm_b_s, gain_ssm_out, gain_gm_out, mix_w_out, norm_ffn2, ffn2_w_in, ffn2_w_out, norm_final, loss_target, m_norm_ffn1, m_ffn1_w_in, m_ffn1_w_out, m_norm_mix, m_mix_w_in, m_ssm_a_re, m_ssm_a_im, m_ssm_log_dt, m_ssm_b_re, m_ssm_b_im, m_ssm_c_re, m_ssm_c_im, m_ssm_d, m_ssm_glu_w, m_ssm_glu_b, m_gm_v_gain, m_gm_w_s, m_gm_b_s, m_gain_ssm_out, m_gain_gm_out, m_mix_w_out, m_norm_ffn2, m_ffn2_w_in, m_ffn2_w_out, m_norm_final, v_norm_ffn1, v_ffn1_w_in, v_ffn1_w_out, v_norm_mix, v_mix_w_in, v_ssm_a_re, v_ssm_a_im, v_ssm_log_dt, v_ssm_b_re, v_ssm_b_im, v_ssm_c_re, v_ssm_c_im, v_ssm_d, v_ssm_glu_w, v_ssm_glu_b, v_gm_v_gain, v_gm_w_s, v_gm_b_s, v_gain_ssm_out, v_gain_gm_out, v_mix_w_out, v_norm_ffn2, v_ffn2_w_in, v_ffn2_w_out, v_norm_final):
    args = locals()
    w = {k: args[k] for k in WEIGHTS}
    m = {k: args["m_" + k] for k in WEIGHTS}
    v = {k: args["v_" + k] for k in WEIGHTS}
    loss, grad_x, grads, deltas, new_m, new_v = _step(x, loss_target, w, m, v)
    return (loss, grad_x, *[grads[k] for k in WEIGHTS], *[deltas[k] for k in WEIGHTS],
            *[new_m[k] for k in WEIGHTS], *[new_v[k] for k in WEIGHTS])
```

```python
import functools
import math

import jax
import jax.numpy as jnp
from jax import lax
from jax.experimental import pallas as pl
from jax.experimental.pallas import tpu as pltpu

F32 = jnp.float32
BF16 = jnp.bfloat16
MESH = pl.DeviceIdType.MESH
AXES = ("x", "y", "c")

N_DEV = 8
D_MODEL = 1024
D_FF = 2816
FF_SHARD = 2 * D_FF // N_DEV
FF_CHUNKS = 4
MXU_DIM = 256
FF_PIECES = tuple((lo, min(lo + MXU_DIM, FF_SHARD)) for lo in range(0, FF_SHARD, MXU_DIM))
SSM_WIDTH = 512
SSM_CH = 16
SSM_GROUPS = 32
SSM_STATE = 64
HALF_GROUPS = 16
HALF_IN = HALF_GROUPS * SSM_CH
HALF_ST = HALF_GROUPS * SSM_STATE
GM_WIDTH = 512
GM_HEADS = 4
GM_HEAD_DIM = 128
GM_CHUNK = 128
IN_COLS = SSM_WIDTH + 2 * GM_WIDTH
EPS = 1e-6
SUBLANES = 8
LANES = 128

ADAM_LR = 0.001
ADAM_B1 = 0.9
ADAM_B2 = 0.999
ADAM_EPS = 1e-08
ADAM_WD = 0.01
ADAM_STEP = 10

VMEM_LIMIT = 46 * 1024 * 1024


def _cp(*sem):
    return pltpu.CompilerParams(dimension_semantics=sem, vmem_limit_bytes=VMEM_LIMIT)


def _rms_fwd(x, g):
    r = lax.rsqrt(jnp.mean(x * x, axis=-1, keepdims=True) + EPS)
    xh = x * r
    return xh * g, xh, r


def _rms_bwd(dy, xh, r, g):
    dxh = dy * g
    dx = r * (dxh - xh * jnp.mean(dxh * xh, axis=-1, keepdims=True))
    return dx, dy * xh


def _rows8(a):
    m, n = a.shape
    return a.reshape(m // SUBLANES, SUBLANES, n).sum(axis=0)


_GELU_K = math.sqrt(2.0 / math.pi)
_GELU_C = 0.044715


def _gelu(x):
    th = jnp.tanh(_GELU_K * (x + _GELU_C * x * x * x))
    return 0.5 * x * (1.0 + th), th


def _gelu_grad(x, th):
    return 0.5 * (1.0 + th) + 0.5 * x * (1.0 - th * th) * (_GELU_K * (1.0 + 3.0 * _GELU_C * x * x))


def _dot(a, b):
    return jnp.dot(a, b, preferred_element_type=F32)


def _dot_nt(a, b):
    return lax.dot_general(a, b, (((1,), (1,)), ((), ())), preferred_element_type=F32)


def _dot_tn(a, b):
    return lax.dot_general(a, b, (((0,), (0,)), ((), ())), preferred_element_type=F32)


def _ffn_fwd(x, gain, w_in_ag, w_out_ag, tm, name):
    n = x.shape[0]

    def body(x_ref, g_ref, wg_ref, wu_ref, wo_ref, o_ref, xn_ref, gu_ref):
        j = pl.program_id(1)

        @pl.when(j == 0)
        def _():
            xv = x_ref[...]
            y, _, _ = _rms_fwd(xv, g_ref[...])
            xn_ref[...] = y.astype(BF16)
            o_ref[...] = xv

        xn = xn_ref[...]
        wo = wo_ref[...].reshape(FF_SHARD, D_MODEL)
        out = None
        for lo, hi in FF_PIECES:
            gg = _dot(xn, wg_ref[:, lo:hi])
            uu = _dot(xn, wu_ref[:, lo:hi])
            gu_ref[0, :, lo:hi] = gg.astype(BF16)
            gu_ref[1, :, lo:hi] = uu.astype(BF16)
            act = (gg * jax.nn.sigmoid(gg) * uu).astype(BF16)
            part = _dot(act, wo[lo:hi, :])
            out = part if out is None else out + part
        o_ref[...] += 0.5 * out

    return pl.pallas_call(
        body, name=name, grid=(n // tm, FF_CHUNKS),
        in_specs=[
            pl.BlockSpec((tm, D_MODEL), lambda i, j: (i, 0)),
            pl.BlockSpec((1, D_MODEL), lambda i, j: (0, 0)),
            pl.BlockSpec((None, D_MODEL, FF_SHARD), lambda i, j: (j, 0, 0)),
            pl.BlockSpec((None, D_MODEL, FF_SHARD), lambda i, j: (j + FF_CHUNKS, 0, 0)),
            pl.BlockSpec((2, FF_SHARD // 2, D_MODEL), lambda i, j: (j, 0, 0)),
        ],
        out_specs=[
            pl.BlockSpec((tm, D_MODEL), lambda i, j: (i, 0)),
            pl.BlockSpec((tm, D_MODEL), lambda i, j: (i, 0)),
            pl.BlockSpec((None, 2, tm, FF_SHARD), lambda i, j: (j, 0, i, 0)),
        ],
        out_shape=[
            jax.ShapeDtypeStruct((n, D_MODEL), F32),
            jax.ShapeDtypeStruct((n, D_MODEL), BF16),
            jax.ShapeDtypeStruct((FF_CHUNKS, 2, n, FF_SHARD), BF16),
        ],
        compiler_params=_cp("parallel", "arbitrary"),
    )(x, gain, w_in_ag, w_in_ag, w_out_ag)


def _ffn_bwd(x, gain, dy, gu, w_in_ag, w_out_ag, tm, name):
    n = x.shape[0]
    work = (n // tm) * FF_CHUNKS
    steps = work + 1
    first = lambda s: jnp.minimum(s, work - 1)
    second = lambda s: jnp.maximum(s - 1, 0)
    tile1, chunk1 = (lambda s: first(s) // FF_CHUNKS), (lambda s: first(s) % FF_CHUNKS)
    tile2, chunk2 = (lambda s: second(s) // FF_CHUNKS), (lambda s: second(s) % FF_CHUNKS)

    def body(x_ref, g_ref, dy_ref, dy1_ref, gu_ref, wg_ref, wu_ref, wo_ref, dx_ref, dgu_ref, act_ref, dgain_ref,
             dyb_ref, held_ref):
        s = pl.program_id(0)

        @pl.when(s == 0)
        def _():
            dgain_ref[...] = jnp.zeros_like(dgain_ref)
            held_ref[...] = jnp.zeros_like(held_ref)

        @pl.when(jnp.logical_and(chunk1(s) == 0, s < work))
        def _():
            dyb_ref[...] = (0.5 * dy1_ref[...]).astype(BF16)

        @pl.when(chunk2(s) == 0)
        def _():
            dx_ref[...] = jnp.zeros_like(dx_ref)

        held = held_ref[1 - s % 2]
        part = _dot_nt(held[0], wg_ref[...]) + _dot_nt(held[1], wu_ref[...])
        dx_ref[...] += jnp.where(s > 0, part, 0.0)

        dyb = dyb_ref[...]
        wo = wo_ref[...].reshape(FF_SHARD, D_MODEL)
        slot = s % 2
        for lo, hi in FF_PIECES:
            gg = gu_ref[0, :, lo:hi].astype(F32)
            uu = gu_ref[1, :, lo:hi].astype(F32)
            dact = _dot_nt(dyb, wo[lo:hi, :])
            sig = jax.nn.sigmoid(gg)
            silu = gg * sig
            act_ref[:, lo:hi] = (silu * uu).astype(BF16)
            du = (dact * silu).astype(BF16)
            dg = (dact * uu * (sig * (1.0 + gg * (1.0 - sig)))).astype(BF16)
            dgu_ref[0, :, lo:hi] = dg
            dgu_ref[1, :, lo:hi] = du
            held_ref[slot, 0, :, lo:hi] = dg
            held_ref[slot, 1, :, lo:hi] = du

        @pl.when(jnp.logical_and(chunk2(s) == FF_CHUNKS - 1, s > 0))
        def _():
            g = g_ref[...]
            _, xh, r = _rms_fwd(x_ref[...], g)
            dx, dgr = _rms_bwd(dx_ref[...], xh, r, g)
            dx_ref[...] = dy_ref[...] + dx
            dgain_ref[...] += _rows8(dgr)

    return pl.pallas_call(
        body, name=name, grid=(steps,),
        in_specs=[
            pl.BlockSpec((tm, D_MODEL), lambda s: (tile2(s), 0)),
            pl.BlockSpec((1, D_MODEL), lambda s: (0, 0)),
            pl.BlockSpec((tm, D_MODEL), lambda s: (tile2(s), 0)),
            pl.BlockSpec((tm, D_MODEL), lambda s: (tile1(s), 0)),
            pl.BlockSpec((None, 2, tm, FF_SHARD), lambda s: (chunk1(s), 0, tile1(s), 0)),
            pl.BlockSpec((None, D_MODEL, FF_SHARD), lambda s: (chunk2(s), 0, 0)),
            pl.BlockSpec((None, D_MODEL, FF_SHARD), lambda s: (chunk2(s) + FF_CHUNKS, 0, 0)),
            pl.BlockSpec((2, FF_SHARD // 2, D_MODEL), lambda s: (chunk1(s), 0, 0)),
        ],
        out_specs=[
            pl.BlockSpec((tm, D_MODEL), lambda s: (tile2(s), 0)),
            pl.BlockSpec((None, 2, tm, FF_SHARD), lambda s: (chunk1(s), 0, tile1(s), 0)),
            pl.BlockSpec((None, tm, FF_SHARD), lambda s: (chunk1(s), tile1(s), 0)),
            pl.BlockSpec((SUBLANES, D_MODEL), lambda s: (0, 0)),
            pl.BlockSpec((tm, D_MODEL), lambda s: (tile1(s), 0)),
        ],
        out_shape=[
            jax.ShapeDtypeStruct((n, D_MODEL), F32),
            jax.ShapeDtypeStruct((FF_CHUNKS, 2, n, FF_SHARD), BF16),
            jax.ShapeDtypeStruct((FF_CHUNKS, n, FF_SHARD), BF16),
            jax.ShapeDtypeStruct((SUBLANES, D_MODEL), F32),
            jax.ShapeDtypeStruct((n, D_MODEL), BF16),
        ],
        scratch_shapes=[pltpu.VMEM((2, 2, tm, FF_SHARD), BF16)],
        compiler_params=_cp("arbitrary"),
    )(x, gain, dy, dy, gu, w_in_ag, w_in_ag, w_out_ag)


def _ffn_dw_in(xn, dgu, tk, name, after=None):
    n = xn.shape[0]
    nk = n // tk
    deps = [] if after is None else [after]

    def body(a_ref, b_ref, *rest):
        o_ref, acc_ref = rest[-2:]
        k = pl.program_id(2)

        @pl.when(k == 0)
        def _():
            acc_ref[...] = jnp.zeros_like(acc_ref)

        acc_ref[...] += _dot_tn(b_ref[...], a_ref[...])

        @pl.when(k == nk - 1)
        def _():
            o_ref[...] = acc_ref[...].astype(BF16)

    return pl.pallas_call(
        body, name=name, grid=(FF_CHUNKS, 2, nk),
        in_specs=[
            pl.BlockSpec((tk, D_MODEL), lambda j, p, k: (k, 0)),
            pl.BlockSpec((None, None, tk, FF_SHARD), lambda j, p, k: (j, p, k, 0)),
        ] + [pl.BlockSpec(memory_space=pl.ANY)] * len(deps),
        out_specs=pl.BlockSpec((None, FF_SHARD, D_MODEL), lambda j, p, k: (FF_CHUNKS * p + j, 0, 0)),
        out_shape=jax.ShapeDtypeStruct((N_DEV, FF_SHARD, D_MODEL), BF16),
        scratch_shapes=[pltpu.VMEM((FF_SHARD, D_MODEL), F32)],
        compiler_params=_cp("parallel", "parallel", "arbitrary"),
    )(xn, dgu, *deps)


def _ffn_dw_out(act, dyb, tk, name, after=None):
    n = act.shape[1]
    nk = n // tk
    deps = [] if after is None else [after]

    def body(a_ref, b_ref, *rest):
        o_ref, acc_ref = rest[-2:]
        k = pl.program_id(1)

        @pl.when(k == 0)
        def _():
            acc_ref[...] = jnp.zeros_like(acc_ref)

        acc_ref[...] += _dot_tn(a_ref[...], b_ref[...])

        @pl.when(k == nk - 1)
        def _():
            o_ref[...] = acc_ref[...].astype(BF16)

    return pl.pallas_call(
        body, name=name, grid=(FF_CHUNKS, nk),
        in_specs=[
            pl.BlockSpec((None, tk, FF_SHARD), lambda j, k: (j, k, 0)),
            pl.BlockSpec((tk, D_MODEL), lambda j, k: (k, 0)),
        ] + [pl.BlockSpec(memory_space=pl.ANY)] * len(deps),
        out_specs=pl.BlockSpec((None, FF_SHARD, D_MODEL), lambda j, k: (j, 0, 0)),
        out_shape=jax.ShapeDtypeStruct((FF_CHUNKS, FF_SHARD, D_MODEL), BF16),
        scratch_shapes=[pltpu.VMEM((FF_SHARD, D_MODEL), F32)],
        compiler_params=_cp("parallel", "arbitrary"),
    )(act, dyb, *deps)


def _mix_in_fwd(x, gain, w, tm, name):
    n = x.shape[0]

    def body(x_ref, g_ref, w_ref, z_ref):
        y, _, _ = _rms_fwd(x_ref[...], g_ref[...])
        z_ref[...] = _dot(y.astype(BF16), w_ref[...])

    return pl.pallas_call(
        body, name=name, grid=(n // tm,),
        in_specs=[
            pl.BlockSpec((tm, D_MODEL), lambda i: (i, 0)),
            pl.BlockSpec((1, D_MODEL), lambda i: (0, 0)),
            pl.BlockSpec((D_MODEL, IN_COLS), lambda i: (0, 0)),
        ],
        out_specs=pl.BlockSpec((tm, IN_COLS), lambda i: (i, 0)),
        out_shape=jax.ShapeDtypeStruct((n, IN_COLS), F32),
        compiler_params=_cp("parallel"),
    )(x, gain, w)


def _mix_in_bwd(x, gain, du_ssm, du_gm, dv_gm, d_res, w, tm, name):
    n = x.shape[0]

    def body(x_ref, g_ref, d0_ref, d1_ref, d2_ref, dres_ref, w_ref, dx_ref, dw_ref, dgain_ref):
        i = pl.program_id(0)

        @pl.when(i == 0)
        def _():
            dw_ref[...] = jnp.zeros_like(dw_ref)
            dgain_ref[...] = jnp.zeros_like(dgain_ref)

        g = g_ref[...]
        y, xh, r = _rms_fwd(x_ref[...], g)
        xn = y.astype(BF16)
        dxn = jnp.zeros((tm, D_MODEL), F32)
        for k, d_ref in enumerate((d0_ref, d1_ref, d2_ref)):
            dz = d_ref[...].astype(BF16)
            cols = slice(k * SSM_WIDTH, (k + 1) * SSM_WIDTH)
            dxn += _dot_nt(dz, w_ref[:, cols])
            dw_ref[cols, :] += _dot_tn(dz, xn)
        dx, dgr = _rms_bwd(dxn, xh, r, g)
        dx_ref[...] = dres_ref[...] + dx
        dgain_ref[...] += _rows8(dgr)

    row = lambda i: (i, 0)
    fixed = lambda i: (0, 0)
    return pl.pallas_call(
        body, name=name, grid=(n // tm,),
        in_specs=[
            pl.BlockSpec((tm, D_MODEL), row),
            pl.BlockSpec((1, D_MODEL), fixed),
            pl.BlockSpec((tm, SSM_WIDTH), row),
            pl.BlockSpec((tm, GM_WIDTH), row),
            pl.BlockSpec((tm, GM_WIDTH), row),
            pl.BlockSpec((tm, D_MODEL), row),
            pl.BlockSpec((D_MODEL, IN_COLS), fixed),
        ],
        out_specs=[
            pl.BlockSpec((tm, D_MODEL), row),
            pl.BlockSpec((IN_COLS, D_MODEL), fixed),
            pl.BlockSpec((SUBLANES, D_MODEL), fixed),
        ],
        out_shape=[
            jax.ShapeDtypeStruct((n, D_MODEL), F32),
            jax.ShapeDtypeStruct((IN_COLS, D_MODEL), F32),
            jax.ShapeDtypeStruct((SUBLANES, D_MODEL), F32),
        ],
        compiler_params=_cp("arbitrary"),
    )(x, gain, du_ssm, du_gm, dv_gm, d_res, w)


def _mix_out_fwd(y_ssm, y_gm, g_ssm, g_gm, w, x, tm, name):
    n = x.shape[0]

    def body(ys_ref, yg_ref, gs_ref, gg_ref, w_ref, x_ref, o_ref):
        a, _, _ = _rms_fwd(ys_ref[...], gs_ref[...])
        b, _, _ = _rms_fwd(yg_ref[...], gg_ref[...])
        o_ref[...] = (x_ref[...] + _dot(a.astype(BF16), w_ref[0:SSM_WIDTH, :])
                      + _dot(b.astype(BF16), w_ref[SSM_WIDTH:D_MODEL, :]))

    row = lambda i: (i, 0)
    fixed = lambda i: (0, 0)
    return pl.pallas_call(
        body, name=name, grid=(n // tm,),
        in_specs=[
            pl.BlockSpec((tm, SSM_WIDTH), row), pl.BlockSpec((tm, GM_WIDTH), row),
            pl.BlockSpec((1, SSM_WIDTH), fixed), pl.BlockSpec((1, GM_WIDTH), fixed),
            pl.BlockSpec((D_MODEL, D_MODEL), fixed), pl.BlockSpec((tm, D_MODEL), row),
        ],
        out_specs=pl.BlockSpec((tm, D_MODEL), row),
        out_shape=jax.ShapeDtypeStruct((n, D_MODEL), F32),
        compiler_params=_cp("parallel"),
    )(y_ssm, y_gm, g_ssm, g_gm, w, x)


def _mix_out_bwd(y_ssm, y_gm, g_ssm, g_gm, w, dx, tm, name):
    n = dx.shape[0]

    def body(ys_ref, yg_ref, gs_ref, gg_ref, w_ref, dx_ref, dys_ref, dyg_ref, dw_ref, dgs_ref, dgg_ref):
        i = pl.program_id(0)

        @pl.when(i == 0)
        def _():
            dw_ref[...] = jnp.zeros_like(dw_ref)
            dgs_ref[...] = jnp.zeros_like(dgs_ref)
            dgg_ref[...] = jnp.zeros_like(dgg_ref)

        dxb = dx_ref[...].astype(BF16)
        parts = ((ys_ref, gs_ref, dys_ref, dgs_ref, 0), (yg_ref, gg_ref, dyg_ref, dgg_ref, SSM_WIDTH))
        for y_ref, g_ref, dy_ref, dg_ref, off in parts:
            g = g_ref[...]
            yn, xh, r = _rms_fwd(y_ref[...], g)
            rows = slice(off, off + SSM_WIDTH)
            dyn = _dot_nt(dxb, w_ref[rows, :])
            dw_ref[rows, :] += _dot_tn(yn.astype(BF16), dxb)
            dy, dgr = _rms_bwd(dyn, xh, r, g)
            dy_ref[...] = dy
            dg_ref[...] += _rows8(dgr)

    row = lambda i: (i, 0)
    fixed = lambda i: (0, 0)
    return pl.pallas_call(
        body, name=name, grid=(n // tm,),
        in_specs=[
            pl.BlockSpec((tm, SSM_WIDTH), row), pl.BlockSpec((tm, GM_WIDTH), row),
            pl.BlockSpec((1, SSM_WIDTH), fixed), pl.BlockSpec((1, GM_WIDTH), fixed),
            pl.BlockSpec((D_MODEL, D_MODEL), fixed), pl.BlockSpec((tm, D_MODEL), row),
        ],
        out_specs=[
            pl.BlockSpec((tm, SSM_WIDTH), row), pl.BlockSpec((tm, GM_WIDTH), row),
            pl.BlockSpec((D_MODEL, D_MODEL), fixed),
            pl.BlockSpec((SUBLANES, SSM_WIDTH), fixed), pl.BlockSpec((SUBLANES, GM_WIDTH), fixed),
        ],
        out_shape=[
            jax.ShapeDtypeStruct((n, SSM_WIDTH), F32), jax.ShapeDtypeStruct((n, GM_WIDTH), F32),
            jax.ShapeDtypeStruct((D_MODEL, D_MODEL), F32),
            jax.ShapeDtypeStruct((SUBLANES, SSM_WIDTH), F32), jax.ShapeDtypeStruct((SUBLANES, GM_WIDTH), F32),
        ],
        compiler_params=_cp("arbitrary"),
    )(y_ssm, y_gm, g_ssm, g_gm, w, dx)


SCAN_W = 512
SCAN_PIECES = HALF_ST // SCAN_W


def _scan_tiles(src_ref, dst_ref, dst_off, c_ref, half, carry_ref, n_tiles, reverse, extra=None):
    shifts = (1, 2, 4)
    carry_row = 0 if reverse else SUBLANES - 1

    def cols(piece, im):
        lo = im * HALF_ST + piece * SCAN_W
        return slice(lo, lo + SCAN_W)

    def step(t, state):
        carries, accs = state
        k = (n_tiles - 1 - t) if reverse else t
        rows = slice(k * SUBLANES, (k + 1) * SUBLANES)
        new_carries, new_accs = [], []
        for piece in range(SCAN_PIECES):
            cr, ci = carries[piece]
            xr0 = src_ref[rows, cols(piece, 0)]
            xi0 = src_ref[rows, cols(piece, 1)]
            xr, xi = xr0, xi0
            for si, s in enumerate(shifts):
                ar = c_ref[half, si, :, cols(piece, 0)]
                ai = c_ref[half, si, :, cols(piece, 1)]
                sh = (SUBLANES - s) if reverse else s
                sr = pltpu.roll(xr, sh, 0)
                sm = pltpu.roll(xi, sh, 0)
                xr, xi = xr + (ar * sr - ai * sm), xi + (ar * sm + ai * sr)
            pr = c_ref[half, 3, :, cols(piece, 0)]
            pi = c_ref[half, 3, :, cols(piece, 1)]
            hr = xr + (pr * cr - pi * ci)
            hi = xi + (pr * ci + pi * cr)
            dst_ref[rows, pl.ds(dst_off + piece * SCAN_W, SCAN_W)] = hr
            dst_ref[rows, pl.ds(dst_off + HALF_ST + piece * SCAN_W, SCAN_W)] = hi
            new_carries.append((jnp.broadcast_to(hr[carry_row:carry_row + 1, :], (SUBLANES, SCAN_W)),
                                jnp.broadcast_to(hi[carry_row:carry_row + 1, :], (SUBLANES, SCAN_W))))
            if extra is not None:
                new_accs.append(extra(rows, piece, (xr0, xi0), (hr, hi), accs[piece]))
        return tuple(new_carries), tuple(new_accs)

    base = half * 2 * HALF_ST
    carries0 = tuple((carry_ref[:, pl.ds(base + p * SCAN_W, SCAN_W)],
                      carry_ref[:, pl.ds(base + HALF_ST + p * SCAN_W, SCAN_W)]) for p in range(SCAN_PIECES))
    zero = jnp.zeros((SUBLANES, SCAN_W), F32)
    accs0 = tuple((zero, zero) for _ in range(SCAN_PIECES)) if extra is not None else ()
    state = (carries0, accs0)
    for t in range(n_tiles):
        state = step(t, state)
    carries, accs = state
    for p in range(SCAN_PIECES):
        carry_ref[:, pl.ds(base + p * SCAN_W, SCAN_W)] = carries[p][0]
        carry_ref[:, pl.ds(base + HALF_ST + p * SCAN_W, SCAN_W)] = carries[p][1]
    return accs


def _ssm_tail(hb, u, c_ref, glu_ref, glub_ref, dskip_ref):
    ypre = u * dskip_ref[...]
    parts = []
    for half in range(2):
        parts.append(_dot(hb[half], c_ref[half]))
    ypre = ypre + jnp.concatenate(parts, axis=1)
    yg, th = _gelu(ypre)
    zz = _dot(yg.astype(BF16), glu_ref[...]) + glub_ref[...]
    z1, z2 = zz[:, :SSM_WIDTH], zz[:, SSM_WIDTH:]
    sg = jax.nn.sigmoid(z2)
    return ypre, th, yg, z1, sg


def _ssm_fwd(z, bblk, cblk, glu, glub, dskip, fwdc, batch, t_chunk, name):
    n = z.shape[0]
    nk = n // batch // t_chunk
    n_tiles = t_chunk // SUBLANES

    def body(u_ref, b_ref, c_ref, glu_ref, glub_ref, dskip_ref, k_ref, y_ref, h_ref, bu_ref, carry_ref):
        @pl.when(pl.program_id(1) == 0)
        def _():
            carry_ref[...] = jnp.zeros_like(carry_ref)

        u = u_ref[...]
        ub = u.astype(BF16)
        for half in range(2):
            bu_ref[half] = _dot(ub[:, half * HALF_IN:(half + 1) * HALF_IN], b_ref[half])
            _scan_tiles(bu_ref.at[half], h_ref, half * 2 * HALF_ST, k_ref, half, carry_ref, n_tiles, False)
        hb = [h_ref[:, half * 2 * HALF_ST:(half + 1) * 2 * HALF_ST].astype(BF16) for half in range(2)]
        _, _, _, z1, sg = _ssm_tail(hb, u, c_ref, glu_ref, glub_ref, dskip_ref)
        y_ref[...] = z1 * sg

    fixed2 = lambda b, k: (0, 0)
    fixed3 = lambda b, k: (0, 0, 0)
    row = lambda b, k: (b * nk + k, 0)
    return pl.pallas_call(
        body, name=name, grid=(batch, nk),
        in_specs=[
            pl.BlockSpec((t_chunk, SSM_WIDTH), row),
            pl.BlockSpec((2, HALF_IN, 2 * HALF_ST), fixed3),
            pl.BlockSpec((2, 2 * HALF_ST, HALF_IN), fixed3),
            pl.BlockSpec((SSM_WIDTH, 2 * SSM_WIDTH), fixed2),
            pl.BlockSpec((1, 2 * SSM_WIDTH), fixed2),
            pl.BlockSpec((1, SSM_WIDTH), fixed2),
            pl.BlockSpec((2, 4, SUBLANES, 2 * HALF_ST), lambda b, k: (0, 0, 0, 0)),
        ],
        out_specs=[pl.BlockSpec((t_chunk, SSM_WIDTH), row), pl.BlockSpec((t_chunk, 4 * HALF_ST), row)],
        out_shape=[jax.ShapeDtypeStruct((n, SSM_WIDTH), F32), jax.ShapeDtypeStruct((n, 4 * HALF_ST), F32)],
        scratch_shapes=[pltpu.VMEM((2, t_chunk, 2 * HALF_ST), F32), pltpu.VMEM((SUBLANES, 4 * HALF_ST), F32)],
        compiler_params=_cp("parallel", "arbitrary"),
    )(z, bblk, cblk, glu, glub, dskip, fwdc)


def _ssm_bwd(z, h, dy, bblk, cblk, glu, glub, dskip, revc, batch, t_chunk, name):
    n = z.shape[0]
    nk = n // batch // t_chunk
    n_tiles = t_chunk // SUBLANES

    def body(u_ref, h_ref, dy_ref, b_ref, c_ref, glu_ref, glub_ref, dskip_ref, k_ref,
             du_ref, dglu_ref, dglub_ref, ddskip_ref, dct_ref, db_ref, q_ref, g_ref, carry_ref):
        first = jnp.logical_and(pl.program_id(0) == 0, pl.program_id(1) == 0)

        @pl.when(first)
        def _():
            for r in (dglu_ref, dglub_ref, ddskip_ref, dct_ref, db_ref, q_ref):
                r[...] = jnp.zeros_like(r)

        @pl.when(pl.program_id(1) == 0)
        def _():
            carry_ref[...] = jnp.zeros_like(carry_ref)

        u = u_ref[...]
        ub = u.astype(BF16)
        hb = [h_ref[:, half * 2 * HALF_ST:(half + 1) * 2 * HALF_ST].astype(BF16) for half in range(2)]
        ypre, th, yg, z1, sg = _ssm_tail(hb, u, c_ref, glu_ref, glub_ref, dskip_ref)
        dout = dy_ref[...]
        dz = jnp.concatenate([dout * sg, dout * z1 * sg * (1.0 - sg)], axis=1)
        dzb = dz.astype(BF16)
        dglu_ref[...] += _dot_tn(yg.astype(BF16), dzb)
        dglub_ref[...] += _rows8(dz)
        dypre = _dot_nt(dzb, glu_ref[...]) * _gelu_grad(ypre, th)
        ddskip_ref[...] += _rows8(dypre * u)
        dypb = dypre.astype(BF16)
        du_parts = []
        for half in range(2):
            dyp_h = dypb[:, half * HALF_IN:(half + 1) * HALF_IN]
            dct_ref[half] += _dot_tn(dyp_h, hb[half])
            g_ref[half] = _dot_nt(dyp_h, c_ref[half])

            def extra(rows, piece, x_in, g_out, acc, half=half):
                er, ei = g_out[0] - x_in[0], g_out[1] - x_in[1]
                base = half * 2 * HALF_ST + piece * SCAN_W
                hr = h_ref[rows, pl.ds(base, SCAN_W)]
                hi = h_ref[rows, pl.ds(base + HALF_ST, SCAN_W)]
                return acc[0] + (er * hr + ei * hi), acc[1] + (er * hi - ei * hr)

            accs = _scan_tiles(g_ref.at[half], g_ref.at[half], 0, k_ref, half, carry_ref, n_tiles, True, extra)
            for piece in range(SCAN_PIECES):
                base = half * 2 * HALF_ST + piece * SCAN_W
                q_ref[:, pl.ds(base, SCAN_W)] += accs[piece][0]
                q_ref[:, pl.ds(base + HALF_ST, SCAN_W)] += accs[piece][1]
            gb = g_ref[half].astype(BF16)
            db_ref[half] += _dot_tn(ub[:, half * HALF_IN:(half + 1) * HALF_IN], gb)
            du_parts.append(_dot_nt(gb, b_ref[half]))
        du_ref[...] = dypre * dskip_ref[...] + jnp.concatenate(du_parts, axis=1)

    fixed2 = lambda b, k: (0, 0)
    fixed3 = lambda b, k: (0, 0, 0)
    row = lambda b, k: (b * nk + (nk - 1 - k), 0)
    return pl.pallas_call(
        body, name=name, grid=(batch, nk),
        in_specs=[
            pl.BlockSpec((t_chunk, SSM_WIDTH), row),
            pl.BlockSpec((t_chunk, 4 * HALF_ST), row),
            pl.BlockSpec((t_chunk, SSM_WIDTH), row),
            pl.BlockSpec((2, HALF_IN, 2 * HALF_ST), fixed3),
            pl.BlockSpec((2, 2 * HALF_ST, HALF_IN), fixed3),
            pl.BlockSpec((SSM_WIDTH, 2 * SSM_WIDTH), fixed2),
            pl.BlockSpec((1, 2 * SSM_WIDTH), fixed2),
            pl.BlockSpec((1, SSM_WIDTH), fixed2),
            pl.BlockSpec((2, 4, SUBLANES, 2 * HALF_ST), lambda b, k: (0, 0, 0, 0)),
        ],
        out_specs=[
            pl.BlockSpec((t_chunk, SSM_WIDTH), row),
            pl.BlockSpec((SSM_WIDTH, 2 * SSM_WIDTH), fixed2),
            pl.BlockSpec((SUBLANES, 2 * SSM_WIDTH), fixed2),
            pl.BlockSpec((SUBLANES, SSM_WIDTH), fixed2),
            pl.BlockSpec((2, HALF_IN, 2 * HALF_ST), fixed3),
            pl.BlockSpec((2, HALF_IN, 2 * HALF_ST), fixed3),
            pl.BlockSpec((SUBLANES, 4 * HALF_ST), fixed2),
        ],
        out_shape=[
            jax.ShapeDtypeStruct((n, SSM_WIDTH), F32),
            jax.ShapeDtypeStruct((SSM_WIDTH, 2 * SSM_WIDTH), F32),
            jax.ShapeDtypeStruct((SUBLANES, 2 * SSM_WIDTH), F32),
            jax.ShapeDtypeStruct((SUBLANES, SSM_WIDTH), F32),
            jax.ShapeDtypeStruct((2, HALF_IN, 2 * HALF_ST), F32),
            jax.ShapeDtypeStruct((2, HALF_IN, 2 * HALF_ST), F32),
            jax.ShapeDtypeStruct((SUBLANES, 4 * HALF_ST), F32),
        ],
        scratch_shapes=[pltpu.VMEM((2, t_chunk, 2 * HALF_ST), F32), pltpu.VMEM((SUBLANES, 4 * HALF_ST), F32)],
        compiler_params=_cp("arbitrary", "arbitrary"),
    )(z, h, dy, bblk, cblk, glu, glub, dskip, revc)


def _gm_chunk_fwd(u, v, gain_ref, w_ref, bias_ref):
    ug, thu = _gelu(u)
    vg, thv = _gelu(v)
    rs, vns, ss = [], [], []
    for hh in range(GM_HEADS):
        cs = slice(hh * GM_HEAD_DIM, (hh + 1) * GM_HEAD_DIM)
        vn, _, r = _rms_fwd(vg[:, cs], gain_ref[:, cs])
        s = _dot(w_ref[hh], vn.astype(BF16)) + bias_ref[:, cs]
        rs.append(r)
        vns.append(vn)
        ss.append(s)
    return ug, thu, thv, vg, rs, vns, ss


def _gm_fwd(z, gain, w_tril, bias, rows, name):
    n = z.shape[0]
    chunks = rows // GM_CHUNK

    def body(u_ref, v_ref, gain_ref, w_ref, bias_ref, y_ref):
        for c in range(chunks):
            rs_ = slice(c * GM_CHUNK, (c + 1) * GM_CHUNK)
            ug, _, _, _, _, _, ss = _gm_chunk_fwd(u_ref[rs_, :], v_ref[rs_, :], gain_ref, w_ref, bias_ref)
            y_ref[rs_, :] = ug * jnp.concatenate(ss, axis=1)

    return pl.pallas_call(
        body, name=name, grid=(n // rows,),
        in_specs=[
            pl.BlockSpec((rows, GM_WIDTH), lambda i: (i, 1)),
            pl.BlockSpec((rows, GM_WIDTH), lambda i: (i, 2)),
            pl.BlockSpec((1, GM_WIDTH), lambda i: (0, 0)),
            pl.BlockSpec((GM_HEADS, GM_CHUNK, GM_CHUNK), lambda i: (0, 0, 0)),
            pl.BlockSpec((GM_CHUNK, GM_WIDTH), lambda i: (0, 0)),
        ],
        out_specs=pl.BlockSpec((rows, GM_WIDTH), lambda i: (i, 0)),
        out_shape=jax.ShapeDtypeStruct((n, GM_WIDTH), F32),
        compiler_params=_cp("parallel"),
    )(z, z, gain, w_tril, bias)


def _gm_bwd(z, dy, gain, w_tril, bias, rows, name):
    n = z.shape[0]
    chunks = rows // GM_CHUNK

    def body(u_ref, v_ref, dy_ref, gain_ref, w_ref, bias_ref, du_ref, dv_ref, dw_ref, dbias_ref, dgain_ref):
        @pl.when(pl.program_id(0) == 0)
        def _():
            dw_ref[...] = jnp.zeros_like(dw_ref)
            dbias_ref[...] = jnp.zeros_like(dbias_ref)
            dgain_ref[...] = jnp.zeros_like(dgain_ref)

        for c in range(chunks):
            rs_ = slice(c * GM_CHUNK, (c + 1) * GM_CHUNK)
            u, v = u_ref[rs_, :], v_ref[rs_, :]
            ug, thu, thv, vg, rs, vns, ss = _gm_chunk_fwd(u, v, gain_ref, w_ref, bias_ref)
            dout = dy_ref[rs_, :]
            ds = dout * ug
            du_ref[rs_, :] = dout * jnp.concatenate(ss, axis=1) * _gelu_grad(u, thu)
            dbias_ref[...] += ds
            dvg_parts, dgain_parts = [], []
            for hh in range(GM_HEADS):
                cs = slice(hh * GM_HEAD_DIM, (hh + 1) * GM_HEAD_DIM)
                dsb = ds[:, cs].astype(BF16)
                dvn = _dot_tn(w_ref[hh], dsb)
                dw_ref[hh] += _dot_nt(dsb, vns[hh].astype(BF16))
                g = gain_ref[:, cs]
                xh = vg[:, cs] * rs[hh]
                dvg, dgr = _rms_bwd(dvn, xh, rs[hh], g)
                dvg_parts.append(dvg)
                dgain_parts.append(dgr)
            dv_ref[rs_, :] = jnp.concatenate(dvg_parts, axis=1) * _gelu_grad(v, thv)
            dgain_ref[...] += _rows8(jnp.concatenate(dgain_parts, axis=1))

    row = lambda i: (i, 0)
    return pl.pallas_call(
        body, name=name, grid=(n // rows,),
        in_specs=[
            pl.BlockSpec((rows, GM_WIDTH), lambda i: (i, 1)),
            pl.BlockSpec((rows, GM_WIDTH), lambda i: (i, 2)),
            pl.BlockSpec((rows, GM_WIDTH), row),
            pl.BlockSpec((1, GM_WIDTH), lambda i: (0, 0)),
            pl.BlockSpec((GM_HEADS, GM_CHUNK, GM_CHUNK), lambda i: (0, 0, 0)),
            pl.BlockSpec((GM_CHUNK, GM_WIDTH), lambda i: (0, 0)),
        ],
        out_specs=[
            pl.BlockSpec((rows, GM_WIDTH), row), pl.BlockSpec((rows, GM_WIDTH), row),
            pl.BlockSpec((GM_HEADS, GM_CHUNK, GM_CHUNK), lambda i: (0, 0, 0)),
            pl.BlockSpec((GM_CHUNK, GM_WIDTH), lambda i: (0, 0)),
            pl.BlockSpec((SUBLANES, GM_WIDTH), lambda i: (0, 0)),
        ],
        out_shape=[
            jax.ShapeDtypeStruct((n, GM_WIDTH), F32), jax.ShapeDtypeStruct((n, GM_WIDTH), F32),
            jax.ShapeDtypeStruct((GM_HEADS, GM_CHUNK, GM_CHUNK), F32),
            jax.ShapeDtypeStruct((GM_CHUNK, GM_WIDTH), F32),
            jax.ShapeDtypeStruct((SUBLANES, GM_WIDTH), F32),
        ],
        compiler_params=_cp("arbitrary"),
    )(z, z, dy, gain, w_tril, bias)


def _loss_head(x, gain, target, tm, name):
    n = x.shape[0]

    def body(x_ref, g_ref, t_ref, dx_ref, sq_ref, dgain_ref):
        @pl.when(pl.program_id(0) == 0)
        def _():
            sq_ref[...] = jnp.zeros_like(sq_ref)
            dgain_ref[...] = jnp.zeros_like(dgain_ref)

        g = g_ref[...]
        y, xh, r = _rms_fwd(x_ref[...], g)
        err = y - t_ref[...]
        sq_ref[...] += _rows8(err * err)
        dx, dgr = _rms_bwd(err * (1.0 / D_MODEL), xh, r, g)
        dx_ref[...] = dx
        dgain_ref[...] += _rows8(dgr)

    row = lambda i: (i, 0)
    fixed = lambda i: (0, 0)
    return pl.pallas_call(
        body, name=name, grid=(n // tm,),
        in_specs=[pl.BlockSpec((tm, D_MODEL), row), pl.BlockSpec((1, D_MODEL), fixed), pl.BlockSpec((tm, D_MODEL), row)],
        out_specs=[pl.BlockSpec((tm, D_MODEL), row), pl.BlockSpec((SUBLANES, D_MODEL), fixed),
                   pl.BlockSpec((SUBLANES, D_MODEL), fixed)],
        out_shape=[jax.ShapeDtypeStruct((n, D_MODEL), F32), jax.ShapeDtypeStruct((SUBLANES, D_MODEL), F32),
                   jax.ShapeDtypeStruct((SUBLANES, D_MODEL), F32)],
        compiler_params=_cp("arbitrary"),
    )(x, gain, target)


def _adam_math(w, g, m, v):
    m2 = ADAM_B1 * m + (1.0 - ADAM_B1) * g
    v2 = ADAM_B2 * v + (1.0 - ADAM_B2) * (g * g)
    m_hat = m2 / (1.0 - ADAM_B1 ** ADAM_STEP)
    v_hat = v2 / (1.0 - ADAM_B2 ** ADAM_STEP)
    delta = -ADAM_LR * (m_hat / (jnp.sqrt(v_hat) + ADAM_EPS) + ADAM_WD * w)
    return delta, m2, v2


def _adam_sharded(parts, w, m, v, layer, earlier, name):
    depth, r, c = w.shape
    tr = max(t for t in range(16, 353, 16) if r % t == 0)

    def body(p_ref, w_ref, m_ref, v_ref, *rest):
        g_ref, d_ref, m2_ref, v2_ref = rest[-4:]
        g = p_ref[0].astype(F32)
        for s in range(1, N_DEV):
            g = g + p_ref[s].astype(F32)
        delta, m2, v2 = _adam_math(w_ref[...], g, m_ref[...], v_ref[...])
        g_ref[...] = g
        d_ref[...] = delta
        m2_ref[...] = m2
        v2_ref[...] = v2

    blk = pl.BlockSpec((None, tr, c), lambda i: (layer, i, 0))
    extra = [] if earlier is None else list(earlier)
    return pl.pallas_call(
        body, name=name, grid=(r // tr,),
        in_specs=[pl.BlockSpec((N_DEV, tr, c), lambda i: (0, i, 0)), blk, blk, blk]
        + [pl.BlockSpec(memory_space=pl.ANY)] * len(extra),
        out_specs=[blk, blk, blk, blk],
        out_shape=[jax.ShapeDtypeStruct((depth, r, c), F32)] * 4,
        input_output_aliases={4 + i: i for i in range(len(extra))},
        compiler_params=_cp("parallel"),
    )(parts, w, m, v, *extra)


def _adam_packed(g, w, m, v, name):
    r, c = g.shape

    def body(g_ref, w_ref, m_ref, v_ref, d_ref, m2_ref, v2_ref):
        delta, m2, v2 = _adam_math(w_ref[...], g_ref[...], m_ref[...], v_ref[...])
        d_ref[...] = delta
        m2_ref[...] = m2
        v2_ref[...] = v2

    blk = pl.BlockSpec((r, c), lambda i: (0, 0))
    return pl.pallas_call(
        body, name=name, grid=(1,),
        in_specs=[blk, blk, blk, blk], out_specs=[blk, blk, blk],
        out_shape=[jax.ShapeDtypeStruct((r, c), F32)] * 3,
        compiler_params=_cp("arbitrary"),
    )(g, w, m, v)


def _my_place():
    return lax.axis_index("x"), lax.axis_index("y"), lax.axis_index("c")


def _flip(place, rel):
    x, y, c = place
    return (1 - x if rel & 4 else x, 1 - y if rel & 2 else y, 1 - c if rel & 1 else c)


def _index(place):
    return 4 * place[0] + 2 * place[1] + place[2]


def _all_gather(shards, name):
    na = len(shards)

    def body(*refs):
        xs, outs = refs[:na], refs[na:2 * na]
        send_sems, recv_sems, local_sems = refs[2 * na:]
        me = _my_place()
        sibling = _flip(me, 1)
        chips = [_flip(me, 4), _flip(me, 2), _flip(me, 6)]

        def copy(a, k, block, to, src=None):
            slot = outs[a].at[_index(block)]
            return pltpu.make_async_remote_copy(
                src_ref=slot if src is None else src, dst_ref=slot,
                send_sem=send_sems.at[a, k], recv_sem=recv_sems.at[a, k],
                device_id=to, device_id_type=MESH)

        mine = [pltpu.make_async_copy(xs[a], outs[a].at[_index(me)], local_sems.at[a]) for a in range(na)]
        for cp in mine:
            cp.start()
        first = []
        for a in range(na):
            first.append(copy(a, 0, me, sibling, src=xs[a]))
            first += [copy(a, 1 + j, me, chip, src=xs[a]) for j, chip in enumerate(chips)]
        for cp in first:
            cp.start()
        passed = []
        for a in range(na):
            for j, chip in enumerate(chips):
                copy(a, 1 + j, chip, me).wait_recv()
                fwd = copy(a, 4 + j, chip, sibling)
                fwd.start()
                passed.append(fwd)
        for a in range(na):
            copy(a, 0, sibling, me).wait_recv()
            for j, chip in enumerate(chips):
                copy(a, 4 + j, _flip(chip, 1), me).wait_recv()
        for cp in first + passed:
            cp.wait_send()
        for cp in mine:
            cp.wait()

    hbm = pl.BlockSpec(memory_space=pl.ANY)
    return pl.pallas_call(
        body, name=name,
        in_specs=[hbm] * na, out_specs=[hbm] * na,
        out_shape=[jax.ShapeDtypeStruct((N_DEV,) + s.shape, s.dtype) for s in shards],
        scratch_shapes=[pltpu.SemaphoreType.DMA((na, 7)), pltpu.SemaphoreType.DMA((na, 7)),
                        pltpu.SemaphoreType.DMA((na,))],
    )(*shards)


_HBM = pl.BlockSpec(memory_space=pltpu.HBM)
_SEM = pl.BlockSpec(memory_space=pltpu.SEMAPHORE)
_EFFECT = pltpu.SideEffectType.DATAFLOW_SIDE_EFFECTING


def _exchange_copy(src_ref, land_ref, send_sems, recv_sems, a, rel, me, scatter, landed):
    peer = _flip(me, rel)
    src = src_ref.at[_index(peer)] if scatter else src_ref
    return pltpu.make_async_remote_copy(
        src_ref=src, dst_ref=land_ref.at[_index(peer if landed else me)],
        send_sem=send_sems.at[a * (N_DEV - 1) + rel - 1], recv_sem=recv_sems.at[a * (N_DEV - 1) + rel - 1],
        device_id=peer, device_id_type=MESH)


def _own_slot(data, me, scatter):
    if scatter:
        own = lax.dynamic_slice_in_dim(data, me, 1, axis=0)
        shape = data.shape
    else:
        own = data[None]
        shape = (N_DEV,) + data.shape
    start = (me,) + (0,) * (len(shape) - 1)
    return lax.dynamic_update_slice(lax.empty(shape, data.dtype), own, start)


def _exchange_start(groups, me, scatter, name, after=None):
    sizes = [len(g) for g in groups]
    srcs = [a for g in groups for a in g]
    lands = [_own_slot(a, me, scatter) for a in srcs]
    na, ng = len(srcs), len(groups)
    deps = [] if after is None else [after]

    def body(*refs):
        src_refs, land_refs = refs[:na], refs[na:2 * na]
        sems = refs[2 * na + len(deps):2 * na + len(deps) + 2 * ng]
        token = refs[-1]
        place = _my_place()
        a = 0
        for g, size in enumerate(sizes):
            for k in range(size):
                for rel in range(1, N_DEV):
                    _exchange_copy(src_refs[a], land_refs[a], sems[2 * g], sems[2 * g + 1], k, rel, place, scatter,
                                   False).start()
                a += 1
        token[...] = jnp.zeros_like(token)

    sem_shapes = [pltpu.SemaphoreType.DMA((size * (N_DEV - 1),)) for size in sizes for _ in range(2)]
    outs = pl.pallas_call(
        body, name=name,
        in_specs=[_HBM] * (2 * na) + [pl.BlockSpec(memory_space=pl.ANY)] * len(deps),
        out_specs=[_SEM] * (2 * ng) + [_HBM] * (2 * na) + [pl.BlockSpec(memory_space=pltpu.VMEM)],
        out_shape=sem_shapes + [pltpu.HBM(a.shape, a.dtype) for a in srcs + lands]
        + [jax.ShapeDtypeStruct((SUBLANES, LANES), F32)],
        input_output_aliases={i: 2 * ng + i for i in range(2 * na)},
        compiler_params=pltpu.CompilerParams(has_side_effects=_EFFECT),
    )(*[pltpu.with_memory_space_constraint(a, pltpu.HBM) for a in srcs + lands], *deps)
    sems, thru, token = outs[:2 * ng], outs[2 * ng:2 * ng + 2 * na], outs[-1]
    handles, a = [], 0
    for g, size in enumerate(sizes):
        handles.append((sems[2 * g], sems[2 * g + 1], thru[a:a + size], thru[na + a:na + a + size]))
        a += size
    return handles, token


def _exchange_wait(handle, after, scatter, name):
    send_sems, recv_sems, srcs, lands = handle
    na = len(srcs)

    def body(*refs):
        src_refs, land_refs = refs[:na], refs[na:2 * na]
        send_ref, recv_ref = refs[2 * na], refs[2 * na + 1]
        place = _my_place()
        for a in range(na):
            for rel in range(1, N_DEV):
                cp = _exchange_copy(src_refs[a], land_refs[a], send_ref, recv_ref, a, rel, place, scatter, True)
                cp.wait_send()
                cp.wait_recv()

    outs = pl.pallas_call(
        body, name=name,
        in_specs=[_HBM] * (2 * na) + [_SEM, _SEM, pl.BlockSpec(memory_space=pl.ANY)],
        out_specs=[_HBM] * (2 * na),
        out_shape=[pltpu.HBM(a.shape, a.dtype) for a in list(srcs) + list(lands)],
        input_output_aliases={i: i for i in range(2 * na)},
        compiler_params=pltpu.CompilerParams(has_side_effects=_EFFECT),
    )(*srcs, *lands, send_sems, recv_sems, after)
    return outs[na:]


def _behind(arr, token):
    return arr + token[0:1, 0:1]


def _all_reduce_small(g, name):
    _, r, c = g.shape

    def body(g_ref, o_ref, land_ref, red_ref, send1, recv1, send2, recv2):
        me = _my_place()
        idx = _index(me)

        def scatter(rel):
            peer = _flip(me, rel)
            return pltpu.make_async_remote_copy(
                src_ref=g_ref.at[_index(peer)], dst_ref=land_ref.at[idx],
                send_sem=send1.at[rel - 1], recv_sem=recv1.at[rel - 1], device_id=peer, device_id_type=MESH)

        def gather(rel):
            peer = _flip(me, rel)
            return pltpu.make_async_remote_copy(
                src_ref=red_ref, dst_ref=o_ref.at[idx],
                send_sem=send2.at[rel - 1], recv_sem=recv2.at[rel - 1], device_id=peer, device_id_type=MESH)

        for rel in range(1, N_DEV):
            scatter(rel).start()
        land_ref[idx] = g_ref[idx]
        for rel in range(1, N_DEV):
            scatter(rel).wait()
        acc = land_ref[0]
        for s in range(1, N_DEV):
            acc = acc + land_ref[s]
        red_ref[...] = acc
        for rel in range(1, N_DEV):
            gather(rel).start()
        o_ref[idx] = acc
        for rel in range(1, N_DEV):
            gather(rel).wait()

    vmem = pl.BlockSpec(memory_space=pltpu.VMEM)
    return pl.pallas_call(
        body, name=name,
        in_specs=[vmem], out_specs=vmem,
        out_shape=jax.ShapeDtypeStruct(g.shape, F32),
        scratch_shapes=[pltpu.VMEM(g.shape, F32), pltpu.VMEM((r, c), F32)]
        + [pltpu.SemaphoreType.DMA((N_DEV - 1,))] * 4,
        compiler_params=pltpu.CompilerParams(vmem_limit_bytes=VMEM_LIMIT),
    )(g)


def _ssm_discretize(a_re, a_im, log_dt, b_re, b_im):
    dt = jnp.exp(log_dt)[:, None]
    mag = jnp.exp(a_re * dt)
    lr, li = mag * jnp.cos(a_im * dt), mag * jnp.sin(a_im * dt)
    den = a_re * a_re + a_im * a_im
    qr = ((lr - 1.0) * a_re + li * a_im) / den
    qi = (li * a_re - (lr - 1.0) * a_im) / den
    bbr = qr[..., None] * b_re - qi[..., None] * b_im
    bbi = qr[..., None] * b_im + qi[..., None] * b_re
    return lr, li, bbr, bbi


def _halves(a):
    return a.reshape((2, HALF_GROUPS) + a.shape[1:])


def _block_diag_mask(g, r, c):
    rows = lax.broadcasted_iota(jnp.int32, (g * r, g * c), 0) // r
    cols = lax.broadcasted_iota(jnp.int32, (g * r, g * c), 1) // c
    return rows == cols


def _block_diag(blocks):
    g, r, c = blocks.shape
    spread = jnp.tile(jnp.eye(c, dtype=blocks.dtype), (1, g))
    full = jnp.dot(blocks.reshape(g * r, c), spread, precision=lax.Precision.HIGHEST)
    return jnp.where(_block_diag_mask(g, r, c), full, 0.0)


def _block_diag_take(dense, g, r, c):
    gather = jnp.tile(jnp.eye(c, dtype=dense.dtype), (g, 1))
    kept = jnp.where(_block_diag_mask(g, r, c), dense, 0.0)
    return jnp.dot(kept, gather, precision=lax.Precision.HIGHEST).reshape(g, r, c)


def _ssm_matrices(bbr, bbi, c_re, c_im, glu_w, glu_b, d_skip):
    bre, bim = _halves(jnp.swapaxes(bbr, 1, 2)), _halves(jnp.swapaxes(bbi, 1, 2))
    bblk = jnp.stack([jnp.concatenate([_block_diag(bre[h]), _block_diag(bim[h])], axis=1) for h in range(2)])
    cre, cim = _halves(jnp.swapaxes(c_re, 1, 2)), _halves(jnp.swapaxes(c_im, 1, 2))
    cblk = jnp.stack([jnp.concatenate([_block_diag(cre[h]), -_block_diag(cim[h])], axis=0) for h in range(2)])
    glu = jnp.concatenate([_block_diag(glu_w[:, :, :SSM_CH]), _block_diag(glu_w[:, :, SSM_CH:])], axis=1)
    glub = jnp.concatenate([glu_b[:, :SSM_CH].reshape(1, -1), glu_b[:, SSM_CH:].reshape(1, -1)], axis=1)
    return bblk.astype(BF16), cblk.astype(BF16), glu.astype(BF16), glub, d_skip.reshape(1, -1)


def _scan_constants(lr, li, reverse):
    if reverse:
        li = -li
    pows = [(lr, li)]
    for _ in range(SUBLANES - 1):
        pr, pi = pows[-1]
        pows.append((pr * lr - pi * li, pr * li + pi * lr))
    row = jnp.arange(SUBLANES)[:, None]

    def flat(a):
        return a.reshape(2, 1, HALF_ST)

    mats = []
    for s in (1, 2, 4):
        keep = (row + s <= SUBLANES - 1) if reverse else (row >= s)
        mats.append(tuple(jnp.where(keep[None], flat(p), 0.0) for p in pows[s - 1]))
    order = [SUBLANES - 1 - j for j in range(SUBLANES)] if reverse else list(range(SUBLANES))
    mats.append(tuple(jnp.concatenate([flat(pows[j][k]) for j in order], axis=1) for k in range(2)))
    return jnp.stack([jnp.concatenate([m[0], m[1]], axis=2) for m in mats], axis=1)


def _pack(arrs, rows):
    flat = jnp.concatenate([a.reshape(-1) for a in arrs])
    return jnp.pad(flat, (0, rows * LANES - flat.shape[0])).reshape(rows, LANES)


def _unpack(buf, like):
    flat = buf.reshape(-1)
    out, off = [], 0
    for a in like:
        out.append(flat[off:off + a.size].reshape(a.shape))
        off += a.size
    return out


SMALL = ("norm_ffn1", "norm_mix", "ssm_a_re", "ssm_a_im", "ssm_log_dt", "ssm_b_re", "ssm_b_im", "ssm_c_re",
         "ssm_c_im", "ssm_d", "ssm_glu_w", "ssm_glu_b", "gm_v_gain", "gm_w_s", "gm_b_s", "gain_ssm_out",
         "gain_gm_out", "norm_ffn2", "norm_final")
BIG = ("ffn1_w_in", "ffn1_w_out", "mix_w_in", "mix_w_out", "ffn2_w_in", "ffn2_w_out")
TRANSPOSED = ("ffn1_w_in", "mix_w_in", "ffn2_w_in")
WEIGHTS = ("norm_ffn1", "ffn1_w_in", "ffn1_w_out", "norm_mix", "mix_w_in", "ssm_a_re", "ssm_a_im", "ssm_log_dt",
           "ssm_b_re", "ssm_b_im", "ssm_c_re", "ssm_c_im", "ssm_d", "ssm_glu_w", "ssm_glu_b", "gm_v_gain", "gm_w_s",
           "gm_b_s", "gain_ssm_out", "gain_gm_out", "mix_w_out", "norm_ffn2", "ffn2_w_in", "ffn2_w_out", "norm_final")


def _step(x, target, w, m, v):
    batch, seq, _ = x.shape
    n = batch * seq
    depth = w["norm_ffn1"].shape[0]
    tm = min(512, n)
    tm_ffn = min(1024, n)
    tk = min(4096, n)
    t_chunk = min(256, seq)
    gm_rows = min(512, seq)
    x = x.reshape(n, D_MODEL)
    target = target.reshape(n, D_MODEL)

    assert depth == 2
    me = _index(_my_place())
    shard = lambda group, l: [w[f"{group}_w_in"][l].astype(BF16), w[f"{group}_w_out"][l].astype(BF16)]
    batches = ([("mix", 0), ("ffn2", 0)], [("ffn1", 1), ("mix", 1)], [("ffn2", 1)])
    gathered, pending = {("ffn1", 0): tuple(_all_gather(shard("ffn1", 0), "all_gather_first"))}, {}

    def gather_start(i, after):
        handles, tok = _exchange_start([shard(g, l) for g, l in batches[i]], me, False, f"all_gather_start_{i}",
                                       after)
        pending.update(zip(batches[i], handles))
        return tok

    def weights(group, l, after=None):
        if (group, l) not in gathered:
            w_in, w_out = _exchange_wait(pending[(group, l)], after, False, f"all_gather_wait_{group}_{l}")
            if group == "mix":
                w_in = jnp.transpose(w_in, (1, 0, 2)).reshape(D_MODEL, IN_COLS)
                w_out = w_out.reshape(D_MODEL, D_MODEL)
            gathered[(group, l)] = (w_in, w_out)
        return gathered[(group, l)]

    tril = jnp.tril(jnp.ones((GM_CHUNK, GM_CHUNK), bool))
    layers = []
    for l in range(depth):
        disc, disc_vjp = jax.vjp(_ssm_discretize, w["ssm_a_re"][l], w["ssm_a_im"][l], w["ssm_log_dt"][l],
                                 w["ssm_b_re"][l], w["ssm_b_im"][l])
        lr, li, bbr, bbi = disc
        bblk, cblk, glu, glub, dskip = _ssm_matrices(bbr, bbi, w["ssm_c_re"][l], w["ssm_c_im"][l],
                                                     w["ssm_glu_w"][l], w["ssm_glu_b"][l], w["ssm_d"][l])
        layers.append(dict(
            disc_vjp=disc_vjp, lr=lr, li=li, bblk=bblk, cblk=cblk, glu=glu, glub=glub, dskip=dskip,
            fwdc=_scan_constants(lr, li, False), revc=_scan_constants(lr, li, True),
            w_tril=jnp.where(tril[None], w["gm_w_s"][l], 0.0).astype(BF16),
            gm_bias=jnp.repeat(w["gm_b_s"][l].T, GM_HEAD_DIM, axis=1),
            g1=w["norm_ffn1"][l][None], gmix=w["norm_mix"][l][None], g2=w["norm_ffn2"][l][None],
            gv=w["gm_v_gain"][l][None], gs=w["gain_ssm_out"][l][None], gg=w["gain_gm_out"][l][None],
        ))

    saved = []
    for l in range(depth):
        p = layers[l]
        x0 = x
        g1, gmix, g2 = p["g1"], p["gmix"], p["g2"]
        w_in, w_out = weights("ffn1", l, x0)
        if l == 0:
            g1 = _behind(g1, gather_start(0, w_in))
        x1, xn1, gu1 = _ffn_fwd(x0, g1, w_in, w_out, tm_ffn, f"ffn1_fwd_{l}")
        if l == 0:
            gmix = _behind(gmix, gather_start(1, x1))
        mwi, mwo = weights("mix", l, x1)
        z = _mix_in_fwd(x1, gmix, mwi, tm_ffn, f"mix_in_fwd_{l}")
        y_ssm, h = _ssm_fwd(z, p["bblk"], p["cblk"], p["glu"], p["glub"], p["dskip"], p["fwdc"], batch, t_chunk,
                            f"ssm_fwd_{l}")
        y_gm = _gm_fwd(z, p["gv"], p["w_tril"], p["gm_bias"], gm_rows, f"gm_fwd_{l}")
        x2 = _mix_out_fwd(y_ssm, y_gm, p["gs"], p["gg"], mwo, x1, tm_ffn, f"mix_out_fwd_{l}")
        if l == 0:
            g2 = _behind(g2, gather_start(2, x2))
        x, xn2, gu2 = _ffn_fwd(x2, g2, *weights("ffn2", l, x2), tm_ffn, f"ffn2_fwd_{l}")
        saved.append((x0, x1, x2, z, h, y_ssm, y_gm, xn1, gu1, xn2, gu2))

    dx, sq, dnf = _loss_head(x, w["norm_final"][None], target, tm_ffn, "loss_head")
    loss = lax.psum((0.5 / D_MODEL) * jnp.sum(sq), AXES)

    small = {k: [None] * depth for k in SMALL if k != "norm_final"}
    sent = []

    def send(group, l, keys, parts):
        (handle,), tok = _exchange_start([parts], me, True, f"reduce_scatter_start_{group}_{l}")
        sent.append((group, l, keys, handle))
        return tok

    token = None
    for l in reversed(range(depth)):
        p = layers[l]
        x0, x1, x2, z, h, y_ssm, y_gm, xn1, gu1, xn2, gu2 = saved[l]
        mwi, mwo = weights("mix", l)
        dx_out = dx
        g2 = p["g2"] if token is None else _behind(p["g2"], token)
        dx, dgu, act, dgain, dyb = _ffn_bwd(x2, g2, dx_out, gu2, *weights("ffn2", l), tm, f"ffn2_bwd_{l}")
        dw_in = _ffn_dw_in(xn2, dgu, tk, f"ffn2_dw_in_{l}")
        dw_out = _ffn_dw_out(act, dyb, tk, f"ffn2_dw_out_{l}").reshape(N_DEV, FF_SHARD // 2, D_MODEL)
        token = send("ffn2", l, ("ffn2_w_in", "ffn2_w_out"), [dw_in, dw_out])
        small["norm_ffn2"][l] = dgain.sum(0)

        dy_ssm, dy_gm, dwo, dgs, dgg = _mix_out_bwd(y_ssm, y_gm, _behind(p["gs"], token), p["gg"], mwo, dx, tm,
                                                    f"mix_out_bwd_{l}")
        dwo = dwo.astype(BF16).reshape(N_DEV, D_MODEL // N_DEV, D_MODEL)
        small["gain_ssm_out"][l] = dgs.sum(0)
        small["gain_gm_out"][l] = dgg.sum(0)

        du_ssm, dglu, dglub, ddskip, dct, db, q = _ssm_bwd(
            z, h, dy_ssm, p["bblk"], p["cblk"], p["glu"], p["glub"], p["dskip"], p["revc"], batch, t_chunk,
            f"ssm_bwd_{l}")
        du_gm, dv_gm, dws, dbias, dgv = _gm_bwd(z, dy_gm, p["gv"], p["w_tril"], p["gm_bias"], gm_rows, f"gm_bwd_{l}")

        q = q.sum(0).reshape(2, 2, HALF_GROUPS, SSM_STATE)
        qr, qi = q[:, 0].reshape(SSM_GROUPS, SSM_STATE), q[:, 1].reshape(SSM_GROUPS, SSM_STATE)
        den = p["lr"] * p["lr"] + p["li"] * p["li"]
        d_re = (qr * p["lr"] + qi * p["li"]) / den
        d_im = (qi * p["lr"] - qr * p["li"]) / den
        dbb = jnp.stack([_block_diag_take(db[hf, :, k * HALF_ST:(k + 1) * HALF_ST], HALF_GROUPS, SSM_CH, SSM_STATE)
                         for k in range(2) for hf in range(2)]).reshape(2, SSM_GROUPS, SSM_CH, SSM_STATE)
        dcc = jnp.stack([_block_diag_take(dct[hf, :, k * HALF_ST:(k + 1) * HALF_ST], HALF_GROUPS, SSM_CH, SSM_STATE)
                         for k in range(2) for hf in range(2)]).reshape(2, SSM_GROUPS, SSM_CH, SSM_STATE)
        da_re, da_im, dlog_dt, db_re, db_im = p["disc_vjp"](
            (d_re, -d_im, jnp.swapaxes(dbb[0], 1, 2), jnp.swapaxes(dbb[1], 1, 2)))
        small["ssm_a_re"][l], small["ssm_a_im"][l], small["ssm_log_dt"][l] = da_re, da_im, dlog_dt
        small["ssm_b_re"][l], small["ssm_b_im"][l] = db_re, db_im
        small["ssm_c_re"][l], small["ssm_c_im"][l] = dcc[0], -dcc[1]
        small["ssm_d"][l] = ddskip.sum(0).reshape(SSM_GROUPS, SSM_CH)
        small["ssm_glu_w"][l] = jnp.concatenate(
            [_block_diag_take(dglu[:, :SSM_WIDTH], SSM_GROUPS, SSM_CH, SSM_CH),
             _block_diag_take(dglu[:, SSM_WIDTH:], SSM_GROUPS, SSM_CH, SSM_CH)], axis=2)
        dglub = dglub.sum(0)
        small["ssm_glu_b"][l] = jnp.concatenate(
            [dglub[:SSM_WIDTH].reshape(SSM_GROUPS, SSM_CH), dglub[SSM_WIDTH:].reshape(SSM_GROUPS, SSM_CH)], axis=1)
        small["gm_v_gain"][l] = dgv.sum(0)
        small["gm_w_s"][l] = jnp.where(tril[None], dws, 0.0)
        small["gm_b_s"][l] = dbias.reshape(GM_CHUNK, GM_HEADS, GM_HEAD_DIM).sum(-1).T

        dx, dwi, dgain = _mix_in_bwd(x1, p["gmix"], du_ssm, du_gm, dv_gm, dx, mwi, tm, f"mix_in_bwd_{l}")
        dwi = dwi.astype(BF16).reshape(N_DEV, IN_COLS // N_DEV, D_MODEL)
        token = send("mix", l, ("mix_w_in", "mix_w_out"), [dwi, dwo])
        small["norm_mix"][l] = dgain.sum(0)

        dx_out = dx
        dx, dgu, act, dgain, dyb = _ffn_bwd(x0, _behind(p["g1"], token), dx_out, gu1, *weights("ffn1", l), tm,
                                       f"ffn1_bwd_{l}")
        small["norm_ffn1"][l] = dgain.sum(0)
        if l > 0:
            dw_in = _ffn_dw_in(xn1, dgu, tk, f"ffn1_dw_in_{l}")
            dw_out = _ffn_dw_out(act, dyb, tk, f"ffn1_dw_out_{l}").reshape(N_DEV, FF_SHARD // 2, D_MODEL)
            token = send("ffn1", l, ("ffn1_w_in", "ffn1_w_out"), [dw_in, dw_out])
            continue
        small_g = [jnp.stack(small[k]) if k != "norm_final" else dnf.sum(0) for k in SMALL]
        total = sum(int(math.prod(w[k].shape)) for k in SMALL)
        rows = -(-total // (LANES * N_DEV * SUBLANES)) * N_DEV * SUBLANES
        g_all = _all_reduce_small(_pack(small_g, rows).reshape(N_DEV, rows // N_DEV, LANES), "all_reduce_small")
        dw_in = _ffn_dw_in(xn1, dgu, tk, f"ffn1_dw_in_{l}", after=g_all)
        token = send("ffn1_in", l, ("ffn1_w_in",), [dw_in])
        dw_out = _ffn_dw_out(act, dyb, tk, f"ffn1_dw_out_{l}", after=token).reshape(
            N_DEV, FF_SHARD // 2, D_MODEL)
        token = send("ffn1_out", l, ("ffn1_w_out",), [dw_out])

    grad_x = dx.reshape(batch, seq, D_MODEL)
    grads, deltas, new_m, new_v = {}, {}, {}, {}

    g_all = _behind(g_all.reshape(rows, LANES), token)
    like = [w[k] for k in SMALL]
    d_p, m_p, v_p = _adam_packed(g_all, _pack(like, rows), _pack([m[k] for k in SMALL], rows),
                                 _pack([v[k] for k in SMALL], rows), "adam_small")
    for k, g_, d_, m_, v_ in zip(SMALL, _unpack(g_all, like), _unpack(d_p, like), _unpack(m_p, like),
                                 _unpack(v_p, like)):
        grads[k], deltas[k], new_m[k], new_v[k] = g_, d_, m_, v_

    results = {}
    after = d_p
    for group, l, keys, handle in sent:
        landed = _exchange_wait(handle, after, True, f"reduce_scatter_wait_{group}_{l}")
        for k, parts in zip(keys, landed):
            view = (lambda a: jnp.swapaxes(a, 1, 2)) if k in TRANSPOSED else (lambda a: a)
            results[k] = _adam_sharded(parts, view(w[k]), view(m[k]), view(v[k]), l, results.get(k),
                                       f"adam_{k}_{l}")
            after = results[k][0]
    for k in BIG:
        view = (lambda a: jnp.swapaxes(a, 1, 2)) if k in TRANSPOSED else (lambda a: a)
        grads[k], deltas[k], new_m[k], new_v[k] = [view(a) for a in results[k]]
    return loss, grad_x, grads, deltas, new_m, new_v


def kernel(x, norm_ffn1, ffn1_w_in, ffn1_w_out, norm_mix, mix_w_in, ssm_a_re, ssm_a_im, ssm_log_dt, ssm_b_re, ssm_b_im, ssm_c_re, ssm_c_im, ssm_d, ssm_glu_w, ssm_glu_b, gm_v_gain, gm_w_s, gm_b_s, gain_ssm_out, gain_gm_out, mix_w_out, norm_ffn2, ffn2_w_in, ffn2_w_out, norm_final, loss_target, m_norm_ffn1, m_ffn1_w_in, m_ffn1_w_out, m_norm_mix, m_mix_w_in, m_ssm_a_re, m_ssm_a_im, m_ssm_log_dt, m_ssm_b_re, m_ssm_b_im, m_ssm_c_re, m_ssm_c_im, m_ssm_d, m_ssm_glu_w, m_ssm_glu_b, m_gm_v_gain, m_gm_w_s, m_gm_b_s, m_gain_ssm_out, m_gain_gm_out, m_mix_w_out, m_norm_ffn2, m_ffn2_w_in, m_ffn2_w_out, m_norm_final, v_norm_ffn1, v_ffn1_w_in, v_ffn1_w_out, v_norm_mix, v_mix_w_in, v_ssm_a_re, v_ssm_a_im, v_ssm_log_dt, v_ssm_b_re, v_ssm_b_im, v_ssm_c_re, v_ssm_c_im, v_ssm_d, v_ssm_glu_w, v_ssm_glu_b, v_gm_v_gain, v_gm_w_s, v_gm_b_s, v_gain_ssm_out, v_gain_gm_out, v_mix_w_out, v_norm_ffn2, v_ffn2_w_in, v_ffn2_w_out, v_norm_final):
    args = locals()
    w = {k: args[k] for k in WEIGHTS}
    m = {k: args["m_" + k] for k in WEIGHTS}
    v = {k: args["v_" + k] for k in WEIGHTS}
    loss, grad_x, grads, deltas, new_m, new_v = _step(x, loss_target, w, m, v)
    return (loss, grad_x, *[grads[k] for k in WEIGHTS], *[deltas[k] for k in WEIGHTS],
            *[new_m[k] for k in WEIGHTS], *[new_v[k] for k in WEIGHTS])
```

```python
import functools
import math

import jax
import jax.numpy as jnp
from jax import lax
from jax.experimental import pallas as pl
from jax.experimental.pallas import tpu as pltpu

F32 = jnp.float32
BF16 = jnp.bfloat16
MESH = pl.DeviceIdType.MESH
AXES = ("x", "y", "c")

N_DEV = 8
D_MODEL = 1024
D_FF = 2816
FF_SHARD = 2 * D_FF // N_DEV
FF_CHUNKS = 4
MXU_DIM = 256
FF_PIECES = tuple((lo, min(lo + MXU_DIM, FF_SHARD)) for lo in range(0, FF_SHARD, MXU_DIM))
SSM_WIDTH = 512
SSM_CH = 16
SSM_GROUPS = 32
SSM_STATE = 64
HALF_GROUPS = 16
HALF_IN = HALF_GROUPS * SSM_CH
HALF_ST = HALF_GROUPS * SSM_STATE
GM_WIDTH = 512
GM_HEADS = 4
GM_HEAD_DIM = 128
GM_CHUNK = 128
IN_COLS = SSM_WIDTH + 2 * GM_WIDTH
EPS = 1e-6
SUBLANES = 8
LANES = 128

ADAM_LR = 0.001
ADAM_B1 = 0.9
ADAM_B2 = 0.999
ADAM_EPS = 1e-08
ADAM_WD = 0.01
ADAM_STEP = 10

VMEM_LIMIT = 46 * 1024 * 1024


def _cp(*sem):
    return pltpu.CompilerParams(dimension_semantics=sem, vmem_limit_bytes=VMEM_LIMIT)


def _rms_fwd(x, g):
    r = lax.rsqrt(jnp.mean(x * x, axis=-1, keepdims=True) + EPS)
    xh = x * r
    return xh * g, xh, r


def _rms_bwd(dy, xh, r, g):
    dxh = dy * g
    dx = r * (dxh - xh * jnp.mean(dxh * xh, axis=-1, keepdims=True))
    return dx, dy * xh


def _rows8(a):
    m, n = a.shape
    return a.reshape(m // SUBLANES, SUBLANES, n).sum(axis=0)


_GELU_K = math.sqrt(2.0 / math.pi)
_GELU_C = 0.044715


def _gelu(x):
    th = jnp.tanh(_GELU_K * (x + _GELU_C * x * x * x))
    return 0.5 * x * (1.0 + th), th


def _gelu_grad(x, th):
    return 0.5 * (1.0 + th) + 0.5 * x * (1.0 - th * th) * (_GELU_K * (1.0 + 3.0 * _GELU_C * x * x))


def _dot(a, b):
    return jnp.dot(a, b, preferred_element_type=F32)


def _dot_nt(a, b):
    return lax.dot_general(a, b, (((1,), (1,)), ((), ())), preferred_element_type=F32)


def _dot_tn(a, b):
    return lax.dot_general(a, b, (((0,), (0,)), ((), ())), preferred_element_type=F32)


def _ffn_fwd(x, gain, w_in_ag, w_out_ag, tm, name):
    n = x.shape[0]

    def body(x_ref, g_ref, wg_ref, wu_ref, wo_ref, o_ref, xn_ref, gu_ref):
        j = pl.program_id(1)

        @pl.when(j == 0)
        def _():
            xv = x_ref[...]
            y, _, _ = _rms_fwd(xv, g_ref[...])
            xn_ref[...] = y.astype(BF16)
            o_ref[...] = xv

        xn = xn_ref[...]
        wo = wo_ref[...].reshape(FF_SHARD, D_MODEL)
        out = None
        for lo, hi in FF_PIECES:
            gg = _dot(xn, wg_ref[:, lo:hi])
            uu = _dot(xn, wu_ref[:, lo:hi])
            gu_ref[0, :, lo:hi] = gg.astype(BF16)
            gu_ref[1, :, lo:hi] = uu.astype(BF16)
            act = (gg * jax.nn.sigmoid(gg) * uu).astype(BF16)
            part = _dot(act, wo[lo:hi, :])
            out = part if out is None else out + part
        o_ref[...] += 0.5 * out

    return pl.pallas_call(
        body, name=name, grid=(n // tm, FF_CHUNKS),
        in_specs=[
            pl.BlockSpec((tm, D_MODEL), lambda i, j: (i, 0)),
            pl.BlockSpec((1, D_MODEL), lambda i, j: (0, 0)),
            pl.BlockSpec((None, D_MODEL, FF_SHARD), lambda i, j: (j, 0, 0)),
            pl.BlockSpec((None, D_MODEL, FF_SHARD), lambda i, j: (j + FF_CHUNKS, 0, 0)),
            pl.BlockSpec((2, FF_SHARD // 2, D_MODEL), lambda i, j: (j, 0, 0)),
        ],
        out_specs=[
            pl.BlockSpec((tm, D_MODEL), lambda i, j: (i, 0)),
            pl.BlockSpec((tm, D_MODEL), lambda i, j: (i, 0)),
            pl.BlockSpec((None, 2, tm, FF_SHARD), lambda i, j: (j, 0, i, 0)),
        ],
        out_shape=[
            jax.ShapeDtypeStruct((n, D_MODEL), F32),
            jax.ShapeDtypeStruct((n, D_MODEL), BF16),
            jax.ShapeDtypeStruct((FF_CHUNKS, 2, n, FF_SHARD), BF16),
        ],
        compiler_params=_cp("parallel", "arbitrary"),
    )(x, gain, w_in_ag, w_in_ag, w_out_ag)


def _ffn_bwd(x, gain, dy, gu, w_in_ag, w_out_ag, tm, name):
    n = x.shape[0]
    work = (n // tm) * FF_CHUNKS
    steps = work + 1
    first = lambda s: jnp.minimum(s, work - 1)
    second = lambda s: jnp.maximum(s - 1, 0)
    tile1, chunk1 = (lambda s: first(s) // FF_CHUNKS), (lambda s: first(s) % FF_CHUNKS)
    tile2, chunk2 = (lambda s: second(s) // FF_CHUNKS), (lambda s: second(s) % FF_CHUNKS)

    def body(x_ref, g_ref, dy_ref, dy1_ref, gu_ref, wg_ref, wu_ref, wo_ref, dx_ref, dgu_ref, act_ref, dgain_ref,
             dyb_ref, held_ref):
        s = pl.program_id(0)

        @pl.when(s == 0)
        def _():
            dgain_ref[...] = jnp.zeros_like(dgain_ref)
            held_ref[...] = jnp.zeros_like(held_ref)

        @pl.when(jnp.logical_and(chunk1(s) == 0, s < work))
        def _():
            dyb_ref[...] = (0.5 * dy1_ref[...]).astype(BF16)

        @pl.when(chunk2(s) == 0)
        def _():
            dx_ref[...] = jnp.zeros_like(dx_ref)

        held = held_ref[1 - s % 2]
        part = _dot_nt(held[0], wg_ref[...]) + _dot_nt(held[1], wu_ref[...])
        dx_ref[...] += jnp.where(s > 0, part, 0.0)

        dyb = dyb_ref[...]
        wo = wo_ref[...].reshape(FF_SHARD, D_MODEL)
        slot = s % 2
        for lo, hi in FF_PIECES:
            gg = gu_ref[0, :, lo:hi].astype(F32)
            uu = gu_ref[1, :, lo:hi].astype(F32)
            dact = _dot_nt(dyb, wo[lo:hi, :])
            sig = jax.nn.sigmoid(gg)
            silu = gg * sig
            act_ref[:, lo:hi] = (silu * uu).astype(BF16)
            du = (dact * silu).astype(BF16)
            dg = (dact * uu * (sig * (1.0 + gg * (1.0 - sig)))).astype(BF16)
            dgu_ref[0, :, lo:hi] = dg
            dgu_ref[1, :, lo:hi] = du
            held_ref[slot, 0, :, lo:hi] = dg
            held_ref[slot, 1, :, lo:hi] = du

        @pl.when(jnp.logical_and(chunk2(s) == FF_CHUNKS - 1, s > 0))
        def _():
            g = g_ref[...]
            _, xh, r = _rms_fwd(x_ref[...], g)
            dx, dgr = _rms_bwd(dx_ref[...], xh, r, g)
            dx_ref[...] = dy_ref[...] + dx
            dgain_ref[...] += _rows8(dgr)

    return pl.pallas_call(
        body, name=name, grid=(steps,),
        in_specs=[
            pl.BlockSpec((tm, D_MODEL), lambda s: (tile2(s), 0)),
            pl.BlockSpec((1, D_MODEL), lambda s: (0, 0)),
            pl.BlockSpec((tm, D_MODEL), lambda s: (tile2(s), 0)),
            pl.BlockSpec((tm, D_MODEL), lambda s: (tile1(s), 0)),
            pl.BlockSpec((None, 2, tm, FF_SHARD), lambda s: (chunk1(s), 0, tile1(s), 0)),
            pl.BlockSpec((None, D_MODEL, FF_SHARD), lambda s: (chunk2(s), 0, 0)),
            pl.BlockSpec((None, D_MODEL, FF_SHARD), lambda s: (chunk2(s) + FF_CHUNKS, 0, 0)),
            pl.BlockSpec((2, FF_SHARD // 2, D_MODEL), lambda s: (chunk1(s), 0, 0)),
        ],
        out_specs=[
            pl.BlockSpec((tm, D_MODEL), lambda s: (tile2(s), 0)),
            pl.BlockSpec((None, 2, tm, FF_SHARD), lambda s: (chunk1(s), 0, tile1(s), 0)),
            pl.BlockSpec((None, tm, FF_SHARD), lambda s: (chunk1(s), tile1(s), 0)),
            pl.BlockSpec((SUBLANES, D_MODEL), lambda s: (0, 0)),
            pl.BlockSpec((tm, D_MODEL), lambda s: (tile1(s), 0)),
        ],
        out_shape=[
            jax.ShapeDtypeStruct((n, D_MODEL), F32),
            jax.ShapeDtypeStruct((FF_CHUNKS, 2, n, FF_SHARD), BF16),
            jax.ShapeDtypeStruct((FF_CHUNKS, n, FF_SHARD), BF16),
            jax.ShapeDtypeStruct((SUBLANES, D_MODEL), F32),
            jax.ShapeDtypeStruct((n, D_MODEL), BF16),
        ],
        scratch_shapes=[pltpu.VMEM((2, 2, tm, FF_SHARD), BF16)],
        compiler_params=_cp("arbitrary"),
    )(x, gain, dy, dy, gu, w_in_ag, w_in_ag, w_out_ag)


def _ffn_dw_in(xn, dgu, tk, name, after=None):
    n = xn.shape[0]
    nk = n // tk
    deps = [] if after is None else [after]

    def body(a_ref, b_ref, *rest):
        o_ref, acc_ref = rest[-2:]
        k = pl.program_id(2)

        @pl.when(k == 0)
        def _():
            acc_ref[...] = jnp.zeros_like(acc_ref)

        acc_ref[...] += _dot_tn(b_ref[...], a_ref[...])

        @pl.when(k == nk - 1)
        def _():
            o_ref[...] = acc_ref[...].astype(BF16)

    return pl.pallas_call(
        body, name=name, grid=(FF_CHUNKS, 2, nk),
        in_specs=[
            pl.BlockSpec((tk, D_MODEL), lambda j, p, k: (k, 0)),
            pl.BlockSpec((None, None, tk, FF_SHARD), lambda j, p, k: (j, p, k, 0)),
        ] + [pl.BlockSpec(memory_space=pl.ANY)] * len(deps),
        out_specs=pl.BlockSpec((None, FF_SHARD, D_MODEL), lambda j, p, k: (FF_CHUNKS * p + j, 0, 0)),
        out_shape=jax.ShapeDtypeStruct((N_DEV, FF_SHARD, D_MODEL), BF16),
        scratch_shapes=[pltpu.VMEM((FF_SHARD, D_MODEL), F32)],
        compiler_params=_cp("parallel", "parallel", "arbitrary"),
    )(xn, dgu, *deps)


def _ffn_dw_out(act, dyb, tk, name, after=None):
    n = act.shape[1]
    nk = n // tk
    deps = [] if after is None else [after]

    def body(a_ref, b_ref, *rest):
        o_ref, acc_ref = rest[-2:]
        k = pl.program_id(1)

        @pl.when(k == 0)
        def _():
            acc_ref[...] = jnp.zeros_like(acc_ref)

        acc_ref[...] += _dot_tn(a_ref[...], b_ref[...])

        @pl.when(k == nk - 1)
        def _():
            o_ref[...] = acc_ref[...].astype(BF16)

    return pl.pallas_call(
        body, name=name, grid=(FF_CHUNKS, nk),
        in_specs=[
            pl.BlockSpec((None, tk, FF_SHARD), lambda j, k: (j, k, 0)),
            pl.BlockSpec((tk, D_MODEL), lambda j, k: (k, 0)),
        ] + [pl.BlockSpec(memory_space=pl.ANY)] * len(deps),
        out_specs=pl.BlockSpec((None, FF_SHARD, D_MODEL), lambda j, k: (j, 0, 0)),
        out_shape=jax.ShapeDtypeStruct((FF_CHUNKS, FF_SHARD, D_MODEL), BF16),
        scratch_shapes=[pltpu.VMEM((FF_SHARD, D_MODEL), F32)],
        compiler_params=_cp("parallel", "arbitrary"),
    )(act, dyb, *deps)


def _mix_in_fwd(x, gain, w, tm, name):
    n = x.shape[0]

    def body(x_ref, g_ref, w_ref, z_ref):
        y, _, _ = _rms_fwd(x_ref[...], g_ref[...])
        z_ref[...] = _dot(y.astype(BF16), w_ref[...])

    return pl.pallas_call(
        body, name=name, grid=(n // tm,),
        in_specs=[
            pl.BlockSpec((tm, D_MODEL), lambda i: (i, 0)),
            pl.BlockSpec((1, D_MODEL), lambda i: (0, 0)),
            pl.BlockSpec((D_MODEL, IN_COLS), lambda i: (0, 0)),
        ],
        out_specs=pl.BlockSpec((tm, IN_COLS), lambda i: (i, 0)),
        out_shape=jax.ShapeDtypeStruct((n, IN_COLS), F32),
        compiler_params=_cp("parallel"),
    )(x, gain, w)


def _mix_in_bwd(x, gain, du_ssm, du_gm, dv_gm, d_res, w, tm, name):
    n = x.shape[0]

    def body(x_ref, g_ref, d0_ref, d1_ref, d2_ref, dres_ref, w_ref, dx_ref, dw_ref, dgain_ref):
        i = pl.program_id(0)

        @pl.when(i == 0)
        def _():
            dw_ref[...] = jnp.zeros_like(dw_ref)
            dgain_ref[...] = jnp.zeros_like(dgain_ref)

        g = g_ref[...]
        y, xh, r = _rms_fwd(x_ref[...], g)
        xn = y.astype(BF16)
        dxn = jnp.zeros((tm, D_MODEL), F32)
        for k, d_ref in enumerate((d0_ref, d1_ref, d2_ref)):
            dz = d_ref[...].astype(BF16)
            cols = slice(k * SSM_WIDTH, (k + 1) * SSM_WIDTH)
            dxn += _dot_nt(dz, w_ref[:, cols])
            dw_ref[cols, :] += _dot_tn(dz, xn)
        dx, dgr = _rms_bwd(dxn, xh, r, g)
        dx_ref[...] = dres_ref[...] + dx
        dgain_ref[...] += _rows8(dgr)

    row = lambda i: (i, 0)
    fixed = lambda i: (0, 0)
    return pl.pallas_call(
        body, name=name, grid=(n // tm,),
        in_specs=[
            pl.BlockSpec((tm, D_MODEL), row),
            pl.BlockSpec((1, D_MODEL), fixed),
            pl.BlockSpec((tm, SSM_WIDTH), row),
            pl.BlockSpec((tm, GM_WIDTH), row),
            pl.BlockSpec((tm, GM_WIDTH), row),
            pl.BlockSpec((tm, D_MODEL), row),
            pl.BlockSpec((D_MODEL, IN_COLS), fixed),
        ],
        out_specs=[
            pl.BlockSpec((tm, D_MODEL), row),
            pl.BlockSpec((IN_COLS, D_MODEL), fixed),
            pl.BlockSpec((SUBLANES, D_MODEL), fixed),
        ],
        out_shape=[
            jax.ShapeDtypeStruct((n, D_MODEL), F32),
            jax.ShapeDtypeStruct((IN_COLS, D_MODEL), F32),
            jax.ShapeDtypeStruct((SUBLANES, D_MODEL), F32),
        ],
        compiler_params=_cp("arbitrary"),
    )(x, gain, du_ssm, du_gm, dv_gm, d_res, w)


def _mix_out_fwd(y_ssm, y_gm, g_ssm, g_gm, w, x, tm, name):
    n = x.shape[0]

    def body(ys_ref, yg_ref, gs_ref, gg_ref, w_ref, x_ref, o_ref):
        a, _, _ = _rms_fwd(ys_ref[...], gs_ref[...])
        b, _, _ = _rms_fwd(yg_ref[...], gg_ref[...])
        o_ref[...] = (x_ref[...] + _dot(a.astype(BF16), w_ref[0:SSM_WIDTH, :])
                      + _dot(b.astype(BF16), w_ref[SSM_WIDTH:D_MODEL, :]))

    row = lambda i: (i, 0)
    fixed = lambda i: (0, 0)
    return pl.pallas_call(
        body, name=name, grid=(n // tm,),
        in_specs=[
            pl.BlockSpec((tm, SSM_WIDTH), row), pl.BlockSpec((tm, GM_WIDTH), row),
            pl.BlockSpec((1, SSM_WIDTH), fixed), pl.BlockSpec((1, GM_WIDTH), fixed),
            pl.BlockSpec((D_MODEL, D_MODEL), fixed), pl.BlockSpec((tm, D_MODEL), row),
        ],
        out_specs=pl.BlockSpec((tm, D_MODEL), row),
        out_shape=jax.ShapeDtypeStruct((n, D_MODEL), F32),
        compiler_params=_cp("parallel"),
    )(y_ssm, y_gm, g_ssm, g_gm, w, x)


def _mix_out_bwd(y_ssm, y_gm, g_ssm, g_gm, w, dx, tm, name):
    n = dx.shape[0]

    def body(ys_ref, yg_ref, gs_ref, gg_ref, w_ref, dx_ref, dys_ref, dyg_ref, dw_ref, dgs_ref, dgg_ref):
        i = pl.program_id(0)

        @pl.when(i == 0)
        def _():
            dw_ref[...] = jnp.zeros_like(dw_ref)
            dgs_ref[...] = jnp.zeros_like(dgs_ref)
            dgg_ref[...] = jnp.zeros_like(dgg_ref)

        dxb = dx_ref[...].astype(BF16)
        parts = ((ys_ref, gs_ref, dys_ref, dgs_ref, 0), (yg_ref, gg_ref, dyg_ref, dgg_ref, SSM_WIDTH))
        for y_ref, g_ref, dy_ref, dg_ref, off in parts:
            g = g_ref[...]
            yn, xh, r = _rms_fwd(y_ref[...], g)
            rows = slice(off, off + SSM_WIDTH)
            dyn = _dot_nt(dxb, w_ref[rows, :])
            dw_ref[rows, :] += _dot_tn(yn.astype(BF16), dxb)
            dy, dgr = _rms_bwd(dyn, xh, r, g)
            dy_ref[...] = dy
            dg_ref[...] += _rows8(dgr)

    row = lambda i: (i, 0)
    fixed = lambda i: (0, 0)
    return pl.pallas_call(
        body, name=name, grid=(n // tm,),
        in_specs=[
            pl.BlockSpec((tm, SSM_WIDTH), row), pl.BlockSpec((tm, GM_WIDTH), row),
            pl.BlockSpec((1, SSM_WIDTH), fixed), pl.BlockSpec((1, GM_WIDTH), fixed),
            pl.BlockSpec((D_MODEL, D_MODEL), fixed), pl.BlockSpec((tm, D_MODEL), row),
        ],
        out_specs=[
            pl.BlockSpec((tm, SSM_WIDTH), row), pl.BlockSpec((tm, GM_WIDTH), row),
            pl.BlockSpec((D_MODEL, D_MODEL), fixed),
            pl.BlockSpec((SUBLANES, SSM_WIDTH), fixed), pl.BlockSpec((SUBLANES, GM_WIDTH), fixed),
        ],
        out_shape=[
            jax.ShapeDtypeStruct((n, SSM_WIDTH), F32), jax.ShapeDtypeStruct((n, GM_WIDTH), F32),
            jax.ShapeDtypeStruct((D_MODEL, D_MODEL), F32),
            jax.ShapeDtypeStruct((SUBLANES, SSM_WIDTH), F32), jax.ShapeDtypeStruct((SUBLANES, GM_WIDTH), F32),
        ],
        compiler_params=_cp("arbitrary"),
    )(y_ssm, y_gm, g_ssm, g_gm, w, dx)


SCAN_W = 512
SCAN_PIECES = HALF_ST // SCAN_W


def _scan_tiles(src_ref, dst_ref, dst_off, c_ref, half, carry_ref, n_tiles, reverse, extra=None):
    shifts = (1, 2, 4)
    carry_row = 0 if reverse else SUBLANES - 1

    def cols(piece, im):
        lo = im * HALF_ST + piece * SCAN_W
        return slice(lo, lo + SCAN_W)

    def step(t, state):
        carries, accs = state
        k = (n_tiles - 1 - t) if reverse else t
        rows = slice(k * SUBLANES, (k + 1) * SUBLANES)
        new_carries, new_accs = [], []
        for piece in range(SCAN_PIECES):
            cr, ci = carries[piece]
            xr0 = src_ref[rows, cols(piece, 0)]
            xi0 = src_ref[rows, cols(piece, 1)]
            xr, xi = xr0, xi0
            for si, s in enumerate(shifts):
                ar = c_ref[half, si, :, cols(piece, 0)]
                ai = c_ref[half, si, :, cols(piece, 1)]
                sh = (SUBLANES - s) if reverse else s
                sr = pltpu.roll(xr, sh, 0)
                sm = pltpu.roll(xi, sh, 0)
                xr, xi = xr + (ar * sr - ai * sm), xi + (ar * sm + ai * sr)
            pr = c_ref[half, 3, :, cols(piece, 0)]
            pi = c_ref[half, 3, :, cols(piece, 1)]
            hr = xr + (pr * cr - pi * ci)
            hi = xi + (pr * ci + pi * cr)
            dst_ref[rows, pl.ds(dst_off + piece * SCAN_W, SCAN_W)] = hr
            dst_ref[rows, pl.ds(dst_off + HALF_ST + piece * SCAN_W, SCAN_W)] = hi
            new_carries.append((jnp.broadcast_to(hr[carry_row:carry_row + 1, :], (SUBLANES, SCAN_W)),
                                jnp.broadcast_to(hi[carry_row:carry_row + 1, :], (SUBLANES, SCAN_W))))
            if extra is not None:
                new_accs.append(extra(rows, piece, (xr0, xi0), (hr, hi), accs[piece]))
        return tuple(new_carries), tuple(new_accs)

    base = half * 2 * HALF_ST
    carries0 = tuple((carry_ref[:, pl.ds(base + p * SCAN_W, SCAN_W)],
                      carry_ref[:, pl.ds(base + HALF_ST + p * SCAN_W, SCAN_W)]) for p in range(SCAN_PIECES))
    zero = jnp.zeros((SUBLANES, SCAN_W), F32)
    accs0 = tuple((zero, zero) for _ in range(SCAN_PIECES)) if extra is not None else ()
    state = (carries0, accs0)
    for t in range(n_tiles):
        state = step(t, state)
    carries, accs = state
    for p in range(SCAN_PIECES):
        carry_ref[:, pl.ds(base + p * SCAN_W, SCAN_W)] = carries[p][0]
        carry_ref[:, pl.ds(base + HALF_ST + p * SCAN_W, SCAN_W)] = carries[p][1]
    return accs


def _ssm_tail(hb, u, c_ref, glu_ref, glub_ref, dskip_ref):
    ypre = u * dskip_ref[...]
    parts = []
    for half in range(2):
        parts.append(_dot(hb[half], c_ref[half]))
    ypre = ypre + jnp.concatenate(parts, axis=1)
    yg, th = _gelu(ypre)
    zz = _dot(yg.astype(BF16), glu_ref[...]) + glub_ref[...]
    z1, z2 = zz[:, :SSM_WIDTH], zz[:, SSM_WIDTH:]
    sg = jax.nn.sigmoid(z2)
    return ypre, th, yg, z1, sg


def _ssm_fwd(z, bblk, cblk, glu, glub, dskip, fwdc, batch, t_chunk, name):
    n = z.shape[0]
    nk = n // batch // t_chunk
    n_tiles = t_chunk // SUBLANES

    def body(u_ref, b_ref, c_ref, glu_ref, glub_ref, dskip_ref, k_ref, y_ref, h_ref, bu_ref, carry_ref):
        @pl.when(pl.program_id(1) == 0)
        def _():
            carry_ref[...] = jnp.zeros_like(carry_ref)

        u = u_ref[...]
        ub = u.astype(BF16)
        for half in range(2):
            bu_ref[half] = _dot(ub[:, half * HALF_IN:(half + 1) * HALF_IN], b_ref[half])
            _scan_tiles(bu_ref.at[half], h_ref, half * 2 * HALF_ST, k_ref, half, carry_ref, n_tiles, False)
        hb = [h_ref[:, half * 2 * HALF_ST:(half + 1) * 2 * HALF_ST].astype(BF16) for half in range(2)]
        _, _, _, z1, sg = _ssm_tail(hb, u, c_ref, glu_ref, glub_ref, dskip_ref)
        y_ref[...] = z1 * sg

    fixed2 = lambda b, k: (0, 0)
    fixed3 = lambda b, k: (0, 0, 0)
    row = lambda b, k: (b * nk + k, 0)
    return pl.pallas_call(
        body, name=name, grid=(batch, nk),
        in_specs=[
            pl.BlockSpec((t_chunk, SSM_WIDTH), row),
            pl.BlockSpec((2, HALF_IN, 2 * HALF_ST), fixed3),
            pl.BlockSpec((2, 2 * HALF_ST, HALF_IN), fixed3),
            pl.BlockSpec((SSM_WIDTH, 2 * SSM_WIDTH), fixed2),
            pl.BlockSpec((1, 2 * SSM_WIDTH), fixed2),
            pl.BlockSpec((1, SSM_WIDTH), fixed2),
            pl.BlockSpec((2, 4, SUBLANES, 2 * HALF_ST), lambda b, k: (0, 0, 0, 0)),
        ],
        out_specs=[pl.BlockSpec((t_chunk, SSM_WIDTH), row), pl.BlockSpec((t_chunk, 4 * HALF_ST), row)],
        out_shape=[jax.ShapeDtypeStruct((n, SSM_WIDTH), F32), jax.ShapeDtypeStruct((n, 4 * HALF_ST), F32)],
        scratch_shapes=[pltpu.VMEM((2, t_chunk, 2 * HALF_ST), F32), pltpu.VMEM((SUBLANES, 4 * HALF_ST), F32)],
        compiler_params=_cp("parallel", "arbitrary"),
    )(z, bblk, cblk, glu, glub, dskip, fwdc)


def _ssm_bwd(z, h, dy, bblk, cblk, glu, glub, dskip, revc, batch, t_chunk, name):
    n = z.shape[0]
    nk = n // batch // t_chunk
    n_tiles = t_chunk // SUBLANES

    def body(u_ref, h_ref, dy_ref, b_ref, c_ref, glu_ref, glub_ref, dskip_ref, k_ref,
             du_ref, dglu_ref, dglub_ref, ddskip_ref, dct_ref, db_ref, q_ref, g_ref, carry_ref):
        first = jnp.logical_and(pl.program_id(0) == 0, pl.program_id(1) == 0)

        @pl.when(first)
        def _():
            for r in (dglu_ref, dglub_ref, ddskip_ref, dct_ref, db_ref, q_ref):
                r[...] = jnp.zeros_like(r)

        @pl.when(pl.program_id(1) == 0)
        def _():
            carry_ref[...] = jnp.zeros_like(carry_ref)

        u = u_ref[...]
        ub = u.astype(BF16)
        hb = [h_ref[:, half * 2 * HALF_ST:(half + 1) * 2 * HALF_ST].astype(BF16) for half in range(2)]
        ypre, th, yg, z1, sg = _ssm_tail(hb, u, c_ref, glu_ref, glub_ref, dskip_ref)
        dout = dy_ref[...]
        dz = jnp.concatenate([dout * sg, dout * z1 * sg * (1.0 - sg)], axis=1)
        dzb = dz.astype(BF16)
        dglu_ref[...] += _dot_tn(yg.astype(BF16), dzb)
        dglub_ref[...] += _rows8(dz)
        dypre = _dot_nt(dzb, glu_ref[...]) * _gelu_grad(ypre, th)
        ddskip_ref[...] += _rows8(dypre * u)
        dypb = dypre.astype(BF16)
        du_parts = []
        for half in range(2):
            dyp_h = dypb[:, half * HALF_IN:(half + 1) * HALF_IN]
            dct_ref[half] += _dot_tn(dyp_h, hb[half])
            g_ref[half] = _dot_nt(dyp_h, c_ref[half])

            def extra(rows, piece, x_in, g_out, acc, half=half):
                er, ei = g_out[0] - x_in[0], g_out[1] - x_in[1]
                base = half * 2 * HALF_ST + piece * SCAN_W
                hr = h_ref[rows, pl.ds(base, SCAN_W)]
                hi = h_ref[rows, pl.ds(base + HALF_ST, SCAN_W)]
                return acc[0] + (er * hr + ei * hi), acc[1] + (er * hi - ei * hr)

            accs = _scan_tiles(g_ref.at[half], g_ref.at[half], 0, k_ref, half, carry_ref, n_tiles, True, extra)
            for piece in range(SCAN_PIECES):
                base = half * 2 * HALF_ST + piece * SCAN_W
                q_ref[:, pl.ds(base, SCAN_W)] += accs[piece][0]
                q_ref[:, pl.ds(base + HALF_ST, SCAN_W)] += accs[piece][1]
            gb = g_ref[half].astype(BF16)
            db_ref[half] += _dot_tn(ub[:, half * HALF_IN:(half + 1) * HALF_IN], gb)
            du_parts.append(_dot_nt(gb, b_ref[half]))
        du_ref[...] = dypre * dskip_ref[...] + jnp.concatenate(du_parts, axis=1)

    fixed2 = lambda b, k: (0, 0)
    fixed3 = lambda b, k: (0, 0, 0)
    row = lambda b, k: (b * nk + (nk - 1 - k), 0)
    return pl.pallas_call(
        body, name=name, grid=(batch, nk),
        in_specs=[
            pl.BlockSpec((t_chunk, SSM_WIDTH), row),
            pl.BlockSpec((t_chunk, 4 * HALF_ST), row),
            pl.BlockSpec((t_chunk, SSM_WIDTH), row),
            pl.BlockSpec((2, HALF_IN, 2 * HALF_ST), fixed3),
            pl.BlockSpec((2, 2 * HALF_ST, HALF_IN), fixed3),
            pl.BlockSpec((SSM_WIDTH, 2 * SSM_WIDTH), fixed2),
            pl.BlockSpec((1, 2 * SSM_WIDTH), fixed2),
            pl.BlockSpec((1, SSM_WIDTH), fixed2),
            pl.BlockSpec((2, 4, SUBLANES, 2 * HALF_ST), lambda b, k: (0, 0, 0, 0)),
        ],
        out_specs=[
            pl.BlockSpec((t_chunk, SSM_WIDTH), row),
            pl.BlockSpec((SSM_WIDTH, 2 * SSM_WIDTH), fixed2),
            pl.BlockSpec((SUBLANES, 2 * SSM_WIDTH), fixed2),
            pl.BlockSpec((SUBLANES, SSM_WIDTH), fixed2),
            pl.BlockSpec((2, HALF_IN, 2 * HALF_ST), fixed3),
            pl.BlockSpec((2, HALF_IN, 2 * HALF_ST), fixed3),
            pl.BlockSpec((SUBLANES, 4 * HALF_ST), fixed2),
        ],
        out_shape=[
            jax.ShapeDtypeStruct((n, SSM_WIDTH), F32),
            jax.ShapeDtypeStruct((SSM_WIDTH, 2 * SSM_WIDTH), F32),
            jax.ShapeDtypeStruct((SUBLANES, 2 * SSM_WIDTH), F32),
            jax.ShapeDtypeStruct((SUBLANES, SSM_WIDTH), F32),
            jax.ShapeDtypeStruct((2, HALF_IN, 2 * HALF_ST), F32),
            jax.ShapeDtypeStruct((2, HALF_IN, 2 * HALF_ST), F32),
            jax.ShapeDtypeStruct((SUBLANES, 4 * HALF_ST), F32),
        ],
        scratch_shapes=[pltpu.VMEM((2, t_chunk, 2 * HALF_ST), F32), pltpu.VMEM((SUBLANES, 4 * HALF_ST), F32)],
        compiler_params=_cp("arbitrary", "arbitrary"),
    )(z, h, dy, bblk, cblk, glu, glub, dskip, revc)


def _gm_chunk_fwd(u, v, gain_ref, w_ref, bias_ref):
    ug, thu = _gelu(u)
    vg, thv = _gelu(v)
    rs, vns, ss = [], [], []
    for hh in range(GM_HEADS):
        cs = slice(hh * GM_HEAD_DIM, (hh + 1) * GM_HEAD_DIM)
        vn, _, r = _rms_fwd(vg[:, cs], gain_ref[:, cs])
        s = _dot(w_ref[hh], vn.astype(BF16)) + bias_ref[:, cs]
        rs.append(r)
        vns.append(vn)
        ss.append(s)
    return ug, thu, thv, vg, rs, vns, ss


def _gm_fwd(z, gain, w_tril, bias, rows, name):
    n = z.shape[0]
    chunks = rows // GM_CHUNK

    def body(u_ref, v_ref, gain_ref, w_ref, bias_ref, y_ref):
        for c in range(chunks):
            rs_ = slice(c * GM_CHUNK, (c + 1) * GM_CHUNK)
            ug, _, _, _, _, _, ss = _gm_chunk_fwd(u_ref[rs_, :], v_ref[rs_, :], gain_ref, w_ref, bias_ref)
            y_ref[rs_, :] = ug * jnp.concatenate(ss, axis=1)

    return pl.pallas_call(
        body, name=name, grid=(n // rows,),
        in_specs=[
            pl.BlockSpec((rows, GM_WIDTH), lambda i: (i, 1)),
            pl.BlockSpec((rows, GM_WIDTH), lambda i: (i, 2)),
            pl.BlockSpec((1, GM_WIDTH), lambda i: (0, 0)),
            pl.BlockSpec((GM_HEADS, GM_CHUNK, GM_CHUNK), lambda i: (0, 0, 0)),
            pl.BlockSpec((GM_CHUNK, GM_WIDTH), lambda i: (0, 0)),
        ],
        out_specs=pl.BlockSpec((rows, GM_WIDTH), lambda i: (i, 0)),
        out_shape=jax.ShapeDtypeStruct((n, GM_WIDTH), F32),
        compiler_params=_cp("parallel"),
    )(z, z, gain, w_tril, bias)


def _gm_bwd(z, dy, gain, w_tril, bias, rows, name):
    n = z.shape[0]
    chunks = rows // GM_CHUNK

    def body(u_ref, v_ref, dy_ref, gain_ref, w_ref, bias_ref, du_ref, dv_ref, dw_ref, dbias_ref, dgain_ref):
        @pl.when(pl.program_id(0) == 0)
        def _():
            dw_ref[...] = jnp.zeros_like(dw_ref)
            dbias_ref[...] = jnp.zeros_like(dbias_ref)
            dgain_ref[...] = jnp.zeros_like(dgain_ref)

        for c in range(chunks):
            rs_ = slice(c * GM_CHUNK, (c + 1) * GM_CHUNK)
            u, v = u_ref[rs_, :], v_ref[rs_, :]
            ug, thu, thv, vg, rs, vns, ss = _gm_chunk_fwd(u, v, gain_ref, w_ref, bias_ref)
            dout = dy_ref[rs_, :]
            ds = dout * ug
            du_ref[rs_, :] = dout * jnp.concatenate(ss, axis=1) * _gelu_grad(u, thu)
            dbias_ref[...] += ds
            dvg_parts, dgain_parts = [], []
            for hh in range(GM_HEADS):
                cs = slice(hh * GM_HEAD_DIM, (hh + 1) * GM_HEAD_DIM)
                dsb = ds[:, cs].astype(BF16)
                dvn = _dot_tn(w_ref[hh], dsb)
                dw_ref[hh] += _dot_nt(dsb, vns[hh].astype(BF16))
                g = gain_ref[:, cs]
                xh = vg[:, cs] * rs[hh]
                dvg, dgr = _rms_bwd(dvn, xh, rs[hh], g)
                dvg_parts.append(dvg)
                dgain_parts.append(dgr)
            dv_ref[rs_, :] = jnp.concatenate(dvg_parts, axis=1) * _gelu_grad(v, thv)
            dgain_ref[...] += _rows8(jnp.concatenate(dgain_parts, axis=1))

    row = lambda i: (i, 0)
    return pl.pallas_call(
        body, name=name, grid=(n // rows,),
        in_specs=[
            pl.BlockSpec((rows, GM_WIDTH), lambda i: (i, 1)),
            pl.BlockSpec((rows, GM_WIDTH), lambda i: (i, 2)),
            pl.BlockSpec((rows, GM_WIDTH), row),
            pl.BlockSpec((1, GM_WIDTH), lambda i: (0, 0)),
            pl.BlockSpec((GM_HEADS, GM_CHUNK, GM_CHUNK), lambda i: (0, 0, 0)),
            pl.BlockSpec((GM_CHUNK, GM_WIDTH), lambda i: (0, 0)),
        ],
        out_specs=[
            pl.BlockSpec((rows, GM_WIDTH), row), pl.BlockSpec((rows, GM_WIDTH), row),
            pl.BlockSpec((GM_HEADS, GM_CHUNK, GM_CHUNK), lambda i: (0, 0, 0)),
            pl.BlockSpec((GM_CHUNK, GM_WIDTH), lambda i: (0, 0)),
            pl.BlockSpec((SUBLANES, GM_WIDTH), lambda i: (0, 0)),
        ],
        out_shape=[
            jax.ShapeDtypeStruct((n, GM_WIDTH), F32), jax.ShapeDtypeStruct((n, GM_WIDTH), F32),
            jax.ShapeDtypeStruct((GM_HEADS, GM_CHUNK, GM_CHUNK), F32),
            jax.ShapeDtypeStruct((GM_CHUNK, GM_WIDTH), F32),
            jax.ShapeDtypeStruct((SUBLANES, GM_WIDTH), F32),
        ],
        compiler_params=_cp("arbitrary"),
    )(z, z, dy, gain, w_tril, bias)


def _loss_head(x, gain, target, tm, name):
    n = x.shape[0]

    def body(x_ref, g_ref, t_ref, dx_ref, sq_ref, dgain_ref):
        @pl.when(pl.program_id(0) == 0)
        def _():
            sq_ref[...] = jnp.zeros_like(sq_ref)
            dgain_ref[...] = jnp.zeros_like(dgain_ref)

        g = g_ref[...]
        y, xh, r = _rms_fwd(x_ref[...], g)
        err = y - t_ref[...]
        sq_ref[...] += _rows8(err * err)
        dx, dgr = _rms_bwd(err * (1.0 / D_MODEL), xh, r, g)
        dx_ref[...] = dx
        dgain_ref[...] += _rows8(dgr)

    row = lambda i: (i, 0)
    fixed = lambda i: (0, 0)
    return pl.pallas_call(
        body, name=name, grid=(n // tm,),
        in_specs=[pl.BlockSpec((tm, D_MODEL), row), pl.BlockSpec((1, D_MODEL), fixed), pl.BlockSpec((tm, D_MODEL), row)],
        out_specs=[pl.BlockSpec((tm, D_MODEL), row), pl.BlockSpec((SUBLANES, D_MODEL), fixed),
                   pl.BlockSpec((SUBLANES, D_MODEL), fixed)],
        out_shape=[jax.ShapeDtypeStruct((n, D_MODEL), F32), jax.ShapeDtypeStruct((SUBLANES, D_MODEL), F32),
                   jax.ShapeDtypeStruct((SUBLANES, D_MODEL), F32)],
        compiler_params=_cp("arbitrary"),
    )(x, gain, target)


def _adam_math(w, g, m, v):
    m2 = ADAM_B1 * m + (1.0 - ADAM_B1) * g
    v2 = ADAM_B2 * v + (1.0 - ADAM_B2) * (g * g)
    m_hat = m2 / (1.0 - ADAM_B1 ** ADAM_STEP)
    v_hat = v2 / (1.0 - ADAM_B2 ** ADAM_STEP)
    delta = -ADAM_LR * (m_hat / (jnp.sqrt(v_hat) + ADAM_EPS) + ADAM_WD * w)
    return delta, m2, v2


def _adam_sharded(parts, w, m, v, layer, earlier, name):
    depth, r, c = w.shape
    tr = max(t for t in range(16, 353, 16) if r % t == 0)

    def body(p_ref, w_ref, m_ref, v_ref, *rest):
        g_ref, d_ref, m2_ref, v2_ref = rest[-4:]
        g = p_ref[0].astype(F32)
        for s in range(1, N_DEV):
            g = g + p_ref[s].astype(F32)
        delta, m2, v2 = _adam_math(w_ref[...], g, m_ref[...], v_ref[...])
        g_ref[...] = g
        d_ref[...] = delta
        m2_ref[...] = m2
        v2_ref[...] = v2

    blk = pl.BlockSpec((None, tr, c), lambda i: (layer, i, 0))
    extra = [] if earlier is None else list(earlier)
    return pl.pallas_call(
        body, name=name, grid=(r // tr,),
        in_specs=[pl.BlockSpec((N_DEV, tr, c), lambda i: (0, i, 0)), blk, blk, blk]
        + [pl.BlockSpec(memory_space=pl.ANY)] * len(extra),
        out_specs=[blk, blk, blk, blk],
        out_shape=[jax.ShapeDtypeStruct((depth, r, c), F32)] * 4,
        input_output_aliases={4 + i: i for i in range(len(extra))},
        compiler_params=_cp("parallel"),
    )(parts, w, m, v, *extra)


def _adam_packed(g, w, m, v, name):
    r, c = g.shape

    def body(g_ref, w_ref, m_ref, v_ref, d_ref, m2_ref, v2_ref):
        delta, m2, v2 = _adam_math(w_ref[...], g_ref[...], m_ref[...], v_ref[...])
        d_ref[...] = delta
        m2_ref[...] = m2
        v2_ref[...] = v2

    blk = pl.BlockSpec((r, c), lambda i: (0, 0))
    return pl.pallas_call(
        body, name=name, grid=(1,),
        in_specs=[blk, blk, blk, blk], out_specs=[blk, blk, blk],
        out_shape=[jax.ShapeDtypeStruct((r, c), F32)] * 3,
        compiler_params=_cp("arbitrary"),
    )(g, w, m, v)


def _my_place():
    return lax.axis_index("x"), lax.axis_index("y"), lax.axis_index("c")


def _flip(place, rel):
    x, y, c = place
    return (1 - x if rel & 4 else x, 1 - y if rel & 2 else y, 1 - c if rel & 1 else c)


def _index(place):
    return 4 * place[0] + 2 * place[1] + place[2]


def _all_gather(shards, name):
    na = len(shards)

    def body(*refs):
        xs, outs = refs[:na], refs[na:2 * na]
        send_sems, recv_sems, local_sems = refs[2 * na:]
        me = _my_place()
        sibling = _flip(me, 1)
        chips = [_flip(me, 4), _flip(me, 2), _flip(me, 6)]

        def copy(a, k, block, to, src=None):
            slot = outs[a].at[_index(block)]
            return pltpu.make_async_remote_copy(
                src_ref=slot if src is None else src, dst_ref=slot,
                send_sem=send_sems.at[a, k], recv_sem=recv_sems.at[a, k],
                device_id=to, device_id_type=MESH)

        mine = [pltpu.make_async_copy(xs[a], outs[a].at[_index(me)], local_sems.at[a]) for a in range(na)]
        for cp in mine:
            cp.start()
        first = []
        for a in range(na):
            first.append(copy(a, 0, me, sibling, src=xs[a]))
            first += [copy(a, 1 + j, me, chip, src=xs[a]) for j, chip in enumerate(chips)]
        for cp in first:
            cp.start()
        passed = []
        for a in range(na):
            for j, chip in enumerate(chips):
                copy(a, 1 + j, chip, me).wait_recv()
                fwd = copy(a, 4 + j, chip, sibling)
                fwd.start()
                passed.append(fwd)
        for a in range(na):
            copy(a, 0, sibling, me).wait_recv()
            for j, chip in enumerate(chips):
                copy(a, 4 + j, _flip(chip, 1), me).wait_recv()
        for cp in first + passed:
            cp.wait_send()
        for cp in mine:
            cp.wait()

    hbm = pl.BlockSpec(memory_space=pl.ANY)
    return pl.pallas_call(
        body, name=name,
        in_specs=[hbm] * na, out_specs=[hbm] * na,
        out_shape=[jax.ShapeDtypeStruct((N_DEV,) + s.shape, s.dtype) for s in shards],
        scratch_shapes=[pltpu.SemaphoreType.DMA((na, 7)), pltpu.SemaphoreType.DMA((na, 7)),
                        pltpu.SemaphoreType.DMA((na,))],
    )(*shards)


_HBM = pl.BlockSpec(memory_space=pltpu.HBM)
_SEM = pl.BlockSpec(memory_space=pltpu.SEMAPHORE)
_EFFECT = pltpu.SideEffectType.DATAFLOW_SIDE_EFFECTING


def _exchange_copy(src_ref, land_ref, send_sems, recv_sems, a, rel, me, scatter, landed):
    peer = _flip(me, rel)
    src = src_ref.at[_index(peer)] if scatter else src_ref
    return pltpu.make_async_remote_copy(
        src_ref=src, dst_ref=land_ref.at[_index(peer if landed else me)],
        send_sem=send_sems.at[a * (N_DEV - 1) + rel - 1], recv_sem=recv_sems.at[a * (N_DEV - 1) + rel - 1],
        device_id=peer, device_id_type=MESH)


def _own_slot(data, me, scatter):
    if scatter:
        own = lax.dynamic_slice_in_dim(data, me, 1, axis=0)
        shape = data.shape
    else:
        own = data[None]
        shape = (N_DEV,) + data.shape
    start = (me,) + (0,) * (len(shape) - 1)
    return lax.dynamic_update_slice(lax.empty(shape, data.dtype), own, start)


def _exchange_start(groups, me, scatter, name, after=None):
    sizes = [len(g) for g in groups]
    srcs = [a for g in groups for a in g]
    lands = [_own_slot(a, me, scatter) for a in srcs]
    na, ng = len(srcs), len(groups)
    deps = [] if after is None else [after]

    def body(*refs):
        src_refs, land_refs = refs[:na], refs[na:2 * na]
        sems = refs[2 * na + len(deps):2 * na + len(deps) + 2 * ng]
        token = refs[-1]
        place = _my_place()
        a = 0
        for g, size in enumerate(sizes):
            for k in range(size):
                for rel in range(1, N_DEV):
                    _exchange_copy(src_refs[a], land_refs[a], sems[2 * g], sems[2 * g + 1], k, rel, place, scatter,
                                   False).start()
                a += 1
        token[...] = jnp.zeros_like(token)

    sem_shapes = [pltpu.SemaphoreType.DMA((size * (N_DEV - 1),)) for size in sizes for _ in range(2)]
    outs = pl.pallas_call(
        body, name=name,
        in_specs=[_HBM] * (2 * na) + [pl.BlockSpec(memory_space=pl.ANY)] * len(deps),
        out_specs=[_SEM] * (2 * ng) + [_HBM] * (2 * na) + [pl.BlockSpec(memory_space=pltpu.VMEM)],
        out_shape=sem_shapes + [pltpu.HBM(a.shape, a.dtype) for a in srcs + lands]
        + [jax.ShapeDtypeStruct((SUBLANES, LANES), F32)],
        input_output_aliases={i: 2 * ng + i for i in range(2 * na)},
        compiler_params=pltpu.CompilerParams(has_side_effects=_EFFECT),
    )(*[pltpu.with_memory_space_constraint(a, pltpu.HBM) for a in srcs + lands], *deps)
    sems, thru, token = outs[:2 * ng], outs[2 * ng:2 * ng + 2 * na], outs[-1]
    handles, a = [], 0
    for g, size in enumerate(sizes):
        handles.append((sems[2 * g], sems[2 * g + 1], thru[a:a + size], thru[na + a:na + a + size]))
        a += size
    return handles, token


def _exchange_wait(handle, after, scatter, name):
    send_sems, recv_sems, srcs, lands = handle
    na = len(srcs)

    def body(*refs):
        src_refs, land_refs = refs[:na], refs[na:2 * na]
        send_ref, recv_ref = refs[2 * na], refs[2 * na + 1]
        place = _my_place()
        for a in range(na):
            for rel in range(1, N_DEV):
                cp = _exchange_copy(src_refs[a], land_refs[a], send_ref, recv_ref, a, rel, place, scatter, True)
                cp.wait_send()
                cp.wait_recv()

    outs = pl.pallas_call(
        body, name=name,
        in_specs=[_HBM] * (2 * na) + [_SEM, _SEM, pl.BlockSpec(memory_space=pl.ANY)],
        out_specs=[_HBM] * (2 * na),
        out_shape=[pltpu.HBM(a.shape, a.dtype) for a in list(srcs) + list(lands)],
        input_output_aliases={i: i for i in range(2 * na)},
        compiler_params=pltpu.CompilerParams(has_side_effects=_EFFECT),
    )(*srcs, *lands, send_sems, recv_sems, after)
    return outs[na:]


def _behind(arr, token):
    return arr + token[0:1, 0:1]


def _all_reduce_small(g, name):
    _, r, c = g.shape

    def body(g_ref, o_ref, land_ref, red_ref, send1, recv1, send2, recv2):
        me = _my_place()
        idx = _index(me)

        def scatter(rel):
            peer = _flip(me, rel)
            return pltpu.make_async_remote_copy(
                src_ref=g_ref.at[_index(peer)], dst_ref=land_ref.at[idx],
                send_sem=send1.at[rel - 1], recv_sem=recv1.at[rel - 1], device_id=peer, device_id_type=MESH)

        def gather(rel):
            peer = _flip(me, rel)
            return pltpu.make_async_remote_copy(
                src_ref=red_ref, dst_ref=o_ref.at[idx],
                send_sem=send2.at[rel - 1], recv_sem=recv2.at[rel - 1], device_id=peer, device_id_type=MESH)

        for rel in range(1, N_DEV):
            scatter(rel).start()
        land_ref[idx] = g_ref[idx]
        for rel in range(1, N_DEV):
            scatter(rel).wait()
        acc = land_ref[0]
        for s in range(1, N_DEV):
            acc = acc + land_ref[s]
        red_ref[...] = acc
        for rel in range(1, N_DEV):
            gather(rel).start()
        o_ref[idx] = acc
        for rel in range(1, N_DEV):
            gather(rel).wait()

    vmem = pl.BlockSpec(memory_space=pltpu.VMEM)
    return pl.pallas_call(
        body, name=name,
        in_specs=[vmem], out_specs=vmem,
        out_shape=jax.ShapeDtypeStruct(g.shape, F32),
        scratch_shapes=[pltpu.VMEM(g.shape, F32), pltpu.VMEM((r, c), F32)]
        + [pltpu.SemaphoreType.DMA((N_DEV - 1,))] * 4,
        compiler_params=pltpu.CompilerParams(vmem_limit_bytes=VMEM_LIMIT),
    )(g)


def _ssm_discretize(a_re, a_im, log_dt, b_re, b_im):
    dt = jnp.exp(log_dt)[:, None]
    mag = jnp.exp(a_re * dt)
    lr, li = mag * jnp.cos(a_im * dt), mag * jnp.sin(a_im * dt)
    den = a_re * a_re + a_im * a_im
    qr = ((lr - 1.0) * a_re + li * a_im) / den
    qi = (li * a_re - (lr - 1.0) * a_im) / den
    bbr = qr[..., None] * b_re - qi[..., None] * b_im
    bbi = qr[..., None] * b_im + qi[..., None] * b_re
    return lr, li, bbr, bbi


def _halves(a):
    return a.reshape((2, HALF_GROUPS) + a.shape[1:])


def _block_diag_mask(g, r, c):
    rows = lax.broadcasted_iota(jnp.int32, (g * r, g * c), 0) // r
    cols = lax.broadcasted_iota(jnp.int32, (g * r, g * c), 1) // c
    return rows == cols


def _block_diag(blocks):
    g, r, c = blocks.shape
    spread = jnp.tile(jnp.eye(c, dtype=blocks.dtype), (1, g))
    full = jnp.dot(blocks.reshape(g * r, c), spread, precision=lax.Precision.HIGHEST)
    return jnp.where(_block_diag_mask(g, r, c), full, 0.0)


def _block_diag_take(dense, g, r, c):
    gather = jnp.tile(jnp.eye(c, dtype=dense.dtype), (g, 1))
    kept = jnp.where(_block_diag_mask(g, r, c), dense, 0.0)
    return jnp.dot(kept, gather, precision=lax.Precision.HIGHEST).reshape(g, r, c)


def _ssm_matrices(bbr, bbi, c_re, c_im, glu_w, glu_b, d_skip):
    bre, bim = _halves(jnp.swapaxes(bbr, 1, 2)), _halves(jnp.swapaxes(bbi, 1, 2))
    bblk = jnp.stack([jnp.concatenate([_block_diag(bre[h]), _block_diag(bim[h])], axis=1) for h in range(2)])
    cre, cim = _halves(jnp.swapaxes(c_re, 1, 2)), _halves(jnp.swapaxes(c_im, 1, 2))
    cblk = jnp.stack([jnp.concatenate([_block_diag(cre[h]), -_block_diag(cim[h])], axis=0) for h in range(2)])
    glu = jnp.concatenate([_block_diag(glu_w[:, :, :SSM_CH]), _block_diag(glu_w[:, :, SSM_CH:])], axis=1)
    glub = jnp.concatenate([glu_b[:, :SSM_CH].reshape(1, -1), glu_b[:, SSM_CH:].reshape(1, -1)], axis=1)
    return bblk.astype(BF16), cblk.astype(BF16), glu.astype(BF16), glub, d_skip.reshape(1, -1)


def _scan_constants(lr, li, reverse):
    if reverse:
        li = -li
    pows = [(lr, li)]
    for _ in range(SUBLANES - 1):
        pr, pi = pows[-1]
        pows.append((pr * lr - pi * li, pr * li + pi * lr))
    row = jnp.arange(SUBLANES)[:, None]

    def flat(a):
        return a.reshape(2, 1, HALF_ST)

    mats = []
    for s in (1, 2, 4):
        keep = (row + s <= SUBLANES - 1) if reverse else (row >= s)
        mats.append(tuple(jnp.where(keep[None], flat(p), 0.0) for p in pows[s - 1]))
    order = [SUBLANES - 1 - j for j in range(SUBLANES)] if reverse else list(range(SUBLANES))
    mats.append(tuple(jnp.concatenate([flat(pows[j][k]) for j in order], axis=1) for k in range(2)))
    return jnp.stack([jnp.concatenate([m[0], m[1]], axis=2) for m in mats], axis=1)


def _pack(arrs, rows):
    flat = jnp.concatenate([a.reshape(-1) for a in arrs])
    return jnp.pad(flat, (0, rows * LANES - flat.shape[0])).reshape(rows, LANES)


def _unpack(buf, like):
    flat = buf.reshape(-1)
    out, off = [], 0
    for a in like:
        out.append(flat[off:off + a.size].reshape(a.shape))
        off += a.size
    return out


SMALL = ("norm_ffn1", "norm_mix", "ssm_a_re", "ssm_a_im", "ssm_log_dt", "ssm_b_re", "ssm_b_im", "ssm_c_re",
         "ssm_c_im", "ssm_d", "ssm_glu_w", "ssm_glu_b", "gm_v_gain", "gm_w_s", "gm_b_s", "gain_ssm_out",
         "gain_gm_out", "norm_ffn2", "norm_final")
BIG = ("ffn1_w_in", "ffn1_w_out", "mix_w_in", "mix_w_out", "ffn2_w_in", "ffn2_w_out")
TRANSPOSED = ("ffn1_w_in", "mix_w_in", "ffn2_w_in")
WEIGHTS = ("norm_ffn1", "ffn1_w_in", "ffn1_w_out", "norm_mix", "mix_w_in", "ssm_a_re", "ssm_a_im", "ssm_log_dt",
           "ssm_b_re", "ssm_b_im", "ssm_c_re", "ssm_c_im", "ssm_d", "ssm_glu_w", "ssm_glu_b", "gm_v_gain", "gm_w_s",
           "gm_b_s", "gain_ssm_out", "gain_gm_out", "mix_w_out", "norm_ffn2", "ffn2_w_in", "ffn2_w_out", "norm_final")


def _step(x, target, w, m, v):
    batch, seq, _ = x.shape
    n = batch * seq
    depth = w["norm_ffn1"].shape[0]
    tm = min(512, n)
    tm_ffn = min(1024, n)
    tk = min(4096, n)
    t_chunk = min(256, seq)
    t_chunk_fwd = min(2 * t_chunk, seq)
    gm_rows = min(1024, seq)
    x = x.reshape(n, D_MODEL)
    target = target.reshape(n, D_MODEL)

    assert depth == 2
    me = _index(_my_place())
    shard = lambda group, l: [w[f"{group}_w_in"][l].astype(BF16), w[f"{group}_w_out"][l].astype(BF16)]
    batches = ([("mix", 0), ("ffn2", 0)], [("ffn1", 1), ("mix", 1)], [("ffn2", 1)])
    gathered, pending = {("ffn1", 0): tuple(_all_gather(shard("ffn1", 0), "all_gather_first"))}, {}

    def gather_start(i, after):
        handles, tok = _exchange_start([shard(g, l) for g, l in batches[i]], me, False, f"all_gather_start_{i}",
                                       after)
        pending.update(zip(batches[i], handles))
        return tok

    def weights(group, l, after=None):
        if (group, l) not in gathered:
            w_in, w_out = _exchange_wait(pending[(group, l)], after, False, f"all_gather_wait_{group}_{l}")
            if group == "mix":
                w_in = jnp.transpose(w_in, (1, 0, 2)).reshape(D_MODEL, IN_COLS)
                w_out = w_out.reshape(D_MODEL, D_MODEL)
            gathered[(group, l)] = (w_in, w_out)
        return gathered[(group, l)]

    tril = jnp.tril(jnp.ones((GM_CHUNK, GM_CHUNK), bool))
    layers = []
    for l in range(depth):
        disc, disc_vjp = jax.vjp(_ssm_discretize, w["ssm_a_re"][l], w["ssm_a_im"][l], w["ssm_log_dt"][l],
                                 w["ssm_b_re"][l], w["ssm_b_im"][l])
        lr, li, bbr, bbi = disc
        bblk, cblk, glu, glub, dskip = _ssm_matrices(bbr, bbi, w["ssm_c_re"][l], w["ssm_c_im"][l],
                                                     w["ssm_glu_w"][l], w["ssm_glu_b"][l], w["ssm_d"][l])
        layers.append(dict(
            disc_vjp=disc_vjp, lr=lr, li=li, bblk=bblk, cblk=cblk, glu=glu, glub=glub, dskip=dskip,
            fwdc=_scan_constants(lr, li, False), revc=_scan_constants(lr, li, True),
            w_tril=jnp.where(tril[None], w["gm_w_s"][l], 0.0).astype(BF16),
            gm_bias=jnp.repeat(w["gm_b_s"][l].T, GM_HEAD_DIM, axis=1),
            g1=w["norm_ffn1"][l][None], gmix=w["norm_mix"][l][None], g2=w["norm_ffn2"][l][None],
            gv=w["gm_v_gain"][l][None], gs=w["gain_ssm_out"][l][None], gg=w["gain_gm_out"][l][None],
        ))

    saved = []
    for l in range(depth):
        p = layers[l]
        x0 = x
        g1, gmix, g2 = p["g1"], p["gmix"], p["g2"]
        w_in, w_out = weights("ffn1", l, x0)
        if l == 0:
            g1 = _behind(g1, gather_start(0, w_in))
        x1, xn1, gu1 = _ffn_fwd(x0, g1, w_in, w_out, tm_ffn, f"ffn1_fwd_{l}")
        if l == 0:
            gmix = _behind(gmix, gather_start(1, x1))
        mwi, mwo = weights("mix", l, x1)
        z = _mix_in_fwd(x1, gmix, mwi, tm_ffn, f"mix_in_fwd_{l}")
        y_ssm, h = _ssm_fwd(z, p["bblk"], p["cblk"], p["glu"], p["glub"], p["dskip"], p["fwdc"], batch, t_chunk_fwd,
                            f"ssm_fwd_{l}")
        y_gm = _gm_fwd(z, p["gv"], p["w_tril"], p["gm_bias"], gm_rows, f"gm_fwd_{l}")
        x2 = _mix_out_fwd(y_ssm, y_gm, p["gs"], p["gg"], mwo, x1, tm_ffn, f"mix_out_fwd_{l}")
        if l == 0:
            g2 = _behind(g2, gather_start(2, x2))
        x, xn2, gu2 = _ffn_fwd(x2, g2, *weights("ffn2", l, x2), tm_ffn, f"ffn2_fwd_{l}")
        saved.append((x0, x1, x2, z, h, y_ssm, y_gm, xn1, gu1, xn2, gu2))

    dx, sq, dnf = _loss_head(x, w["norm_final"][None], target, tm_ffn, "loss_head")
    loss = lax.psum((0.5 / D_MODEL) * jnp.sum(sq), AXES)

    small = {k: [None] * depth for k in SMALL if k != "norm_final"}
    sent = []

    def send(group, l, keys, parts):
        (handle,), tok = _exchange_start([parts], me, True, f"reduce_scatter_start_{group}_{l}")
        sent.append((group, l, keys, handle))
        return tok

    token = None
    for l in reversed(range(depth)):
        p = layers[l]
        x0, x1, x2, z, h, y_ssm, y_gm, xn1, gu1, xn2, gu2 = saved[l]
        mwi, mwo = weights("mix", l)
        dx_out = dx
        g2 = p["g2"] if token is None else _behind(p["g2"], token)
        dx, dgu, act, dgain, dyb = _ffn_bwd(x2, g2, dx_out, gu2, *weights("ffn2", l), tm, f"ffn2_bwd_{l}")
        dw_in = _ffn_dw_in(xn2, dgu, tk, f"ffn2_dw_in_{l}")
        dw_out = _ffn_dw_out(act, dyb, tk, f"ffn2_dw_out_{l}").reshape(N_DEV, FF_SHARD // 2, D_MODEL)
        token = send("ffn2", l, ("ffn2_w_in", "ffn2_w_out"), [dw_in, dw_out])
        small["norm_ffn2"][l] = dgain.sum(0)

        dy_ssm, dy_gm, dwo, dgs, dgg = _mix_out_bwd(y_ssm, y_gm, _behind(p["gs"], token), p["gg"], mwo, dx, tm,
                                                    f"mix_out_bwd_{l}")
        dwo = dwo.astype(BF16).reshape(N_DEV, D_MODEL // N_DEV, D_MODEL)
        small["gain_ssm_out"][l] = dgs.sum(0)
        small["gain_gm_out"][l] = dgg.sum(0)

        du_ssm, dglu, dglub, ddskip, dct, db, q = _ssm_bwd(
            z, h, dy_ssm, p["bblk"], p["cblk"], p["glu"], p["glub"], p["dskip"], p["revc"], batch, t_chunk,
            f"ssm_bwd_{l}")
        du_gm, dv_gm, dws, dbias, dgv = _gm_bwd(z, dy_gm, p["gv"], p["w_tril"], p["gm_bias"], gm_rows, f"gm_bwd_{l}")

        q = q.sum(0).reshape(2, 2, HALF_GROUPS, SSM_STATE)
        qr, qi = q[:, 0].reshape(SSM_GROUPS, SSM_STATE), q[:, 1].reshape(SSM_GROUPS, SSM_STATE)
        den = p["lr"] * p["lr"] + p["li"] * p["li"]
        d_re = (qr * p["lr"] + qi * p["li"]) / den
        d_im = (qi * p["lr"] - qr * p["li"]) / den
        dbb = jnp.stack([_block_diag_take(db[hf, :, k * HALF_ST:(k + 1) * HALF_ST], HALF_GROUPS, SSM_CH, SSM_STATE)
                         for k in range(2) for hf in range(2)]).reshape(2, SSM_GROUPS, SSM_CH, SSM_STATE)
        dcc = jnp.stack([_block_diag_take(dct[hf, :, k * HALF_ST:(k + 1) * HALF_ST], HALF_GROUPS, SSM_CH, SSM_STATE)
                         for k in range(2) for hf in range(2)]).reshape(2, SSM_GROUPS, SSM_CH, SSM_STATE)
        da_re, da_im, dlog_dt, db_re, db_im = p["disc_vjp"](
            (d_re, -d_im, jnp.swapaxes(dbb[0], 1, 2), jnp.swapaxes(dbb[1], 1, 2)))
        small["ssm_a_re"][l], small["ssm_a_im"][l], small["ssm_log_dt"][l] = da_re, da_im, dlog_dt
        small["ssm_b_re"][l], small["ssm_b_im"][l] = db_re, db_im
        small["ssm_c_re"][l], small["ssm_c_im"][l] = dcc[0], -dcc[1]
        small["ssm_d"][l] = ddskip.sum(0).reshape(SSM_GROUPS, SSM_CH)
        small["ssm_glu_w"][l] = jnp.concatenate(
            [_block_diag_take(dglu[:, :SSM_WIDTH], SSM_GROUPS, SSM_CH, SSM_CH),
             _block_diag_take(dglu[:, SSM_WIDTH:], SSM_GROUPS, SSM_CH, SSM_CH)], axis=2)
        dglub = dglub.sum(0)
        small["ssm_glu_b"][l] = jnp.concatenate(
            [dglub[:SSM_WIDTH].reshape(SSM_GROUPS, SSM_CH), dglub[SSM_WIDTH:].reshape(SSM_GROUPS, SSM_CH)], axis=1)
        small["gm_v_gain"][l] = dgv.sum(0)
        small["gm_w_s"][l] = jnp.where(tril[None], dws, 0.0)
        small["gm_b_s"][l] = dbias.reshape(GM_CHUNK, GM_HEADS, GM_HEAD_DIM).sum(-1).T

        dx, dwi, dgain = _mix_in_bwd(x1, p["gmix"], du_ssm, du_gm, dv_gm, dx, mwi, tm, f"mix_in_bwd_{l}")
        dwi = dwi.astype(BF16).reshape(N_DEV, IN_COLS // N_DEV, D_MODEL)
        token = send("mix", l, ("mix_w_in", "mix_w_out"), [dwi, dwo])
        small["norm_mix"][l] = dgain.sum(0)

        dx_out = dx
        dx, dgu, act, dgain, dyb = _ffn_bwd(x0, _behind(p["g1"], token), dx_out, gu1, *weights("ffn1", l), tm,
                                       f"ffn1_bwd_{l}")
        small["norm_ffn1"][l] = dgain.sum(0)
        if l > 0:
            dw_in = _ffn_dw_in(xn1, dgu, tk, f"ffn1_dw_in_{l}")
            dw_out = _ffn_dw_out(act, dyb, tk, f"ffn1_dw_out_{l}").reshape(N_DEV, FF_SHARD // 2, D_MODEL)
            token = send("ffn1", l, ("ffn1_w_in", "ffn1_w_out"), [dw_in, dw_out])
            continue
        small_g = [jnp.stack(small[k]) if k != "norm_final" else dnf.sum(0) for k in SMALL]
        total = sum(int(math.prod(w[k].shape)) for k in SMALL)
        rows = -(-total // (LANES * N_DEV * SUBLANES)) * N_DEV * SUBLANES
        g_all = _all_reduce_small(_pack(small_g, rows).reshape(N_DEV, rows // N_DEV, LANES), "all_reduce_small")
        dw_in = _ffn_dw_in(xn1, dgu, tk, f"ffn1_dw_in_{l}", after=g_all)
        token = send("ffn1_in", l, ("ffn1_w_in",), [dw_in])
        dw_out = _ffn_dw_out(act, dyb, tk, f"ffn1_dw_out_{l}", after=token).reshape(
            N_DEV, FF_SHARD // 2, D_MODEL)
        token = send("ffn1_out", l, ("ffn1_w_out",), [dw_out])

    grad_x = dx.reshape(batch, seq, D_MODEL)
    grads, deltas, new_m, new_v = {}, {}, {}, {}

    g_all = _behind(g_all.reshape(rows, LANES), token)
    like = [w[k] for k in SMALL]
    d_p, m_p, v_p = _adam_packed(g_all, _pack(like, rows), _pack([m[k] for k in SMALL], rows),
                                 _pack([v[k] for k in SMALL], rows), "adam_small")
    for k, g_, d_, m_, v_ in zip(SMALL, _unpack(g_all, like), _unpack(d_p, like), _unpack(m_p, like),
                                 _unpack(v_p, like)):
        grads[k], deltas[k], new_m[k], new_v[k] = g_, d_, m_, v_

    results = {}
    after = d_p
    for group, l, keys, handle in sent:
        landed = _exchange_wait(handle, after, True, f"reduce_scatter_wait_{group}_{l}")
        for k, parts in zip(keys, landed):
            view = (lambda a: jnp.swapaxes(a, 1, 2)) if k in TRANSPOSED else (lambda a: a)
            results[k] = _adam_sharded(parts, view(w[k]), view(m[k]), view(v[k]), l, results.get(k),
                                       f"adam_{k}_{l}")
            after = results[k][0]
    for k in BIG:
        view = (lambda a: jnp.swapaxes(a, 1, 2)) if k in TRANSPOSED else (lambda a: a)
        grads[k], deltas[k], new_m[k], new_v[k] = [view(a) for a in results[k]]
    return loss, grad_x, grads, deltas, new_m, new_v


def kernel(x, norm_ffn1, ffn1_w_in, ffn1_w_out, norm_mix, mix_w_in, ssm_a_re, ssm_a_im, ssm_log_dt, ssm_b_re, ssm_b_im, ssm_c_re, ssm_c_im, ssm_d, ssm_glu_w, ssm_glu_b, gm_v_gain, gm_w_s, gm_b_s, gain_ssm_out, gain_gm_out, mix_w_out, norm_ffn2, ffn2_w_in, ffn2_w_out, norm_final, loss_target, m_norm_ffn1, m_ffn1_w_in, m_ffn1_w_out, m_norm_mix, m_mix_w_in, m_ssm_a_re, m_ssm_a_im, m_ssm_log_dt, m_ssm_b_re, m_ssm_b_im, m_ssm_c_re, m_ssm_c_im, m_ssm_d, m_ssm_glu_w, m_ssm_glu_b, m_gm_v_gain, m_gm_w_s, m_gm_b_s, m_gain_ssm_out, m_gain_gm_out, m_mix_w_out, m_norm_ffn2, m_ffn2_w_in, m_ffn2_w_out, m_norm_final, v_norm_ffn1, v_ffn1_w_in, v_ffn1_w_out, v_norm_mix, v_mix_w_in, v_ssm_a_re, v_ssm_a_im, v_ssm_log_dt, v_ssm_b_re, v_ssm_b_im, v_ssm_c_re, v_ssm_c_im, v_ssm_d, v_ssm_glu_w, v_ssm_glu_b, v_gm_v_gain, v_gm_w_s, v_gm_b_s, v_gain_ssm_out, v_gain_gm_out, v_mix_w_out, v_norm_ffn2, v_ffn2_w_in, v_ffn2_w_out, v_norm_final):
    args = locals()
    w = {k: args[k] for k in WEIGHTS}
    m = {k: args["m_" + k] for k in WEIGHTS}
    v = {k: args["v_" + k] for k in WEIGHTS}
    loss, grad_x, grads, deltas, new_m, new_v = _step(x, loss_target, w, m, v)
    return (loss, grad_x, *[grads[k] for k in WEIGHTS], *[deltas[k] for k in WEIGHTS],
            *[new_m[k] for k in WEIGHTS], *[new_v[k] for k in WEIGHTS])
```

```python
import functools
import math

import jax
import jax.numpy as jnp
from jax import lax
from jax.experimental import pallas as pl
from jax.experimental.pallas import tpu as pltpu

F32 = jnp.float32
BF16 = jnp.bfloat16
MESH = pl.DeviceIdType.MESH
AXES = ("x", "y", "c")

N_DEV = 8
D_MODEL = 1024
D_FF = 2816
FF_SHARD = 2 * D_FF // N_DEV
FF_CHUNKS = 4
MXU_DIM = 256
FF_PIECES = tuple((lo, min(lo + MXU_DIM, FF_SHARD)) for lo in range(0, FF_SHARD, MXU_DIM))
SSM_WIDTH = 512
SSM_CH = 16
SSM_GROUPS = 32
SSM_STATE = 64
HALF_GROUPS = 16
HALF_IN = HALF_GROUPS * SSM_CH
HALF_ST = HALF_GROUPS * SSM_STATE
GM_WIDTH = 512
GM_HEADS = 4
GM_HEAD_DIM = 128
GM_CHUNK = 128
IN_COLS = SSM_WIDTH + 2 * GM_WIDTH
EPS = 1e-6
SUBLANES = 8
LANES = 128

ADAM_LR = 0.001
ADAM_B1 = 0.9
ADAM_B2 = 0.999
ADAM_EPS = 1e-08
ADAM_WD = 0.01
ADAM_STEP = 10

VMEM_LIMIT = 46 * 1024 * 1024


def _cp(*sem):
    return pltpu.CompilerParams(dimension_semantics=sem, vmem_limit_bytes=VMEM_LIMIT)


def _rms_fwd(x, g):
    r = lax.rsqrt(jnp.mean(x * x, axis=-1, keepdims=True) + EPS)
    xh = x * r
    return xh * g, xh, r


def _rms_bwd(dy, xh, r, g):
    dxh = dy * g
    dx = r * (dxh - xh * jnp.mean(dxh * xh, axis=-1, keepdims=True))
    return dx, dy * xh


def _rows8(a):
    m, n = a.shape
    return a.reshape(m // SUBLANES, SUBLANES, n).sum(axis=0)


_GELU_K = math.sqrt(2.0 / math.pi)
_GELU_C = 0.044715


def _gelu(x):
    th = jnp.tanh(_GELU_K * (x + _GELU_C * x * x * x))
    return 0.5 * x * (1.0 + th), th


def _gelu_grad(x, th):
    return 0.5 * (1.0 + th) + 0.5 * x * (1.0 - th * th) * (_GELU_K * (1.0 + 3.0 * _GELU_C * x * x))


def _dot(a, b):
    return jnp.dot(a, b, preferred_element_type=F32)


def _dot_nt(a, b):
    return lax.dot_general(a, b, (((1,), (1,)), ((), ())), preferred_element_type=F32)


def _dot_tn(a, b):
    return lax.dot_general(a, b, (((0,), (0,)), ((), ())), preferred_element_type=F32)


def _ffn_fwd(x, gain, w_in_ag, w_out_ag, tm, name):
    n = x.shape[0]

    def body(x_ref, g_ref, wg_ref, wu_ref, wo_ref, o_ref, xn_ref, gu_ref):
        j = pl.program_id(1)

        @pl.when(j == 0)
        def _():
            xv = x_ref[...]
            y, _, _ = _rms_fwd(xv, g_ref[...])
            xn_ref[...] = y.astype(BF16)
            o_ref[...] = xv

        xn = xn_ref[...]
        wo = wo_ref[...].reshape(FF_SHARD, D_MODEL)
        out = None
        for lo, hi in FF_PIECES:
            gg = _dot(xn, wg_ref[:, lo:hi])
            uu = _dot(xn, wu_ref[:, lo:hi])
            gu_ref[0, :, lo:hi] = gg.astype(BF16)
            gu_ref[1, :, lo:hi] = uu.astype(BF16)
            act = (gg * jax.nn.sigmoid(gg) * uu).astype(BF16)
            part = _dot(act, wo[lo:hi, :])
            out = part if out is None else out + part
        o_ref[...] += 0.5 * out

    return pl.pallas_call(
        body, name=name, grid=(n // tm, FF_CHUNKS),
        in_specs=[
            pl.BlockSpec((tm, D_MODEL), lambda i, j: (i, 0)),
            pl.BlockSpec((1, D_MODEL), lambda i, j: (0, 0)),
            pl.BlockSpec((None, D_MODEL, FF_SHARD), lambda i, j: (j, 0, 0)),
            pl.BlockSpec((None, D_MODEL, FF_SHARD), lambda i, j: (j + FF_CHUNKS, 0, 0)),
            pl.BlockSpec((2, FF_SHARD // 2, D_MODEL), lambda i, j: (j, 0, 0)),
        ],
        out_specs=[
            pl.BlockSpec((tm, D_MODEL), lambda i, j: (i, 0)),
            pl.BlockSpec((tm, D_MODEL), lambda i, j: (i, 0)),
            pl.BlockSpec((None, 2, tm, FF_SHARD), lambda i, j: (j, 0, i, 0)),
        ],
        out_shape=[
            jax.ShapeDtypeStruct((n, D_MODEL), F32),
            jax.ShapeDtypeStruct((n, D_MODEL), BF16),
            jax.ShapeDtypeStruct((FF_CHUNKS, 2, n, FF_SHARD), BF16),
        ],
        compiler_params=_cp("parallel", "arbitrary"),
    )(x, gain, w_in_ag, w_in_ag, w_out_ag)


def _ffn_bwd(x, gain, dy, gu, w_in_ag, w_out_ag, tm, name):
    n = x.shape[0]
    work = (n // tm) * FF_CHUNKS
    steps = work + 1
    first = lambda s: jnp.minimum(s, work - 1)
    second = lambda s: jnp.maximum(s - 1, 0)
    tile1, chunk1 = (lambda s: first(s) // FF_CHUNKS), (lambda s: first(s) % FF_CHUNKS)
    tile2, chunk2 = (lambda s: second(s) // FF_CHUNKS), (lambda s: second(s) % FF_CHUNKS)

    def body(x_ref, g_ref, dy_ref, dy1_ref, gu_ref, wg_ref, wu_ref, wo_ref, dx_ref, dgu_ref, act_ref, dgain_ref,
             dyb_ref, held_ref):
        s = pl.program_id(0)

        @pl.when(s == 0)
        def _():
            dgain_ref[...] = jnp.zeros_like(dgain_ref)
            held_ref[...] = jnp.zeros_like(held_ref)

        @pl.when(jnp.logical_and(chunk1(s) == 0, s < work))
        def _():
            dyb_ref[...] = (0.5 * dy1_ref[...]).astype(BF16)

        @pl.when(chunk2(s) == 0)
        def _():
            dx_ref[...] = jnp.zeros_like(dx_ref)

        held = held_ref[1 - s % 2]
        part = _dot_nt(held[0], wg_ref[...]) + _dot_nt(held[1], wu_ref[...])
        dx_ref[...] += jnp.where(s > 0, part, 0.0)

        dyb = dyb_ref[...]
        wo = wo_ref[...].reshape(FF_SHARD, D_MODEL)
        slot = s % 2
        for lo, hi in FF_PIECES:
            gg = gu_ref[0, :, lo:hi].astype(F32)
            uu = gu_ref[1, :, lo:hi].astype(F32)
            dact = _dot_nt(dyb, wo[lo:hi, :])
            sig = jax.nn.sigmoid(gg)
            silu = gg * sig
            act_ref[:, lo:hi] = (silu * uu).astype(BF16)
            du = (dact * silu).astype(BF16)
            dg = (dact * uu * (sig * (1.0 + gg * (1.0 - sig)))).astype(BF16)
            dgu_ref[0, :, lo:hi] = dg
            dgu_ref[1, :, lo:hi] = du
            held_ref[slot, 0, :, lo:hi] = dg
            held_ref[slot, 1, :, lo:hi] = du

        @pl.when(jnp.logical_and(chunk2(s) == FF_CHUNKS - 1, s > 0))
        def _():
            g = g_ref[...]
            _, xh, r = _rms_fwd(x_ref[...], g)
            dx, dgr = _rms_bwd(dx_ref[...], xh, r, g)
            dx_ref[...] = dy_ref[...] + dx
            dgain_ref[...] += _rows8(dgr)

    return pl.pallas_call(
        body, name=name, grid=(steps,),
        in_specs=[
            pl.BlockSpec((tm, D_MODEL), lambda s: (tile2(s), 0)),
            pl.BlockSpec((1, D_MODEL), lambda s: (0, 0)),
            pl.BlockSpec((tm, D_MODEL), lambda s: (tile2(s), 0)),
            pl.BlockSpec((tm, D_MODEL), lambda s: (tile1(s), 0)),
            pl.BlockSpec((None, 2, tm, FF_SHARD), lambda s: (chunk1(s), 0, tile1(s), 0)),
            pl.BlockSpec((None, D_MODEL, FF_SHARD), lambda s: (chunk2(s), 0, 0)),
            pl.BlockSpec((None, D_MODEL, FF_SHARD), lambda s: (chunk2(s) + FF_CHUNKS, 0, 0)),
            pl.BlockSpec((2, FF_SHARD // 2, D_MODEL), lambda s: (chunk1(s), 0, 0)),
        ],
        out_specs=[
            pl.BlockSpec((tm, D_MODEL), lambda s: (tile2(s), 0)),
            pl.BlockSpec((None, 2, tm, FF_SHARD), lambda s: (chunk1(s), 0, tile1(s), 0)),
            pl.BlockSpec((None, tm, FF_SHARD), lambda s: (chunk1(s), tile1(s), 0)),
            pl.BlockSpec((SUBLANES, D_MODEL), lambda s: (0, 0)),
            pl.BlockSpec((tm, D_MODEL), lambda s: (tile1(s), 0)),
        ],
        out_shape=[
            jax.ShapeDtypeStruct((n, D_MODEL), F32),
            jax.ShapeDtypeStruct((FF_CHUNKS, 2, n, FF_SHARD), BF16),
            jax.ShapeDtypeStruct((FF_CHUNKS, n, FF_SHARD), BF16),
            jax.ShapeDtypeStruct((SUBLANES, D_MODEL), F32),
            jax.ShapeDtypeStruct((n, D_MODEL), BF16),
        ],
        scratch_shapes=[pltpu.VMEM((2, 2, tm, FF_SHARD), BF16)],
        compiler_params=_cp("arbitrary"),
    )(x, gain, dy, dy, gu, w_in_ag, w_in_ag, w_out_ag)


def _ffn_dw_in(xn, dgu, tk, name, after=None):
    n = xn.shape[0]
    nk = n // tk
    deps = [] if after is None else [after]

    def body(a_ref, b_ref, *rest):
        o_ref, acc_ref = rest[-2:]
        k = pl.program_id(2)

        @pl.when(k == 0)
        def _():
            acc_ref[...] = jnp.zeros_like(acc_ref)

        acc_ref[...] += _dot_tn(b_ref[...], a_ref[...])

        @pl.when(k == nk - 1)
        def _():
            o_ref[...] = acc_ref[...].astype(BF16)

    return pl.pallas_call(
        body, name=name, grid=(FF_CHUNKS, 2, nk),
        in_specs=[
            pl.BlockSpec((tk, D_MODEL), lambda j, p, k: (k, 0)),
            pl.BlockSpec((None, None, tk, FF_SHARD), lambda j, p, k: (j, p, k, 0)),
        ] + [pl.BlockSpec(memory_space=pl.ANY)] * len(deps),
        out_specs=pl.BlockSpec((None, FF_SHARD, D_MODEL), lambda j, p, k: (FF_CHUNKS * p + j, 0, 0)),
        out_shape=jax.ShapeDtypeStruct((N_DEV, FF_SHARD, D_MODEL), BF16),
        scratch_shapes=[pltpu.VMEM((FF_SHARD, D_MODEL), F32)],
        compiler_params=_cp("parallel", "parallel", "arbitrary"),
    )(xn, dgu, *deps)


def _ffn_dw_out(act, dyb, tk, name, after=None):
    n = act.shape[1]
    nk = n // tk
    deps = [] if after is None else [after]

    def body(a_ref, b_ref, *rest):
        o_ref, acc_ref = rest[-2:]
        k = pl.program_id(1)

        @pl.when(k == 0)
        def _():
            acc_ref[...] = jnp.zeros_like(acc_ref)

        acc_ref[...] += _dot_tn(a_ref[...], b_ref[...])

        @pl.when(k == nk - 1)
        def _():
            o_ref[...] = acc_ref[...].astype(BF16)

    return pl.pallas_call(
        body, name=name, grid=(FF_CHUNKS, nk),
        in_specs=[
            pl.BlockSpec((None, tk, FF_SHARD), lambda j, k: (j, k, 0)),
            pl.BlockSpec((tk, D_MODEL), lambda j, k: (k, 0)),
        ] + [pl.BlockSpec(memory_space=pl.ANY)] * len(deps),
        out_specs=pl.BlockSpec((None, FF_SHARD, D_MODEL), lambda j, k: (j, 0, 0)),
        out_shape=jax.ShapeDtypeStruct((FF_CHUNKS, FF_SHARD, D_MODEL), BF16),
        scratch_shapes=[pltpu.VMEM((FF_SHARD, D_MODEL), F32)],
        compiler_params=_cp("parallel", "arbitrary"),
    )(act, dyb, *deps)


def _mix_in_fwd(x, gain, w, tm, name):
    n = x.shape[0]

    def body(x_ref, g_ref, w_ref, z_ref):
        y, _, _ = _rms_fwd(x_ref[...], g_ref[...])
        z_ref[...] = _dot(y.astype(BF16), w_ref[...])

    return pl.pallas_call(
        body, name=name, grid=(n // tm,),
        in_specs=[
            pl.BlockSpec((tm, D_MODEL), lambda i: (i, 0)),
            pl.BlockSpec((1, D_MODEL), lambda i: (0, 0)),
            pl.BlockSpec((D_MODEL, IN_COLS), lambda i: (0, 0)),
        ],
        out_specs=pl.BlockSpec((tm, IN_COLS), lambda i: (i, 0)),
        out_shape=jax.ShapeDtypeStruct((n, IN_COLS), F32),
        compiler_params=_cp("parallel"),
    )(x, gain, w)


def _mix_in_bwd(x, gain, du_ssm, du_gm, dv_gm, d_res, w, tm, name):
    n = x.shape[0]

    def body(x_ref, g_ref, d0_ref, d1_ref, d2_ref, dres_ref, w_ref, dx_ref, dw_ref, dgain_ref):
        i = pl.program_id(0)

        @pl.when(i == 0)
        def _():
            dw_ref[...] = jnp.zeros_like(dw_ref)
            dgain_ref[...] = jnp.zeros_like(dgain_ref)

        g = g_ref[...]
        y, xh, r = _rms_fwd(x_ref[...], g)
        xn = y.astype(BF16)
        dxn = jnp.zeros((tm, D_MODEL), F32)
        for k, d_ref in enumerate((d0_ref, d1_ref, d2_ref)):
            dz = d_ref[...].astype(BF16)
            cols = slice(k * SSM_WIDTH, (k + 1) * SSM_WIDTH)
            dxn += _dot_nt(dz, w_ref[:, cols])
            dw_ref[cols, :] += _dot_tn(dz, xn)
        dx, dgr = _rms_bwd(dxn, xh, r, g)
        dx_ref[...] = dres_ref[...] + dx
        dgain_ref[...] += _rows8(dgr)

    row = lambda i: (i, 0)
    fixed = lambda i: (0, 0)
    return pl.pallas_call(
        body, name=name, grid=(n // tm,),
        in_specs=[
            pl.BlockSpec((tm, D_MODEL), row),
            pl.BlockSpec((1, D_MODEL), fixed),
            pl.BlockSpec((tm, SSM_WIDTH), row),
            pl.BlockSpec((tm, GM_WIDTH), row),
            pl.BlockSpec((tm, GM_WIDTH), row),
            pl.BlockSpec((tm, D_MODEL), row),
            pl.BlockSpec((D_MODEL, IN_COLS), fixed),
        ],
        out_specs=[
            pl.BlockSpec((tm, D_MODEL), row),
            pl.BlockSpec((IN_COLS, D_MODEL), fixed),
            pl.BlockSpec((SUBLANES, D_MODEL), fixed),
        ],
        out_shape=[
            jax.ShapeDtypeStruct((n, D_MODEL), F32),
            jax.ShapeDtypeStruct((IN_COLS, D_MODEL), F32),
            jax.ShapeDtypeStruct((SUBLANES, D_MODEL), F32),
        ],
        compiler_params=_cp("arbitrary"),
    )(x, gain, du_ssm, du_gm, dv_gm, d_res, w)


def _mix_out_fwd(y_ssm, y_gm, g_ssm, g_gm, w, x, tm, name):
    n = x.shape[0]

    def body(ys_ref, yg_ref, gs_ref, gg_ref, w_ref, x_ref, o_ref):
        a, _, _ = _rms_fwd(ys_ref[...], gs_ref[...])
        b, _, _ = _rms_fwd(yg_ref[...], gg_ref[...])
        o_ref[...] = (x_ref[...] + _dot(a.astype(BF16), w_ref[0:SSM_WIDTH, :])
                      + _dot(b.astype(BF16), w_ref[SSM_WIDTH:D_MODEL, :]))

    row = lambda i: (i, 0)
    fixed = lambda i: (0, 0)
    return pl.pallas_call(
        body, name=name, grid=(n // tm,),
        in_specs=[
            pl.BlockSpec((tm, SSM_WIDTH), row), pl.BlockSpec((tm, GM_WIDTH), row),
            pl.BlockSpec((1, SSM_WIDTH), fixed), pl.BlockSpec((1, GM_WIDTH), fixed),
            pl.BlockSpec((D_MODEL, D_MODEL), fixed), pl.BlockSpec((tm, D_MODEL), row),
        ],
        out_specs=pl.BlockSpec((tm, D_MODEL), row),
        out_shape=jax.ShapeDtypeStruct((n, D_MODEL), F32),
        compiler_params=_cp("parallel"),
    )(y_ssm, y_gm, g_ssm, g_gm, w, x)


def _mix_out_bwd(y_ssm, y_gm, g_ssm, g_gm, w, dx, tm, name):
    n = dx.shape[0]

    def body(ys_ref, yg_ref, gs_ref, gg_ref, w_ref, dx_ref, dys_ref, dyg_ref, dw_ref, dgs_ref, dgg_ref):
        i = pl.program_id(0)

        @pl.when(i == 0)
        def _():
            dw_ref[...] = jnp.zeros_like(dw_ref)
            dgs_ref[...] = jnp.zeros_like(dgs_ref)
            dgg_ref[...] = jnp.zeros_like(dgg_ref)

        dxb = dx_ref[...].astype(BF16)
        parts = ((ys_ref, gs_ref, dys_ref, dgs_ref, 0), (yg_ref, gg_ref, dyg_ref, dgg_ref, SSM_WIDTH))
        for y_ref, g_ref, dy_ref, dg_ref, off in parts:
            g = g_ref[...]
            yn, xh, r = _rms_fwd(y_ref[...], g)
            rows = slice(off, off + SSM_WIDTH)
            dyn = _dot_nt(dxb, w_ref[rows, :])
            dw_ref[rows, :] += _dot_tn(yn.astype(BF16), dxb)
            dy, dgr = _rms_bwd(dyn, xh, r, g)
            dy_ref[...] = dy
            dg_ref[...] += _rows8(dgr)

    row = lambda i: (i, 0)
    fixed = lambda i: (0, 0)
    return pl.pallas_call(
        body, name=name, grid=(n // tm,),
        in_specs=[
            pl.BlockSpec((tm, SSM_WIDTH), row), pl.BlockSpec((tm, GM_WIDTH), row),
            pl.BlockSpec((1, SSM_WIDTH), fixed), pl.BlockSpec((1, GM_WIDTH), fixed),
            pl.BlockSpec((D_MODEL, D_MODEL), fixed), pl.BlockSpec((tm, D_MODEL), row),
        ],
        out_specs=[
            pl.BlockSpec((tm, SSM_WIDTH), row), pl.BlockSpec((tm, GM_WIDTH), row),
            pl.BlockSpec((D_MODEL, D_MODEL), fixed),
            pl.BlockSpec((SUBLANES, SSM_WIDTH), fixed), pl.BlockSpec((SUBLANES, GM_WIDTH), fixed),
        ],
        out_shape=[
            jax.ShapeDtypeStruct((n, SSM_WIDTH), F32), jax.ShapeDtypeStruct((n, GM_WIDTH), F32),
            jax.ShapeDtypeStruct((D_MODEL, D_MODEL), F32),
            jax.ShapeDtypeStruct((SUBLANES, SSM_WIDTH), F32), jax.ShapeDtypeStruct((SUBLANES, GM_WIDTH), F32),
        ],
        compiler_params=_cp("arbitrary"),
    )(y_ssm, y_gm, g_ssm, g_gm, w, dx)


SCAN_W = 512
SCAN_PIECES = HALF_ST // SCAN_W


def _scan_tiles(src_ref, dst_ref, dst_off, c_ref, half, carry_ref, n_tiles, reverse, extra=None):
    shifts = (1, 2, 4)
    carry_row = 0 if reverse else SUBLANES - 1

    def cols(piece, im):
        lo = im * HALF_ST + piece * SCAN_W
        return slice(lo, lo + SCAN_W)

    def step(t, state):
        carries, accs = state
        k = (n_tiles - 1 - t) if reverse else t
        rows = slice(k * SUBLANES, (k + 1) * SUBLANES)
        new_carries, new_accs = [], []
        for piece in range(SCAN_PIECES):
            cr, ci = carries[piece]
            xr0 = src_ref[rows, cols(piece, 0)]
            xi0 = src_ref[rows, cols(piece, 1)]
            xr, xi = xr0, xi0
            for si, s in enumerate(shifts):
                ar = c_ref[half, si, :, cols(piece, 0)]
                ai = c_ref[half, si, :, cols(piece, 1)]
                sh = (SUBLANES - s) if reverse else s
                sr = pltpu.roll(xr, sh, 0)
                sm = pltpu.roll(xi, sh, 0)
                xr, xi = xr + (ar * sr - ai * sm), xi + (ar * sm + ai * sr)
            pr = c_ref[half, 3, :, cols(piece, 0)]
            pi = c_ref[half, 3, :, cols(piece, 1)]
            hr = xr + (pr * cr - pi * ci)
            hi = xi + (pr * ci + pi * cr)
            dst_ref[rows, pl.ds(dst_off + piece * SCAN_W, SCAN_W)] = hr
            dst_ref[rows, pl.ds(dst_off + HALF_ST + piece * SCAN_W, SCAN_W)] = hi
            new_carries.append((jnp.broadcast_to(hr[carry_row:carry_row + 1, :], (SUBLANES, SCAN_W)),
                                jnp.broadcast_to(hi[carry_row:carry_row + 1, :], (SUBLANES, SCAN_W))))
            if extra is not None:
                new_accs.append(extra(rows, piece, (xr0, xi0), (hr, hi), accs[piece]))
        return tuple(new_carries), tuple(new_accs)

    base = half * 2 * HALF_ST
    carries0 = tuple((carry_ref[:, pl.ds(base + p * SCAN_W, SCAN_W)],
                      carry_ref[:, pl.ds(base + HALF_ST + p * SCAN_W, SCAN_W)]) for p in range(SCAN_PIECES))
    zero = jnp.zeros((SUBLANES, SCAN_W), F32)
    accs0 = tuple((zero, zero) for _ in range(SCAN_PIECES)) if extra is not None else ()
    state = (carries0, accs0)
    for t in range(n_tiles):
        state = step(t, state)
    carries, accs = state
    for p in range(SCAN_PIECES):
        carry_ref[:, pl.ds(base + p * SCAN_W, SCAN_W)] = carries[p][0]
        carry_ref[:, pl.ds(base + HALF_ST + p * SCAN_W, SCAN_W)] = carries[p][1]
    return accs


def _ssm_tail(hb, u, c_ref, glu_ref, glub_ref, dskip_ref):
    ypre = u * dskip_ref[...]
    parts = []
    for half in range(2):
        parts.append(_dot(hb[half], c_ref[half]))
    ypre = ypre + jnp.concatenate(parts, axis=1)
    yg, th = _gelu(ypre)
    zz = _dot(yg.astype(BF16), glu_ref[...]) + glub_ref[...]
    z1, z2 = zz[:, :SSM_WIDTH], zz[:, SSM_WIDTH:]
    sg = jax.nn.sigmoid(z2)
    return ypre, th, yg, z1, sg


def _ssm_fwd(z, bblk, cblk, glu, glub, dskip, fwdc, batch, t_chunk, name):
    n = z.shape[0]
    nk = n // batch // t_chunk
    n_tiles = t_chunk // SUBLANES

    def body(u_ref, b_ref, c_ref, glu_ref, glub_ref, dskip_ref, k_ref, y_ref, h_ref, bu_ref, carry_ref):
        @pl.when(pl.program_id(1) == 0)
        def _():
            carry_ref[...] = jnp.zeros_like(carry_ref)

        u = u_ref[...]
        ub = u.astype(BF16)
        for half in range(2):
            bu_ref[half] = _dot(ub[:, half * HALF_IN:(half + 1) * HALF_IN], b_ref[half])
            _scan_tiles(bu_ref.at[half], h_ref, half * 2 * HALF_ST, k_ref, half, carry_ref, n_tiles, False)
        hb = [h_ref[:, half * 2 * HALF_ST:(half + 1) * 2 * HALF_ST].astype(BF16) for half in range(2)]
        _, _, _, z1, sg = _ssm_tail(hb, u, c_ref, glu_ref, glub_ref, dskip_ref)
        y_ref[...] = z1 * sg

    fixed2 = lambda b, k: (0, 0)
    fixed3 = lambda b, k: (0, 0, 0)
    row = lambda b, k: (b * nk + k, 0)
    return pl.pallas_call(
        body, name=name, grid=(batch, nk),
        in_specs=[
            pl.BlockSpec((t_chunk, SSM_WIDTH), row),
            pl.BlockSpec((2, HALF_IN, 2 * HALF_ST), fixed3),
            pl.BlockSpec((2, 2 * HALF_ST, HALF_IN), fixed3),
            pl.BlockSpec((SSM_WIDTH, 2 * SSM_WIDTH), fixed2),
            pl.BlockSpec((1, 2 * SSM_WIDTH), fixed2),
            pl.BlockSpec((1, SSM_WIDTH), fixed2),
            pl.BlockSpec((2, 4, SUBLANES, 2 * HALF_ST), lambda b, k: (0, 0, 0, 0)),
        ],
        out_specs=[pl.BlockSpec((t_chunk, SSM_WIDTH), row), pl.BlockSpec((t_chunk, 4 * HALF_ST), row)],
        out_shape=[jax.ShapeDtypeStruct((n, SSM_WIDTH), F32), jax.ShapeDtypeStruct((n, 4 * HALF_ST), F32)],
        scratch_shapes=[pltpu.VMEM((2, t_chunk, 2 * HALF_ST), F32), pltpu.VMEM((SUBLANES, 4 * HALF_ST), F32)],
        compiler_params=_cp("parallel", "arbitrary"),
    )(z, bblk, cblk, glu, glub, dskip, fwdc)


def _ssm_bwd(z, h, dy, bblk, cblk, glu, glub, dskip, revc, batch, t_chunk, name):
    n = z.shape[0]
    nk = n // batch // t_chunk
    n_tiles = t_chunk // SUBLANES

    def body(u_ref, h_ref, dy_ref, b_ref, c_ref, glu_ref, glub_ref, dskip_ref, k_ref,
             du_ref, dglu_ref, dglub_ref, ddskip_ref, dct_ref, db_ref, q_ref, g_ref, carry_ref):
        first = jnp.logical_and(pl.program_id(0) == 0, pl.program_id(1) == 0)

        @pl.when(first)
        def _():
            for r in (dglu_ref, dglub_ref, ddskip_ref, dct_ref, db_ref, q_ref):
                r[...] = jnp.zeros_like(r)

        @pl.when(pl.program_id(1) == 0)
        def _():
            carry_ref[...] = jnp.zeros_like(carry_ref)

        u = u_ref[...]
        ub = u.astype(BF16)
        hb = [h_ref[:, half * 2 * HALF_ST:(half + 1) * 2 * HALF_ST].astype(BF16) for half in range(2)]
        ypre, th, yg, z1, sg = _ssm_tail(hb, u, c_ref, glu_ref, glub_ref, dskip_ref)
        dout = dy_ref[...]
        dz = jnp.concatenate([dout * sg, dout * z1 * sg * (1.0 - sg)], axis=1)
        dzb = dz.astype(BF16)
        dglu_ref[...] += _dot_tn(yg.astype(BF16), dzb)
        dglub_ref[...] += _rows8(dz)
        dypre = _dot_nt(dzb, glu_ref[...]) * _gelu_grad(ypre, th)
        ddskip_ref[...] += _rows8(dypre * u)
        dypb = dypre.astype(BF16)
        du_parts = []
        for half in range(2):
            dyp_h = dypb[:, half * HALF_IN:(half + 1) * HALF_IN]
            dct_ref[half] += _dot_tn(dyp_h, hb[half])
            g_ref[half] = _dot_nt(dyp_h, c_ref[half])

            def extra(rows, piece, x_in, g_out, acc, half=half):
                er, ei = g_out[0] - x_in[0], g_out[1] - x_in[1]
                base = half * 2 * HALF_ST + piece * SCAN_W
                hr = h_ref[rows, pl.ds(base, SCAN_W)]
                hi = h_ref[rows, pl.ds(base + HALF_ST, SCAN_W)]
                return acc[0] + (er * hr + ei * hi), acc[1] + (er * hi - ei * hr)

            accs = _scan_tiles(g_ref.at[half], g_ref.at[half], 0, k_ref, half, carry_ref, n_tiles, True, extra)
            for piece in range(SCAN_PIECES):
                base = half * 2 * HALF_ST + piece * SCAN_W
                q_ref[:, pl.ds(base, SCAN_W)] += accs[piece][0]
                q_ref[:, pl.ds(base + HALF_ST, SCAN_W)] += accs[piece][1]
            gb = g_ref[half].astype(BF16)
            db_ref[half] += _dot_tn(ub[:, half * HALF_IN:(half + 1) * HALF_IN], gb)
            du_parts.append(_dot_nt(gb, b_ref[half]))
        du_ref[...] = dypre * dskip_ref[...] + jnp.concatenate(du_parts, axis=1)

    fixed2 = lambda b, k: (0, 0)
    fixed3 = lambda b, k: (0, 0, 0)
    row = lambda b, k: (b * nk + (nk - 1 - k), 0)
    return pl.pallas_call(
        body, name=name, grid=(batch, nk),
        in_specs=[
            pl.BlockSpec((t_chunk, SSM_WIDTH), row),
            pl.BlockSpec((t_chunk, 4 * HALF_ST), row),
            pl.BlockSpec((t_chunk, SSM_WIDTH), row),
            pl.BlockSpec((2, HALF_IN, 2 * HALF_ST), fixed3),
            pl.BlockSpec((2, 2 * HALF_ST, HALF_IN), fixed3),
            pl.BlockSpec((SSM_WIDTH, 2 * SSM_WIDTH), fixed2),
            pl.BlockSpec((1, 2 * SSM_WIDTH), fixed2),
            pl.BlockSpec((1, SSM_WIDTH), fixed2),
            pl.BlockSpec((2, 4, SUBLANES, 2 * HALF_ST), lambda b, k: (0, 0, 0, 0)),
        ],
        out_specs=[
            pl.BlockSpec((t_chunk, SSM_WIDTH), row),
            pl.BlockSpec((SSM_WIDTH, 2 * SSM_WIDTH), fixed2),
            pl.BlockSpec((SUBLANES, 2 * SSM_WIDTH), fixed2),
            pl.BlockSpec((SUBLANES, SSM_WIDTH), fixed2),
            pl.BlockSpec((2, HALF_IN, 2 * HALF_ST), fixed3),
            pl.BlockSpec((2, HALF_IN, 2 * HALF_ST), fixed3),
            pl.BlockSpec((SUBLANES, 4 * HALF_ST), fixed2),
        ],
        out_shape=[
            jax.ShapeDtypeStruct((n, SSM_WIDTH), F32),
            jax.ShapeDtypeStruct((SSM_WIDTH, 2 * SSM_WIDTH), F32),
            jax.ShapeDtypeStruct((SUBLANES, 2 * SSM_WIDTH), F32),
            jax.ShapeDtypeStruct((SUBLANES, SSM_WIDTH), F32),
            jax.ShapeDtypeStruct((2, HALF_IN, 2 * HALF_ST), F32),
            jax.ShapeDtypeStruct((2, HALF_IN, 2 * HALF_ST), F32),
            jax.ShapeDtypeStruct((SUBLANES, 4 * HALF_ST), F32),
        ],
        scratch_shapes=[pltpu.VMEM((2, t_chunk, 2 * HALF_ST), F32), pltpu.VMEM((SUBLANES, 4 * HALF_ST), F32)],
        compiler_params=_cp("arbitrary", "arbitrary"),
    )(z, h, dy, bblk, cblk, glu, glub, dskip, revc)


def _gm_chunk_fwd(u, v, gain_ref, w_ref, bias_ref):
    ug, thu = _gelu(u)
    vg, thv = _gelu(v)
    rs, vns, ss = [], [], []
    for hh in range(GM_HEADS):
        cs = slice(hh * GM_HEAD_DIM, (hh + 1) * GM_HEAD_DIM)
        vn, _, r = _rms_fwd(vg[:, cs], gain_ref[:, cs])
        s = _dot(w_ref[hh], vn.astype(BF16)) + bias_ref[:, cs]
        rs.append(r)
        vns.append(vn)
        ss.append(s)
    return ug, thu, thv, vg, rs, vns, ss


def _gm_fwd(z, gain, w_tril, bias, rows, name):
    n = z.shape[0]
    chunks = rows // GM_CHUNK

    def body(u_ref, v_ref, gain_ref, w_ref, bias_ref, y_ref):
        for c in range(chunks):
            rs_ = slice(c * GM_CHUNK, (c + 1) * GM_CHUNK)
            ug, _, _, _, _, _, ss = _gm_chunk_fwd(u_ref[rs_, :], v_ref[rs_, :], gain_ref, w_ref, bias_ref)
            y_ref[rs_, :] = ug * jnp.concatenate(ss, axis=1)

    return pl.pallas_call(
        body, name=name, grid=(n // rows,),
        in_specs=[
            pl.BlockSpec((rows, GM_WIDTH), lambda i: (i, 1)),
            pl.BlockSpec((rows, GM_WIDTH), lambda i: (i, 2)),
            pl.BlockSpec((1, GM_WIDTH), lambda i: (0, 0)),
            pl.BlockSpec((GM_HEADS, GM_CHUNK, GM_CHUNK), lambda i: (0, 0, 0)),
            pl.BlockSpec((GM_CHUNK, GM_WIDTH), lambda i: (0, 0)),
        ],
        out_specs=pl.BlockSpec((rows, GM_WIDTH), lambda i: (i, 0)),
        out_shape=jax.ShapeDtypeStruct((n, GM_WIDTH), F32),
        compiler_params=_cp("parallel"),
    )(z, z, gain, w_tril, bias)


def _gm_bwd(z, dy, gain, w_tril, bias, rows, name):
    n = z.shape[0]
    chunks = rows // GM_CHUNK

    def body(u_ref, v_ref, dy_ref, gain_ref, w_ref, bias_ref, du_ref, dv_ref, dw_ref, dbias_ref, dgain_ref):
        @pl.when(pl.program_id(0) == 0)
        def _():
            dw_ref[...] = jnp.zeros_like(dw_ref)
            dbias_ref[...] = jnp.zeros_like(dbias_ref)
            dgain_ref[...] = jnp.zeros_like(dgain_ref)

        for c in range(chunks):
            rs_ = slice(c * GM_CHUNK, (c + 1) * GM_CHUNK)
            u, v = u_ref[rs_, :], v_ref[rs_, :]
            ug, thu, thv, vg, rs, vns, ss = _gm_chunk_fwd(u, v, gain_ref, w_ref, bias_ref)
            dout = dy_ref[rs_, :]
            ds = dout * ug
            du_ref[rs_, :] = dout * jnp.concatenate(ss, axis=1) * _gelu_grad(u, thu)
            dbias_ref[...] += ds
            dvg_parts, dgain_parts = [], []
            for hh in range(GM_HEADS):
                cs = slice(hh * GM_HEAD_DIM, (hh + 1) * GM_HEAD_DIM)
                dsb = ds[:, cs].astype(BF16)
                dvn = _dot_tn(w_ref[hh], dsb)
                dw_ref[hh] += _dot_nt(dsb, vns[hh].astype(BF16))
                g = gain_ref[:, cs]
                xh = vg[:, cs] * rs[hh]
                dvg, dgr = _rms_bwd(dvn, xh, rs[hh], g)
                dvg_parts.append(dvg)
                dgain_parts.append(dgr)
            dv_ref[rs_, :] = jnp.concatenate(dvg_parts, axis=1) * _gelu_grad(v, thv)
            dgain_ref[...] += _rows8(jnp.concatenate(dgain_parts, axis=1))

    row = lambda i: (i, 0)
    return pl.pallas_call(
        body, name=name, grid=(n // rows,),
        in_specs=[
            pl.BlockSpec((rows, GM_WIDTH), lambda i: (i, 1)),
            pl.BlockSpec((rows, GM_WIDTH), lambda i: (i, 2)),
            pl.BlockSpec((rows, GM_WIDTH), row),
            pl.BlockSpec((1, GM_WIDTH), lambda i: (0, 0)),
            pl.BlockSpec((GM_HEADS, GM_CHUNK, GM_CHUNK), lambda i: (0, 0, 0)),
            pl.BlockSpec((GM_CHUNK, GM_WIDTH), lambda i: (0, 0)),
        ],
        out_specs=[
            pl.BlockSpec((rows, GM_WIDTH), row), pl.BlockSpec((rows, GM_WIDTH), row),
            pl.BlockSpec((GM_HEADS, GM_CHUNK, GM_CHUNK), lambda i: (0, 0, 0)),
            pl.BlockSpec((GM_CHUNK, GM_WIDTH), lambda i: (0, 0)),
            pl.BlockSpec((SUBLANES, GM_WIDTH), lambda i: (0, 0)),
        ],
        out_shape=[
            jax.ShapeDtypeStruct((n, GM_WIDTH), F32), jax.ShapeDtypeStruct((n, GM_WIDTH), F32),
            jax.ShapeDtypeStruct((GM_HEADS, GM_CHUNK, GM_CHUNK), F32),
            jax.ShapeDtypeStruct((GM_CHUNK, GM_WIDTH), F32),
            jax.ShapeDtypeStruct((SUBLANES, GM_WIDTH), F32),
        ],
        compiler_params=_cp("arbitrary"),
    )(z, z, dy, gain, w_tril, bias)


def _loss_head(x, gain, target, tm, name):
    n = x.shape[0]

    def body(x_ref, g_ref, t_ref, dx_ref, sq_ref, dgain_ref):
        @pl.when(pl.program_id(0) == 0)
        def _():
            sq_ref[...] = jnp.zeros_like(sq_ref)
            dgain_ref[...] = jnp.zeros_like(dgain_ref)

        g = g_ref[...]
        y, xh, r = _rms_fwd(x_ref[...], g)
        err = y - t_ref[...]
        sq_ref[...] += _rows8(err * err)
        dx, dgr = _rms_bwd(err * (1.0 / D_MODEL), xh, r, g)
        dx_ref[...] = dx
        dgain_ref[...] += _rows8(dgr)

    row = lambda i: (i, 0)
    fixed = lambda i: (0, 0)
    return pl.pallas_call(
        body, name=name, grid=(n // tm,),
        in_specs=[pl.BlockSpec((tm, D_MODEL), row), pl.BlockSpec((1, D_MODEL), fixed), pl.BlockSpec((tm, D_MODEL), row)],
        out_specs=[pl.BlockSpec((tm, D_MODEL), row), pl.BlockSpec((SUBLANES, D_MODEL), fixed),
                   pl.BlockSpec((SUBLANES, D_MODEL), fixed)],
        out_shape=[jax.ShapeDtypeStruct((n, D_MODEL), F32), jax.ShapeDtypeStruct((SUBLANES, D_MODEL), F32),
                   jax.ShapeDtypeStruct((SUBLANES, D_MODEL), F32)],
        compiler_params=_cp("arbitrary"),
    )(x, gain, target)


def _adam_math(w, g, m, v):
    m2 = ADAM_B1 * m + (1.0 - ADAM_B1) * g
    v2 = ADAM_B2 * v + (1.0 - ADAM_B2) * (g * g)
    m_hat = m2 / (1.0 - ADAM_B1 ** ADAM_STEP)
    v_hat = v2 / (1.0 - ADAM_B2 ** ADAM_STEP)
    delta = -ADAM_LR * (m_hat / (jnp.sqrt(v_hat) + ADAM_EPS) + ADAM_WD * w)
    return delta, m2, v2


def _adam_sharded(parts, w, m, v, layer, earlier, name):
    depth, r, c = w.shape
    tr = max(t for t in range(16, 353, 16) if r % t == 0)

    def body(p_ref, w_ref, m_ref, v_ref, *rest):
        g_ref, d_ref, m2_ref, v2_ref = rest[-4:]
        g = p_ref[0].astype(F32)
        for s in range(1, N_DEV):
            g = g + p_ref[s].astype(F32)
        delta, m2, v2 = _adam_math(w_ref[...], g, m_ref[...], v_ref[...])
        g_ref[...] = g
        d_ref[...] = delta
        m2_ref[...] = m2
        v2_ref[...] = v2

    blk = pl.BlockSpec((None, tr, c), lambda i: (layer, i, 0))
    extra = [] if earlier is None else list(earlier)
    return pl.pallas_call(
        body, name=name, grid=(r // tr,),
        in_specs=[pl.BlockSpec((N_DEV, tr, c), lambda i: (0, i, 0)), blk, blk, blk]
        + [pl.BlockSpec(memory_space=pl.ANY)] * len(extra),
        out_specs=[blk, blk, blk, blk],
        out_shape=[jax.ShapeDtypeStruct((depth, r, c), F32)] * 4,
        input_output_aliases={4 + i: i for i in range(len(extra))},
        compiler_params=_cp("parallel"),
    )(parts, w, m, v, *extra)


def _adam_packed(g, w, m, v, name):
    r, c = g.shape

    def body(g_ref, w_ref, m_ref, v_ref, d_ref, m2_ref, v2_ref):
        delta, m2, v2 = _adam_math(w_ref[...], g_ref[...], m_ref[...], v_ref[...])
        d_ref[...] = delta
        m2_ref[...] = m2
        v2_ref[...] = v2

    blk = pl.BlockSpec((r, c), lambda i: (0, 0))
    return pl.pallas_call(
        body, name=name, grid=(1,),
        in_specs=[blk, blk, blk, blk], out_specs=[blk, blk, blk],
        out_shape=[jax.ShapeDtypeStruct((r, c), F32)] * 3,
        compiler_params=_cp("arbitrary"),
    )(g, w, m, v)


def _my_place():
    return lax.axis_index("x"), lax.axis_index("y"), lax.axis_index("c")


def _flip(place, rel):
    x, y, c = place
    return (1 - x if rel & 4 else x, 1 - y if rel & 2 else y, 1 - c if rel & 1 else c)


def _index(place):
    return 4 * place[0] + 2 * place[1] + place[2]


def _all_gather(shards, name):
    na = len(shards)

    def body(*refs):
        xs, outs = refs[:na], refs[na:2 * na]
        send_sems, recv_sems, local_sems = refs[2 * na:]
        me = _my_place()
        sibling = _flip(me, 1)
        chips = [_flip(me, 4), _flip(me, 2), _flip(me, 6)]

        def copy(a, k, block, to, src=None):
            slot = outs[a].at[_index(block)]
            return pltpu.make_async_remote_copy(
                src_ref=slot if src is None else src, dst_ref=slot,
                send_sem=send_sems.at[a, k], recv_sem=recv_sems.at[a, k],
                device_id=to, device_id_type=MESH)

        mine = [pltpu.make_async_copy(xs[a], outs[a].at[_index(me)], local_sems.at[a]) for a in range(na)]
        for cp in mine:
            cp.start()
        first = []
        for a in range(na):
            first.append(copy(a, 0, me, sibling, src=xs[a]))
            first += [copy(a, 1 + j, me, chip, src=xs[a]) for j, chip in enumerate(chips)]
        for cp in first:
            cp.start()
        passed = []
        for a in range(na):
            for j, chip in enumerate(chips):
                copy(a, 1 + j, chip, me).wait_recv()
                fwd = copy(a, 4 + j, chip, sibling)
                fwd.start()
                passed.append(fwd)
        for a in range(na):
            copy(a, 0, sibling, me).wait_recv()
            for j, chip in enumerate(chips):
                copy(a, 4 + j, _flip(chip, 1), me).wait_recv()
        for cp in first + passed:
            cp.wait_send()
        for cp in mine:
            cp.wait()

    hbm = pl.BlockSpec(memory_space=pl.ANY)
    return pl.pallas_call(
        body, name=name,
        in_specs=[hbm] * na, out_specs=[hbm] * na,
        out_shape=[jax.ShapeDtypeStruct((N_DEV,) + s.shape, s.dtype) for s in shards],
        scratch_shapes=[pltpu.SemaphoreType.DMA((na, 7)), pltpu.SemaphoreType.DMA((na, 7)),
                        pltpu.SemaphoreType.DMA((na,))],
    )(*shards)


_HBM = pl.BlockSpec(memory_space=pltpu.HBM)
_SEM = pl.BlockSpec(memory_space=pltpu.SEMAPHORE)
_EFFECT = pltpu.SideEffectType.DATAFLOW_SIDE_EFFECTING


def _exchange_copy(src_ref, land_ref, send_sems, recv_sems, a, rel, me, scatter, landed):
    peer = _flip(me, rel)
    src = src_ref.at[_index(peer)] if scatter else src_ref
    return pltpu.make_async_remote_copy(
        src_ref=src, dst_ref=land_ref.at[_index(peer if landed else me)],
        send_sem=send_sems.at[a * (N_DEV - 1) + rel - 1], recv_sem=recv_sems.at[a * (N_DEV - 1) + rel - 1],
        device_id=peer, device_id_type=MESH)


def _own_slot(data, me, scatter):
    if scatter:
        own = lax.dynamic_slice_in_dim(data, me, 1, axis=0)
        shape = data.shape
    else:
        own = data[None]
        shape = (N_DEV,) + data.shape
    start = (me,) + (0,) * (len(shape) - 1)
    return lax.dynamic_update_slice(lax.empty(shape, data.dtype), own, start)


def _exchange_start(groups, me, scatter, name, after=None):
    sizes = [len(g) for g in groups]
    srcs = [a for g in groups for a in g]
    lands = [_own_slot(a, me, scatter) for a in srcs]
    na, ng = len(srcs), len(groups)
    deps = [] if after is None else [after]

    def body(*refs):
        src_refs, land_refs = refs[:na], refs[na:2 * na]
        sems = refs[2 * na + len(deps):2 * na + len(deps) + 2 * ng]
        token = refs[-1]
        place = _my_place()
        a = 0
        for g, size in enumerate(sizes):
            for k in range(size):
                for rel in range(1, N_DEV):
                    _exchange_copy(src_refs[a], land_refs[a], sems[2 * g], sems[2 * g + 1], k, rel, place, scatter,
                                   False).start()
                a += 1
        token[...] = jnp.zeros_like(token)

    sem_shapes = [pltpu.SemaphoreType.DMA((size * (N_DEV - 1),)) for size in sizes for _ in range(2)]
    outs = pl.pallas_call(
        body, name=name,
        in_specs=[_HBM] * (2 * na) + [pl.BlockSpec(memory_space=pl.ANY)] * len(deps),
        out_specs=[_SEM] * (2 * ng) + [_HBM] * (2 * na) + [pl.BlockSpec(memory_space=pltpu.VMEM)],
        out_shape=sem_shapes + [pltpu.HBM(a.shape, a.dtype) for a in srcs + lands]
        + [jax.ShapeDtypeStruct((SUBLANES, LANES), F32)],
        input_output_aliases={i: 2 * ng + i for i in range(2 * na)},
        compiler_params=pltpu.CompilerParams(has_side_effects=_EFFECT),
    )(*[pltpu.with_memory_space_constraint(a, pltpu.HBM) for a in srcs + lands], *deps)
    sems, thru, token = outs[:2 * ng], outs[2 * ng:2 * ng + 2 * na], outs[-1]
    handles, a = [], 0
    for g, size in enumerate(sizes):
        handles.append((sems[2 * g], sems[2 * g + 1], thru[a:a + size], thru[na + a:na + a + size]))
        a += size
    return handles, token


def _exchange_wait(handle, after, scatter, name):
    send_sems, recv_sems, srcs, lands = handle
    na = len(srcs)

    def body(*refs):
        src_refs, land_refs = refs[:na], refs[na:2 * na]
        send_ref, recv_ref = refs[2 * na], refs[2 * na + 1]
        place = _my_place()
        for a in range(na):
            for rel in range(1, N_DEV):
                cp = _exchange_copy(src_refs[a], land_refs[a], send_ref, recv_ref, a, rel, place, scatter, True)
                cp.wait_send()
                cp.wait_recv()

    outs = pl.pallas_call(
        body, name=name,
        in_specs=[_HBM] * (2 * na) + [_SEM, _SEM, pl.BlockSpec(memory_space=pl.ANY)],
        out_specs=[_HBM] * (2 * na),
        out_shape=[pltpu.HBM(a.shape, a.dtype) for a in list(srcs) + list(lands)],
        input_output_aliases={i: i for i in range(2 * na)},
        compiler_params=pltpu.CompilerParams(has_side_effects=_EFFECT),
    )(*srcs, *lands, send_sems, recv_sems, after)
    return outs[na:]


def _behind(arr, token):
    return arr + token[0:1, 0:1]


def _all_reduce_small(g, name):
    _, r, c = g.shape

    def body(g_ref, o_ref, land_ref, red_ref, send1, recv1, send2, recv2):
        me = _my_place()
        idx = _index(me)

        def scatter(rel):
            peer = _flip(me, rel)
            return pltpu.make_async_remote_copy(
                src_ref=g_ref.at[_index(peer)], dst_ref=land_ref.at[idx],
                send_sem=send1.at[rel - 1], recv_sem=recv1.at[rel - 1], device_id=peer, device_id_type=MESH)

        def gather(rel):
            peer = _flip(me, rel)
            return pltpu.make_async_remote_copy(
                src_ref=red_ref, dst_ref=o_ref.at[idx],
                send_sem=send2.at[rel - 1], recv_sem=recv2.at[rel - 1], device_id=peer, device_id_type=MESH)

        for rel in range(1, N_DEV):
            scatter(rel).start()
        land_ref[idx] = g_ref[idx]
        for rel in range(1, N_DEV):
            scatter(rel).wait()
        acc = land_ref[0]
        for s in range(1, N_DEV):
            acc = acc + land_ref[s]
        red_ref[...] = acc
        for rel in range(1, N_DEV):
            gather(rel).start()
        o_ref[idx] = acc
        for rel in range(1, N_DEV):
            gather(rel).wait()

    vmem = pl.BlockSpec(memory_space=pltpu.VMEM)
    return pl.pallas_call(
        body, name=name,
        in_specs=[vmem], out_specs=vmem,
        out_shape=jax.ShapeDtypeStruct(g.shape, F32),
        scratch_shapes=[pltpu.VMEM(g.shape, F32), pltpu.VMEM((r, c), F32)]
        + [pltpu.SemaphoreType.DMA((N_DEV - 1,))] * 4,
        compiler_params=pltpu.CompilerParams(vmem_limit_bytes=VMEM_LIMIT),
    )(g)


def _ssm_discretize(a_re, a_im, log_dt, b_re, b_im):
    dt = jnp.exp(log_dt)[:, None]
    mag = jnp.exp(a_re * dt)
    lr, li = mag * jnp.cos(a_im * dt), mag * jnp.sin(a_im * dt)
    den = a_re * a_re + a_im * a_im
    qr = ((lr - 1.0) * a_re + li * a_im) / den
    qi = (li * a_re - (lr - 1.0) * a_im) / den
    bbr = qr[..., None] * b_re - qi[..., None] * b_im
    bbi = qr[..., None] * b_im + qi[..., None] * b_re
    return lr, li, bbr, bbi


def _halves(a):
    return a.reshape((2, HALF_GROUPS) + a.shape[1:])


def _block_diag_mask(g, r, c):
    rows = lax.broadcasted_iota(jnp.int32, (g * r, g * c), 0) // r
    cols = lax.broadcasted_iota(jnp.int32, (g * r, g * c), 1) // c
    return rows == cols


def _block_diag(blocks):
    g, r, c = blocks.shape
    spread = jnp.tile(jnp.eye(c, dtype=blocks.dtype), (1, g))
    full = jnp.dot(blocks.reshape(g * r, c), spread, precision=lax.Precision.HIGHEST)
    return jnp.where(_block_diag_mask(g, r, c), full, 0.0)


def _block_diag_take(dense, g, r, c):
    gather = jnp.tile(jnp.eye(c, dtype=dense.dtype), (g, 1))
    kept = jnp.where(_block_diag_mask(g, r, c), dense, 0.0)
    return jnp.dot(kept, gather, precision=lax.Precision.HIGHEST).reshape(g, r, c)


def _ssm_matrices(bbr, bbi, c_re, c_im, glu_w, glu_b, d_skip):
    bre, bim = _halves(jnp.swapaxes(bbr, 1, 2)), _halves(jnp.swapaxes(bbi, 1, 2))
    bblk = jnp.stack([jnp.concatenate([_block_diag(bre[h]), _block_diag(bim[h])], axis=1) for h in range(2)])
    cre, cim = _halves(jnp.swapaxes(c_re, 1, 2)), _halves(jnp.swapaxes(c_im, 1, 2))
    cblk = jnp.stack([jnp.concatenate([_block_diag(cre[h]), -_block_diag(cim[h])], axis=0) for h in range(2)])
    glu = jnp.concatenate([_block_diag(glu_w[:, :, :SSM_CH]), _block_diag(glu_w[:, :, SSM_CH:])], axis=1)
    glub = jnp.concatenate([glu_b[:, :SSM_CH].reshape(1, -1), glu_b[:, SSM_CH:].reshape(1, -1)], axis=1)
    return bblk.astype(BF16), cblk.astype(BF16), glu.astype(BF16), glub, d_skip.reshape(1, -1)


def _scan_constants(lr, li, reverse):
    if reverse:
        li = -li
    pows = [(lr, li)]
    for _ in range(SUBLANES - 1):
        pr, pi = pows[-1]
        pows.append((pr * lr - pi * li, pr * li + pi * lr))
    row = jnp.arange(SUBLANES)[:, None]

    def flat(a):
        return a.reshape(2, 1, HALF_ST)

    mats = []
    for s in (1, 2, 4):
        keep = (row + s <= SUBLANES - 1) if reverse else (row >= s)
        mats.append(tuple(jnp.where(keep[None], flat(p), 0.0) for p in pows[s - 1]))
    order = [SUBLANES - 1 - j for j in range(SUBLANES)] if reverse else list(range(SUBLANES))
    mats.append(tuple(jnp.concatenate([flat(pows[j][k]) for j in order], axis=1) for k in range(2)))
    return jnp.stack([jnp.concatenate([m[0], m[1]], axis=2) for m in mats], axis=1)


def _pack(arrs, rows):
    flat = jnp.concatenate([a.reshape(-1) for a in arrs])
    return jnp.pad(flat, (0, rows * LANES - flat.shape[0])).reshape(rows, LANES)


def _unpack(buf, like):
    flat = buf.reshape(-1)
    out, off = [], 0
    for a in like:
        out.append(flat[off:off + a.size].reshape(a.shape))
        off += a.size
    return out


SMALL = ("norm_ffn1", "norm_mix", "ssm_a_re", "ssm_a_im", "ssm_log_dt", "ssm_b_re", "ssm_b_im", "ssm_c_re",
         "ssm_c_im", "ssm_d", "ssm_glu_w", "ssm_glu_b", "gm_v_gain", "gm_w_s", "gm_b_s", "gain_ssm_out",
         "gain_gm_out", "norm_ffn2", "norm_final")
BIG = ("ffn1_w_in", "ffn1_w_out", "mix_w_in", "mix_w_out", "ffn2_w_in", "ffn2_w_out")
TRANSPOSED = ("ffn1_w_in", "mix_w_in", "ffn2_w_in")
WEIGHTS = ("norm_ffn1", "ffn1_w_in", "ffn1_w_out", "norm_mix", "mix_w_in", "ssm_a_re", "ssm_a_im", "ssm_log_dt",
           "ssm_b_re", "ssm_b_im", "ssm_c_re", "ssm_c_im", "ssm_d", "ssm_glu_w", "ssm_glu_b", "gm_v_gain", "gm_w_s",
           "gm_b_s", "gain_ssm_out", "gain_gm_out", "mix_w_out", "norm_ffn2", "ffn2_w_in", "ffn2_w_out", "norm_final")


def _step(x, target, w, m, v):
    batch, seq, _ = x.shape
    n = batch * seq
    depth = w["norm_ffn1"].shape[0]
    tm = min(512, n)
    tm_ffn = min(1024, n)
    tk = min(4096, n)
    t_chunk = min(256, seq)
    t_chunk_fwd = min(2 * t_chunk, seq)
    gm_rows = min(1024, seq)
    gm_rows_bwd = min(512, seq)
    x = x.reshape(n, D_MODEL)
    target = target.reshape(n, D_MODEL)

    assert depth == 2
    me = _index(_my_place())
    shard = lambda group, l: [w[f"{group}_w_in"][l].astype(BF16), w[f"{group}_w_out"][l].astype(BF16)]
    batches = ([("mix", 0), ("ffn2", 0)], [("ffn1", 1), ("mix", 1)], [("ffn2", 1)])
    gathered, pending = {("ffn1", 0): tuple(_all_gather(shard("ffn1", 0), "all_gather_first"))}, {}

    def gather_start(i, after):
        handles, tok = _exchange_start([shard(g, l) for g, l in batches[i]], me, False, f"all_gather_start_{i}",
                                       after)
        pending.update(zip(batches[i], handles))
        return tok

    def weights(group, l, after=None):
        if (group, l) not in gathered:
            w_in, w_out = _exchange_wait(pending[(group, l)], after, False, f"all_gather_wait_{group}_{l}")
            if group == "mix":
                w_in = jnp.transpose(w_in, (1, 0, 2)).reshape(D_MODEL, IN_COLS)
                w_out = w_out.reshape(D_MODEL, D_MODEL)
            gathered[(group, l)] = (w_in, w_out)
        return gathered[(group, l)]

    tril = jnp.tril(jnp.ones((GM_CHUNK, GM_CHUNK), bool))
    layers = []
    for l in range(depth):
        disc, disc_vjp = jax.vjp(_ssm_discretize, w["ssm_a_re"][l], w["ssm_a_im"][l], w["ssm_log_dt"][l],
                                 w["ssm_b_re"][l], w["ssm_b_im"][l])
        lr, li, bbr, bbi = disc
        bblk, cblk, glu, glub, dskip = _ssm_matrices(bbr, bbi, w["ssm_c_re"][l], w["ssm_c_im"][l],
                                                     w["ssm_glu_w"][l], w["ssm_glu_b"][l], w["ssm_d"][l])
        layers.append(dict(
            disc_vjp=disc_vjp, lr=lr, li=li, bblk=bblk, cblk=cblk, glu=glu, glub=glub, dskip=dskip,
            fwdc=_scan_constants(lr, li, False), revc=_scan_constants(lr, li, True),
            w_tril=jnp.where(tril[None], w["gm_w_s"][l], 0.0).astype(BF16),
            gm_bias=jnp.repeat(w["gm_b_s"][l].T, GM_HEAD_DIM, axis=1),
            g1=w["norm_ffn1"][l][None], gmix=w["norm_mix"][l][None], g2=w["norm_ffn2"][l][None],
            gv=w["gm_v_gain"][l][None], gs=w["gain_ssm_out"][l][None], gg=w["gain_gm_out"][l][None],
        ))

    saved = []
    for l in range(depth):
        p = layers[l]
        x0 = x
        g1, gmix, g2 = p["g1"], p["gmix"], p["g2"]
        w_in, w_out = weights("ffn1", l, x0)
        if l == 0:
            g1 = _behind(g1, gather_start(0, w_in))
        x1, xn1, gu1 = _ffn_fwd(x0, g1, w_in, w_out, tm_ffn, f"ffn1_fwd_{l}")
        if l == 0:
            gmix = _behind(gmix, gather_start(1, x1))
        mwi, mwo = weights("mix", l, x1)
        z = _mix_in_fwd(x1, gmix, mwi, tm_ffn, f"mix_in_fwd_{l}")
        y_ssm, h = _ssm_fwd(z, p["bblk"], p["cblk"], p["glu"], p["glub"], p["dskip"], p["fwdc"], batch, t_chunk_fwd,
                            f"ssm_fwd_{l}")
        y_gm = _gm_fwd(z, p["gv"], p["w_tril"], p["gm_bias"], gm_rows, f"gm_fwd_{l}")
        x2 = _mix_out_fwd(y_ssm, y_gm, p["gs"], p["gg"], mwo, x1, tm_ffn, f"mix_out_fwd_{l}")
        if l == 0:
            g2 = _behind(g2, gather_start(2, x2))
        x, xn2, gu2 = _ffn_fwd(x2, g2, *weights("ffn2", l, x2), tm_ffn, f"ffn2_fwd_{l}")
        saved.append((x0, x1, x2, z, h, y_ssm, y_gm, xn1, gu1, xn2, gu2))

    dx, sq, dnf = _loss_head(x, w["norm_final"][None], target, tm_ffn, "loss_head")
    loss = lax.psum((0.5 / D_MODEL) * jnp.sum(sq), AXES)

    small = {k: [None] * depth for k in SMALL if k != "norm_final"}
    sent = []

    def send(group, l, keys, parts):
        (handle,), tok = _exchange_start([parts], me, True, f"reduce_scatter_start_{group}_{l}")
        sent.append((group, l, keys, handle))
        return tok

    token = None
    for l in reversed(range(depth)):
        p = layers[l]
        x0, x1, x2, z, h, y_ssm, y_gm, xn1, gu1, xn2, gu2 = saved[l]
        mwi, mwo = weights("mix", l)
        dx_out = dx
        g2 = p["g2"] if token is None else _behind(p["g2"], token)
        dx, dgu, act, dgain, dyb = _ffn_bwd(x2, g2, dx_out, gu2, *weights("ffn2", l), tm, f"ffn2_bwd_{l}")
        dw_in = _ffn_dw_in(xn2, dgu, tk, f"ffn2_dw_in_{l}")
        dw_out = _ffn_dw_out(act, dyb, tk, f"ffn2_dw_out_{l}").reshape(N_DEV, FF_SHARD // 2, D_MODEL)
        token = send("ffn2", l, ("ffn2_w_in", "ffn2_w_out"), [dw_in, dw_out])
        small["norm_ffn2"][l] = dgain.sum(0)

        dy_ssm, dy_gm, dwo, dgs, dgg = _mix_out_bwd(y_ssm, y_gm, _behind(p["gs"], token), p["gg"], mwo, dx, tm,
                                                    f"mix_out_bwd_{l}")
        dwo = dwo.astype(BF16).reshape(N_DEV, D_MODEL // N_DEV, D_MODEL)
        small["gain_ssm_out"][l] = dgs.sum(0)
        small["gain_gm_out"][l] = dgg.sum(0)

        du_ssm, dglu, dglub, ddskip, dct, db, q = _ssm_bwd(
            z, h, dy_ssm, p["bblk"], p["cblk"], p["glu"], p["glub"], p["dskip"], p["revc"], batch, t_chunk,
            f"ssm_bwd_{l}")
        du_gm, dv_gm, dws, dbias, dgv = _gm_bwd(z, dy_gm, p["gv"], p["w_tril"], p["gm_bias"], gm_rows_bwd,
                                                f"gm_bwd_{l}")

        q = q.sum(0).reshape(2, 2, HALF_GROUPS, SSM_STATE)
        qr, qi = q[:, 0].reshape(SSM_GROUPS, SSM_STATE), q[:, 1].reshape(SSM_GROUPS, SSM_STATE)
        den = p["lr"] * p["lr"] + p["li"] * p["li"]
        d_re = (qr * p["lr"] + qi * p["li"]) / den
        d_im = (qi * p["lr"] - qr * p["li"]) / den
        dbb = jnp.stack([_block_diag_take(db[hf, :, k * HALF_ST:(k + 1) * HALF_ST], HALF_GROUPS, SSM_CH, SSM_STATE)
                         for k in range(2) for hf in range(2)]).reshape(2, SSM_GROUPS, SSM_CH, SSM_STATE)
        dcc = jnp.stack([_block_diag_take(dct[hf, :, k * HALF_ST:(k + 1) * HALF_ST], HALF_GROUPS, SSM_CH, SSM_STATE)
                         for k in range(2) for hf in range(2)]).reshape(2, SSM_GROUPS, SSM_CH, SSM_STATE)
        da_re, da_im, dlog_dt, db_re, db_im = p["disc_vjp"](
            (d_re, -d_im, jnp.swapaxes(dbb[0], 1, 2), jnp.swapaxes(dbb[1], 1, 2)))
        small["ssm_a_re"][l], small["ssm_a_im"][l], small["ssm_log_dt"][l] = da_re, da_im, dlog_dt
        small["ssm_b_re"][l], small["ssm_b_im"][l] = db_re, db_im
        small["ssm_c_re"][l], small["ssm_c_im"][l] = dcc[0], -dcc[1]
        small["ssm_d"][l] = ddskip.sum(0).reshape(SSM_GROUPS, SSM_CH)
        small["ssm_glu_w"][l] = jnp.concatenate(
            [_block_diag_take(dglu[:, :SSM_WIDTH], SSM_GROUPS, SSM_CH, SSM_CH),
             _block_diag_take(dglu[:, SSM_WIDTH:], SSM_GROUPS, SSM_CH, SSM_CH)], axis=2)
        dglub = dglub.sum(0)
        small["ssm_glu_b"][l] = jnp.concatenate(
            [dglub[:SSM_WIDTH].reshape(SSM_GROUPS, SSM_CH), dglub[SSM_WIDTH:].reshape(SSM_GROUPS, SSM_CH)], axis=1)
        small["gm_v_gain"][l] = dgv.sum(0)
        small["gm_w_s"][l] = jnp.where(tril[None], dws, 0.0)
        small["gm_b_s"][l] = dbias.reshape(GM_CHUNK, GM_HEADS, GM_HEAD_DIM).sum(-1).T

        dx, dwi, dgain = _mix_in_bwd(x1, p["gmix"], du_ssm, du_gm, dv_gm, dx, mwi, tm, f"mix_in_bwd_{l}")
        dwi = dwi.astype(BF16).reshape(N_DEV, IN_COLS // N_DEV, D_MODEL)
        token = send("mix", l, ("mix_w_in", "mix_w_out"), [dwi, dwo])
        small["norm_mix"][l] = dgain.sum(0)

        dx_out = dx
        dx, dgu, act, dgain, dyb = _ffn_bwd(x0, _behind(p["g1"], token), dx_out, gu1, *weights("ffn1", l), tm,
                                       f"ffn1_bwd_{l}")
        small["norm_ffn1"][l] = dgain.sum(0)
        if l > 0:
            dw_in = _ffn_dw_in(xn1, dgu, tk, f"ffn1_dw_in_{l}")
            dw_out = _ffn_dw_out(act, dyb, tk, f"ffn1_dw_out_{l}").reshape(N_DEV, FF_SHARD // 2, D_MODEL)
            token = send("ffn1", l, ("ffn1_w_in", "ffn1_w_out"), [dw_in, dw_out])
            continue
        small_g = [jnp.stack(small[k]) if k != "norm_final" else dnf.sum(0) for k in SMALL]
        total = sum(int(math.prod(w[k].shape)) for k in SMALL)
        rows = -(-total // (LANES * N_DEV * SUBLANES)) * N_DEV * SUBLANES
        g_all = _all_reduce_small(_pack(small_g, rows).reshape(N_DEV, rows // N_DEV, LANES), "all_reduce_small")
        dw_in = _ffn_dw_in(xn1, dgu, tk, f"ffn1_dw_in_{l}", after=g_all)
        token = send("ffn1_in", l, ("ffn1_w_in",), [dw_in])
        dw_out = _ffn_dw_out(act, dyb, tk, f"ffn1_dw_out_{l}", after=token).reshape(
            N_DEV, FF_SHARD // 2, D_MODEL)
        token = send("ffn1_out", l, ("ffn1_w_out",), [dw_out])

    grad_x = dx.reshape(batch, seq, D_MODEL)
    grads, deltas, new_m, new_v = {}, {}, {}, {}

    g_all = _behind(g_all.reshape(rows, LANES), token)
    like = [w[k] for k in SMALL]
    d_p, m_p, v_p = _adam_packed(g_all, _pack(like, rows), _pack([m[k] for k in SMALL], rows),
                                 _pack([v[k] for k in SMALL], rows), "adam_small")
    for k, g_, d_, m_, v_ in zip(SMALL, _unpack(g_all, like), _unpack(d_p, like), _unpack(m_p, like),
                                 _unpack(v_p, like)):
        grads[k], deltas[k], new_m[k], new_v[k] = g_, d_, m_, v_

    results = {}
    after = d_p
    for group, l, keys, handle in sent:
        landed = _exchange_wait(handle, after, True, f"reduce_scatter_wait_{group}_{l}")
        for k, parts in zip(keys, landed):
            view = (lambda a: jnp.swapaxes(a, 1, 2)) if k in TRANSPOSED else (lambda a: a)
            results[k] = _adam_sharded(parts, view(w[k]), view(m[k]), view(v[k]), l, results.get(k),
                                       f"adam_{k}_{l}")
            after = results[k][0]
    for k in BIG:
        view = (lambda a: jnp.swapaxes(a, 1, 2)) if k in TRANSPOSED else (lambda a: a)
        grads[k], deltas[k], new_m[k], new_v[k] = [view(a) for a in results[k]]
    return loss, grad_x, grads, deltas, new_m, new_v


def kernel(x, norm_ffn1, ffn1_w_in, ffn1_w_out, norm_mix, mix_w_in, ssm_a_re, ssm_a_im, ssm_log_dt, ssm_b_re, ssm_b_im, ssm_c_re, ssm_c_im, ssm_d, ssm_glu_w, ssm_glu_b, gm_v_gain, gm_w_s, gm_b_s, gain_ssm_out, gain_gm_out, mix_w_out, norm_ffn2, ffn2_w_in, ffn2_w_out, norm_final, loss_target, m_norm_ffn1, m_ffn1_w_in, m_ffn1_w_out, m_norm_mix, m_mix_w_in, m_ssm_a_re, m_ssm_a_im, m_ssm_log_dt, m_ssm_b_re, m_ssm_b_im, m_ssm_c_re, m_ssm_c_im, m_ssm_d, m_ssm_glu_w, m_ssm_glu_b, m_gm_v_gain, m_gm_w_s, m_gm_b_s, m_gain_ssm_out, m_gain_gm_out, m_mix_w_out, m_norm_ffn2, m_ffn2_w_in, m_ffn2_w_out, m_norm_final, v_norm_ffn1, v_ffn1_w_in, v_ffn1_w_out, v_norm_mix, v_mix_w_in, v_ssm_a_re, v_ssm_a_im, v_ssm_log_dt, v_ssm_b_re, v_ssm_b_im, v_ssm_c_re, v_ssm_c_im, v_ssm_d, v_ssm_glu_w, v_ssm_glu_b, v_gm_v_gain, v_gm_w_s, v_gm_b_s, v_gain_ssm_out, v_gain_gm_out, v_mix_w_out, v_norm_ffn2, v_ffn2_w_in, v_ffn2_w_out, v_norm_final):
    args = locals()
    w = {k: args[k] for k in WEIGHTS}
    m = {k: args["m_" + k] for k in WEIGHTS}
    v = {k: args["v_" + k] for k in WEIGHTS}
    loss, grad_x, grads, deltas, new_m, new_v = _step(x, loss_target, w, m, v)
    return (loss, grad_x, *[grads[k] for k in WEIGHTS], *[deltas[k] for k in WEIGHTS],
            *[new_m[k] for k in WEIGHTS], *[new_v[k] for k in WEIGHTS])
```

```python
import functools
import math

import jax
import jax.numpy as jnp
from jax import lax
from jax.experimental import pallas as pl
from jax.experimental.pallas import tpu as pltpu

F32 = jnp.float32
BF16 = jnp.bfloat16
MESH = pl.DeviceIdType.MESH
AXES = ("x", "y", "c")

N_DEV = 8
D_MODEL = 1024
D_FF = 2816
FF_SHARD = 2 * D_FF // N_DEV
FF_CHUNKS = 4
MXU_DIM = 256
FF_PIECES = tuple((lo, min(lo + MXU_DIM, FF_SHARD)) for lo in range(0, FF_SHARD, MXU_DIM))
SSM_WIDTH = 512
SSM_CH = 16
SSM_GROUPS = 32
SSM_STATE = 64
HALF_GROUPS = 16
HALF_IN = HALF_GROUPS * SSM_CH
HALF_ST = HALF_GROUPS * SSM_STATE
GM_WIDTH = 512
GM_HEADS = 4
GM_HEAD_DIM = 128
GM_CHUNK = 128
IN_COLS = SSM_WIDTH + 2 * GM_WIDTH
EPS = 1e-6
SUBLANES = 8
LANES = 128

ADAM_LR = 0.001
ADAM_B1 = 0.9
ADAM_B2 = 0.999
ADAM_EPS = 1e-08
ADAM_WD = 0.01
ADAM_STEP = 10

VMEM_LIMIT = 46 * 1024 * 1024


def _cp(*sem):
    return pltpu.CompilerParams(dimension_semantics=sem, vmem_limit_bytes=VMEM_LIMIT)


def _rms_fwd(x, g):
    r = lax.rsqrt(jnp.mean(x * x, axis=-1, keepdims=True) + EPS)
    xh = x * r
    return xh * g, xh, r


def _rms_bwd(dy, xh, r, g):
    dxh = dy * g
    dx = r * (dxh - xh * jnp.mean(dxh * xh, axis=-1, keepdims=True))
    return dx, dy * xh


def _rows8(a):
    m, n = a.shape
    return a.reshape(m // SUBLANES, SUBLANES, n).sum(axis=0)


_GELU_K = math.sqrt(2.0 / math.pi)
_GELU_C = 0.044715


def _gelu(x):
    th = jnp.tanh(_GELU_K * (x + _GELU_C * x * x * x))
    return 0.5 * x * (1.0 + th), th


def _gelu_grad(x, th):
    return 0.5 * (1.0 + th) + 0.5 * x * (1.0 - th * th) * (_GELU_K * (1.0 + 3.0 * _GELU_C * x * x))


def _dot(a, b):
    return jnp.dot(a, b, preferred_element_type=F32)


def _dot_nt(a, b):
    return lax.dot_general(a, b, (((1,), (1,)), ((), ())), preferred_element_type=F32)


def _dot_tn(a, b):
    return lax.dot_general(a, b, (((0,), (0,)), ((), ())), preferred_element_type=F32)


def _ffn_fwd(x, gain, w_in_ag, w_out_ag, tm, name):
    n = x.shape[0]

    def body(x_ref, g_ref, wg_ref, wu_ref, wo_ref, o_ref, xn_ref, gu_ref):
        j = pl.program_id(1)

        @pl.when(j == 0)
        def _():
            xv = x_ref[...]
            y, _, _ = _rms_fwd(xv, g_ref[...])
            xn_ref[...] = y.astype(BF16)
            o_ref[...] = xv

        xn = xn_ref[...]
        wo = wo_ref[...].reshape(FF_SHARD, D_MODEL)
        out = None
        for lo, hi in FF_PIECES:
            gg = _dot(xn, wg_ref[:, lo:hi])
            uu = _dot(xn, wu_ref[:, lo:hi])
            gu_ref[0, :, lo:hi] = gg.astype(BF16)
            gu_ref[1, :, lo:hi] = uu.astype(BF16)
            act = (gg * jax.nn.sigmoid(gg) * uu).astype(BF16)
            part = _dot(act, wo[lo:hi, :])
            out = part if out is None else out + part
        o_ref[...] += 0.5 * out

    return pl.pallas_call(
        body, name=name, grid=(n // tm, FF_CHUNKS),
        in_specs=[
            pl.BlockSpec((tm, D_MODEL), lambda i, j: (i, 0)),
            pl.BlockSpec((1, D_MODEL), lambda i, j: (0, 0)),
            pl.BlockSpec((None, D_MODEL, FF_SHARD), lambda i, j: (j, 0, 0)),
            pl.BlockSpec((None, D_MODEL, FF_SHARD), lambda i, j: (j + FF_CHUNKS, 0, 0)),
            pl.BlockSpec((2, FF_SHARD // 2, D_MODEL), lambda i, j: (j, 0, 0)),
        ],
        out_specs=[
            pl.BlockSpec((tm, D_MODEL), lambda i, j: (i, 0)),
            pl.BlockSpec((tm, D_MODEL), lambda i, j: (i, 0)),
            pl.BlockSpec((None, 2, tm, FF_SHARD), lambda i, j: (j, 0, i, 0)),
        ],
        out_shape=[
            jax.ShapeDtypeStruct((n, D_MODEL), F32),
            jax.ShapeDtypeStruct((n, D_MODEL), BF16),
            jax.ShapeDtypeStruct((FF_CHUNKS, 2, n, FF_SHARD), BF16),
        ],
        compiler_params=_cp("parallel", "arbitrary"),
    )(x, gain, w_in_ag, w_in_ag, w_out_ag)


def _ffn_bwd(x, gain, dy, gu, w_in_ag, w_out_ag, tm, name):
    n = x.shape[0]
    work = (n // tm) * FF_CHUNKS
    steps = work + 1
    first = lambda s: jnp.minimum(s, work - 1)
    second = lambda s: jnp.maximum(s - 1, 0)
    tile1, chunk1 = (lambda s: first(s) // FF_CHUNKS), (lambda s: first(s) % FF_CHUNKS)
    tile2, chunk2 = (lambda s: second(s) // FF_CHUNKS), (lambda s: second(s) % FF_CHUNKS)

    def body(x_ref, g_ref, dy_ref, dy1_ref, gu_ref, wg_ref, wu_ref, wo_ref, dx_ref, dgu_ref, act_ref, dgain_ref,
             dyb_ref, held_ref):
        s = pl.program_id(0)

        @pl.when(s == 0)
        def _():
            dgain_ref[...] = jnp.zeros_like(dgain_ref)
            held_ref[...] = jnp.zeros_like(held_ref)

        @pl.when(jnp.logical_and(chunk1(s) == 0, s < work))
        def _():
            dyb_ref[...] = (0.5 * dy1_ref[...]).astype(BF16)

        @pl.when(chunk2(s) == 0)
        def _():
            dx_ref[...] = jnp.zeros_like(dx_ref)

        held = held_ref[1 - s % 2]
        part = _dot_nt(held[0], wg_ref[...]) + _dot_nt(held[1], wu_ref[...])
        dx_ref[...] += jnp.where(s > 0, part, 0.0)

        dyb = dyb_ref[...]
        wo = wo_ref[...].reshape(FF_SHARD, D_MODEL)
        slot = s % 2
        for lo, hi in FF_PIECES:
            gg = gu_ref[0, :, lo:hi].astype(F32)
            uu = gu_ref[1, :, lo:hi].astype(F32)
            dact = _dot_nt(dyb, wo[lo:hi, :])
            sig = jax.nn.sigmoid(gg)
            silu = gg * sig
            act_ref[:, lo:hi] = (silu * uu).astype(BF16)
            du = (dact * silu).astype(BF16)
            dg = (dact * uu * (sig * (1.0 + gg * (1.0 - sig)))).astype(BF16)
            dgu_ref[0, :, lo:hi] = dg
            dgu_ref[1, :, lo:hi] = du
            held_ref[slot, 0, :, lo:hi] = dg
            held_ref[slot, 1, :, lo:hi] = du

        @pl.when(jnp.logical_and(chunk2(s) == FF_CHUNKS - 1, s > 0))
        def _():
            g = g_ref[...]
            _, xh, r = _rms_fwd(x_ref[...], g)
            dx, dgr = _rms_bwd(dx_ref[...], xh, r, g)
            dx_ref[...] = dy_ref[...] + dx
            dgain_ref[...] += _rows8(dgr)

    return pl.pallas_call(
        body, name=name, grid=(steps,),
        in_specs=[
            pl.BlockSpec((tm, D_MODEL), lambda s: (tile2(s), 0)),
            pl.BlockSpec((1, D_MODEL), lambda s: (0, 0)),
            pl.BlockSpec((tm, D_MODEL), lambda s: (tile2(s), 0)),
            pl.BlockSpec((tm, D_MODEL), lambda s: (tile1(s), 0)),
            pl.BlockSpec((None, 2, tm, FF_SHARD), lambda s: (chunk1(s), 0, tile1(s), 0)),
            pl.BlockSpec((None, D_MODEL, FF_SHARD), lambda s: (chunk2(s), 0, 0)),
            pl.BlockSpec((None, D_MODEL, FF_SHARD), lambda s: (chunk2(s) + FF_CHUNKS, 0, 0)),
            pl.BlockSpec((2, FF_SHARD // 2, D_MODEL), lambda s: (chunk1(s), 0, 0)),
        ],
        out_specs=[
            pl.BlockSpec((tm, D_MODEL), lambda s: (tile2(s), 0)),
            pl.BlockSpec((None, 2, tm, FF_SHARD), lambda s: (chunk1(s), 0, tile1(s), 0)),
            pl.BlockSpec((None, tm, FF_SHARD), lambda s: (chunk1(s), tile1(s), 0)),
            pl.BlockSpec((SUBLANES, D_MODEL), lambda s: (0, 0)),
            pl.BlockSpec((tm, D_MODEL), lambda s: (tile1(s), 0)),
        ],
        out_shape=[
            jax.ShapeDtypeStruct((n, D_MODEL), F32),
            jax.ShapeDtypeStruct((FF_CHUNKS, 2, n, FF_SHARD), BF16),
            jax.ShapeDtypeStruct((FF_CHUNKS, n, FF_SHARD), BF16),
            jax.ShapeDtypeStruct((SUBLANES, D_MODEL), F32),
            jax.ShapeDtypeStruct((n, D_MODEL), BF16),
        ],
        scratch_shapes=[pltpu.VMEM((2, 2, tm, FF_SHARD), BF16)],
        compiler_params=_cp("arbitrary"),
    )(x, gain, dy, dy, gu, w_in_ag, w_in_ag, w_out_ag)


def _ffn_dw_in(xn, dgu, tk, name, after=None):
    n = xn.shape[0]
    nk = n // tk
    deps = [] if after is None else [after]

    def body(a_ref, b_ref, *rest):
        o_ref, acc_ref = rest[-2:]
        k = pl.program_id(2)

        @pl.when(k == 0)
        def _():
            acc_ref[...] = jnp.zeros_like(acc_ref)

        acc_ref[...] += _dot_tn(b_ref[...], a_ref[...])

        @pl.when(k == nk - 1)
        def _():
            o_ref[...] = acc_ref[...].astype(BF16)

    return pl.pallas_call(
        body, name=name, grid=(FF_CHUNKS, 2, nk),
        in_specs=[
            pl.BlockSpec((tk, D_MODEL), lambda j, p, k: (k, 0)),
            pl.BlockSpec((None, None, tk, FF_SHARD), lambda j, p, k: (j, p, k, 0)),
        ] + [pl.BlockSpec(memory_space=pl.ANY)] * len(deps),
        out_specs=pl.BlockSpec((None, FF_SHARD, D_MODEL), lambda j, p, k: (FF_CHUNKS * p + j, 0, 0)),
        out_shape=jax.ShapeDtypeStruct((N_DEV, FF_SHARD, D_MODEL), BF16),
        scratch_shapes=[pltpu.VMEM((FF_SHARD, D_MODEL), F32)],
        compiler_params=_cp("parallel", "parallel", "arbitrary"),
    )(xn, dgu, *deps)


def _ffn_dw_out(act, dyb, tk, name, after=None):
    n = act.shape[1]
    nk = n // tk
    deps = [] if after is None else [after]

    def body(a_ref, b_ref, *rest):
        o_ref, acc_ref = rest[-2:]
        k = pl.program_id(1)

        @pl.when(k == 0)
        def _():
            acc_ref[...] = jnp.zeros_like(acc_ref)

        acc_ref[...] += _dot_tn(a_ref[...], b_ref[...])

        @pl.when(k == nk - 1)
        def _():
            o_ref[...] = acc_ref[...].astype(BF16)

    return pl.pallas_call(
        body, name=name, grid=(FF_CHUNKS, nk),
        in_specs=[
            pl.BlockSpec((None, tk, FF_SHARD), lambda j, k: (j, k, 0)),
            pl.BlockSpec((tk, D_MODEL), lambda j, k: (k, 0)),
        ] + [pl.BlockSpec(memory_space=pl.ANY)] * len(deps),
        out_specs=pl.BlockSpec((None, FF_SHARD, D_MODEL), lambda j, k: (j, 0, 0)),
        out_shape=jax.ShapeDtypeStruct((FF_CHUNKS, FF_SHARD, D_MODEL), BF16),
        scratch_shapes=[pltpu.VMEM((FF_SHARD, D_MODEL), F32)],
        compiler_params=_cp("parallel", "arbitrary"),
    )(act, dyb, *deps)


def _mix_in_fwd(x, gain, w, tm, name):
    n = x.shape[0]

    def body(x_ref, g_ref, w_ref, z_ref):
        y, _, _ = _rms_fwd(x_ref[...], g_ref[...])
        z_ref[...] = _dot(y.astype(BF16), w_ref[...])

    return pl.pallas_call(
        body, name=name, grid=(n // tm,),
        in_specs=[
            pl.BlockSpec((tm, D_MODEL), lambda i: (i, 0)),
            pl.BlockSpec((1, D_MODEL), lambda i: (0, 0)),
            pl.BlockSpec((D_MODEL, IN_COLS), lambda i: (0, 0)),
        ],
        out_specs=pl.BlockSpec((tm, IN_COLS), lambda i: (i, 0)),
        out_shape=jax.ShapeDtypeStruct((n, IN_COLS), F32),
        compiler_params=_cp("parallel"),
    )(x, gain, w)


def _mix_in_bwd(x, gain, du_ssm, du_gm, dv_gm, d_res, w, tm, name):
    n = x.shape[0]

    def body(x_ref, g_ref, d0_ref, d1_ref, d2_ref, dres_ref, w_ref, dx_ref, dw_ref, dgain_ref):
        i = pl.program_id(0)

        @pl.when(i == 0)
        def _():
            dw_ref[...] = jnp.zeros_like(dw_ref)
            dgain_ref[...] = jnp.zeros_like(dgain_ref)

        g = g_ref[...]
        y, xh, r = _rms_fwd(x_ref[...], g)
        xn = y.astype(BF16)
        dxn = jnp.zeros((tm, D_MODEL), F32)
        for k, d_ref in enumerate((d0_ref, d1_ref, d2_ref)):
            dz = d_ref[...].astype(BF16)
            cols = slice(k * SSM_WIDTH, (k + 1) * SSM_WIDTH)
            dxn += _dot_nt(dz, w_ref[:, cols])
            dw_ref[cols, :] += _dot_tn(dz, xn)
        dx, dgr = _rms_bwd(dxn, xh, r, g)
        dx_ref[...] = dres_ref[...] + dx
        dgain_ref[...] += _rows8(dgr)

    row = lambda i: (i, 0)
    fixed = lambda i: (0, 0)
    return pl.pallas_call(
        body, name=name, grid=(n // tm,),
        in_specs=[
            pl.BlockSpec((tm, D_MODEL), row),
            pl.BlockSpec((1, D_MODEL), fixed),
            pl.BlockSpec((tm, SSM_WIDTH), row),
            pl.BlockSpec((tm, GM_WIDTH), row),
            pl.BlockSpec((tm, GM_WIDTH), row),
            pl.BlockSpec((tm, D_MODEL), row),
            pl.BlockSpec((D_MODEL, IN_COLS), fixed),
        ],
        out_specs=[
            pl.BlockSpec((tm, D_MODEL), row),
            pl.BlockSpec((IN_COLS, D_MODEL), fixed),
            pl.BlockSpec((SUBLANES, D_MODEL), fixed),
        ],
        out_shape=[
            jax.ShapeDtypeStruct((n, D_MODEL), F32),
            jax.ShapeDtypeStruct((IN_COLS, D_MODEL), F32),
            jax.ShapeDtypeStruct((SUBLANES, D_MODEL), F32),
        ],
        compiler_params=_cp("arbitrary"),
    )(x, gain, du_ssm, du_gm, dv_gm, d_res, w)


def _mix_out_fwd(y_ssm, y_gm, g_ssm, g_gm, w, x, tm, name):
    n = x.shape[0]

    def body(ys_ref, yg_ref, gs_ref, gg_ref, w_ref, x_ref, o_ref):
        a, _, _ = _rms_fwd(ys_ref[...], gs_ref[...])
        b, _, _ = _rms_fwd(yg_ref[...], gg_ref[...])
        o_ref[...] = (x_ref[...] + _dot(a.astype(BF16), w_ref[0:SSM_WIDTH, :])
                      + _dot(b.astype(BF16), w_ref[SSM_WIDTH:D_MODEL, :]))

    row = lambda i: (i, 0)
    fixed = lambda i: (0, 0)
    return pl.pallas_call(
        body, name=name, grid=(n // tm,),
        in_specs=[
            pl.BlockSpec((tm, SSM_WIDTH), row), pl.BlockSpec((tm, GM_WIDTH), row),
            pl.BlockSpec((1, SSM_WIDTH), fixed), pl.BlockSpec((1, GM_WIDTH), fixed),
            pl.BlockSpec((D_MODEL, D_MODEL), fixed), pl.BlockSpec((tm, D_MODEL), row),
        ],
        out_specs=pl.BlockSpec((tm, D_MODEL), row),
        out_shape=jax.ShapeDtypeStruct((n, D_MODEL), F32),
        compiler_params=_cp("parallel"),
    )(y_ssm, y_gm, g_ssm, g_gm, w, x)


def _mix_out_bwd(y_ssm, y_gm, g_ssm, g_gm, w, dx, tm, name):
    n = dx.shape[0]

    def body(ys_ref, yg_ref, gs_ref, gg_ref, w_ref, dx_ref, dys_ref, dyg_ref, dw_ref, dgs_ref, dgg_ref):
        i = pl.program_id(0)

        @pl.when(i == 0)
        def _():
            dw_ref[...] = jnp.zeros_like(dw_ref)
            dgs_ref[...] = jnp.zeros_like(dgs_ref)
            dgg_ref[...] = jnp.zeros_like(dgg_ref)

        dxb = dx_ref[...].astype(BF16)
        parts = ((ys_ref, gs_ref, dys_ref, dgs_ref, 0), (yg_ref, gg_ref, dyg_ref, dgg_ref, SSM_WIDTH))
        for y_ref, g_ref, dy_ref, dg_ref, off in parts:
            g = g_ref[...]
            yn, xh, r = _rms_fwd(y_ref[...], g)
            rows = slice(off, off + SSM_WIDTH)
            dyn = _dot_nt(dxb, w_ref[rows, :])
            dw_ref[rows, :] += _dot_tn(yn.astype(BF16), dxb)
            dy, dgr = _rms_bwd(dyn, xh, r, g)
            dy_ref[...] = dy
            dg_ref[...] += _rows8(dgr)

    row = lambda i: (i, 0)
    fixed = lambda i: (0, 0)
    return pl.pallas_call(
        body, name=name, grid=(n // tm,),
        in_specs=[
            pl.BlockSpec((tm, SSM_WIDTH), row), pl.BlockSpec((tm, GM_WIDTH), row),
            pl.BlockSpec((1, SSM_WIDTH), fixed), pl.BlockSpec((1, GM_WIDTH), fixed),
            pl.BlockSpec((D_MODEL, D_MODEL), fixed), pl.BlockSpec((tm, D_MODEL), row),
        ],
        out_specs=[
            pl.BlockSpec((tm, SSM_WIDTH), row), pl.BlockSpec((tm, GM_WIDTH), row),
            pl.BlockSpec((D_MODEL, D_MODEL), fixed),
            pl.BlockSpec((SUBLANES, SSM_WIDTH), fixed), pl.BlockSpec((SUBLANES, GM_WIDTH), fixed),
        ],
        out_shape=[
            jax.ShapeDtypeStruct((n, SSM_WIDTH), F32), jax.ShapeDtypeStruct((n, GM_WIDTH), F32),
            jax.ShapeDtypeStruct((D_MODEL, D_MODEL), F32),
            jax.ShapeDtypeStruct((SUBLANES, SSM_WIDTH), F32), jax.ShapeDtypeStruct((SUBLANES, GM_WIDTH), F32),
        ],
        compiler_params=_cp("arbitrary"),
    )(y_ssm, y_gm, g_ssm, g_gm, w, dx)


SCAN_W = 512
SCAN_PIECES = HALF_ST // SCAN_W


def _scan_tiles(src_ref, dst_ref, dst_off, c_ref, half, carry_ref, n_tiles, reverse, extra=None):
    shifts = (1, 2, 4)
    carry_row = 0 if reverse else SUBLANES - 1

    def cols(piece, im):
        lo = im * HALF_ST + piece * SCAN_W
        return slice(lo, lo + SCAN_W)

    def step(t, state):
        carries, accs = state
        k = (n_tiles - 1 - t) if reverse else t
        rows = slice(k * SUBLANES, (k + 1) * SUBLANES)
        new_carries, new_accs = [], []
        for piece in range(SCAN_PIECES):
            cr, ci = carries[piece]
            xr0 = src_ref[rows, cols(piece, 0)]
            xi0 = src_ref[rows, cols(piece, 1)]
            xr, xi = xr0, xi0
            for si, s in enumerate(shifts):
                ar = c_ref[half, si, :, cols(piece, 0)]
                ai = c_ref[half, si, :, cols(piece, 1)]
                sh = (SUBLANES - s) if reverse else s
                sr = pltpu.roll(xr, sh, 0)
                sm = pltpu.roll(xi, sh, 0)
                xr, xi = xr + (ar * sr - ai * sm), xi + (ar * sm + ai * sr)
            pr = c_ref[half, 3, :, cols(piece, 0)]
            pi = c_ref[half, 3, :, cols(piece, 1)]
            hr = xr + (pr * cr - pi * ci)
            hi = xi + (pr * ci + pi * cr)
            dst_ref[rows, pl.ds(dst_off + piece * SCAN_W, SCAN_W)] = hr
            dst_ref[rows, pl.ds(dst_off + HALF_ST + piece * SCAN_W, SCAN_W)] = hi
            new_carries.append((jnp.broadcast_to(hr[carry_row:carry_row + 1, :], (SUBLANES, SCAN_W)),
                                jnp.broadcast_to(hi[carry_row:carry_row + 1, :], (SUBLANES, SCAN_W))))
            if extra is not None:
                new_accs.append(extra(rows, piece, (xr0, xi0), (hr, hi), accs[piece]))
        return tuple(new_carries), tuple(new_accs)

    base = half * 2 * HALF_ST
    carries0 = tuple((carry_ref[:, pl.ds(base + p * SCAN_W, SCAN_W)],
                      carry_ref[:, pl.ds(base + HALF_ST + p * SCAN_W, SCAN_W)]) for p in range(SCAN_PIECES))
    zero = jnp.zeros((SUBLANES, SCAN_W), F32)
    accs0 = tuple((zero, zero) for _ in range(SCAN_PIECES)) if extra is not None else ()
    state = (carries0, accs0)
    for t in range(n_tiles):
        state = step(t, state)
    carries, accs = state
    for p in range(SCAN_PIECES):
        carry_ref[:, pl.ds(base + p * SCAN_W, SCAN_W)] = carries[p][0]
        carry_ref[:, pl.ds(base + HALF_ST + p * SCAN_W, SCAN_W)] = carries[p][1]
    return accs


def _ssm_tail(hb, u, c_ref, glu_ref, glub_ref, dskip_ref):
    ypre = u * dskip_ref[...]
    parts = []
    for half in range(2):
        parts.append(_dot(hb[half], c_ref[half]))
    ypre = ypre + jnp.concatenate(parts, axis=1)
    yg, th = _gelu(ypre)
    zz = _dot(yg.astype(BF16), glu_ref[...]) + glub_ref[...]
    z1, z2 = zz[:, :SSM_WIDTH], zz[:, SSM_WIDTH:]
    sg = jax.nn.sigmoid(z2)
    return ypre, th, yg, z1, sg


def _ssm_fwd(z, bblk, cblk, glu, glub, dskip, fwdc, batch, t_chunk, name):
    n = z.shape[0]
    nk = n // batch // t_chunk
    n_tiles = t_chunk // SUBLANES

    def body(u_ref, b_ref, c_ref, glu_ref, glub_ref, dskip_ref, k_ref, y_ref, h_ref, bu_ref, carry_ref):
        @pl.when(pl.program_id(1) == 0)
        def _():
            carry_ref[...] = jnp.zeros_like(carry_ref)

        u = u_ref[...]
        ub = u.astype(BF16)
        for half in range(2):
            bu_ref[half] = _dot(ub[:, half * HALF_IN:(half + 1) * HALF_IN], b_ref[half])
            _scan_tiles(bu_ref.at[half], h_ref, half * 2 * HALF_ST, k_ref, half, carry_ref, n_tiles, False)
        hb = [h_ref[:, half * 2 * HALF_ST:(half + 1) * 2 * HALF_ST].astype(BF16) for half in range(2)]
        _, _, _, z1, sg = _ssm_tail(hb, u, c_ref, glu_ref, glub_ref, dskip_ref)
        y_ref[...] = z1 * sg

    fixed2 = lambda b, k: (0, 0)
    fixed3 = lambda b, k: (0, 0, 0)
    row = lambda b, k: (b * nk + k, 0)
    return pl.pallas_call(
        body, name=name, grid=(batch, nk),
        in_specs=[
            pl.BlockSpec((t_chunk, SSM_WIDTH), row),
            pl.BlockSpec((2, HALF_IN, 2 * HALF_ST), fixed3),
            pl.BlockSpec((2, 2 * HALF_ST, HALF_IN), fixed3),
            pl.BlockSpec((SSM_WIDTH, 2 * SSM_WIDTH), fixed2),
            pl.BlockSpec((1, 2 * SSM_WIDTH), fixed2),
            pl.BlockSpec((1, SSM_WIDTH), fixed2),
            pl.BlockSpec((2, 4, SUBLANES, 2 * HALF_ST), lambda b, k: (0, 0, 0, 0)),
        ],
        out_specs=[pl.BlockSpec((t_chunk, SSM_WIDTH), row), pl.BlockSpec((t_chunk, 4 * HALF_ST), row)],
        out_shape=[jax.ShapeDtypeStruct((n, SSM_WIDTH), F32), jax.ShapeDtypeStruct((n, 4 * HALF_ST), F32)],
        scratch_shapes=[pltpu.VMEM((2, t_chunk, 2 * HALF_ST), F32), pltpu.VMEM((SUBLANES, 4 * HALF_ST), F32)],
        compiler_params=_cp("parallel", "arbitrary"),
    )(z, bblk, cblk, glu, glub, dskip, fwdc)


def _ssm_bwd(z, h, dy, bblk, cblk, glu, glub, dskip, revc, batch, t_chunk, name):
    n = z.shape[0]
    nk = n // batch // t_chunk
    n_tiles = t_chunk // SUBLANES

    def body(u_ref, h_ref, dy_ref, b_ref, c_ref, glu_ref, glub_ref, dskip_ref, k_ref,
             du_ref, dglu_ref, dglub_ref, ddskip_ref, dct_ref, db_ref, q_ref, g_ref, carry_ref):
        first = jnp.logical_and(pl.program_id(0) == 0, pl.program_id(1) == 0)

        @pl.when(first)
        def _():
            for r in (dglu_ref, dglub_ref, ddskip_ref, dct_ref, db_ref, q_ref):
                r[...] = jnp.zeros_like(r)

        @pl.when(pl.program_id(1) == 0)
        def _():
            carry_ref[...] = jnp.zeros_like(carry_ref)

        u = u_ref[...]
        ub = u.astype(BF16)
        hb = [h_ref[:, half * 2 * HALF_ST:(half + 1) * 2 * HALF_ST].astype(BF16) for half in range(2)]
        ypre, th, yg, z1, sg = _ssm_tail(hb, u, c_ref, glu_ref, glub_ref, dskip_ref)
        dout = dy_ref[...]
        dz = jnp.concatenate([dout * sg, dout * z1 * sg * (1.0 - sg)], axis=1)
        dzb = dz.astype(BF16)
        dglu_ref[...] += _dot_tn(yg.astype(BF16), dzb)
        dglub_ref[...] += _rows8(dz)
        dypre = _dot_nt(dzb, glu_ref[...]) * _gelu_grad(ypre, th)
        ddskip_ref[...] += _rows8(dypre * u)
        dypb = dypre.astype(BF16)
        du_parts = []
        for half in range(2):
            dyp_h = dypb[:, half * HALF_IN:(half + 1) * HALF_IN]
            dct_ref[half] += _dot_tn(dyp_h, hb[half])
            g_ref[half] = _dot_nt(dyp_h, c_ref[half])

            def extra(rows, piece, x_in, g_out, acc, half=half):
                er, ei = g_out[0] - x_in[0], g_out[1] - x_in[1]
                base = half * 2 * HALF_ST + piece * SCAN_W
                hr = h_ref[rows, pl.ds(base, SCAN_W)]
                hi = h_ref[rows, pl.ds(base + HALF_ST, SCAN_W)]
                return acc[0] + (er * hr + ei * hi), acc[1] + (er * hi - ei * hr)

            accs = _scan_tiles(g_ref.at[half], g_ref.at[half], 0, k_ref, half, carry_ref, n_tiles, True, extra)
            for piece in range(SCAN_PIECES):
                base = half * 2 * HALF_ST + piece * SCAN_W
                q_ref[:, pl.ds(base, SCAN_W)] += accs[piece][0]
                q_ref[:, pl.ds(base + HALF_ST, SCAN_W)] += accs[piece][1]
            gb = g_ref[half].astype(BF16)
            db_ref[half] += _dot_tn(ub[:, half * HALF_IN:(half + 1) * HALF_IN], gb)
            du_parts.append(_dot_nt(gb, b_ref[half]))
        du_ref[...] = dypre * dskip_ref[...] + jnp.concatenate(du_parts, axis=1)

    fixed2 = lambda b, k: (0, 0)
    fixed3 = lambda b, k: (0, 0, 0)
    row = lambda b, k: (b * nk + (nk - 1 - k), 0)
    return pl.pallas_call(
        body, name=name, grid=(batch, nk),
        in_specs=[
            pl.BlockSpec((t_chunk, SSM_WIDTH), row),
            pl.BlockSpec((t_chunk, 4 * HALF_ST), row),
            pl.BlockSpec((t_chunk, SSM_WIDTH), row),
            pl.BlockSpec((2, HALF_IN, 2 * HALF_ST), fixed3),
            pl.BlockSpec((2, 2 * HALF_ST, HALF_IN), fixed3),
            pl.BlockSpec((SSM_WIDTH, 2 * SSM_WIDTH), fixed2),
            pl.BlockSpec((1, 2 * SSM_WIDTH), fixed2),
            pl.BlockSpec((1, SSM_WIDTH), fixed2),
            pl.BlockSpec((2, 4, SUBLANES, 2 * HALF_ST), lambda b, k: (0, 0, 0, 0)),
        ],
        out_specs=[
            pl.BlockSpec((t_chunk, SSM_WIDTH), row),
            pl.BlockSpec((SSM_WIDTH, 2 * SSM_WIDTH), fixed2),
            pl.BlockSpec((SUBLANES, 2 * SSM_WIDTH), fixed2),
            pl.BlockSpec((SUBLANES, SSM_WIDTH), fixed2),
            pl.BlockSpec((2, HALF_IN, 2 * HALF_ST), fixed3),
            pl.BlockSpec((2, HALF_IN, 2 * HALF_ST), fixed3),
            pl.BlockSpec((SUBLANES, 4 * HALF_ST), fixed2),
        ],
        out_shape=[
            jax.ShapeDtypeStruct((n, SSM_WIDTH), F32),
            jax.ShapeDtypeStruct((SSM_WIDTH, 2 * SSM_WIDTH), F32),
            jax.ShapeDtypeStruct((SUBLANES, 2 * SSM_WIDTH), F32),
            jax.ShapeDtypeStruct((SUBLANES, SSM_WIDTH), F32),
            jax.ShapeDtypeStruct((2, HALF_IN, 2 * HALF_ST), F32),
            jax.ShapeDtypeStruct((2, HALF_IN, 2 * HALF_ST), F32),
            jax.ShapeDtypeStruct((SUBLANES, 4 * HALF_ST), F32),
        ],
        scratch_shapes=[pltpu.VMEM((2, t_chunk, 2 * HALF_ST), F32), pltpu.VMEM((SUBLANES, 4 * HALF_ST), F32)],
        compiler_params=_cp("arbitrary", "arbitrary"),
    )(z, h, dy, bblk, cblk, glu, glub, dskip, revc)


def _gm_chunk_fwd(u, v, gain_ref, w_ref, bias_ref):
    ug, thu = _gelu(u)
    vg, thv = _gelu(v)
    rs, vns, ss = [], [], []
    for hh in range(GM_HEADS):
        cs = slice(hh * GM_HEAD_DIM, (hh + 1) * GM_HEAD_DIM)
        vn, _, r = _rms_fwd(vg[:, cs], gain_ref[:, cs])
        s = _dot(w_ref[hh], vn.astype(BF16)) + bias_ref[:, cs]
        rs.append(r)
        vns.append(vn)
        ss.append(s)
    return ug, thu, thv, vg, rs, vns, ss


def _gm_fwd(z, gain, w_tril, bias, rows, name):
    n = z.shape[0]
    chunks = rows // GM_CHUNK

    def body(u_ref, v_ref, gain_ref, w_ref, bias_ref, y_ref):
        for c in range(chunks):
            rs_ = slice(c * GM_CHUNK, (c + 1) * GM_CHUNK)
            ug, _, _, _, _, _, ss = _gm_chunk_fwd(u_ref[rs_, :], v_ref[rs_, :], gain_ref, w_ref, bias_ref)
            y_ref[rs_, :] = ug * jnp.concatenate(ss, axis=1)

    return pl.pallas_call(
        body, name=name, grid=(n // rows,),
        in_specs=[
            pl.BlockSpec((rows, GM_WIDTH), lambda i: (i, 1)),
            pl.BlockSpec((rows, GM_WIDTH), lambda i: (i, 2)),
            pl.BlockSpec((1, GM_WIDTH), lambda i: (0, 0)),
            pl.BlockSpec((GM_HEADS, GM_CHUNK, GM_CHUNK), lambda i: (0, 0, 0)),
            pl.BlockSpec((GM_CHUNK, GM_WIDTH), lambda i: (0, 0)),
        ],
        out_specs=pl.BlockSpec((rows, GM_WIDTH), lambda i: (i, 0)),
        out_shape=jax.ShapeDtypeStruct((n, GM_WIDTH), F32),
        compiler_params=_cp("parallel"),
    )(z, z, gain, w_tril, bias)


def _gm_bwd(z, dy, gain, w_tril, bias, rows, name):
    n = z.shape[0]
    chunks = rows // GM_CHUNK

    def body(u_ref, v_ref, dy_ref, gain_ref, w_ref, bias_ref, du_ref, dv_ref, dw_ref, dbias_ref, dgain_ref):
        @pl.when(pl.program_id(0) == 0)
        def _():
            dw_ref[...] = jnp.zeros_like(dw_ref)
            dbias_ref[...] = jnp.zeros_like(dbias_ref)
            dgain_ref[...] = jnp.zeros_like(dgain_ref)

        for c in range(chunks):
            rs_ = slice(c * GM_CHUNK, (c + 1) * GM_CHUNK)
            u, v = u_ref[rs_, :], v_ref[rs_, :]
            ug, thu, thv, vg, rs, vns, ss = _gm_chunk_fwd(u, v, gain_ref, w_ref, bias_ref)
            dout = dy_ref[rs_, :]
            ds = dout * ug
            du_ref[rs_, :] = dout * jnp.concatenate(ss, axis=1) * _gelu_grad(u, thu)
            dbias_ref[...] += ds
            dvg_parts, dgain_parts = [], []
            for hh in range(GM_HEADS):
                cs = slice(hh * GM_HEAD_DIM, (hh + 1) * GM_HEAD_DIM)
                dsb = ds[:, cs].astype(BF16)
                dvn = _dot_tn(w_ref[hh], dsb)
                dw_ref[hh] += _dot_nt(dsb, vns[hh].astype(BF16))
                g = gain_ref[:, cs]
                xh = vg[:, cs] * rs[hh]
                dvg, dgr = _rms_bwd(dvn, xh, rs[hh], g)
                dvg_parts.append(dvg)
                dgain_parts.append(dgr)
            dv_ref[rs_, :] = jnp.concatenate(dvg_parts, axis=1) * _gelu_grad(v, thv)
            dgain_ref[...] += _rows8(jnp.concatenate(dgain_parts, axis=1))

    row = lambda i: (i, 0)
    return pl.pallas_call(
        body, name=name, grid=(n // rows,),
        in_specs=[
            pl.BlockSpec((rows, GM_WIDTH), lambda i: (i, 1)),
            pl.BlockSpec((rows, GM_WIDTH), lambda i: (i, 2)),
            pl.BlockSpec((rows, GM_WIDTH), row),
            pl.BlockSpec((1, GM_WIDTH), lambda i: (0, 0)),
            pl.BlockSpec((GM_HEADS, GM_CHUNK, GM_CHUNK), lambda i: (0, 0, 0)),
            pl.BlockSpec((GM_CHUNK, GM_WIDTH), lambda i: (0, 0)),
        ],
        out_specs=[
            pl.BlockSpec((rows, GM_WIDTH), row), pl.BlockSpec((rows, GM_WIDTH), row),
            pl.BlockSpec((GM_HEADS, GM_CHUNK, GM_CHUNK), lambda i: (0, 0, 0)),
            pl.BlockSpec((GM_CHUNK, GM_WIDTH), lambda i: (0, 0)),
            pl.BlockSpec((SUBLANES, GM_WIDTH), lambda i: (0, 0)),
        ],
        out_shape=[
            jax.ShapeDtypeStruct((n, GM_WIDTH), F32), jax.ShapeDtypeStruct((n, GM_WIDTH), F32),
            jax.ShapeDtypeStruct((GM_HEADS, GM_CHUNK, GM_CHUNK), F32),
            jax.ShapeDtypeStruct((GM_CHUNK, GM_WIDTH), F32),
            jax.ShapeDtypeStruct((SUBLANES, GM_WIDTH), F32),
        ],
        compiler_params=_cp("arbitrary"),
    )(z, z, dy, gain, w_tril, bias)


def _loss_head(x, gain, target, tm, name):
    n = x.shape[0]

    def body(x_ref, g_ref, t_ref, dx_ref, sq_ref, dgain_ref):
        @pl.when(pl.program_id(0) == 0)
        def _():
            sq_ref[...] = jnp.zeros_like(sq_ref)
            dgain_ref[...] = jnp.zeros_like(dgain_ref)

        g = g_ref[...]
        y, xh, r = _rms_fwd(x_ref[...], g)
        err = y - t_ref[...]
        sq_ref[...] += _rows8(err * err)
        dx, dgr = _rms_bwd(err * (1.0 / D_MODEL), xh, r, g)
        dx_ref[...] = dx
        dgain_ref[...] += _rows8(dgr)

    row = lambda i: (i, 0)
    fixed = lambda i: (0, 0)
    return pl.pallas_call(
        body, name=name, grid=(n // tm,),
        in_specs=[pl.BlockSpec((tm, D_MODEL), row), pl.BlockSpec((1, D_MODEL), fixed), pl.BlockSpec((tm, D_MODEL), row)],
        out_specs=[pl.BlockSpec((tm, D_MODEL), row), pl.BlockSpec((SUBLANES, D_MODEL), fixed),
                   pl.BlockSpec((SUBLANES, D_MODEL), fixed)],
        out_shape=[jax.ShapeDtypeStruct((n, D_MODEL), F32), jax.ShapeDtypeStruct((SUBLANES, D_MODEL), F32),
                   jax.ShapeDtypeStruct((SUBLANES, D_MODEL), F32)],
        compiler_params=_cp("arbitrary"),
    )(x, gain, target)


def _adam_math(w, g, m, v):
    m2 = ADAM_B1 * m + (1.0 - ADAM_B1) * g
    v2 = ADAM_B2 * v + (1.0 - ADAM_B2) * (g * g)
    m_hat = m2 / (1.0 - ADAM_B1 ** ADAM_STEP)
    v_hat = v2 / (1.0 - ADAM_B2 ** ADAM_STEP)
    delta = -ADAM_LR * (m_hat / (jnp.sqrt(v_hat) + ADAM_EPS) + ADAM_WD * w)
    return delta, m2, v2


def _adam_sharded(parts, w, m, v, layer, earlier, name):
    depth, r, c = w.shape
    tr = max(t for t in range(16, 353, 16) if r % t == 0)

    def body(p_ref, w_ref, m_ref, v_ref, *rest):
        g_ref, d_ref, m2_ref, v2_ref = rest[-4:]
        g = p_ref[0].astype(F32)
        for s in range(1, N_DEV):
            g = g + p_ref[s].astype(F32)
        delta, m2, v2 = _adam_math(w_ref[...], g, m_ref[...], v_ref[...])
        g_ref[...] = g
        d_ref[...] = delta
        m2_ref[...] = m2
        v2_ref[...] = v2

    blk = pl.BlockSpec((None, tr, c), lambda i: (layer, i, 0))
    extra = [] if earlier is None else list(earlier)
    return pl.pallas_call(
        body, name=name, grid=(r // tr,),
        in_specs=[pl.BlockSpec((N_DEV, tr, c), lambda i: (0, i, 0)), blk, blk, blk]
        + [pl.BlockSpec(memory_space=pl.ANY)] * len(extra),
        out_specs=[blk, blk, blk, blk],
        out_shape=[jax.ShapeDtypeStruct((depth, r, c), F32)] * 4,
        input_output_aliases={4 + i: i for i in range(len(extra))},
        compiler_params=_cp("parallel"),
    )(parts, w, m, v, *extra)


def _adam_packed(g, w, m, v, name):
    r, c = g.shape

    def body(g_ref, w_ref, m_ref, v_ref, d_ref, m2_ref, v2_ref):
        delta, m2, v2 = _adam_math(w_ref[...], g_ref[...], m_ref[...], v_ref[...])
        d_ref[...] = delta
        m2_ref[...] = m2
        v2_ref[...] = v2

    blk = pl.BlockSpec((r, c), lambda i: (0, 0))
    return pl.pallas_call(
        body, name=name, grid=(1,),
        in_specs=[blk, blk, blk, blk], out_specs=[blk, blk, blk],
        out_shape=[jax.ShapeDtypeStruct((r, c), F32)] * 3,
        compiler_params=_cp("arbitrary"),
    )(g, w, m, v)


def _my_place():
    return lax.axis_index("x"), lax.axis_index("y"), lax.axis_index("c")


def _flip(place, rel):
    x, y, c = place
    return (1 - x if rel & 4 else x, 1 - y if rel & 2 else y, 1 - c if rel & 1 else c)


def _index(place):
    return 4 * place[0] + 2 * place[1] + place[2]


def _all_gather(shards, name):
    na = len(shards)

    def body(*refs):
        xs, outs = refs[:na], refs[na:2 * na]
        send_sems, recv_sems, local_sems = refs[2 * na:]
        me = _my_place()
        sibling = _flip(me, 1)
        chips = [_flip(me, 4), _flip(me, 2), _flip(me, 6)]

        def copy(a, k, block, to, src=None):
            slot = outs[a].at[_index(block)]
            return pltpu.make_async_remote_copy(
                src_ref=slot if src is None else src, dst_ref=slot,
                send_sem=send_sems.at[a, k], recv_sem=recv_sems.at[a, k],
                device_id=to, device_id_type=MESH)

        mine = [pltpu.make_async_copy(xs[a], outs[a].at[_index(me)], local_sems.at[a]) for a in range(na)]
        for cp in mine:
            cp.start()
        first = []
        for a in range(na):
            first.append(copy(a, 0, me, sibling, src=xs[a]))
            first += [copy(a, 1 + j, me, chip, src=xs[a]) for j, chip in enumerate(chips)]
        for cp in first:
            cp.start()
        passed = []
        for a in range(na):
            for j, chip in enumerate(chips):
                copy(a, 1 + j, chip, me).wait_recv()
                fwd = copy(a, 4 + j, chip, sibling)
                fwd.start()
                passed.append(fwd)
        for a in range(na):
            copy(a, 0, sibling, me).wait_recv()
            for j, chip in enumerate(chips):
                copy(a, 4 + j, _flip(chip, 1), me).wait_recv()
        for cp in first + passed:
            cp.wait_send()
        for cp in mine:
            cp.wait()

    hbm = pl.BlockSpec(memory_space=pl.ANY)
    return pl.pallas_call(
        body, name=name,
        in_specs=[hbm] * na, out_specs=[hbm] * na,
        out_shape=[jax.ShapeDtypeStruct((N_DEV,) + s.shape, s.dtype) for s in shards],
        scratch_shapes=[pltpu.SemaphoreType.DMA((na, 7)), pltpu.SemaphoreType.DMA((na, 7)),
                        pltpu.SemaphoreType.DMA((na,))],
    )(*shards)


_HBM = pl.BlockSpec(memory_space=pltpu.HBM)
_SEM = pl.BlockSpec(memory_space=pltpu.SEMAPHORE)
_EFFECT = pltpu.SideEffectType.DATAFLOW_SIDE_EFFECTING


def _exchange_copy(src_ref, land_ref, send_sems, recv_sems, a, rel, me, scatter, landed):
    peer = _flip(me, rel)
    src = src_ref.at[_index(peer)] if scatter else src_ref
    return pltpu.make_async_remote_copy(
        src_ref=src, dst_ref=land_ref.at[_index(peer if landed else me)],
        send_sem=send_sems.at[a * (N_DEV - 1) + rel - 1], recv_sem=recv_sems.at[a * (N_DEV - 1) + rel - 1],
        device_id=peer, device_id_type=MESH)


def _own_slot(data, me, scatter):
    if scatter:
        own = lax.dynamic_slice_in_dim(data, me, 1, axis=0)
        shape = data.shape
    else:
        own = data[None]
        shape = (N_DEV,) + data.shape
    start = (me,) + (0,) * (len(shape) - 1)
    return lax.dynamic_update_slice(lax.empty(shape, data.dtype), own, start)


def _exchange_start(groups, me, scatter, name, after=None):
    sizes = [len(g) for g in groups]
    srcs = [a for g in groups for a in g]
    lands = [_own_slot(a, me, scatter) for a in srcs]
    na, ng = len(srcs), len(groups)
    deps = [] if after is None else [after]

    def body(*refs):
        src_refs, land_refs = refs[:na], refs[na:2 * na]
        sems = refs[2 * na + len(deps):2 * na + len(deps) + 2 * ng]
        token = refs[-1]
        place = _my_place()
        a = 0
        for g, size in enumerate(sizes):
            for k in range(size):
                for rel in range(1, N_DEV):
                    _exchange_copy(src_refs[a], land_refs[a], sems[2 * g], sems[2 * g + 1], k, rel, place, scatter,
                                   False).start()
                a += 1
        token[...] = jnp.zeros_like(token)

    sem_shapes = [pltpu.SemaphoreType.DMA((size * (N_DEV - 1),)) for size in sizes for _ in range(2)]
    outs = pl.pallas_call(
        body, name=name,
        in_specs=[_HBM] * (2 * na) + [pl.BlockSpec(memory_space=pl.ANY)] * len(deps),
        out_specs=[_SEM] * (2 * ng) + [_HBM] * (2 * na) + [pl.BlockSpec(memory_space=pltpu.VMEM)],
        out_shape=sem_shapes + [pltpu.HBM(a.shape, a.dtype) for a in srcs + lands]
        + [jax.ShapeDtypeStruct((SUBLANES, LANES), F32)],
        input_output_aliases={i: 2 * ng + i for i in range(2 * na)},
        compiler_params=pltpu.CompilerParams(has_side_effects=_EFFECT),
    )(*[pltpu.with_memory_space_constraint(a, pltpu.HBM) for a in srcs + lands], *deps)
    sems, thru, token = outs[:2 * ng], outs[2 * ng:2 * ng + 2 * na], outs[-1]
    handles, a = [], 0
    for g, size in enumerate(sizes):
        handles.append((sems[2 * g], sems[2 * g + 1], thru[a:a + size], thru[na + a:na + a + size]))
        a += size
    return handles, token


def _exchange_wait(handle, after, scatter, name):
    send_sems, recv_sems, srcs, lands = handle
    na = len(srcs)

    def body(*refs):
        src_refs, land_refs = refs[:na], refs[na:2 * na]
        send_ref, recv_ref = refs[2 * na], refs[2 * na + 1]
        place = _my_place()
        for a in range(na):
            for rel in range(1, N_DEV):
                cp = _exchange_copy(src_refs[a], land_refs[a], send_ref, recv_ref, a, rel, place, scatter, True)
                cp.wait_send()
                cp.wait_recv()

    outs = pl.pallas_call(
        body, name=name,
        in_specs=[_HBM] * (2 * na) + [_SEM, _SEM, pl.BlockSpec(memory_space=pl.ANY)],
        out_specs=[_HBM] * (2 * na),
        out_shape=[pltpu.HBM(a.shape, a.dtype) for a in list(srcs) + list(lands)],
        input_output_aliases={i: i for i in range(2 * na)},
        compiler_params=pltpu.CompilerParams(has_side_effects=_EFFECT),
    )(*srcs, *lands, send_sems, recv_sems, after)
    return outs[na:]


def _behind(arr, token):
    return arr + token[0:1, 0:1]


def _all_reduce_small(g, name):
    _, r, c = g.shape

    def body(g_ref, o_ref, land_ref, red_ref, send1, recv1, send2, recv2):
        me = _my_place()
        idx = _index(me)

        def scatter(rel):
            peer = _flip(me, rel)
            return pltpu.make_async_remote_copy(
                src_ref=g_ref.at[_index(peer)], dst_ref=land_ref.at[idx],
                send_sem=send1.at[rel - 1], recv_sem=recv1.at[rel - 1], device_id=peer, device_id_type=MESH)

        def gather(rel):
            peer = _flip(me, rel)
            return pltpu.make_async_remote_copy(
                src_ref=red_ref, dst_ref=o_ref.at[idx],
                send_sem=send2.at[rel - 1], recv_sem=recv2.at[rel - 1], device_id=peer, device_id_type=MESH)

        for rel in range(1, N_DEV):
            scatter(rel).start()
        land_ref[idx] = g_ref[idx]
        for rel in range(1, N_DEV):
            scatter(rel).wait()
        acc = land_ref[0]
        for s in range(1, N_DEV):
            acc = acc + land_ref[s]
        red_ref[...] = acc
        for rel in range(1, N_DEV):
            gather(rel).start()
        o_ref[idx] = acc
        for rel in range(1, N_DEV):
            gather(rel).wait()

    vmem = pl.BlockSpec(memory_space=pltpu.VMEM)
    return pl.pallas_call(
        body, name=name,
        in_specs=[vmem], out_specs=vmem,
        out_shape=jax.ShapeDtypeStruct(g.shape, F32),
        scratch_shapes=[pltpu.VMEM(g.shape, F32), pltpu.VMEM((r, c), F32)]
        + [pltpu.SemaphoreType.DMA((N_DEV - 1,))] * 4,
        compiler_params=pltpu.CompilerParams(vmem_limit_bytes=VMEM_LIMIT),
    )(g)


def _ssm_discretize(a_re, a_im, log_dt, b_re, b_im):
    dt = jnp.exp(log_dt)[:, None]
    mag = jnp.exp(a_re * dt)
    lr, li = mag * jnp.cos(a_im * dt), mag * jnp.sin(a_im * dt)
    den = a_re * a_re + a_im * a_im
    qr = ((lr - 1.0) * a_re + li * a_im) / den
    qi = (li * a_re - (lr - 1.0) * a_im) / den
    bbr = qr[..., None] * b_re - qi[..., None] * b_im
    bbi = qr[..., None] * b_im + qi[..., None] * b_re
    return lr, li, bbr, bbi


def _halves(a):
    return a.reshape((2, HALF_GROUPS) + a.shape[1:])


def _block_diag_mask(g, r, c):
    rows = lax.broadcasted_iota(jnp.int32, (g * r, g * c), 0) // r
    cols = lax.broadcasted_iota(jnp.int32, (g * r, g * c), 1) // c
    return rows == cols


def _block_diag(blocks):
    g, r, c = blocks.shape
    spread = jnp.tile(jnp.eye(c, dtype=blocks.dtype), (1, g))
    full = jnp.dot(blocks.reshape(g * r, c), spread, precision=lax.Precision.HIGHEST)
    return jnp.where(_block_diag_mask(g, r, c), full, 0.0)


def _block_diag_take(dense, g, r, c):
    gather = jnp.tile(jnp.eye(c, dtype=dense.dtype), (g, 1))
    kept = jnp.where(_block_diag_mask(g, r, c), dense, 0.0)
    return jnp.dot(kept, gather, precision=lax.Precision.HIGHEST).reshape(g, r, c)


def _ssm_matrices(bbr, bbi, c_re, c_im, glu_w, glu_b, d_skip):
    bre, bim = _halves(jnp.swapaxes(bbr, 1, 2)), _halves(jnp.swapaxes(bbi, 1, 2))
    bblk = jnp.stack([jnp.concatenate([_block_diag(bre[h]), _block_diag(bim[h])], axis=1) for h in range(2)])
    cre, cim = _halves(jnp.swapaxes(c_re, 1, 2)), _halves(jnp.swapaxes(c_im, 1, 2))
    cblk = jnp.stack([jnp.concatenate([_block_diag(cre[h]), -_block_diag(cim[h])], axis=0) for h in range(2)])
    glu = jnp.concatenate([_block_diag(glu_w[:, :, :SSM_CH]), _block_diag(glu_w[:, :, SSM_CH:])], axis=1)
    glub = jnp.concatenate([glu_b[:, :SSM_CH].reshape(1, -1), glu_b[:, SSM_CH:].reshape(1, -1)], axis=1)
    return bblk.astype(BF16), cblk.astype(BF16), glu.astype(BF16), glub, d_skip.reshape(1, -1)


def _scan_constants(lr, li, reverse):
    if reverse:
        li = -li
    pows = [(lr, li)]
    for _ in range(SUBLANES - 1):
        pr, pi = pows[-1]
        pows.append((pr * lr - pi * li, pr * li + pi * lr))
    row = jnp.arange(SUBLANES)[:, None]

    def flat(a):
        return a.reshape(2, 1, HALF_ST)

    mats = []
    for s in (1, 2, 4):
        keep = (row + s <= SUBLANES - 1) if reverse else (row >= s)
        mats.append(tuple(jnp.where(keep[None], flat(p), 0.0) for p in pows[s - 1]))
    order = [SUBLANES - 1 - j for j in range(SUBLANES)] if reverse else list(range(SUBLANES))
    mats.append(tuple(jnp.concatenate([flat(pows[j][k]) for j in order], axis=1) for k in range(2)))
    return jnp.stack([jnp.concatenate([m[0], m[1]], axis=2) for m in mats], axis=1)


def _pack(arrs, rows):
    flat = jnp.concatenate([a.reshape(-1) for a in arrs])
    return jnp.pad(flat, (0, rows * LANES - flat.shape[0])).reshape(rows, LANES)


def _unpack(buf, like):
    flat = buf.reshape(-1)
    out, off = [], 0
    for a in like:
        out.append(flat[off:off + a.size].reshape(a.shape))
        off += a.size
    return out


SMALL = ("norm_ffn1", "norm_mix", "ssm_a_re", "ssm_a_im", "ssm_log_dt", "ssm_b_re", "ssm_b_im", "ssm_c_re",
         "ssm_c_im", "ssm_d", "ssm_glu_w", "ssm_glu_b", "gm_v_gain", "gm_w_s", "gm_b_s", "gain_ssm_out",
         "gain_gm_out", "norm_ffn2", "norm_final")
BIG = ("ffn1_w_in", "ffn1_w_out", "mix_w_in", "mix_w_out", "ffn2_w_in", "ffn2_w_out")
TRANSPOSED = ("ffn1_w_in", "mix_w_in", "ffn2_w_in")
WEIGHTS = ("norm_ffn1", "ffn1_w_in", "ffn1_w_out", "norm_mix", "mix_w_in", "ssm_a_re", "ssm_a_im", "ssm_log_dt",
           "ssm_b_re", "ssm_b_im", "ssm_c_re", "ssm_c_im", "ssm_d", "ssm_glu_w", "ssm_glu_b", "gm_v_gain", "gm_w_s",
           "gm_b_s", "gain_ssm_out", "gain_gm_out", "mix_w_out", "norm_ffn2", "ffn2_w_in", "ffn2_w_out", "norm_final")


def _step(x, target, w, m, v):
    batch, seq, _ = x.shape
    n = batch * seq
    depth = w["norm_ffn1"].shape[0]
    tm = min(512, n)
    tm_ffn = min(1024, n)
    tk = min(4096, n)
    t_chunk = min(256, seq)
    t_chunk_fwd = min(2 * t_chunk, seq)
    gm_rows = min(1024, seq)
    x = x.reshape(n, D_MODEL)
    target = target.reshape(n, D_MODEL)

    assert depth == 2
    me = _index(_my_place())
    shard = lambda group, l: [w[f"{group}_w_in"][l].astype(BF16), w[f"{group}_w_out"][l].astype(BF16)]
    batches = ([("mix", 0), ("ffn2", 0)], [("ffn1", 1), ("mix", 1)], [("ffn2", 1)])
    gathered, pending = {("ffn1", 0): tuple(_all_gather(shard("ffn1", 0), "all_gather_first"))}, {}

    def gather_start(i, after):
        handles, tok = _exchange_start([shard(g, l) for g, l in batches[i]], me, False, f"all_gather_start_{i}",
                                       after)
        pending.update(zip(batches[i], handles))
        return tok

    def weights(group, l, after=None):
        if (group, l) not in gathered:
            w_in, w_out = _exchange_wait(pending[(group, l)], after, False, f"all_gather_wait_{group}_{l}")
            if group == "mix":
                w_in = jnp.transpose(w_in, (1, 0, 2)).reshape(D_MODEL, IN_COLS)
                w_out = w_out.reshape(D_MODEL, D_MODEL)
            gathered[(group, l)] = (w_in, w_out)
        return gathered[(group, l)]

    tril = jnp.tril(jnp.ones((GM_CHUNK, GM_CHUNK), bool))
    layers = []
    for l in range(depth):
        disc, disc_vjp = jax.vjp(_ssm_discretize, w["ssm_a_re"][l], w["ssm_a_im"][l], w["ssm_log_dt"][l],
                                 w["ssm_b_re"][l], w["ssm_b_im"][l])
        lr, li, bbr, bbi = disc
        bblk, cblk, glu, glub, dskip = _ssm_matrices(bbr, bbi, w["ssm_c_re"][l], w["ssm_c_im"][l],
                                                     w["ssm_glu_w"][l], w["ssm_glu_b"][l], w["ssm_d"][l])
        layers.append(dict(
            disc_vjp=disc_vjp, lr=lr, li=li, bblk=bblk, cblk=cblk, glu=glu, glub=glub, dskip=dskip,
            fwdc=_scan_constants(lr, li, False), revc=_scan_constants(lr, li, True),
            w_tril=jnp.where(tril[None], w["gm_w_s"][l], 0.0).astype(BF16),
            gm_bias=jnp.repeat(w["gm_b_s"][l].T, GM_HEAD_DIM, axis=1),
            g1=w["norm_ffn1"][l][None], gmix=w["norm_mix"][l][None], g2=w["norm_ffn2"][l][None],
            gv=w["gm_v_gain"][l][None], gs=w["gain_ssm_out"][l][None], gg=w["gain_gm_out"][l][None],
        ))

    saved = []
    for l in range(depth):
        p = layers[l]
        x0 = x
        g1, gmix, g2 = p["g1"], p["gmix"], p["g2"]
        w_in, w_out = weights("ffn1", l, x0)
        if l == 0:
            g1 = _behind(g1, gather_start(0, w_in))
        x1, xn1, gu1 = _ffn_fwd(x0, g1, w_in, w_out, tm_ffn, f"ffn1_fwd_{l}")
        if l == 0:
            gmix = _behind(gmix, gather_start(1, x1))
        mwi, mwo = weights("mix", l, x1)
        z = _mix_in_fwd(x1, gmix, mwi, tm_ffn, f"mix_in_fwd_{l}")
        y_ssm, h = _ssm_fwd(z, p["bblk"], p["cblk"], p["glu"], p["glub"], p["dskip"], p["fwdc"], batch, t_chunk_fwd,
                            f"ssm_fwd_{l}")
        y_gm = _gm_fwd(z, p["gv"], p["w_tril"], p["gm_bias"], gm_rows, f"gm_fwd_{l}")
        x2 = _mix_out_fwd(y_ssm, y_gm, p["gs"], p["gg"], mwo, x1, tm_ffn, f"mix_out_fwd_{l}")
        if l == 0:
            g2 = _behind(g2, gather_start(2, x2))
        x, xn2, gu2 = _ffn_fwd(x2, g2, *weights("ffn2", l, x2), tm_ffn, f"ffn2_fwd_{l}")
        saved.append((x0, x1, x2, z, h, y_ssm, y_gm, xn1, gu1, xn2, gu2))

    dx, sq, dnf = _loss_head(x, w["norm_final"][None], target, tm_ffn, "loss_head")
    loss_part = ((0.5 / D_MODEL) * jnp.sum(sq)).reshape(1)

    small = {k: [None] * depth for k in SMALL if k != "norm_final"}
    sent = []

    def send(group, l, keys, parts):
        (handle,), tok = _exchange_start([parts], me, True, f"reduce_scatter_start_{group}_{l}")
        sent.append((group, l, keys, handle))
        return tok

    token = None
    for l in reversed(range(depth)):
        p = layers[l]
        x0, x1, x2, z, h, y_ssm, y_gm, xn1, gu1, xn2, gu2 = saved[l]
        mwi, mwo = weights("mix", l)
        dx_out = dx
        g2 = p["g2"] if token is None else _behind(p["g2"], token)
        dx, dgu, act, dgain, dyb = _ffn_bwd(x2, g2, dx_out, gu2, *weights("ffn2", l), tm, f"ffn2_bwd_{l}")
        dw_in = _ffn_dw_in(xn2, dgu, tk, f"ffn2_dw_in_{l}")
        dw_out = _ffn_dw_out(act, dyb, tk, f"ffn2_dw_out_{l}").reshape(N_DEV, FF_SHARD // 2, D_MODEL)
        token = send("ffn2", l, ("ffn2_w_in", "ffn2_w_out"), [dw_in, dw_out])
        small["norm_ffn2"][l] = dgain.sum(0)

        dy_ssm, dy_gm, dwo, dgs, dgg = _mix_out_bwd(y_ssm, y_gm, _behind(p["gs"], token), p["gg"], mwo, dx, tm,
                                                    f"mix_out_bwd_{l}")
        dwo = dwo.astype(BF16).reshape(N_DEV, D_MODEL // N_DEV, D_MODEL)
        small["gain_ssm_out"][l] = dgs.sum(0)
        small["gain_gm_out"][l] = dgg.sum(0)

        du_ssm, dglu, dglub, ddskip, dct, db, q = _ssm_bwd(
            z, h, dy_ssm, p["bblk"], p["cblk"], p["glu"], p["glub"], p["dskip"], p["revc"], batch, t_chunk,
            f"ssm_bwd_{l}")
        du_gm, dv_gm, dws, dbias, dgv = _gm_bwd(z, dy_gm, p["gv"], p["w_tril"], p["gm_bias"], gm_rows, f"gm_bwd_{l}")

        q = q.sum(0).reshape(2, 2, HALF_GROUPS, SSM_STATE)
        qr, qi = q[:, 0].reshape(SSM_GROUPS, SSM_STATE), q[:, 1].reshape(SSM_GROUPS, SSM_STATE)
        den = p["lr"] * p["lr"] + p["li"] * p["li"]
        d_re = (qr * p["lr"] + qi * p["li"]) / den
        d_im = (qi * p["lr"] - qr * p["li"]) / den
        dbb = jnp.stack([_block_diag_take(db[hf, :, k * HALF_ST:(k + 1) * HALF_ST], HALF_GROUPS, SSM_CH, SSM_STATE)
                         for k in range(2) for hf in range(2)]).reshape(2, SSM_GROUPS, SSM_CH, SSM_STATE)
        dcc = jnp.stack([_block_diag_take(dct[hf, :, k * HALF_ST:(k + 1) * HALF_ST], HALF_GROUPS, SSM_CH, SSM_STATE)
                         for k in range(2) for hf in range(2)]).reshape(2, SSM_GROUPS, SSM_CH, SSM_STATE)
        da_re, da_im, dlog_dt, db_re, db_im = p["disc_vjp"](
            (d_re, -d_im, jnp.swapaxes(dbb[0], 1, 2), jnp.swapaxes(dbb[1], 1, 2)))
        small["ssm_a_re"][l], small["ssm_a_im"][l], small["ssm_log_dt"][l] = da_re, da_im, dlog_dt
        small["ssm_b_re"][l], small["ssm_b_im"][l] = db_re, db_im
        small["ssm_c_re"][l], small["ssm_c_im"][l] = dcc[0], -dcc[1]
        small["ssm_d"][l] = ddskip.sum(0).reshape(SSM_GROUPS, SSM_CH)
        small["ssm_glu_w"][l] = jnp.concatenate(
            [_block_diag_take(dglu[:, :SSM_WIDTH], SSM_GROUPS, SSM_CH, SSM_CH),
             _block_diag_take(dglu[:, SSM_WIDTH:], SSM_GROUPS, SSM_CH, SSM_CH)], axis=2)
        dglub = dglub.sum(0)
        small["ssm_glu_b"][l] = jnp.concatenate(
            [dglub[:SSM_WIDTH].reshape(SSM_GROUPS, SSM_CH), dglub[SSM_WIDTH:].reshape(SSM_GROUPS, SSM_CH)], axis=1)
        small["gm_v_gain"][l] = dgv.sum(0)
        small["gm_w_s"][l] = jnp.where(tril[None], dws, 0.0)
        small["gm_b_s"][l] = dbias.reshape(GM_CHUNK, GM_HEADS, GM_HEAD_DIM).sum(-1).T

        dx, dwi, dgain = _mix_in_bwd(x1, p["gmix"], du_ssm, du_gm, dv_gm, dx, mwi, tm, f"mix_in_bwd_{l}")
        dwi = dwi.astype(BF16).reshape(N_DEV, IN_COLS // N_DEV, D_MODEL)
        token = send("mix", l, ("mix_w_in", "mix_w_out"), [dwi, dwo])
        small["norm_mix"][l] = dgain.sum(0)

        dx_out = dx
        dx, dgu, act, dgain, dyb = _ffn_bwd(x0, _behind(p["g1"], token), dx_out, gu1, *weights("ffn1", l), tm,
                                       f"ffn1_bwd_{l}")
        small["norm_ffn1"][l] = dgain.sum(0)
        if l > 0:
            dw_in = _ffn_dw_in(xn1, dgu, tk, f"ffn1_dw_in_{l}")
            dw_out = _ffn_dw_out(act, dyb, tk, f"ffn1_dw_out_{l}").reshape(N_DEV, FF_SHARD // 2, D_MODEL)
            token = send("ffn1", l, ("ffn1_w_in", "ffn1_w_out"), [dw_in, dw_out])
            continue
        small_g = [jnp.stack(small[k]) if k != "norm_final" else dnf.sum(0) for k in SMALL]
        total = sum(int(math.prod(w[k].shape)) for k in SMALL)
        rows = -(-total // (LANES * N_DEV * SUBLANES)) * N_DEV * SUBLANES
        assert rows * LANES > total
        g_all = _all_reduce_small(_pack(small_g + [loss_part], rows).reshape(N_DEV, rows // N_DEV, LANES),
                                  "all_reduce_small")
        loss = g_all.reshape(-1)[total]
        dw_in = _ffn_dw_in(xn1, dgu, tk, f"ffn1_dw_in_{l}", after=g_all)
        token = send("ffn1_in", l, ("ffn1_w_in",), [dw_in])
        dw_out = _ffn_dw_out(act, dyb, tk, f"ffn1_dw_out_{l}", after=token).reshape(
            N_DEV, FF_SHARD // 2, D_MODEL)
        token = send("ffn1_out", l, ("ffn1_w_out",), [dw_out])

    grad_x = dx.reshape(batch, seq, D_MODEL)
    grads, deltas, new_m, new_v = {}, {}, {}, {}

    g_all = _behind(g_all.reshape(rows, LANES), token)
    like = [w[k] for k in SMALL]
    d_p, m_p, v_p = _adam_packed(g_all, _pack(like, rows), _pack([m[k] for k in SMALL], rows),
                                 _pack([v[k] for k in SMALL], rows), "adam_small")
    for k, g_, d_, m_, v_ in zip(SMALL, _unpack(g_all, like), _unpack(d_p, like), _unpack(m_p, like),
                                 _unpack(v_p, like)):
        grads[k], deltas[k], new_m[k], new_v[k] = g_, d_, m_, v_

    results = {}
    after = d_p
    for group, l, keys, handle in sent:
        landed = _exchange_wait(handle, after, True, f"reduce_scatter_wait_{group}_{l}")
        for k, parts in zip(keys, landed):
            view = (lambda a: jnp.swapaxes(a, 1, 2)) if k in TRANSPOSED else (lambda a: a)
            results[k] = _adam_sharded(parts, view(w[k]), view(m[k]), view(v[k]), l, results.get(k),
                                       f"adam_{k}_{l}")
            after = results[k][0]
    for k in BIG:
        view = (lambda a: jnp.swapaxes(a, 1, 2)) if k in TRANSPOSED else (lambda a: a)
        grads[k], deltas[k], new_m[k], new_v[k] = [view(a) for a in results[k]]
    return loss, grad_x, grads, deltas, new_m, new_v


def kernel(x, norm_ffn1, ffn1_w_in, ffn1_w_out, norm_mix, mix_w_in, ssm_a_re, ssm_a_im, ssm_log_dt, ssm_b_re, ssm_b_im, ssm_c_re, ssm_c_im, ssm_d, ssm_glu_w, ssm_glu_b, gm_v_gain, gm_w_s, gm_b_s, gain_ssm_out, gain_gm_out, mix_w_out, norm_ffn2, ffn2_w_in, ffn2_w_out, norm_final, loss_target, m_norm_ffn1, m_ffn1_w_in, m_ffn1_w_out, m_norm_mix, m_mix_w_in, m_ssm_a_re, m_ssm_a_im, m_ssm_log_dt, m_ssm_b_re, m_ssm_b_im, m_ssm_c_re, m_ssm_c_im, m_ssm_d, m_ssm_glu_w, m_ssm_glu_b, m_gm_v_gain, m_gm_w_s, m_gm_b_s, m_gain_ssm_out, m_gain_gm_out, m_mix_w_out, m_norm_ffn2, m_ffn2_w_in, m_ffn2_w_out, m_norm_final, v_norm_ffn1, v_ffn1_w_in, v_ffn1_w_out, v_norm_mix, v_mix_w_in, v_ssm_a_re, v_ssm_a_im, v_ssm_log_dt, v_ssm_b_re, v_ssm_b_im, v_ssm_c_re, v_ssm_c_im, v_ssm_d, v_ssm_glu_w, v_ssm_glu_b, v_gm_v_gain, v_gm_w_s, v_gm_b_s, v_gain_ssm_out, v_gain_gm_out, v_mix_w_out, v_norm_ffn2, v_ffn2_w_in, v_ffn2_w_out, v_norm_final):
    args = locals()
    w = {k: args[k] for k in WEIGHTS}
    m = {k: args["m_" + k] for k in WEIGHTS}
    v = {k: args["v_" + k] for k in WEIGHTS}
    loss, grad_x, grads, deltas, new_m, new_v = _step(x, loss_target, w, m, v)
    return (loss, grad_x, *[grads[k] for k in WEIGHTS], *[deltas[k] for k in WEIGHTS],
            *[new_m[k] for k in WEIGHTS], *[new_v[k] for k in WEIGHTS])
```

```python
import functools
import math

import jax
import jax.numpy as jnp
from jax import lax
from jax.experimental import pallas as pl
from jax.experimental.pallas import tpu as pltpu

F32 = jnp.float32
BF16 = jnp.bfloat16
MESH = pl.DeviceIdType.MESH
AXES = ("x", "y", "c")

N_DEV = 8
D_MODEL = 1024
D_FF = 2816
FF_SHARD = 2 * D_FF // N_DEV
FF_CHUNKS = 4
MXU_DIM = 256
FF_PIECES = tuple((lo, min(lo + MXU_DIM, FF_SHARD)) for lo in range(0, FF_SHARD, MXU_DIM))
SSM_WIDTH = 512
SSM_CH = 16
SSM_GROUPS = 32
SSM_STATE = 64
HALF_GROUPS = 16
HALF_IN = HALF_GROUPS * SSM_CH
HALF_ST = HALF_GROUPS * SSM_STATE
GM_WIDTH = 512
GM_HEADS = 4
GM_HEAD_DIM = 128
GM_CHUNK = 128
IN_COLS = SSM_WIDTH + 2 * GM_WIDTH
EPS = 1e-6
SUBLANES = 8
LANES = 128

ADAM_LR = 0.001
ADAM_B1 = 0.9
ADAM_B2 = 0.999
ADAM_EPS = 1e-08
ADAM_WD = 0.01
ADAM_STEP = 10

VMEM_LIMIT = 46 * 1024 * 1024


def _cp(*sem):
    return pltpu.CompilerParams(dimension_semantics=sem, vmem_limit_bytes=VMEM_LIMIT)


def _rms_fwd(x, g):
    r = lax.rsqrt(jnp.mean(x * x, axis=-1, keepdims=True) + EPS)
    xh = x * r
    return xh * g, xh, r


def _rms_bwd(dy, xh, r, g):
    dxh = dy * g
    dx = r * (dxh - xh * jnp.mean(dxh * xh, axis=-1, keepdims=True))
    return dx, dy * xh


def _rows8(a):
    m, n = a.shape
    return a.reshape(m // SUBLANES, SUBLANES, n).sum(axis=0)


_GELU_K = math.sqrt(2.0 / math.pi)
_GELU_C = 0.044715


def _gelu(x):
    th = jnp.tanh(_GELU_K * (x + _GELU_C * x * x * x))
    return 0.5 * x * (1.0 + th), th


def _gelu_grad(x, th):
    return 0.5 * (1.0 + th) + 0.5 * x * (1.0 - th * th) * (_GELU_K * (1.0 + 3.0 * _GELU_C * x * x))


def _dot(a, b):
    return jnp.dot(a, b, preferred_element_type=F32)


def _dot_nt(a, b):
    return lax.dot_general(a, b, (((1,), (1,)), ((), ())), preferred_element_type=F32)


def _dot_tn(a, b):
    return lax.dot_general(a, b, (((0,), (0,)), ((), ())), preferred_element_type=F32)


def _ffn_fwd(x, gain, w_in_ag, w_out_ag, tm, name):
    n = x.shape[0]

    def body(x_ref, g_ref, wg_ref, wu_ref, wo_ref, o_ref, xn_ref, gu_ref):
        j = pl.program_id(1)

        @pl.when(j == 0)
        def _():
            xv = x_ref[...]
            y, _, _ = _rms_fwd(xv, g_ref[...])
            xn_ref[...] = y.astype(BF16)
            o_ref[...] = xv

        xn = xn_ref[...]
        wo = wo_ref[...].reshape(FF_SHARD, D_MODEL)
        out = None
        for lo, hi in FF_PIECES:
            gg = _dot(xn, wg_ref[:, lo:hi])
            uu = _dot(xn, wu_ref[:, lo:hi])
            gu_ref[0, :, lo:hi] = gg.astype(BF16)
            gu_ref[1, :, lo:hi] = uu.astype(BF16)
            act = (gg * jax.nn.sigmoid(gg) * uu).astype(BF16)
            part = _dot(act, wo[lo:hi, :])
            out = part if out is None else out + part
        o_ref[...] += 0.5 * out

    return pl.pallas_call(
        body, name=name, grid=(n // tm, FF_CHUNKS),
        in_specs=[
            pl.BlockSpec((tm, D_MODEL), lambda i, j: (i, 0)),
            pl.BlockSpec((1, D_MODEL), lambda i, j: (0, 0)),
            pl.BlockSpec((None, D_MODEL, FF_SHARD), lambda i, j: (j, 0, 0)),
            pl.BlockSpec((None, D_MODEL, FF_SHARD), lambda i, j: (j + FF_CHUNKS, 0, 0)),
            pl.BlockSpec((2, FF_SHARD // 2, D_MODEL), lambda i, j: (j, 0, 0)),
        ],
        out_specs=[
            pl.BlockSpec((tm, D_MODEL), lambda i, j: (i, 0)),
            pl.BlockSpec((tm, D_MODEL), lambda i, j: (i, 0)),
            pl.BlockSpec((None, 2, tm, FF_SHARD), lambda i, j: (j, 0, i, 0)),
        ],
        out_shape=[
            jax.ShapeDtypeStruct((n, D_MODEL), F32),
            jax.ShapeDtypeStruct((n, D_MODEL), BF16),
            jax.ShapeDtypeStruct((FF_CHUNKS, 2, n, FF_SHARD), BF16),
        ],
        compiler_params=_cp("parallel", "arbitrary"),
    )(x, gain, w_in_ag, w_in_ag, w_out_ag)


def _ffn_bwd(x, gain, dy, gu, w_in_ag, w_out_ag, tm, name):
    n = x.shape[0]
    work = (n // tm) * FF_CHUNKS
    steps = work + 1
    first = lambda s: jnp.minimum(s, work - 1)
    second = lambda s: jnp.maximum(s - 1, 0)
    tile1, chunk1 = (lambda s: first(s) // FF_CHUNKS), (lambda s: first(s) % FF_CHUNKS)
    tile2, chunk2 = (lambda s: second(s) // FF_CHUNKS), (lambda s: second(s) % FF_CHUNKS)

    def body(x_ref, g_ref, dy_ref, dy1_ref, gu_ref, wg_ref, wu_ref, wo_ref, dx_ref, dgu_ref, act_ref, dgain_ref,
             dyb_ref, held_ref):
        s = pl.program_id(0)

        @pl.when(s == 0)
        def _():
            dgain_ref[...] = jnp.zeros_like(dgain_ref)
            held_ref[...] = jnp.zeros_like(held_ref)

        @pl.when(jnp.logical_and(chunk1(s) == 0, s < work))
        def _():
            dyb_ref[...] = (0.5 * dy1_ref[...]).astype(BF16)

        @pl.when(chunk2(s) == 0)
        def _():
            dx_ref[...] = jnp.zeros_like(dx_ref)

        held = held_ref[1 - s % 2]
        part = _dot_nt(held[0], wg_ref[...]) + _dot_nt(held[1], wu_ref[...])
        dx_ref[...] += jnp.where(s > 0, part, 0.0)

        dyb = dyb_ref[...]
        wo = wo_ref[...].reshape(FF_SHARD, D_MODEL)
        slot = s % 2
        for lo, hi in FF_PIECES:
            gg = gu_ref[0, :, lo:hi].astype(F32)
            uu = gu_ref[1, :, lo:hi].astype(F32)
            dact = _dot_nt(dyb, wo[lo:hi, :])
            sig = jax.nn.sigmoid(gg)
            silu = gg * sig
            act_ref[:, lo:hi] = (silu * uu).astype(BF16)
            du = (dact * silu).astype(BF16)
            dg = (dact * uu * (sig * (1.0 + gg * (1.0 - sig)))).astype(BF16)
            dgu_ref[0, :, lo:hi] = dg
            dgu_ref[1, :, lo:hi] = du
            held_ref[slot, 0, :, lo:hi] = dg
            held_ref[slot, 1, :, lo:hi] = du

        @pl.when(jnp.logical_and(chunk2(s) == FF_CHUNKS - 1, s > 0))
        def _():
            g = g_ref[...]
            _, xh, r = _rms_fwd(x_ref[...], g)
            dx, dgr = _rms_bwd(dx_ref[...], xh, r, g)
            dx_ref[...] = dy_ref[...] + dx
            dgain_ref[...] += _rows8(dgr)

    return pl.pallas_call(
        body, name=name, grid=(steps,),
        in_specs=[
            pl.BlockSpec((tm, D_MODEL), lambda s: (tile2(s), 0)),
            pl.BlockSpec((1, D_MODEL), lambda s: (0, 0)),
            pl.BlockSpec((tm, D_MODEL), lambda s: (tile2(s), 0)),
            pl.BlockSpec((tm, D_MODEL), lambda s: (tile1(s), 0)),
            pl.BlockSpec((None, 2, tm, FF_SHARD), lambda s: (chunk1(s), 0, tile1(s), 0)),
            pl.BlockSpec((None, D_MODEL, FF_SHARD), lambda s: (chunk2(s), 0, 0)),
            pl.BlockSpec((None, D_MODEL, FF_SHARD), lambda s: (chunk2(s) + FF_CHUNKS, 0, 0)),
            pl.BlockSpec((2, FF_SHARD // 2, D_MODEL), lambda s: (chunk1(s), 0, 0)),
        ],
        out_specs=[
            pl.BlockSpec((tm, D_MODEL), lambda s: (tile2(s), 0)),
            pl.BlockSpec((None, 2, tm, FF_SHARD), lambda s: (chunk1(s), 0, tile1(s), 0)),
            pl.BlockSpec((None, tm, FF_SHARD), lambda s: (chunk1(s), tile1(s), 0)),
            pl.BlockSpec((SUBLANES, D_MODEL), lambda s: (0, 0)),
            pl.BlockSpec((tm, D_MODEL), lambda s: (tile1(s), 0)),
        ],
        out_shape=[
            jax.ShapeDtypeStruct((n, D_MODEL), F32),
            jax.ShapeDtypeStruct((FF_CHUNKS, 2, n, FF_SHARD), BF16),
            jax.ShapeDtypeStruct((FF_CHUNKS, n, FF_SHARD), BF16),
            jax.ShapeDtypeStruct((SUBLANES, D_MODEL), F32),
            jax.ShapeDtypeStruct((n, D_MODEL), BF16),
        ],
        scratch_shapes=[pltpu.VMEM((2, 2, tm, FF_SHARD), BF16)],
        compiler_params=_cp("arbitrary"),
    )(x, gain, dy, dy, gu, w_in_ag, w_in_ag, w_out_ag)


def _ffn_dw_in(xn, dgu, tk, name, after=None):
    n = xn.shape[0]
    nk = n // tk
    deps = [] if after is None else [after]

    def body(a_ref, b_ref, *rest):
        o_ref, acc_ref = rest[-2:]
        k = pl.program_id(2)

        @pl.when(k == 0)
        def _():
            acc_ref[...] = jnp.zeros_like(acc_ref)

        acc_ref[...] += _dot_tn(b_ref[...], a_ref[...])

        @pl.when(k == nk - 1)
        def _():
            o_ref[...] = acc_ref[...].astype(BF16)

    return pl.pallas_call(
        body, name=name, grid=(FF_CHUNKS, 2, nk),
        in_specs=[
            pl.BlockSpec((tk, D_MODEL), lambda j, p, k: (k, 0)),
            pl.BlockSpec((None, None, tk, FF_SHARD), lambda j, p, k: (j, p, k, 0)),
        ] + [pl.BlockSpec(memory_space=pl.ANY)] * len(deps),
        out_specs=pl.BlockSpec((None, FF_SHARD, D_MODEL), lambda j, p, k: (FF_CHUNKS * p + j, 0, 0)),
        out_shape=jax.ShapeDtypeStruct((N_DEV, FF_SHARD, D_MODEL), BF16),
        scratch_shapes=[pltpu.VMEM((FF_SHARD, D_MODEL), F32)],
        compiler_params=_cp("parallel", "parallel", "arbitrary"),
    )(xn, dgu, *deps)


def _ffn_dw_out(act, dyb, tk, name, after=None):
    n = act.shape[1]
    nk = n // tk
    deps = [] if after is None else [after]

    def body(a_ref, b_ref, *rest):
        o_ref, acc_ref = rest[-2:]
        k = pl.program_id(1)

        @pl.when(k == 0)
        def _():
            acc_ref[...] = jnp.zeros_like(acc_ref)

        acc_ref[...] += _dot_tn(a_ref[...], b_ref[...])

        @pl.when(k == nk - 1)
        def _():
            o_ref[...] = acc_ref[...].astype(BF16)

    return pl.pallas_call(
        body, name=name, grid=(FF_CHUNKS, nk),
        in_specs=[
            pl.BlockSpec((None, tk, FF_SHARD), lambda j, k: (j, k, 0)),
            pl.BlockSpec((tk, D_MODEL), lambda j, k: (k, 0)),
        ] + [pl.BlockSpec(memory_space=pl.ANY)] * len(deps),
        out_specs=pl.BlockSpec((None, FF_SHARD, D_MODEL), lambda j, k: (j, 0, 0)),
        out_shape=jax.ShapeDtypeStruct((FF_CHUNKS, FF_SHARD, D_MODEL), BF16),
        scratch_shapes=[pltpu.VMEM((FF_SHARD, D_MODEL), F32)],
        compiler_params=_cp("parallel", "arbitrary"),
    )(act, dyb, *deps)


def _mix_in_fwd(x, gain, w, tm, name):
    n = x.shape[0]

    def body(x_ref, g_ref, w_ref, z_ref):
        y, _, _ = _rms_fwd(x_ref[...], g_ref[...])
        z_ref[...] = _dot(y.astype(BF16), w_ref[...])

    return pl.pallas_call(
        body, name=name, grid=(n // tm,),
        in_specs=[
            pl.BlockSpec((tm, D_MODEL), lambda i: (i, 0)),
            pl.BlockSpec((1, D_MODEL), lambda i: (0, 0)),
            pl.BlockSpec((D_MODEL, IN_COLS), lambda i: (0, 0)),
        ],
        out_specs=pl.BlockSpec((tm, IN_COLS), lambda i: (i, 0)),
        out_shape=jax.ShapeDtypeStruct((n, IN_COLS), F32),
        compiler_params=_cp("parallel"),
    )(x, gain, w)


def _mix_in_bwd(x, gain, du_ssm, du_gm, dv_gm, d_res, w, tm, name):
    n = x.shape[0]

    def body(x_ref, g_ref, d0_ref, d1_ref, d2_ref, dres_ref, w_ref, dx_ref, dw_ref, dgain_ref):
        i = pl.program_id(0)

        @pl.when(i == 0)
        def _():
            dw_ref[...] = jnp.zeros_like(dw_ref)
            dgain_ref[...] = jnp.zeros_like(dgain_ref)

        g = g_ref[...]
        y, xh, r = _rms_fwd(x_ref[...], g)
        xn = y.astype(BF16)
        dxn = jnp.zeros((tm, D_MODEL), F32)
        for k, d_ref in enumerate((d0_ref, d1_ref, d2_ref)):
            dz = d_ref[...].astype(BF16)
            cols = slice(k * SSM_WIDTH, (k + 1) * SSM_WIDTH)
            dxn += _dot_nt(dz, w_ref[:, cols])
            dw_ref[cols, :] += _dot_tn(dz, xn)
        dx, dgr = _rms_bwd(dxn, xh, r, g)
        dx_ref[...] = dres_ref[...] + dx
        dgain_ref[...] += _rows8(dgr)

    row = lambda i: (i, 0)
    fixed = lambda i: (0, 0)
    return pl.pallas_call(
        body, name=name, grid=(n // tm,),
        in_specs=[
            pl.BlockSpec((tm, D_MODEL), row),
            pl.BlockSpec((1, D_MODEL), fixed),
            pl.BlockSpec((tm, SSM_WIDTH), row),
            pl.BlockSpec((tm, GM_WIDTH), row),
            pl.BlockSpec((tm, GM_WIDTH), row),
            pl.BlockSpec((tm, D_MODEL), row),
            pl.BlockSpec((D_MODEL, IN_COLS), fixed),
        ],
        out_specs=[
            pl.BlockSpec((tm, D_MODEL), row),
            pl.BlockSpec((IN_COLS, D_MODEL), fixed),
            pl.BlockSpec((SUBLANES, D_MODEL), fixed),
        ],
        out_shape=[
            jax.ShapeDtypeStruct((n, D_MODEL), F32),
            jax.ShapeDtypeStruct((IN_COLS, D_MODEL), F32),
            jax.ShapeDtypeStruct((SUBLANES, D_MODEL), F32),
        ],
        compiler_params=_cp("arbitrary"),
    )(x, gain, du_ssm, du_gm, dv_gm, d_res, w)


def _mix_out_fwd(y_ssm, y_gm, g_ssm, g_gm, w, x, tm, name):
    n = x.shape[0]

    def body(ys_ref, yg_ref, gs_ref, gg_ref, w_ref, x_ref, o_ref):
        a, _, _ = _rms_fwd(ys_ref[...], gs_ref[...])
        b, _, _ = _rms_fwd(yg_ref[...], gg_ref[...])
        o_ref[...] = (x_ref[...] + _dot(a.astype(BF16), w_ref[0:SSM_WIDTH, :])
                      + _dot(b.astype(BF16), w_ref[SSM_WIDTH:D_MODEL, :]))

    row = lambda i: (i, 0)
    fixed = lambda i: (0, 0)
    return pl.pallas_call(
        body, name=name, grid=(n // tm,),
        in_specs=[
            pl.BlockSpec((tm, SSM_WIDTH), row), pl.BlockSpec((tm, GM_WIDTH), row),
            pl.BlockSpec((1, SSM_WIDTH), fixed), pl.BlockSpec((1, GM_WIDTH), fixed),
            pl.BlockSpec((D_MODEL, D_MODEL), fixed), pl.BlockSpec((tm, D_MODEL), row),
        ],
        out_specs=pl.BlockSpec((tm, D_MODEL), row),
        out_shape=jax.ShapeDtypeStruct((n, D_MODEL), F32),
        compiler_params=_cp("parallel"),
    )(y_ssm, y_gm, g_ssm, g_gm, w, x)


def _mix_out_bwd(y_ssm, y_gm, g_ssm, g_gm, w, dx, tm, name):
    n = dx.shape[0]

    def body(ys_ref, yg_ref, gs_ref, gg_ref, w_ref, dx_ref, dys_ref, dyg_ref, dw_ref, dgs_ref, dgg_ref):
        i = pl.program_id(0)

        @pl.when(i == 0)
        def _():
            dw_ref[...] = jnp.zeros_like(dw_ref)
            dgs_ref[...] = jnp.zeros_like(dgs_ref)
            dgg_ref[...] = jnp.zeros_like(dgg_ref)

        dxb = dx_ref[...].astype(BF16)
        parts = ((ys_ref, gs_ref, dys_ref, dgs_ref, 0), (yg_ref, gg_ref, dyg_ref, dgg_ref, SSM_WIDTH))
        for y_ref, g_ref, dy_ref, dg_ref, off in parts:
            g = g_ref[...]
            yn, xh, r = _rms_fwd(y_ref[...], g)
            rows = slice(off, off + SSM_WIDTH)
            dyn = _dot_nt(dxb, w_ref[rows, :])
            dw_ref[rows, :] += _dot_tn(yn.astype(BF16), dxb)
            dy, dgr = _rms_bwd(dyn, xh, r, g)
            dy_ref[...] = dy
            dg_ref[...] += _rows8(dgr)

    row = lambda i: (i, 0)
    fixed = lambda i: (0, 0)
    return pl.pallas_call(
        body, name=name, grid=(n // tm,),
        in_specs=[
            pl.BlockSpec((tm, SSM_WIDTH), row), pl.BlockSpec((tm, GM_WIDTH), row),
            pl.BlockSpec((1, SSM_WIDTH), fixed), pl.BlockSpec((1, GM_WIDTH), fixed),
            pl.BlockSpec((D_MODEL, D_MODEL), fixed), pl.BlockSpec((tm, D_MODEL), row),
        ],
        out_specs=[
            pl.BlockSpec((tm, SSM_WIDTH), row), pl.BlockSpec((tm, GM_WIDTH), row),
            pl.BlockSpec((D_MODEL, D_MODEL), fixed),
            pl.BlockSpec((SUBLANES, SSM_WIDTH), fixed), pl.BlockSpec((SUBLANES, GM_WIDTH), fixed),
        ],
        out_shape=[
            jax.ShapeDtypeStruct((n, SSM_WIDTH), F32), jax.ShapeDtypeStruct((n, GM_WIDTH), F32),
            jax.ShapeDtypeStruct((D_MODEL, D_MODEL), F32),
            jax.ShapeDtypeStruct((SUBLANES, SSM_WIDTH), F32), jax.ShapeDtypeStruct((SUBLANES, GM_WIDTH), F32),
        ],
        compiler_params=_cp("arbitrary"),
    )(y_ssm, y_gm, g_ssm, g_gm, w, dx)


SCAN_W = 256
SCAN_PIECES = HALF_ST // SCAN_W


def _scan_tiles(src_ref, dst_ref, dst_off, c_ref, half, carry_ref, n_tiles, reverse, extra=None):
    shifts = (1, 2, 4)
    carry_row = 0 if reverse else SUBLANES - 1

    def cols(piece, im):
        lo = im * HALF_ST + piece * SCAN_W
        return slice(lo, lo + SCAN_W)

    def step(t, state):
        carries, accs = state
        k = (n_tiles - 1 - t) if reverse else t
        rows = slice(k * SUBLANES, (k + 1) * SUBLANES)
        new_carries, new_accs = [], []
        for piece in range(SCAN_PIECES):
            cr, ci = carries[piece]
            xr0 = src_ref[rows, cols(piece, 0)]
            xi0 = src_ref[rows, cols(piece, 1)]
            xr, xi = xr0, xi0
            for si, s in enumerate(shifts):
                ar = c_ref[half, si, :, cols(piece, 0)]
                ai = c_ref[half, si, :, cols(piece, 1)]
                sh = (SUBLANES - s) if reverse else s
                sr = pltpu.roll(xr, sh, 0)
                sm = pltpu.roll(xi, sh, 0)
                xr, xi = xr + (ar * sr - ai * sm), xi + (ar * sm + ai * sr)
            pr = c_ref[half, 3, :, cols(piece, 0)]
            pi = c_ref[half, 3, :, cols(piece, 1)]
            hr = xr + (pr * cr - pi * ci)
            hi = xi + (pr * ci + pi * cr)
            dst_ref[rows, pl.ds(dst_off + piece * SCAN_W, SCAN_W)] = hr
            dst_ref[rows, pl.ds(dst_off + HALF_ST + piece * SCAN_W, SCAN_W)] = hi
            new_carries.append((jnp.broadcast_to(hr[carry_row:carry_row + 1, :], (SUBLANES, SCAN_W)),
                                jnp.broadcast_to(hi[carry_row:carry_row + 1, :], (SUBLANES, SCAN_W))))
            if extra is not None:
                new_accs.append(extra(rows, piece, (xr0, xi0), (hr, hi), accs[piece]))
        return tuple(new_carries), tuple(new_accs)

    base = half * 2 * HALF_ST
    carries0 = tuple((carry_ref[:, pl.ds(base + p * SCAN_W, SCAN_W)],
                      carry_ref[:, pl.ds(base + HALF_ST + p * SCAN_W, SCAN_W)]) for p in range(SCAN_PIECES))
    zero = jnp.zeros((SUBLANES, SCAN_W), F32)
    accs0 = tuple((zero, zero) for _ in range(SCAN_PIECES)) if extra is not None else ()
    state = (carries0, accs0)
    for t in range(n_tiles):
        state = step(t, state)
    carries, accs = state
    for p in range(SCAN_PIECES):
        carry_ref[:, pl.ds(base + p * SCAN_W, SCAN_W)] = carries[p][0]
        carry_ref[:, pl.ds(base + HALF_ST + p * SCAN_W, SCAN_W)] = carries[p][1]
    return accs


def _ssm_tail(hb, u, c_ref, glu_ref, glub_ref, dskip_ref):
    ypre = u * dskip_ref[...]
    parts = []
    for half in range(2):
        parts.append(_dot(hb[half], c_ref[half]))
    ypre = ypre + jnp.concatenate(parts, axis=1)
    yg, th = _gelu(ypre)
    zz = _dot(yg.astype(BF16), glu_ref[...]) + glub_ref[...]
    z1, z2 = zz[:, :SSM_WIDTH], zz[:, SSM_WIDTH:]
    sg = jax.nn.sigmoid(z2)
    return ypre, th, yg, z1, sg


def _ssm_fwd(z, bblk, cblk, glu, glub, dskip, fwdc, batch, t_chunk, name):
    n = z.shape[0]
    nk = n // batch // t_chunk
    n_tiles = t_chunk // SUBLANES

    def body(u_ref, b_ref, c_ref, glu_ref, glub_ref, dskip_ref, k_ref, y_ref, h_ref, bu_ref, carry_ref):
        @pl.when(pl.program_id(1) == 0)
        def _():
            carry_ref[...] = jnp.zeros_like(carry_ref)

        u = u_ref[...]
        ub = u.astype(BF16)
        for half in range(2):
            bu_ref[half] = _dot(ub[:, half * HALF_IN:(half + 1) * HALF_IN], b_ref[half])
            _scan_tiles(bu_ref.at[half], h_ref, half * 2 * HALF_ST, k_ref, half, carry_ref, n_tiles, False)
        hb = [h_ref[:, half * 2 * HALF_ST:(half + 1) * 2 * HALF_ST].astype(BF16) for half in range(2)]
        _, _, _, z1, sg = _ssm_tail(hb, u, c_ref, glu_ref, glub_ref, dskip_ref)
        y_ref[...] = z1 * sg

    fixed2 = lambda b, k: (0, 0)
    fixed3 = lambda b, k: (0, 0, 0)
    row = lambda b, k: (b * nk + k, 0)
    return pl.pallas_call(
        body, name=name, grid=(batch, nk),
        in_specs=[
            pl.BlockSpec((t_chunk, SSM_WIDTH), row),
            pl.BlockSpec((2, HALF_IN, 2 * HALF_ST), fixed3),
            pl.BlockSpec((2, 2 * HALF_ST, HALF_IN), fixed3),
            pl.BlockSpec((SSM_WIDTH, 2 * SSM_WIDTH), fixed2),
            pl.BlockSpec((1, 2 * SSM_WIDTH), fixed2),
            pl.BlockSpec((1, SSM_WIDTH), fixed2),
            pl.BlockSpec((2, 4, SUBLANES, 2 * HALF_ST), lambda b, k: (0, 0, 0, 0)),
        ],
        out_specs=[pl.BlockSpec((t_chunk, SSM_WIDTH), row), pl.BlockSpec((t_chunk, 4 * HALF_ST), row)],
        out_shape=[jax.ShapeDtypeStruct((n, SSM_WIDTH), F32), jax.ShapeDtypeStruct((n, 4 * HALF_ST), F32)],
        scratch_shapes=[pltpu.VMEM((2, t_chunk, 2 * HALF_ST), F32), pltpu.VMEM((SUBLANES, 4 * HALF_ST), F32)],
        compiler_params=_cp("parallel", "arbitrary"),
    )(z, bblk, cblk, glu, glub, dskip, fwdc)


def _ssm_bwd(z, h, dy, bblk, cblk, glu, glub, dskip, revc, batch, t_chunk, name):
    n = z.shape[0]
    nk = n // batch // t_chunk
    n_tiles = t_chunk // SUBLANES

    def body(u_ref, h_ref, dy_ref, b_ref, c_ref, glu_ref, glub_ref, dskip_ref, k_ref,
             du_ref, dglu_ref, dglub_ref, ddskip_ref, dct_ref, db_ref, q_ref, g_ref, carry_ref):
        first = jnp.logical_and(pl.program_id(0) == 0, pl.program_id(1) == 0)

        @pl.when(first)
        def _():
            for r in (dglu_ref, dglub_ref, ddskip_ref, dct_ref, db_ref, q_ref):
                r[...] = jnp.zeros_like(r)

        @pl.when(pl.program_id(1) == 0)
        def _():
            carry_ref[...] = jnp.zeros_like(carry_ref)

        u = u_ref[...]
        ub = u.astype(BF16)
        hb = [h_ref[:, half * 2 * HALF_ST:(half + 1) * 2 * HALF_ST].astype(BF16) for half in range(2)]
        ypre, th, yg, z1, sg = _ssm_tail(hb, u, c_ref, glu_ref, glub_ref, dskip_ref)
        dout = dy_ref[...]
        dz = jnp.concatenate([dout * sg, dout * z1 * sg * (1.0 - sg)], axis=1)
        dzb = dz.astype(BF16)
        dglu_ref[...] += _dot_tn(yg.astype(BF16), dzb)
        dglub_ref[...] += _rows8(dz)
        dypre = _dot_nt(dzb, glu_ref[...]) * _gelu_grad(ypre, th)
        ddskip_ref[...] += _rows8(dypre * u)
        dypb = dypre.astype(BF16)
        du_parts = []
        for half in range(2):
            dyp_h = dypb[:, half * HALF_IN:(half + 1) * HALF_IN]
            dct_ref[half] += _dot_tn(dyp_h, hb[half])
            g_ref[half] = _dot_nt(dyp_h, c_ref[half])

            def extra(rows, piece, x_in, g_out, acc, half=half):
                er, ei = g_out[0] - x_in[0], g_out[1] - x_in[1]
                base = half * 2 * HALF_ST + piece * SCAN_W
                hr = h_ref[rows, pl.ds(base, SCAN_W)]
                hi = h_ref[rows, pl.ds(base + HALF_ST, SCAN_W)]
                return acc[0] + (er * hr + ei * hi), acc[1] + (er * hi - ei * hr)

            accs = _scan_tiles(g_ref.at[half], g_ref.at[half], 0, k_ref, half, carry_ref, n_tiles, True, extra)
            for piece in range(SCAN_PIECES):
                base = half * 2 * HALF_ST + piece * SCAN_W
                q_ref[:, pl.ds(base, SCAN_W)] += accs[piece][0]
                q_ref[:, pl.ds(base + HALF_ST, SCAN_W)] += accs[piece][1]
            gb = g_ref[half].astype(BF16)
            db_ref[half] += _dot_tn(ub[:, half * HALF_IN:(half + 1) * HALF_IN], gb)
            du_parts.append(_dot_nt(gb, b_ref[half]))
        du_ref[...] = dypre * dskip_ref[...] + jnp.concatenate(du_parts, axis=1)

    fixed2 = lambda b, k: (0, 0)
    fixed3 = lambda b, k: (0, 0, 0)
    row = lambda b, k: (b * nk + (nk - 1 - k), 0)
    return pl.pallas_call(
        body, name=name, grid=(batch, nk),
        in_specs=[
            pl.BlockSpec((t_chunk, SSM_WIDTH), row),
            pl.BlockSpec((t_chunk, 4 * HALF_ST), row),
            pl.BlockSpec((t_chunk, SSM_WIDTH), row),
            pl.BlockSpec((2, HALF_IN, 2 * HALF_ST), fixed3),
            pl.BlockSpec((2, 2 * HALF_ST, HALF_IN), fixed3),
            pl.BlockSpec((SSM_WIDTH, 2 * SSM_WIDTH), fixed2),
            pl.BlockSpec((1, 2 * SSM_WIDTH), fixed2),
            pl.BlockSpec((1, SSM_WIDTH), fixed2),
            pl.BlockSpec((2, 4, SUBLANES, 2 * HALF_ST), lambda b, k: (0, 0, 0, 0)),
        ],
        out_specs=[
            pl.BlockSpec((t_chunk, SSM_WIDTH), row),
            pl.BlockSpec((SSM_WIDTH, 2 * SSM_WIDTH), fixed2),
            pl.BlockSpec((SUBLANES, 2 * SSM_WIDTH), fixed2),
            pl.BlockSpec((SUBLANES, SSM_WIDTH), fixed2),
            pl.BlockSpec((2, HALF_IN, 2 * HALF_ST), fixed3),
            pl.BlockSpec((2, HALF_IN, 2 * HALF_ST), fixed3),
            pl.BlockSpec((SUBLANES, 4 * HALF_ST), fixed2),
        ],
        out_shape=[
            jax.ShapeDtypeStruct((n, SSM_WIDTH), F32),
            jax.ShapeDtypeStruct((SSM_WIDTH, 2 * SSM_WIDTH), F32),
            jax.ShapeDtypeStruct((SUBLANES, 2 * SSM_WIDTH), F32),
            jax.ShapeDtypeStruct((SUBLANES, SSM_WIDTH), F32),
            jax.ShapeDtypeStruct((2, HALF_IN, 2 * HALF_ST), F32),
            jax.ShapeDtypeStruct((2, HALF_IN, 2 * HALF_ST), F32),
            jax.ShapeDtypeStruct((SUBLANES, 4 * HALF_ST), F32),
        ],
        scratch_shapes=[pltpu.VMEM((2, t_chunk, 2 * HALF_ST), F32), pltpu.VMEM((SUBLANES, 4 * HALF_ST), F32)],
        compiler_params=_cp("arbitrary", "arbitrary"),
    )(z, h, dy, bblk, cblk, glu, glub, dskip, revc)


def _gm_chunk_fwd(u, v, gain_ref, w_ref, bias_ref):
    ug, thu = _gelu(u)
    vg, thv = _gelu(v)
    rs, vns, ss = [], [], []
    for hh in range(GM_HEADS):
        cs = slice(hh * GM_HEAD_DIM, (hh + 1) * GM_HEAD_DIM)
        vn, _, r = _rms_fwd(vg[:, cs], gain_ref[:, cs])
        s = _dot(w_ref[hh], vn.astype(BF16)) + bias_ref[:, cs]
        rs.append(r)
        vns.append(vn)
        ss.append(s)
    return ug, thu, thv, vg, rs, vns, ss


def _gm_fwd(z, gain, w_tril, bias, rows, name):
    n = z.shape[0]
    chunks = rows // GM_CHUNK

    def body(u_ref, v_ref, gain_ref, w_ref, bias_ref, y_ref):
        for c in range(chunks):
            rs_ = slice(c * GM_CHUNK, (c + 1) * GM_CHUNK)
            ug, _, _, _, _, _, ss = _gm_chunk_fwd(u_ref[rs_, :], v_ref[rs_, :], gain_ref, w_ref, bias_ref)
            y_ref[rs_, :] = ug * jnp.concatenate(ss, axis=1)

    return pl.pallas_call(
        body, name=name, grid=(n // rows,),
        in_specs=[
            pl.BlockSpec((rows, GM_WIDTH), lambda i: (i, 1)),
            pl.BlockSpec((rows, GM_WIDTH), lambda i: (i, 2)),
            pl.BlockSpec((1, GM_WIDTH), lambda i: (0, 0)),
            pl.BlockSpec((GM_HEADS, GM_CHUNK, GM_CHUNK), lambda i: (0, 0, 0)),
            pl.BlockSpec((GM_CHUNK, GM_WIDTH), lambda i: (0, 0)),
        ],
        out_specs=pl.BlockSpec((rows, GM_WIDTH), lambda i: (i, 0)),
        out_shape=jax.ShapeDtypeStruct((n, GM_WIDTH), F32),
        compiler_params=_cp("parallel"),
    )(z, z, gain, w_tril, bias)


def _gm_bwd(z, dy, gain, w_tril, bias, rows, name):
    n = z.shape[0]
    chunks = rows // GM_CHUNK

    def body(u_ref, v_ref, dy_ref, gain_ref, w_ref, bias_ref, du_ref, dv_ref, dw_ref, dbias_ref, dgain_ref):
        @pl.when(pl.program_id(0) == 0)
        def _():
            dw_ref[...] = jnp.zeros_like(dw_ref)
            dbias_ref[...] = jnp.zeros_like(dbias_ref)
            dgain_ref[...] = jnp.zeros_like(dgain_ref)

        for c in range(chunks):
            rs_ = slice(c * GM_CHUNK, (c + 1) * GM_CHUNK)
            u, v = u_ref[rs_, :], v_ref[rs_, :]
            ug, thu, thv, vg, rs, vns, ss = _gm_chunk_fwd(u, v, gain_ref, w_ref, bias_ref)
            dout = dy_ref[rs_, :]
            ds = dout * ug
            du_ref[rs_, :] = dout * jnp.concatenate(ss, axis=1) * _gelu_grad(u, thu)
            dbias_ref[...] += ds
            dvg_parts, dgain_parts = [], []
            for hh in range(GM_HEADS):
                cs = slice(hh * GM_HEAD_DIM, (hh + 1) * GM_HEAD_DIM)
                dsb = ds[:, cs].astype(BF16)
                dvn = _dot_tn(w_ref[hh], dsb)
                dw_ref[hh] += _dot_nt(dsb, vns[hh].astype(BF16))
                g = gain_ref[:, cs]
                xh = vg[:, cs] * rs[hh]
                dvg, dgr = _rms_bwd(dvn, xh, rs[hh], g)
                dvg_parts.append(dvg)
                dgain_parts.append(dgr)
            dv_ref[rs_, :] = jnp.concatenate(dvg_parts, axis=1) * _gelu_grad(v, thv)
            dgain_ref[...] += _rows8(jnp.concatenate(dgain_parts, axis=1))

    row = lambda i: (i, 0)
    return pl.pallas_call(
        body, name=name, grid=(n // rows,),
        in_specs=[
            pl.BlockSpec((rows, GM_WIDTH), lambda i: (i, 1)),
            pl.BlockSpec((rows, GM_WIDTH), lambda i: (i, 2)),
            pl.BlockSpec((rows, GM_WIDTH), row),
            pl.BlockSpec((1, GM_WIDTH), lambda i: (0, 0)),
            pl.BlockSpec((GM_HEADS, GM_CHUNK, GM_CHUNK), lambda i: (0, 0, 0)),
            pl.BlockSpec((GM_CHUNK, GM_WIDTH), lambda i: (0, 0)),
        ],
        out_specs=[
            pl.BlockSpec((rows, GM_WIDTH), row), pl.BlockSpec((rows, GM_WIDTH), row),
            pl.BlockSpec((GM_HEADS, GM_CHUNK, GM_CHUNK), lambda i: (0, 0, 0)),
            pl.BlockSpec((GM_CHUNK, GM_WIDTH), lambda i: (0, 0)),
            pl.BlockSpec((SUBLANES, GM_WIDTH), lambda i: (0, 0)),
        ],
        out_shape=[
            jax.ShapeDtypeStruct((n, GM_WIDTH), F32), jax.ShapeDtypeStruct((n, GM_WIDTH), F32),
            jax.ShapeDtypeStruct((GM_HEADS, GM_CHUNK, GM_CHUNK), F32),
            jax.ShapeDtypeStruct((GM_CHUNK, GM_WIDTH), F32),
            jax.ShapeDtypeStruct((SUBLANES, GM_WIDTH), F32),
        ],
        compiler_params=_cp("arbitrary"),
    )(z, z, dy, gain, w_tril, bias)


def _loss_head(x, gain, target, tm, name):
    n = x.shape[0]

    def body(x_ref, g_ref, t_ref, dx_ref, sq_ref, dgain_ref):
        @pl.when(pl.program_id(0) == 0)
        def _():
            sq_ref[...] = jnp.zeros_like(sq_ref)
            dgain_ref[...] = jnp.zeros_like(dgain_ref)

        g = g_ref[...]
        y, xh, r = _rms_fwd(x_ref[...], g)
        err = y - t_ref[...]
        sq_ref[...] += _rows8(err * err)
        dx, dgr = _rms_bwd(err * (1.0 / D_MODEL), xh, r, g)
        dx_ref[...] = dx
        dgain_ref[...] += _rows8(dgr)

    row = lambda i: (i, 0)
    fixed = lambda i: (0, 0)
    return pl.pallas_call(
        body, name=name, grid=(n // tm,),
        in_specs=[pl.BlockSpec((tm, D_MODEL), row), pl.BlockSpec((1, D_MODEL), fixed), pl.BlockSpec((tm, D_MODEL), row)],
        out_specs=[pl.BlockSpec((tm, D_MODEL), row), pl.BlockSpec((SUBLANES, D_MODEL), fixed),
                   pl.BlockSpec((SUBLANES, D_MODEL), fixed)],
        out_shape=[jax.ShapeDtypeStruct((n, D_MODEL), F32), jax.ShapeDtypeStruct((SUBLANES, D_MODEL), F32),
                   jax.ShapeDtypeStruct((SUBLANES, D_MODEL), F32)],
        compiler_params=_cp("arbitrary"),
    )(x, gain, target)


def _adam_math(w, g, m, v):
    m2 = ADAM_B1 * m + (1.0 - ADAM_B1) * g
    v2 = ADAM_B2 * v + (1.0 - ADAM_B2) * (g * g)
    m_hat = m2 / (1.0 - ADAM_B1 ** ADAM_STEP)
    v_hat = v2 / (1.0 - ADAM_B2 ** ADAM_STEP)
    delta = -ADAM_LR * (m_hat / (jnp.sqrt(v_hat) + ADAM_EPS) + ADAM_WD * w)
    return delta, m2, v2


def _adam_sharded(parts, w, m, v, layer, earlier, name):
    depth, r, c = w.shape
    tr = max(t for t in range(16, 353, 16) if r % t == 0)

    def body(p_ref, w_ref, m_ref, v_ref, *rest):
        g_ref, d_ref, m2_ref, v2_ref = rest[-4:]
        g = p_ref[0].astype(F32)
        for s in range(1, N_DEV):
            g = g + p_ref[s].astype(F32)
        delta, m2, v2 = _adam_math(w_ref[...], g, m_ref[...], v_ref[...])
        g_ref[...] = g
        d_ref[...] = delta
        m2_ref[...] = m2
        v2_ref[...] = v2

    blk = pl.BlockSpec((None, tr, c), lambda i: (layer, i, 0))
    extra = [] if earlier is None else list(earlier)
    return pl.pallas_call(
        body, name=name, grid=(r // tr,),
        in_specs=[pl.BlockSpec((N_DEV, tr, c), lambda i: (0, i, 0)), blk, blk, blk]
        + [pl.BlockSpec(memory_space=pl.ANY)] * len(extra),
        out_specs=[blk, blk, blk, blk],
        out_shape=[jax.ShapeDtypeStruct((depth, r, c), F32)] * 4,
        input_output_aliases={4 + i: i for i in range(len(extra))},
        compiler_params=_cp("parallel"),
    )(parts, w, m, v, *extra)


def _adam_packed(g, w, m, v, name):
    r, c = g.shape

    def body(g_ref, w_ref, m_ref, v_ref, d_ref, m2_ref, v2_ref):
        delta, m2, v2 = _adam_math(w_ref[...], g_ref[...], m_ref[...], v_ref[...])
        d_ref[...] = delta
        m2_ref[...] = m2
        v2_ref[...] = v2

    blk = pl.BlockSpec((r, c), lambda i: (0, 0))
    return pl.pallas_call(
        body, name=name, grid=(1,),
        in_specs=[blk, blk, blk, blk], out_specs=[blk, blk, blk],
        out_shape=[jax.ShapeDtypeStruct((r, c), F32)] * 3,
        compiler_params=_cp("arbitrary"),
    )(g, w, m, v)


def _my_place():
    return lax.axis_index("x"), lax.axis_index("y"), lax.axis_index("c")


def _flip(place, rel):
    x, y, c = place
    return (1 - x if rel & 4 else x, 1 - y if rel & 2 else y, 1 - c if rel & 1 else c)


def _index(place):
    return 4 * place[0] + 2 * place[1] + place[2]


def _all_gather(shards, name):
    na = len(shards)

    def body(*refs):
        xs, outs = refs[:na], refs[na:2 * na]
        send_sems, recv_sems, local_sems = refs[2 * na:]
        me = _my_place()
        sibling = _flip(me, 1)
        chips = [_flip(me, 4), _flip(me, 2), _flip(me, 6)]

        def copy(a, k, block, to, src=None):
            slot = outs[a].at[_index(block)]
            return pltpu.make_async_remote_copy(
                src_ref=slot if src is None else src, dst_ref=slot,
                send_sem=send_sems.at[a, k], recv_sem=recv_sems.at[a, k],
                device_id=to, device_id_type=MESH)

        mine = [pltpu.make_async_copy(xs[a], outs[a].at[_index(me)], local_sems.at[a]) for a in range(na)]
        for cp in mine:
            cp.start()
        first = []
        for a in range(na):
            first.append(copy(a, 0, me, sibling, src=xs[a]))
            first += [copy(a, 1 + j, me, chip, src=xs[a]) for j, chip in enumerate(chips)]
        for cp in first:
            cp.start()
        passed = []
        for a in range(na):
            for j, chip in enumerate(chips):
                copy(a, 1 + j, chip, me).wait_recv()
                fwd = copy(a, 4 + j, chip, sibling)
                fwd.start()
                passed.append(fwd)
        for a in range(na):
            copy(a, 0, sibling, me).wait_recv()
            for j, chip in enumerate(chips):
                copy(a, 4 + j, _flip(chip, 1), me).wait_recv()
        for cp in first + passed:
            cp.wait_send()
        for cp in mine:
            cp.wait()

    hbm = pl.BlockSpec(memory_space=pl.ANY)
    return pl.pallas_call(
        body, name=name,
        in_specs=[hbm] * na, out_specs=[hbm] * na,
        out_shape=[jax.ShapeDtypeStruct((N_DEV,) + s.shape, s.dtype) for s in shards],
        scratch_shapes=[pltpu.SemaphoreType.DMA((na, 7)), pltpu.SemaphoreType.DMA((na, 7)),
                        pltpu.SemaphoreType.DMA((na,))],
    )(*shards)


_HBM = pl.BlockSpec(memory_space=pltpu.HBM)
_SEM = pl.BlockSpec(memory_space=pltpu.SEMAPHORE)
_EFFECT = pltpu.SideEffectType.DATAFLOW_SIDE_EFFECTING


def _exchange_copy(src_ref, land_ref, send_sems, recv_sems, a, rel, me, scatter, landed):
    peer = _flip(me, rel)
    src = src_ref.at[_index(peer)] if scatter else src_ref
    return pltpu.make_async_remote_copy(
        src_ref=src, dst_ref=land_ref.at[_index(peer if landed else me)],
        send_sem=send_sems.at[a * (N_DEV - 1) + rel - 1], recv_sem=recv_sems.at[a * (N_DEV - 1) + rel - 1],
        device_id=peer, device_id_type=MESH)


def _own_slot(data, me, scatter):
    if scatter:
        own = lax.dynamic_slice_in_dim(data, me, 1, axis=0)
        shape = data.shape
    else:
        own = data[None]
        shape = (N_DEV,) + data.shape
    start = (me,) + (0,) * (len(shape) - 1)
    return lax.dynamic_update_slice(lax.empty(shape, data.dtype), own, start)


def _exchange_start(groups, me, scatter, name, after=None):
    sizes = [len(g) for g in groups]
    srcs = [a for g in groups for a in g]
    lands = [_own_slot(a, me, scatter) for a in srcs]
    na, ng = len(srcs), len(groups)
    deps = [] if after is None else [after]

    def body(*refs):
        src_refs, land_refs = refs[:na], refs[na:2 * na]
        sems = refs[2 * na + len(deps):2 * na + len(deps) + 2 * ng]
        token = refs[-1]
        place = _my_place()
        a = 0
        for g, size in enumerate(sizes):
            for k in range(size):
                for rel in range(1, N_DEV):
                    _exchange_copy(src_refs[a], land_refs[a], sems[2 * g], sems[2 * g + 1], k, rel, place, scatter,
                                   False).start()
                a += 1
        token[...] = jnp.zeros_like(token)

    sem_shapes = [pltpu.SemaphoreType.DMA((size * (N_DEV - 1),)) for size in sizes for _ in range(2)]
    outs = pl.pallas_call(
        body, name=name,
        in_specs=[_HBM] * (2 * na) + [pl.BlockSpec(memory_space=pl.ANY)] * len(deps),
        out_specs=[_SEM] * (2 * ng) + [_HBM] * (2 * na) + [pl.BlockSpec(memory_space=pltpu.VMEM)],
        out_shape=sem_shapes + [pltpu.HBM(a.shape, a.dtype) for a in srcs + lands]
        + [jax.ShapeDtypeStruct((SUBLANES, LANES), F32)],
        input_output_aliases={i: 2 * ng + i for i in range(2 * na)},
        compiler_params=pltpu.CompilerParams(has_side_effects=_EFFECT),
    )(*[pltpu.with_memory_space_constraint(a, pltpu.HBM) for a in srcs + lands], *deps)
    sems, thru, token = outs[:2 * ng], outs[2 * ng:2 * ng + 2 * na], outs[-1]
    handles, a = [], 0
    for g, size in enumerate(sizes):
        handles.append((sems[2 * g], sems[2 * g + 1], thru[a:a + size], thru[na + a:na + a + size]))
        a += size
    return handles, token


def _exchange_wait(handle, after, scatter, name):
    send_sems, recv_sems, srcs, lands = handle
    na = len(srcs)

    def body(*refs):
        src_refs, land_refs = refs[:na], refs[na:2 * na]
        send_ref, recv_ref = refs[2 * na], refs[2 * na + 1]
        place = _my_place()
        for a in range(na):
            for rel in range(1, N_DEV):
                cp = _exchange_copy(src_refs[a], land_refs[a], send_ref, recv_ref, a, rel, place, scatter, True)
                cp.wait_send()
                cp.wait_recv()

    outs = pl.pallas_call(
        body, name=name,
        in_specs=[_HBM] * (2 * na) + [_SEM, _SEM, pl.BlockSpec(memory_space=pl.ANY)],
        out_specs=[_HBM] * (2 * na),
        out_shape=[pltpu.HBM(a.shape, a.dtype) for a in list(srcs) + list(lands)],
        input_output_aliases={i: i for i in range(2 * na)},
        compiler_params=pltpu.CompilerParams(has_side_effects=_EFFECT),
    )(*srcs, *lands, send_sems, recv_sems, after)
    return outs[na:]


def _behind(arr, token):
    return arr + token[0:1, 0:1]


def _all_reduce_small(g, name):
    _, r, c = g.shape

    def body(g_ref, o_ref, land_ref, red_ref, send1, recv1, send2, recv2):
        me = _my_place()
        idx = _index(me)

        def scatter(rel):
            peer = _flip(me, rel)
            return pltpu.make_async_remote_copy(
                src_ref=g_ref.at[_index(peer)], dst_ref=land_ref.at[idx],
                send_sem=send1.at[rel - 1], recv_sem=recv1.at[rel - 1], device_id=peer, device_id_type=MESH)

        def gather(rel):
            peer = _flip(me, rel)
            return pltpu.make_async_remote_copy(
                src_ref=red_ref, dst_ref=o_ref.at[idx],
                send_sem=send2.at[rel - 1], recv_sem=recv2.at[rel - 1], device_id=peer, device_id_type=MESH)

        for rel in range(1, N_DEV):
            scatter(rel).start()
        land_ref[idx] = g_ref[idx]
        for rel in range(1, N_DEV):
            scatter(rel).wait()
        acc = land_ref[0]
        for s in range(1, N_DEV):
            acc = acc + land_ref[s]
        red_ref[...] = acc
        for rel in range(1, N_DEV):
            gather(rel).start()
        o_ref[idx] = acc
        for rel in range(1, N_DEV):
            gather(rel).wait()

    vmem = pl.BlockSpec(memory_space=pltpu.VMEM)
    return pl.pallas_call(
        body, name=name,
        in_specs=[vmem], out_specs=vmem,
        out_shape=jax.ShapeDtypeStruct(g.shape, F32),
        scratch_shapes=[pltpu.VMEM(g.shape, F32), pltpu.VMEM((r, c), F32)]
        + [pltpu.SemaphoreType.DMA((N_DEV - 1,))] * 4,
        compiler_params=pltpu.CompilerParams(vmem_limit_bytes=VMEM_LIMIT),
    )(g)


def _ssm_discretize(a_re, a_im, log_dt, b_re, b_im):
    dt = jnp.exp(log_dt)[:, None]
    mag = jnp.exp(a_re * dt)
    lr, li = mag * jnp.cos(a_im * dt), mag * jnp.sin(a_im * dt)
    den = a_re * a_re + a_im * a_im
    qr = ((lr - 1.0) * a_re + li * a_im) / den
    qi = (li * a_re - (lr - 1.0) * a_im) / den
    bbr = qr[..., None] * b_re - qi[..., None] * b_im
    bbi = qr[..., None] * b_im + qi[..., None] * b_re
    return lr, li, bbr, bbi


def _halves(a):
    return a.reshape((2, HALF_GROUPS) + a.shape[1:])


def _block_diag_mask(g, r, c):
    rows = lax.broadcasted_iota(jnp.int32, (g * r, g * c), 0) // r
    cols = lax.broadcasted_iota(jnp.int32, (g * r, g * c), 1) // c
    return rows == cols


def _block_diag(blocks):
    g, r, c = blocks.shape
    spread = jnp.tile(jnp.eye(c, dtype=blocks.dtype), (1, g))
    full = jnp.dot(blocks.reshape(g * r, c), spread, precision=lax.Precision.HIGHEST)
    return jnp.where(_block_diag_mask(g, r, c), full, 0.0)


def _block_diag_take(dense, g, r, c):
    gather = jnp.tile(jnp.eye(c, dtype=dense.dtype), (g, 1))
    kept = jnp.where(_block_diag_mask(g, r, c), dense, 0.0)
    return jnp.dot(kept, gather, precision=lax.Precision.HIGHEST).reshape(g, r, c)


def _ssm_matrices(bbr, bbi, c_re, c_im, glu_w, glu_b, d_skip):
    bre, bim = _halves(jnp.swapaxes(bbr, 1, 2)), _halves(jnp.swapaxes(bbi, 1, 2))
    bblk = jnp.stack([jnp.concatenate([_block_diag(bre[h]), _block_diag(bim[h])], axis=1) for h in range(2)])
    cre, cim = _halves(jnp.swapaxes(c_re, 1, 2)), _halves(jnp.swapaxes(c_im, 1, 2))
    cblk = jnp.stack([jnp.concatenate([_block_diag(cre[h]), -_block_diag(cim[h])], axis=0) for h in range(2)])
    glu = jnp.concatenate([_block_diag(glu_w[:, :, :SSM_CH]), _block_diag(glu_w[:, :, SSM_CH:])], axis=1)
    glub = jnp.concatenate([glu_b[:, :SSM_CH].reshape(1, -1), glu_b[:, SSM_CH:].reshape(1, -1)], axis=1)
    return bblk.astype(BF16), cblk.astype(BF16), glu.astype(BF16), glub, d_skip.reshape(1, -1)


def _scan_constants(lr, li, reverse):
    if reverse:
        li = -li
    pows = [(lr, li)]
    for _ in range(SUBLANES - 1):
        pr, pi = pows[-1]
        pows.append((pr * lr - pi * li, pr * li + pi * lr))
    row = jnp.arange(SUBLANES)[:, None]

    def flat(a):
        return a.reshape(2, 1, HALF_ST)

    mats = []
    for s in (1, 2, 4):
        keep = (row + s <= SUBLANES - 1) if reverse else (row >= s)
        mats.append(tuple(jnp.where(keep[None], flat(p), 0.0) for p in pows[s - 1]))
    order = [SUBLANES - 1 - j for j in range(SUBLANES)] if reverse else list(range(SUBLANES))
    mats.append(tuple(jnp.concatenate([flat(pows[j][k]) for j in order], axis=1) for k in range(2)))
    return jnp.stack([jnp.concatenate([m[0], m[1]], axis=2) for m in mats], axis=1)


def _pack(arrs, rows):
    flat = jnp.concatenate([a.reshape(-1) for a in arrs])
    return jnp.pad(flat, (0, rows * LANES - flat.shape[0])).reshape(rows, LANES)


def _unpack(buf, like):
    flat = buf.reshape(-1)
    out, off = [], 0
    for a in like:
        out.append(flat[off:off + a.size].reshape(a.shape))
        off += a.size
    return out


SMALL = ("norm_ffn1", "norm_mix", "ssm_a_re", "ssm_a_im", "ssm_log_dt", "ssm_b_re", "ssm_b_im", "ssm_c_re",
         "ssm_c_im", "ssm_d", "ssm_glu_w", "ssm_glu_b", "gm_v_gain", "gm_w_s", "gm_b_s", "gain_ssm_out",
         "gain_gm_out", "norm_ffn2", "norm_final")
BIG = ("ffn1_w_in", "ffn1_w_out", "mix_w_in", "mix_w_out", "ffn2_w_in", "ffn2_w_out")
TRANSPOSED = ("ffn1_w_in", "mix_w_in", "ffn2_w_in")
WEIGHTS = ("norm_ffn1", "ffn1_w_in", "ffn1_w_out", "norm_mix", "mix_w_in", "ssm_a_re", "ssm_a_im", "ssm_log_dt",
           "ssm_b_re", "ssm_b_im", "ssm_c_re", "ssm_c_im", "ssm_d", "ssm_glu_w", "ssm_glu_b", "gm_v_gain", "gm_w_s",
           "gm_b_s", "gain_ssm_out", "gain_gm_out", "mix_w_out", "norm_ffn2", "ffn2_w_in", "ffn2_w_out", "norm_final")


def _step(x, target, w, m, v):
    batch, seq, _ = x.shape
    n = batch * seq
    depth = w["norm_ffn1"].shape[0]
    tm = min(512, n)
    tm_ffn = min(1024, n)
    tk = min(4096, n)
    t_chunk = min(256, seq)
    t_chunk_fwd = min(2 * t_chunk, seq)
    gm_rows = min(1024, seq)
    x = x.reshape(n, D_MODEL)
    target = target.reshape(n, D_MODEL)

    assert depth == 2
    me = _index(_my_place())
    shard = lambda group, l: [w[f"{group}_w_in"][l].astype(BF16), w[f"{group}_w_out"][l].astype(BF16)]
    batches = ([("mix", 0), ("ffn2", 0)], [("ffn1", 1), ("mix", 1)], [("ffn2", 1)])
    gathered, pending = {("ffn1", 0): tuple(_all_gather(shard("ffn1", 0), "all_gather_first"))}, {}

    def gather_start(i, after):
        handles, tok = _exchange_start([shard(g, l) for g, l in batches[i]], me, False, f"all_gather_start_{i}",
                                       after)
        pending.update(zip(batches[i], handles))
        return tok

    def weights(group, l, after=None):
        if (group, l) not in gathered:
            w_in, w_out = _exchange_wait(pending[(group, l)], after, False, f"all_gather_wait_{group}_{l}")
            if group == "mix":
                w_in = jnp.transpose(w_in, (1, 0, 2)).reshape(D_MODEL, IN_COLS)
                w_out = w_out.reshape(D_MODEL, D_MODEL)
            gathered[(group, l)] = (w_in, w_out)
        return gathered[(group, l)]

    tril = jnp.tril(jnp.ones((GM_CHUNK, GM_CHUNK), bool))
    layers = []
    for l in range(depth):
        disc, disc_vjp = jax.vjp(_ssm_discretize, w["ssm_a_re"][l], w["ssm_a_im"][l], w["ssm_log_dt"][l],
                                 w["ssm_b_re"][l], w["ssm_b_im"][l])
        lr, li, bbr, bbi = disc
        bblk, cblk, glu, glub, dskip = _ssm_matrices(bbr, bbi, w["ssm_c_re"][l], w["ssm_c_im"][l],
                                                     w["ssm_glu_w"][l], w["ssm_glu_b"][l], w["ssm_d"][l])
        layers.append(dict(
            disc_vjp=disc_vjp, lr=lr, li=li, bblk=bblk, cblk=cblk, glu=glu, glub=glub, dskip=dskip,
            fwdc=_scan_constants(lr, li, False), revc=_scan_constants(lr, li, True),
            w_tril=jnp.where(tril[None], w["gm_w_s"][l], 0.0).astype(BF16),
            gm_bias=jnp.repeat(w["gm_b_s"][l].T, GM_HEAD_DIM, axis=1),
            g1=w["norm_ffn1"][l][None], gmix=w["norm_mix"][l][None], g2=w["norm_ffn2"][l][None],
            gv=w["gm_v_gain"][l][None], gs=w["gain_ssm_out"][l][None], gg=w["gain_gm_out"][l][None],
        ))

    saved = []
    for l in range(depth):
        p = layers[l]
        x0 = x
        g1, gmix, g2 = p["g1"], p["gmix"], p["g2"]
        w_in, w_out = weights("ffn1", l, x0)
        if l == 0:
            g1 = _behind(g1, gather_start(0, w_in))
        x1, xn1, gu1 = _ffn_fwd(x0, g1, w_in, w_out, tm_ffn, f"ffn1_fwd_{l}")
        if l == 0:
            gmix = _behind(gmix, gather_start(1, x1))
        mwi, mwo = weights("mix", l, x1)
        z = _mix_in_fwd(x1, gmix, mwi, tm_ffn, f"mix_in_fwd_{l}")
        y_ssm, h = _ssm_fwd(z, p["bblk"], p["cblk"], p["glu"], p["glub"], p["dskip"], p["fwdc"], batch, t_chunk_fwd,
                            f"ssm_fwd_{l}")
        y_gm = _gm_fwd(z, p["gv"], p["w_tril"], p["gm_bias"], gm_rows, f"gm_fwd_{l}")
        x2 = _mix_out_fwd(y_ssm, y_gm, p["gs"], p["gg"], mwo, x1, tm_ffn, f"mix_out_fwd_{l}")
        if l == 0:
            g2 = _behind(g2, gather_start(2, x2))
        x, xn2, gu2 = _ffn_fwd(x2, g2, *weights("ffn2", l, x2), tm_ffn, f"ffn2_fwd_{l}")
        saved.append((x0, x1, x2, z, h, y_ssm, y_gm, xn1, gu1, xn2, gu2))

    dx, sq, dnf = _loss_head(x, w["norm_final"][None], target, tm_ffn, "loss_head")
    loss = lax.psum((0.5 / D_MODEL) * jnp.sum(sq), AXES)

    small = {k: [None] * depth for k in SMALL if k != "norm_final"}
    sent = []

    def send(group, l, keys, parts):
        (handle,), tok = _exchange_start([parts], me, True, f"reduce_scatter_start_{group}_{l}")
        sent.append((group, l, keys, handle))
        return tok

    token = None
    for l in reversed(range(depth)):
        p = layers[l]
        x0, x1, x2, z, h, y_ssm, y_gm, xn1, gu1, xn2, gu2 = saved[l]
        mwi, mwo = weights("mix", l)
        dx_out = dx
        g2 = p["g2"] if token is None else _behind(p["g2"], token)
        dx, dgu, act, dgain, dyb = _ffn_bwd(x2, g2, dx_out, gu2, *weights("ffn2", l), tm, f"ffn2_bwd_{l}")
        dw_in = _ffn_dw_in(xn2, dgu, tk, f"ffn2_dw_in_{l}")
        dw_out = _ffn_dw_out(act, dyb, tk, f"ffn2_dw_out_{l}").reshape(N_DEV, FF_SHARD // 2, D_MODEL)
        token = send("ffn2", l, ("ffn2_w_in", "ffn2_w_out"), [dw_in, dw_out])
        small["norm_ffn2"][l] = dgain.sum(0)

        dy_ssm, dy_gm, dwo, dgs, dgg = _mix_out_bwd(y_ssm, y_gm, _behind(p["gs"], token), p["gg"], mwo, dx, tm,
                                                    f"mix_out_bwd_{l}")
        dwo = dwo.astype(BF16).reshape(N_DEV, D_MODEL // N_DEV, D_MODEL)
        small["gain_ssm_out"][l] = dgs.sum(0)
        small["gain_gm_out"][l] = dgg.sum(0)

        du_ssm, dglu, dglub, ddskip, dct, db, q = _ssm_bwd(
            z, h, dy_ssm, p["bblk"], p["cblk"], p["glu"], p["glub"], p["dskip"], p["revc"], batch, t_chunk,
            f"ssm_bwd_{l}")
        du_gm, dv_gm, dws, dbias, dgv = _gm_bwd(z, dy_gm, p["gv"], p["w_tril"], p["gm_bias"], gm_rows, f"gm_bwd_{l}")

        q = q.sum(0).reshape(2, 2, HALF_GROUPS, SSM_STATE)
        qr, qi = q[:, 0].reshape(SSM_GROUPS, SSM_STATE), q[:, 1].reshape(SSM_GROUPS, SSM_STATE)
        den = p["lr"] * p["lr"] + p["li"] * p["li"]
        d_re = (qr * p["lr"] + qi * p["li"]) / den
        d_im = (qi * p["lr"] - qr * p["li"]) / den
        dbb = jnp.stack([_block_diag_take(db[hf, :, k * HALF_ST:(k + 1) * HALF_ST], HALF_GROUPS, SSM_CH, SSM_STATE)
                         for k in range(2) for hf in range(2)]).reshape(2, SSM_GROUPS, SSM_CH, SSM_STATE)
        dcc = jnp.stack([_block_diag_take(dct[hf, :, k * HALF_ST:(k + 1) * HALF_ST], HALF_GROUPS, SSM_CH, SSM_STATE)
                         for k in range(2) for hf in range(2)]).reshape(2, SSM_GROUPS, SSM_CH, SSM_STATE)
        da_re, da_im, dlog_dt, db_re, db_im = p["disc_vjp"](
            (d_re, -d_im, jnp.swapaxes(dbb[0], 1, 2), jnp.swapaxes(dbb[1], 1, 2)))
        small["ssm_a_re"][l], small["ssm_a_im"][l], small["ssm_log_dt"][l] = da_re, da_im, dlog_dt
        small["ssm_b_re"][l], small["ssm_b_im"][l] = db_re, db_im
        small["ssm_c_re"][l], small["ssm_c_im"][l] = dcc[0], -dcc[1]
        small["ssm_d"][l] = ddskip.sum(0).reshape(SSM_GROUPS, SSM_CH)
        small["ssm_glu_w"][l] = jnp.concatenate(
            [_block_diag_take(dglu[:, :SSM_WIDTH], SSM_GROUPS, SSM_CH, SSM_CH),
             _block_diag_take(dglu[:, SSM_WIDTH:], SSM_GROUPS, SSM_CH, SSM_CH)], axis=2)
        dglub = dglub.sum(0)
        small["ssm_glu_b"][l] = jnp.concatenate(
            [dglub[:SSM_WIDTH].reshape(SSM_GROUPS, SSM_CH), dglub[SSM_WIDTH:].reshape(SSM_GROUPS, SSM_CH)], axis=1)
        small["gm_v_gain"][l] = dgv.sum(0)
        small["gm_w_s"][l] = jnp.where(tril[None], dws, 0.0)
        small["gm_b_s"][l] = dbias.reshape(GM_CHUNK, GM_HEADS, GM_HEAD_DIM).sum(-1).T

        dx, dwi, dgain = _mix_in_bwd(x1, p["gmix"], du_ssm, du_gm, dv_gm, dx, mwi, tm, f"mix_in_bwd_{l}")
        dwi = dwi.astype(BF16).reshape(N_DEV, IN_COLS // N_DEV, D_MODEL)
        token = send("mix", l, ("mix_w_in", "mix_w_out"), [dwi, dwo])
        small["norm_mix"][l] = dgain.sum(0)

        dx_out = dx
        dx, dgu, act, dgain, dyb = _ffn_bwd(x0, _behind(p["g1"], token), dx_out, gu1, *weights("ffn1", l), tm,
                                       f"ffn1_bwd_{l}")
        small["norm_ffn1"][l] = dgain.sum(0)
        if l > 0:
            dw_in = _ffn_dw_in(xn1, dgu, tk, f"ffn1_dw_in_{l}")
            dw_out = _ffn_dw_out(act, dyb, tk, f"ffn1_dw_out_{l}").reshape(N_DEV, FF_SHARD // 2, D_MODEL)
            token = send("ffn1", l, ("ffn1_w_in", "ffn1_w_out"), [dw_in, dw_out])
            continue
        small_g = [jnp.stack(small[k]) if k != "norm_final" else dnf.sum(0) for k in SMALL]
        total = sum(int(math.prod(w[k].shape)) for k in SMALL)
        rows = -(-total // (LANES * N_DEV * SUBLANES)) * N_DEV * SUBLANES
        g_all = _all_reduce_small(_pack(small_g, rows).reshape(N_DEV, rows // N_DEV, LANES), "all_reduce_small")
        dw_in = _ffn_dw_in(xn1, dgu, tk, f"ffn1_dw_in_{l}", after=g_all)
        token = send("ffn1_in", l, ("ffn1_w_in",), [dw_in])
        dw_out = _ffn_dw_out(act, dyb, tk, f"ffn1_dw_out_{l}", after=token).reshape(
            N_DEV, FF_SHARD // 2, D_MODEL)
        token = send("ffn1_out", l, ("ffn1_w_out",), [dw_out])

    grad_x = dx.reshape(batch, seq, D_MODEL)
    grads, deltas, new_m, new_v = {}, {}, {}, {}

    g_all = _behind(g_all.reshape(rows, LANES), token)
    like = [w[k] for k in SMALL]
    d_p, m_p, v_p = _adam_packed(g_all, _pack(like, rows), _pack([m[k] for k in SMALL], rows),
                                 _pack([v[k] for k in SMALL], rows), "adam_small")
    for k, g_, d_, m_, v_ in zip(SMALL, _unpack(g_all, like), _unpack(d_p, like), _unpack(m_p, like),
                                 _unpack(v_p, like)):
        grads[k], deltas[k], new_m[k], new_v[k] = g_, d_, m_, v_

    results = {}
    after = d_p
    for group, l, keys, handle in sent:
        landed = _exchange_wait(handle, after, True, f"reduce_scatter_wait_{group}_{l}")
        for k, parts in zip(keys, landed):
            view = (lambda a: jnp.swapaxes(a, 1, 2)) if k in TRANSPOSED else (lambda a: a)
            results[k] = _adam_sharded(parts, view(w[k]), view(m[k]), view(v[k]), l, results.get(k),
                                       f"adam_{k}_{l}")
            after = results[k][0]
    for k in BIG:
        view = (lambda a: jnp.swapaxes(a, 1, 2)) if k in TRANSPOSED else (lambda a: a)
        grads[k], deltas[k], new_m[k], new_v[k] = [view(a) for a in results[k]]
    return loss, grad_x, grads, deltas, new_m, new_v


def kernel(x, norm_ffn1, ffn1_w_in, ffn1_w_out, norm_mix, mix_w_in, ssm_a_re, ssm_a_im, ssm_log_dt, ssm_b_re, ssm_b_im, ssm_c_re, ssm_c_im, ssm_d, ssm_glu_w, ssm_glu_b, gm_v_gain, gm_w_s, gm_b_s, gain_ssm_out, gain_gm_out, mix_w_out, norm_ffn2, ffn2_w_in, ffn2_w_out, norm_final, loss_target, m_norm_ffn1, m_ffn1_w_in, m_ffn1_w_out, m_norm_mix, m_mix_w_in, m_ssm_a_re, m_ssm_a_im, m_ssm_log_dt, m_ssm_b_re, m_ssm_b_im, m_ssm_c_re, m_ssm_c_im, m_ssm_d, m_ssm_glu_w, m_ssm_glu_b, m_gm_v_gain, m_gm_w_s, m_gm_b_s, m_gain_ssm_out, m_gain_gm_out, m_mix_w_out, m_norm_ffn2, m_ffn2_w_in, m_ffn2_w_out, m_norm_final, v_norm_ffn1, v_ffn1_w_in, v_ffn1_w_out, v_norm_mix, v_mix_w_in, v_ssm_a_re, v_ssm_a_im, v_ssm_log_dt, v_ssm_b_re, v_ssm_b_im, v_ssm_c_re, v_ssm_c_im, v_ssm_d, v_ssm_glu_w, v_ssm_glu_b, v_gm_v_gain, v_gm_w_s, v_gm_b_s, v_gain_ssm_out, v_gain_gm_out, v_mix_w_out, v_norm_ffn2, v_ffn2_w_in, v_ffn2_w_out, v_norm_final):
    args = locals()
    w = {k: args[k] for k in WEIGHTS}
    m = {k: args["m_" + k] for k in WEIGHTS}
    v = {k: args["v_" + k] for k in WEIGHTS}
    loss, grad_x, grads, deltas, new_m, new_v = _step(x, loss_target, w, m, v)
    return (loss, grad_x, *[grads[k] for k in WEIGHTS], *[deltas[k] for k in WEIGHTS],
            *[new_m[k] for k in WEIGHTS], *[new_v[k] for k in WEIGHTS])
```
